```python
import math
import jax, jax.numpy as jnp
from jax import lax
import numpy as np

D_MODEL = 1024
BATCH = 2
SEQ = 8192
DEPTH = 1

CHUNK = 64
CONV_CH = 512
CONV_HEADS = 8
CONV_WIDTH = 31
SSM_CH = 512
SSM_GROUP_CH = 16
SSM_GROUPS = SSM_CH // SSM_GROUP_CH
SSM_STATE = 64
D_MIX = CONV_CH + SSM_CH
D_IN = 2 * CONV_CH + SSM_CH
N_EXPERTS = 64
TOP_K = 8
N_ROUTE_GROUPS = 8
TOPK_ROUTE_GROUPS = 4
D_EXPERT = 256
ROUTED_SCALE = 2.5
MOE_BLOCK = 128
NORM_EPS = 1e-6

kernel_name = 'hymba_conformer_s5_moe_block'


def rms_norm(x, g):
    xf = x.astype(jnp.float32)
    y = xf * lax.rsqrt(jnp.mean(xf * xf, axis=-1, keepdims=True) + NORM_EPS)
    return (y * g.astype(jnp.float32)).astype(x.dtype)


def swiglu(x, w_gate, w_up, w_down):
    return (jax.nn.silu(x @ w_gate) * (x @ w_up)) @ w_down


def conv_module(v, conv_w, conv_b, ln_g, ln_b):
    y = lax.conv_general_dilated(v, conv_w, window_strides=(1,), padding=[(CONV_WIDTH - 1, 0)],
                                 dimension_numbers=('NWC', 'WIO', 'NWC'),
                                 feature_group_count=CONV_CH) + conv_b
    yf = y.astype(jnp.float32)
    mu = jnp.mean(yf, axis=-1, keepdims=True)
    var = jnp.mean(jnp.square(yf - mu), axis=-1, keepdims=True)
    yn = (yf - mu) * lax.rsqrt(var + NORM_EPS) * ln_g.astype(jnp.float32) + ln_b.astype(jnp.float32)
    return jax.nn.silu(yn).astype(v.dtype)


def _complex_affine_combine(e1, e2):
    a1r, a1i, b1r, b1i = e1
    a2r, a2i, b2r, b2i = e2
    ar = a2r * a1r - a2i * a1i
    ai = a2r * a1i + a2i * a1r
    br = a2r * b1r - a2i * b1i + b2r
    bi = a2r * b1i + a2i * b1r + b2i
    return (ar, ai, br, bi)


def s5_layer(u, a_re, a_im, log_dt, b_re, b_im, c_re, c_im, d_skip, w_glu, b_glu):
    bsz, seq, _ = u.shape
    f32 = jnp.float32
    uf = u.astype(f32).reshape(bsz, seq, SSM_GROUPS, SSM_GROUP_CH)
    ar = a_re.astype(f32)
    ai = a_im.astype(f32)
    dt = jnp.exp(log_dt.astype(f32))[:, None]
    mag = jnp.exp(ar * dt)
    lr = mag * jnp.cos(ai * dt)
    li = mag * jnp.sin(ai * dt)
    den = ar * ar + ai * ai
    nr = lr - 1.0
    kr = (nr * ar + li * ai) / den
    ki = (li * ar - nr * ai) / den
    br_ = b_re.astype(f32)
    bi_ = b_im.astype(f32)
    bbr = kr[..., None] * br_ - ki[..., None] * bi_
    bbi = kr[..., None] * bi_ + ki[..., None] * br_
    bu_r = jnp.einsum('blgh,gph->blgp', uf, bbr)
    bu_i = jnp.einsum('blgh,gph->blgp', uf, bbi)
    lam_r = jnp.broadcast_to(lr, bu_r.shape)
    lam_i = jnp.broadcast_to(li, bu_i.shape)
    _, _, s_r, s_i = lax.associative_scan(_complex_affine_combine, (lam_r, lam_i, bu_r, bu_i), axis=1)
    y = (jnp.einsum('blgp,ghp->blgh', s_r, c_re.astype(f32))
         - jnp.einsum('blgp,ghp->blgh', s_i, c_im.astype(f32))
         + d_skip.astype(f32).reshape(SSM_GROUPS, SSM_GROUP_CH) * uf)
    g = jax.nn.gelu(y.reshape(bsz, seq, SSM_CH))
    out = g * jax.nn.sigmoid(g @ w_glu.astype(f32) + b_glu.astype(f32))
    return out.astype(u.dtype)


def dropless_experts(hf, gate_w, top_idx, we_gate, we_up, we_down):
    n_tok, d = hf.shape
    n_assign = n_tok * TOP_K
    flat_e = top_idx.reshape(-1).astype(jnp.int32)
    flat_t = jnp.repeat(jnp.arange(n_tok, dtype=jnp.int32), TOP_K)
    flat_w = gate_w.reshape(-1).astype(hf.dtype)
    order = jnp.argsort(flat_e)
    sorted_e = flat_e[order]
    counts = jnp.bincount(flat_e, length=N_EXPERTS).astype(jnp.int32)
    padded = (counts + MOE_BLOCK - 1) // MOE_BLOCK * MOE_BLOCK
    starts = jnp.cumsum(counts) - counts
    pends = jnp.cumsum(padded)
    pstarts = pends - padded
    dest = pstarts[sorted_e] + jnp.arange(n_assign, dtype=jnp.int32) - starts[sorted_e]
    n_blocks = (n_assign + N_EXPERTS * (MOE_BLOCK - 1) + MOE_BLOCK - 1) // MOE_BLOCK
    n_rows = n_blocks * MOE_BLOCK
    row_tok = jnp.full((n_rows,), n_tok, jnp.int32).at[dest].set(flat_t[order])
    row_w = jnp.zeros((n_rows,), hf.dtype).at[dest].set(flat_w[order])
    block_e = jnp.minimum(
        jnp.searchsorted(pends, jnp.arange(n_blocks, dtype=jnp.int32) * MOE_BLOCK, side='right'),
        N_EXPERTS - 1).astype(jnp.int32)
    x_pad = jnp.concatenate([hf, jnp.zeros((1, d), hf.dtype)], axis=0)

    def block_fn(args):
        tok, w, e = args
        return swiglu(x_pad[tok], we_gate[e], we_up[e], we_down[e]) * w[:, None]

    ys = lax.map(block_fn, (row_tok.reshape(n_blocks, MOE_BLOCK), row_w.reshape(n_blocks, MOE_BLOCK), block_e))
    out = jnp.zeros((n_tok + 1, d), hf.dtype).at[row_tok].add(ys.reshape(n_rows, d))
    return out[:n_tok]


def moe_ffn(h, w_router, b_router, we_gate, we_up, we_down, ws_gate, ws_up, ws_down):
    bsz, seq, d = h.shape
    n_tok = bsz * seq
    hf = h.reshape(n_tok, d)
    scores = jax.nn.sigmoid(hf.astype(jnp.float32) @ w_router.astype(jnp.float32))
    biased = scores + b_router.astype(jnp.float32)
    grp = biased.reshape(n_tok, N_ROUTE_GROUPS, N_EXPERTS // N_ROUTE_GROUPS)
    grp_score = lax.top_k(grp, 2)[0].sum(-1)
    _, top_grp = lax.top_k(grp_score, TOPK_ROUTE_GROUPS)
    grp_mask = jax.nn.one_hot(top_grp, N_ROUTE_GROUPS, dtype=jnp.float32).sum(1) > 0
    exp_mask = jnp.repeat(grp_mask, N_EXPERTS // N_ROUTE_GROUPS, axis=1)
    _, top_idx = lax.top_k(jnp.where(exp_mask, biased, -jnp.inf), TOP_K)
    sel = jnp.take_along_axis(scores, top_idx, axis=1)
    gate_w = sel / (sel.sum(-1, keepdims=True) + 1e-20) * ROUTED_SCALE
    routed = dropless_experts(hf, gate_w, top_idx, we_gate, we_up, we_down)
    shared = swiglu(hf, ws_gate, ws_up, ws_down)
    return (routed + shared).reshape(bsz, seq, d)


def setup_inputs(seed: int = 0) -> dict:
    key = jax.random.key(seed)
    ks = iter(jax.random.split(key, 40))
    f32 = jnp.float32

    def nrm(shape, scale):
        return jax.random.normal(next(ks), shape, f32) * scale

    def gain(shape):
        return 1.0 + nrm(shape, 0.05)

    L = DEPTH
    G, P, H = SSM_GROUPS, SSM_STATE, SSM_GROUP_CH
    inp = {}
    inp['x'] = nrm((BATCH, SEQ, D_MODEL), 1.0)
    inp['c'] = nrm((BATCH, D_MODEL), 1.0)
    inp['w_ada'] = nrm((L, D_MODEL, 6 * D_MODEL), D_MODEL ** -0.5)
    inp['b_ada'] = nrm((L, 6 * D_MODEL), 0.02)
    inp['g_pre_mix'] = gain((L, D_MODEL))
    inp['g_post_mix'] = gain((L, D_MODEL))
    inp['w_in'] = nrm((L, D_MODEL, D_IN), D_MODEL ** -0.5)
    inp['conv_w'] = nrm((L, CONV_WIDTH, 1, CONV_CH), CONV_WIDTH ** -0.5)
    inp['conv_b'] = nrm((L, CONV_CH), 0.02)
    inp['conv_ln_g'] = gain((L, CONV_CH))
    inp['conv_ln_b'] = nrm((L, CONV_CH), 0.02)
    inp['ssm_a_re'] = -0.5 + nrm((L, G, P), 0.01)
    inp['ssm_a_im'] = jnp.broadcast_to(jnp.pi * jnp.arange(P, dtype=f32), (L, G, P)) + nrm((L, G, P), 0.01)
    inp['ssm_log_dt'] = jax.random.uniform(next(ks), (L, G), f32, minval=math.log(0.001), maxval=math.log(0.1))
    inp['ssm_b_re'] = nrm((L, G, P, H), (2 * H) ** -0.5)
    inp['ssm_b_im'] = nrm((L, G, P, H), (2 * H) ** -0.5)
    inp['ssm_c_re'] = nrm((L, G, H, P), (2 * P) ** -0.5)
    inp['ssm_c_im'] = nrm((L, G, H, P), (2 * P) ** -0.5)
    inp['ssm_d'] = nrm((L, SSM_CH), 1.0)
    inp['ssm_w_glu'] = nrm((L, SSM_CH, SSM_CH), SSM_CH ** -0.5)
    inp['ssm_b_glu'] = nrm((L, SSM_CH), 0.02)
    inp['g_out_conv'] = gain((L, CONV_CH))
    inp['g_out_ssm'] = gain((L, SSM_CH))
    inp['w_out'] = nrm((L, D_MIX, D_MODEL), D_MIX ** -0.5)
    inp['g_pre_ffn'] = gain((L, D_MODEL))
    inp['g_post_ffn'] = gain((L, D_MODEL))
    inp['w_router'] = nrm((L, D_MODEL, N_EXPERTS), D_MODEL ** -0.5)
    inp['b_router'] = nrm((L, N_EXPERTS), 0.01)
    inp['we_gate'] = nrm((L, N_EXPERTS, D_MODEL, D_EXPERT), D_MODEL ** -0.5)
    inp['we_up'] = nrm((L, N_EXPERTS, D_MODEL, D_EXPERT), D_MODEL ** -0.5)
    inp['we_down'] = nrm((L, N_EXPERTS, D_EXPERT, D_MODEL), D_EXPERT ** -0.5)
    inp['ws_gate'] = nrm((L, D_MODEL, D_EXPERT), D_MODEL ** -0.5)
    inp['ws_up'] = nrm((L, D_MODEL, D_EXPERT), D_MODEL ** -0.5)
    inp['ws_down'] = nrm((L, D_EXPERT, D_MODEL), D_EXPERT ** -0.5)
    return inp


def reference(x, c, w_ada, b_ada, g_pre_mix, g_post_mix, w_in, conv_w, conv_b, conv_ln_g, conv_ln_b,
              ssm_a_re, ssm_a_im, ssm_log_dt, ssm_b_re, ssm_b_im, ssm_c_re, ssm_c_im, ssm_d,
              ssm_w_glu, ssm_b_glu, g_out_conv, g_out_ssm, w_out, g_pre_ffn, g_post_ffn,
              w_router, b_router, we_gate, we_up, we_down, ws_gate, ws_up, ws_down):
    for l in range(DEPTH):
        mod = jax.nn.silu(c) @ w_ada[l] + b_ada[l]
        sh_m, sc_m, gt_m, sh_f, sc_f, gt_f = [m[:, None, :] for m in jnp.split(mod, 6, axis=-1)]

        h = rms_norm(x, g_pre_mix[l]) * (1.0 + sc_m) + sh_m
        z = h @ w_in[l]
        z_val = z[..., :CONV_CH]
        z_gate = z[..., CONV_CH:2 * CONV_CH]
        z_ssm = z[..., 2 * CONV_CH:]
        a_out = conv_module(z_val * jax.nn.sigmoid(z_gate), conv_w[l], conv_b[l], conv_ln_g[l], conv_ln_b[l])
        b_out = s5_layer(z_ssm, ssm_a_re[l], ssm_a_im[l], ssm_log_dt[l], ssm_b_re[l], ssm_b_im[l],
                         ssm_c_re[l], ssm_c_im[l], ssm_d[l], ssm_w_glu[l], ssm_b_glu[l])
        mixed = jnp.concatenate([rms_norm(a_out, g_out_conv[l]), rms_norm(b_out, g_out_ssm[l])], axis=-1)
        o = mixed @ w_out[l]
        x = x + gt_m * rms_norm(o, g_post_mix[l])

        h = rms_norm(x, g_pre_ffn[l]) * (1.0 + sc_f) + sh_f
        y = moe_ffn(h, w_router[l], b_router[l], we_gate[l], we_up[l], we_down[l],
                    ws_gate[l], ws_up[l], ws_down[l])
        x = x + gt_f * rms_norm(y, g_post_ffn[l])
    return x
```

```python
import functools
import math

import jax
import jax.numpy as jnp
from jax import lax
from jax.experimental import pallas as pl
from jax.experimental.pallas import tpu as pltpu

F32 = jnp.float32
BF16 = jnp.bfloat16

D_MODEL = 1024
BATCH = 2
SEQ = 8192
N_TOK = BATCH * SEQ
CONV_CH = 512
CONV_WIDTH = 31
SSM_CH = 512
SSM_GROUP_CH = 16
SSM_GROUPS = 32
SSM_STATE = 64
D_IN = 2 * CONV_CH + SSM_CH
N_EXPERTS = 64
TOP_K = 8
N_ROUTE_GROUPS = 8
TOPK_ROUTE_GROUPS = 4
D_EXPERT = 256
ROUTED_SCALE = 2.5
NORM_EPS = 1e-6

TM_IN = 512
TL_CONV = 512
CONV_HALO = 32
CONV_ROWS = 64
S5_Q = 32
S5_QH = S5_Q * SSM_GROUP_CH
S5_CHUNKS = N_TOK // S5_Q
S5_CHUNKS_PER_SEQ = SEQ // S5_Q
TM_MIX = 512
ROW_BLOCK = 256
N_BLOCKS = N_TOK * TOP_K // ROW_BLOCK + N_EXPERTS
N_ROWS = N_BLOCKS * ROW_BLOCK
TM_OUT = 512
VMEM_LIMIT = 48 * 1024 * 1024


def _cparams(sem):
    return pltpu.CompilerParams(dimension_semantics=sem, vmem_limit_bytes=VMEM_LIMIT)


def _rms(x, g):
    return x * lax.rsqrt(jnp.mean(x * x, axis=-1, keepdims=True) + NORM_EPS) * g


def _ada_kernel(c_ref, w_ref, b_ref, o_ref):
    c = c_ref[...]
    a = c * jax.nn.sigmoid(c)
    o_ref[...] = jnp.dot(a, w_ref[...], preferred_element_type=F32,
                         precision=lax.Precision.HIGHEST) + b_ref[...]


def _ada(c_pad, w_ada, b_ada):
    n = w_ada.shape[1]
    bn = 1536
    return pl.pallas_call(
        _ada_kernel,
        grid=(n // bn,),
        in_specs=[pl.BlockSpec((8, D_MODEL), lambda j: (0, 0)),
                  pl.BlockSpec((D_MODEL, bn), lambda j: (0, j)),
                  pl.BlockSpec((1, bn), lambda j: (0, j))],
        out_specs=pl.BlockSpec((8, bn), lambda j: (0, j)),
        out_shape=jax.ShapeDtypeStruct((8, n), F32),
        compiler_params=_cparams(("arbitrary",)),
        name="ada_mod",
    )(c_pad, w_ada, b_ada)


def _inproj_kernel(x_ref, mod_ref, g_ref, w_ref, v_ref, u_ref):
    x = x_ref[...]
    sh = mod_ref[0, 0:1, :]
    sc = mod_ref[0, 1:2, :]
    h = _rms(x, g_ref[...]) * (1.0 + sc) + sh
    z = jnp.dot(h.astype(BF16), w_ref[...], preferred_element_type=F32)
    v_ref[...] = z[:, :CONV_CH] * jax.nn.sigmoid(z[:, CONV_CH:2 * CONV_CH])
    u_ref[...] = z[:, 2 * CONV_CH:]


def _inproj(x2, mod, g_pre, w_in_bf):
    tiles_per_seq = SEQ // TM_IN
    return pl.pallas_call(
        _inproj_kernel,
        grid=(N_TOK // TM_IN,),
        in_specs=[pl.BlockSpec((TM_IN, D_MODEL), lambda i: (i, 0)),
                  pl.BlockSpec((1, 8, D_MODEL), lambda i: (i // tiles_per_seq, 0, 0)),
                  pl.BlockSpec((1, D_MODEL), lambda i: (0, 0)),
                  pl.BlockSpec((D_MODEL, D_IN), lambda i: (0, 0))],
        out_specs=[pl.BlockSpec((TM_IN, CONV_CH), lambda i: (i, 0)),
                   pl.BlockSpec((TM_IN, SSM_CH), lambda i: (i, 0))],
        out_shape=[jax.ShapeDtypeStruct((N_TOK, CONV_CH), F32),
                   jax.ShapeDtypeStruct((N_TOK, SSM_CH), F32)],
        compiler_params=_cparams(("parallel",)),
        name="in_proj",
    )(x2, mod, g_pre, w_in_bf)


def _conv_kernel(vc_ref, vp_ref, w_ref, cb_ref, lg_ref, lb_ref, go_ref, o_ref, ext_ref):
    i = pl.program_id(1)
    keep = (i > 0).astype(F32)
    ext_ref[0:CONV_HALO, :] = vp_ref[0] * keep
    ext_ref[CONV_HALO:, :] = vc_ref[0]
    off = CONV_HALO - (CONV_WIDTH - 1)
    for r in range(TL_CONV // CONV_ROWS):
        base = r * CONV_ROWS + off
        acc = w_ref[0:1, :] * ext_ref[base:base + CONV_ROWS, :]
        for j in range(1, CONV_WIDTH):
            acc = acc + w_ref[j:j + 1, :] * ext_ref[base + j:base + j + CONV_ROWS, :]
        y = acc + cb_ref[...]
        mu = jnp.mean(y, axis=-1, keepdims=True)
        d = y - mu
        var = jnp.mean(d * d, axis=-1, keepdims=True)
        yn = d * lax.rsqrt(var + NORM_EPS) * lg_ref[...] + lb_ref[...]
        a = yn * jax.nn.sigmoid(yn)
        o_ref[0, r * CONV_ROWS:(r + 1) * CONV_ROWS, :] = _rms(a, go_ref[...]).astype(BF16)


def _conv(v3, conv_w, conv_b, ln_g, ln_b, g_out):
    halo_per_tile = TL_CONV // CONV_HALO
    vec = pl.BlockSpec((1, CONV_CH), lambda b, i: (0, 0))
    return pl.pallas_call(
        _conv_kernel,
        grid=(BATCH, SEQ // TL_CONV),
        in_specs=[pl.BlockSpec((1, TL_CONV, CONV_CH), lambda b, i: (b, i, 0)),
                  pl.BlockSpec((1, CONV_HALO, CONV_CH),
                               lambda b, i: (b, jnp.maximum(i * halo_per_tile - 1, 0), 0)),
                  pl.BlockSpec((CONV_WIDTH + 1, CONV_CH), lambda b, i: (0, 0)),
                  vec, vec, vec, vec],
        out_specs=pl.BlockSpec((1, TL_CONV, CONV_CH), lambda b, i: (b, i, 0)),
        out_shape=jax.ShapeDtypeStruct((BATCH, SEQ, CONV_CH), BF16),
        scratch_shapes=[pltpu.VMEM((TL_CONV + CONV_HALO, CONV_CH), F32)],
        compiler_params=_cparams(("parallel", "arbitrary")),
        name="conv_module",
    )(v3, v3, conv_w, conv_b, ln_g, ln_b, g_out)


def _s5_state_in_kernel(ut_ref, wst_ref, o_ref):
    o_ref[0] = jnp.dot(ut_ref[0], wst_ref[0], preferred_element_type=F32)


def _s5_state_in(ut, wst):
    return pl.pallas_call(
        _s5_state_in_kernel,
        grid=(SSM_GROUPS,),
        in_specs=[pl.BlockSpec((1, S5_CHUNKS, S5_QH), lambda g: (g, 0, 0)),
                  pl.BlockSpec((1, S5_QH, 4 * SSM_STATE), lambda g: (g, 0, 0))],
        out_specs=pl.BlockSpec((1, S5_CHUNKS, 4 * SSM_STATE), lambda g: (g, 0, 0)),
        out_shape=jax.ShapeDtypeStruct((SSM_GROUPS, S5_CHUNKS, 4 * SSM_STATE), F32),
        compiler_params=_cparams(("parallel",)),
        name="s5_state_in",
    )(ut, wst)


def _s5_scan_kernel(sin_ref, a_ref, bq_ref, bs_ref, o_ref):
    a = a_ref[...]
    bq = bq_ref[...]
    bs = bs_ref[...]
    n = 2 * SSM_STATE

    def body(c, carry):
        x, xs = carry
        o_ref[c] = x
        s = sin_ref[c]
        xn = a * x + bq * xs + s[:, :n]
        xsn = a * xs + bs * x + s[:, n:]
        return xn, xsn

    z = jnp.zeros((SSM_GROUPS, n), F32)
    lax.fori_loop(0, S5_CHUNKS_PER_SEQ, body, (z, z))


def _s5_scan(sin_t, a_cat, b_q, b_s):
    vec = pl.BlockSpec((SSM_GROUPS, 2 * SSM_STATE), lambda b: (0, 0))
    return pl.pallas_call(
        _s5_scan_kernel,
        grid=(BATCH,),
        in_specs=[pl.BlockSpec((S5_CHUNKS_PER_SEQ, SSM_GROUPS, 4 * SSM_STATE), lambda b: (b, 0, 0)),
                  vec, vec, vec],
        out_specs=pl.BlockSpec((S5_CHUNKS_PER_SEQ, SSM_GROUPS, 2 * SSM_STATE), lambda b: (b, 0, 0)),
        out_shape=jax.ShapeDtypeStruct((S5_CHUNKS, SSM_GROUPS, 2 * SSM_STATE), F32),
        compiler_params=_cparams(("parallel",)),
        name="s5_chunk_scan",
    )(sin_t, a_cat, b_q, b_s)


def _s5_out_kernel(ut_ref, tg_ref, sp_ref, vg_ref, o_ref):
    y = jnp.dot(ut_ref[0], tg_ref[0], preferred_element_type=F32)
    y = y + jnp.dot(sp_ref[0].astype(BF16), vg_ref[0], preferred_element_type=F32)
    o_ref[0] = y


def _s5_out(ut, tg, sprev, vg):
    return pl.pallas_call(
        _s5_out_kernel,
        grid=(SSM_GROUPS,),
        in_specs=[pl.BlockSpec((1, S5_CHUNKS, S5_QH), lambda g: (g, 0, 0)),
                  pl.BlockSpec((1, S5_QH, S5_QH), lambda g: (g, 0, 0)),
                  pl.BlockSpec((1, S5_CHUNKS, 2 * SSM_STATE), lambda g: (g, 0, 0)),
                  pl.BlockSpec((1, 2 * SSM_STATE, S5_QH), lambda g: (g, 0, 0))],
        out_specs=pl.BlockSpec((1, S5_CHUNKS, S5_QH), lambda g: (g, 0, 0)),
        out_shape=jax.ShapeDtypeStruct((SSM_GROUPS, S5_CHUNKS, S5_QH), F32),
        compiler_params=_cparams(("parallel",)),
        name="s5_out",
    )(ut, tg, sprev, vg)


def _s5_operators(a_re, a_im, log_dt, b_re, b_im, c_re, c_im):
    q = S5_Q
    dt = jnp.exp(log_dt)[:, None]
    ar, ai = a_re, a_im
    mag = jnp.exp(ar * dt)
    lr = mag * jnp.cos(ai * dt)
    li = mag * jnp.sin(ai * dt)
    den = ar * ar + ai * ai
    nr = lr - 1.0
    kr = (nr * ar + li * ai) / den
    ki = (li * ar - nr * ai) / den
    bbr = kr[..., None] * b_re - ki[..., None] * b_im
    bbi = kr[..., None] * b_im + ki[..., None] * b_re
    j = jnp.arange(q + 1, dtype=F32)[None, :, None]
    pmag = jnp.exp(ar[:, None, :] * dt[:, :, None] * j)
    pang = ai[:, None, :] * dt[:, :, None] * j
    pr = pmag * jnp.cos(pang)
    pi_ = pmag * jnp.sin(pang)
    hi = lax.Precision.HIGHEST
    cl_r = c_re[:, None] * pr[:, :q, None, :] - c_im[:, None] * pi_[:, :q, None, :]
    cl_i = c_re[:, None] * pi_[:, :q, None, :] + c_im[:, None] * pr[:, :q, None, :]
    kk = (jnp.einsum('gjhp,gpk->gjhk', cl_r, bbr, precision=hi)
          - jnp.einsum('gjhp,gpk->gjhk', cl_i, bbi, precision=hi))
    tp = jnp.arange(q)[:, None]
    t = jnp.arange(q)[None, :]
    lag = t - tp
    kt = kk[:, jnp.clip(lag, 0, q - 1)]
    kt = jnp.where((lag >= 0)[None, :, :, None, None], kt, 0.0)
    tg = kt.transpose(0, 1, 4, 2, 3).reshape(SSM_GROUPS, S5_QH, S5_QH)
    wr = pr[:, q - 1::-1][:, :q]
    wi = pi_[:, q - 1::-1][:, :q]
    w_re = wr[:, :, None, :] * bbr.transpose(0, 2, 1)[:, None] - wi[:, :, None, :] * bbi.transpose(0, 2, 1)[:, None]
    w_im = wr[:, :, None, :] * bbi.transpose(0, 2, 1)[:, None] + wi[:, :, None, :] * bbr.transpose(0, 2, 1)[:, None]
    w_re = w_re.reshape(SSM_GROUPS, S5_QH, SSM_STATE)
    w_im = w_im.reshape(SSM_GROUPS, S5_QH, SSM_STATE)
    wst = jnp.concatenate([w_re, w_im, w_im, w_re], axis=-1)
    vl_r = c_re[:, None] * pr[:, 1:, None, :] - c_im[:, None] * pi_[:, 1:, None, :]
    vl_i = c_re[:, None] * pi_[:, 1:, None, :] + c_im[:, None] * pr[:, 1:, None, :]
    v_re = vl_r.transpose(0, 3, 1, 2).reshape(SSM_GROUPS, SSM_STATE, S5_QH)
    v_im = vl_i.transpose(0, 3, 1, 2).reshape(SSM_GROUPS, SSM_STATE, S5_QH)
    vg = jnp.concatenate([v_re, -v_im], axis=1)
    aq_r, aq_i = pr[:, q], pi_[:, q]
    a_cat = jnp.concatenate([aq_r, aq_r], axis=-1)
    b_q = jnp.concatenate([-aq_i, aq_i], axis=-1)
    b_s = jnp.concatenate([aq_i, -aq_i], axis=-1)
    return tg.astype(BF16), wst.astype(BF16), vg.astype(BF16), a_cat, b_q, b_s


def _gelu_tanh(x):
    return 0.5 * x * (1.0 + jnp.tanh(math.sqrt(2.0 / math.pi) * (x + 0.044715 * (x * x * x))))


def _mix_kernel(an_ref, y_ref, u_ref, x_ref, mod_ref, d_ref, wglu_ref, bglu_ref, gos_ref,
                woa_ref, wob_ref, gpm_ref, gpf_ref, wr_ref, br_ref, tri_ref,
                x1_ref, h2_ref, eidx_ref, rank_ref, gw_ref, cnt_ref, run_ref):
    i = pl.program_id(0)
    tm = TM_MIX

    @pl.when(i == 0)
    def _():
        run_ref[...] = jnp.zeros_like(run_ref)

    gt_m = mod_ref[0, 2:3, :]
    sh_f = mod_ref[0, 3:4, :]
    sc_f = mod_ref[0, 4:5, :]

    yy = y_ref[...] + d_ref[...] * u_ref[...]
    g = _gelu_tanh(yy)
    gl = jnp.dot(g.astype(BF16), wglu_ref[...], preferred_element_type=F32) + bglu_ref[...]
    ob = g * jax.nn.sigmoid(gl)
    bn = _rms(ob, gos_ref[...]).astype(BF16)
    o = (jnp.dot(an_ref[...], woa_ref[...], preferred_element_type=F32)
         + jnp.dot(bn, wob_ref[...], preferred_element_type=F32))
    x1 = x_ref[...] + gt_m * _rms(o, gpm_ref[...])
    x1_ref[...] = x1
    h2 = _rms(x1, gpf_ref[...]) * (1.0 + sc_f) + sh_f
    h2_ref[...] = h2.astype(BF16)

    logits = lax.dot_general(wr_ref[...], h2, (((1,), (1,)), ((), ())),
                             preferred_element_type=F32, precision=lax.Precision.HIGHEST)
    scores = jax.nn.sigmoid(logits)
    biased = scores + br_ref[...]
    ng = N_ROUTE_GROUPS
    gsz = N_EXPERTS // ng
    b3 = biased.reshape(ng, gsz, tm)
    s3 = scores.reshape(ng, gsz, tm)
    sub = lax.broadcasted_iota(jnp.int32, (ng, gsz, tm), 1).astype(F32)
    grp = lax.broadcasted_iota(jnp.int32, (ng, gsz, tm), 0).astype(F32)
    eid = grp * gsz + sub
    neg = -jnp.inf
    m1 = jnp.max(b3, axis=1, keepdims=True)
    i1 = jnp.min(jnp.where(b3 == m1, sub, float(gsz)), axis=1, keepdims=True)
    m2 = jnp.max(jnp.where(sub == i1, neg, b3), axis=1, keepdims=True)
    gs = m1 + m2
    gi = lax.broadcasted_iota(jnp.int32, (ng, 1, tm), 0)
    beaten = jnp.zeros((ng, 1, tm), F32)
    for gp in range(ng):
        o_ = gs[gp:gp + 1]
        beats = (o_ > gs) | ((o_ == gs) & (gi > gp))
        beaten = beaten + beats.astype(F32)
    gmask = beaten < float(TOPK_ROUTE_GROUPS)
    masked = jnp.where(gmask, b3, neg)

    sels = []
    picked = jnp.zeros((ng, gsz, tm), F32)
    for k in range(TOP_K):
        m = jnp.max(jnp.max(masked, axis=0, keepdims=True), axis=1, keepdims=True)
        cand = jnp.where(masked == m, eid, float(N_EXPERTS))
        sel = jnp.min(jnp.min(cand, axis=0, keepdims=True), axis=1, keepdims=True)
        oh = eid == sel
        masked = jnp.where(oh, neg, masked)
        picked = jnp.where(oh, 1.0, picked)
        sels.append(sel)

    pm = picked.reshape(N_EXPERTS, tm)
    prefix = jnp.dot(pm.astype(BF16), tri_ref[...], preferred_element_type=F32) + run_ref[:, 0:1]
    p3 = prefix.reshape(ng, gsz, tm)
    run_new = run_ref[...] + jnp.sum(pm, axis=1, keepdims=True)
    run_ref[...] = run_new
    cnt_ref[...] = run_new

    sc_rows = []
    for k in range(TOP_K):
        oh = eid == sels[k]
        sc_k = jnp.sum(jnp.sum(jnp.where(oh, s3, 0.0), axis=0, keepdims=True), axis=1, keepdims=True)
        rk_k = jnp.sum(jnp.sum(jnp.where(oh, p3, 0.0), axis=0, keepdims=True), axis=1, keepdims=True)
        sc_rows.append(sc_k)
        eidx_ref[k:k + 1, :] = sels[k].reshape(1, tm).astype(jnp.int32)
        rank_ref[k:k + 1, :] = rk_k.reshape(1, tm).astype(jnp.int32)
    tot = sc_rows[0]
    for k in range(1, TOP_K):
        tot = tot + sc_rows[k]
    inv = ROUTED_SCALE / (tot + 1e-20)
    for k in range(TOP_K):
        gw_ref[k:k + 1, :] = (sc_rows[k] * inv).reshape(1, tm)


def _mix(a_n, y2, u2, x2, mod, d_skip, wglu_bf, b_glu, g_out_ssm, wo_a, wo_b, g_post_mix,
         g_pre_ffn, w_router_t, b_router_col, tri):
    tm = TM_MIX
    tiles_per_seq = SEQ // tm
    row = lambda n: pl.BlockSpec((1, n), lambda i: (0, 0))
    full = lambda a, b: pl.BlockSpec((a, b), lambda i: (0, 0))
    tok = lambda n: pl.BlockSpec((tm, n), lambda i: (i, 0))
    col = pl.BlockSpec((TOP_K, tm), lambda i: (0, i))
    return pl.pallas_call(
        _mix_kernel,
        grid=(N_TOK // tm,),
        in_specs=[tok(CONV_CH), tok(SSM_CH), tok(SSM_CH), tok(D_MODEL),
                  pl.BlockSpec((1, 8, D_MODEL), lambda i: (i // tiles_per_seq, 0, 0)),
                  row(SSM_CH), full(SSM_CH, SSM_CH), row(SSM_CH), row(SSM_CH),
                  full(CONV_CH, D_MODEL), full(SSM_CH, D_MODEL), row(D_MODEL), row(D_MODEL),
                  full(N_EXPERTS, D_MODEL), full(N_EXPERTS, 1), full(tm, tm)],
        out_specs=[tok(D_MODEL), tok(D_MODEL), col, col, col,
                   pl.BlockSpec((N_EXPERTS, 128), lambda i: (0, 0))],
        out_shape=[jax.ShapeDtypeStruct((N_TOK, D_MODEL), F32),
                   jax.ShapeDtypeStruct((N_TOK, D_MODEL), BF16),
                   jax.ShapeDtypeStruct((TOP_K, N_TOK), jnp.int32),
                   jax.ShapeDtypeStruct((TOP_K, N_TOK), jnp.int32),
                   jax.ShapeDtypeStruct((TOP_K, N_TOK), F32),
                   jax.ShapeDtypeStruct((N_EXPERTS, 128), F32)],
        scratch_shapes=[pltpu.VMEM((N_EXPERTS, 128), F32)],
        compiler_params=_cparams(("arbitrary",)),
        name="mix_out_router",
    )(a_n, y2, u2, x2, mod, d_skip, wglu_bf, b_glu, g_out_ssm, wo_a, wo_b, g_post_mix,
      g_pre_ffn, w_router_t, b_router_col, tri)


def _expert_kernel(be_ref, nu_ref, xs_ref, wg_ref, wu_ref, wd_ref, ys_ref, wgu_s, wd_s):
    j = pl.program_id(0)
    prev = be_ref[jnp.maximum(j - 1, 0)]
    fresh = (j == 0) | (be_ref[j] != prev)

    @pl.when(fresh)
    def _():
        wgu_s[:, :D_EXPERT] = wg_ref[0].astype(BF16)
        wgu_s[:, D_EXPERT:] = wu_ref[0].astype(BF16)
        wd_s[...] = wd_ref[0].astype(BF16)

    @pl.when(j < nu_ref[0])
    def _():
        h = jnp.dot(xs_ref[...], wgu_s[...], preferred_element_type=F32)
        hg = h[:, :D_EXPERT]
        act = hg * jax.nn.sigmoid(hg) * h[:, D_EXPERT:]
        ys_ref[...] = jnp.dot(act.astype(BF16), wd_s[...], preferred_element_type=F32)

    @pl.when(j >= nu_ref[0])
    def _():
        ys_ref[...] = jnp.zeros_like(ys_ref)


def _experts(block_e, n_used, xs, we_gate, we_up, we_down):
    grid_spec = pltpu.PrefetchScalarGridSpec(
        num_scalar_prefetch=2,
        grid=(N_BLOCKS,),
        in_specs=[pl.BlockSpec((ROW_BLOCK, D_MODEL), lambda j, be, nu: (jnp.minimum(j, nu[0] - 1), 0)),
                  pl.BlockSpec((1, D_MODEL, D_EXPERT), lambda j, be, nu: (be[j], 0, 0)),
                  pl.BlockSpec((1, D_MODEL, D_EXPERT), lambda j, be, nu: (be[j], 0, 0)),
                  pl.BlockSpec((1, D_EXPERT, D_MODEL), lambda j, be, nu: (be[j], 0, 0))],
        out_specs=pl.BlockSpec((ROW_BLOCK, D_MODEL), lambda j, be, nu: (j, 0)),
        scratch_shapes=[pltpu.VMEM((D_MODEL, 2 * D_EXPERT), BF16),
                        pltpu.VMEM((D_EXPERT, D_MODEL), BF16)],
    )
    return pl.pallas_call(
        _expert_kernel,
        grid_spec=grid_spec,
        out_shape=jax.ShapeDtypeStruct((N_ROWS, D_MODEL), F32),
        compiler_params=_cparams(("arbitrary",)),
        name="routed_experts",
    )(block_e, n_used, xs, we_gate, we_up, we_down)


def _final_kernel(h2_ref, routed_ref, x1_ref, mod_ref, wgu_ref, wd_ref, g_ref, o_ref):
    gt_f = mod_ref[0, 5:6, :]
    h = jnp.dot(h2_ref[...], wgu_ref[...], preferred_element_type=F32)
    hg = h[:, :D_EXPERT]
    act = hg * jax.nn.sigmoid(hg) * h[:, D_EXPERT:]
    shared = jnp.dot(act.astype(BF16), wd_ref[...], preferred_element_type=F32)
    y = routed_ref[...] + shared
    o_ref[...] = x1_ref[...] + gt_f * _rms(y, g_ref[...])


def _final(h2, routed, x1, mod, ws_gu, ws_d, g_post_ffn):
    tm = TM_OUT
    tiles_per_seq = SEQ // tm
    tok = pl.BlockSpec((tm, D_MODEL), lambda i: (i, 0))
    return pl.pallas_call(
        _final_kernel,
        grid=(N_TOK // tm,),
        in_specs=[tok, tok, tok,
                  pl.BlockSpec((1, 8, D_MODEL), lambda i: (i // tiles_per_seq, 0, 0)),
                  pl.BlockSpec((D_MODEL, 2 * D_EXPERT), lambda i: (0, 0)),
                  pl.BlockSpec((D_EXPERT, D_MODEL), lambda i: (0, 0)),
                  pl.BlockSpec((1, D_MODEL), lambda i: (0, 0))],
        out_specs=tok,
        out_shape=jax.ShapeDtypeStruct((N_TOK, D_MODEL), F32),
        compiler_params=_cparams(("parallel",)),
        name="shared_final",
    )(h2, routed, x1, mod, ws_gu, ws_d, g_post_ffn)


def kernel(x, c, w_ada, b_ada, g_pre_mix, g_post_mix, w_in, conv_w, conv_b, conv_ln_g, conv_ln_b,
           ssm_a_re, ssm_a_im, ssm_log_dt, ssm_b_re, ssm_b_im, ssm_c_re, ssm_c_im, ssm_d,
           ssm_w_glu, ssm_b_glu, g_out_conv, g_out_ssm, w_out, g_pre_ffn, g_post_ffn,
           w_router, b_router, we_gate, we_up, we_down, ws_gate, ws_up, ws_down):
    l = 0
    x2 = x.reshape(N_TOK, D_MODEL)
    r1 = lambda a: a.reshape(1, -1)

    c_pad = jnp.zeros((8, D_MODEL), F32).at[:BATCH].set(c)
    mod = _ada(c_pad, w_ada[l], r1(b_ada[l]))[:BATCH].reshape(BATCH, 6, D_MODEL)
    mod = jnp.concatenate([mod, jnp.zeros((BATCH, 2, D_MODEL), F32)], axis=1)

    v, u = _inproj(x2, mod, r1(g_pre_mix[l]), w_in[l].astype(BF16))
    cw = jnp.concatenate([conv_w[l].reshape(CONV_WIDTH, CONV_CH), jnp.zeros((1, CONV_CH), F32)], axis=0)
    a_n = _conv(v.reshape(BATCH, SEQ, CONV_CH), cw, r1(conv_b[l]), r1(conv_ln_g[l]),
                r1(conv_ln_b[l]), r1(g_out_conv[l])).reshape(N_TOK, CONV_CH)

    tg, wst, vg, a_cat, b_q, b_s = _s5_operators(
        ssm_a_re[l], ssm_a_im[l], ssm_log_dt[l], ssm_b_re[l], ssm_b_im[l], ssm_c_re[l], ssm_c_im[l])
    ut = (u.astype(BF16).reshape(S5_CHUNKS, S5_Q, SSM_GROUPS, SSM_GROUP_CH)
          .transpose(2, 0, 1, 3).reshape(SSM_GROUPS, S5_CHUNKS, S5_QH))
    sin = _s5_state_in(ut, wst)
    sprev_t = _s5_scan(sin.transpose(1, 0, 2), a_cat, b_q, b_s)
    yt = _s5_out(ut, tg, sprev_t.transpose(1, 0, 2), vg)
    y2 = (yt.reshape(SSM_GROUPS, S5_CHUNKS, S5_Q, SSM_GROUP_CH)
          .transpose(1, 2, 0, 3).reshape(N_TOK, SSM_CH))

    tm = TM_MIX
    tri = (jnp.arange(tm)[:, None] < jnp.arange(tm)[None, :]).astype(BF16)
    wo = w_out[l].astype(BF16)
    x1, h2, eidx, rank, gw, cnt = _mix(
        a_n, y2, u, x2, mod, r1(ssm_d[l]), ssm_w_glu[l].astype(BF16), r1(ssm_b_glu[l]),
        r1(g_out_ssm[l]), wo[:CONV_CH], wo[CONV_CH:], r1(g_post_mix[l]), r1(g_pre_ffn[l]),
        w_router[l].T, b_router[l].reshape(N_EXPERTS, 1), tri)

    counts = cnt[:, 0].astype(jnp.int32)
    padded = (counts + ROW_BLOCK - 1) // ROW_BLOCK * ROW_BLOCK
    pends = jnp.cumsum(padded)
    pstart = pends - padded
    dest = pstart[eidx] + rank
    n_used = (pends[-1] // ROW_BLOCK).astype(jnp.int32)
    blk = jnp.minimum(jnp.arange(N_BLOCKS, dtype=jnp.int32), n_used - 1) * ROW_BLOCK
    block_e = jnp.minimum(jnp.searchsorted(pends, blk, side='right'), N_EXPERTS - 1).astype(jnp.int32)

    tok_ids = jnp.broadcast_to(jnp.arange(N_TOK, dtype=jnp.int32)[None, :], (TOP_K, N_TOK))
    row_tok = jnp.zeros((N_ROWS,), jnp.int32).at[dest.reshape(-1)].set(tok_ids.reshape(-1))
    xs = h2[row_tok]
    ys = _experts(block_e, n_used.reshape(1), xs, we_gate[l], we_up[l], we_down[l])
    routed = jnp.sum(ys[dest] * gw[:, :, None], axis=0)

    ws_gu = jnp.concatenate([ws_gate[l], ws_up[l]], axis=1).astype(BF16)
    out = _final(h2, routed, x1, mod, ws_gu, ws_down[l].astype(BF16), r1(g_post_ffn[l]))
    return out.reshape(BATCH, SEQ, D_MODEL)
```

```python
import functools
import math

import jax
import jax.numpy as jnp
from jax import lax
from jax.experimental import pallas as pl
from jax.experimental.pallas import tpu as pltpu
from jax.experimental.pallas import tpu_sc as plsc

F32 = jnp.float32
BF16 = jnp.bfloat16

D_MODEL = 1024
BATCH = 2
SEQ = 8192
N_TOK = BATCH * SEQ
CONV_CH = 512
CONV_WIDTH = 31
SSM_CH = 512
SSM_GROUP_CH = 16
SSM_GROUPS = 32
SSM_STATE = 64
D_IN = 2 * CONV_CH + SSM_CH
N_EXPERTS = 64
TOP_K = 8
N_ROUTE_GROUPS = 8
TOPK_ROUTE_GROUPS = 4
D_EXPERT = 256
ROUTED_SCALE = 2.5
NORM_EPS = 1e-6

TM_IN = 512
TL_CONV = 512
CONV_HALO = 32
CONV_ROWS = 64
S5_Q = 32
S5_QH = S5_Q * SSM_GROUP_CH
S5_CHUNKS = N_TOK // S5_Q
S5_CHUNKS_PER_SEQ = SEQ // S5_Q
TM_MIX = 512
ROW_BLOCK = 256
N_BLOCKS = N_TOK * TOP_K // ROW_BLOCK + N_EXPERTS
N_ROWS = N_BLOCKS * ROW_BLOCK
TM_OUT = 512
SC_CORES = 2
SC_SUBCORES = 16
SC_WORKERS = SC_CORES * SC_SUBCORES
SC_W = 64
SC_CHUNKS_PER_WORKER = N_TOK // (SC_WORKERS * SC_W)
VMEM_LIMIT = 48 * 1024 * 1024


def _cparams(sem):
    return pltpu.CompilerParams(dimension_semantics=sem, vmem_limit_bytes=VMEM_LIMIT)


def _pack_halves(x):
    n = x.shape[-1] // 2
    lo = lax.bitcast_convert_type(x[:, :n].astype(BF16).astype(F32), jnp.uint32)
    hi = lax.bitcast_convert_type(x[:, n:].astype(BF16).astype(F32), jnp.uint32)
    return hi | (lo >> 16)


def _unpack_halves(p):
    lo = lax.bitcast_convert_type(p << 16, F32)
    hi = lax.bitcast_convert_type(p & jnp.uint32(0xFFFF0000), F32)
    return lo, hi


def _rms(x, g):
    return x * lax.rsqrt(jnp.mean(x * x, axis=-1, keepdims=True) + NORM_EPS) * g


def _ada_kernel(c_ref, w_ref, b_ref, o_ref):
    c = c_ref[...]
    a = c * jax.nn.sigmoid(c)
    o_ref[...] = jnp.dot(a, w_ref[...], preferred_element_type=F32,
                         precision=lax.Precision.HIGHEST) + b_ref[...]


def _ada(c_pad, w_ada, b_ada):
    n = w_ada.shape[1]
    bn = 1536
    return pl.pallas_call(
        _ada_kernel,
        grid=(n // bn,),
        in_specs=[pl.BlockSpec((8, D_MODEL), lambda j: (0, 0)),
                  pl.BlockSpec((D_MODEL, bn), lambda j: (0, j)),
                  pl.BlockSpec((1, bn), lambda j: (0, j))],
        out_specs=pl.BlockSpec((8, bn), lambda j: (0, j)),
        out_shape=jax.ShapeDtypeStruct((8, n), F32),
        compiler_params=_cparams(("arbitrary",)),
        name="ada_mod",
    )(c_pad, w_ada, b_ada)


def _inproj_kernel(x_ref, mod_ref, g_ref, w_ref, v_ref, u_ref):
    x = x_ref[...]
    sh = mod_ref[0, 0:1, :]
    sc = mod_ref[0, 1:2, :]
    h = _rms(x, g_ref[...]) * (1.0 + sc) + sh
    z = jnp.dot(h.astype(BF16), w_ref[...], preferred_element_type=F32)
    v_ref[...] = z[:, :CONV_CH] * jax.nn.sigmoid(z[:, CONV_CH:2 * CONV_CH])
    u_ref[...] = z[:, 2 * CONV_CH:]


def _inproj(x2, mod, g_pre, w_in_bf):
    tiles_per_seq = SEQ // TM_IN
    return pl.pallas_call(
        _inproj_kernel,
        grid=(N_TOK // TM_IN,),
        in_specs=[pl.BlockSpec((TM_IN, D_MODEL), lambda i: (i, 0)),
                  pl.BlockSpec((1, 8, D_MODEL), lambda i: (i // tiles_per_seq, 0, 0)),
                  pl.BlockSpec((1, D_MODEL), lambda i: (0, 0)),
                  pl.BlockSpec((D_MODEL, D_IN), lambda i: (0, 0))],
        out_specs=[pl.BlockSpec((TM_IN, CONV_CH), lambda i: (i, 0)),
                   pl.BlockSpec((TM_IN, SSM_CH), lambda i: (i, 0))],
        out_shape=[jax.ShapeDtypeStruct((N_TOK, CONV_CH), F32),
                   jax.ShapeDtypeStruct((N_TOK, SSM_CH), F32)],
        compiler_params=_cparams(("parallel",)),
        name="in_proj",
    )(x2, mod, g_pre, w_in_bf)


def _conv_kernel(vc_ref, vp_ref, w_ref, cb_ref, lg_ref, lb_ref, go_ref, o_ref, ext_ref):
    i = pl.program_id(1)
    keep = (i > 0).astype(F32)
    ext_ref[0:CONV_HALO, :] = vp_ref[0] * keep
    ext_ref[CONV_HALO:, :] = vc_ref[0]
    off = CONV_HALO - (CONV_WIDTH - 1)
    for r in range(TL_CONV // CONV_ROWS):
        base = r * CONV_ROWS + off
        acc = w_ref[0:1, :] * ext_ref[base:base + CONV_ROWS, :]
        for j in range(1, CONV_WIDTH):
            acc = acc + w_ref[j:j + 1, :] * ext_ref[base + j:base + j + CONV_ROWS, :]
        y = acc + cb_ref[...]
        mu = jnp.mean(y, axis=-1, keepdims=True)
        d = y - mu
        var = jnp.mean(d * d, axis=-1, keepdims=True)
        yn = d * lax.rsqrt(var + NORM_EPS) * lg_ref[...] + lb_ref[...]
        a = yn * jax.nn.sigmoid(yn)
        o_ref[0, r * CONV_ROWS:(r + 1) * CONV_ROWS, :] = _rms(a, go_ref[...]).astype(BF16)


def _conv(v3, conv_w, conv_b, ln_g, ln_b, g_out):
    halo_per_tile = TL_CONV // CONV_HALO
    vec = pl.BlockSpec((1, CONV_CH), lambda b, i: (0, 0))
    return pl.pallas_call(
        _conv_kernel,
        grid=(BATCH, SEQ // TL_CONV),
        in_specs=[pl.BlockSpec((1, TL_CONV, CONV_CH), lambda b, i: (b, i, 0)),
                  pl.BlockSpec((1, CONV_HALO, CONV_CH),
                               lambda b, i: (b, jnp.maximum(i * halo_per_tile - 1, 0), 0)),
                  pl.BlockSpec((CONV_WIDTH + 1, CONV_CH), lambda b, i: (0, 0)),
                  vec, vec, vec, vec],
        out_specs=pl.BlockSpec((1, TL_CONV, CONV_CH), lambda b, i: (b, i, 0)),
        out_shape=jax.ShapeDtypeStruct((BATCH, SEQ, CONV_CH), BF16),
        scratch_shapes=[pltpu.VMEM((TL_CONV + CONV_HALO, CONV_CH), F32)],
        compiler_params=_cparams(("parallel", "arbitrary")),
        name="conv_module",
    )(v3, v3, conv_w, conv_b, ln_g, ln_b, g_out)


def _s5_state_in_kernel(ut_ref, wst_ref, o_ref):
    o_ref[0] = jnp.dot(ut_ref[0], wst_ref[0], preferred_element_type=F32)


def _s5_state_in(ut, wst):
    return pl.pallas_call(
        _s5_state_in_kernel,
        grid=(SSM_GROUPS,),
        in_specs=[pl.BlockSpec((1, S5_CHUNKS, S5_QH), lambda g: (g, 0, 0)),
                  pl.BlockSpec((1, S5_QH, 4 * SSM_STATE), lambda g: (g, 0, 0))],
        out_specs=pl.BlockSpec((1, S5_CHUNKS, 4 * SSM_STATE), lambda g: (g, 0, 0)),
        out_shape=jax.ShapeDtypeStruct((SSM_GROUPS, S5_CHUNKS, 4 * SSM_STATE), F32),
        compiler_params=_cparams(("parallel",)),
        name="s5_state_in",
    )(ut, wst)


def _s5_scan_kernel(sin_ref, a_ref, bq_ref, bs_ref, o_ref):
    a = a_ref[...]
    bq = bq_ref[...]
    bs = bs_ref[...]
    n = 2 * SSM_STATE

    def body(c, carry):
        x, xs = carry
        o_ref[c] = x
        s = sin_ref[c]
        xn = a * x + bq * xs + s[:, :n]
        xsn = a * xs + bs * x + s[:, n:]
        return xn, xsn

    z = jnp.zeros((SSM_GROUPS, n), F32)
    lax.fori_loop(0, S5_CHUNKS_PER_SEQ, body, (z, z))


def _s5_scan(sin_t, a_cat, b_q, b_s):
    vec = pl.BlockSpec((SSM_GROUPS, 2 * SSM_STATE), lambda b: (0, 0))
    return pl.pallas_call(
        _s5_scan_kernel,
        grid=(BATCH,),
        in_specs=[pl.BlockSpec((S5_CHUNKS_PER_SEQ, SSM_GROUPS, 4 * SSM_STATE), lambda b: (b, 0, 0)),
                  vec, vec, vec],
        out_specs=pl.BlockSpec((S5_CHUNKS_PER_SEQ, SSM_GROUPS, 2 * SSM_STATE), lambda b: (b, 0, 0)),
        out_shape=jax.ShapeDtypeStruct((S5_CHUNKS, SSM_GROUPS, 2 * SSM_STATE), F32),
        compiler_params=_cparams(("parallel",)),
        name="s5_chunk_scan",
    )(sin_t, a_cat, b_q, b_s)


def _s5_out_kernel(ut_ref, tg_ref, sp_ref, vg_ref, o_ref):
    y = jnp.dot(ut_ref[0], tg_ref[0], preferred_element_type=F32)
    y = y + jnp.dot(sp_ref[0].astype(BF16), vg_ref[0], preferred_element_type=F32)
    o_ref[0] = y


def _s5_out(ut, tg, sprev, vg):
    return pl.pallas_call(
        _s5_out_kernel,
        grid=(SSM_GROUPS,),
        in_specs=[pl.BlockSpec((1, S5_CHUNKS, S5_QH), lambda g: (g, 0, 0)),
                  pl.BlockSpec((1, S5_QH, S5_QH), lambda g: (g, 0, 0)),
                  pl.BlockSpec((1, S5_CHUNKS, 2 * SSM_STATE), lambda g: (g, 0, 0)),
                  pl.BlockSpec((1, 2 * SSM_STATE, S5_QH), lambda g: (g, 0, 0))],
        out_specs=pl.BlockSpec((1, S5_CHUNKS, S5_QH), lambda g: (g, 0, 0)),
        out_shape=jax.ShapeDtypeStruct((SSM_GROUPS, S5_CHUNKS, S5_QH), F32),
        compiler_params=_cparams(("parallel",)),
        name="s5_out",
    )(ut, tg, sprev, vg)


def _s5_operators(a_re, a_im, log_dt, b_re, b_im, c_re, c_im):
    q = S5_Q
    dt = jnp.exp(log_dt)[:, None]
    ar, ai = a_re, a_im
    mag = jnp.exp(ar * dt)
    lr = mag * jnp.cos(ai * dt)
    li = mag * jnp.sin(ai * dt)
    den = ar * ar + ai * ai
    nr = lr - 1.0
    kr = (nr * ar + li * ai) / den
    ki = (li * ar - nr * ai) / den
    bbr = kr[..., None] * b_re - ki[..., None] * b_im
    bbi = kr[..., None] * b_im + ki[..., None] * b_re
    j = jnp.arange(q + 1, dtype=F32)[None, :, None]
    pmag = jnp.exp(ar[:, None, :] * dt[:, :, None] * j)
    pang = ai[:, None, :] * dt[:, :, None] * j
    pr = pmag * jnp.cos(pang)
    pi_ = pmag * jnp.sin(pang)
    hi = lax.Precision.HIGHEST
    cl_r = c_re[:, None] * pr[:, :q, None, :] - c_im[:, None] * pi_[:, :q, None, :]
    cl_i = c_re[:, None] * pi_[:, :q, None, :] + c_im[:, None] * pr[:, :q, None, :]
    kk = (jnp.einsum('gjhp,gpk->gjhk', cl_r, bbr, precision=hi)
          - jnp.einsum('gjhp,gpk->gjhk', cl_i, bbi, precision=hi))
    tp = jnp.arange(q)[:, None]
    t = jnp.arange(q)[None, :]
    lag = t - tp
    kt = kk[:, jnp.clip(lag, 0, q - 1)]
    kt = jnp.where((lag >= 0)[None, :, :, None, None], kt, 0.0)
    tg = kt.transpose(0, 1, 4, 2, 3).reshape(SSM_GROUPS, S5_QH, S5_QH)
    wr = pr[:, q - 1::-1][:, :q]
    wi = pi_[:, q - 1::-1][:, :q]
    w_re = wr[:, :, None, :] * bbr.transpose(0, 2, 1)[:, None] - wi[:, :, None, :] * bbi.transpose(0, 2, 1)[:, None]
    w_im = wr[:, :, None, :] * bbi.transpose(0, 2, 1)[:, None] + wi[:, :, None, :] * bbr.transpose(0, 2, 1)[:, None]
    w_re = w_re.reshape(SSM_GROUPS, S5_QH, SSM_STATE)
    w_im = w_im.reshape(SSM_GROUPS, S5_QH, SSM_STATE)
    wst = jnp.concatenate([w_re, w_im, w_im, w_re], axis=-1)
    vl_r = c_re[:, None] * pr[:, 1:, None, :] - c_im[:, None] * pi_[:, 1:, None, :]
    vl_i = c_re[:, None] * pi_[:, 1:, None, :] + c_im[:, None] * pr[:, 1:, None, :]
    v_re = vl_r.transpose(0, 3, 1, 2).reshape(SSM_GROUPS, SSM_STATE, S5_QH)
    v_im = vl_i.transpose(0, 3, 1, 2).reshape(SSM_GROUPS, SSM_STATE, S5_QH)
    vg = jnp.concatenate([v_re, -v_im], axis=1)
    aq_r, aq_i = pr[:, q], pi_[:, q]
    a_cat = jnp.concatenate([aq_r, aq_r], axis=-1)
    b_q = jnp.concatenate([-aq_i, aq_i], axis=-1)
    b_s = jnp.concatenate([aq_i, -aq_i], axis=-1)
    return tg.astype(BF16), wst.astype(BF16), vg.astype(BF16), a_cat, b_q, b_s


def _gelu_tanh(x):
    return 0.5 * x * (1.0 + jnp.tanh(math.sqrt(2.0 / math.pi) * (x + 0.044715 * (x * x * x))))


def _mix_kernel(an_ref, y_ref, u_ref, x_ref, mod_ref, d_ref, wglu_ref, bglu_ref, gos_ref,
                woa_ref, wob_ref, gpm_ref, gpf_ref, wr_ref, br_ref, tri_ref,
                x1_ref, h2_ref, eidx_ref, rank_ref, gw_ref, cnt_ref, run_ref):
    i = pl.program_id(0)
    tm = TM_MIX

    @pl.when(i == 0)
    def _():
        run_ref[...] = jnp.zeros_like(run_ref)

    gt_m = mod_ref[0, 2:3, :]
    sh_f = mod_ref[0, 3:4, :]
    sc_f = mod_ref[0, 4:5, :]

    yy = y_ref[...] + d_ref[...] * u_ref[...]
    g = _gelu_tanh(yy)
    gl = jnp.dot(g.astype(BF16), wglu_ref[...], preferred_element_type=F32) + bglu_ref[...]
    ob = g * jax.nn.sigmoid(gl)
    bn = _rms(ob, gos_ref[...]).astype(BF16)
    o = (jnp.dot(an_ref[...], woa_ref[...], preferred_element_type=F32)
         + jnp.dot(bn, wob_ref[...], preferred_element_type=F32))
    x1 = x_ref[...] + gt_m * _rms(o, gpm_ref[...])
    x1_ref[...] = x1
    h2 = _rms(x1, gpf_ref[...]) * (1.0 + sc_f) + sh_f
    h2_ref[...] = _pack_halves(h2)

    logits = lax.dot_general(wr_ref[...], h2, (((1,), (1,)), ((), ())),
                             preferred_element_type=F32, precision=lax.Precision.HIGHEST)
    scores = jax.nn.sigmoid(logits)
    biased = scores + br_ref[...]
    ng = N_ROUTE_GROUPS
    gsz = N_EXPERTS // ng
    b3 = biased.reshape(ng, gsz, tm)
    s3 = scores.reshape(ng, gsz, tm)
    sub = lax.broadcasted_iota(jnp.int32, (ng, gsz, tm), 1).astype(F32)
    grp = lax.broadcasted_iota(jnp.int32, (ng, gsz, tm), 0).astype(F32)
    eid = grp * gsz + sub
    neg = -jnp.inf
    m1 = jnp.max(b3, axis=1, keepdims=True)
    i1 = jnp.min(jnp.where(b3 == m1, sub, float(gsz)), axis=1, keepdims=True)
    m2 = jnp.max(jnp.where(sub == i1, neg, b3), axis=1, keepdims=True)
    gs = m1 + m2
    gi = lax.broadcasted_iota(jnp.int32, (ng, 1, tm), 0)
    beaten = jnp.zeros((ng, 1, tm), F32)
    for gp in range(ng):
        o_ = gs[gp:gp + 1]
        beats = (o_ > gs) | ((o_ == gs) & (gi > gp))
        beaten = beaten + beats.astype(F32)
    gmask = beaten < float(TOPK_ROUTE_GROUPS)
    masked = jnp.where(gmask, b3, neg)

    sels = []
    picked = jnp.zeros((ng, gsz, tm), F32)
    for k in range(TOP_K):
        m = jnp.max(jnp.max(masked, axis=0, keepdims=True), axis=1, keepdims=True)
        cand = jnp.where(masked == m, eid, float(N_EXPERTS))
        sel = jnp.min(jnp.min(cand, axis=0, keepdims=True), axis=1, keepdims=True)
        oh = eid == sel
        masked = jnp.where(oh, neg, masked)
        picked = jnp.where(oh, 1.0, picked)
        sels.append(sel)

    pm = picked.reshape(N_EXPERTS, tm)
    prefix = jnp.dot(pm.astype(BF16), tri_ref[...], preferred_element_type=F32) + run_ref[:, 0:1]
    p3 = prefix.reshape(ng, gsz, tm)
    run_new = run_ref[...] + jnp.sum(pm, axis=1, keepdims=True)
    run_ref[...] = run_new
    cnt_ref[...] = run_new

    sc_rows = []
    for k in range(TOP_K):
        oh = eid == sels[k]
        sc_k = jnp.sum(jnp.sum(jnp.where(oh, s3, 0.0), axis=0, keepdims=True), axis=1, keepdims=True)
        rk_k = jnp.sum(jnp.sum(jnp.where(oh, p3, 0.0), axis=0, keepdims=True), axis=1, keepdims=True)
        sc_rows.append(sc_k)
        eidx_ref[k:k + 1, :] = sels[k].reshape(1, tm).astype(jnp.int32)
        rank_ref[k:k + 1, :] = rk_k.reshape(1, tm).astype(jnp.int32)
    tot = sc_rows[0]
    for k in range(1, TOP_K):
        tot = tot + sc_rows[k]
    inv = ROUTED_SCALE / (tot + 1e-20)
    for k in range(TOP_K):
        gw_ref[k:k + 1, :] = (sc_rows[k] * inv).reshape(1, tm)


def _mix(a_n, y2, u2, x2, mod, d_skip, wglu_bf, b_glu, g_out_ssm, wo_a, wo_b, g_post_mix,
         g_pre_ffn, w_router_t, b_router_col, tri):
    tm = TM_MIX
    tiles_per_seq = SEQ // tm
    row = lambda n: pl.BlockSpec((1, n), lambda i: (0, 0))
    full = lambda a, b: pl.BlockSpec((a, b), lambda i: (0, 0))
    tok = lambda n: pl.BlockSpec((tm, n), lambda i: (i, 0))
    col = pl.BlockSpec((TOP_K, tm), lambda i: (0, i))
    return pl.pallas_call(
        _mix_kernel,
        grid=(N_TOK // tm,),
        in_specs=[tok(CONV_CH), tok(SSM_CH), tok(SSM_CH), tok(D_MODEL),
                  pl.BlockSpec((1, 8, D_MODEL), lambda i: (i // tiles_per_seq, 0, 0)),
                  row(SSM_CH), full(SSM_CH, SSM_CH), row(SSM_CH), row(SSM_CH),
                  full(CONV_CH, D_MODEL), full(SSM_CH, D_MODEL), row(D_MODEL), row(D_MODEL),
                  full(N_EXPERTS, D_MODEL), full(N_EXPERTS, 1), full(tm, tm)],
        out_specs=[tok(D_MODEL), tok(D_MODEL // 2), col, col, col,
                   pl.BlockSpec((N_EXPERTS, 128), lambda i: (0, 0))],
        out_shape=[jax.ShapeDtypeStruct((N_TOK, D_MODEL), F32),
                   jax.ShapeDtypeStruct((N_TOK, D_MODEL // 2), jnp.uint32),
                   jax.ShapeDtypeStruct((TOP_K, N_TOK), jnp.int32),
                   jax.ShapeDtypeStruct((TOP_K, N_TOK), jnp.int32),
                   jax.ShapeDtypeStruct((TOP_K, N_TOK), F32),
                   jax.ShapeDtypeStruct((N_EXPERTS, 128), F32)],
        scratch_shapes=[pltpu.VMEM((N_EXPERTS, 128), F32)],
        compiler_params=_cparams(("arbitrary",)),
        name="mix_out_router",
    )(a_n, y2, u2, x2, mod, d_skip, wglu_bf, b_glu, g_out_ssm, wo_a, wo_b, g_post_mix,
      g_pre_ffn, w_router_t, b_router_col, tri)


def _expert_kernel(be_ref, nu_ref, xs_ref, wg_ref, wu_ref, wd_ref, ys_ref, wgu_s, wd_s):
    j = pl.program_id(0)
    prev = be_ref[jnp.maximum(j - 1, 0)]
    fresh = (j == 0) | (be_ref[j] != prev)

    @pl.when(fresh)
    def _():
        wgu_s[:, :D_EXPERT] = wg_ref[0].astype(BF16)
        wgu_s[:, D_EXPERT:] = wu_ref[0].astype(BF16)
        wd_s[...] = wd_ref[0].astype(BF16)

    @pl.when(j < nu_ref[0])
    def _():
        x_lo, x_hi = _unpack_halves(xs_ref[...])
        half = D_MODEL // 2
        h = (jnp.dot(x_lo.astype(BF16), wgu_s[:half, :], preferred_element_type=F32)
             + jnp.dot(x_hi.astype(BF16), wgu_s[half:, :], preferred_element_type=F32))
        hg = h[:, :D_EXPERT]
        act = hg * jax.nn.sigmoid(hg) * h[:, D_EXPERT:]
        ys_ref[...] = _pack_halves(jnp.dot(act.astype(BF16), wd_s[...], preferred_element_type=F32))

    @pl.when(j >= nu_ref[0])
    def _():
        ys_ref[...] = jnp.zeros_like(ys_ref)


def _experts(block_e, n_used, xs, we_gate, we_up, we_down):
    grid_spec = pltpu.PrefetchScalarGridSpec(
        num_scalar_prefetch=2,
        grid=(N_BLOCKS,),
        in_specs=[pl.BlockSpec((ROW_BLOCK, D_MODEL // 2), lambda j, be, nu: (jnp.minimum(j, nu[0] - 1), 0)),
                  pl.BlockSpec((1, D_MODEL, D_EXPERT), lambda j, be, nu: (be[j], 0, 0)),
                  pl.BlockSpec((1, D_MODEL, D_EXPERT), lambda j, be, nu: (be[j], 0, 0)),
                  pl.BlockSpec((1, D_EXPERT, D_MODEL), lambda j, be, nu: (be[j], 0, 0))],
        out_specs=pl.BlockSpec((ROW_BLOCK, D_MODEL // 2), lambda j, be, nu: (j, 0)),
        scratch_shapes=[pltpu.VMEM((D_MODEL, 2 * D_EXPERT), BF16),
                        pltpu.VMEM((D_EXPERT, D_MODEL), BF16)],
    )
    return pl.pallas_call(
        _expert_kernel,
        grid_spec=grid_spec,
        out_shape=jax.ShapeDtypeStruct((N_ROWS, D_MODEL // 2), jnp.uint32),
        compiler_params=_cparams(("arbitrary",)),
        name="routed_experts",
    )(block_e, n_used, xs, we_gate, we_up, we_down)


def _final_kernel(h2_ref, yg_ref, gw_ref, x1_ref, mod_ref, wgu_ref, wd_ref, g_ref, o_ref):
    half = D_MODEL // 2
    gt_f = mod_ref[0, 5:6, :]
    x_lo, x_hi = _unpack_halves(h2_ref[...])
    h = (jnp.dot(x_lo.astype(BF16), wgu_ref[:half, :], preferred_element_type=F32)
         + jnp.dot(x_hi.astype(BF16), wgu_ref[half:, :], preferred_element_type=F32))
    hg = h[:, :D_EXPERT]
    act = hg * jax.nn.sigmoid(hg) * h[:, D_EXPERT:]
    shared = jnp.dot(act.astype(BF16), wd_ref[...], preferred_element_type=F32)
    y_lo = shared[:, :half]
    y_hi = shared[:, half:]
    for k in range(TOP_K):
        r_lo, r_hi = _unpack_halves(yg_ref[k])
        w = gw_ref[:, k:k + 1]
        y_lo = y_lo + w * r_lo
        y_hi = y_hi + w * r_hi
    ms = (jnp.sum(y_lo * y_lo, axis=-1, keepdims=True)
          + jnp.sum(y_hi * y_hi, axis=-1, keepdims=True)) * (1.0 / D_MODEL)
    inv = lax.rsqrt(ms + NORM_EPS)
    o_ref[:, :half] = x1_ref[:, :half] + gt_f[:, :half] * (y_lo * inv * g_ref[:, :half])
    o_ref[:, half:] = x1_ref[:, half:] + gt_f[:, half:] * (y_hi * inv * g_ref[:, half:])


def _final(h2p, yg, gw_t, x1, mod, ws_gu, ws_d, g_post_ffn):
    tm = TM_OUT
    tiles_per_seq = SEQ // tm
    tok = pl.BlockSpec((tm, D_MODEL), lambda i: (i, 0))
    return pl.pallas_call(
        _final_kernel,
        grid=(N_TOK // tm,),
        in_specs=[pl.BlockSpec((tm, D_MODEL // 2), lambda i: (i, 0)),
                  pl.BlockSpec((TOP_K, tm, D_MODEL // 2), lambda i: (0, i, 0)),
                  pl.BlockSpec((tm, TOP_K), lambda i: (i, 0)),
                  tok,
                  pl.BlockSpec((1, 8, D_MODEL), lambda i: (i // tiles_per_seq, 0, 0)),
                  pl.BlockSpec((D_MODEL, 2 * D_EXPERT), lambda i: (0, 0)),
                  pl.BlockSpec((D_EXPERT, D_MODEL), lambda i: (0, 0)),
                  pl.BlockSpec((1, D_MODEL), lambda i: (0, 0))],
        out_specs=tok,
        out_shape=jax.ShapeDtypeStruct((N_TOK, D_MODEL), F32),
        compiler_params=_cparams(("parallel",)),
        name="shared_final",
    )(h2p, yg, gw_t, x1, mod, ws_gu, ws_d, g_post_ffn)


def _sc_worker_id():
    return lax.axis_index("s") * SC_CORES + lax.axis_index("c")


def _dispatch_body(h_hbm, dest_hbm, xs_hbm, idx_v, rows_v, sem_l, sem_s):
    n = SC_CHUNKS_PER_WORKER
    c0 = _sc_worker_id() * n

    def load(i, b):
        return pltpu.async_copy(h_hbm.at[pl.ds((c0 + i) * SC_W, SC_W)], rows_v.at[b], sem_l.at[b])

    loads = [None] * n
    scat = [None] * n
    loads[0] = load(0, 0)
    for i in range(n):
        b = i % 2
        pltpu.sync_copy(dest_hbm.at[c0 + i], idx_v.at[b])
        loads[i].wait()
        if i + 1 < n:
            if i >= 1:
                for d in scat[i - 1]:
                    d.wait()
            loads[i + 1] = load(i + 1, 1 - b)
        scat[i] = [pltpu.async_copy(rows_v.at[b], xs_hbm.at[idx_v.at[b].at[k]], sem_s.at[b])
                   for k in range(TOP_K)]
    for i in (n - 2, n - 1):
        for d in scat[i]:
            d.wait()


def _sc_dispatch(h2p, dest3):
    mesh = plsc.VectorSubcoreMesh(core_axis_name="c", subcore_axis_name="s")
    return pl.kernel(
        _dispatch_body, mesh=mesh,
        out_type=jax.ShapeDtypeStruct((N_ROWS, D_MODEL // 2), jnp.uint32),
        scratch_types=[pltpu.VMEM((2, TOP_K, SC_W), jnp.int32),
                       pltpu.VMEM((2, SC_W, D_MODEL // 2), jnp.uint32),
                       pltpu.SemaphoreType.DMA((2,)), pltpu.SemaphoreType.DMA((2,))],
    )(h2p, dest3)


def _combine_body(ys_hbm, dest_hbm, yg_hbm, idx_v, rows_v, sem_g, sem_w):
    c0 = _sc_worker_id() * SC_CHUNKS_PER_WORKER

    @pl.loop(0, SC_CHUNKS_PER_WORKER)
    def _(i):
        c = c0 + i
        pltpu.sync_copy(dest_hbm.at[c], idx_v)
        g = [None] * TOP_K
        w = [None] * TOP_K
        g[0] = pltpu.async_copy(ys_hbm.at[idx_v.at[0]], rows_v.at[0], sem_g.at[0])
        for k in range(TOP_K):
            b = k % 2
            g[k].wait()
            if k + 1 < TOP_K:
                if k >= 1:
                    w[k - 1].wait()
                g[k + 1] = pltpu.async_copy(ys_hbm.at[idx_v.at[k + 1]], rows_v.at[1 - b], sem_g.at[1 - b])
            w[k] = pltpu.async_copy(rows_v.at[b], yg_hbm.at[k].at[pl.ds(c * SC_W, SC_W)], sem_w.at[b])
        w[TOP_K - 2].wait()
        w[TOP_K - 1].wait()


def _sc_combine(ysp, dest3):
    mesh = plsc.VectorSubcoreMesh(core_axis_name="c", subcore_axis_name="s")
    return pl.kernel(
        _combine_body, mesh=mesh,
        out_type=jax.ShapeDtypeStruct((TOP_K, N_TOK, D_MODEL // 2), jnp.uint32),
        scratch_types=[pltpu.VMEM((TOP_K, SC_W), jnp.int32),
                       pltpu.VMEM((2, SC_W, D_MODEL // 2), jnp.uint32),
                       pltpu.SemaphoreType.DMA((2,)), pltpu.SemaphoreType.DMA((2,))],
    )(ysp, dest3)


def kernel(x, c, w_ada, b_ada, g_pre_mix, g_post_mix, w_in, conv_w, conv_b, conv_ln_g, conv_ln_b,
           ssm_a_re, ssm_a_im, ssm_log_dt, ssm_b_re, ssm_b_im, ssm_c_re, ssm_c_im, ssm_d,
           ssm_w_glu, ssm_b_glu, g_out_conv, g_out_ssm, w_out, g_pre_ffn, g_post_ffn,
           w_router, b_router, we_gate, we_up, we_down, ws_gate, ws_up, ws_down):
    l = 0
    x2 = x.reshape(N_TOK, D_MODEL)
    r1 = lambda a: a.reshape(1, -1)

    c_pad = jnp.zeros((8, D_MODEL), F32).at[:BATCH].set(c)
    mod = _ada(c_pad, w_ada[l], r1(b_ada[l]))[:BATCH].reshape(BATCH, 6, D_MODEL)
    mod = jnp.concatenate([mod, jnp.zeros((BATCH, 2, D_MODEL), F32)], axis=1)

    v, u = _inproj(x2, mod, r1(g_pre_mix[l]), w_in[l].astype(BF16))
    cw = jnp.concatenate([conv_w[l].reshape(CONV_WIDTH, CONV_CH), jnp.zeros((1, CONV_CH), F32)], axis=0)
    a_n = _conv(v.reshape(BATCH, SEQ, CONV_CH), cw, r1(conv_b[l]), r1(conv_ln_g[l]),
                r1(conv_ln_b[l]), r1(g_out_conv[l])).reshape(N_TOK, CONV_CH)

    tg, wst, vg, a_cat, b_q, b_s = _s5_operators(
        ssm_a_re[l], ssm_a_im[l], ssm_log_dt[l], ssm_b_re[l], ssm_b_im[l], ssm_c_re[l], ssm_c_im[l])
    ut = (u.astype(BF16).reshape(S5_CHUNKS, S5_Q, SSM_GROUPS, SSM_GROUP_CH)
          .transpose(2, 0, 1, 3).reshape(SSM_GROUPS, S5_CHUNKS, S5_QH))
    sin = _s5_state_in(ut, wst)
    sprev_t = _s5_scan(sin.transpose(1, 0, 2), a_cat, b_q, b_s)
    yt = _s5_out(ut, tg, sprev_t.transpose(1, 0, 2), vg)
    y2 = (yt.reshape(SSM_GROUPS, S5_CHUNKS, S5_Q, SSM_GROUP_CH)
          .transpose(1, 2, 0, 3).reshape(N_TOK, SSM_CH))

    tm = TM_MIX
    tri = (jnp.arange(tm)[:, None] < jnp.arange(tm)[None, :]).astype(BF16)
    wo = w_out[l].astype(BF16)
    x1, h2, eidx, rank, gw, cnt = _mix(
        a_n, y2, u, x2, mod, r1(ssm_d[l]), ssm_w_glu[l].astype(BF16), r1(ssm_b_glu[l]),
        r1(g_out_ssm[l]), wo[:CONV_CH], wo[CONV_CH:], r1(g_post_mix[l]), r1(g_pre_ffn[l]),
        w_router[l].T, b_router[l].reshape(N_EXPERTS, 1), tri)

    counts = cnt[:, 0].astype(jnp.int32)
    padded = (counts + ROW_BLOCK - 1) // ROW_BLOCK * ROW_BLOCK
    pends = jnp.cumsum(padded)
    pstart = pends - padded
    e_ids = jnp.arange(N_EXPERTS, dtype=jnp.int32)
    dest = rank + jnp.sum(jnp.where(eidx[..., None] == e_ids, pstart, 0), axis=-1)
    n_used = (pends[-1] // ROW_BLOCK).astype(jnp.int32)
    blk = jnp.minimum(jnp.arange(N_BLOCKS, dtype=jnp.int32), n_used - 1) * ROW_BLOCK
    block_e = jnp.minimum(jnp.sum((pends[None, :] <= blk[:, None]).astype(jnp.int32), axis=1),
                          N_EXPERTS - 1)
    dest3 = dest.reshape(TOP_K, N_TOK // SC_W, SC_W).transpose(1, 0, 2)

    xs = _sc_dispatch(h2, dest3)
    ys = _experts(block_e, n_used.reshape(1), xs, we_gate[l], we_up[l], we_down[l])
    yg = _sc_combine(ys, dest3)

    ws_gu = jnp.concatenate([ws_gate[l], ws_up[l]], axis=1).astype(BF16)
    out = _final(h2, yg, gw.T, x1, mod, ws_gu, ws_down[l].astype(BF16), r1(g_post_ffn[l]))
    return out.reshape(BATCH, SEQ, D_MODEL)
```

```python
import functools
import math

import jax
import jax.numpy as jnp
from jax import lax
from jax.experimental import pallas as pl
from jax.experimental.pallas import tpu as pltpu
from jax.experimental.pallas import tpu_sc as plsc

F32 = jnp.float32
BF16 = jnp.bfloat16

D_MODEL = 1024
BATCH = 2
SEQ = 8192
N_TOK = BATCH * SEQ
CONV_CH = 512
CONV_WIDTH = 31
SSM_CH = 512
SSM_GROUP_CH = 16
SSM_GROUPS = 32
SSM_STATE = 64
D_IN = 2 * CONV_CH + SSM_CH
N_EXPERTS = 64
TOP_K = 8
N_ROUTE_GROUPS = 8
TOPK_ROUTE_GROUPS = 4
D_EXPERT = 256
ROUTED_SCALE = 2.5
NORM_EPS = 1e-6

TM_IN = 512
TL_CONV = 512
CONV_HALO = 32
CONV_ROWS = 64
S5_Q = 32
S5_QH = S5_Q * SSM_GROUP_CH
S5_CHUNKS = N_TOK // S5_Q
S5_CHUNKS_PER_SEQ = SEQ // S5_Q
TM_MIX = 512
ROW_BLOCK = 512
N_BLOCKS = N_TOK * TOP_K // ROW_BLOCK + N_EXPERTS
N_ROWS = N_BLOCKS * ROW_BLOCK
TM_OUT = 512
SC_CORES = 2
SC_SUBCORES = 16
SC_WORKERS = SC_CORES * SC_SUBCORES
SC_W = 64
SC_CHUNKS_PER_WORKER = N_TOK // (SC_WORKERS * SC_W)
VMEM_LIMIT = 48 * 1024 * 1024


def _cparams(sem):
    return pltpu.CompilerParams(dimension_semantics=sem, vmem_limit_bytes=VMEM_LIMIT)


def _pack_halves(x):
    n = x.shape[-1] // 2
    lo = lax.bitcast_convert_type(x[:, :n].astype(BF16).astype(F32), jnp.uint32)
    hi = lax.bitcast_convert_type(x[:, n:].astype(BF16).astype(F32), jnp.uint32)
    return hi | (lo >> 16)


def _unpack_halves(p):
    lo = lax.bitcast_convert_type(p << 16, F32)
    hi = lax.bitcast_convert_type(p & jnp.uint32(0xFFFF0000), F32)
    return lo, hi


def _rms(x, g):
    return x * lax.rsqrt(jnp.mean(x * x, axis=-1, keepdims=True) + NORM_EPS) * g


def _ada_kernel(c_ref, w_ref, b_ref, o_ref):
    c = c_ref[...]
    a = c * jax.nn.sigmoid(c)
    o_ref[...] = jnp.dot(a, w_ref[...], preferred_element_type=F32,
                         precision=lax.Precision.HIGHEST) + b_ref[...]


def _ada(c_pad, w_ada, b_ada):
    n = w_ada.shape[1]
    bn = 1536
    return pl.pallas_call(
        _ada_kernel,
        grid=(n // bn,),
        in_specs=[pl.BlockSpec((8, D_MODEL), lambda j: (0, 0)),
                  pl.BlockSpec((D_MODEL, bn), lambda j: (0, j)),
                  pl.BlockSpec((1, bn), lambda j: (0, j))],
        out_specs=pl.BlockSpec((8, bn), lambda j: (0, j)),
        out_shape=jax.ShapeDtypeStruct((8, n), F32),
        compiler_params=_cparams(("arbitrary",)),
        name="ada_mod",
    )(c_pad, w_ada, b_ada)


LANES = 128
GROUPS_PER_LANE_TILE = LANES // SSM_GROUP_CH


def _to_group_chunks(u, tile_ref, ut_ref):
    n_chunks = u.shape[0] // S5_Q
    for j in range(SSM_CH // LANES):
        tile_ref[j] = u[:, LANES * j:LANES * (j + 1)]
    for j in range(SSM_CH // LANES):
        rows_t = [tile_ref[j, pl.ds(t, n_chunks, stride=S5_Q), :] for t in range(S5_Q)]
        for gg in range(GROUPS_PER_LANE_TILE):
            lo = gg * SSM_GROUP_CH
            row = jnp.concatenate([r[:, lo:lo + SSM_GROUP_CH] for r in rows_t], axis=1)
            ut_ref[j * GROUPS_PER_LANE_TILE + gg] = row.astype(ut_ref.dtype)


def _from_group_chunks(yt_ref, tile_ref):
    n_chunks = yt_ref.shape[1]
    for j in range(SSM_CH // LANES):
        for t in range(S5_Q):
            lo = t * SSM_GROUP_CH
            piece = jnp.concatenate(
                [yt_ref[j * GROUPS_PER_LANE_TILE + gg, :, lo:lo + SSM_GROUP_CH]
                 for gg in range(GROUPS_PER_LANE_TILE)], axis=1)
            tile_ref[j, pl.ds(t, n_chunks, stride=S5_Q), :] = piece
    return jnp.concatenate([tile_ref[j] for j in range(SSM_CH // LANES)], axis=1)


def _inproj_kernel(x_ref, mod_ref, g_ref, w_ref, v_ref, u_ref, ut_ref, tile_ref):
    x = x_ref[...]
    sh = mod_ref[0, 0:1, :]
    sc = mod_ref[0, 1:2, :]
    h = _rms(x, g_ref[...]) * (1.0 + sc) + sh
    z = jnp.dot(h.astype(BF16), w_ref[...], preferred_element_type=F32)
    v_ref[...] = z[:, :CONV_CH] * jax.nn.sigmoid(z[:, CONV_CH:2 * CONV_CH])
    u = z[:, 2 * CONV_CH:]
    u_ref[...] = u
    _to_group_chunks(u, tile_ref, ut_ref)


def _inproj(x2, mod, g_pre, w_in_bf):
    tiles_per_seq = SEQ // TM_IN
    return pl.pallas_call(
        _inproj_kernel,
        grid=(N_TOK // TM_IN,),
        in_specs=[pl.BlockSpec((TM_IN, D_MODEL), lambda i: (i, 0)),
                  pl.BlockSpec((1, 8, D_MODEL), lambda i: (i // tiles_per_seq, 0, 0)),
                  pl.BlockSpec((1, D_MODEL), lambda i: (0, 0)),
                  pl.BlockSpec((D_MODEL, D_IN), lambda i: (0, 0))],
        out_specs=[pl.BlockSpec((TM_IN, CONV_CH), lambda i: (i, 0)),
                   pl.BlockSpec((TM_IN, SSM_CH), lambda i: (i, 0)),
                   pl.BlockSpec((SSM_GROUPS, TM_IN // S5_Q, S5_QH), lambda i: (0, i, 0))],
        out_shape=[jax.ShapeDtypeStruct((N_TOK, CONV_CH), F32),
                   jax.ShapeDtypeStruct((N_TOK, SSM_CH), F32),
                   jax.ShapeDtypeStruct((SSM_GROUPS, S5_CHUNKS, S5_QH), BF16)],
        scratch_shapes=[pltpu.VMEM((SSM_CH // LANES, TM_IN, LANES), F32)],
        compiler_params=_cparams(("parallel",)),
        name="in_proj",
    )(x2, mod, g_pre, w_in_bf)


def _conv_kernel(vc_ref, vp_ref, w_ref, cb_ref, lg_ref, lb_ref, go_ref, o_ref, ext_ref):
    i = pl.program_id(1)
    keep = (i > 0).astype(F32)
    ext_ref[0:CONV_HALO, :] = vp_ref[0] * keep
    ext_ref[CONV_HALO:, :] = vc_ref[0]
    off = CONV_HALO - (CONV_WIDTH - 1)
    for r in range(TL_CONV // CONV_ROWS):
        base = r * CONV_ROWS + off
        acc = w_ref[0:1, :] * ext_ref[base:base + CONV_ROWS, :]
        for j in range(1, CONV_WIDTH):
            acc = acc + w_ref[j:j + 1, :] * ext_ref[base + j:base + j + CONV_ROWS, :]
        y = acc + cb_ref[...]
        mu = jnp.mean(y, axis=-1, keepdims=True)
        d = y - mu
        var = jnp.mean(d * d, axis=-1, keepdims=True)
        yn = d * lax.rsqrt(var + NORM_EPS) * lg_ref[...] + lb_ref[...]
        a = yn * jax.nn.sigmoid(yn)
        o_ref[0, r * CONV_ROWS:(r + 1) * CONV_ROWS, :] = _rms(a, go_ref[...]).astype(BF16)


def _conv(v3, conv_w, conv_b, ln_g, ln_b, g_out):
    halo_per_tile = TL_CONV // CONV_HALO
    vec = pl.BlockSpec((1, CONV_CH), lambda b, i: (0, 0))
    return pl.pallas_call(
        _conv_kernel,
        grid=(BATCH, SEQ // TL_CONV),
        in_specs=[pl.BlockSpec((1, TL_CONV, CONV_CH), lambda b, i: (b, i, 0)),
                  pl.BlockSpec((1, CONV_HALO, CONV_CH),
                               lambda b, i: (b, jnp.maximum(i * halo_per_tile - 1, 0), 0)),
                  pl.BlockSpec((CONV_WIDTH + 1, CONV_CH), lambda b, i: (0, 0)),
                  vec, vec, vec, vec],
        out_specs=pl.BlockSpec((1, TL_CONV, CONV_CH), lambda b, i: (b, i, 0)),
        out_shape=jax.ShapeDtypeStruct((BATCH, SEQ, CONV_CH), BF16),
        scratch_shapes=[pltpu.VMEM((TL_CONV + CONV_HALO, CONV_CH), F32)],
        compiler_params=_cparams(("parallel", "arbitrary")),
        name="conv_module",
    )(v3, v3, conv_w, conv_b, ln_g, ln_b, g_out)


S5_GROUP_ROWS = S5_CHUNKS + 8


def _s5_kernel(ut_ref, wst_ref, tg_ref, vg_ref, a_ref, bq_ref, bs_ref, yt_ref, sin_s, sp_s):
    phase = pl.program_id(0)
    g = pl.program_id(1)
    n = 2 * SSM_STATE
    row0 = pl.multiple_of(g * S5_GROUP_ROWS, 8)

    @pl.when(phase == 0)
    def _():
        r = jnp.dot(ut_ref[0], wst_ref[0], preferred_element_type=F32)
        sin_s[0, pl.ds(row0, S5_CHUNKS), :] = r[:, :n]
        sin_s[1, pl.ds(row0, S5_CHUNKS), :] = r[:, n:]

    @pl.when((phase == 1) & (g == 0))
    def _():
        a = a_ref[...]
        bq = bq_ref[...]
        bs = bs_ref[...]

        def body(c, carry):
            nxt = []
            for b in range(BATCH):
                x, xs = carry[b]
                rows = pl.ds(b * S5_CHUNKS_PER_SEQ + c, SSM_GROUPS, stride=S5_GROUP_ROWS)
                sp_s[rows, :] = x
                nxt.append((a * x + bq * xs + sin_s[0, rows, :], a * xs + bs * x + sin_s[1, rows, :]))
            return tuple(nxt)

        z = jnp.zeros((SSM_GROUPS, n), F32)
        lax.fori_loop(0, S5_CHUNKS_PER_SEQ, body, tuple((z, z) for _ in range(BATCH)))

    @pl.when(phase == 1)
    def _():
        sp = sp_s[pl.ds(row0, S5_CHUNKS), :]
        y = jnp.dot(ut_ref[0], tg_ref[0], preferred_element_type=F32)
        yt_ref[0] = y + jnp.dot(sp.astype(BF16), vg_ref[0], preferred_element_type=F32)


def _s5(ut, wst, tg, vg, a_cat, b_q, b_s):
    last = SSM_GROUPS - 1
    vec = pl.BlockSpec((SSM_GROUPS, 2 * SSM_STATE), lambda p, g: (0, 0))
    return pl.pallas_call(
        _s5_kernel,
        grid=(2, SSM_GROUPS),
        in_specs=[pl.BlockSpec((1, S5_CHUNKS, S5_QH), lambda p, g: (g, 0, 0)),
                  pl.BlockSpec((1, S5_QH, 4 * SSM_STATE), lambda p, g: (g * (1 - p) + last * p, 0, 0)),
                  pl.BlockSpec((1, S5_QH, S5_QH), lambda p, g: (g * p, 0, 0)),
                  pl.BlockSpec((1, 2 * SSM_STATE, S5_QH), lambda p, g: (g * p, 0, 0)),
                  vec, vec, vec],
        out_specs=pl.BlockSpec((1, S5_CHUNKS, S5_QH), lambda p, g: (g * p, 0, 0)),
        out_shape=jax.ShapeDtypeStruct((SSM_GROUPS, S5_CHUNKS, S5_QH), F32),
        scratch_shapes=[pltpu.VMEM((2, SSM_GROUPS * S5_GROUP_ROWS, 2 * SSM_STATE), F32),
                        pltpu.VMEM((SSM_GROUPS * S5_GROUP_ROWS, 2 * SSM_STATE), F32)],
        compiler_params=_cparams(("arbitrary", "arbitrary")),
        name="s5_chunked",
    )(ut, wst, tg, vg, a_cat, b_q, b_s)


def _s5_operators(a_re, a_im, log_dt, b_re, b_im, c_re, c_im):
    q = S5_Q
    dt = jnp.exp(log_dt)[:, None]
    ar, ai = a_re, a_im
    mag = jnp.exp(ar * dt)
    lr = mag * jnp.cos(ai * dt)
    li = mag * jnp.sin(ai * dt)
    den = ar * ar + ai * ai
    nr = lr - 1.0
    kr = (nr * ar + li * ai) / den
    ki = (li * ar - nr * ai) / den
    bbr = kr[..., None] * b_re - ki[..., None] * b_im
    bbi = kr[..., None] * b_im + ki[..., None] * b_re
    j = jnp.arange(q + 1, dtype=F32)[None, :, None]
    pmag = jnp.exp(ar[:, None, :] * dt[:, :, None] * j)
    pang = ai[:, None, :] * dt[:, :, None] * j
    pr = pmag * jnp.cos(pang)
    pi_ = pmag * jnp.sin(pang)
    hi = lax.Precision.HIGHEST
    cl_r = c_re[:, None] * pr[:, :q, None, :] - c_im[:, None] * pi_[:, :q, None, :]
    cl_i = c_re[:, None] * pi_[:, :q, None, :] + c_im[:, None] * pr[:, :q, None, :]
    kk = (jnp.einsum('gjhp,gpk->gjhk', cl_r, bbr, precision=hi)
          - jnp.einsum('gjhp,gpk->gjhk', cl_i, bbi, precision=hi))
    tp = jnp.arange(q)[:, None]
    t = jnp.arange(q)[None, :]
    lag = t - tp
    kt = kk[:, jnp.clip(lag, 0, q - 1)]
    kt = jnp.where((lag >= 0)[None, :, :, None, None], kt, 0.0)
    tg = kt.transpose(0, 1, 4, 2, 3).reshape(SSM_GROUPS, S5_QH, S5_QH)
    wr = pr[:, q - 1::-1][:, :q]
    wi = pi_[:, q - 1::-1][:, :q]
    w_re = wr[:, :, None, :] * bbr.transpose(0, 2, 1)[:, None] - wi[:, :, None, :] * bbi.transpose(0, 2, 1)[:, None]
    w_im = wr[:, :, None, :] * bbi.transpose(0, 2, 1)[:, None] + wi[:, :, None, :] * bbr.transpose(0, 2, 1)[:, None]
    w_re = w_re.reshape(SSM_GROUPS, S5_QH, SSM_STATE)
    w_im = w_im.reshape(SSM_GROUPS, S5_QH, SSM_STATE)
    wst = jnp.concatenate([w_re, w_im, w_im, w_re], axis=-1)
    vl_r = c_re[:, None] * pr[:, 1:, None, :] - c_im[:, None] * pi_[:, 1:, None, :]
    vl_i = c_re[:, None] * pi_[:, 1:, None, :] + c_im[:, None] * pr[:, 1:, None, :]
    v_re = vl_r.transpose(0, 3, 1, 2).reshape(SSM_GROUPS, SSM_STATE, S5_QH)
    v_im = vl_i.transpose(0, 3, 1, 2).reshape(SSM_GROUPS, SSM_STATE, S5_QH)
    vg = jnp.concatenate([v_re, -v_im], axis=1)
    aq_r, aq_i = pr[:, q], pi_[:, q]
    a_cat = jnp.concatenate([aq_r, aq_r], axis=-1)
    b_q = jnp.concatenate([-aq_i, aq_i], axis=-1)
    b_s = jnp.concatenate([aq_i, -aq_i], axis=-1)
    return tg.astype(BF16), wst.astype(BF16), vg.astype(BF16), a_cat, b_q, b_s


def _gelu_tanh(x):
    return 0.5 * x * (1.0 + jnp.tanh(math.sqrt(2.0 / math.pi) * (x + 0.044715 * (x * x * x))))


def _mix_kernel(an_ref, yt_ref, u_ref, x_ref, mod_ref, d_ref, wglu_ref, bglu_ref, gos_ref,
                woa_ref, wob_ref, gpm_ref, gpf_ref, wr_ref, br_ref, tri_ref,
                x1_ref, h2_ref, eidx_ref, rank_ref, gw_ref, cnt_ref, run_ref, tile_ref):
    i = pl.program_id(0)
    tm = TM_MIX

    @pl.when(i == 0)
    def _():
        run_ref[...] = jnp.zeros_like(run_ref)

    gt_m = mod_ref[0, 2:3, :]
    sh_f = mod_ref[0, 3:4, :]
    sc_f = mod_ref[0, 4:5, :]

    yy = _from_group_chunks(yt_ref, tile_ref) + d_ref[...] * u_ref[...]
    g = _gelu_tanh(yy)
    gl = jnp.dot(g.astype(BF16), wglu_ref[...], preferred_element_type=F32) + bglu_ref[...]
    ob = g * jax.nn.sigmoid(gl)
    bn = _rms(ob, gos_ref[...]).astype(BF16)
    o = (jnp.dot(an_ref[...], woa_ref[...], preferred_element_type=F32)
         + jnp.dot(bn, wob_ref[...], preferred_element_type=F32))
    x1 = x_ref[...] + gt_m * _rms(o, gpm_ref[...])
    x1_ref[...] = x1
    h2 = _rms(x1, gpf_ref[...]) * (1.0 + sc_f) + sh_f
    h2_ref[...] = _pack_halves(h2)

    logits = lax.dot_general(wr_ref[...], h2, (((1,), (1,)), ((), ())),
                             preferred_element_type=F32, precision=lax.Precision.HIGHEST)
    scores = jax.nn.sigmoid(logits)
    biased = scores + br_ref[...]
    ng = N_ROUTE_GROUPS
    gsz = N_EXPERTS // ng
    b3 = biased.reshape(ng, gsz, tm)
    s3 = scores.reshape(ng, gsz, tm)
    sub = lax.broadcasted_iota(jnp.int32, (ng, gsz, tm), 1).astype(F32)
    grp = lax.broadcasted_iota(jnp.int32, (ng, gsz, tm), 0).astype(F32)
    eid = grp * gsz + sub
    neg = -jnp.inf
    m1 = jnp.max(b3, axis=1, keepdims=True)
    i1 = jnp.min(jnp.where(b3 == m1, sub, float(gsz)), axis=1, keepdims=True)
    m2 = jnp.max(jnp.where(sub == i1, neg, b3), axis=1, keepdims=True)
    gs = m1 + m2
    gi = lax.broadcasted_iota(jnp.int32, (ng, 1, tm), 0)
    beaten = jnp.zeros((ng, 1, tm), F32)
    for gp in range(ng):
        o_ = gs[gp:gp + 1]
        beats = (o_ > gs) | ((o_ == gs) & (gi > gp))
        beaten = beaten + beats.astype(F32)
    gmask = beaten < float(TOPK_ROUTE_GROUPS)
    masked = jnp.where(gmask, b3, neg)

    sels = []
    picked = jnp.zeros((ng, gsz, tm), F32)
    for k in range(TOP_K):
        m = jnp.max(jnp.max(masked, axis=0, keepdims=True), axis=1, keepdims=True)
        cand = jnp.where(masked == m, eid, float(N_EXPERTS))
        sel = jnp.min(jnp.min(cand, axis=0, keepdims=True), axis=1, keepdims=True)
        oh = eid == sel
        masked = jnp.where(oh, neg, masked)
        picked = jnp.where(oh, 1.0, picked)
        sels.append(sel)

    pm = picked.reshape(N_EXPERTS, tm)
    prefix = jnp.dot(pm.astype(BF16), tri_ref[...], preferred_element_type=F32) + run_ref[:, 0:1]
    p3 = prefix.reshape(ng, gsz, tm)
    run_new = run_ref[...] + jnp.sum(pm, axis=1, keepdims=True)
    run_ref[...] = run_new
    cnt_ref[...] = run_new

    sc_rows = []
    for k in range(TOP_K):
        oh = eid == sels[k]
        sc_k = jnp.sum(jnp.sum(jnp.where(oh, s3, 0.0), axis=0, keepdims=True), axis=1, keepdims=True)
        rk_k = jnp.sum(jnp.sum(jnp.where(oh, p3, 0.0), axis=0, keepdims=True), axis=1, keepdims=True)
        sc_rows.append(sc_k)
        eidx_ref[k:k + 1, :] = sels[k].reshape(1, tm).astype(jnp.int32)
        rank_ref[k:k + 1, :] = rk_k.reshape(1, tm).astype(jnp.int32)
    tot = sc_rows[0]
    for k in range(1, TOP_K):
        tot = tot + sc_rows[k]
    inv = ROUTED_SCALE / (tot + 1e-20)
    for k in range(TOP_K):
        gw_ref[k:k + 1, :] = (sc_rows[k] * inv).reshape(1, tm)


def _mix(a_n, yt, u2, x2, mod, d_skip, wglu_bf, b_glu, g_out_ssm, wo_a, wo_b, g_post_mix,
         g_pre_ffn, w_router_t, b_router_col, tri):
    tm = TM_MIX
    tiles_per_seq = SEQ // tm
    row = lambda n: pl.BlockSpec((1, n), lambda i: (0, 0))
    full = lambda a, b: pl.BlockSpec((a, b), lambda i: (0, 0))
    tok = lambda n: pl.BlockSpec((tm, n), lambda i: (i, 0))
    col = pl.BlockSpec((TOP_K, tm), lambda i: (0, i))
    return pl.pallas_call(
        _mix_kernel,
        grid=(N_TOK // tm,),
        in_specs=[tok(CONV_CH),
                  pl.BlockSpec((SSM_GROUPS, tm // S5_Q, S5_QH), lambda i: (0, i, 0)),
                  tok(SSM_CH), tok(D_MODEL),
                  pl.BlockSpec((1, 8, D_MODEL), lambda i: (i // tiles_per_seq, 0, 0)),
                  row(SSM_CH), full(SSM_CH, SSM_CH), row(SSM_CH), row(SSM_CH),
                  full(CONV_CH, D_MODEL), full(SSM_CH, D_MODEL), row(D_MODEL), row(D_MODEL),
                  full(N_EXPERTS, D_MODEL), full(N_EXPERTS, 1), full(tm, tm)],
        out_specs=[tok(D_MODEL), tok(D_MODEL // 2), col, col, col,
                   pl.BlockSpec((N_EXPERTS, 128), lambda i: (0, 0))],
        out_shape=[jax.ShapeDtypeStruct((N_TOK, D_MODEL), F32),
                   jax.ShapeDtypeStruct((N_TOK, D_MODEL // 2), jnp.uint32),
                   jax.ShapeDtypeStruct((TOP_K, N_TOK), jnp.int32),
                   jax.ShapeDtypeStruct((TOP_K, N_TOK), jnp.int32),
                   jax.ShapeDtypeStruct((TOP_K, N_TOK), F32),
                   jax.ShapeDtypeStruct((N_EXPERTS, 128), F32)],
        scratch_shapes=[pltpu.VMEM((N_EXPERTS, 128), F32),
                        pltpu.VMEM((SSM_CH // LANES, tm, LANES), F32)],
        compiler_params=_cparams(("arbitrary",)),
        name="mix_out_router",
    )(a_n, yt, u2, x2, mod, d_skip, wglu_bf, b_glu, g_out_ssm, wo_a, wo_b, g_post_mix,
      g_pre_ffn, w_router_t, b_router_col, tri)


def _expert_kernel(be_ref, nu_ref, xs_ref, wg_ref, wu_ref, wd_ref, ys_ref, wgu_s, wd_s):
    j = pl.program_id(0)
    prev = be_ref[jnp.maximum(j - 1, 0)]
    fresh = (j == 0) | (be_ref[j] != prev)

    @pl.when(fresh)
    def _():
        wgu_s[:, :D_EXPERT] = wg_ref[0].astype(BF16)
        wgu_s[:, D_EXPERT:] = wu_ref[0].astype(BF16)
        wd_s[...] = wd_ref[0].astype(BF16)

    @pl.when(j < nu_ref[0])
    def _():
        x_lo, x_hi = _unpack_halves(xs_ref[...])
        x = jnp.concatenate([x_lo.astype(BF16), x_hi.astype(BF16)], axis=1)
        h = jnp.dot(x, wgu_s[...], preferred_element_type=F32)
        hg = h[:, :D_EXPERT]
        act = hg * jax.nn.sigmoid(hg) * h[:, D_EXPERT:]
        ys_ref[...] = _pack_halves(jnp.dot(act.astype(BF16), wd_s[...], preferred_element_type=F32))

    @pl.when(j >= nu_ref[0])
    def _():
        ys_ref[...] = jnp.zeros_like(ys_ref)


def _experts(block_e, n_used, xs, we_gate, we_up, we_down):
    grid_spec = pltpu.PrefetchScalarGridSpec(
        num_scalar_prefetch=2,
        grid=(N_BLOCKS,),
        in_specs=[pl.BlockSpec((ROW_BLOCK, D_MODEL // 2), lambda j, be, nu: (jnp.minimum(j, nu[0] - 1), 0)),
                  pl.BlockSpec((1, D_MODEL, D_EXPERT), lambda j, be, nu: (be[j], 0, 0)),
                  pl.BlockSpec((1, D_MODEL, D_EXPERT), lambda j, be, nu: (be[j], 0, 0)),
                  pl.BlockSpec((1, D_EXPERT, D_MODEL), lambda j, be, nu: (be[j], 0, 0))],
        out_specs=pl.BlockSpec((ROW_BLOCK, D_MODEL // 2), lambda j, be, nu: (j, 0)),
        scratch_shapes=[pltpu.VMEM((D_MODEL, 2 * D_EXPERT), BF16),
                        pltpu.VMEM((D_EXPERT, D_MODEL), BF16)],
    )
    return pl.pallas_call(
        _expert_kernel,
        grid_spec=grid_spec,
        out_shape=jax.ShapeDtypeStruct((N_ROWS, D_MODEL // 2), jnp.uint32),
        compiler_params=_cparams(("arbitrary",)),
        name="routed_experts",
    )(block_e, n_used, xs, we_gate, we_up, we_down)


def _final_kernel(h2_ref, yg_ref, gw_ref, x1_ref, mod_ref, wgu_ref, wd_ref, g_ref, o_ref):
    half = D_MODEL // 2
    gt_f = mod_ref[0, 5:6, :]
    x_lo, x_hi = _unpack_halves(h2_ref[...])
    h = (jnp.dot(x_lo.astype(BF16), wgu_ref[:half, :], preferred_element_type=F32)
         + jnp.dot(x_hi.astype(BF16), wgu_ref[half:, :], preferred_element_type=F32))
    hg = h[:, :D_EXPERT]
    act = hg * jax.nn.sigmoid(hg) * h[:, D_EXPERT:]
    shared = jnp.dot(act.astype(BF16), wd_ref[...], preferred_element_type=F32)
    y_lo = shared[:, :half]
    y_hi = shared[:, half:]
    for k in range(TOP_K):
        r_lo, r_hi = _unpack_halves(yg_ref[k])
        w = gw_ref[:, k:k + 1]
        y_lo = y_lo + w * r_lo
        y_hi = y_hi + w * r_hi
    ms = (jnp.sum(y_lo * y_lo, axis=-1, keepdims=True)
          + jnp.sum(y_hi * y_hi, axis=-1, keepdims=True)) * (1.0 / D_MODEL)
    inv = lax.rsqrt(ms + NORM_EPS)
    o_ref[:, :half] = x1_ref[:, :half] + gt_f[:, :half] * (y_lo * inv * g_ref[:, :half])
    o_ref[:, half:] = x1_ref[:, half:] + gt_f[:, half:] * (y_hi * inv * g_ref[:, half:])


def _final(h2p, yg, gw_t, x1, mod, ws_gu, ws_d, g_post_ffn):
    tm = TM_OUT
    tiles_per_seq = SEQ // tm
    tok = pl.BlockSpec((tm, D_MODEL), lambda i: (i, 0))
    return pl.pallas_call(
        _final_kernel,
        grid=(N_TOK // tm,),
        in_specs=[pl.BlockSpec((tm, D_MODEL // 2), lambda i: (i, 0)),
                  pl.BlockSpec((TOP_K, tm, D_MODEL // 2), lambda i: (0, i, 0)),
                  pl.BlockSpec((tm, TOP_K), lambda i: (i, 0)),
                  tok,
                  pl.BlockSpec((1, 8, D_MODEL), lambda i: (i // tiles_per_seq, 0, 0)),
                  pl.BlockSpec((D_MODEL, 2 * D_EXPERT), lambda i: (0, 0)),
                  pl.BlockSpec((D_EXPERT, D_MODEL), lambda i: (0, 0)),
                  pl.BlockSpec((1, D_MODEL), lambda i: (0, 0))],
        out_specs=tok,
        out_shape=jax.ShapeDtypeStruct((N_TOK, D_MODEL), F32),
        compiler_params=_cparams(("parallel",)),
        name="shared_final",
    )(h2p, yg, gw_t, x1, mod, ws_gu, ws_d, g_post_ffn)


def _sc_worker_id():
    return lax.axis_index("s") * SC_CORES + lax.axis_index("c")


def _dispatch_body(h_hbm, dest_hbm, xs_hbm, idx_v, rows_v, sem_l, sem_s):
    n = SC_CHUNKS_PER_WORKER
    c0 = _sc_worker_id() * n

    def load(i, b):
        return pltpu.async_copy(h_hbm.at[pl.ds((c0 + i) * SC_W, SC_W)], rows_v.at[b], sem_l.at[b])

    loads = [None] * n
    scat = [None] * n
    loads[0] = load(0, 0)
    for i in range(n):
        b = i % 2
        pltpu.sync_copy(dest_hbm.at[c0 + i], idx_v.at[b])
        loads[i].wait()
        if i + 1 < n:
            if i >= 1:
                for d in scat[i - 1]:
                    d.wait()
            loads[i + 1] = load(i + 1, 1 - b)
        scat[i] = [pltpu.async_copy(rows_v.at[b], xs_hbm.at[idx_v.at[b].at[k]], sem_s.at[b])
                   for k in range(TOP_K)]
    for i in (n - 2, n - 1):
        for d in scat[i]:
            d.wait()


def _sc_dispatch(h2p, dest3):
    mesh = plsc.VectorSubcoreMesh(core_axis_name="c", subcore_axis_name="s")
    return pl.kernel(
        _dispatch_body, mesh=mesh,
        out_type=jax.ShapeDtypeStruct((N_ROWS, D_MODEL // 2), jnp.uint32),
        scratch_types=[pltpu.VMEM((2, TOP_K, SC_W), jnp.int32),
                       pltpu.VMEM((2, SC_W, D_MODEL // 2), jnp.uint32),
                       pltpu.SemaphoreType.DMA((2,)), pltpu.SemaphoreType.DMA((2,))],
    )(h2p, dest3)


def _combine_body(ys_hbm, dest_hbm, yg_hbm, idx_v, rows_v, sem_g, sem_w):
    c0 = _sc_worker_id() * SC_CHUNKS_PER_WORKER

    @pl.loop(0, SC_CHUNKS_PER_WORKER)
    def _(i):
        c = c0 + i
        pltpu.sync_copy(dest_hbm.at[c], idx_v)
        g = [None] * TOP_K
        w = [None] * TOP_K
        g[0] = pltpu.async_copy(ys_hbm.at[idx_v.at[0]], rows_v.at[0], sem_g.at[0])
        for k in range(TOP_K):
            b = k % 2
            g[k].wait()
            if k + 1 < TOP_K:
                if k >= 1:
                    w[k - 1].wait()
                g[k + 1] = pltpu.async_copy(ys_hbm.at[idx_v.at[k + 1]], rows_v.at[1 - b], sem_g.at[1 - b])
            w[k] = pltpu.async_copy(rows_v.at[b], yg_hbm.at[k].at[pl.ds(c * SC_W, SC_W)], sem_w.at[b])
        w[TOP_K - 2].wait()
        w[TOP_K - 1].wait()


def _sc_combine(ysp, dest3):
    mesh = plsc.VectorSubcoreMesh(core_axis_name="c", subcore_axis_name="s")
    return pl.kernel(
        _combine_body, mesh=mesh,
        out_type=jax.ShapeDtypeStruct((TOP_K, N_TOK, D_MODEL // 2), jnp.uint32),
        scratch_types=[pltpu.VMEM((TOP_K, SC_W), jnp.int32),
                       pltpu.VMEM((2, SC_W, D_MODEL // 2), jnp.uint32),
                       pltpu.SemaphoreType.DMA((2,)), pltpu.SemaphoreType.DMA((2,))],
    )(ysp, dest3)


def kernel(x, c, w_ada, b_ada, g_pre_mix, g_post_mix, w_in, conv_w, conv_b, conv_ln_g, conv_ln_b,
           ssm_a_re, ssm_a_im, ssm_log_dt, ssm_b_re, ssm_b_im, ssm_c_re, ssm_c_im, ssm_d,
           ssm_w_glu, ssm_b_glu, g_out_conv, g_out_ssm, w_out, g_pre_ffn, g_post_ffn,
           w_router, b_router, we_gate, we_up, we_down, ws_gate, ws_up, ws_down):
    l = 0
    x2 = x.reshape(N_TOK, D_MODEL)
    r1 = lambda a: a.reshape(1, -1)

    c_pad = jnp.zeros((8, D_MODEL), F32).at[:BATCH].set(c)
    mod = _ada(c_pad, w_ada[l], r1(b_ada[l]))[:BATCH].reshape(BATCH, 6, D_MODEL)
    mod = jnp.concatenate([mod, jnp.zeros((BATCH, 2, D_MODEL), F32)], axis=1)

    v, u, ut = _inproj(x2, mod, r1(g_pre_mix[l]), w_in[l].astype(BF16))
    cw = jnp.concatenate([conv_w[l].reshape(CONV_WIDTH, CONV_CH), jnp.zeros((1, CONV_CH), F32)], axis=0)
    a_n = _conv(v.reshape(BATCH, SEQ, CONV_CH), cw, r1(conv_b[l]), r1(conv_ln_g[l]),
                r1(conv_ln_b[l]), r1(g_out_conv[l])).reshape(N_TOK, CONV_CH)

    tg, wst, vg, a_cat, b_q, b_s = _s5_operators(
        ssm_a_re[l], ssm_a_im[l], ssm_log_dt[l], ssm_b_re[l], ssm_b_im[l], ssm_c_re[l], ssm_c_im[l])
    yt = _s5(ut, wst, tg, vg, a_cat, b_q, b_s)

    tm = TM_MIX
    tri = (jnp.arange(tm)[:, None] < jnp.arange(tm)[None, :]).astype(BF16)
    wo = w_out[l].astype(BF16)
    x1, h2, eidx, rank, gw, cnt = _mix(
        a_n, yt, u, x2, mod, r1(ssm_d[l]), ssm_w_glu[l].astype(BF16), r1(ssm_b_glu[l]),
        r1(g_out_ssm[l]), wo[:CONV_CH], wo[CONV_CH:], r1(g_post_mix[l]), r1(g_pre_ffn[l]),
        w_router[l].T, b_router[l].reshape(N_EXPERTS, 1), tri)

    counts = cnt[:, 0].astype(jnp.int32)
    padded = (counts + ROW_BLOCK - 1) // ROW_BLOCK * ROW_BLOCK
    pends = jnp.cumsum(padded)
    pstart = pends - padded
    e_ids = jnp.arange(N_EXPERTS, dtype=jnp.int32)
    dest = rank + jnp.sum(jnp.where(eidx[..., None] == e_ids, pstart, 0), axis=-1)
    n_used = (pends[-1] // ROW_BLOCK).astype(jnp.int32)
    blk = jnp.minimum(jnp.arange(N_BLOCKS, dtype=jnp.int32), n_used - 1) * ROW_BLOCK
    block_e = jnp.minimum(jnp.sum((pends[None, :] <= blk[:, None]).astype(jnp.int32), axis=1),
                          N_EXPERTS - 1)
    dest3 = dest.reshape(TOP_K, N_TOK // SC_W, SC_W).transpose(1, 0, 2)

    xs = _sc_dispatch(h2, dest3)
    ys = _experts(block_e, n_used.reshape(1), xs, we_gate[l], we_up[l], we_down[l])
    yg = _sc_combine(ys, dest3)

    ws_gu = jnp.concatenate([ws_gate[l], ws_up[l]], axis=1).astype(BF16)
    out = _final(h2, yg, gw.T, x1, mod, ws_gu, ws_down[l].astype(BF16), r1(g_post_ffn[l]))
    return out.reshape(BATCH, SEQ, D_MODEL)
```

```python
import functools
import math

import jax
import jax.numpy as jnp
from jax import lax
from jax.experimental import pallas as pl
from jax.experimental.pallas import tpu as pltpu
from jax.experimental.pallas import tpu_sc as plsc

F32 = jnp.float32
BF16 = jnp.bfloat16

D_MODEL = 1024
BATCH = 2
SEQ = 8192
N_TOK = BATCH * SEQ
CONV_CH = 512
CONV_WIDTH = 31
SSM_CH = 512
SSM_GROUP_CH = 16
SSM_GROUPS = 32
SSM_STATE = 64
D_IN = 2 * CONV_CH + SSM_CH
N_EXPERTS = 64
TOP_K = 8
N_ROUTE_GROUPS = 8
TOPK_ROUTE_GROUPS = 4
D_EXPERT = 256
ROUTED_SCALE = 2.5
NORM_EPS = 1e-6

SUBLANES = 8
LANES = 128

TM_IN = 512
TL_CONV = 512
CONV_HALO = 32
CONV_ROWS = 64
S5_Q = 32
S5_QH = S5_Q * SSM_GROUP_CH
S5_CHUNKS = N_TOK // S5_Q
S5_CHUNKS_PER_SEQ = SEQ // S5_Q
TM_MIX = 512
ROW_BLOCK = 512
N_BLOCKS = N_TOK * TOP_K // ROW_BLOCK + N_EXPERTS
N_ROWS = N_BLOCKS * ROW_BLOCK
TM_OUT = 512
SC_CORES = 2
SC_SUBCORES = 16
SC_WORKERS = SC_CORES * SC_SUBCORES
SC_W = 64
SC_CHUNKS_PER_WORKER = N_TOK // (SC_WORKERS * SC_W)
VMEM_LIMIT = 48 * 1024 * 1024


def _cparams(sem):
    return pltpu.CompilerParams(dimension_semantics=sem, vmem_limit_bytes=VMEM_LIMIT)


def _pack_halves(x):
    n = x.shape[-1] // 2
    lo = lax.bitcast_convert_type(x[:, :n].astype(BF16).astype(F32), jnp.uint32)
    hi = lax.bitcast_convert_type(x[:, n:].astype(BF16).astype(F32), jnp.uint32)
    return hi | (lo >> 16)


def _unpack_halves(p):
    lo = lax.bitcast_convert_type(p << 16, F32)
    hi = lax.bitcast_convert_type(p & jnp.uint32(0xFFFF0000), F32)
    return lo, hi


def _rms(x, g):
    return x * lax.rsqrt(jnp.mean(x * x, axis=-1, keepdims=True) + NORM_EPS) * g


def _ada_kernel(c_ref, w_ref, b_ref, o_ref):
    c = c_ref[...]
    a = c * jax.nn.sigmoid(c)
    o_ref[...] = jnp.dot(a, w_ref[...], preferred_element_type=F32,
                         precision=lax.Precision.HIGHEST) + b_ref[...]


def _ada(c_pad, w_ada, b_ada):
    n = w_ada.shape[1]
    bn = 1536
    return pl.pallas_call(
        _ada_kernel,
        grid=(n // bn,),
        in_specs=[pl.BlockSpec((8, D_MODEL), lambda j: (0, 0)),
                  pl.BlockSpec((D_MODEL, bn), lambda j: (0, j)),
                  pl.BlockSpec((1, bn), lambda j: (0, j))],
        out_specs=pl.BlockSpec((8, bn), lambda j: (0, j)),
        out_shape=jax.ShapeDtypeStruct((8, n), F32),
        compiler_params=_cparams(("arbitrary",)),
        name="ada_mod",
    )(c_pad, w_ada, b_ada)


GROUPS_PER_LANE_TILE = LANES // SSM_GROUP_CH


def _to_group_chunks(u, tile_ref, ut_ref):
    n_chunks = u.shape[0] // S5_Q
    for j in range(SSM_CH // LANES):
        tile_ref[j] = u[:, LANES * j:LANES * (j + 1)]
    for j in range(SSM_CH // LANES):
        rows_t = [tile_ref[j, pl.ds(t, n_chunks, stride=S5_Q), :] for t in range(S5_Q)]
        for gg in range(GROUPS_PER_LANE_TILE):
            lo = gg * SSM_GROUP_CH
            row = jnp.concatenate([r[:, lo:lo + SSM_GROUP_CH] for r in rows_t], axis=1)
            ut_ref[j * GROUPS_PER_LANE_TILE + gg] = row.astype(ut_ref.dtype)


def _from_group_chunks(yt_ref, tile_ref):
    n_chunks = yt_ref.shape[1]
    for j in range(SSM_CH // LANES):
        for t in range(S5_Q):
            lo = t * SSM_GROUP_CH
            piece = jnp.concatenate(
                [yt_ref[j * GROUPS_PER_LANE_TILE + gg, :, lo:lo + SSM_GROUP_CH]
                 for gg in range(GROUPS_PER_LANE_TILE)], axis=1)
            tile_ref[j, pl.ds(t, n_chunks, stride=S5_Q), :] = piece
    return jnp.concatenate([tile_ref[j] for j in range(SSM_CH // LANES)], axis=1)


def _inproj_kernel(x_ref, mod_ref, g_ref, w_ref, v_ref, u_ref, ut_ref, tile_ref):
    x = x_ref[...]
    sh = mod_ref[0, 0:1, :]
    sc = mod_ref[0, 1:2, :]
    h = _rms(x, g_ref[...]) * (1.0 + sc) + sh
    z = jnp.dot(h.astype(BF16), w_ref[...], preferred_element_type=F32)
    v_ref[...] = z[:, :CONV_CH] * jax.nn.sigmoid(z[:, CONV_CH:2 * CONV_CH])
    u = z[:, 2 * CONV_CH:]
    u_ref[...] = u
    _to_group_chunks(u, tile_ref, ut_ref)


def _inproj(x2, mod, g_pre, w_in_bf):
    tiles_per_seq = SEQ // TM_IN
    return pl.pallas_call(
        _inproj_kernel,
        grid=(N_TOK // TM_IN,),
        in_specs=[pl.BlockSpec((TM_IN, D_MODEL), lambda i: (i, 0)),
                  pl.BlockSpec((1, 8, D_MODEL), lambda i: (i // tiles_per_seq, 0, 0)),
                  pl.BlockSpec((1, D_MODEL), lambda i: (0, 0)),
                  pl.BlockSpec((D_MODEL, D_IN), lambda i: (0, 0))],
        out_specs=[pl.BlockSpec((TM_IN, CONV_CH), lambda i: (i, 0)),
                   pl.BlockSpec((TM_IN, SSM_CH), lambda i: (i, 0)),
                   pl.BlockSpec((SSM_GROUPS, TM_IN // S5_Q, S5_QH), lambda i: (0, i, 0))],
        out_shape=[jax.ShapeDtypeStruct((N_TOK, CONV_CH), F32),
                   jax.ShapeDtypeStruct((N_TOK, SSM_CH), F32),
                   jax.ShapeDtypeStruct((SSM_GROUPS, S5_CHUNKS, S5_QH), BF16)],
        scratch_shapes=[pltpu.VMEM((SSM_CH // LANES, TM_IN, LANES), F32)],
        compiler_params=_cparams(("parallel",)),
        name="in_proj",
    )(x2, mod, g_pre, w_in_bf)


def _conv_kernel(vc_ref, vp_ref, w_ref, cb_ref, lg_ref, lb_ref, go_ref, o_ref, sh_ref):
    i = pl.program_id(1)
    keep = (i > 0).astype(F32)
    n_ext = TL_CONV + CONV_HALO
    sh_ref[0, 0:CONV_HALO, :] = vp_ref[0] * keep
    sh_ref[0, CONV_HALO:, :] = vc_ref[0]
    for s in range(1, SUBLANES):
        sh_ref[s, 0:n_ext - s, :] = sh_ref[0, s:n_ext, :]
    off = CONV_HALO - (CONV_WIDTH - 1)
    for r in range(TL_CONV // CONV_ROWS):
        acc = None
        for j in range(CONV_WIDTH):
            s = (off + j) % SUBLANES
            al = r * CONV_ROWS + (off + j) - s
            term = w_ref[j:j + 1, :] * sh_ref[s, al:al + CONV_ROWS, :]
            acc = term if acc is None else acc + term
        y = acc + cb_ref[...]
        mu = jnp.mean(y, axis=-1, keepdims=True)
        d = y - mu
        var = jnp.mean(d * d, axis=-1, keepdims=True)
        yn = d * lax.rsqrt(var + NORM_EPS) * lg_ref[...] + lb_ref[...]
        a = yn * jax.nn.sigmoid(yn)
        o_ref[0, r * CONV_ROWS:(r + 1) * CONV_ROWS, :] = _rms(a, go_ref[...]).astype(BF16)


def _conv(v3, conv_w, conv_b, ln_g, ln_b, g_out):
    halo_per_tile = TL_CONV // CONV_HALO
    vec = pl.BlockSpec((1, CONV_CH), lambda b, i: (0, 0))
    return pl.pallas_call(
        _conv_kernel,
        grid=(BATCH, SEQ // TL_CONV),
        in_specs=[pl.BlockSpec((1, TL_CONV, CONV_CH), lambda b, i: (b, i, 0)),
                  pl.BlockSpec((1, CONV_HALO, CONV_CH),
                               lambda b, i: (b, jnp.maximum(i * halo_per_tile - 1, 0), 0)),
                  pl.BlockSpec((CONV_WIDTH + 1, CONV_CH), lambda b, i: (0, 0)),
                  vec, vec, vec, vec],
        out_specs=pl.BlockSpec((1, TL_CONV, CONV_CH), lambda b, i: (b, i, 0)),
        out_shape=jax.ShapeDtypeStruct((BATCH, SEQ, CONV_CH), BF16),
        scratch_shapes=[pltpu.VMEM((SUBLANES, TL_CONV + CONV_HALO, CONV_CH), F32)],
        compiler_params=_cparams(("parallel", "arbitrary")),
        name="conv_module",
    )(v3, v3, conv_w, conv_b, ln_g, ln_b, g_out)


S5_GROUP_ROWS = S5_CHUNKS + 8


S5_POW_ROWS = (S5_Q + 1 + SUBLANES - 1) // SUBLANES * SUBLANES
(S5_BB_RI, S5_BB_NIR, S5_BB_IR, S5_BB_RNI, S5_CC_RI, S5_CC_NIR, S5_N_PARAM) = range(7)


def _s5_kernel(ut_ref, pwr_ref, pwi_ref, par_ref, a_ref, bq_ref, bs_ref, yt_ref, sin_s, sp_s):
    phase = pl.program_id(0)
    g = pl.program_id(1)
    q = S5_Q
    n = 2 * SSM_STATE
    row0 = pl.multiple_of(g * S5_GROUP_ROWS, 8)

    def lam_pow(j):
        return pwr_ref[0, j:j + 1, :], pwi_ref[0, j:j + 1, :]

    @pl.when(phase == 0)
    def _():
        bb_ri, bb_nir = par_ref[0, S5_BB_RI], par_ref[0, S5_BB_NIR]
        bb_ir, bb_rni = par_ref[0, S5_BB_IR], par_ref[0, S5_BB_RNI]
        blk_q, blk_s = [], []
        for t in range(q):
            pr, pi_ = lam_pow(q - 1 - t)
            blk_q.append(pr * bb_ri + pi_ * bb_nir)
            blk_s.append(pr * bb_ir + pi_ * bb_rni)
        wst = jnp.concatenate([jnp.concatenate(blk_q, axis=0), jnp.concatenate(blk_s, axis=0)], axis=1)
        r = jnp.dot(ut_ref[0], wst.astype(BF16), preferred_element_type=F32)
        sin_s[0, pl.ds(row0, S5_CHUNKS), :] = r[:, :n]
        sin_s[1, pl.ds(row0, S5_CHUNKS), :] = r[:, n:]

    @pl.when((phase == 1) & (g == 0))
    def _():
        a = a_ref[...]
        bq = bq_ref[...]
        bs = bs_ref[...]

        def body(c, carry):
            nxt = []
            for b in range(BATCH):
                x, xs = carry[b]
                rows = pl.ds(b * S5_CHUNKS_PER_SEQ + c, SSM_GROUPS, stride=S5_GROUP_ROWS)
                sp_s[rows, :] = x
                nxt.append((a * x + bq * xs + sin_s[0, rows, :], a * xs + bs * x + sin_s[1, rows, :]))
            return tuple(nxt)

        z = jnp.zeros((SSM_GROUPS, n), F32)
        lax.fori_loop(0, S5_CHUNKS_PER_SEQ, body, tuple((z, z) for _ in range(BATCH)))

    @pl.when(phase == 1)
    def _():
        cc_ri, cc_nir = par_ref[0, S5_CC_RI], par_ref[0, S5_CC_NIR]
        cl = []
        for j in range(q + 1):
            pr, pi_ = lam_pow(j)
            cl.append(pr * cc_ri + pi_ * cc_nir)
        cl_lo = jnp.concatenate(cl[:q], axis=0)
        cl_hi = jnp.concatenate(cl[1:], axis=0)
        lane = lax.broadcasted_iota(jnp.int32, (1, n), 1)
        vgt = (cl_hi * jnp.where(lane < SSM_STATE, 1.0, -1.0)).astype(BF16)
        kt = lax.dot_general(par_ref[0, S5_BB_RNI], cl_lo, (((1,), (1,)), ((), ())),
                             preferred_element_type=F32, precision=lax.Precision.HIGHEST)
        padded = jnp.concatenate([jnp.zeros_like(kt), kt], axis=1)
        tg = jnp.concatenate(
            [padded[:, (q - t) * SSM_GROUP_CH:(q - t) * SSM_GROUP_CH + S5_QH] for t in range(q)],
            axis=0).astype(BF16)
        sp = sp_s[pl.ds(row0, S5_CHUNKS), :]
        y = jnp.dot(ut_ref[0], tg, preferred_element_type=F32)
        yt_ref[0] = y + lax.dot_general(sp.astype(BF16), vgt, (((1,), (1,)), ((), ())),
                                        preferred_element_type=F32)


def _s5(ut, pwr, pwi, params, a_cat, b_q, b_s):
    vec = pl.BlockSpec((SSM_GROUPS, 2 * SSM_STATE), lambda p, g: (0, 0))
    powers = pl.BlockSpec((1, S5_POW_ROWS, 2 * SSM_STATE), lambda p, g: (g, 0, 0))
    return pl.pallas_call(
        _s5_kernel,
        grid=(2, SSM_GROUPS),
        in_specs=[pl.BlockSpec((1, S5_CHUNKS, S5_QH), lambda p, g: (g, 0, 0)),
                  powers, powers,
                  pl.BlockSpec((1, S5_N_PARAM, SSM_GROUP_CH, 2 * SSM_STATE), lambda p, g: (g, 0, 0, 0)),
                  vec, vec, vec],
        out_specs=pl.BlockSpec((1, S5_CHUNKS, S5_QH), lambda p, g: (g * p, 0, 0)),
        out_shape=jax.ShapeDtypeStruct((SSM_GROUPS, S5_CHUNKS, S5_QH), F32),
        scratch_shapes=[pltpu.VMEM((2, SSM_GROUPS * S5_GROUP_ROWS, 2 * SSM_STATE), F32),
                        pltpu.VMEM((SSM_GROUPS * S5_GROUP_ROWS, 2 * SSM_STATE), F32)],
        compiler_params=_cparams(("arbitrary", "arbitrary")),
        name="s5_chunked",
    )(ut, pwr, pwi, params, a_cat, b_q, b_s)


def _s5_operators(a_re, a_im, log_dt, b_re, b_im, c_re, c_im):
    q = S5_Q
    dt = jnp.exp(log_dt)[:, None]
    ar, ai = a_re, a_im
    mag = jnp.exp(ar * dt)
    lr = mag * jnp.cos(ai * dt)
    li = mag * jnp.sin(ai * dt)
    den = ar * ar + ai * ai
    nr = lr - 1.0
    kr = (nr * ar + li * ai) / den
    ki = (li * ar - nr * ai) / den
    bbr = kr[..., None] * b_re - ki[..., None] * b_im
    bbi = kr[..., None] * b_im + ki[..., None] * b_re
    j = jnp.arange(q + 1, dtype=F32)[None, :, None]
    pmag = jnp.exp(ar[:, None, :] * dt[:, :, None] * j)
    pang = ai[:, None, :] * dt[:, :, None] * j
    pr = pmag * jnp.cos(pang)
    pi_ = pmag * jnp.sin(pang)
    pad = ((0, 0), (0, S5_POW_ROWS - (q + 1)), (0, 0))
    pwr = jnp.pad(jnp.concatenate([pr, pr], axis=-1), pad)
    pwi = jnp.pad(jnp.concatenate([pi_, pi_], axis=-1), pad)
    br_t = bbr.transpose(0, 2, 1)
    bi_t = bbi.transpose(0, 2, 1)
    cat = lambda a, b: jnp.concatenate([a, b], axis=-1)
    stack = [None] * S5_N_PARAM
    stack[S5_BB_RI] = cat(br_t, bi_t)
    stack[S5_BB_NIR] = cat(-bi_t, br_t)
    stack[S5_BB_IR] = cat(bi_t, br_t)
    stack[S5_BB_RNI] = cat(br_t, -bi_t)
    stack[S5_CC_RI] = cat(c_re, c_im)
    stack[S5_CC_NIR] = cat(-c_im, c_re)
    params = jnp.stack(stack, axis=1)
    aq_r, aq_i = pr[:, q], pi_[:, q]
    a_cat = cat(aq_r, aq_r)
    b_q = cat(-aq_i, aq_i)
    b_s = cat(aq_i, -aq_i)
    return pwr, pwi, params, a_cat, b_q, b_s


def _gelu_tanh(x):
    return 0.5 * x * (1.0 + jnp.tanh(math.sqrt(2.0 / math.pi) * (x + 0.044715 * (x * x * x))))


def _mix_kernel(an_ref, yt_ref, u_ref, x_ref, mod_ref, d_ref, wglu_ref, bglu_ref, gos_ref,
                woa_ref, wob_ref, gpm_ref, gpf_ref, wr_ref, br_ref, tri_ref,
                x1_ref, h2_ref, eidx_ref, rank_ref, gw_ref, cnt_ref, run_ref, tile_ref):
    i = pl.program_id(0)
    tm = TM_MIX

    @pl.when(i == 0)
    def _():
        run_ref[...] = jnp.zeros_like(run_ref)

    gt_m = mod_ref[0, 2:3, :]
    sh_f = mod_ref[0, 3:4, :]
    sc_f = mod_ref[0, 4:5, :]

    yy = _from_group_chunks(yt_ref, tile_ref) + d_ref[...] * u_ref[...]
    g = _gelu_tanh(yy)
    gl = jnp.dot(g.astype(BF16), wglu_ref[...], preferred_element_type=F32) + bglu_ref[...]
    ob = g * jax.nn.sigmoid(gl)
    bn = _rms(ob, gos_ref[...]).astype(BF16)
    o = (jnp.dot(an_ref[...], woa_ref[...], preferred_element_type=F32)
         + jnp.dot(bn, wob_ref[...], preferred_element_type=F32))
    x1 = x_ref[...] + gt_m * _rms(o, gpm_ref[...])
    x1_ref[...] = x1
    h2 = _rms(x1, gpf_ref[...]) * (1.0 + sc_f) + sh_f
    h2_ref[...] = _pack_halves(h2)

    logits = lax.dot_general(wr_ref[...], h2, (((1,), (1,)), ((), ())),
                             preferred_element_type=F32, precision=lax.Precision.HIGHEST)
    scores = jax.nn.sigmoid(logits)
    biased = scores + br_ref[...]
    ng = N_ROUTE_GROUPS
    gsz = N_EXPERTS // ng
    b3 = biased.reshape(ng, gsz, tm)
    s3 = scores.reshape(ng, gsz, tm)
    sub = lax.broadcasted_iota(jnp.int32, (ng, gsz, tm), 1).astype(F32)
    grp = lax.broadcasted_iota(jnp.int32, (ng, gsz, tm), 0).astype(F32)
    eid = grp * gsz + sub
    neg = -jnp.inf
    m1 = jnp.max(b3, axis=1, keepdims=True)
    i1 = jnp.min(jnp.where(b3 == m1, sub, float(gsz)), axis=1, keepdims=True)
    m2 = jnp.max(jnp.where(sub == i1, neg, b3), axis=1, keepdims=True)
    gs = m1 + m2
    gi = lax.broadcasted_iota(jnp.int32, (ng, 1, tm), 0)
    beaten = jnp.zeros((ng, 1, tm), F32)
    for gp in range(ng):
        o_ = gs[gp:gp + 1]
        beats = (o_ > gs) | ((o_ == gs) & (gi > gp))
        beaten = beaten + beats.astype(F32)
    gmask = beaten < float(TOPK_ROUTE_GROUPS)
    masked = jnp.where(gmask, b3, neg)

    sels = []
    picked = jnp.zeros((ng, gsz, tm), F32)
    for k in range(TOP_K):
        m = jnp.max(jnp.max(masked, axis=0, keepdims=True), axis=1, keepdims=True)
        cand = jnp.where(masked == m, eid, float(N_EXPERTS))
        sel = jnp.min(jnp.min(cand, axis=0, keepdims=True), axis=1, keepdims=True)
        oh = eid == sel
        masked = jnp.where(oh, neg, masked)
        picked = jnp.where(oh, 1.0, picked)
        sels.append(sel)

    pm = picked.reshape(N_EXPERTS, tm)
    prefix = jnp.dot(pm.astype(BF16), tri_ref[...], preferred_element_type=F32) + run_ref[:, 0:1]
    p3 = prefix.reshape(ng, gsz, tm)
    run_new = run_ref[...] + jnp.sum(pm, axis=1, keepdims=True)
    run_ref[...] = run_new
    cnt_ref[...] = run_new

    sc_rows = []
    for k in range(TOP_K):
        oh = eid == sels[k]
        sc_k = jnp.sum(jnp.sum(jnp.where(oh, s3, 0.0), axis=0, keepdims=True), axis=1, keepdims=True)
        rk_k = jnp.sum(jnp.sum(jnp.where(oh, p3, 0.0), axis=0, keepdims=True), axis=1, keepdims=True)
        sc_rows.append(sc_k)
        eidx_ref[k:k + 1, :] = sels[k].reshape(1, tm).astype(jnp.int32)
        rank_ref[k:k + 1, :] = rk_k.reshape(1, tm).astype(jnp.int32)
    tot = sc_rows[0]
    for k in range(1, TOP_K):
        tot = tot + sc_rows[k]
    inv = ROUTED_SCALE / (tot + 1e-20)
    for k in range(TOP_K):
        gw_ref[k:k + 1, :] = (sc_rows[k] * inv).reshape(1, tm)


def _mix(a_n, yt, u2, x2, mod, d_skip, wglu_bf, b_glu, g_out_ssm, wo_a, wo_b, g_post_mix,
         g_pre_ffn, w_router_t, b_router_col, tri):
    tm = TM_MIX
    tiles_per_seq = SEQ // tm
    row = lambda n: pl.BlockSpec((1, n), lambda i: (0, 0))
    full = lambda a, b: pl.BlockSpec((a, b), lambda i: (0, 0))
    tok = lambda n: pl.BlockSpec((tm, n), lambda i: (i, 0))
    col = pl.BlockSpec((TOP_K, tm), lambda i: (0, i))
    return pl.pallas_call(
        _mix_kernel,
        grid=(N_TOK // tm,),
        in_specs=[tok(CONV_CH),
                  pl.BlockSpec((SSM_GROUPS, tm // S5_Q, S5_QH), lambda i: (0, i, 0)),
                  tok(SSM_CH), tok(D_MODEL),
                  pl.BlockSpec((1, 8, D_MODEL), lambda i: (i // tiles_per_seq, 0, 0)),
                  row(SSM_CH), full(SSM_CH, SSM_CH), row(SSM_CH), row(SSM_CH),
                  full(CONV_CH, D_MODEL), full(SSM_CH, D_MODEL), row(D_MODEL), row(D_MODEL),
                  full(N_EXPERTS, D_MODEL), full(N_EXPERTS, 1), full(tm, tm)],
        out_specs=[tok(D_MODEL), tok(D_MODEL // 2), col, col, col,
                   pl.BlockSpec((N_EXPERTS, 128), lambda i: (0, 0))],
        out_shape=[jax.ShapeDtypeStruct((N_TOK, D_MODEL), F32),
                   jax.ShapeDtypeStruct((N_TOK, D_MODEL // 2), jnp.uint32),
                   jax.ShapeDtypeStruct((TOP_K, N_TOK), jnp.int32),
                   jax.ShapeDtypeStruct((TOP_K, N_TOK), jnp.int32),
                   jax.ShapeDtypeStruct((TOP_K, N_TOK), F32),
                   jax.ShapeDtypeStruct((N_EXPERTS, 128), F32)],
        scratch_shapes=[pltpu.VMEM((N_EXPERTS, 128), F32),
                        pltpu.VMEM((SSM_CH // LANES, tm, LANES), F32)],
        compiler_params=_cparams(("arbitrary",)),
        name="mix_out_router",
    )(a_n, yt, u2, x2, mod, d_skip, wglu_bf, b_glu, g_out_ssm, wo_a, wo_b, g_post_mix,
      g_pre_ffn, w_router_t, b_router_col, tri)


def _expert_kernel(be_ref, nu_ref, xs_ref, wg_ref, wu_ref, wd_ref, ys_ref, wgu_s, wd_s):
    j = pl.program_id(0)
    prev = be_ref[jnp.maximum(j - 1, 0)]
    fresh = (j == 0) | (be_ref[j] != prev)

    @pl.when(fresh)
    def _():
        wgu_s[:, :D_EXPERT] = wg_ref[0].astype(BF16)
        wgu_s[:, D_EXPERT:] = wu_ref[0].astype(BF16)
        wd_s[...] = wd_ref[0].astype(BF16)

    @pl.when(j < nu_ref[0])
    def _():
        x_lo, x_hi = _unpack_halves(xs_ref[...])
        x = jnp.concatenate([x_lo.astype(BF16), x_hi.astype(BF16)], axis=1)
        h = jnp.dot(x, wgu_s[...], preferred_element_type=F32)
        hg = h[:, :D_EXPERT]
        act = hg * jax.nn.sigmoid(hg) * h[:, D_EXPERT:]
        ys_ref[...] = _pack_halves(jnp.dot(act.astype(BF16), wd_s[...], preferred_element_type=F32))

    @pl.when(j >= nu_ref[0])
    def _():
        ys_ref[...] = jnp.zeros_like(ys_ref)


def _experts(block_e, n_used, xs, we_gate, we_up, we_down):
    grid_spec = pltpu.PrefetchScalarGridSpec(
        num_scalar_prefetch=2,
        grid=(N_BLOCKS,),
        in_specs=[pl.BlockSpec((ROW_BLOCK, D_MODEL // 2), lambda j, be, nu: (jnp.minimum(j, nu[0] - 1), 0)),
                  pl.BlockSpec((1, D_MODEL, D_EXPERT), lambda j, be, nu: (be[j], 0, 0)),
                  pl.BlockSpec((1, D_MODEL, D_EXPERT), lambda j, be, nu: (be[j], 0, 0)),
                  pl.BlockSpec((1, D_EXPERT, D_MODEL), lambda j, be, nu: (be[j], 0, 0))],
        out_specs=pl.BlockSpec((ROW_BLOCK, D_MODEL // 2), lambda j, be, nu: (j, 0)),
        scratch_shapes=[pltpu.VMEM((D_MODEL, 2 * D_EXPERT), BF16),
                        pltpu.VMEM((D_EXPERT, D_MODEL), BF16)],
    )
    return pl.pallas_call(
        _expert_kernel,
        grid_spec=grid_spec,
        out_shape=jax.ShapeDtypeStruct((N_ROWS, D_MODEL // 2), jnp.uint32),
        compiler_params=_cparams(("arbitrary",)),
        name="routed_experts",
    )(block_e, n_used, xs, we_gate, we_up, we_down)


def _final_kernel(h2_ref, yg_ref, gw_ref, x1_ref, mod_ref, wgu_ref, wd_ref, g_ref, o_ref):
    half = D_MODEL // 2
    gt_f = mod_ref[0, 5:6, :]
    x_lo, x_hi = _unpack_halves(h2_ref[...])
    h = (jnp.dot(x_lo.astype(BF16), wgu_ref[:half, :], preferred_element_type=F32)
         + jnp.dot(x_hi.astype(BF16), wgu_ref[half:, :], preferred_element_type=F32))
    hg = h[:, :D_EXPERT]
    act = hg * jax.nn.sigmoid(hg) * h[:, D_EXPERT:]
    shared = jnp.dot(act.astype(BF16), wd_ref[...], preferred_element_type=F32)
    y_lo = shared[:, :half]
    y_hi = shared[:, half:]
    for k in range(TOP_K):
        r_lo, r_hi = _unpack_halves(yg_ref[k])
        w = gw_ref[:, k:k + 1]
        y_lo = y_lo + w * r_lo
        y_hi = y_hi + w * r_hi
    ms = (jnp.sum(y_lo * y_lo, axis=-1, keepdims=True)
          + jnp.sum(y_hi * y_hi, axis=-1, keepdims=True)) * (1.0 / D_MODEL)
    inv = lax.rsqrt(ms + NORM_EPS)
    o_ref[:, :half] = x1_ref[:, :half] + gt_f[:, :half] * (y_lo * inv * g_ref[:, :half])
    o_ref[:, half:] = x1_ref[:, half:] + gt_f[:, half:] * (y_hi * inv * g_ref[:, half:])


def _final(h2p, yg, gw_t, x1, mod, ws_gu, ws_d, g_post_ffn):
    tm = TM_OUT
    tiles_per_seq = SEQ // tm
    tok = pl.BlockSpec((tm, D_MODEL), lambda i: (i, 0))
    return pl.pallas_call(
        _final_kernel,
        grid=(N_TOK // tm,),
        in_specs=[pl.BlockSpec((tm, D_MODEL // 2), lambda i: (i, 0)),
                  pl.BlockSpec((TOP_K, tm, D_MODEL // 2), lambda i: (0, i, 0)),
                  pl.BlockSpec((tm, TOP_K), lambda i: (i, 0)),
                  tok,
                  pl.BlockSpec((1, 8, D_MODEL), lambda i: (i // tiles_per_seq, 0, 0)),
                  pl.BlockSpec((D_MODEL, 2 * D_EXPERT), lambda i: (0, 0)),
                  pl.BlockSpec((D_EXPERT, D_MODEL), lambda i: (0, 0)),
                  pl.BlockSpec((1, D_MODEL), lambda i: (0, 0))],
        out_specs=tok,
        out_shape=jax.ShapeDtypeStruct((N_TOK, D_MODEL), F32),
        compiler_params=_cparams(("parallel",)),
        name="shared_final",
    )(h2p, yg, gw_t, x1, mod, ws_gu, ws_d, g_post_ffn)


def _sc_worker_id():
    return lax.axis_index("s") * SC_CORES + lax.axis_index("c")


def _dispatch_body(h_hbm, dest_hbm, xs_hbm, idx_v, rows_v, sem_l, sem_s):
    n = SC_CHUNKS_PER_WORKER
    c0 = _sc_worker_id() * n

    def load(i, b):
        return pltpu.async_copy(h_hbm.at[pl.ds((c0 + i) * SC_W, SC_W)], rows_v.at[b], sem_l.at[b])

    loads = [None] * n
    scat = [None] * n
    loads[0] = load(0, 0)
    for i in range(n):
        b = i % 2
        pltpu.sync_copy(dest_hbm.at[c0 + i], idx_v.at[b])
        loads[i].wait()
        if i + 1 < n:
            if i >= 1:
                for d in scat[i - 1]:
                    d.wait()
            loads[i + 1] = load(i + 1, 1 - b)
        scat[i] = [pltpu.async_copy(rows_v.at[b], xs_hbm.at[idx_v.at[b].at[k]], sem_s.at[b])
                   for k in range(TOP_K)]
    for i in (n - 2, n - 1):
        for d in scat[i]:
            d.wait()


def _sc_dispatch(h2p, dest3):
    mesh = plsc.VectorSubcoreMesh(core_axis_name="c", subcore_axis_name="s")
    return pl.kernel(
        _dispatch_body, mesh=mesh,
        out_type=jax.ShapeDtypeStruct((N_ROWS, D_MODEL // 2), jnp.uint32),
        scratch_types=[pltpu.VMEM((2, TOP_K, SC_W), jnp.int32),
                       pltpu.VMEM((2, SC_W, D_MODEL // 2), jnp.uint32),
                       pltpu.SemaphoreType.DMA((2,)), pltpu.SemaphoreType.DMA((2,))],
    )(h2p, dest3)


def _combine_body(ys_hbm, dest_hbm, yg_hbm, idx_v, rows_v, sem_g, sem_w):
    c0 = _sc_worker_id() * SC_CHUNKS_PER_WORKER

    @pl.loop(0, SC_CHUNKS_PER_WORKER)
    def _(i):
        c = c0 + i
        pltpu.sync_copy(dest_hbm.at[c], idx_v)
        g = [None] * TOP_K
        w = [None] * TOP_K
        g[0] = pltpu.async_copy(ys_hbm.at[idx_v.at[0]], rows_v.at[0], sem_g.at[0])
        for k in range(TOP_K):
            b = k % 2
            g[k].wait()
            if k + 1 < TOP_K:
                if k >= 1:
                    w[k - 1].wait()
                g[k + 1] = pltpu.async_copy(ys_hbm.at[idx_v.at[k + 1]], rows_v.at[1 - b], sem_g.at[1 - b])
            w[k] = pltpu.async_copy(rows_v.at[b], yg_hbm.at[k].at[pl.ds(c * SC_W, SC_W)], sem_w.at[b])
        w[TOP_K - 2].wait()
        w[TOP_K - 1].wait()


def _sc_combine(ysp, dest3):
    mesh = plsc.VectorSubcoreMesh(core_axis_name="c", subcore_axis_name="s")
    return pl.kernel(
        _combine_body, mesh=mesh,
        out_type=jax.ShapeDtypeStruct((TOP_K, N_TOK, D_MODEL // 2), jnp.uint32),
        scratch_types=[pltpu.VMEM((TOP_K, SC_W), jnp.int32),
                       pltpu.VMEM((2, SC_W, D_MODEL // 2), jnp.uint32),
                       pltpu.SemaphoreType.DMA((2,)), pltpu.SemaphoreType.DMA((2,))],
    )(ysp, dest3)


def kernel(x, c, w_ada, b_ada, g_pre_mix, g_post_mix, w_in, conv_w, conv_b, conv_ln_g, conv_ln_b,
           ssm_a_re, ssm_a_im, ssm_log_dt, ssm_b_re, ssm_b_im, ssm_c_re, ssm_c_im, ssm_d,
           ssm_w_glu, ssm_b_glu, g_out_conv, g_out_ssm, w_out, g_pre_ffn, g_post_ffn,
           w_router, b_router, we_gate, we_up, we_down, ws_gate, ws_up, ws_down):
    l = 0
    x2 = x.reshape(N_TOK, D_MODEL)
    r1 = lambda a: a.reshape(1, -1)

    c_pad = jnp.zeros((8, D_MODEL), F32).at[:BATCH].set(c)
    mod = _ada(c_pad, w_ada[l], r1(b_ada[l]))[:BATCH].reshape(BATCH, 6, D_MODEL)
    mod = jnp.concatenate([mod, jnp.zeros((BATCH, 2, D_MODEL), F32)], axis=1)

    v, u, ut = _inproj(x2, mod, r1(g_pre_mix[l]), w_in[l].astype(BF16))
    cw = jnp.concatenate([conv_w[l].reshape(CONV_WIDTH, CONV_CH), jnp.zeros((1, CONV_CH), F32)], axis=0)
    a_n = _conv(v.reshape(BATCH, SEQ, CONV_CH), cw, r1(conv_b[l]), r1(conv_ln_g[l]),
                r1(conv_ln_b[l]), r1(g_out_conv[l])).reshape(N_TOK, CONV_CH)

    pwr, pwi, s5_params, a_cat, b_q, b_s = _s5_operators(
        ssm_a_re[l], ssm_a_im[l], ssm_log_dt[l], ssm_b_re[l], ssm_b_im[l], ssm_c_re[l], ssm_c_im[l])
    yt = _s5(ut, pwr, pwi, s5_params, a_cat, b_q, b_s)

    tm = TM_MIX
    tri = (jnp.arange(tm)[:, None] < jnp.arange(tm)[None, :]).astype(BF16)
    wo = w_out[l].astype(BF16)
    x1, h2, eidx, rank, gw, cnt = _mix(
        a_n, yt, u, x2, mod, r1(ssm_d[l]), ssm_w_glu[l].astype(BF16), r1(ssm_b_glu[l]),
        r1(g_out_ssm[l]), wo[:CONV_CH], wo[CONV_CH:], r1(g_post_mix[l]), r1(g_pre_ffn[l]),
        w_router[l].T, b_router[l].reshape(N_EXPERTS, 1), tri)

    counts = cnt[:, 0].astype(jnp.int32)
    padded = (counts + ROW_BLOCK - 1) // ROW_BLOCK * ROW_BLOCK
    pends = jnp.cumsum(padded)
    pstart = pends - padded
    e_ids = jnp.arange(N_EXPERTS, dtype=jnp.int32)
    dest = rank + jnp.sum(jnp.where(eidx[..., None] == e_ids, pstart, 0), axis=-1)
    n_used = (pends[-1] // ROW_BLOCK).astype(jnp.int32)
    blk = jnp.minimum(jnp.arange(N_BLOCKS, dtype=jnp.int32), n_used - 1) * ROW_BLOCK
    block_e = jnp.minimum(jnp.sum((pends[None, :] <= blk[:, None]).astype(jnp.int32), axis=1),
                          N_EXPERTS - 1)
    dest3 = dest.reshape(TOP_K, N_TOK // SC_W, SC_W).transpose(1, 0, 2)

    xs = _sc_dispatch(h2, dest3)
    ys = _experts(block_e, n_used.reshape(1), xs, we_gate[l], we_up[l], we_down[l])
    yg = _sc_combine(ys, dest3)

    ws_gu = jnp.concatenate([ws_gate[l], ws_up[l]], axis=1).astype(BF16)
    out = _final(h2, yg, gw.T, x1, mod, ws_gu, ws_down[l].astype(BF16), r1(g_post_ffn[l]))
    return out.reshape(BATCH, SEQ, D_MODEL)
```

```python
import functools
import math

import jax
import jax.numpy as jnp
from jax import lax
from jax.experimental import pallas as pl
from jax.experimental.pallas import tpu as pltpu
from jax.experimental.pallas import tpu_sc as plsc

F32 = jnp.float32
BF16 = jnp.bfloat16

D_MODEL = 1024
BATCH = 2
SEQ = 8192
N_TOK = BATCH * SEQ
CONV_CH = 512
CONV_WIDTH = 31
SSM_CH = 512
SSM_GROUP_CH = 16
SSM_GROUPS = 32
SSM_STATE = 64
D_IN = 2 * CONV_CH + SSM_CH
N_EXPERTS = 64
TOP_K = 8
N_ROUTE_GROUPS = 8
TOPK_ROUTE_GROUPS = 4
D_EXPERT = 256
ROUTED_SCALE = 2.5
NORM_EPS = 1e-6

SUBLANES = 8
LANES = 128

TM_IN = 512
TL_CONV = 512
CONV_HALO = 32
CONV_ROWS = 64
S5_Q = 32
S5_QH = S5_Q * SSM_GROUP_CH
S5_CHUNKS = N_TOK // S5_Q
S5_CHUNKS_PER_SEQ = SEQ // S5_Q
TM_MIX = 512
ROW_BLOCK = 512
N_BLOCKS = N_TOK * TOP_K // ROW_BLOCK + N_EXPERTS
N_ROWS = N_BLOCKS * ROW_BLOCK
TM_OUT = 512
SC_CORES = 2
SC_SUBCORES = 16
SC_WORKERS = SC_CORES * SC_SUBCORES
SC_W = 64
SC_CHUNKS_PER_WORKER = N_TOK // (SC_WORKERS * SC_W)
VMEM_LIMIT = 48 * 1024 * 1024


def _cparams(sem):
    return pltpu.CompilerParams(dimension_semantics=sem, vmem_limit_bytes=VMEM_LIMIT)


def _pack_halves(x):
    n = x.shape[-1] // 2
    lo = lax.bitcast_convert_type(x[:, :n].astype(BF16).astype(F32), jnp.uint32)
    hi = lax.bitcast_convert_type(x[:, n:].astype(BF16).astype(F32), jnp.uint32)
    return hi | (lo >> 16)


def _unpack_halves(p):
    lo = lax.bitcast_convert_type(p << 16, F32)
    hi = lax.bitcast_convert_type(p & jnp.uint32(0xFFFF0000), F32)
    return lo, hi


def _rms(x, g):
    return x * lax.rsqrt(jnp.mean(x * x, axis=-1, keepdims=True) + NORM_EPS) * g


def _ada_kernel(c_ref, w_ref, b_ref, o_ref):
    c = c_ref[...]
    a = c * jax.nn.sigmoid(c)
    o_ref[...] = jnp.dot(a, w_ref[...], preferred_element_type=F32,
                         precision=lax.Precision.HIGHEST) + b_ref[...]


def _ada(c_pad, w_ada, b_ada):
    n = w_ada.shape[1]
    bn = 1536
    return pl.pallas_call(
        _ada_kernel,
        grid=(n // bn,),
        in_specs=[pl.BlockSpec((8, D_MODEL), lambda j: (0, 0)),
                  pl.BlockSpec((D_MODEL, bn), lambda j: (0, j)),
                  pl.BlockSpec((1, bn), lambda j: (0, j))],
        out_specs=pl.BlockSpec((8, bn), lambda j: (0, j)),
        out_shape=jax.ShapeDtypeStruct((8, n), F32),
        compiler_params=_cparams(("arbitrary",)),
        name="ada_mod",
    )(c_pad, w_ada, b_ada)


GROUPS_PER_LANE_TILE = LANES // SSM_GROUP_CH


def _to_group_chunks(u, tile_ref, ut_ref):
    n_chunks = u.shape[0] // S5_Q
    for j in range(SSM_CH // LANES):
        tile_ref[j] = u[:, LANES * j:LANES * (j + 1)]
    for j in range(SSM_CH // LANES):
        rows_t = [tile_ref[j, pl.ds(t, n_chunks, stride=S5_Q), :] for t in range(S5_Q)]
        for gg in range(GROUPS_PER_LANE_TILE):
            lo = gg * SSM_GROUP_CH
            row = jnp.concatenate([r[:, lo:lo + SSM_GROUP_CH] for r in rows_t], axis=1)
            ut_ref[j * GROUPS_PER_LANE_TILE + gg] = row.astype(ut_ref.dtype)


def _from_group_chunks(yt_ref, tile_ref):
    n_chunks = yt_ref.shape[1]
    for j in range(SSM_CH // LANES):
        for t in range(S5_Q):
            lo = t * SSM_GROUP_CH
            piece = jnp.concatenate(
                [yt_ref[j * GROUPS_PER_LANE_TILE + gg, :, lo:lo + SSM_GROUP_CH]
                 for gg in range(GROUPS_PER_LANE_TILE)], axis=1)
            tile_ref[j, pl.ds(t, n_chunks, stride=S5_Q), :] = piece
    return jnp.concatenate([tile_ref[j] for j in range(SSM_CH // LANES)], axis=1)


def _inproj_kernel(x_ref, mod_ref, g_ref, w_ref, v_ref, u_ref, ut_ref, tile_ref):
    x = x_ref[...]
    sh = mod_ref[0, 0:1, :]
    sc = mod_ref[0, 1:2, :]
    h = _rms(x, g_ref[...]) * (1.0 + sc) + sh
    z = jnp.dot(h.astype(BF16), w_ref[...], preferred_element_type=F32)
    v_ref[...] = z[:, :CONV_CH] * jax.nn.sigmoid(z[:, CONV_CH:2 * CONV_CH])
    u = z[:, 2 * CONV_CH:]
    u_ref[...] = u
    _to_group_chunks(u, tile_ref, ut_ref)


def _inproj(x2, mod, g_pre, w_in_bf):
    tiles_per_seq = SEQ // TM_IN
    return pl.pallas_call(
        _inproj_kernel,
        grid=(N_TOK // TM_IN,),
        in_specs=[pl.BlockSpec((TM_IN, D_MODEL), lambda i: (i, 0)),
                  pl.BlockSpec((1, 8, D_MODEL), lambda i: (i // tiles_per_seq, 0, 0)),
                  pl.BlockSpec((1, D_MODEL), lambda i: (0, 0)),
                  pl.BlockSpec((D_MODEL, D_IN), lambda i: (0, 0))],
        out_specs=[pl.BlockSpec((TM_IN, CONV_CH), lambda i: (i, 0)),
                   pl.BlockSpec((TM_IN, SSM_CH), lambda i: (i, 0)),
                   pl.BlockSpec((SSM_GROUPS, TM_IN // S5_Q, S5_QH), lambda i: (0, i, 0))],
        out_shape=[jax.ShapeDtypeStruct((N_TOK, CONV_CH), F32),
                   jax.ShapeDtypeStruct((N_TOK, SSM_CH), F32),
                   jax.ShapeDtypeStruct((SSM_GROUPS, S5_CHUNKS, S5_QH), BF16)],
        scratch_shapes=[pltpu.VMEM((SSM_CH // LANES, TM_IN, LANES), F32)],
        compiler_params=_cparams(("parallel",)),
        name="in_proj",
    )(x2, mod, g_pre, w_in_bf)


def _conv_kernel(vc_ref, vp_ref, w_ref, cb_ref, lg_ref, lb_ref, go_ref, o_ref, sh_ref):
    i = pl.program_id(1)
    keep = (i > 0).astype(F32)
    n_ext = TL_CONV + CONV_HALO
    sh_ref[0, 0:CONV_HALO, :] = vp_ref[0] * keep
    sh_ref[0, CONV_HALO:, :] = vc_ref[0]
    for s in range(1, SUBLANES):
        sh_ref[s, 0:n_ext - s, :] = sh_ref[0, s:n_ext, :]
    off = CONV_HALO - (CONV_WIDTH - 1)
    for r in range(TL_CONV // CONV_ROWS):
        acc = None
        for j in range(CONV_WIDTH):
            s = (off + j) % SUBLANES
            al = r * CONV_ROWS + (off + j) - s
            term = w_ref[j:j + 1, :] * sh_ref[s, al:al + CONV_ROWS, :]
            acc = term if acc is None else acc + term
        y = acc + cb_ref[...]
        mu = jnp.mean(y, axis=-1, keepdims=True)
        d = y - mu
        var = jnp.mean(d * d, axis=-1, keepdims=True)
        yn = d * lax.rsqrt(var + NORM_EPS) * lg_ref[...] + lb_ref[...]
        a = yn * jax.nn.sigmoid(yn)
        o_ref[0, r * CONV_ROWS:(r + 1) * CONV_ROWS, :] = _rms(a, go_ref[...]).astype(BF16)


def _conv(v3, conv_w, conv_b, ln_g, ln_b, g_out):
    halo_per_tile = TL_CONV // CONV_HALO
    vec = pl.BlockSpec((1, CONV_CH), lambda b, i: (0, 0))
    return pl.pallas_call(
        _conv_kernel,
        grid=(BATCH, SEQ // TL_CONV),
        in_specs=[pl.BlockSpec((1, TL_CONV, CONV_CH), lambda b, i: (b, i, 0)),
                  pl.BlockSpec((1, CONV_HALO, CONV_CH),
                               lambda b, i: (b, jnp.maximum(i * halo_per_tile - 1, 0), 0)),
                  pl.BlockSpec((CONV_WIDTH + 1, CONV_CH), lambda b, i: (0, 0)),
                  vec, vec, vec, vec],
        out_specs=pl.BlockSpec((1, TL_CONV, CONV_CH), lambda b, i: (b, i, 0)),
        out_shape=jax.ShapeDtypeStruct((BATCH, SEQ, CONV_CH), BF16),
        scratch_shapes=[pltpu.VMEM((SUBLANES, TL_CONV + CONV_HALO, CONV_CH), F32)],
        compiler_params=_cparams(("parallel", "arbitrary")),
        name="conv_module",
    )(v3, v3, conv_w, conv_b, ln_g, ln_b, g_out)


S5_GROUP_ROWS = S5_CHUNKS + 8


S5_POW_ROWS = (S5_Q + 1 + SUBLANES - 1) // SUBLANES * SUBLANES
(S5_BB_RI, S5_BB_NIR, S5_BB_IR, S5_BB_RNI, S5_CC_RI, S5_CC_NIR, S5_N_PARAM) = range(7)


def _s5_kernel(ut_ref, pwr_ref, pwi_ref, par_ref, a_ref, bq_ref, bs_ref, yt_ref, sin_s, sp_s):
    phase = pl.program_id(0)
    g = pl.program_id(1)
    q = S5_Q
    n = 2 * SSM_STATE
    row0 = pl.multiple_of(g * S5_GROUP_ROWS, 8)

    def lam_pow(j):
        return pwr_ref[0, j:j + 1, :], pwi_ref[0, j:j + 1, :]

    @pl.when(phase == 0)
    def _():
        bb_ri, bb_nir = par_ref[0, S5_BB_RI], par_ref[0, S5_BB_NIR]
        bb_ir, bb_rni = par_ref[0, S5_BB_IR], par_ref[0, S5_BB_RNI]
        blk_q, blk_s = [], []
        for t in range(q):
            pr, pi_ = lam_pow(q - 1 - t)
            blk_q.append(pr * bb_ri + pi_ * bb_nir)
            blk_s.append(pr * bb_ir + pi_ * bb_rni)
        wst = jnp.concatenate([jnp.concatenate(blk_q, axis=0), jnp.concatenate(blk_s, axis=0)], axis=1)
        r = jnp.dot(ut_ref[0], wst.astype(BF16), preferred_element_type=F32)
        sin_s[0, pl.ds(row0, S5_CHUNKS), :] = r[:, :n]
        sin_s[1, pl.ds(row0, S5_CHUNKS), :] = r[:, n:]

    @pl.when((phase == 1) & (g == 0))
    def _():
        a = a_ref[...]
        bq = bq_ref[...]
        bs = bs_ref[...]

        def body(c, carry):
            nxt = []
            for b in range(BATCH):
                x, xs = carry[b]
                rows = pl.ds(b * S5_CHUNKS_PER_SEQ + c, SSM_GROUPS, stride=S5_GROUP_ROWS)
                sp_s[rows, :] = x
                nxt.append((a * x + bq * xs + sin_s[0, rows, :], a * xs + bs * x + sin_s[1, rows, :]))
            return tuple(nxt)

        z = jnp.zeros((SSM_GROUPS, n), F32)
        lax.fori_loop(0, S5_CHUNKS_PER_SEQ, body, tuple((z, z) for _ in range(BATCH)))

    @pl.when(phase == 1)
    def _():
        cc_ri, cc_nir = par_ref[0, S5_CC_RI], par_ref[0, S5_CC_NIR]
        cl = []
        for j in range(q + 1):
            pr, pi_ = lam_pow(j)
            cl.append(pr * cc_ri + pi_ * cc_nir)
        cl_lo = jnp.concatenate(cl[:q], axis=0)
        cl_hi = jnp.concatenate(cl[1:], axis=0)
        lane = lax.broadcasted_iota(jnp.int32, (1, n), 1)
        vgt = (cl_hi * jnp.where(lane < SSM_STATE, 1.0, -1.0)).astype(BF16)
        kt = lax.dot_general(par_ref[0, S5_BB_RNI], cl_lo, (((1,), (1,)), ((), ())),
                             preferred_element_type=F32, precision=lax.Precision.HIGHEST)
        padded = jnp.concatenate([jnp.zeros_like(kt), kt], axis=1)
        tg = jnp.concatenate(
            [padded[:, (q - t) * SSM_GROUP_CH:(q - t) * SSM_GROUP_CH + S5_QH] for t in range(q)],
            axis=0).astype(BF16)
        sp = sp_s[pl.ds(row0, S5_CHUNKS), :]
        y = jnp.dot(ut_ref[0], tg, preferred_element_type=F32)
        yt_ref[0] = y + lax.dot_general(sp.astype(BF16), vgt, (((1,), (1,)), ((), ())),
                                        preferred_element_type=F32)


def _s5(ut, pwr, pwi, params, a_cat, b_q, b_s):
    vec = pl.BlockSpec((SSM_GROUPS, 2 * SSM_STATE), lambda p, g: (0, 0))
    powers = pl.BlockSpec((1, S5_POW_ROWS, 2 * SSM_STATE), lambda p, g: (g, 0, 0))
    return pl.pallas_call(
        _s5_kernel,
        grid=(2, SSM_GROUPS),
        in_specs=[pl.BlockSpec((1, S5_CHUNKS, S5_QH), lambda p, g: (g, 0, 0)),
                  powers, powers,
                  pl.BlockSpec((1, S5_N_PARAM, SSM_GROUP_CH, 2 * SSM_STATE), lambda p, g: (g, 0, 0, 0)),
                  vec, vec, vec],
        out_specs=pl.BlockSpec((1, S5_CHUNKS, S5_QH), lambda p, g: (g * p, 0, 0)),
        out_shape=jax.ShapeDtypeStruct((SSM_GROUPS, S5_CHUNKS, S5_QH), F32),
        scratch_shapes=[pltpu.VMEM((2, SSM_GROUPS * S5_GROUP_ROWS, 2 * SSM_STATE), F32),
                        pltpu.VMEM((SSM_GROUPS * S5_GROUP_ROWS, 2 * SSM_STATE), F32)],
        compiler_params=_cparams(("arbitrary", "arbitrary")),
        name="s5_chunked",
    )(ut, pwr, pwi, params, a_cat, b_q, b_s)


def _s5_operators(a_re, a_im, log_dt, b_re, b_im, c_re, c_im):
    q = S5_Q
    dt = jnp.exp(log_dt)[:, None]
    ar, ai = a_re, a_im
    mag = jnp.exp(ar * dt)
    lr = mag * jnp.cos(ai * dt)
    li = mag * jnp.sin(ai * dt)
    den = ar * ar + ai * ai
    nr = lr - 1.0
    kr = (nr * ar + li * ai) / den
    ki = (li * ar - nr * ai) / den
    bbr = kr[..., None] * b_re - ki[..., None] * b_im
    bbi = kr[..., None] * b_im + ki[..., None] * b_re
    j = jnp.arange(q + 1, dtype=F32)[None, :, None]
    pmag = jnp.exp(ar[:, None, :] * dt[:, :, None] * j)
    pang = ai[:, None, :] * dt[:, :, None] * j
    pr = pmag * jnp.cos(pang)
    pi_ = pmag * jnp.sin(pang)
    pad = ((0, 0), (0, S5_POW_ROWS - (q + 1)), (0, 0))
    pwr = jnp.pad(jnp.concatenate([pr, pr], axis=-1), pad)
    pwi = jnp.pad(jnp.concatenate([pi_, pi_], axis=-1), pad)
    br_t = bbr.transpose(0, 2, 1)
    bi_t = bbi.transpose(0, 2, 1)
    cat = lambda a, b: jnp.concatenate([a, b], axis=-1)
    stack = [None] * S5_N_PARAM
    stack[S5_BB_RI] = cat(br_t, bi_t)
    stack[S5_BB_NIR] = cat(-bi_t, br_t)
    stack[S5_BB_IR] = cat(bi_t, br_t)
    stack[S5_BB_RNI] = cat(br_t, -bi_t)
    stack[S5_CC_RI] = cat(c_re, c_im)
    stack[S5_CC_NIR] = cat(-c_im, c_re)
    params = jnp.stack(stack, axis=1)
    aq_r, aq_i = pr[:, q], pi_[:, q]
    a_cat = cat(aq_r, aq_r)
    b_q = cat(-aq_i, aq_i)
    b_s = cat(aq_i, -aq_i)
    return pwr, pwi, params, a_cat, b_q, b_s


def _gelu_tanh(x):
    return 0.5 * x * (1.0 + jnp.tanh(math.sqrt(2.0 / math.pi) * (x + 0.044715 * (x * x * x))))


def _mix_kernel(an_ref, yt_ref, u_ref, x_ref, mod_ref, d_ref, wglu_ref, bglu_ref, gos_ref,
                woa_ref, wob_ref, gpm_ref, gpf_ref, wr_ref, br_ref, tri_ref,
                x1_ref, h2_ref, eidx_ref, rank_ref, gw_ref, cnt_ref, run_ref, tile_ref):
    i = pl.program_id(0)
    tm = TM_MIX

    @pl.when(i == 0)
    def _():
        run_ref[...] = jnp.zeros_like(run_ref)

    gt_m = mod_ref[0, 2:3, :]
    sh_f = mod_ref[0, 3:4, :]
    sc_f = mod_ref[0, 4:5, :]

    yy = _from_group_chunks(yt_ref, tile_ref) + d_ref[...] * u_ref[...]
    g = _gelu_tanh(yy)
    gl = jnp.dot(g.astype(BF16), wglu_ref[...], preferred_element_type=F32) + bglu_ref[...]
    ob = g * jax.nn.sigmoid(gl)
    bn = _rms(ob, gos_ref[...]).astype(BF16)
    o = (jnp.dot(an_ref[...], woa_ref[...], preferred_element_type=F32)
         + jnp.dot(bn, wob_ref[...], preferred_element_type=F32))
    x1 = x_ref[...] + gt_m * _rms(o, gpm_ref[...])
    x1_ref[...] = x1
    h2 = _rms(x1, gpf_ref[...]) * (1.0 + sc_f) + sh_f
    h2_ref[...] = _pack_halves(h2)

    logits = lax.dot_general(wr_ref[...], h2, (((1,), (1,)), ((), ())),
                             preferred_element_type=F32, precision=lax.Precision.HIGHEST)
    scores = jax.nn.sigmoid(logits)
    biased = scores + br_ref[...]
    ng = N_ROUTE_GROUPS
    gsz = N_EXPERTS // ng
    b3 = biased.reshape(ng, gsz, tm)
    s3 = scores.reshape(ng, gsz, tm)
    sub = lax.broadcasted_iota(jnp.int32, (ng, gsz, tm), 1).astype(F32)
    grp = lax.broadcasted_iota(jnp.int32, (ng, gsz, tm), 0).astype(F32)
    eid = grp * gsz + sub
    neg = -jnp.inf
    m1 = jnp.max(b3, axis=1, keepdims=True)
    i1 = jnp.min(jnp.where(b3 == m1, sub, float(gsz)), axis=1, keepdims=True)
    m2 = jnp.max(jnp.where(sub == i1, neg, b3), axis=1, keepdims=True)
    gs = m1 + m2
    gi = lax.broadcasted_iota(jnp.int32, (ng, 1, tm), 0)
    beaten = jnp.zeros((ng, 1, tm), F32)
    for gp in range(ng):
        o_ = gs[gp:gp + 1]
        beats = (o_ > gs) | ((o_ == gs) & (gi > gp))
        beaten = beaten + beats.astype(F32)
    gmask = beaten < float(TOPK_ROUTE_GROUPS)
    masked = jnp.where(gmask, b3, neg)

    sels = []
    picked = jnp.zeros((ng, gsz, tm), F32)
    for k in range(TOP_K):
        m = jnp.max(jnp.max(masked, axis=0, keepdims=True), axis=1, keepdims=True)
        cand = jnp.where(masked == m, eid, float(N_EXPERTS))
        sel = jnp.min(jnp.min(cand, axis=0, keepdims=True), axis=1, keepdims=True)
        oh = eid == sel
        masked = jnp.where(oh, neg, masked)
        picked = jnp.where(oh, 1.0, picked)
        sels.append(sel)

    pm = picked.reshape(N_EXPERTS, tm)
    prefix = jnp.dot(pm.astype(BF16), tri_ref[...], preferred_element_type=F32) + run_ref[:, 0:1]
    p3 = prefix.reshape(ng, gsz, tm)
    run_new = run_ref[...] + jnp.sum(pm, axis=1, keepdims=True)
    run_ref[...] = run_new
    cnt_ref[...] = run_new

    sc_rows = []
    for k in range(TOP_K):
        oh = eid == sels[k]
        sc_k = jnp.sum(jnp.sum(jnp.where(oh, s3, 0.0), axis=0, keepdims=True), axis=1, keepdims=True)
        rk_k = jnp.sum(jnp.sum(jnp.where(oh, p3, 0.0), axis=0, keepdims=True), axis=1, keepdims=True)
        sc_rows.append(sc_k)
        eidx_ref[k:k + 1, :] = sels[k].reshape(1, tm).astype(jnp.int32)
        rank_ref[k:k + 1, :] = rk_k.reshape(1, tm).astype(jnp.int32)
    tot = sc_rows[0]
    for k in range(1, TOP_K):
        tot = tot + sc_rows[k]
    inv = ROUTED_SCALE / (tot + 1e-20)
    for k in range(TOP_K):
        gw_ref[k:k + 1, :] = (sc_rows[k] * inv).reshape(1, tm)


def _mix(a_n, yt, u2, x2, mod, d_skip, wglu_bf, b_glu, g_out_ssm, wo_a, wo_b, g_post_mix,
         g_pre_ffn, w_router_t, b_router_col, tri):
    tm = TM_MIX
    tiles_per_seq = SEQ // tm
    row = lambda n: pl.BlockSpec((1, n), lambda i: (0, 0))
    full = lambda a, b: pl.BlockSpec((a, b), lambda i: (0, 0))
    tok = lambda n: pl.BlockSpec((tm, n), lambda i: (i, 0))
    col = pl.BlockSpec((TOP_K, tm), lambda i: (0, i))
    return pl.pallas_call(
        _mix_kernel,
        grid=(N_TOK // tm,),
        in_specs=[tok(CONV_CH),
                  pl.BlockSpec((SSM_GROUPS, tm // S5_Q, S5_QH), lambda i: (0, i, 0)),
                  tok(SSM_CH), tok(D_MODEL),
                  pl.BlockSpec((1, 8, D_MODEL), lambda i: (i // tiles_per_seq, 0, 0)),
                  row(SSM_CH), full(SSM_CH, SSM_CH), row(SSM_CH), row(SSM_CH),
                  full(CONV_CH, D_MODEL), full(SSM_CH, D_MODEL), row(D_MODEL), row(D_MODEL),
                  full(N_EXPERTS, D_MODEL), full(N_EXPERTS, 1), full(tm, tm)],
        out_specs=[tok(D_MODEL), tok(D_MODEL // 2), col, col, col,
                   pl.BlockSpec((N_EXPERTS, 128), lambda i: (0, 0))],
        out_shape=[jax.ShapeDtypeStruct((N_TOK, D_MODEL), F32),
                   jax.ShapeDtypeStruct((N_TOK, D_MODEL // 2), jnp.uint32),
                   jax.ShapeDtypeStruct((TOP_K, N_TOK), jnp.int32),
                   jax.ShapeDtypeStruct((TOP_K, N_TOK), jnp.int32),
                   jax.ShapeDtypeStruct((TOP_K, N_TOK), F32),
                   jax.ShapeDtypeStruct((N_EXPERTS, 128), F32)],
        scratch_shapes=[pltpu.VMEM((N_EXPERTS, 128), F32),
                        pltpu.VMEM((SSM_CH // LANES, tm, LANES), F32)],
        compiler_params=_cparams(("arbitrary",)),
        name="mix_out_router",
    )(a_n, yt, u2, x2, mod, d_skip, wglu_bf, b_glu, g_out_ssm, wo_a, wo_b, g_post_mix,
      g_pre_ffn, w_router_t, b_router_col, tri)


def _expert_kernel(blk0_ref, nblk_ref, xs_hbm, wg_ref, wu_ref, wd_ref, ys_hbm,
                   xbuf, ybuf, wgu_s, wd_s, sem_in, sem_out):
    e = pl.program_id(0)
    n = nblk_ref[e]
    b0 = blk0_ref[e]

    def rows(b):
        return pl.ds(pl.multiple_of((b0 + b) * ROW_BLOCK, ROW_BLOCK), ROW_BLOCK)

    def in_copy(b, slot):
        return pltpu.make_async_copy(xs_hbm.at[rows(b)], xbuf.at[slot], sem_in.at[slot])

    def out_copy(b, slot):
        return pltpu.make_async_copy(ybuf.at[slot], ys_hbm.at[rows(b)], sem_out.at[slot])

    @pl.when(n > 0)
    def _():
        in_copy(0, 0).start()

    wgu_s[:, :D_EXPERT] = wg_ref[0].astype(BF16)
    wgu_s[:, D_EXPERT:] = wu_ref[0].astype(BF16)
    wd_s[...] = wd_ref[0].astype(BF16)

    def body(b, carry):
        slot = b % 2
        in_copy(b, slot).wait()

        @pl.when(b + 1 < n)
        def _():
            in_copy(b + 1, 1 - slot).start()

        @pl.when(b >= 2)
        def _():
            out_copy(b - 2, slot).wait()

        x_lo, x_hi = _unpack_halves(xbuf[slot])
        x = jnp.concatenate([x_lo.astype(BF16), x_hi.astype(BF16)], axis=1)
        h = jnp.dot(x, wgu_s[...], preferred_element_type=F32)
        hg = h[:, :D_EXPERT]
        act = hg * jax.nn.sigmoid(hg) * h[:, D_EXPERT:]
        ybuf[slot] = _pack_halves(jnp.dot(act.astype(BF16), wd_s[...], preferred_element_type=F32))
        out_copy(b, slot).start()
        return carry

    lax.fori_loop(0, n, body, 0)

    @pl.when(n >= 2)
    def _():
        out_copy(n - 2, n % 2).wait()

    @pl.when(n >= 1)
    def _():
        out_copy(n - 1, (n - 1) % 2).wait()


def _experts(blk0, nblk, xs, we_gate, we_up, we_down):
    any_spec = pl.BlockSpec(memory_space=pl.ANY)
    grid_spec = pltpu.PrefetchScalarGridSpec(
        num_scalar_prefetch=2,
        grid=(N_EXPERTS,),
        in_specs=[any_spec,
                  pl.BlockSpec((1, D_MODEL, D_EXPERT), lambda e, b0, nb: (e, 0, 0)),
                  pl.BlockSpec((1, D_MODEL, D_EXPERT), lambda e, b0, nb: (e, 0, 0)),
                  pl.BlockSpec((1, D_EXPERT, D_MODEL), lambda e, b0, nb: (e, 0, 0))],
        out_specs=any_spec,
        scratch_shapes=[pltpu.VMEM((2, ROW_BLOCK, D_MODEL // 2), jnp.uint32),
                        pltpu.VMEM((2, ROW_BLOCK, D_MODEL // 2), jnp.uint32),
                        pltpu.VMEM((D_MODEL, 2 * D_EXPERT), BF16),
                        pltpu.VMEM((D_EXPERT, D_MODEL), BF16),
                        pltpu.SemaphoreType.DMA((2,)),
                        pltpu.SemaphoreType.DMA((2,))],
    )
    return pl.pallas_call(
        _expert_kernel,
        grid_spec=grid_spec,
        out_shape=jax.ShapeDtypeStruct((N_ROWS, D_MODEL // 2), jnp.uint32),
        compiler_params=_cparams(("arbitrary",)),
        name="routed_experts",
    )(blk0, nblk, xs, we_gate, we_up, we_down)


def _final_kernel(h2_ref, yg_ref, gw_ref, x1_ref, mod_ref, wgu_ref, wd_ref, g_ref, o_ref):
    half = D_MODEL // 2
    gt_f = mod_ref[0, 5:6, :]
    x_lo, x_hi = _unpack_halves(h2_ref[...])
    h = (jnp.dot(x_lo.astype(BF16), wgu_ref[:half, :], preferred_element_type=F32)
         + jnp.dot(x_hi.astype(BF16), wgu_ref[half:, :], preferred_element_type=F32))
    hg = h[:, :D_EXPERT]
    act = hg * jax.nn.sigmoid(hg) * h[:, D_EXPERT:]
    shared = jnp.dot(act.astype(BF16), wd_ref[...], preferred_element_type=F32)
    y_lo = shared[:, :half]
    y_hi = shared[:, half:]
    for k in range(TOP_K):
        r_lo, r_hi = _unpack_halves(yg_ref[k])
        w = gw_ref[:, k:k + 1]
        y_lo = y_lo + w * r_lo
        y_hi = y_hi + w * r_hi
    ms = (jnp.sum(y_lo * y_lo, axis=-1, keepdims=True)
          + jnp.sum(y_hi * y_hi, axis=-1, keepdims=True)) * (1.0 / D_MODEL)
    inv = lax.rsqrt(ms + NORM_EPS)
    o_ref[:, :half] = x1_ref[:, :half] + gt_f[:, :half] * (y_lo * inv * g_ref[:, :half])
    o_ref[:, half:] = x1_ref[:, half:] + gt_f[:, half:] * (y_hi * inv * g_ref[:, half:])


def _final(h2p, yg, gw_t, x1, mod, ws_gu, ws_d, g_post_ffn):
    tm = TM_OUT
    tiles_per_seq = SEQ // tm
    tok = pl.BlockSpec((tm, D_MODEL), lambda i: (i, 0))
    return pl.pallas_call(
        _final_kernel,
        grid=(N_TOK // tm,),
        in_specs=[pl.BlockSpec((tm, D_MODEL // 2), lambda i: (i, 0)),
                  pl.BlockSpec((TOP_K, tm, D_MODEL // 2), lambda i: (0, i, 0)),
                  pl.BlockSpec((tm, TOP_K), lambda i: (i, 0)),
                  tok,
                  pl.BlockSpec((1, 8, D_MODEL), lambda i: (i // tiles_per_seq, 0, 0)),
                  pl.BlockSpec((D_MODEL, 2 * D_EXPERT), lambda i: (0, 0)),
                  pl.BlockSpec((D_EXPERT, D_MODEL), lambda i: (0, 0)),
                  pl.BlockSpec((1, D_MODEL), lambda i: (0, 0))],
        out_specs=tok,
        out_shape=jax.ShapeDtypeStruct((N_TOK, D_MODEL), F32),
        compiler_params=_cparams(("parallel",)),
        name="shared_final",
    )(h2p, yg, gw_t, x1, mod, ws_gu, ws_d, g_post_ffn)


def _sc_worker_id():
    return lax.axis_index("s") * SC_CORES + lax.axis_index("c")


def _dispatch_body(h_hbm, dest_hbm, xs_hbm, idx_v, rows_v, sem_l, sem_s):
    n = SC_CHUNKS_PER_WORKER
    c0 = _sc_worker_id() * n

    def load(i, b):
        return pltpu.async_copy(h_hbm.at[pl.ds((c0 + i) * SC_W, SC_W)], rows_v.at[b], sem_l.at[b])

    loads = [None] * n
    scat = [None] * n
    loads[0] = load(0, 0)
    for i in range(n):
        b = i % 2
        pltpu.sync_copy(dest_hbm.at[c0 + i], idx_v.at[b])
        loads[i].wait()
        if i + 1 < n:
            if i >= 1:
                for d in scat[i - 1]:
                    d.wait()
            loads[i + 1] = load(i + 1, 1 - b)
        scat[i] = [pltpu.async_copy(rows_v.at[b], xs_hbm.at[idx_v.at[b].at[k]], sem_s.at[b])
                   for k in range(TOP_K)]
    for i in (n - 2, n - 1):
        for d in scat[i]:
            d.wait()


def _sc_dispatch(h2p, dest3):
    mesh = plsc.VectorSubcoreMesh(core_axis_name="c", subcore_axis_name="s")
    return pl.kernel(
        _dispatch_body, mesh=mesh,
        out_type=jax.ShapeDtypeStruct((N_ROWS, D_MODEL // 2), jnp.uint32),
        scratch_types=[pltpu.VMEM((2, TOP_K, SC_W), jnp.int32),
                       pltpu.VMEM((2, SC_W, D_MODEL // 2), jnp.uint32),
                       pltpu.SemaphoreType.DMA((2,)), pltpu.SemaphoreType.DMA((2,))],
    )(h2p, dest3)


def _combine_body(ys_hbm, dest_hbm, yg_hbm, idx_v, rows_v, sem_g, sem_w):
    c0 = _sc_worker_id() * SC_CHUNKS_PER_WORKER

    @pl.loop(0, SC_CHUNKS_PER_WORKER)
    def _(i):
        c = c0 + i
        pltpu.sync_copy(dest_hbm.at[c], idx_v)
        g = [None] * TOP_K
        w = [None] * TOP_K
        g[0] = pltpu.async_copy(ys_hbm.at[idx_v.at[0]], rows_v.at[0], sem_g.at[0])
        for k in range(TOP_K):
            b = k % 2
            g[k].wait()
            if k + 1 < TOP_K:
                if k >= 1:
                    w[k - 1].wait()
                g[k + 1] = pltpu.async_copy(ys_hbm.at[idx_v.at[k + 1]], rows_v.at[1 - b], sem_g.at[1 - b])
            w[k] = pltpu.async_copy(rows_v.at[b], yg_hbm.at[k].at[pl.ds(c * SC_W, SC_W)], sem_w.at[b])
        w[TOP_K - 2].wait()
        w[TOP_K - 1].wait()


def _sc_combine(ysp, dest3):
    mesh = plsc.VectorSubcoreMesh(core_axis_name="c", subcore_axis_name="s")
    return pl.kernel(
        _combine_body, mesh=mesh,
        out_type=jax.ShapeDtypeStruct((TOP_K, N_TOK, D_MODEL // 2), jnp.uint32),
        scratch_types=[pltpu.VMEM((TOP_K, SC_W), jnp.int32),
                       pltpu.VMEM((2, SC_W, D_MODEL // 2), jnp.uint32),
                       pltpu.SemaphoreType.DMA((2,)), pltpu.SemaphoreType.DMA((2,))],
    )(ysp, dest3)


def kernel(x, c, w_ada, b_ada, g_pre_mix, g_post_mix, w_in, conv_w, conv_b, conv_ln_g, conv_ln_b,
           ssm_a_re, ssm_a_im, ssm_log_dt, ssm_b_re, ssm_b_im, ssm_c_re, ssm_c_im, ssm_d,
           ssm_w_glu, ssm_b_glu, g_out_conv, g_out_ssm, w_out, g_pre_ffn, g_post_ffn,
           w_router, b_router, we_gate, we_up, we_down, ws_gate, ws_up, ws_down):
    l = 0
    x2 = x.reshape(N_TOK, D_MODEL)
    r1 = lambda a: a.reshape(1, -1)

    c_pad = jnp.zeros((8, D_MODEL), F32).at[:BATCH].set(c)
    mod = _ada(c_pad, w_ada[l], r1(b_ada[l]))[:BATCH].reshape(BATCH, 6, D_MODEL)
    mod = jnp.concatenate([mod, jnp.zeros((BATCH, 2, D_MODEL), F32)], axis=1)

    v, u, ut = _inproj(x2, mod, r1(g_pre_mix[l]), w_in[l].astype(BF16))
    cw = jnp.concatenate([conv_w[l].reshape(CONV_WIDTH, CONV_CH), jnp.zeros((1, CONV_CH), F32)], axis=0)
    a_n = _conv(v.reshape(BATCH, SEQ, CONV_CH), cw, r1(conv_b[l]), r1(conv_ln_g[l]),
                r1(conv_ln_b[l]), r1(g_out_conv[l])).reshape(N_TOK, CONV_CH)

    pwr, pwi, s5_params, a_cat, b_q, b_s = _s5_operators(
        ssm_a_re[l], ssm_a_im[l], ssm_log_dt[l], ssm_b_re[l], ssm_b_im[l], ssm_c_re[l], ssm_c_im[l])
    yt = _s5(ut, pwr, pwi, s5_params, a_cat, b_q, b_s)

    tm = TM_MIX
    tri = (jnp.arange(tm)[:, None] < jnp.arange(tm)[None, :]).astype(BF16)
    wo = w_out[l].astype(BF16)
    x1, h2, eidx, rank, gw, cnt = _mix(
        a_n, yt, u, x2, mod, r1(ssm_d[l]), ssm_w_glu[l].astype(BF16), r1(ssm_b_glu[l]),
        r1(g_out_ssm[l]), wo[:CONV_CH], wo[CONV_CH:], r1(g_post_mix[l]), r1(g_pre_ffn[l]),
        w_router[l].T, b_router[l].reshape(N_EXPERTS, 1), tri)

    counts = cnt[:, 0].astype(jnp.int32)
    padded = (counts + ROW_BLOCK - 1) // ROW_BLOCK * ROW_BLOCK
    pends = jnp.cumsum(padded)
    pstart = pends - padded
    e_ids = jnp.arange(N_EXPERTS, dtype=jnp.int32)
    dest = rank + jnp.sum(jnp.where(eidx[..., None] == e_ids, pstart, 0), axis=-1)
    dest3 = dest.reshape(TOP_K, N_TOK // SC_W, SC_W).transpose(1, 0, 2)

    xs = _sc_dispatch(h2, dest3)
    ys = _experts(pstart // ROW_BLOCK, padded // ROW_BLOCK, xs, we_gate[l], we_up[l], we_down[l])
    yg = _sc_combine(ys, dest3)

    ws_gu = jnp.concatenate([ws_gate[l], ws_up[l]], axis=1).astype(BF16)
    out = _final(h2, yg, gw.T, x1, mod, ws_gu, ws_down[l].astype(BF16), r1(g_post_ffn[l]))
    return out.reshape(BATCH, SEQ, D_MODEL)
```

```python
import functools
import math

import jax
import jax.numpy as jnp
from jax import lax
from jax.experimental import pallas as pl
from jax.experimental.pallas import tpu as pltpu
from jax.experimental.pallas import tpu_sc as plsc

F32 = jnp.float32
BF16 = jnp.bfloat16

D_MODEL = 1024
BATCH = 2
SEQ = 8192
N_TOK = BATCH * SEQ
CONV_CH = 512
CONV_WIDTH = 31
SSM_CH = 512
SSM_GROUP_CH = 16
SSM_GROUPS = 32
SSM_STATE = 64
D_IN = 2 * CONV_CH + SSM_CH
N_EXPERTS = 64
TOP_K = 8
N_ROUTE_GROUPS = 8
TOPK_ROUTE_GROUPS = 4
D_EXPERT = 256
ROUTED_SCALE = 2.5
NORM_EPS = 1e-6

SUBLANES = 8
LANES = 128

TM_IN = 512
TL_CONV = 512
CONV_HALO = 32
CONV_ROWS = 64
S5_Q = 32
S5_QH = S5_Q * SSM_GROUP_CH
S5_CHUNKS = N_TOK // S5_Q
S5_CHUNKS_PER_SEQ = SEQ // S5_Q
TM_MIX = 512
ROW_BLOCK = 512
EXPERT_IN_SLOTS = 3
N_BLOCKS = N_TOK * TOP_K // ROW_BLOCK + N_EXPERTS
N_ROWS = N_BLOCKS * ROW_BLOCK
TM_OUT = 512
SC_CORES = 2
SC_SUBCORES = 16
SC_WORKERS = SC_CORES * SC_SUBCORES
SC_W = 64
SC_CHUNKS_PER_WORKER = N_TOK // (SC_WORKERS * SC_W)
VMEM_LIMIT = 48 * 1024 * 1024


def _cparams(sem):
    return pltpu.CompilerParams(dimension_semantics=sem, vmem_limit_bytes=VMEM_LIMIT)


def _pack_halves(x):
    n = x.shape[-1] // 2
    lo = lax.bitcast_convert_type(x[:, :n].astype(BF16).astype(F32), jnp.uint32)
    hi = lax.bitcast_convert_type(x[:, n:].astype(BF16).astype(F32), jnp.uint32)
    return hi | (lo >> 16)


def _unpack_halves(p):
    lo = lax.bitcast_convert_type(p << 16, F32)
    hi = lax.bitcast_convert_type(p & jnp.uint32(0xFFFF0000), F32)
    return lo, hi


def _rms(x, g):
    return x * lax.rsqrt(jnp.mean(x * x, axis=-1, keepdims=True) + NORM_EPS) * g


def _ada_kernel(c_ref, w_ref, b_ref, o_ref):
    c = c_ref[...]
    a = c * jax.nn.sigmoid(c)
    o_ref[...] = jnp.dot(a, w_ref[...], preferred_element_type=F32,
                         precision=lax.Precision.HIGHEST) + b_ref[...]


def _ada(c_pad, w_ada, b_ada):
    n = w_ada.shape[1]
    bn = 1536
    return pl.pallas_call(
        _ada_kernel,
        grid=(n // bn,),
        in_specs=[pl.BlockSpec((8, D_MODEL), lambda j: (0, 0)),
                  pl.BlockSpec((D_MODEL, bn), lambda j: (0, j)),
                  pl.BlockSpec((1, bn), lambda j: (0, j))],
        out_specs=pl.BlockSpec((8, bn), lambda j: (0, j)),
        out_shape=jax.ShapeDtypeStruct((8, n), F32),
        compiler_params=_cparams(("arbitrary",)),
        name="ada_mod",
    )(c_pad, w_ada, b_ada)


GROUPS_PER_LANE_TILE = LANES // SSM_GROUP_CH


def _to_group_chunks(u, tile_ref, ut_ref):
    n_chunks = u.shape[0] // S5_Q
    for j in range(SSM_CH // LANES):
        tile_ref[j] = u[:, LANES * j:LANES * (j + 1)]
    for j in range(SSM_CH // LANES):
        rows_t = [tile_ref[j, pl.ds(t, n_chunks, stride=S5_Q), :] for t in range(S5_Q)]
        for gg in range(GROUPS_PER_LANE_TILE):
            lo = gg * SSM_GROUP_CH
            row = jnp.concatenate([r[:, lo:lo + SSM_GROUP_CH] for r in rows_t], axis=1)
            ut_ref[j * GROUPS_PER_LANE_TILE + gg] = row.astype(ut_ref.dtype)


def _from_group_chunks(yt_ref, tile_ref):
    n_chunks = yt_ref.shape[1]
    for j in range(SSM_CH // LANES):
        for t in range(S5_Q):
            lo = t * SSM_GROUP_CH
            piece = jnp.concatenate(
                [yt_ref[j * GROUPS_PER_LANE_TILE + gg, :, lo:lo + SSM_GROUP_CH]
                 for gg in range(GROUPS_PER_LANE_TILE)], axis=1)
            tile_ref[j, pl.ds(t, n_chunks, stride=S5_Q), :] = piece
    return jnp.concatenate([tile_ref[j] for j in range(SSM_CH // LANES)], axis=1)


def _inproj_kernel(x_ref, mod_ref, g_ref, w_ref, v_ref, u_ref, ut_ref, tile_ref):
    x = x_ref[...]
    sh = mod_ref[0, 0:1, :]
    sc = mod_ref[0, 1:2, :]
    h = _rms(x, g_ref[...]) * (1.0 + sc) + sh
    z = jnp.dot(h.astype(BF16), w_ref[...], preferred_element_type=F32)
    v_ref[...] = z[:, :CONV_CH] * jax.nn.sigmoid(z[:, CONV_CH:2 * CONV_CH])
    u = z[:, 2 * CONV_CH:]
    u_ref[...] = u
    _to_group_chunks(u, tile_ref, ut_ref)


def _inproj(x2, mod, g_pre, w_in_bf):
    tiles_per_seq = SEQ // TM_IN
    return pl.pallas_call(
        _inproj_kernel,
        grid=(N_TOK // TM_IN,),
        in_specs=[pl.BlockSpec((TM_IN, D_MODEL), lambda i: (i, 0)),
                  pl.BlockSpec((1, 8, D_MODEL), lambda i: (i // tiles_per_seq, 0, 0)),
                  pl.BlockSpec((1, D_MODEL), lambda i: (0, 0)),
                  pl.BlockSpec((D_MODEL, D_IN), lambda i: (0, 0))],
        out_specs=[pl.BlockSpec((TM_IN, CONV_CH), lambda i: (i, 0)),
                   pl.BlockSpec((TM_IN, SSM_CH), lambda i: (i, 0)),
                   pl.BlockSpec((SSM_GROUPS, TM_IN // S5_Q, S5_QH), lambda i: (0, i, 0))],
        out_shape=[jax.ShapeDtypeStruct((N_TOK, CONV_CH), F32),
                   jax.ShapeDtypeStruct((N_TOK, SSM_CH), F32),
                   jax.ShapeDtypeStruct((SSM_GROUPS, S5_CHUNKS, S5_QH), BF16)],
        scratch_shapes=[pltpu.VMEM((SSM_CH // LANES, TM_IN, LANES), F32)],
        compiler_params=_cparams(("parallel",)),
        name="in_proj",
    )(x2, mod, g_pre, w_in_bf)


def _conv_kernel(vc_ref, vp_ref, w_ref, cb_ref, lg_ref, lb_ref, go_ref, o_ref, sh_ref):
    i = pl.program_id(1)
    keep = (i > 0).astype(F32)
    n_ext = TL_CONV + CONV_HALO
    sh_ref[0, 0:CONV_HALO, :] = vp_ref[0] * keep
    sh_ref[0, CONV_HALO:, :] = vc_ref[0]
    for s in range(1, SUBLANES):
        sh_ref[s, 0:n_ext - s, :] = sh_ref[0, s:n_ext, :]
    off = CONV_HALO - (CONV_WIDTH - 1)
    for r in range(TL_CONV // CONV_ROWS):
        acc = None
        for j in range(CONV_WIDTH):
            s = (off + j) % SUBLANES
            al = r * CONV_ROWS + (off + j) - s
            term = w_ref[j:j + 1, :] * sh_ref[s, al:al + CONV_ROWS, :]
            acc = term if acc is None else acc + term
        y = acc + cb_ref[...]
        mu = jnp.mean(y, axis=-1, keepdims=True)
        d = y - mu
        var = jnp.mean(d * d, axis=-1, keepdims=True)
        yn = d * lax.rsqrt(var + NORM_EPS) * lg_ref[...] + lb_ref[...]
        a = yn * jax.nn.sigmoid(yn)
        o_ref[0, r * CONV_ROWS:(r + 1) * CONV_ROWS, :] = _rms(a, go_ref[...]).astype(BF16)


def _conv(v3, conv_w, conv_b, ln_g, ln_b, g_out):
    halo_per_tile = TL_CONV // CONV_HALO
    vec = pl.BlockSpec((1, CONV_CH), lambda b, i: (0, 0))
    return pl.pallas_call(
        _conv_kernel,
        grid=(BATCH, SEQ // TL_CONV),
        in_specs=[pl.BlockSpec((1, TL_CONV, CONV_CH), lambda b, i: (b, i, 0)),
                  pl.BlockSpec((1, CONV_HALO, CONV_CH),
                               lambda b, i: (b, jnp.maximum(i * halo_per_tile - 1, 0), 0)),
                  pl.BlockSpec((CONV_WIDTH + 1, CONV_CH), lambda b, i: (0, 0)),
                  vec, vec, vec, vec],
        out_specs=pl.BlockSpec((1, TL_CONV, CONV_CH), lambda b, i: (b, i, 0)),
        out_shape=jax.ShapeDtypeStruct((BATCH, SEQ, CONV_CH), BF16),
        scratch_shapes=[pltpu.VMEM((SUBLANES, TL_CONV + CONV_HALO, CONV_CH), F32)],
        compiler_params=_cparams(("parallel", "arbitrary")),
        name="conv_module",
    )(v3, v3, conv_w, conv_b, ln_g, ln_b, g_out)


S5_GROUP_ROWS = S5_CHUNKS + 8


S5_POW_ROWS = (S5_Q + 1 + SUBLANES - 1) // SUBLANES * SUBLANES
(S5_BB_RI, S5_BB_NIR, S5_BB_IR, S5_BB_RNI, S5_CC_RI, S5_CC_NIR, S5_N_PARAM) = range(7)


def _s5_kernel(ut_ref, pwr_ref, pwi_ref, par_ref, a_ref, bq_ref, bs_ref, yt_ref, sin_s, sp_s):
    phase = pl.program_id(0)
    g = pl.program_id(1)
    q = S5_Q
    n = 2 * SSM_STATE
    row0 = pl.multiple_of(g * S5_GROUP_ROWS, 8)

    def lam_pow(j):
        return pwr_ref[0, j:j + 1, :], pwi_ref[0, j:j + 1, :]

    @pl.when(phase == 0)
    def _():
        bb_ri, bb_nir = par_ref[0, S5_BB_RI], par_ref[0, S5_BB_NIR]
        bb_ir, bb_rni = par_ref[0, S5_BB_IR], par_ref[0, S5_BB_RNI]
        blk_q, blk_s = [], []
        for t in range(q):
            pr, pi_ = lam_pow(q - 1 - t)
            blk_q.append(pr * bb_ri + pi_ * bb_nir)
            blk_s.append(pr * bb_ir + pi_ * bb_rni)
        wst = jnp.concatenate([jnp.concatenate(blk_q, axis=0), jnp.concatenate(blk_s, axis=0)], axis=1)
        r = jnp.dot(ut_ref[0], wst.astype(BF16), preferred_element_type=F32)
        sin_s[0, pl.ds(row0, S5_CHUNKS), :] = r[:, :n]
        sin_s[1, pl.ds(row0, S5_CHUNKS), :] = r[:, n:]

    @pl.when((phase == 1) & (g == 0))
    def _():
        a = a_ref[...]
        bq = bq_ref[...]
        bs = bs_ref[...]

        def body(c, carry):
            nxt = []
            for b in range(BATCH):
                x, xs = carry[b]
                rows = pl.ds(b * S5_CHUNKS_PER_SEQ + c, SSM_GROUPS, stride=S5_GROUP_ROWS)
                sp_s[rows, :] = x
                nxt.append((a * x + bq * xs + sin_s[0, rows, :], a * xs + bs * x + sin_s[1, rows, :]))
            return tuple(nxt)

        z = jnp.zeros((SSM_GROUPS, n), F32)
        lax.fori_loop(0, S5_CHUNKS_PER_SEQ, body, tuple((z, z) for _ in range(BATCH)))

    @pl.when(phase == 1)
    def _():
        cc_ri, cc_nir = par_ref[0, S5_CC_RI], par_ref[0, S5_CC_NIR]
        cl = []
        for j in range(q + 1):
            pr, pi_ = lam_pow(j)
            cl.append(pr * cc_ri + pi_ * cc_nir)
        cl_lo = jnp.concatenate(cl[:q], axis=0)
        cl_hi = jnp.concatenate(cl[1:], axis=0)
        lane = lax.broadcasted_iota(jnp.int32, (1, n), 1)
        vgt = (cl_hi * jnp.where(lane < SSM_STATE, 1.0, -1.0)).astype(BF16)
        kt = lax.dot_general(par_ref[0, S5_BB_RNI], cl_lo, (((1,), (1,)), ((), ())),
                             preferred_element_type=F32, precision=lax.Precision.HIGHEST)
        padded = jnp.concatenate([jnp.zeros_like(kt), kt], axis=1)
        tg = jnp.concatenate(
            [padded[:, (q - t) * SSM_GROUP_CH:(q - t) * SSM_GROUP_CH + S5_QH] for t in range(q)],
            axis=0).astype(BF16)
        sp = sp_s[pl.ds(row0, S5_CHUNKS), :]
        y = jnp.dot(ut_ref[0], tg, preferred_element_type=F32)
        yt_ref[0] = y + lax.dot_general(sp.astype(BF16), vgt, (((1,), (1,)), ((), ())),
                                        preferred_element_type=F32)


def _s5(ut, pwr, pwi, params, a_cat, b_q, b_s):
    vec = pl.BlockSpec((SSM_GROUPS, 2 * SSM_STATE), lambda p, g: (0, 0))
    powers = pl.BlockSpec((1, S5_POW_ROWS, 2 * SSM_STATE), lambda p, g: (g, 0, 0))
    return pl.pallas_call(
        _s5_kernel,
        grid=(2, SSM_GROUPS),
        in_specs=[pl.BlockSpec((1, S5_CHUNKS, S5_QH), lambda p, g: (g, 0, 0)),
                  powers, powers,
                  pl.BlockSpec((1, S5_N_PARAM, SSM_GROUP_CH, 2 * SSM_STATE), lambda p, g: (g, 0, 0, 0)),
                  vec, vec, vec],
        out_specs=pl.BlockSpec((1, S5_CHUNKS, S5_QH), lambda p, g: (g * p, 0, 0)),
        out_shape=jax.ShapeDtypeStruct((SSM_GROUPS, S5_CHUNKS, S5_QH), F32),
        scratch_shapes=[pltpu.VMEM((2, SSM_GROUPS * S5_GROUP_ROWS, 2 * SSM_STATE), F32),
                        pltpu.VMEM((SSM_GROUPS * S5_GROUP_ROWS, 2 * SSM_STATE), F32)],
        compiler_params=_cparams(("arbitrary", "arbitrary")),
        name="s5_chunked",
    )(ut, pwr, pwi, params, a_cat, b_q, b_s)


def _s5_operators(a_re, a_im, log_dt, b_re, b_im, c_re, c_im):
    q = S5_Q
    dt = jnp.exp(log_dt)[:, None]
    ar, ai = a_re, a_im
    mag = jnp.exp(ar * dt)
    lr = mag * jnp.cos(ai * dt)
    li = mag * jnp.sin(ai * dt)
    den = ar * ar + ai * ai
    nr = lr - 1.0
    kr = (nr * ar + li * ai) / den
    ki = (li * ar - nr * ai) / den
    bbr = kr[..., None] * b_re - ki[..., None] * b_im
    bbi = kr[..., None] * b_im + ki[..., None] * b_re
    j = jnp.arange(q + 1, dtype=F32)[None, :, None]
    pmag = jnp.exp(ar[:, None, :] * dt[:, :, None] * j)
    pang = ai[:, None, :] * dt[:, :, None] * j
    pr = pmag * jnp.cos(pang)
    pi_ = pmag * jnp.sin(pang)
    pad = ((0, 0), (0, S5_POW_ROWS - (q + 1)), (0, 0))
    pwr = jnp.pad(jnp.concatenate([pr, pr], axis=-1), pad)
    pwi = jnp.pad(jnp.concatenate([pi_, pi_], axis=-1), pad)
    br_t = bbr.transpose(0, 2, 1)
    bi_t = bbi.transpose(0, 2, 1)
    cat = lambda a, b: jnp.concatenate([a, b], axis=-1)
    stack = [None] * S5_N_PARAM
    stack[S5_BB_RI] = cat(br_t, bi_t)
    stack[S5_BB_NIR] = cat(-bi_t, br_t)
    stack[S5_BB_IR] = cat(bi_t, br_t)
    stack[S5_BB_RNI] = cat(br_t, -bi_t)
    stack[S5_CC_RI] = cat(c_re, c_im)
    stack[S5_CC_NIR] = cat(-c_im, c_re)
    params = jnp.stack(stack, axis=1)
    aq_r, aq_i = pr[:, q], pi_[:, q]
    a_cat = cat(aq_r, aq_r)
    b_q = cat(-aq_i, aq_i)
    b_s = cat(aq_i, -aq_i)
    return pwr, pwi, params, a_cat, b_q, b_s


def _gelu_tanh(x):
    return 0.5 * x * (1.0 + jnp.tanh(math.sqrt(2.0 / math.pi) * (x + 0.044715 * (x * x * x))))


def _mix_kernel(an_ref, yt_ref, u_ref, x_ref, mod_ref, d_ref, wglu_ref, bglu_ref, gos_ref,
                woa_ref, wob_ref, gpm_ref, gpf_ref, wr_ref, br_ref, tri_ref,
                x1_ref, h2_ref, eidx_ref, rank_ref, gw_ref, cnt_ref, run_ref, tile_ref):
    i = pl.program_id(0)
    tm = TM_MIX

    @pl.when(i == 0)
    def _():
        run_ref[...] = jnp.zeros_like(run_ref)

    gt_m = mod_ref[0, 2:3, :]
    sh_f = mod_ref[0, 3:4, :]
    sc_f = mod_ref[0, 4:5, :]

    yy = _from_group_chunks(yt_ref, tile_ref) + d_ref[...] * u_ref[...]
    g = _gelu_tanh(yy)
    gl = jnp.dot(g.astype(BF16), wglu_ref[...], preferred_element_type=F32) + bglu_ref[...]
    ob = g * jax.nn.sigmoid(gl)
    bn = _rms(ob, gos_ref[...]).astype(BF16)
    o = (jnp.dot(an_ref[...], woa_ref[...], preferred_element_type=F32)
         + jnp.dot(bn, wob_ref[...], preferred_element_type=F32))
    x1 = x_ref[...] + gt_m * _rms(o, gpm_ref[...])
    x1_ref[...] = x1
    h2 = _rms(x1, gpf_ref[...]) * (1.0 + sc_f) + sh_f
    h2_ref[...] = _pack_halves(h2)

    logits = lax.dot_general(wr_ref[...], h2, (((1,), (1,)), ((), ())),
                             preferred_element_type=F32, precision=lax.Precision.HIGHEST)
    scores = jax.nn.sigmoid(logits)
    biased = scores + br_ref[...]
    ng = N_ROUTE_GROUPS
    gsz = N_EXPERTS // ng
    b3 = biased.reshape(ng, gsz, tm)
    s3 = scores.reshape(ng, gsz, tm)
    sub = lax.broadcasted_iota(jnp.int32, (ng, gsz, tm), 1).astype(F32)
    grp = lax.broadcasted_iota(jnp.int32, (ng, gsz, tm), 0).astype(F32)
    eid = grp * gsz + sub
    neg = -jnp.inf
    m1 = jnp.max(b3, axis=1, keepdims=True)
    i1 = jnp.min(jnp.where(b3 == m1, sub, float(gsz)), axis=1, keepdims=True)
    m2 = jnp.max(jnp.where(sub == i1, neg, b3), axis=1, keepdims=True)
    gs = m1 + m2
    gi = lax.broadcasted_iota(jnp.int32, (ng, 1, tm), 0)
    beaten = jnp.zeros((ng, 1, tm), F32)
    for gp in range(ng):
        o_ = gs[gp:gp + 1]
        beats = (o_ > gs) | ((o_ == gs) & (gi > gp))
        beaten = beaten + beats.astype(F32)
    gmask = beaten < float(TOPK_ROUTE_GROUPS)
    masked = jnp.where(gmask, b3, neg)

    sels = []
    picked = jnp.zeros((ng, gsz, tm), F32)
    for k in range(TOP_K):
        m = jnp.max(jnp.max(masked, axis=0, keepdims=True), axis=1, keepdims=True)
        cand = jnp.where(masked == m, eid, float(N_EXPERTS))
        sel = jnp.min(jnp.min(cand, axis=0, keepdims=True), axis=1, keepdims=True)
        oh = eid == sel
        masked = jnp.where(oh, neg, masked)
        picked = jnp.where(oh, 1.0, picked)
        sels.append(sel)

    pm = picked.reshape(N_EXPERTS, tm)
    prefix = jnp.dot(pm.astype(BF16), tri_ref[...], preferred_element_type=F32) + run_ref[:, 0:1]
    p3 = prefix.reshape(ng, gsz, tm)
    run_new = run_ref[...] + jnp.sum(pm, axis=1, keepdims=True)
    run_ref[...] = run_new
    cnt_ref[...] = run_new

    sc_rows = []
    for k in range(TOP_K):
        oh = eid == sels[k]
        sc_k = jnp.sum(jnp.sum(jnp.where(oh, s3, 0.0), axis=0, keepdims=True), axis=1, keepdims=True)
        rk_k = jnp.sum(jnp.sum(jnp.where(oh, p3, 0.0), axis=0, keepdims=True), axis=1, keepdims=True)
        sc_rows.append(sc_k)
        eidx_ref[k:k + 1, :] = sels[k].reshape(1, tm).astype(jnp.int32)
        rank_ref[k:k + 1, :] = rk_k.reshape(1, tm).astype(jnp.int32)
    tot = sc_rows[0]
    for k in range(1, TOP_K):
        tot = tot + sc_rows[k]
    inv = ROUTED_SCALE / (tot + 1e-20)
    for k in range(TOP_K):
        gw_ref[k:k + 1, :] = (sc_rows[k] * inv).reshape(1, tm)


def _mix(a_n, yt, u2, x2, mod, d_skip, wglu_bf, b_glu, g_out_ssm, wo_a, wo_b, g_post_mix,
         g_pre_ffn, w_router_t, b_router_col, tri):
    tm = TM_MIX
    tiles_per_seq = SEQ // tm
    row = lambda n: pl.BlockSpec((1, n), lambda i: (0, 0))
    full = lambda a, b: pl.BlockSpec((a, b), lambda i: (0, 0))
    tok = lambda n: pl.BlockSpec((tm, n), lambda i: (i, 0))
    col = pl.BlockSpec((TOP_K, tm), lambda i: (0, i))
    return pl.pallas_call(
        _mix_kernel,
        grid=(N_TOK // tm,),
        in_specs=[tok(CONV_CH),
                  pl.BlockSpec((SSM_GROUPS, tm // S5_Q, S5_QH), lambda i: (0, i, 0)),
                  tok(SSM_CH), tok(D_MODEL),
                  pl.BlockSpec((1, 8, D_MODEL), lambda i: (i // tiles_per_seq, 0, 0)),
                  row(SSM_CH), full(SSM_CH, SSM_CH), row(SSM_CH), row(SSM_CH),
                  full(CONV_CH, D_MODEL), full(SSM_CH, D_MODEL), row(D_MODEL), row(D_MODEL),
                  full(N_EXPERTS, D_MODEL), full(N_EXPERTS, 1), full(tm, tm)],
        out_specs=[tok(D_MODEL), tok(D_MODEL // 2), col, col, col,
                   pl.BlockSpec((N_EXPERTS, 128), lambda i: (0, 0))],
        out_shape=[jax.ShapeDtypeStruct((N_TOK, D_MODEL), F32),
                   jax.ShapeDtypeStruct((N_TOK, D_MODEL // 2), jnp.uint32),
                   jax.ShapeDtypeStruct((TOP_K, N_TOK), jnp.int32),
                   jax.ShapeDtypeStruct((TOP_K, N_TOK), jnp.int32),
                   jax.ShapeDtypeStruct((TOP_K, N_TOK), F32),
                   jax.ShapeDtypeStruct((N_EXPERTS, 128), F32)],
        scratch_shapes=[pltpu.VMEM((N_EXPERTS, 128), F32),
                        pltpu.VMEM((SSM_CH // LANES, tm, LANES), F32)],
        compiler_params=_cparams(("arbitrary",)),
        name="mix_out_router",
    )(a_n, yt, u2, x2, mod, d_skip, wglu_bf, b_glu, g_out_ssm, wo_a, wo_b, g_post_mix,
      g_pre_ffn, w_router_t, b_router_col, tri)


def _expert_kernel(blk0_ref, nblk_ref, xs_hbm, wg_ref, wu_ref, wd_ref, ys_hbm,
                   xbuf, ybuf, wgu_s, wd_s, sem_in, sem_out):
    e = pl.program_id(0)
    n = nblk_ref[e]
    b0 = blk0_ref[e]

    def rows(b):
        return pl.ds(pl.multiple_of((b0 + b) * ROW_BLOCK, ROW_BLOCK), ROW_BLOCK)

    def in_copy(b, slot):
        return pltpu.make_async_copy(xs_hbm.at[rows(b)], xbuf.at[slot], sem_in.at[slot])

    def out_copy(b, slot):
        return pltpu.make_async_copy(ybuf.at[slot], ys_hbm.at[rows(b)], sem_out.at[slot])

    for b in range(EXPERT_IN_SLOTS - 1):
        @pl.when(b < n)
        def _():
            in_copy(b, b).start()

    wgu_s[:, :D_EXPERT] = wg_ref[0].astype(BF16)
    wgu_s[:, D_EXPERT:] = wu_ref[0].astype(BF16)
    wd_s[...] = wd_ref[0].astype(BF16)

    def body(b, carry):
        slot = b % 2
        in_slot = b % EXPERT_IN_SLOTS
        in_copy(b, in_slot).wait()
        ahead = b + EXPERT_IN_SLOTS - 1

        @pl.when(ahead < n)
        def _():
            in_copy(ahead, ahead % EXPERT_IN_SLOTS).start()

        @pl.when(b >= 2)
        def _():
            out_copy(b - 2, slot).wait()

        x_lo, x_hi = _unpack_halves(xbuf[in_slot])
        x = jnp.concatenate([x_lo.astype(BF16), x_hi.astype(BF16)], axis=1)
        h = jnp.dot(x, wgu_s[...], preferred_element_type=F32)
        hg = h[:, :D_EXPERT]
        act = hg * jax.nn.sigmoid(hg) * h[:, D_EXPERT:]
        ybuf[slot] = _pack_halves(jnp.dot(act.astype(BF16), wd_s[...], preferred_element_type=F32))
        out_copy(b, slot).start()
        return carry

    lax.fori_loop(0, n, body, 0)

    @pl.when(n >= 2)
    def _():
        out_copy(n - 2, n % 2).wait()

    @pl.when(n >= 1)
    def _():
        out_copy(n - 1, (n - 1) % 2).wait()


def _experts(blk0, nblk, xs, we_gate, we_up, we_down):
    any_spec = pl.BlockSpec(memory_space=pl.ANY)
    grid_spec = pltpu.PrefetchScalarGridSpec(
        num_scalar_prefetch=2,
        grid=(N_EXPERTS,),
        in_specs=[any_spec,
                  pl.BlockSpec((1, D_MODEL, D_EXPERT), lambda e, b0, nb: (e, 0, 0)),
                  pl.BlockSpec((1, D_MODEL, D_EXPERT), lambda e, b0, nb: (e, 0, 0)),
                  pl.BlockSpec((1, D_EXPERT, D_MODEL), lambda e, b0, nb: (e, 0, 0))],
        out_specs=any_spec,
        scratch_shapes=[pltpu.VMEM((EXPERT_IN_SLOTS, ROW_BLOCK, D_MODEL // 2), jnp.uint32),
                        pltpu.VMEM((2, ROW_BLOCK, D_MODEL // 2), jnp.uint32),
                        pltpu.VMEM((D_MODEL, 2 * D_EXPERT), BF16),
                        pltpu.VMEM((D_EXPERT, D_MODEL), BF16),
                        pltpu.SemaphoreType.DMA((EXPERT_IN_SLOTS,)),
                        pltpu.SemaphoreType.DMA((2,))],
    )
    return pl.pallas_call(
        _expert_kernel,
        grid_spec=grid_spec,
        out_shape=jax.ShapeDtypeStruct((N_ROWS, D_MODEL // 2), jnp.uint32),
        compiler_params=_cparams(("arbitrary",)),
        name="routed_experts",
    )(blk0, nblk, xs, we_gate, we_up, we_down)


def _final_kernel(h2_ref, yg_ref, gw_ref, x1_ref, mod_ref, wgu_ref, wd_ref, g_ref, o_ref):
    half = D_MODEL // 2
    gt_f = mod_ref[0, 5:6, :]
    x_lo, x_hi = _unpack_halves(h2_ref[...])
    h = (jnp.dot(x_lo.astype(BF16), wgu_ref[:half, :], preferred_element_type=F32)
         + jnp.dot(x_hi.astype(BF16), wgu_ref[half:, :], preferred_element_type=F32))
    hg = h[:, :D_EXPERT]
    act = hg * jax.nn.sigmoid(hg) * h[:, D_EXPERT:]
    shared = jnp.dot(act.astype(BF16), wd_ref[...], preferred_element_type=F32)
    y_lo = shared[:, :half]
    y_hi = shared[:, half:]
    for k in range(TOP_K):
        r_lo, r_hi = _unpack_halves(yg_ref[k])
        w = gw_ref[:, k:k + 1]
        y_lo = y_lo + w * r_lo
        y_hi = y_hi + w * r_hi
    ms = (jnp.sum(y_lo * y_lo, axis=-1, keepdims=True)
          + jnp.sum(y_hi * y_hi, axis=-1, keepdims=True)) * (1.0 / D_MODEL)
    inv = lax.rsqrt(ms + NORM_EPS)
    o_ref[:, :half] = x1_ref[:, :half] + gt_f[:, :half] * (y_lo * inv * g_ref[:, :half])
    o_ref[:, half:] = x1_ref[:, half:] + gt_f[:, half:] * (y_hi * inv * g_ref[:, half:])


def _final(h2p, yg, gw_t, x1, mod, ws_gu, ws_d, g_post_ffn):
    tm = TM_OUT
    tiles_per_seq = SEQ // tm
    tok = pl.BlockSpec((tm, D_MODEL), lambda i: (i, 0))
    return pl.pallas_call(
        _final_kernel,
        grid=(N_TOK // tm,),
        in_specs=[pl.BlockSpec((tm, D_MODEL // 2), lambda i: (i, 0)),
                  pl.BlockSpec((TOP_K, tm, D_MODEL // 2), lambda i: (0, i, 0)),
                  pl.BlockSpec((tm, TOP_K), lambda i: (i, 0)),
                  tok,
                  pl.BlockSpec((1, 8, D_MODEL), lambda i: (i // tiles_per_seq, 0, 0)),
                  pl.BlockSpec((D_MODEL, 2 * D_EXPERT), lambda i: (0, 0)),
                  pl.BlockSpec((D_EXPERT, D_MODEL), lambda i: (0, 0)),
                  pl.BlockSpec((1, D_MODEL), lambda i: (0, 0))],
        out_specs=tok,
        out_shape=jax.ShapeDtypeStruct((N_TOK, D_MODEL), F32),
        compiler_params=_cparams(("parallel",)),
        name="shared_final",
    )(h2p, yg, gw_t, x1, mod, ws_gu, ws_d, g_post_ffn)


def _sc_worker_id():
    return lax.axis_index("s") * SC_CORES + lax.axis_index("c")


def _dispatch_body(h_hbm, dest_hbm, xs_hbm, idx_v, rows_v, sem_l, sem_s):
    n = SC_CHUNKS_PER_WORKER
    c0 = _sc_worker_id() * n

    def load(i, b):
        return pltpu.async_copy(h_hbm.at[pl.ds((c0 + i) * SC_W, SC_W)], rows_v.at[b], sem_l.at[b])

    loads = [None] * n
    scat = [None] * n
    loads[0] = load(0, 0)
    for i in range(n):
        b = i % 2
        pltpu.sync_copy(dest_hbm.at[c0 + i], idx_v.at[b])
        loads[i].wait()
        if i + 1 < n:
            if i >= 1:
                for d in scat[i - 1]:
                    d.wait()
            loads[i + 1] = load(i + 1, 1 - b)
        scat[i] = [pltpu.async_copy(rows_v.at[b], xs_hbm.at[idx_v.at[b].at[k]], sem_s.at[b])
                   for k in range(TOP_K)]
    for i in (n - 2, n - 1):
        for d in scat[i]:
            d.wait()


def _sc_dispatch(h2p, dest3):
    mesh = plsc.VectorSubcoreMesh(core_axis_name="c", subcore_axis_name="s")
    return pl.kernel(
        _dispatch_body, mesh=mesh,
        out_type=jax.ShapeDtypeStruct((N_ROWS, D_MODEL // 2), jnp.uint32),
        scratch_types=[pltpu.VMEM((2, TOP_K, SC_W), jnp.int32),
                       pltpu.VMEM((2, SC_W, D_MODEL // 2), jnp.uint32),
                       pltpu.SemaphoreType.DMA((2,)), pltpu.SemaphoreType.DMA((2,))],
    )(h2p, dest3)


def _combine_body(ys_hbm, dest_hbm, yg_hbm, idx_v, rows_v, sem_g, sem_w):
    c0 = _sc_worker_id() * SC_CHUNKS_PER_WORKER

    @pl.loop(0, SC_CHUNKS_PER_WORKER)
    def _(i):
        c = c0 + i
        pltpu.sync_copy(dest_hbm.at[c], idx_v)
        g = [None] * TOP_K
        w = [None] * TOP_K
        g[0] = pltpu.async_copy(ys_hbm.at[idx_v.at[0]], rows_v.at[0], sem_g.at[0])
        for k in range(TOP_K):
            b = k % 2
            g[k].wait()
            if k + 1 < TOP_K:
                if k >= 1:
                    w[k - 1].wait()
                g[k + 1] = pltpu.async_copy(ys_hbm.at[idx_v.at[k + 1]], rows_v.at[1 - b], sem_g.at[1 - b])
            w[k] = pltpu.async_copy(rows_v.at[b], yg_hbm.at[k].at[pl.ds(c * SC_W, SC_W)], sem_w.at[b])
        w[TOP_K - 2].wait()
        w[TOP_K - 1].wait()


def _sc_combine(ysp, dest3):
    mesh = plsc.VectorSubcoreMesh(core_axis_name="c", subcore_axis_name="s")
    return pl.kernel(
        _combine_body, mesh=mesh,
        out_type=jax.ShapeDtypeStruct((TOP_K, N_TOK, D_MODEL // 2), jnp.uint32),
        scratch_types=[pltpu.VMEM((TOP_K, SC_W), jnp.int32),
                       pltpu.VMEM((2, SC_W, D_MODEL // 2), jnp.uint32),
                       pltpu.SemaphoreType.DMA((2,)), pltpu.SemaphoreType.DMA((2,))],
    )(ysp, dest3)


def kernel(x, c, w_ada, b_ada, g_pre_mix, g_post_mix, w_in, conv_w, conv_b, conv_ln_g, conv_ln_b,
           ssm_a_re, ssm_a_im, ssm_log_dt, ssm_b_re, ssm_b_im, ssm_c_re, ssm_c_im, ssm_d,
           ssm_w_glu, ssm_b_glu, g_out_conv, g_out_ssm, w_out, g_pre_ffn, g_post_ffn,
           w_router, b_router, we_gate, we_up, we_down, ws_gate, ws_up, ws_down):
    l = 0
    x2 = x.reshape(N_TOK, D_MODEL)
    r1 = lambda a: a.reshape(1, -1)

    c_pad = jnp.zeros((8, D_MODEL), F32).at[:BATCH].set(c)
    mod = _ada(c_pad, w_ada[l], r1(b_ada[l]))[:BATCH].reshape(BATCH, 6, D_MODEL)
    mod = jnp.concatenate([mod, jnp.zeros((BATCH, 2, D_MODEL), F32)], axis=1)

    v, u, ut = _inproj(x2, mod, r1(g_pre_mix[l]), w_in[l].astype(BF16))
    cw = jnp.concatenate([conv_w[l].reshape(CONV_WIDTH, CONV_CH), jnp.zeros((1, CONV_CH), F32)], axis=0)
    a_n = _conv(v.reshape(BATCH, SEQ, CONV_CH), cw, r1(conv_b[l]), r1(conv_ln_g[l]),
                r1(conv_ln_b[l]), r1(g_out_conv[l])).reshape(N_TOK, CONV_CH)

    pwr, pwi, s5_params, a_cat, b_q, b_s = _s5_operators(
        ssm_a_re[l], ssm_a_im[l], ssm_log_dt[l], ssm_b_re[l], ssm_b_im[l], ssm_c_re[l], ssm_c_im[l])
    yt = _s5(ut, pwr, pwi, s5_params, a_cat, b_q, b_s)

    tm = TM_MIX
    tri = (jnp.arange(tm)[:, None] < jnp.arange(tm)[None, :]).astype(BF16)
    wo = w_out[l].astype(BF16)
    x1, h2, eidx, rank, gw, cnt = _mix(
        a_n, yt, u, x2, mod, r1(ssm_d[l]), ssm_w_glu[l].astype(BF16), r1(ssm_b_glu[l]),
        r1(g_out_ssm[l]), wo[:CONV_CH], wo[CONV_CH:], r1(g_post_mix[l]), r1(g_pre_ffn[l]),
        w_router[l].T, b_router[l].reshape(N_EXPERTS, 1), tri)

    counts = cnt[:, 0].astype(jnp.int32)
    padded = (counts + ROW_BLOCK - 1) // ROW_BLOCK * ROW_BLOCK
    pends = jnp.cumsum(padded)
    pstart = pends - padded
    e_ids = jnp.arange(N_EXPERTS, dtype=jnp.int32)
    dest = rank + jnp.sum(jnp.where(eidx[..., None] == e_ids, pstart, 0), axis=-1)
    dest3 = dest.reshape(TOP_K, N_TOK // SC_W, SC_W).transpose(1, 0, 2)

    xs = _sc_dispatch(h2, dest3)
    ys = _experts(pstart // ROW_BLOCK, padded // ROW_BLOCK, xs, we_gate[l], we_up[l], we_down[l])
    yg = _sc_combine(ys, dest3)

    ws_gu = jnp.concatenate([ws_gate[l], ws_up[l]], axis=1).astype(BF16)
    out = _final(h2, yg, gw.T, x1, mod, ws_gu, ws_down[l].astype(BF16), r1(g_post_ffn[l]))
    return out.reshape(BATCH, SEQ, D_MODEL)
```

```python
import functools
import math

import jax
import jax.numpy as jnp
from jax import lax
from jax.experimental import pallas as pl
from jax.experimental.pallas import tpu as pltpu
from jax.experimental.pallas import tpu_sc as plsc

F32 = jnp.float32
BF16 = jnp.bfloat16

D_MODEL = 1024
BATCH = 2
SEQ = 8192
N_TOK = BATCH * SEQ
CONV_CH = 512
CONV_WIDTH = 31
SSM_CH = 512
SSM_GROUP_CH = 16
SSM_GROUPS = 32
SSM_STATE = 64
D_IN = 2 * CONV_CH + SSM_CH
N_EXPERTS = 64
TOP_K = 8
N_ROUTE_GROUPS = 8
TOPK_ROUTE_GROUPS = 4
D_EXPERT = 256
ROUTED_SCALE = 2.5
NORM_EPS = 1e-6

SUBLANES = 8
LANES = 128

TM_IN = 512
TL_CONV = 512
CONV_HALO = 32
CONV_ROWS = 64
S5_Q = 32
S5_QH = S5_Q * SSM_GROUP_CH
S5_CHUNKS = N_TOK // S5_Q
S5_CHUNKS_PER_SEQ = SEQ // S5_Q
TM_MIX = 512
ROW_BLOCK = 512
EXPERT_IN_SLOTS = 3
N_BLOCKS = N_TOK * TOP_K // ROW_BLOCK + N_EXPERTS
N_ROWS = N_BLOCKS * ROW_BLOCK
TM_OUT = 512
SC_CORES = 2
SC_SUBCORES = 16
SC_WORKERS = SC_CORES * SC_SUBCORES
SC_W = 64
SC_CHUNKS_PER_WORKER = N_TOK // (SC_WORKERS * SC_W)
VMEM_LIMIT = 48 * 1024 * 1024


def _cparams(sem):
    return pltpu.CompilerParams(dimension_semantics=sem, vmem_limit_bytes=VMEM_LIMIT)


def _pack_halves(x):
    n = x.shape[-1] // 2
    lo = lax.bitcast_convert_type(x[:, :n].astype(BF16).astype(F32), jnp.uint32)
    hi = lax.bitcast_convert_type(x[:, n:].astype(BF16).astype(F32), jnp.uint32)
    return hi | (lo >> 16)


def _unpack_halves(p):
    lo = lax.bitcast_convert_type(p << 16, F32)
    hi = lax.bitcast_convert_type(p & jnp.uint32(0xFFFF0000), F32)
    return lo, hi


def _rms(x, g):
    return x * lax.rsqrt(jnp.mean(x * x, axis=-1, keepdims=True) + NORM_EPS) * g


def _ada_kernel(c_ref, w_ref, b_ref, o_ref):
    c = c_ref[...]
    a = c * jax.nn.sigmoid(c)
    o_ref[...] = jnp.dot(a, w_ref[...], preferred_element_type=F32,
                         precision=lax.Precision.HIGHEST) + b_ref[...]


def _ada(c_pad, w_ada, b_ada):
    n = w_ada.shape[1]
    bn = 1536
    return pl.pallas_call(
        _ada_kernel,
        grid=(n // bn,),
        in_specs=[pl.BlockSpec((8, D_MODEL), lambda j: (0, 0)),
                  pl.BlockSpec((D_MODEL, bn), lambda j: (0, j)),
                  pl.BlockSpec((1, bn), lambda j: (0, j))],
        out_specs=pl.BlockSpec((8, bn), lambda j: (0, j)),
        out_shape=jax.ShapeDtypeStruct((8, n), F32),
        compiler_params=_cparams(("arbitrary",)),
        name="ada_mod",
    )(c_pad, w_ada, b_ada)


GROUPS_PER_LANE_TILE = LANES // SSM_GROUP_CH


def _to_group_chunks(u, tile_ref, ut_ref):
    n_chunks = u.shape[0] // S5_Q
    for j in range(SSM_CH // LANES):
        tile_ref[j] = u[:, LANES * j:LANES * (j + 1)]
    for j in range(SSM_CH // LANES):
        rows_t = [tile_ref[j, pl.ds(t, n_chunks, stride=S5_Q), :] for t in range(S5_Q)]
        for gg in range(GROUPS_PER_LANE_TILE):
            lo = gg * SSM_GROUP_CH
            row = jnp.concatenate([r[:, lo:lo + SSM_GROUP_CH] for r in rows_t], axis=1)
            ut_ref[j * GROUPS_PER_LANE_TILE + gg] = row.astype(ut_ref.dtype)


def _from_group_chunks(yt_ref, tile_ref):
    n_chunks = yt_ref.shape[1]
    for j in range(SSM_CH // LANES):
        for t in range(S5_Q):
            lo = t * SSM_GROUP_CH
            piece = jnp.concatenate(
                [yt_ref[j * GROUPS_PER_LANE_TILE + gg, :, lo:lo + SSM_GROUP_CH]
                 for gg in range(GROUPS_PER_LANE_TILE)], axis=1)
            tile_ref[j, pl.ds(t, n_chunks, stride=S5_Q), :] = piece
    return jnp.concatenate([tile_ref[j] for j in range(SSM_CH // LANES)], axis=1)


def _inproj_kernel(x_ref, mod_ref, g_ref, w_ref, v_ref, u_ref, ut_ref, tile_ref):
    x = x_ref[...]
    sh = mod_ref[0, 0:1, :]
    sc = mod_ref[0, 1:2, :]
    h = _rms(x, g_ref[...]) * (1.0 + sc) + sh
    z = jnp.dot(h.astype(BF16), w_ref[...], preferred_element_type=F32)
    v_ref[...] = z[:, :CONV_CH] * jax.nn.sigmoid(z[:, CONV_CH:2 * CONV_CH])
    u = z[:, 2 * CONV_CH:]
    u_ref[...] = u
    _to_group_chunks(u, tile_ref, ut_ref)


def _inproj(x2, mod, g_pre, w_in_bf):
    tiles_per_seq = SEQ // TM_IN
    return pl.pallas_call(
        _inproj_kernel,
        grid=(N_TOK // TM_IN,),
        in_specs=[pl.BlockSpec((TM_IN, D_MODEL), lambda i: (i, 0)),
                  pl.BlockSpec((1, 8, D_MODEL), lambda i: (i // tiles_per_seq, 0, 0)),
                  pl.BlockSpec((1, D_MODEL), lambda i: (0, 0)),
                  pl.BlockSpec((D_MODEL, D_IN), lambda i: (0, 0))],
        out_specs=[pl.BlockSpec((TM_IN, CONV_CH), lambda i: (i, 0)),
                   pl.BlockSpec((TM_IN, SSM_CH), lambda i: (i, 0)),
                   pl.BlockSpec((SSM_GROUPS, TM_IN // S5_Q, S5_QH), lambda i: (0, i, 0))],
        out_shape=[jax.ShapeDtypeStruct((N_TOK, CONV_CH), F32),
                   jax.ShapeDtypeStruct((N_TOK, SSM_CH), F32),
                   jax.ShapeDtypeStruct((SSM_GROUPS, S5_CHUNKS, S5_QH), BF16)],
        scratch_shapes=[pltpu.VMEM((SSM_CH // LANES, TM_IN, LANES), F32)],
        compiler_params=_cparams(("parallel",)),
        name="in_proj",
    )(x2, mod, g_pre, w_in_bf)


def _conv_kernel(vc_ref, vp_ref, w_ref, cb_ref, lg_ref, lb_ref, go_ref, o_ref, sh_ref):
    i = pl.program_id(1)
    keep = (i > 0).astype(F32)
    n_ext = TL_CONV + CONV_HALO
    sh_ref[0, 0:CONV_HALO, :] = vp_ref[0] * keep
    sh_ref[0, CONV_HALO:, :] = vc_ref[0]
    for s in range(1, SUBLANES):
        sh_ref[s, 0:n_ext - s, :] = sh_ref[0, s:n_ext, :]
    off = CONV_HALO - (CONV_WIDTH - 1)
    for r in range(TL_CONV // CONV_ROWS):
        acc = None
        for j in range(CONV_WIDTH):
            s = (off + j) % SUBLANES
            al = r * CONV_ROWS + (off + j) - s
            term = w_ref[j:j + 1, :] * sh_ref[s, al:al + CONV_ROWS, :]
            acc = term if acc is None else acc + term
        y = acc + cb_ref[...]
        mu = jnp.mean(y, axis=-1, keepdims=True)
        d = y - mu
        var = jnp.mean(d * d, axis=-1, keepdims=True)
        yn = d * lax.rsqrt(var + NORM_EPS) * lg_ref[...] + lb_ref[...]
        a = yn * jax.nn.sigmoid(yn)
        o_ref[0, r * CONV_ROWS:(r + 1) * CONV_ROWS, :] = _rms(a, go_ref[...]).astype(BF16)


def _conv(v3, conv_w, conv_b, ln_g, ln_b, g_out):
    halo_per_tile = TL_CONV // CONV_HALO
    vec = pl.BlockSpec((1, CONV_CH), lambda b, i: (0, 0))
    return pl.pallas_call(
        _conv_kernel,
        grid=(BATCH, SEQ // TL_CONV),
        in_specs=[pl.BlockSpec((1, TL_CONV, CONV_CH), lambda b, i: (b, i, 0)),
                  pl.BlockSpec((1, CONV_HALO, CONV_CH),
                               lambda b, i: (b, jnp.maximum(i * halo_per_tile - 1, 0), 0)),
                  pl.BlockSpec((CONV_WIDTH + 1, CONV_CH), lambda b, i: (0, 0)),
                  vec, vec, vec, vec],
        out_specs=pl.BlockSpec((1, TL_CONV, CONV_CH), lambda b, i: (b, i, 0)),
        out_shape=jax.ShapeDtypeStruct((BATCH, SEQ, CONV_CH), BF16),
        scratch_shapes=[pltpu.VMEM((SUBLANES, TL_CONV + CONV_HALO, CONV_CH), F32)],
        compiler_params=_cparams(("parallel", "arbitrary")),
        name="conv_module",
    )(v3, v3, conv_w, conv_b, ln_g, ln_b, g_out)


S5_GROUP_ROWS = S5_CHUNKS + 8


S5_POW_ROWS = (S5_Q + 1 + SUBLANES - 1) // SUBLANES * SUBLANES
(S5_BB_RI, S5_BB_NIR, S5_BB_IR, S5_BB_RNI, S5_CC_RI, S5_CC_NIR, S5_N_PARAM) = range(7)


def _s5_kernel(ut_ref, pwr_ref, pwi_ref, par_ref, a_ref, bq_ref, bs_ref, yt_ref, sin_s, sp_s):
    phase = pl.program_id(0)
    g = pl.program_id(1)
    q = S5_Q
    n = 2 * SSM_STATE
    row0 = pl.multiple_of(g * S5_GROUP_ROWS, 8)

    def lam_pow(j):
        return pwr_ref[0, j:j + 1, :], pwi_ref[0, j:j + 1, :]

    @pl.when(phase == 0)
    def _():
        bb_ri, bb_nir = par_ref[0, S5_BB_RI], par_ref[0, S5_BB_NIR]
        bb_ir, bb_rni = par_ref[0, S5_BB_IR], par_ref[0, S5_BB_RNI]
        blk_q, blk_s = [], []
        for t in range(q):
            pr, pi_ = lam_pow(q - 1 - t)
            blk_q.append(pr * bb_ri + pi_ * bb_nir)
            blk_s.append(pr * bb_ir + pi_ * bb_rni)
        wst = jnp.concatenate([jnp.concatenate(blk_q, axis=0), jnp.concatenate(blk_s, axis=0)], axis=1)
        r = jnp.dot(ut_ref[0], wst.astype(BF16), preferred_element_type=F32)
        sin_s[0, pl.ds(row0, S5_CHUNKS), :] = r[:, :n]
        sin_s[1, pl.ds(row0, S5_CHUNKS), :] = r[:, n:]

    @pl.when((phase == 1) & (g == 0))
    def _():
        a = a_ref[...]
        bq = bq_ref[...]
        bs = bs_ref[...]

        def body(c, carry):
            nxt = []
            for b in range(BATCH):
                x, xs = carry[b]
                rows = pl.ds(b * S5_CHUNKS_PER_SEQ + c, SSM_GROUPS, stride=S5_GROUP_ROWS)
                sp_s[rows, :] = x
                nxt.append((a * x + bq * xs + sin_s[0, rows, :], a * xs + bs * x + sin_s[1, rows, :]))
            return tuple(nxt)

        z = jnp.zeros((SSM_GROUPS, n), F32)
        lax.fori_loop(0, S5_CHUNKS_PER_SEQ, body, tuple((z, z) for _ in range(BATCH)))

    @pl.when(phase == 1)
    def _():
        cc_ri, cc_nir = par_ref[0, S5_CC_RI], par_ref[0, S5_CC_NIR]
        cl = []
        for j in range(q + 1):
            pr, pi_ = lam_pow(j)
            cl.append(pr * cc_ri + pi_ * cc_nir)
        cl_lo = jnp.concatenate(cl[:q], axis=0)
        cl_hi = jnp.concatenate(cl[1:], axis=0)
        lane = lax.broadcasted_iota(jnp.int32, (1, n), 1)
        vgt = (cl_hi * jnp.where(lane < SSM_STATE, 1.0, -1.0)).astype(BF16)
        kt = lax.dot_general(par_ref[0, S5_BB_RNI], cl_lo, (((1,), (1,)), ((), ())),
                             preferred_element_type=F32, precision=lax.Precision.HIGHEST)
        padded = jnp.concatenate([jnp.zeros_like(kt), kt], axis=1)
        tg = jnp.concatenate(
            [padded[:, (q - t) * SSM_GROUP_CH:(q - t) * SSM_GROUP_CH + S5_QH] for t in range(q)],
            axis=0).astype(BF16)
        sp = sp_s[pl.ds(row0, S5_CHUNKS), :]
        y = jnp.dot(ut_ref[0], tg, preferred_element_type=F32)
        yt_ref[0] = y + lax.dot_general(sp.astype(BF16), vgt, (((1,), (1,)), ((), ())),
                                        preferred_element_type=F32)


def _s5(ut, pwr, pwi, params, a_cat, b_q, b_s):
    vec = pl.BlockSpec((SSM_GROUPS, 2 * SSM_STATE), lambda p, g: (0, 0))
    powers = pl.BlockSpec((1, S5_POW_ROWS, 2 * SSM_STATE), lambda p, g: (g, 0, 0))
    return pl.pallas_call(
        _s5_kernel,
        grid=(2, SSM_GROUPS),
        in_specs=[pl.BlockSpec((1, S5_CHUNKS, S5_QH), lambda p, g: (g, 0, 0)),
                  powers, powers,
                  pl.BlockSpec((1, S5_N_PARAM, SSM_GROUP_CH, 2 * SSM_STATE), lambda p, g: (g, 0, 0, 0)),
                  vec, vec, vec],
        out_specs=pl.BlockSpec((1, S5_CHUNKS, S5_QH), lambda p, g: (g * p, 0, 0)),
        out_shape=jax.ShapeDtypeStruct((SSM_GROUPS, S5_CHUNKS, S5_QH), F32),
        scratch_shapes=[pltpu.VMEM((2, SSM_GROUPS * S5_GROUP_ROWS, 2 * SSM_STATE), F32),
                        pltpu.VMEM((SSM_GROUPS * S5_GROUP_ROWS, 2 * SSM_STATE), F32)],
        compiler_params=_cparams(("arbitrary", "arbitrary")),
        name="s5_chunked",
    )(ut, pwr, pwi, params, a_cat, b_q, b_s)


def _s5_operators(a_re, a_im, log_dt, b_re, b_im, c_re, c_im):
    q = S5_Q
    dt = jnp.exp(log_dt)[:, None]
    ar, ai = a_re, a_im
    mag = jnp.exp(ar * dt)
    lr = mag * jnp.cos(ai * dt)
    li = mag * jnp.sin(ai * dt)
    den = ar * ar + ai * ai
    nr = lr - 1.0
    kr = (nr * ar + li * ai) / den
    ki = (li * ar - nr * ai) / den
    bbr = kr[..., None] * b_re - ki[..., None] * b_im
    bbi = kr[..., None] * b_im + ki[..., None] * b_re
    j = jnp.arange(q + 1, dtype=F32)[None, :, None]
    pmag = jnp.exp(ar[:, None, :] * dt[:, :, None] * j)
    pang = ai[:, None, :] * dt[:, :, None] * j
    pr = pmag * jnp.cos(pang)
    pi_ = pmag * jnp.sin(pang)
    pad = ((0, 0), (0, S5_POW_ROWS - (q + 1)), (0, 0))
    pwr = jnp.pad(jnp.concatenate([pr, pr], axis=-1), pad)
    pwi = jnp.pad(jnp.concatenate([pi_, pi_], axis=-1), pad)
    br_t = bbr.transpose(0, 2, 1)
    bi_t = bbi.transpose(0, 2, 1)
    cat = lambda a, b: jnp.concatenate([a, b], axis=-1)
    stack = [None] * S5_N_PARAM
    stack[S5_BB_RI] = cat(br_t, bi_t)
    stack[S5_BB_NIR] = cat(-bi_t, br_t)
    stack[S5_BB_IR] = cat(bi_t, br_t)
    stack[S5_BB_RNI] = cat(br_t, -bi_t)
    stack[S5_CC_RI] = cat(c_re, c_im)
    stack[S5_CC_NIR] = cat(-c_im, c_re)
    params = jnp.stack(stack, axis=1)
    aq_r, aq_i = pr[:, q], pi_[:, q]
    a_cat = cat(aq_r, aq_r)
    b_q = cat(-aq_i, aq_i)
    b_s = cat(aq_i, -aq_i)
    return pwr, pwi, params, a_cat, b_q, b_s


def _gelu_tanh(x):
    return 0.5 * x * (1.0 + jnp.tanh(math.sqrt(2.0 / math.pi) * (x + 0.044715 * (x * x * x))))


def _mix_kernel(an_ref, yt_ref, u_ref, x_ref, mod_ref, d_ref, wglu_ref, bglu_ref, gos_ref,
                woa_ref, wob_ref, gpm_ref, gpf_ref, wr_ref, br_ref, tri_ref,
                x1_ref, h2_ref, eidx_ref, rank_ref, gw_ref, cnt_ref, run_ref, tile_ref):
    i = pl.program_id(0)
    tm = TM_MIX

    @pl.when(i == 0)
    def _():
        run_ref[...] = jnp.zeros_like(run_ref)

    gt_m = mod_ref[0, 2:3, :]
    sh_f = mod_ref[0, 3:4, :]
    sc_f = mod_ref[0, 4:5, :]

    yy = _from_group_chunks(yt_ref, tile_ref) + d_ref[...] * u_ref[...]
    g = _gelu_tanh(yy)
    gl = jnp.dot(g.astype(BF16), wglu_ref[...], preferred_element_type=F32) + bglu_ref[...]
    ob = g * jax.nn.sigmoid(gl)
    bn = _rms(ob, gos_ref[...]).astype(BF16)
    o = (jnp.dot(an_ref[...], woa_ref[...], preferred_element_type=F32)
         + jnp.dot(bn, wob_ref[...], preferred_element_type=F32))
    x1 = x_ref[...] + gt_m * _rms(o, gpm_ref[...])
    x1_ref[...] = x1
    h2 = _rms(x1, gpf_ref[...]) * (1.0 + sc_f) + sh_f
    h2_ref[...] = _pack_halves(h2)

    logits = lax.dot_general(wr_ref[...], h2, (((1,), (1,)), ((), ())),
                             preferred_element_type=F32, precision=lax.Precision.HIGHEST)
    scores = jax.nn.sigmoid(logits)
    biased = scores + br_ref[...]
    ng = N_ROUTE_GROUPS
    gsz = N_EXPERTS // ng
    b3 = biased.reshape(ng, gsz, tm)
    s3 = scores.reshape(ng, gsz, tm)
    sub = lax.broadcasted_iota(jnp.int32, (ng, gsz, tm), 1).astype(F32)
    grp = lax.broadcasted_iota(jnp.int32, (ng, gsz, tm), 0).astype(F32)
    eid = grp * gsz + sub
    neg = -jnp.inf
    m1 = jnp.max(b3, axis=1, keepdims=True)
    i1 = jnp.min(jnp.where(b3 == m1, sub, float(gsz)), axis=1, keepdims=True)
    m2 = jnp.max(jnp.where(sub == i1, neg, b3), axis=1, keepdims=True)
    gs = m1 + m2
    gi = lax.broadcasted_iota(jnp.int32, (ng, 1, tm), 0)
    beaten = jnp.zeros((ng, 1, tm), F32)
    for gp in range(ng):
        o_ = gs[gp:gp + 1]
        beats = (o_ > gs) | ((o_ == gs) & (gi > gp))
        beaten = beaten + beats.astype(F32)
    gmask = beaten < float(TOPK_ROUTE_GROUPS)
    masked = jnp.where(gmask, b3, neg)

    sels = []
    picked = jnp.zeros((ng, gsz, tm), F32)
    for k in range(TOP_K):
        m = jnp.max(jnp.max(masked, axis=0, keepdims=True), axis=1, keepdims=True)
        cand = jnp.where(masked == m, eid, float(N_EXPERTS))
        sel = jnp.min(jnp.min(cand, axis=0, keepdims=True), axis=1, keepdims=True)
        oh = eid == sel
        masked = jnp.where(oh, neg, masked)
        picked = jnp.where(oh, 1.0, picked)
        sels.append(sel)

    pm = picked.reshape(N_EXPERTS, tm)
    prefix = jnp.dot(pm.astype(BF16), tri_ref[...], preferred_element_type=F32) + run_ref[:, 0:1]
    p3 = prefix.reshape(ng, gsz, tm)
    run_new = run_ref[...] + jnp.sum(pm, axis=1, keepdims=True)
    run_ref[...] = run_new
    cnt_ref[...] = run_new

    sc_rows = []
    for k in range(TOP_K):
        oh = eid == sels[k]
        sc_k = jnp.sum(jnp.sum(jnp.where(oh, s3, 0.0), axis=0, keepdims=True), axis=1, keepdims=True)
        rk_k = jnp.sum(jnp.sum(jnp.where(oh, p3, 0.0), axis=0, keepdims=True), axis=1, keepdims=True)
        sc_rows.append(sc_k)
        eidx_ref[k:k + 1, :] = sels[k].reshape(1, tm).astype(jnp.int32)
        rank_ref[k:k + 1, :] = rk_k.reshape(1, tm).astype(jnp.int32)
    tot = sc_rows[0]
    for k in range(1, TOP_K):
        tot = tot + sc_rows[k]
    inv = ROUTED_SCALE / (tot + 1e-20)
    for k in range(TOP_K):
        gw_ref[k:k + 1, :] = (sc_rows[k] * inv).reshape(1, tm)


def _mix(a_n, yt, u2, x2, mod, d_skip, wglu_bf, b_glu, g_out_ssm, wo_a, wo_b, g_post_mix,
         g_pre_ffn, w_router_t, b_router_col, tri):
    tm = TM_MIX
    tiles_per_seq = SEQ // tm
    row = lambda n: pl.BlockSpec((1, n), lambda i: (0, 0))
    full = lambda a, b: pl.BlockSpec((a, b), lambda i: (0, 0))
    tok = lambda n: pl.BlockSpec((tm, n), lambda i: (i, 0))
    col = pl.BlockSpec((TOP_K, tm), lambda i: (0, i))
    return pl.pallas_call(
        _mix_kernel,
        grid=(N_TOK // tm,),
        in_specs=[tok(CONV_CH),
                  pl.BlockSpec((SSM_GROUPS, tm // S5_Q, S5_QH), lambda i: (0, i, 0)),
                  tok(SSM_CH), tok(D_MODEL),
                  pl.BlockSpec((1, 8, D_MODEL), lambda i: (i // tiles_per_seq, 0, 0)),
                  row(SSM_CH), full(SSM_CH, SSM_CH), row(SSM_CH), row(SSM_CH),
                  full(CONV_CH, D_MODEL), full(SSM_CH, D_MODEL), row(D_MODEL), row(D_MODEL),
                  full(N_EXPERTS, D_MODEL), full(N_EXPERTS, 1), full(tm, tm)],
        out_specs=[tok(D_MODEL), tok(D_MODEL // 2), col, col, col,
                   pl.BlockSpec((N_EXPERTS, 128), lambda i: (0, 0))],
        out_shape=[jax.ShapeDtypeStruct((N_TOK, D_MODEL), F32),
                   jax.ShapeDtypeStruct((N_TOK, D_MODEL // 2), jnp.uint32),
                   jax.ShapeDtypeStruct((TOP_K, N_TOK), jnp.int32),
                   jax.ShapeDtypeStruct((TOP_K, N_TOK), jnp.int32),
                   jax.ShapeDtypeStruct((TOP_K, N_TOK), F32),
                   jax.ShapeDtypeStruct((N_EXPERTS, 128), F32)],
        scratch_shapes=[pltpu.VMEM((N_EXPERTS, 128), F32),
                        pltpu.VMEM((SSM_CH // LANES, tm, LANES), F32)],
        compiler_params=_cparams(("arbitrary",)),
        name="mix_out_router",
    )(a_n, yt, u2, x2, mod, d_skip, wglu_bf, b_glu, g_out_ssm, wo_a, wo_b, g_post_mix,
      g_pre_ffn, w_router_t, b_router_col, tri)


def _expert_kernel(blk0_ref, nblk_ref, xs_hbm, wg_ref, wu_ref, wd_ref, ys_hbm,
                   xbuf, ybuf, wgu_s, wd_s, sem_in, sem_out):
    e = pl.program_id(0)
    n = nblk_ref[e]
    b0 = blk0_ref[e]

    def rows(b):
        return pl.ds(pl.multiple_of((b0 + b) * ROW_BLOCK, ROW_BLOCK), ROW_BLOCK)

    def in_copy(b, slot):
        return pltpu.make_async_copy(xs_hbm.at[rows(b)], xbuf.at[slot], sem_in.at[slot])

    def out_copy(b, slot):
        return pltpu.make_async_copy(ybuf.at[slot], ys_hbm.at[rows(b)], sem_out.at[slot])

    for b in range(EXPERT_IN_SLOTS - 1):
        @pl.when(b < n)
        def _():
            in_copy(b, b).start()

    wgu_s[:, :D_EXPERT] = wg_ref[0].astype(BF16)
    wgu_s[:, D_EXPERT:] = wu_ref[0].astype(BF16)
    wd_s[...] = wd_ref[0].astype(BF16)

    def body(b, carry):
        slot = b % 2
        in_slot = b % EXPERT_IN_SLOTS
        in_copy(b, in_slot).wait()
        ahead = b + EXPERT_IN_SLOTS - 1

        @pl.when(ahead < n)
        def _():
            in_copy(ahead, ahead % EXPERT_IN_SLOTS).start()

        @pl.when(b >= 2)
        def _():
            out_copy(b - 2, slot).wait()

        x_lo, x_hi = _unpack_halves(xbuf[in_slot])
        x = jnp.concatenate([x_lo.astype(BF16), x_hi.astype(BF16)], axis=1)
        h = jnp.dot(x, wgu_s[...], preferred_element_type=F32)
        hg = h[:, :D_EXPERT]
        act = hg * jax.nn.sigmoid(hg) * h[:, D_EXPERT:]
        ybuf[slot] = _pack_halves(jnp.dot(act.astype(BF16), wd_s[...], preferred_element_type=F32))
        out_copy(b, slot).start(priority=1)
        return carry

    lax.fori_loop(0, n, body, 0)

    @pl.when(n >= 2)
    def _():
        out_copy(n - 2, n % 2).wait()

    @pl.when(n >= 1)
    def _():
        out_copy(n - 1, (n - 1) % 2).wait()


def _experts(blk0, nblk, xs, we_gate, we_up, we_down):
    any_spec = pl.BlockSpec(memory_space=pl.ANY)
    grid_spec = pltpu.PrefetchScalarGridSpec(
        num_scalar_prefetch=2,
        grid=(N_EXPERTS,),
        in_specs=[any_spec,
                  pl.BlockSpec((1, D_MODEL, D_EXPERT), lambda e, b0, nb: (e, 0, 0)),
                  pl.BlockSpec((1, D_MODEL, D_EXPERT), lambda e, b0, nb: (e, 0, 0)),
                  pl.BlockSpec((1, D_EXPERT, D_MODEL), lambda e, b0, nb: (e, 0, 0))],
        out_specs=any_spec,
        scratch_shapes=[pltpu.VMEM((EXPERT_IN_SLOTS, ROW_BLOCK, D_MODEL // 2), jnp.uint32),
                        pltpu.VMEM((2, ROW_BLOCK, D_MODEL // 2), jnp.uint32),
                        pltpu.VMEM((D_MODEL, 2 * D_EXPERT), BF16),
                        pltpu.VMEM((D_EXPERT, D_MODEL), BF16),
                        pltpu.SemaphoreType.DMA((EXPERT_IN_SLOTS,)),
                        pltpu.SemaphoreType.DMA((2,))],
    )
    return pl.pallas_call(
        _expert_kernel,
        grid_spec=grid_spec,
        out_shape=jax.ShapeDtypeStruct((N_ROWS, D_MODEL // 2), jnp.uint32),
        compiler_params=_cparams(("arbitrary",)),
        name="routed_experts",
    )(blk0, nblk, xs, we_gate, we_up, we_down)


def _final_kernel(h2_ref, yg_ref, gw_ref, x1_ref, mod_ref, wgu_ref, wd_ref, g_ref, o_ref):
    half = D_MODEL // 2
    gt_f = mod_ref[0, 5:6, :]
    x_lo, x_hi = _unpack_halves(h2_ref[...])
    h = (jnp.dot(x_lo.astype(BF16), wgu_ref[:half, :], preferred_element_type=F32)
         + jnp.dot(x_hi.astype(BF16), wgu_ref[half:, :], preferred_element_type=F32))
    hg = h[:, :D_EXPERT]
    act = hg * jax.nn.sigmoid(hg) * h[:, D_EXPERT:]
    shared = jnp.dot(act.astype(BF16), wd_ref[...], preferred_element_type=F32)
    y_lo = shared[:, :half]
    y_hi = shared[:, half:]
    for k in range(TOP_K):
        r_lo, r_hi = _unpack_halves(yg_ref[k])
        w = gw_ref[:, k:k + 1]
        y_lo = y_lo + w * r_lo
        y_hi = y_hi + w * r_hi
    ms = (jnp.sum(y_lo * y_lo, axis=-1, keepdims=True)
          + jnp.sum(y_hi * y_hi, axis=-1, keepdims=True)) * (1.0 / D_MODEL)
    inv = lax.rsqrt(ms + NORM_EPS)
    o_ref[:, :half] = x1_ref[:, :half] + gt_f[:, :half] * (y_lo * inv * g_ref[:, :half])
    o_ref[:, half:] = x1_ref[:, half:] + gt_f[:, half:] * (y_hi * inv * g_ref[:, half:])


def _final(h2p, yg, gw_t, x1, mod, ws_gu, ws_d, g_post_ffn):
    tm = TM_OUT
    tiles_per_seq = SEQ // tm
    tok = pl.BlockSpec((tm, D_MODEL), lambda i: (i, 0))
    return pl.pallas_call(
        _final_kernel,
        grid=(N_TOK // tm,),
        in_specs=[pl.BlockSpec((tm, D_MODEL // 2), lambda i: (i, 0)),
                  pl.BlockSpec((TOP_K, tm, D_MODEL // 2), lambda i: (0, i, 0)),
                  pl.BlockSpec((tm, TOP_K), lambda i: (i, 0)),
                  tok,
                  pl.BlockSpec((1, 8, D_MODEL), lambda i: (i // tiles_per_seq, 0, 0)),
                  pl.BlockSpec((D_MODEL, 2 * D_EXPERT), lambda i: (0, 0)),
                  pl.BlockSpec((D_EXPERT, D_MODEL), lambda i: (0, 0)),
                  pl.BlockSpec((1, D_MODEL), lambda i: (0, 0))],
        out_specs=tok,
        out_shape=jax.ShapeDtypeStruct((N_TOK, D_MODEL), F32),
        compiler_params=_cparams(("parallel",)),
        name="shared_final",
    )(h2p, yg, gw_t, x1, mod, ws_gu, ws_d, g_post_ffn)


def _sc_worker_id():
    return lax.axis_index("s") * SC_CORES + lax.axis_index("c")


def _dispatch_body(h_hbm, dest_hbm, xs_hbm, idx_v, rows_v, sem_l, sem_s):
    n = SC_CHUNKS_PER_WORKER
    c0 = _sc_worker_id() * n

    def load(i, b):
        return pltpu.async_copy(h_hbm.at[pl.ds((c0 + i) * SC_W, SC_W)], rows_v.at[b], sem_l.at[b])

    loads = [None] * n
    scat = [None] * n
    loads[0] = load(0, 0)
    for i in range(n):
        b = i % 2
        pltpu.sync_copy(dest_hbm.at[c0 + i], idx_v.at[b])
        loads[i].wait()
        if i + 1 < n:
            if i >= 1:
                for d in scat[i - 1]:
                    d.wait()
            loads[i + 1] = load(i + 1, 1 - b)
        scat[i] = [pltpu.async_copy(rows_v.at[b], xs_hbm.at[idx_v.at[b].at[k]], sem_s.at[b])
                   for k in range(TOP_K)]
    for i in (n - 2, n - 1):
        for d in scat[i]:
            d.wait()


def _sc_dispatch(h2p, dest3):
    mesh = plsc.VectorSubcoreMesh(core_axis_name="c", subcore_axis_name="s")
    return pl.kernel(
        _dispatch_body, mesh=mesh,
        out_type=jax.ShapeDtypeStruct((N_ROWS, D_MODEL // 2), jnp.uint32),
        scratch_types=[pltpu.VMEM((2, TOP_K, SC_W), jnp.int32),
                       pltpu.VMEM((2, SC_W, D_MODEL // 2), jnp.uint32),
                       pltpu.SemaphoreType.DMA((2,)), pltpu.SemaphoreType.DMA((2,))],
    )(h2p, dest3)


def _combine_body(ys_hbm, dest_hbm, yg_hbm, idx_v, rows_v, sem_g, sem_w):
    c0 = _sc_worker_id() * SC_CHUNKS_PER_WORKER

    @pl.loop(0, SC_CHUNKS_PER_WORKER)
    def _(i):
        c = c0 + i
        pltpu.sync_copy(dest_hbm.at[c], idx_v)
        g = [None] * TOP_K
        w = [None] * TOP_K
        g[0] = pltpu.async_copy(ys_hbm.at[idx_v.at[0]], rows_v.at[0], sem_g.at[0])
        for k in range(TOP_K):
            b = k % 2
            g[k].wait()
            if k + 1 < TOP_K:
                if k >= 1:
                    w[k - 1].wait()
                g[k + 1] = pltpu.async_copy(ys_hbm.at[idx_v.at[k + 1]], rows_v.at[1 - b], sem_g.at[1 - b])
            w[k] = pltpu.async_copy(rows_v.at[b], yg_hbm.at[k].at[pl.ds(c * SC_W, SC_W)], sem_w.at[b])
        w[TOP_K - 2].wait()
        w[TOP_K - 1].wait()


def _sc_combine(ysp, dest3):
    mesh = plsc.VectorSubcoreMesh(core_axis_name="c", subcore_axis_name="s")
    return pl.kernel(
        _combine_body, mesh=mesh,
        out_type=jax.ShapeDtypeStruct((TOP_K, N_TOK, D_MODEL // 2), jnp.uint32),
        scratch_types=[pltpu.VMEM((TOP_K, SC_W), jnp.int32),
                       pltpu.VMEM((2, SC_W, D_MODEL // 2), jnp.uint32),
                       pltpu.SemaphoreType.DMA((2,)), pltpu.SemaphoreType.DMA((2,))],
    )(ysp, dest3)


def kernel(x, c, w_ada, b_ada, g_pre_mix, g_post_mix, w_in, conv_w, conv_b, conv_ln_g, conv_ln_b,
           ssm_a_re, ssm_a_im, ssm_log_dt, ssm_b_re, ssm_b_im, ssm_c_re, ssm_c_im, ssm_d,
           ssm_w_glu, ssm_b_glu, g_out_conv, g_out_ssm, w_out, g_pre_ffn, g_post_ffn,
           w_router, b_router, we_gate, we_up, we_down, ws_gate, ws_up, ws_down):
    l = 0
    x2 = x.reshape(N_TOK, D_MODEL)
    r1 = lambda a: a.reshape(1, -1)

    c_pad = jnp.zeros((8, D_MODEL), F32).at[:BATCH].set(c)
    mod = _ada(c_pad, w_ada[l], r1(b_ada[l]))[:BATCH].reshape(BATCH, 6, D_MODEL)
    mod = jnp.concatenate([mod, jnp.zeros((BATCH, 2, D_MODEL), F32)], axis=1)

    v, u, ut = _inproj(x2, mod, r1(g_pre_mix[l]), w_in[l].astype(BF16))
    cw = jnp.concatenate([conv_w[l].reshape(CONV_WIDTH, CONV_CH), jnp.zeros((1, CONV_CH), F32)], axis=0)
    a_n = _conv(v.reshape(BATCH, SEQ, CONV_CH), cw, r1(conv_b[l]), r1(conv_ln_g[l]),
                r1(conv_ln_b[l]), r1(g_out_conv[l])).reshape(N_TOK, CONV_CH)

    pwr, pwi, s5_params, a_cat, b_q, b_s = _s5_operators(
        ssm_a_re[l], ssm_a_im[l], ssm_log_dt[l], ssm_b_re[l], ssm_b_im[l], ssm_c_re[l], ssm_c_im[l])
    yt = _s5(ut, pwr, pwi, s5_params, a_cat, b_q, b_s)

    tm = TM_MIX
    tri = (jnp.arange(tm)[:, None] < jnp.arange(tm)[None, :]).astype(BF16)
    wo = w_out[l].astype(BF16)
    x1, h2, eidx, rank, gw, cnt = _mix(
        a_n, yt, u, x2, mod, r1(ssm_d[l]), ssm_w_glu[l].astype(BF16), r1(ssm_b_glu[l]),
        r1(g_out_ssm[l]), wo[:CONV_CH], wo[CONV_CH:], r1(g_post_mix[l]), r1(g_pre_ffn[l]),
        w_router[l].T, b_router[l].reshape(N_EXPERTS, 1), tri)

    counts = cnt[:, 0].astype(jnp.int32)
    padded = (counts + ROW_BLOCK - 1) // ROW_BLOCK * ROW_BLOCK
    pends = jnp.cumsum(padded)
    pstart = pends - padded
    e_ids = jnp.arange(N_EXPERTS, dtype=jnp.int32)
    dest = rank + jnp.sum(jnp.where(eidx[..., None] == e_ids, pstart, 0), axis=-1)
    dest3 = dest.reshape(TOP_K, N_TOK // SC_W, SC_W).transpose(1, 0, 2)

    xs = _sc_dispatch(h2, dest3)
    ys = _experts(pstart // ROW_BLOCK, padded // ROW_BLOCK, xs, we_gate[l], we_up[l], we_down[l])
    yg = _sc_combine(ys, dest3)

    ws_gu = jnp.concatenate([ws_gate[l], ws_up[l]], axis=1).astype(BF16)
    out = _final(h2, yg, gw.T, x1, mod, ws_gu, ws_down[l].astype(BF16), r1(g_post_ffn[l]))
    return out.reshape(BATCH, SEQ, D_MODEL)
```

```python
import functools
import math

import jax
import jax.numpy as jnp
from jax import lax
from jax.experimental import pallas as pl
from jax.experimental.pallas import tpu as pltpu
from jax.experimental.pallas import tpu_sc as plsc

F32 = jnp.float32
BF16 = jnp.bfloat16

D_MODEL = 1024
BATCH = 2
SEQ = 8192
N_TOK = BATCH * SEQ
CONV_CH = 512
CONV_WIDTH = 31
SSM_CH = 512
SSM_GROUP_CH = 16
SSM_GROUPS = 32
SSM_STATE = 64
D_IN = 2 * CONV_CH + SSM_CH
N_EXPERTS = 64
TOP_K = 8
N_ROUTE_GROUPS = 8
TOPK_ROUTE_GROUPS = 4
D_EXPERT = 256
ROUTED_SCALE = 2.5
NORM_EPS = 1e-6

SUBLANES = 8
LANES = 128

TM_IN = 512
TL_CONV = 512
CONV_HALO = 32
CONV_ROWS = 64
S5_Q = 32
S5_QH = S5_Q * SSM_GROUP_CH
S5_CHUNKS = N_TOK // S5_Q
S5_CHUNKS_PER_SEQ = SEQ // S5_Q
TM_MIX = 512
ROW_BLOCK = 512
EXPERT_IN_SLOTS = 3
HALF_TOK = SEQ
N_HALVES = N_TOK // HALF_TOK
N_BLOCKS = HALF_TOK * TOP_K // ROW_BLOCK + N_EXPERTS
N_ROWS = N_BLOCKS * ROW_BLOCK
TM_OUT = 512
SC_CORES = 2
SC_SUBCORES = 16
SC_WORKERS = SC_CORES * SC_SUBCORES
SC_W = 64
SC_CHUNKS_PER_WORKER = HALF_TOK // (SC_WORKERS * SC_W)
VMEM_LIMIT = 48 * 1024 * 1024


def _cparams(sem):
    return pltpu.CompilerParams(dimension_semantics=sem, vmem_limit_bytes=VMEM_LIMIT)


def _pack_halves(x):
    n = x.shape[-1] // 2
    lo = lax.bitcast_convert_type(x[:, :n].astype(BF16).astype(F32), jnp.uint32)
    hi = lax.bitcast_convert_type(x[:, n:].astype(BF16).astype(F32), jnp.uint32)
    return hi | (lo >> 16)


def _unpack_halves(p):
    lo = lax.bitcast_convert_type(p << 16, F32)
    hi = lax.bitcast_convert_type(p & jnp.uint32(0xFFFF0000), F32)
    return lo, hi


def _rms(x, g):
    return x * lax.rsqrt(jnp.mean(x * x, axis=-1, keepdims=True) + NORM_EPS) * g


def _ada_kernel(c_ref, w_ref, b_ref, o_ref):
    c = c_ref[...]
    a = c * jax.nn.sigmoid(c)
    o_ref[...] = jnp.dot(a, w_ref[...], preferred_element_type=F32,
                         precision=lax.Precision.HIGHEST) + b_ref[...]


def _ada(c_pad, w_ada, b_ada):
    n = w_ada.shape[1]
    bn = 1536
    return pl.pallas_call(
        _ada_kernel,
        grid=(n // bn,),
        in_specs=[pl.BlockSpec((8, D_MODEL), lambda j: (0, 0)),
                  pl.BlockSpec((D_MODEL, bn), lambda j: (0, j)),
                  pl.BlockSpec((1, bn), lambda j: (0, j))],
        out_specs=pl.BlockSpec((8, bn), lambda j: (0, j)),
        out_shape=jax.ShapeDtypeStruct((8, n), F32),
        compiler_params=_cparams(("arbitrary",)),
        name="ada_mod",
    )(c_pad, w_ada, b_ada)


GROUPS_PER_LANE_TILE = LANES // SSM_GROUP_CH


def _to_group_chunks(u, tile_ref, ut_ref):
    n_chunks = u.shape[0] // S5_Q
    for j in range(SSM_CH // LANES):
        tile_ref[j] = u[:, LANES * j:LANES * (j + 1)]
    for j in range(SSM_CH // LANES):
        rows_t = [tile_ref[j, pl.ds(t, n_chunks, stride=S5_Q), :] for t in range(S5_Q)]
        for gg in range(GROUPS_PER_LANE_TILE):
            lo = gg * SSM_GROUP_CH
            row = jnp.concatenate([r[:, lo:lo + SSM_GROUP_CH] for r in rows_t], axis=1)
            ut_ref[j * GROUPS_PER_LANE_TILE + gg] = row.astype(ut_ref.dtype)


def _from_group_chunks(yt_ref, tile_ref):
    n_chunks = yt_ref.shape[1]
    for j in range(SSM_CH // LANES):
        for t in range(S5_Q):
            lo = t * SSM_GROUP_CH
            piece = jnp.concatenate(
                [yt_ref[j * GROUPS_PER_LANE_TILE + gg, :, lo:lo + SSM_GROUP_CH]
                 for gg in range(GROUPS_PER_LANE_TILE)], axis=1)
            tile_ref[j, pl.ds(t, n_chunks, stride=S5_Q), :] = piece
    return jnp.concatenate([tile_ref[j] for j in range(SSM_CH // LANES)], axis=1)


def _inproj_kernel(x_ref, mod_ref, g_ref, w_ref, v_ref, u_ref, ut_ref, tile_ref):
    x = x_ref[...]
    sh = mod_ref[0, 0:1, :]
    sc = mod_ref[0, 1:2, :]
    h = _rms(x, g_ref[...]) * (1.0 + sc) + sh
    z = jnp.dot(h.astype(BF16), w_ref[...], preferred_element_type=F32)
    v_ref[...] = z[:, :CONV_CH] * jax.nn.sigmoid(z[:, CONV_CH:2 * CONV_CH])
    u = z[:, 2 * CONV_CH:]
    u_ref[...] = u
    _to_group_chunks(u, tile_ref, ut_ref)


def _inproj(x2, mod, g_pre, w_in_bf):
    tiles_per_seq = SEQ // TM_IN
    return pl.pallas_call(
        _inproj_kernel,
        grid=(N_TOK // TM_IN,),
        in_specs=[pl.BlockSpec((TM_IN, D_MODEL), lambda i: (i, 0)),
                  pl.BlockSpec((1, 8, D_MODEL), lambda i: (i // tiles_per_seq, 0, 0)),
                  pl.BlockSpec((1, D_MODEL), lambda i: (0, 0)),
                  pl.BlockSpec((D_MODEL, D_IN), lambda i: (0, 0))],
        out_specs=[pl.BlockSpec((TM_IN, CONV_CH), lambda i: (i, 0)),
                   pl.BlockSpec((TM_IN, SSM_CH), lambda i: (i, 0)),
                   pl.BlockSpec((SSM_GROUPS, TM_IN // S5_Q, S5_QH), lambda i: (0, i, 0))],
        out_shape=[jax.ShapeDtypeStruct((N_TOK, CONV_CH), F32),
                   jax.ShapeDtypeStruct((N_TOK, SSM_CH), F32),
                   jax.ShapeDtypeStruct((SSM_GROUPS, S5_CHUNKS, S5_QH), BF16)],
        scratch_shapes=[pltpu.VMEM((SSM_CH // LANES, TM_IN, LANES), F32)],
        compiler_params=_cparams(("parallel",)),
        name="in_proj",
    )(x2, mod, g_pre, w_in_bf)


def _conv_kernel(vc_ref, vp_ref, w_ref, cb_ref, lg_ref, lb_ref, go_ref, o_ref, sh_ref):
    i = pl.program_id(1)
    keep = (i > 0).astype(F32)
    n_ext = TL_CONV + CONV_HALO
    sh_ref[0, 0:CONV_HALO, :] = vp_ref[0] * keep
    sh_ref[0, CONV_HALO:, :] = vc_ref[0]
    for s in range(1, SUBLANES):
        sh_ref[s, 0:n_ext - s, :] = sh_ref[0, s:n_ext, :]
    off = CONV_HALO - (CONV_WIDTH - 1)
    for r in range(TL_CONV // CONV_ROWS):
        acc = None
        for j in range(CONV_WIDTH):
            s = (off + j) % SUBLANES
            al = r * CONV_ROWS + (off + j) - s
            term = w_ref[j:j + 1, :] * sh_ref[s, al:al + CONV_ROWS, :]
            acc = term if acc is None else acc + term
        y = acc + cb_ref[...]
        mu = jnp.mean(y, axis=-1, keepdims=True)
        d = y - mu
        var = jnp.mean(d * d, axis=-1, keepdims=True)
        yn = d * lax.rsqrt(var + NORM_EPS) * lg_ref[...] + lb_ref[...]
        a = yn * jax.nn.sigmoid(yn)
        o_ref[0, r * CONV_ROWS:(r + 1) * CONV_ROWS, :] = _rms(a, go_ref[...]).astype(BF16)


def _conv(v3, conv_w, conv_b, ln_g, ln_b, g_out):
    halo_per_tile = TL_CONV // CONV_HALO
    vec = pl.BlockSpec((1, CONV_CH), lambda b, i: (0, 0))
    return pl.pallas_call(
        _conv_kernel,
        grid=(BATCH, SEQ // TL_CONV),
        in_specs=[pl.BlockSpec((1, TL_CONV, CONV_CH), lambda b, i: (b, i, 0)),
                  pl.BlockSpec((1, CONV_HALO, CONV_CH),
                               lambda b, i: (b, jnp.maximum(i * halo_per_tile - 1, 0), 0)),
                  pl.BlockSpec((CONV_WIDTH + 1, CONV_CH), lambda b, i: (0, 0)),
                  vec, vec, vec, vec],
        out_specs=pl.BlockSpec((1, TL_CONV, CONV_CH), lambda b, i: (b, i, 0)),
        out_shape=jax.ShapeDtypeStruct((BATCH, SEQ, CONV_CH), BF16),
        scratch_shapes=[pltpu.VMEM((SUBLANES, TL_CONV + CONV_HALO, CONV_CH), F32)],
        compiler_params=_cparams(("parallel", "arbitrary")),
        name="conv_module",
    )(v3, v3, conv_w, conv_b, ln_g, ln_b, g_out)


S5_GROUP_ROWS = S5_CHUNKS + 8


S5_POW_ROWS = (S5_Q + 1 + SUBLANES - 1) // SUBLANES * SUBLANES
(S5_BB_RI, S5_BB_NIR, S5_BB_IR, S5_BB_RNI, S5_CC_RI, S5_CC_NIR, S5_N_PARAM) = range(7)


def _s5_kernel(ut_ref, pwr_ref, pwi_ref, par_ref, a_ref, bq_ref, bs_ref, yt_ref, sin_s, sp_s):
    phase = pl.program_id(0)
    g = pl.program_id(1)
    q = S5_Q
    n = 2 * SSM_STATE
    row0 = pl.multiple_of(g * S5_GROUP_ROWS, 8)

    def lam_pow(j):
        return pwr_ref[0, j:j + 1, :], pwi_ref[0, j:j + 1, :]

    @pl.when(phase == 0)
    def _():
        bb_ri, bb_nir = par_ref[0, S5_BB_RI], par_ref[0, S5_BB_NIR]
        bb_ir, bb_rni = par_ref[0, S5_BB_IR], par_ref[0, S5_BB_RNI]
        blk_q, blk_s = [], []
        for t in range(q):
            pr, pi_ = lam_pow(q - 1 - t)
            blk_q.append(pr * bb_ri + pi_ * bb_nir)
            blk_s.append(pr * bb_ir + pi_ * bb_rni)
        wst = jnp.concatenate([jnp.concatenate(blk_q, axis=0), jnp.concatenate(blk_s, axis=0)], axis=1)
        r = jnp.dot(ut_ref[0], wst.astype(BF16), preferred_element_type=F32)
        sin_s[0, pl.ds(row0, S5_CHUNKS), :] = r[:, :n]
        sin_s[1, pl.ds(row0, S5_CHUNKS), :] = r[:, n:]

    @pl.when((phase == 1) & (g == 0))
    def _():
        a = a_ref[...]
        bq = bq_ref[...]
        bs = bs_ref[...]

        def body(c, carry):
            nxt = []
            for b in range(BATCH):
                x, xs = carry[b]
                rows = pl.ds(b * S5_CHUNKS_PER_SEQ + c, SSM_GROUPS, stride=S5_GROUP_ROWS)
                sp_s[rows, :] = x
                nxt.append((a * x + bq * xs + sin_s[0, rows, :], a * xs + bs * x + sin_s[1, rows, :]))
            return tuple(nxt)

        z = jnp.zeros((SSM_GROUPS, n), F32)
        lax.fori_loop(0, S5_CHUNKS_PER_SEQ, body, tuple((z, z) for _ in range(BATCH)))

    @pl.when(phase == 1)
    def _():
        cc_ri, cc_nir = par_ref[0, S5_CC_RI], par_ref[0, S5_CC_NIR]
        cl = []
        for j in range(q + 1):
            pr, pi_ = lam_pow(j)
            cl.append(pr * cc_ri + pi_ * cc_nir)
        cl_lo = jnp.concatenate(cl[:q], axis=0)
        cl_hi = jnp.concatenate(cl[1:], axis=0)
        lane = lax.broadcasted_iota(jnp.int32, (1, n), 1)
        vgt = (cl_hi * jnp.where(lane < SSM_STATE, 1.0, -1.0)).astype(BF16)
        kt = lax.dot_general(par_ref[0, S5_BB_RNI], cl_lo, (((1,), (1,)), ((), ())),
                             preferred_element_type=F32, precision=lax.Precision.HIGHEST)
        padded = jnp.concatenate([jnp.zeros_like(kt), kt], axis=1)
        tg = jnp.concatenate(
            [padded[:, (q - t) * SSM_GROUP_CH:(q - t) * SSM_GROUP_CH + S5_QH] for t in range(q)],
            axis=0).astype(BF16)
        sp = sp_s[pl.ds(row0, S5_CHUNKS), :]
        y = jnp.dot(ut_ref[0], tg, preferred_element_type=F32)
        yt_ref[0] = y + lax.dot_general(sp.astype(BF16), vgt, (((1,), (1,)), ((), ())),
                                        preferred_element_type=F32)


def _s5(ut, pwr, pwi, params, a_cat, b_q, b_s):
    vec = pl.BlockSpec((SSM_GROUPS, 2 * SSM_STATE), lambda p, g: (0, 0))
    powers = pl.BlockSpec((1, S5_POW_ROWS, 2 * SSM_STATE), lambda p, g: (g, 0, 0))
    return pl.pallas_call(
        _s5_kernel,
        grid=(2, SSM_GROUPS),
        in_specs=[pl.BlockSpec((1, S5_CHUNKS, S5_QH), lambda p, g: (g, 0, 0)),
                  powers, powers,
                  pl.BlockSpec((1, S5_N_PARAM, SSM_GROUP_CH, 2 * SSM_STATE), lambda p, g: (g, 0, 0, 0)),
                  vec, vec, vec],
        out_specs=pl.BlockSpec((1, S5_CHUNKS, S5_QH), lambda p, g: (g * p, 0, 0)),
        out_shape=jax.ShapeDtypeStruct((SSM_GROUPS, S5_CHUNKS, S5_QH), F32),
        scratch_shapes=[pltpu.VMEM((2, SSM_GROUPS * S5_GROUP_ROWS, 2 * SSM_STATE), F32),
                        pltpu.VMEM((SSM_GROUPS * S5_GROUP_ROWS, 2 * SSM_STATE), F32)],
        compiler_params=_cparams(("arbitrary", "arbitrary")),
        name="s5_chunked",
    )(ut, pwr, pwi, params, a_cat, b_q, b_s)


def _s5_operators(a_re, a_im, log_dt, b_re, b_im, c_re, c_im):
    q = S5_Q
    dt = jnp.exp(log_dt)[:, None]
    ar, ai = a_re, a_im
    mag = jnp.exp(ar * dt)
    lr = mag * jnp.cos(ai * dt)
    li = mag * jnp.sin(ai * dt)
    den = ar * ar + ai * ai
    nr = lr - 1.0
    kr = (nr * ar + li * ai) / den
    ki = (li * ar - nr * ai) / den
    bbr = kr[..., None] * b_re - ki[..., None] * b_im
    bbi = kr[..., None] * b_im + ki[..., None] * b_re
    j = jnp.arange(q + 1, dtype=F32)[None, :, None]
    pmag = jnp.exp(ar[:, None, :] * dt[:, :, None] * j)
    pang = ai[:, None, :] * dt[:, :, None] * j
    pr = pmag * jnp.cos(pang)
    pi_ = pmag * jnp.sin(pang)
    pad = ((0, 0), (0, S5_POW_ROWS - (q + 1)), (0, 0))
    pwr = jnp.pad(jnp.concatenate([pr, pr], axis=-1), pad)
    pwi = jnp.pad(jnp.concatenate([pi_, pi_], axis=-1), pad)
    br_t = bbr.transpose(0, 2, 1)
    bi_t = bbi.transpose(0, 2, 1)
    cat = lambda a, b: jnp.concatenate([a, b], axis=-1)
    stack = [None] * S5_N_PARAM
    stack[S5_BB_RI] = cat(br_t, bi_t)
    stack[S5_BB_NIR] = cat(-bi_t, br_t)
    stack[S5_BB_IR] = cat(bi_t, br_t)
    stack[S5_BB_RNI] = cat(br_t, -bi_t)
    stack[S5_CC_RI] = cat(c_re, c_im)
    stack[S5_CC_NIR] = cat(-c_im, c_re)
    params = jnp.stack(stack, axis=1)
    aq_r, aq_i = pr[:, q], pi_[:, q]
    a_cat = cat(aq_r, aq_r)
    b_q = cat(-aq_i, aq_i)
    b_s = cat(aq_i, -aq_i)
    return pwr, pwi, params, a_cat, b_q, b_s


def _gelu_tanh(x):
    return 0.5 * x * (1.0 + jnp.tanh(math.sqrt(2.0 / math.pi) * (x + 0.044715 * (x * x * x))))


def _mix_kernel(an_ref, yt_ref, u_ref, x_ref, mod_ref, d_ref, wglu_ref, bglu_ref, gos_ref,
                woa_ref, wob_ref, gpm_ref, gpf_ref, wr_ref, br_ref, tri_ref,
                x1_ref, h2_ref, eidx_ref, rank_ref, gw_ref, cnt_ref, run_ref, tile_ref):
    i = pl.program_id(0)
    tm = TM_MIX

    @pl.when(i == 0)
    def _():
        run_ref[...] = jnp.zeros_like(run_ref)

    gt_m = mod_ref[0, 2:3, :]
    sh_f = mod_ref[0, 3:4, :]
    sc_f = mod_ref[0, 4:5, :]

    yy = _from_group_chunks(yt_ref, tile_ref) + d_ref[...] * u_ref[...]
    g = _gelu_tanh(yy)
    gl = jnp.dot(g.astype(BF16), wglu_ref[...], preferred_element_type=F32) + bglu_ref[...]
    ob = g * jax.nn.sigmoid(gl)
    bn = _rms(ob, gos_ref[...]).astype(BF16)
    o = (jnp.dot(an_ref[...], woa_ref[...], preferred_element_type=F32)
         + jnp.dot(bn, wob_ref[...], preferred_element_type=F32))
    x1 = x_ref[...] + gt_m * _rms(o, gpm_ref[...])
    x1_ref[...] = x1
    h2 = _rms(x1, gpf_ref[...]) * (1.0 + sc_f) + sh_f
    h2_ref[...] = _pack_halves(h2)

    logits = lax.dot_general(wr_ref[...], h2, (((1,), (1,)), ((), ())),
                             preferred_element_type=F32, precision=lax.Precision.HIGHEST)
    scores = jax.nn.sigmoid(logits)
    biased = scores + br_ref[...]
    ng = N_ROUTE_GROUPS
    gsz = N_EXPERTS // ng
    b3 = biased.reshape(ng, gsz, tm)
    s3 = scores.reshape(ng, gsz, tm)
    sub = lax.broadcasted_iota(jnp.int32, (ng, gsz, tm), 1).astype(F32)
    grp = lax.broadcasted_iota(jnp.int32, (ng, gsz, tm), 0).astype(F32)
    eid = grp * gsz + sub
    neg = -jnp.inf
    m1 = jnp.max(b3, axis=1, keepdims=True)
    i1 = jnp.min(jnp.where(b3 == m1, sub, float(gsz)), axis=1, keepdims=True)
    m2 = jnp.max(jnp.where(sub == i1, neg, b3), axis=1, keepdims=True)
    gs = m1 + m2
    gi = lax.broadcasted_iota(jnp.int32, (ng, 1, tm), 0)
    beaten = jnp.zeros((ng, 1, tm), F32)
    for gp in range(ng):
        o_ = gs[gp:gp + 1]
        beats = (o_ > gs) | ((o_ == gs) & (gi > gp))
        beaten = beaten + beats.astype(F32)
    gmask = beaten < float(TOPK_ROUTE_GROUPS)
    masked = jnp.where(gmask, b3, neg)

    sels = []
    picked = jnp.zeros((ng, gsz, tm), F32)
    for k in range(TOP_K):
        m = jnp.max(jnp.max(masked, axis=0, keepdims=True), axis=1, keepdims=True)
        cand = jnp.where(masked == m, eid, float(N_EXPERTS))
        sel = jnp.min(jnp.min(cand, axis=0, keepdims=True), axis=1, keepdims=True)
        oh = eid == sel
        masked = jnp.where(oh, neg, masked)
        picked = jnp.where(oh, 1.0, picked)
        sels.append(sel)

    pm = picked.reshape(N_EXPERTS, tm)
    prefix = jnp.dot(pm.astype(BF16), tri_ref[...], preferred_element_type=F32) + run_ref[:, 0:1]
    p3 = prefix.reshape(ng, gsz, tm)
    run_new = run_ref[...] + jnp.sum(pm, axis=1, keepdims=True)
    run_ref[...] = run_new
    cnt_ref[...] = run_new

    sc_rows = []
    for k in range(TOP_K):
        oh = eid == sels[k]
        sc_k = jnp.sum(jnp.sum(jnp.where(oh, s3, 0.0), axis=0, keepdims=True), axis=1, keepdims=True)
        rk_k = jnp.sum(jnp.sum(jnp.where(oh, p3, 0.0), axis=0, keepdims=True), axis=1, keepdims=True)
        sc_rows.append(sc_k)
        eidx_ref[k:k + 1, :] = sels[k].reshape(1, tm).astype(jnp.int32)
        rank_ref[k:k + 1, :] = rk_k.reshape(1, tm).astype(jnp.int32)
    tot = sc_rows[0]
    for k in range(1, TOP_K):
        tot = tot + sc_rows[k]
    inv = ROUTED_SCALE / (tot + 1e-20)
    for k in range(TOP_K):
        gw_ref[k:k + 1, :] = (sc_rows[k] * inv).reshape(1, tm)


def _mix(half, a_n, yt, u2, x2, mod, d_skip, wglu_bf, b_glu, g_out_ssm, wo_a, wo_b, g_post_mix,
         g_pre_ffn, w_router_t, b_router_col, tri):
    tm = TM_MIX
    t0 = half * (HALF_TOK // tm)
    row = lambda n: pl.BlockSpec((1, n), lambda i: (0, 0))
    full = lambda a, b: pl.BlockSpec((a, b), lambda i: (0, 0))
    tok_in = lambda n: pl.BlockSpec((tm, n), lambda i: (t0 + i, 0))
    tok = lambda n: pl.BlockSpec((tm, n), lambda i: (i, 0))
    col = pl.BlockSpec((TOP_K, tm), lambda i: (0, i))
    return pl.pallas_call(
        _mix_kernel,
        grid=(HALF_TOK // tm,),
        in_specs=[tok_in(CONV_CH),
                  pl.BlockSpec((SSM_GROUPS, tm // S5_Q, S5_QH), lambda i: (0, t0 + i, 0)),
                  tok_in(SSM_CH), tok_in(D_MODEL),
                  pl.BlockSpec((1, 8, D_MODEL), lambda i: (half, 0, 0)),
                  row(SSM_CH), full(SSM_CH, SSM_CH), row(SSM_CH), row(SSM_CH),
                  full(CONV_CH, D_MODEL), full(SSM_CH, D_MODEL), row(D_MODEL), row(D_MODEL),
                  full(N_EXPERTS, D_MODEL), full(N_EXPERTS, 1), full(tm, tm)],
        out_specs=[tok(D_MODEL), tok(D_MODEL // 2), col, col, col,
                   pl.BlockSpec((N_EXPERTS, 128), lambda i: (0, 0))],
        out_shape=[jax.ShapeDtypeStruct((HALF_TOK, D_MODEL), F32),
                   jax.ShapeDtypeStruct((HALF_TOK, D_MODEL // 2), jnp.uint32),
                   jax.ShapeDtypeStruct((TOP_K, HALF_TOK), jnp.int32),
                   jax.ShapeDtypeStruct((TOP_K, HALF_TOK), jnp.int32),
                   jax.ShapeDtypeStruct((TOP_K, HALF_TOK), F32),
                   jax.ShapeDtypeStruct((N_EXPERTS, 128), F32)],
        scratch_shapes=[pltpu.VMEM((N_EXPERTS, 128), F32),
                        pltpu.VMEM((SSM_CH // LANES, tm, LANES), F32)],
        compiler_params=_cparams(("arbitrary",)),
        name="mix_out_router",
    )(a_n, yt, u2, x2, mod, d_skip, wglu_bf, b_glu, g_out_ssm, wo_a, wo_b, g_post_mix,
      g_pre_ffn, w_router_t, b_router_col, tri)


def _expert_kernel(blk0_ref, nblk_ref, xs_hbm, wg_ref, wu_ref, wd_ref, ys_hbm,
                   xbuf, ybuf, wgu_s, wd_s, sem_in, sem_out):
    e = pl.program_id(0)
    n = nblk_ref[e]
    b0 = blk0_ref[e]

    def rows(b):
        return pl.ds(pl.multiple_of((b0 + b) * ROW_BLOCK, ROW_BLOCK), ROW_BLOCK)

    def in_copy(b, slot):
        return pltpu.make_async_copy(xs_hbm.at[rows(b)], xbuf.at[slot], sem_in.at[slot])

    def out_copy(b, slot):
        return pltpu.make_async_copy(ybuf.at[slot], ys_hbm.at[rows(b)], sem_out.at[slot])

    for b in range(EXPERT_IN_SLOTS - 1):
        @pl.when(b < n)
        def _():
            in_copy(b, b).start()

    wgu_s[:, :D_EXPERT] = wg_ref[0].astype(BF16)
    wgu_s[:, D_EXPERT:] = wu_ref[0].astype(BF16)
    wd_s[...] = wd_ref[0].astype(BF16)

    def body(b, carry):
        slot = b % 2
        in_slot = b % EXPERT_IN_SLOTS
        in_copy(b, in_slot).wait()
        ahead = b + EXPERT_IN_SLOTS - 1

        @pl.when(ahead < n)
        def _():
            in_copy(ahead, ahead % EXPERT_IN_SLOTS).start()

        @pl.when(b >= 2)
        def _():
            out_copy(b - 2, slot).wait()

        x_lo, x_hi = _unpack_halves(xbuf[in_slot])
        x = jnp.concatenate([x_lo.astype(BF16), x_hi.astype(BF16)], axis=1)
        h = jnp.dot(x, wgu_s[...], preferred_element_type=F32)
        hg = h[:, :D_EXPERT]
        act = hg * jax.nn.sigmoid(hg) * h[:, D_EXPERT:]
        ybuf[slot] = _pack_halves(jnp.dot(act.astype(BF16), wd_s[...], preferred_element_type=F32))
        out_copy(b, slot).start(priority=1)
        return carry

    lax.fori_loop(0, n, body, 0)

    @pl.when(n >= 2)
    def _():
        out_copy(n - 2, n % 2).wait()

    @pl.when(n >= 1)
    def _():
        out_copy(n - 1, (n - 1) % 2).wait()


def _experts(blk0, nblk, xs, we_gate, we_up, we_down):
    any_spec = pl.BlockSpec(memory_space=pl.ANY)
    grid_spec = pltpu.PrefetchScalarGridSpec(
        num_scalar_prefetch=2,
        grid=(N_EXPERTS,),
        in_specs=[any_spec,
                  pl.BlockSpec((1, D_MODEL, D_EXPERT), lambda e, b0, nb: (e, 0, 0)),
                  pl.BlockSpec((1, D_MODEL, D_EXPERT), lambda e, b0, nb: (e, 0, 0)),
                  pl.BlockSpec((1, D_EXPERT, D_MODEL), lambda e, b0, nb: (e, 0, 0))],
        out_specs=any_spec,
        scratch_shapes=[pltpu.VMEM((EXPERT_IN_SLOTS, ROW_BLOCK, D_MODEL // 2), jnp.uint32),
                        pltpu.VMEM((2, ROW_BLOCK, D_MODEL // 2), jnp.uint32),
                        pltpu.VMEM((D_MODEL, 2 * D_EXPERT), BF16),
                        pltpu.VMEM((D_EXPERT, D_MODEL), BF16),
                        pltpu.SemaphoreType.DMA((EXPERT_IN_SLOTS,)),
                        pltpu.SemaphoreType.DMA((2,))],
    )
    return pl.pallas_call(
        _expert_kernel,
        grid_spec=grid_spec,
        out_shape=jax.ShapeDtypeStruct((N_ROWS, D_MODEL // 2), jnp.uint32),
        compiler_params=_cparams(("arbitrary",)),
        name="routed_experts",
    )(blk0, nblk, xs, we_gate, we_up, we_down)


def _final_kernel(h2_ref, yg_ref, gw_ref, x1_ref, mod_ref, wgu_ref, wd_ref, g_ref, *rest):
    o_ref = rest[-1]
    half = D_MODEL // 2
    gt_f = mod_ref[0, 5:6, :]
    x_lo, x_hi = _unpack_halves(h2_ref[...])
    h = (jnp.dot(x_lo.astype(BF16), wgu_ref[:half, :], preferred_element_type=F32)
         + jnp.dot(x_hi.astype(BF16), wgu_ref[half:, :], preferred_element_type=F32))
    hg = h[:, :D_EXPERT]
    act = hg * jax.nn.sigmoid(hg) * h[:, D_EXPERT:]
    shared = jnp.dot(act.astype(BF16), wd_ref[...], preferred_element_type=F32)
    y_lo = shared[:, :half]
    y_hi = shared[:, half:]
    for k in range(TOP_K):
        r_lo, r_hi = _unpack_halves(yg_ref[k])
        w = gw_ref[:, k:k + 1]
        y_lo = y_lo + w * r_lo
        y_hi = y_hi + w * r_hi
    ms = (jnp.sum(y_lo * y_lo, axis=-1, keepdims=True)
          + jnp.sum(y_hi * y_hi, axis=-1, keepdims=True)) * (1.0 / D_MODEL)
    inv = lax.rsqrt(ms + NORM_EPS)
    o_ref[:, :half] = x1_ref[:, :half] + gt_f[:, :half] * (y_lo * inv * g_ref[:, :half])
    o_ref[:, half:] = x1_ref[:, half:] + gt_f[:, half:] * (y_hi * inv * g_ref[:, half:])


def _final(half, out_prev, h2p, yg, gw_t, x1, mod, ws_gu, ws_d, g_post_ffn):
    tm = TM_OUT
    t0 = half * (HALF_TOK // tm)
    tok = pl.BlockSpec((tm, D_MODEL), lambda i: (i, 0))
    in_specs = [pl.BlockSpec((tm, D_MODEL // 2), lambda i: (i, 0)),
                pl.BlockSpec((TOP_K, tm, D_MODEL // 2), lambda i: (0, i, 0)),
                pl.BlockSpec((tm, TOP_K), lambda i: (i, 0)),
                tok,
                pl.BlockSpec((1, 8, D_MODEL), lambda i: (half, 0, 0)),
                pl.BlockSpec((D_MODEL, 2 * D_EXPERT), lambda i: (0, 0)),
                pl.BlockSpec((D_EXPERT, D_MODEL), lambda i: (0, 0)),
                pl.BlockSpec((1, D_MODEL), lambda i: (0, 0))]
    args = [h2p, yg, gw_t, x1, mod, ws_gu, ws_d, g_post_ffn]
    aliases = {}
    if out_prev is not None:
        aliases = {len(args): 0}
        in_specs.append(pl.BlockSpec(memory_space=pl.ANY))
        args.append(out_prev)
    return pl.pallas_call(
        _final_kernel,
        grid=(HALF_TOK // tm,),
        in_specs=in_specs,
        out_specs=pl.BlockSpec((tm, D_MODEL), lambda i: (t0 + i, 0)),
        out_shape=jax.ShapeDtypeStruct((N_TOK, D_MODEL), F32),
        input_output_aliases=aliases,
        compiler_params=_cparams(("parallel",)),
        name="shared_final",
    )(*args)


def _sc_worker_id():
    return lax.axis_index("s") * SC_CORES + lax.axis_index("c")


def _dispatch_body(h_hbm, dest_hbm, xs_hbm, idx_v, rows_v, sem_l, sem_s):
    n = SC_CHUNKS_PER_WORKER
    c0 = _sc_worker_id() * n

    def load(i, b):
        return pltpu.async_copy(h_hbm.at[pl.ds((c0 + i) * SC_W, SC_W)], rows_v.at[b], sem_l.at[b])

    loads = [None] * n
    scat = [None] * n
    loads[0] = load(0, 0)
    for i in range(n):
        b = i % 2
        pltpu.sync_copy(dest_hbm.at[c0 + i], idx_v.at[b])
        loads[i].wait()
        if i + 1 < n:
            if i >= 1:
                for d in scat[i - 1]:
                    d.wait()
            loads[i + 1] = load(i + 1, 1 - b)
        scat[i] = [pltpu.async_copy(rows_v.at[b], xs_hbm.at[idx_v.at[b].at[k]], sem_s.at[b])
                   for k in range(TOP_K)]
    for i in (n - 2, n - 1):
        for d in scat[i]:
            d.wait()


def _sc_dispatch(h2p, dest3):
    mesh = plsc.VectorSubcoreMesh(core_axis_name="c", subcore_axis_name="s")
    return pl.kernel(
        _dispatch_body, mesh=mesh,
        out_type=jax.ShapeDtypeStruct((N_ROWS, D_MODEL // 2), jnp.uint32),
        scratch_types=[pltpu.VMEM((2, TOP_K, SC_W), jnp.int32),
                       pltpu.VMEM((2, SC_W, D_MODEL // 2), jnp.uint32),
                       pltpu.SemaphoreType.DMA((2,)), pltpu.SemaphoreType.DMA((2,))],
    )(h2p, dest3)


def _combine_body(ys_hbm, dest_hbm, yg_hbm, idx_v, rows_v, sem_g, sem_w):
    c0 = _sc_worker_id() * SC_CHUNKS_PER_WORKER

    @pl.loop(0, SC_CHUNKS_PER_WORKER)
    def _(i):
        c = c0 + i
        pltpu.sync_copy(dest_hbm.at[c], idx_v)
        g = [None] * TOP_K
        w = [None] * TOP_K
        g[0] = pltpu.async_copy(ys_hbm.at[idx_v.at[0]], rows_v.at[0], sem_g.at[0])
        for k in range(TOP_K):
            b = k % 2
            g[k].wait()
            if k + 1 < TOP_K:
                if k >= 1:
                    w[k - 1].wait()
                g[k + 1] = pltpu.async_copy(ys_hbm.at[idx_v.at[k + 1]], rows_v.at[1 - b], sem_g.at[1 - b])
            w[k] = pltpu.async_copy(rows_v.at[b], yg_hbm.at[k].at[pl.ds(c * SC_W, SC_W)], sem_w.at[b])
        w[TOP_K - 2].wait()
        w[TOP_K - 1].wait()


def _sc_combine(ysp, dest3):
    mesh = plsc.VectorSubcoreMesh(core_axis_name="c", subcore_axis_name="s")
    return pl.kernel(
        _combine_body, mesh=mesh,
        out_type=jax.ShapeDtypeStruct((TOP_K, HALF_TOK, D_MODEL // 2), jnp.uint32),
        scratch_types=[pltpu.VMEM((TOP_K, SC_W), jnp.int32),
                       pltpu.VMEM((2, SC_W, D_MODEL // 2), jnp.uint32),
                       pltpu.SemaphoreType.DMA((2,)), pltpu.SemaphoreType.DMA((2,))],
    )(ysp, dest3)


def kernel(x, c, w_ada, b_ada, g_pre_mix, g_post_mix, w_in, conv_w, conv_b, conv_ln_g, conv_ln_b,
           ssm_a_re, ssm_a_im, ssm_log_dt, ssm_b_re, ssm_b_im, ssm_c_re, ssm_c_im, ssm_d,
           ssm_w_glu, ssm_b_glu, g_out_conv, g_out_ssm, w_out, g_pre_ffn, g_post_ffn,
           w_router, b_router, we_gate, we_up, we_down, ws_gate, ws_up, ws_down):
    l = 0
    x2 = x.reshape(N_TOK, D_MODEL)
    r1 = lambda a: a.reshape(1, -1)

    c_pad = jnp.zeros((8, D_MODEL), F32).at[:BATCH].set(c)
    mod = _ada(c_pad, w_ada[l], r1(b_ada[l]))[:BATCH].reshape(BATCH, 6, D_MODEL)
    mod = jnp.concatenate([mod, jnp.zeros((BATCH, 2, D_MODEL), F32)], axis=1)

    v, u, ut = _inproj(x2, mod, r1(g_pre_mix[l]), w_in[l].astype(BF16))
    cw = jnp.concatenate([conv_w[l].reshape(CONV_WIDTH, CONV_CH), jnp.zeros((1, CONV_CH), F32)], axis=0)
    a_n = _conv(v.reshape(BATCH, SEQ, CONV_CH), cw, r1(conv_b[l]), r1(conv_ln_g[l]),
                r1(conv_ln_b[l]), r1(g_out_conv[l])).reshape(N_TOK, CONV_CH)

    pwr, pwi, s5_params, a_cat, b_q, b_s = _s5_operators(
        ssm_a_re[l], ssm_a_im[l], ssm_log_dt[l], ssm_b_re[l], ssm_b_im[l], ssm_c_re[l], ssm_c_im[l])
    yt = _s5(ut, pwr, pwi, s5_params, a_cat, b_q, b_s)

    tm = TM_MIX
    tri = (jnp.arange(tm)[:, None] < jnp.arange(tm)[None, :]).astype(BF16)
    wo = w_out[l].astype(BF16)
    mix_params = (r1(ssm_d[l]), ssm_w_glu[l].astype(BF16), r1(ssm_b_glu[l]), r1(g_out_ssm[l]),
                  wo[:CONV_CH], wo[CONV_CH:], r1(g_post_mix[l]), r1(g_pre_ffn[l]),
                  w_router[l].T, b_router[l].reshape(N_EXPERTS, 1), tri)
    ws_gu = jnp.concatenate([ws_gate[l], ws_up[l]], axis=1).astype(BF16)
    ws_d = ws_down[l].astype(BF16)
    e_ids = jnp.arange(N_EXPERTS, dtype=jnp.int32)

    out = None
    for half in range(N_HALVES):
        x1, h2, eidx, rank, gw, cnt = _mix(half, a_n, yt, u, x2, mod, *mix_params)
        counts = cnt[:, 0].astype(jnp.int32)
        padded = (counts + ROW_BLOCK - 1) // ROW_BLOCK * ROW_BLOCK
        pstart = jnp.cumsum(padded) - padded
        dest = rank + jnp.sum(jnp.where(eidx[..., None] == e_ids, pstart, 0), axis=-1)
        dest3 = dest.reshape(TOP_K, HALF_TOK // SC_W, SC_W).transpose(1, 0, 2)

        xs = _sc_dispatch(h2, dest3)
        ys = _experts(pstart // ROW_BLOCK, padded // ROW_BLOCK, xs, we_gate[l], we_up[l], we_down[l])
        yg = _sc_combine(ys, dest3)
        out = _final(half, out, h2, yg, gw.T, x1, mod, ws_gu, ws_d, r1(g_post_ffn[l]))
    return out.reshape(BATCH, SEQ, D_MODEL)
```

```python
import functools
import math

import jax
import jax.numpy as jnp
from jax import lax
from jax.experimental import pallas as pl
from jax.experimental.pallas import tpu as pltpu
from jax.experimental.pallas import tpu_sc as plsc

F32 = jnp.float32
BF16 = jnp.bfloat16

D_MODEL = 1024
BATCH = 2
SEQ = 8192
N_TOK = BATCH * SEQ
CONV_CH = 512
CONV_WIDTH = 31
SSM_CH = 512
SSM_GROUP_CH = 16
SSM_GROUPS = 32
SSM_STATE = 64
D_IN = 2 * CONV_CH + SSM_CH
N_EXPERTS = 64
TOP_K = 8
N_ROUTE_GROUPS = 8
TOPK_ROUTE_GROUPS = 4
D_EXPERT = 256
ROUTED_SCALE = 2.5
NORM_EPS = 1e-6

SUBLANES = 8
LANES = 128

TM_IN = 512
TL_CONV = 512
CONV_HALO = 32
CONV_ROWS = 64
S5_Q = 32
S5_QH = S5_Q * SSM_GROUP_CH
S5_CHUNKS = N_TOK // S5_Q
S5_CHUNKS_PER_SEQ = SEQ // S5_Q
TM_MIX = 512
ROW_BLOCK = 512
EXPERT_IN_SLOTS = 3
HALF_TOK = SEQ
N_HALVES = N_TOK // HALF_TOK
N_BLOCKS = HALF_TOK * TOP_K // ROW_BLOCK + N_EXPERTS
N_ROWS = N_BLOCKS * ROW_BLOCK
TM_OUT = 512
SC_CORES = 2
SC_SUBCORES = 16
SC_WORKERS = SC_CORES * SC_SUBCORES
SC_W = 64
SC_CHUNKS_PER_WORKER = HALF_TOK // (SC_WORKERS * SC_W)
VMEM_LIMIT = 48 * 1024 * 1024


def _cparams(sem):
    return pltpu.CompilerParams(dimension_semantics=sem, vmem_limit_bytes=VMEM_LIMIT)


def _pack_halves(x):
    n = x.shape[-1] // 2
    lo = lax.bitcast_convert_type(x[:, :n].astype(BF16).astype(F32), jnp.uint32)
    hi = lax.bitcast_convert_type(x[:, n:].astype(BF16).astype(F32), jnp.uint32)
    return hi | (lo >> 16)


def _unpack_halves(p):
    lo = lax.bitcast_convert_type(p << 16, F32)
    hi = lax.bitcast_convert_type(p & jnp.uint32(0xFFFF0000), F32)
    return lo, hi


def _rms(x, g):
    return x * lax.rsqrt(jnp.mean(x * x, axis=-1, keepdims=True) + NORM_EPS) * g


def _ada_kernel(c_ref, w_ref, b_ref, o_ref):
    c = c_ref[...]
    a = c * jax.nn.sigmoid(c)
    o_ref[...] = jnp.dot(a, w_ref[...], preferred_element_type=F32,
                         precision=lax.Precision.HIGHEST) + b_ref[...]


def _ada(c_pad, w_ada, b_ada):
    n = w_ada.shape[1]
    bn = 1536
    return pl.pallas_call(
        _ada_kernel,
        grid=(n // bn,),
        in_specs=[pl.BlockSpec((8, D_MODEL), lambda j: (0, 0)),
                  pl.BlockSpec((D_MODEL, bn), lambda j: (0, j)),
                  pl.BlockSpec((1, bn), lambda j: (0, j))],
        out_specs=pl.BlockSpec((8, bn), lambda j: (0, j)),
        out_shape=jax.ShapeDtypeStruct((8, n), F32),
        compiler_params=_cparams(("arbitrary",)),
        name="ada_mod",
    )(c_pad, w_ada, b_ada)


GROUPS_PER_LANE_TILE = LANES // SSM_GROUP_CH


def _to_group_chunks(u, tile_ref, ut_ref):
    n_chunks = u.shape[0] // S5_Q
    for j in range(SSM_CH // LANES):
        tile_ref[j] = u[:, LANES * j:LANES * (j + 1)]
    for j in range(SSM_CH // LANES):
        rows_t = [tile_ref[j, pl.ds(t, n_chunks, stride=S5_Q), :] for t in range(S5_Q)]
        for gg in range(GROUPS_PER_LANE_TILE):
            lo = gg * SSM_GROUP_CH
            row = jnp.concatenate([r[:, lo:lo + SSM_GROUP_CH] for r in rows_t], axis=1)
            ut_ref[j * GROUPS_PER_LANE_TILE + gg] = row.astype(ut_ref.dtype)


def _from_group_chunks(yt_ref, tile_ref):
    n_chunks = yt_ref.shape[1]
    for j in range(SSM_CH // LANES):
        for t in range(S5_Q):
            lo = t * SSM_GROUP_CH
            piece = jnp.concatenate(
                [yt_ref[j * GROUPS_PER_LANE_TILE + gg, :, lo:lo + SSM_GROUP_CH]
                 for gg in range(GROUPS_PER_LANE_TILE)], axis=1)
            tile_ref[j, pl.ds(t, n_chunks, stride=S5_Q), :] = piece
    return jnp.concatenate([tile_ref[j] for j in range(SSM_CH // LANES)], axis=1)


def _inproj_kernel(x_ref, mod_ref, g_ref, w_ref, v_ref, u_ref, ut_ref, tile_ref):
    x = x_ref[...]
    sh = mod_ref[0, 0:1, :]
    sc = mod_ref[0, 1:2, :]
    h = _rms(x, g_ref[...]) * (1.0 + sc) + sh
    z = jnp.dot(h.astype(BF16), w_ref[...], preferred_element_type=F32)
    v_ref[...] = z[:, :CONV_CH] * jax.nn.sigmoid(z[:, CONV_CH:2 * CONV_CH])
    u = z[:, 2 * CONV_CH:]
    u_ref[...] = u
    _to_group_chunks(u, tile_ref, ut_ref)


def _inproj(x2, mod, g_pre, w_in_bf):
    tiles_per_seq = SEQ // TM_IN
    return pl.pallas_call(
        _inproj_kernel,
        grid=(N_TOK // TM_IN,),
        in_specs=[pl.BlockSpec((TM_IN, D_MODEL), lambda i: (i, 0)),
                  pl.BlockSpec((1, 8, D_MODEL), lambda i: (i // tiles_per_seq, 0, 0)),
                  pl.BlockSpec((1, D_MODEL), lambda i: (0, 0)),
                  pl.BlockSpec((D_MODEL, D_IN), lambda i: (0, 0))],
        out_specs=[pl.BlockSpec((TM_IN, CONV_CH), lambda i: (i, 0)),
                   pl.BlockSpec((TM_IN, SSM_CH), lambda i: (i, 0)),
                   pl.BlockSpec((SSM_GROUPS, TM_IN // S5_Q, S5_QH), lambda i: (0, i, 0))],
        out_shape=[jax.ShapeDtypeStruct((N_TOK, CONV_CH), F32),
                   jax.ShapeDtypeStruct((N_TOK, SSM_CH), F32),
                   jax.ShapeDtypeStruct((SSM_GROUPS, S5_CHUNKS, S5_QH), BF16)],
        scratch_shapes=[pltpu.VMEM((SSM_CH // LANES, TM_IN, LANES), F32)],
        compiler_params=_cparams(("parallel",)),
        name="in_proj",
    )(x2, mod, g_pre, w_in_bf)


def _conv_kernel(vc_ref, vp_ref, w_ref, cb_ref, lg_ref, lb_ref, go_ref, o_ref, sh_ref):
    i = pl.program_id(1)
    keep = (i > 0).astype(F32)
    n_ext = TL_CONV + CONV_HALO
    sh_ref[0, 0:CONV_HALO, :] = vp_ref[0] * keep
    sh_ref[0, CONV_HALO:, :] = vc_ref[0]
    for s in range(1, SUBLANES):
        sh_ref[s, 0:n_ext - s, :] = sh_ref[0, s:n_ext, :]
    off = CONV_HALO - (CONV_WIDTH - 1)
    for r in range(TL_CONV // CONV_ROWS):
        acc = None
        for j in range(CONV_WIDTH):
            s = (off + j) % SUBLANES
            al = r * CONV_ROWS + (off + j) - s
            term = w_ref[j:j + 1, :] * sh_ref[s, al:al + CONV_ROWS, :]
            acc = term if acc is None else acc + term
        y = acc + cb_ref[...]
        mu = jnp.mean(y, axis=-1, keepdims=True)
        d = y - mu
        var = jnp.mean(d * d, axis=-1, keepdims=True)
        yn = d * lax.rsqrt(var + NORM_EPS) * lg_ref[...] + lb_ref[...]
        a = yn * jax.nn.sigmoid(yn)
        o_ref[0, r * CONV_ROWS:(r + 1) * CONV_ROWS, :] = _rms(a, go_ref[...]).astype(BF16)


def _conv(v3, conv_w, conv_b, ln_g, ln_b, g_out):
    halo_per_tile = TL_CONV // CONV_HALO
    vec = pl.BlockSpec((1, CONV_CH), lambda b, i: (0, 0))
    return pl.pallas_call(
        _conv_kernel,
        grid=(BATCH, SEQ // TL_CONV),
        in_specs=[pl.BlockSpec((1, TL_CONV, CONV_CH), lambda b, i: (b, i, 0)),
                  pl.BlockSpec((1, CONV_HALO, CONV_CH),
                               lambda b, i: (b, jnp.maximum(i * halo_per_tile - 1, 0), 0)),
                  pl.BlockSpec((CONV_WIDTH + 1, CONV_CH), lambda b, i: (0, 0)),
                  vec, vec, vec, vec],
        out_specs=pl.BlockSpec((1, TL_CONV, CONV_CH), lambda b, i: (b, i, 0)),
        out_shape=jax.ShapeDtypeStruct((BATCH, SEQ, CONV_CH), BF16),
        scratch_shapes=[pltpu.VMEM((SUBLANES, TL_CONV + CONV_HALO, CONV_CH), F32)],
        compiler_params=_cparams(("parallel", "arbitrary")),
        name="conv_module",
    )(v3, v3, conv_w, conv_b, ln_g, ln_b, g_out)


S5_GROUP_ROWS = S5_CHUNKS + 8


S5_POW_ROWS = (S5_Q + 1 + SUBLANES - 1) // SUBLANES * SUBLANES
(S5_BB_RI, S5_BB_NIR, S5_BB_IR, S5_BB_RNI, S5_CC_RI, S5_CC_NIR, S5_N_PARAM) = range(7)


def _s5_kernel(ut_ref, pwr_ref, pwi_ref, par_ref, a_ref, bq_ref, bs_ref, yt_ref, sin_s, sp_s):
    phase = pl.program_id(0)
    g = pl.program_id(1)
    q = S5_Q
    n = 2 * SSM_STATE
    row0 = pl.multiple_of(g * S5_GROUP_ROWS, 8)

    def lam_pow(j):
        return pwr_ref[0, j:j + 1, :], pwi_ref[0, j:j + 1, :]

    @pl.when(phase == 0)
    def _():
        bb_ri, bb_nir = par_ref[0, S5_BB_RI], par_ref[0, S5_BB_NIR]
        bb_ir, bb_rni = par_ref[0, S5_BB_IR], par_ref[0, S5_BB_RNI]
        blk_q, blk_s = [], []
        for t in range(q):
            pr, pi_ = lam_pow(q - 1 - t)
            blk_q.append(pr * bb_ri + pi_ * bb_nir)
            blk_s.append(pr * bb_ir + pi_ * bb_rni)
        wst = jnp.concatenate([jnp.concatenate(blk_q, axis=0), jnp.concatenate(blk_s, axis=0)], axis=1)
        r = jnp.dot(ut_ref[0], wst.astype(BF16), preferred_element_type=F32)
        sin_s[0, pl.ds(row0, S5_CHUNKS), :] = r[:, :n]
        sin_s[1, pl.ds(row0, S5_CHUNKS), :] = r[:, n:]

    @pl.when((phase == 1) & (g == 0))
    def _():
        a = a_ref[...]
        bq = bq_ref[...]
        bs = bs_ref[...]

        def body(c, carry):
            nxt = []
            for b in range(BATCH):
                x, xs = carry[b]
                rows = pl.ds(b * S5_CHUNKS_PER_SEQ + c, SSM_GROUPS, stride=S5_GROUP_ROWS)
                sp_s[rows, :] = x
                nxt.append((a * x + bq * xs + sin_s[0, rows, :], a * xs + bs * x + sin_s[1, rows, :]))
            return tuple(nxt)

        z = jnp.zeros((SSM_GROUPS, n), F32)
        lax.fori_loop(0, S5_CHUNKS_PER_SEQ, body, tuple((z, z) for _ in range(BATCH)))

    @pl.when(phase == 1)
    def _():
        cc_ri, cc_nir = par_ref[0, S5_CC_RI], par_ref[0, S5_CC_NIR]
        cl = []
        for j in range(q + 1):
            pr, pi_ = lam_pow(j)
            cl.append(pr * cc_ri + pi_ * cc_nir)
        cl_lo = jnp.concatenate(cl[:q], axis=0)
        cl_hi = jnp.concatenate(cl[1:], axis=0)
        lane = lax.broadcasted_iota(jnp.int32, (1, n), 1)
        vgt = (cl_hi * jnp.where(lane < SSM_STATE, 1.0, -1.0)).astype(BF16)
        kt = lax.dot_general(par_ref[0, S5_BB_RNI], cl_lo, (((1,), (1,)), ((), ())),
                             preferred_element_type=F32, precision=lax.Precision.HIGHEST)
        padded = jnp.concatenate([jnp.zeros_like(kt), kt], axis=1)
        tg = jnp.concatenate(
            [padded[:, (q - t) * SSM_GROUP_CH:(q - t) * SSM_GROUP_CH + S5_QH] for t in range(q)],
            axis=0).astype(BF16)
        sp = sp_s[pl.ds(row0, S5_CHUNKS), :]
        y = jnp.dot(ut_ref[0], tg, preferred_element_type=F32)
        yt_ref[0] = y + lax.dot_general(sp.astype(BF16), vgt, (((1,), (1,)), ((), ())),
                                        preferred_element_type=F32)


def _s5(ut, pwr, pwi, params, a_cat, b_q, b_s):
    vec = pl.BlockSpec((SSM_GROUPS, 2 * SSM_STATE), lambda p, g: (0, 0))
    powers = pl.BlockSpec((1, S5_POW_ROWS, 2 * SSM_STATE), lambda p, g: (g, 0, 0))
    return pl.pallas_call(
        _s5_kernel,
        grid=(2, SSM_GROUPS),
        in_specs=[pl.BlockSpec((1, S5_CHUNKS, S5_QH), lambda p, g: (g, 0, 0)),
                  powers, powers,
                  pl.BlockSpec((1, S5_N_PARAM, SSM_GROUP_CH, 2 * SSM_STATE), lambda p, g: (g, 0, 0, 0)),
                  vec, vec, vec],
        out_specs=pl.BlockSpec((1, S5_CHUNKS, S5_QH), lambda p, g: (g * p, 0, 0)),
        out_shape=jax.ShapeDtypeStruct((SSM_GROUPS, S5_CHUNKS, S5_QH), F32),
        scratch_shapes=[pltpu.VMEM((2, SSM_GROUPS * S5_GROUP_ROWS, 2 * SSM_STATE), F32),
                        pltpu.VMEM((SSM_GROUPS * S5_GROUP_ROWS, 2 * SSM_STATE), F32)],
        compiler_params=_cparams(("arbitrary", "arbitrary")),
        name="s5_chunked",
    )(ut, pwr, pwi, params, a_cat, b_q, b_s)


def _s5_operators(a_re, a_im, log_dt, b_re, b_im, c_re, c_im):
    q = S5_Q
    dt = jnp.exp(log_dt)[:, None]
    ar, ai = a_re, a_im
    mag = jnp.exp(ar * dt)
    lr = mag * jnp.cos(ai * dt)
    li = mag * jnp.sin(ai * dt)
    den = ar * ar + ai * ai
    nr = lr - 1.0
    kr = (nr * ar + li * ai) / den
    ki = (li * ar - nr * ai) / den
    bbr = kr[..., None] * b_re - ki[..., None] * b_im
    bbi = kr[..., None] * b_im + ki[..., None] * b_re
    j = jnp.arange(q + 1, dtype=F32)[None, :, None]
    pmag = jnp.exp(ar[:, None, :] * dt[:, :, None] * j)
    pang = ai[:, None, :] * dt[:, :, None] * j
    pr = pmag * jnp.cos(pang)
    pi_ = pmag * jnp.sin(pang)
    pad = ((0, 0), (0, S5_POW_ROWS - (q + 1)), (0, 0))
    pwr = jnp.pad(jnp.concatenate([pr, pr], axis=-1), pad)
    pwi = jnp.pad(jnp.concatenate([pi_, pi_], axis=-1), pad)
    br_t = bbr.transpose(0, 2, 1)
    bi_t = bbi.transpose(0, 2, 1)
    cat = lambda a, b: jnp.concatenate([a, b], axis=-1)
    stack = [None] * S5_N_PARAM
    stack[S5_BB_RI] = cat(br_t, bi_t)
    stack[S5_BB_NIR] = cat(-bi_t, br_t)
    stack[S5_BB_IR] = cat(bi_t, br_t)
    stack[S5_BB_RNI] = cat(br_t, -bi_t)
    stack[S5_CC_RI] = cat(c_re, c_im)
    stack[S5_CC_NIR] = cat(-c_im, c_re)
    params = jnp.stack(stack, axis=1)
    aq_r, aq_i = pr[:, q], pi_[:, q]
    a_cat = cat(aq_r, aq_r)
    b_q = cat(-aq_i, aq_i)
    b_s = cat(aq_i, -aq_i)
    return pwr, pwi, params, a_cat, b_q, b_s


def _gelu_tanh(x):
    return 0.5 * x * (1.0 + jnp.tanh(math.sqrt(2.0 / math.pi) * (x + 0.044715 * (x * x * x))))


def _mix_kernel(an_ref, yt_ref, u_ref, x_ref, mod_ref, d_ref, wglu_ref, bglu_ref, gos_ref,
                woa_ref, wob_ref, gpm_ref, gpf_ref, wr_ref, br_ref, tri_ref,
                x1_ref, h2_ref, eidx_ref, rank_ref, gw_ref, cnt_ref, run_ref, tile_ref):
    i = pl.program_id(0)
    tm = TM_MIX

    @pl.when(i == 0)
    def _():
        run_ref[...] = jnp.zeros_like(run_ref)

    gt_m = mod_ref[0, 2:3, :]
    sh_f = mod_ref[0, 3:4, :]
    sc_f = mod_ref[0, 4:5, :]

    yy = _from_group_chunks(yt_ref, tile_ref) + d_ref[...] * u_ref[...]
    g = _gelu_tanh(yy)
    gl = jnp.dot(g.astype(BF16), wglu_ref[...], preferred_element_type=F32) + bglu_ref[...]
    ob = g * jax.nn.sigmoid(gl)
    bn = _rms(ob, gos_ref[...]).astype(BF16)
    o = (jnp.dot(an_ref[...], woa_ref[...], preferred_element_type=F32)
         + jnp.dot(bn, wob_ref[...], preferred_element_type=F32))
    x1 = x_ref[...] + gt_m * _rms(o, gpm_ref[...])
    x1_ref[...] = x1
    h2 = _rms(x1, gpf_ref[...]) * (1.0 + sc_f) + sh_f
    h2_ref[...] = _pack_halves(h2)

    logits = lax.dot_general(wr_ref[...], h2, (((1,), (1,)), ((), ())),
                             preferred_element_type=F32, precision=lax.Precision.HIGHEST)
    scores = jax.nn.sigmoid(logits)
    biased = scores + br_ref[...]
    ng = N_ROUTE_GROUPS
    gsz = N_EXPERTS // ng
    b3 = biased.reshape(ng, gsz, tm)
    s3 = scores.reshape(ng, gsz, tm)
    sub = lax.broadcasted_iota(jnp.int32, (ng, gsz, tm), 1).astype(F32)
    grp = lax.broadcasted_iota(jnp.int32, (ng, gsz, tm), 0).astype(F32)
    eid = grp * gsz + sub
    neg = -jnp.inf
    m1 = jnp.max(b3, axis=1, keepdims=True)
    i1 = jnp.min(jnp.where(b3 == m1, sub, float(gsz)), axis=1, keepdims=True)
    m2 = jnp.max(jnp.where(sub == i1, neg, b3), axis=1, keepdims=True)
    gs = m1 + m2
    gi = lax.broadcasted_iota(jnp.int32, (ng, 1, tm), 0)
    beaten = jnp.zeros((ng, 1, tm), F32)
    for gp in range(ng):
        o_ = gs[gp:gp + 1]
        beats = (o_ > gs) | ((o_ == gs) & (gi > gp))
        beaten = beaten + beats.astype(F32)
    gmask = beaten < float(TOPK_ROUTE_GROUPS)
    masked = jnp.where(gmask, b3, neg)

    sels = []
    picked = jnp.zeros((ng, gsz, tm), F32)
    for k in range(TOP_K):
        m = jnp.max(jnp.max(masked, axis=0, keepdims=True), axis=1, keepdims=True)
        cand = jnp.where(masked == m, eid, float(N_EXPERTS))
        sel = jnp.min(jnp.min(cand, axis=0, keepdims=True), axis=1, keepdims=True)
        oh = eid == sel
        masked = jnp.where(oh, neg, masked)
        picked = jnp.where(oh, 1.0, picked)
        sels.append(sel)

    pm = picked.reshape(N_EXPERTS, tm)
    prefix = jnp.dot(pm.astype(BF16), tri_ref[...], preferred_element_type=F32) + run_ref[:, 0:1]
    p3 = prefix.reshape(ng, gsz, tm)
    run_new = run_ref[...] + jnp.sum(pm, axis=1, keepdims=True)
    run_ref[...] = run_new
    cnt_ref[...] = run_new

    sc_rows = []
    for k in range(TOP_K):
        oh = eid == sels[k]
        sc_k = jnp.sum(jnp.sum(jnp.where(oh, s3, 0.0), axis=0, keepdims=True), axis=1, keepdims=True)
        rk_k = jnp.sum(jnp.sum(jnp.where(oh, p3, 0.0), axis=0, keepdims=True), axis=1, keepdims=True)
        sc_rows.append(sc_k)
        eidx_ref[k:k + 1, :] = sels[k].reshape(1, tm).astype(jnp.int32)
        rank_ref[k:k + 1, :] = rk_k.reshape(1, tm).astype(jnp.int32)
    tot = sc_rows[0]
    for k in range(1, TOP_K):
        tot = tot + sc_rows[k]
    inv = ROUTED_SCALE / (tot + 1e-20)
    for k in range(TOP_K):
        gw_ref[k:k + 1, :] = (sc_rows[k] * inv).reshape(1, tm)


def _mix(half, a_n, yt, u2, x2, mod, d_skip, wglu_bf, b_glu, g_out_ssm, wo_a, wo_b, g_post_mix,
         g_pre_ffn, w_router_t, b_router_col, tri):
    tm = TM_MIX
    t0 = half * (HALF_TOK // tm)
    row = lambda n: pl.BlockSpec((1, n), lambda i: (0, 0))
    full = lambda a, b: pl.BlockSpec((a, b), lambda i: (0, 0))
    tok_in = lambda n: pl.BlockSpec((tm, n), lambda i: (t0 + i, 0))
    tok = lambda n: pl.BlockSpec((tm, n), lambda i: (i, 0))
    col = pl.BlockSpec((TOP_K, tm), lambda i: (0, i))
    return pl.pallas_call(
        _mix_kernel,
        grid=(HALF_TOK // tm,),
        in_specs=[tok_in(CONV_CH),
                  pl.BlockSpec((SSM_GROUPS, tm // S5_Q, S5_QH), lambda i: (0, t0 + i, 0)),
                  tok_in(SSM_CH), tok_in(D_MODEL),
                  pl.BlockSpec((1, 8, D_MODEL), lambda i: (half, 0, 0)),
                  row(SSM_CH), full(SSM_CH, SSM_CH), row(SSM_CH), row(SSM_CH),
                  full(CONV_CH, D_MODEL), full(SSM_CH, D_MODEL), row(D_MODEL), row(D_MODEL),
                  full(N_EXPERTS, D_MODEL), full(N_EXPERTS, 1), full(tm, tm)],
        out_specs=[tok(D_MODEL), tok(D_MODEL // 2), col, col, col,
                   pl.BlockSpec((N_EXPERTS, 128), lambda i: (0, 0))],
        out_shape=[jax.ShapeDtypeStruct((HALF_TOK, D_MODEL), F32),
                   jax.ShapeDtypeStruct((HALF_TOK, D_MODEL // 2), jnp.uint32),
                   jax.ShapeDtypeStruct((TOP_K, HALF_TOK), jnp.int32),
                   jax.ShapeDtypeStruct((TOP_K, HALF_TOK), jnp.int32),
                   jax.ShapeDtypeStruct((TOP_K, HALF_TOK), F32),
                   jax.ShapeDtypeStruct((N_EXPERTS, 128), F32)],
        scratch_shapes=[pltpu.VMEM((N_EXPERTS, 128), F32),
                        pltpu.VMEM((SSM_CH // LANES, tm, LANES), F32)],
        compiler_params=_cparams(("arbitrary",)),
        name="mix_out_router",
    )(a_n, yt, u2, x2, mod, d_skip, wglu_bf, b_glu, g_out_ssm, wo_a, wo_b, g_post_mix,
      g_pre_ffn, w_router_t, b_router_col, tri)


def _expert_kernel(blk0_ref, nblk_ref, xs_hbm, wg_ref, wu_ref, wd_ref, ys_hbm,
                   xbuf, ybuf, wgu_s, wd_s, sem_in, sem_out):
    e = pl.program_id(0)
    n = nblk_ref[e]
    b0 = blk0_ref[e]
    n_all = blk0_ref[N_EXPERTS - 1] + nblk_ref[N_EXPERTS - 1]

    def rows(b):
        return pl.ds(pl.multiple_of(b * ROW_BLOCK, ROW_BLOCK), ROW_BLOCK)

    def in_copy(b, slot):
        return pltpu.make_async_copy(xs_hbm.at[rows(b)], xbuf.at[slot], sem_in.at[slot])

    def out_copy(b, slot):
        return pltpu.make_async_copy(ybuf.at[slot], ys_hbm.at[rows(b)], sem_out.at[slot])

    for b in range(EXPERT_IN_SLOTS - 1):
        @pl.when((e == 0) & (b < n_all))
        def _():
            in_copy(b, b).start()

    wgu_s[:, :D_EXPERT] = wg_ref[0].astype(BF16)
    wgu_s[:, D_EXPERT:] = wu_ref[0].astype(BF16)
    wd_s[...] = wd_ref[0].astype(BF16)

    def body(b, carry):
        slot = b % 2
        in_slot = b % EXPERT_IN_SLOTS
        in_copy(b, in_slot).wait()
        ahead = b + EXPERT_IN_SLOTS - 1

        @pl.when(ahead < n_all)
        def _():
            in_copy(ahead, ahead % EXPERT_IN_SLOTS).start()

        @pl.when(b >= 2)
        def _():
            out_copy(b - 2, slot).wait()

        x_lo, x_hi = _unpack_halves(xbuf[in_slot])
        x = jnp.concatenate([x_lo.astype(BF16), x_hi.astype(BF16)], axis=1)
        h = jnp.dot(x, wgu_s[...], preferred_element_type=F32)
        hg = h[:, :D_EXPERT]
        act = hg * jax.nn.sigmoid(hg) * h[:, D_EXPERT:]
        ybuf[slot] = _pack_halves(jnp.dot(act.astype(BF16), wd_s[...], preferred_element_type=F32))
        out_copy(b, slot).start(priority=1)
        return carry

    lax.fori_loop(b0, b0 + n, body, 0)

    last = e == N_EXPERTS - 1

    @pl.when(last & (n_all >= 2))
    def _():
        out_copy(n_all - 2, n_all % 2).wait()

    @pl.when(last & (n_all >= 1))
    def _():
        out_copy(n_all - 1, (n_all - 1) % 2).wait()


def _experts(blk0, nblk, xs, we_gate, we_up, we_down):
    any_spec = pl.BlockSpec(memory_space=pl.ANY)
    grid_spec = pltpu.PrefetchScalarGridSpec(
        num_scalar_prefetch=2,
        grid=(N_EXPERTS,),
        in_specs=[any_spec,
                  pl.BlockSpec((1, D_MODEL, D_EXPERT), lambda e, b0, nb: (e, 0, 0)),
                  pl.BlockSpec((1, D_MODEL, D_EXPERT), lambda e, b0, nb: (e, 0, 0)),
                  pl.BlockSpec((1, D_EXPERT, D_MODEL), lambda e, b0, nb: (e, 0, 0))],
        out_specs=any_spec,
        scratch_shapes=[pltpu.VMEM((EXPERT_IN_SLOTS, ROW_BLOCK, D_MODEL // 2), jnp.uint32),
                        pltpu.VMEM((2, ROW_BLOCK, D_MODEL // 2), jnp.uint32),
                        pltpu.VMEM((D_MODEL, 2 * D_EXPERT), BF16),
                        pltpu.VMEM((D_EXPERT, D_MODEL), BF16),
                        pltpu.SemaphoreType.DMA((EXPERT_IN_SLOTS,)),
                        pltpu.SemaphoreType.DMA((2,))],
    )
    return pl.pallas_call(
        _expert_kernel,
        grid_spec=grid_spec,
        out_shape=jax.ShapeDtypeStruct((N_ROWS, D_MODEL // 2), jnp.uint32),
        compiler_params=_cparams(("arbitrary",)),
        name="routed_experts",
    )(blk0, nblk, xs, we_gate, we_up, we_down)


def _final_kernel(h2_ref, yg_ref, gw_ref, x1_ref, mod_ref, wgu_ref, wd_ref, g_ref, *rest):
    o_ref = rest[-1]
    half = D_MODEL // 2
    gt_f = mod_ref[0, 5:6, :]
    x_lo, x_hi = _unpack_halves(h2_ref[...])
    h = (jnp.dot(x_lo.astype(BF16), wgu_ref[:half, :], preferred_element_type=F32)
         + jnp.dot(x_hi.astype(BF16), wgu_ref[half:, :], preferred_element_type=F32))
    hg = h[:, :D_EXPERT]
    act = hg * jax.nn.sigmoid(hg) * h[:, D_EXPERT:]
    shared = jnp.dot(act.astype(BF16), wd_ref[...], preferred_element_type=F32)
    y_lo = shared[:, :half]
    y_hi = shared[:, half:]
    for k in range(TOP_K):
        r_lo, r_hi = _unpack_halves(yg_ref[k])
        w = gw_ref[:, k:k + 1]
        y_lo = y_lo + w * r_lo
        y_hi = y_hi + w * r_hi
    ms = (jnp.sum(y_lo * y_lo, axis=-1, keepdims=True)
          + jnp.sum(y_hi * y_hi, axis=-1, keepdims=True)) * (1.0 / D_MODEL)
    inv = lax.rsqrt(ms + NORM_EPS)
    o_ref[:, :half] = x1_ref[:, :half] + gt_f[:, :half] * (y_lo * inv * g_ref[:, :half])
    o_ref[:, half:] = x1_ref[:, half:] + gt_f[:, half:] * (y_hi * inv * g_ref[:, half:])


def _final(half, out_prev, h2p, yg, gw_t, x1, mod, ws_gu, ws_d, g_post_ffn):
    tm = TM_OUT
    t0 = half * (HALF_TOK // tm)
    tok = pl.BlockSpec((tm, D_MODEL), lambda i: (i, 0))
    in_specs = [pl.BlockSpec((tm, D_MODEL // 2), lambda i: (i, 0)),
                pl.BlockSpec((TOP_K, tm, D_MODEL // 2), lambda i: (0, i, 0)),
                pl.BlockSpec((tm, TOP_K), lambda i: (i, 0)),
                tok,
                pl.BlockSpec((1, 8, D_MODEL), lambda i: (half, 0, 0)),
                pl.BlockSpec((D_MODEL, 2 * D_EXPERT), lambda i: (0, 0)),
                pl.BlockSpec((D_EXPERT, D_MODEL), lambda i: (0, 0)),
                pl.BlockSpec((1, D_MODEL), lambda i: (0, 0))]
    args = [h2p, yg, gw_t, x1, mod, ws_gu, ws_d, g_post_ffn]
    aliases = {}
    if out_prev is not None:
        aliases = {len(args): 0}
        in_specs.append(pl.BlockSpec(memory_space=pl.ANY))
        args.append(out_prev)
    return pl.pallas_call(
        _final_kernel,
        grid=(HALF_TOK // tm,),
        in_specs=in_specs,
        out_specs=pl.BlockSpec((tm, D_MODEL), lambda i: (t0 + i, 0)),
        out_shape=jax.ShapeDtypeStruct((N_TOK, D_MODEL), F32),
        input_output_aliases=aliases,
        compiler_params=_cparams(("parallel",)),
        name="shared_final",
    )(*args)


def _sc_worker_id():
    return lax.axis_index("s") * SC_CORES + lax.axis_index("c")


def _dispatch_body(h_hbm, dest_hbm, xs_hbm, idx_v, rows_v, sem_l, sem_s):
    n = SC_CHUNKS_PER_WORKER
    c0 = _sc_worker_id() * n

    def load(i, b):
        return pltpu.async_copy(h_hbm.at[pl.ds((c0 + i) * SC_W, SC_W)], rows_v.at[b], sem_l.at[b])

    loads = [None] * n
    scat = [None] * n
    loads[0] = load(0, 0)
    for i in range(n):
        b = i % 2
        pltpu.sync_copy(dest_hbm.at[c0 + i], idx_v.at[b])
        loads[i].wait()
        if i + 1 < n:
            if i >= 1:
                for d in scat[i - 1]:
                    d.wait()
            loads[i + 1] = load(i + 1, 1 - b)
        scat[i] = [pltpu.async_copy(rows_v.at[b], xs_hbm.at[idx_v.at[b].at[k]], sem_s.at[b])
                   for k in range(TOP_K)]
    for i in (n - 2, n - 1):
        for d in scat[i]:
            d.wait()


def _sc_dispatch(h2p, dest3):
    mesh = plsc.VectorSubcoreMesh(core_axis_name="c", subcore_axis_name="s")
    return pl.kernel(
        _dispatch_body, mesh=mesh,
        out_type=jax.ShapeDtypeStruct((N_ROWS, D_MODEL // 2), jnp.uint32),
        scratch_types=[pltpu.VMEM((2, TOP_K, SC_W), jnp.int32),
                       pltpu.VMEM((2, SC_W, D_MODEL // 2), jnp.uint32),
                       pltpu.SemaphoreType.DMA((2,)), pltpu.SemaphoreType.DMA((2,))],
    )(h2p, dest3)


def _combine_body(ys_hbm, dest_hbm, yg_hbm, idx_v, rows_v, sem_g, sem_w):
    c0 = _sc_worker_id() * SC_CHUNKS_PER_WORKER

    @pl.loop(0, SC_CHUNKS_PER_WORKER)
    def _(i):
        c = c0 + i
        pltpu.sync_copy(dest_hbm.at[c], idx_v)
        g = [None] * TOP_K
        w = [None] * TOP_K
        g[0] = pltpu.async_copy(ys_hbm.at[idx_v.at[0]], rows_v.at[0], sem_g.at[0])
        for k in range(TOP_K):
            b = k % 2
            g[k].wait()
            if k + 1 < TOP_K:
                if k >= 1:
                    w[k - 1].wait()
                g[k + 1] = pltpu.async_copy(ys_hbm.at[idx_v.at[k + 1]], rows_v.at[1 - b], sem_g.at[1 - b])
            w[k] = pltpu.async_copy(rows_v.at[b], yg_hbm.at[k].at[pl.ds(c * SC_W, SC_W)], sem_w.at[b])
        w[TOP_K - 2].wait()
        w[TOP_K - 1].wait()


def _sc_combine(ysp, dest3):
    mesh = plsc.VectorSubcoreMesh(core_axis_name="c", subcore_axis_name="s")
    return pl.kernel(
        _combine_body, mesh=mesh,
        out_type=jax.ShapeDtypeStruct((TOP_K, HALF_TOK, D_MODEL // 2), jnp.uint32),
        scratch_types=[pltpu.VMEM((TOP_K, SC_W), jnp.int32),
                       pltpu.VMEM((2, SC_W, D_MODEL // 2), jnp.uint32),
                       pltpu.SemaphoreType.DMA((2,)), pltpu.SemaphoreType.DMA((2,))],
    )(ysp, dest3)


def kernel(x, c, w_ada, b_ada, g_pre_mix, g_post_mix, w_in, conv_w, conv_b, conv_ln_g, conv_ln_b,
           ssm_a_re, ssm_a_im, ssm_log_dt, ssm_b_re, ssm_b_im, ssm_c_re, ssm_c_im, ssm_d,
           ssm_w_glu, ssm_b_glu, g_out_conv, g_out_ssm, w_out, g_pre_ffn, g_post_ffn,
           w_router, b_router, we_gate, we_up, we_down, ws_gate, ws_up, ws_down):
    l = 0
    x2 = x.reshape(N_TOK, D_MODEL)
    r1 = lambda a: a.reshape(1, -1)

    c_pad = jnp.zeros((8, D_MODEL), F32).at[:BATCH].set(c)
    mod = _ada(c_pad, w_ada[l], r1(b_ada[l]))[:BATCH].reshape(BATCH, 6, D_MODEL)
    mod = jnp.concatenate([mod, jnp.zeros((BATCH, 2, D_MODEL), F32)], axis=1)

    v, u, ut = _inproj(x2, mod, r1(g_pre_mix[l]), w_in[l].astype(BF16))
    cw = jnp.concatenate([conv_w[l].reshape(CONV_WIDTH, CONV_CH), jnp.zeros((1, CONV_CH), F32)], axis=0)
    a_n = _conv(v.reshape(BATCH, SEQ, CONV_CH), cw, r1(conv_b[l]), r1(conv_ln_g[l]),
                r1(conv_ln_b[l]), r1(g_out_conv[l])).reshape(N_TOK, CONV_CH)

    pwr, pwi, s5_params, a_cat, b_q, b_s = _s5_operators(
        ssm_a_re[l], ssm_a_im[l], ssm_log_dt[l], ssm_b_re[l], ssm_b_im[l], ssm_c_re[l], ssm_c_im[l])
    yt = _s5(ut, pwr, pwi, s5_params, a_cat, b_q, b_s)

    tm = TM_MIX
    tri = (jnp.arange(tm)[:, None] < jnp.arange(tm)[None, :]).astype(BF16)
    wo = w_out[l].astype(BF16)
    mix_params = (r1(ssm_d[l]), ssm_w_glu[l].astype(BF16), r1(ssm_b_glu[l]), r1(g_out_ssm[l]),
                  wo[:CONV_CH], wo[CONV_CH:], r1(g_post_mix[l]), r1(g_pre_ffn[l]),
                  w_router[l].T, b_router[l].reshape(N_EXPERTS, 1), tri)
    ws_gu = jnp.concatenate([ws_gate[l], ws_up[l]], axis=1).astype(BF16)
    ws_d = ws_down[l].astype(BF16)
    e_ids = jnp.arange(N_EXPERTS, dtype=jnp.int32)

    out = None
    for half in range(N_HALVES):
        x1, h2, eidx, rank, gw, cnt = _mix(half, a_n, yt, u, x2, mod, *mix_params)
        counts = cnt[:, 0].astype(jnp.int32)
        padded = (counts + ROW_BLOCK - 1) // ROW_BLOCK * ROW_BLOCK
        pstart = jnp.cumsum(padded) - padded
        dest = rank + jnp.sum(jnp.where(eidx[..., None] == e_ids, pstart, 0), axis=-1)
        dest3 = dest.reshape(TOP_K, HALF_TOK // SC_W, SC_W).transpose(1, 0, 2)

        xs = _sc_dispatch(h2, dest3)
        ys = _experts(pstart // ROW_BLOCK, padded // ROW_BLOCK, xs, we_gate[l], we_up[l], we_down[l])
        yg = _sc_combine(ys, dest3)
        out = _final(half, out, h2, yg, gw.T, x1, mod, ws_gu, ws_d, r1(g_post_ffn[l]))
    return out.reshape(BATCH, SEQ, D_MODEL)
```

```python
import functools
import math

import jax
import jax.numpy as jnp
from jax import lax
from jax.experimental import pallas as pl
from jax.experimental.pallas import tpu as pltpu
from jax.experimental.pallas import tpu_sc as plsc

F32 = jnp.float32
BF16 = jnp.bfloat16

D_MODEL = 1024
BATCH = 2
SEQ = 8192
N_TOK = BATCH * SEQ
CONV_CH = 512
CONV_WIDTH = 31
SSM_CH = 512
SSM_GROUP_CH = 16
SSM_GROUPS = 32
SSM_STATE = 64
D_IN = 2 * CONV_CH + SSM_CH
N_EXPERTS = 64
TOP_K = 8
N_ROUTE_GROUPS = 8
TOPK_ROUTE_GROUPS = 4
D_EXPERT = 256
ROUTED_SCALE = 2.5
NORM_EPS = 1e-6

SUBLANES = 8
LANES = 128

TM_IN = 512
TL_CONV = 512
CONV_HALO = 32
CONV_ROWS = 64
S5_Q = 32
S5_QH = S5_Q * SSM_GROUP_CH
S5_CHUNKS = N_TOK // S5_Q
S5_CHUNKS_PER_SEQ = SEQ // S5_Q
TM_MIX = 512
ROW_BLOCK = 512
EXPERT_AHEAD = 2
EXPERT_SLOTS = 4
HALF_TOK = SEQ
N_HALVES = N_TOK // HALF_TOK
N_BLOCKS = HALF_TOK * TOP_K // ROW_BLOCK + N_EXPERTS
N_ROWS = N_BLOCKS * ROW_BLOCK
TM_OUT = 512
SC_CORES = 2
SC_SUBCORES = 16
SC_WORKERS = SC_CORES * SC_SUBCORES
SC_W = 64
SC_CHUNKS_PER_WORKER = HALF_TOK // (SC_WORKERS * SC_W)
VMEM_LIMIT = 48 * 1024 * 1024


def _cparams(sem):
    return pltpu.CompilerParams(dimension_semantics=sem, vmem_limit_bytes=VMEM_LIMIT)


def _pack_halves(x):
    n = x.shape[-1] // 2
    lo = lax.bitcast_convert_type(x[:, :n].astype(BF16).astype(F32), jnp.uint32)
    hi = lax.bitcast_convert_type(x[:, n:].astype(BF16).astype(F32), jnp.uint32)
    return hi | (lo >> 16)


def _unpack_halves(p):
    lo = lax.bitcast_convert_type(p << 16, F32)
    hi = lax.bitcast_convert_type(p & jnp.uint32(0xFFFF0000), F32)
    return lo, hi


def _rms(x, g):
    return x * lax.rsqrt(jnp.mean(x * x, axis=-1, keepdims=True) + NORM_EPS) * g


def _ada_kernel(c_ref, w_ref, b_ref, o_ref):
    c = c_ref[...]
    a = c * jax.nn.sigmoid(c)
    o_ref[...] = jnp.dot(a, w_ref[...], preferred_element_type=F32,
                         precision=lax.Precision.HIGHEST) + b_ref[...]


def _ada(c_pad, w_ada, b_ada):
    n = w_ada.shape[1]
    bn = 1536
    return pl.pallas_call(
        _ada_kernel,
        grid=(n // bn,),
        in_specs=[pl.BlockSpec((8, D_MODEL), lambda j: (0, 0)),
                  pl.BlockSpec((D_MODEL, bn), lambda j: (0, j)),
                  pl.BlockSpec((1, bn), lambda j: (0, j))],
        out_specs=pl.BlockSpec((8, bn), lambda j: (0, j)),
        out_shape=jax.ShapeDtypeStruct((8, n), F32),
        compiler_params=_cparams(("arbitrary",)),
        name="ada_mod",
    )(c_pad, w_ada, b_ada)


GROUPS_PER_LANE_TILE = LANES // SSM_GROUP_CH


def _to_group_chunks(u, tile_ref, ut_ref):
    n_chunks = u.shape[0] // S5_Q
    for j in range(SSM_CH // LANES):
        tile_ref[j] = u[:, LANES * j:LANES * (j + 1)]
    for j in range(SSM_CH // LANES):
        rows_t = [tile_ref[j, pl.ds(t, n_chunks, stride=S5_Q), :] for t in range(S5_Q)]
        for gg in range(GROUPS_PER_LANE_TILE):
            lo = gg * SSM_GROUP_CH
            row = jnp.concatenate([r[:, lo:lo + SSM_GROUP_CH] for r in rows_t], axis=1)
            ut_ref[j * GROUPS_PER_LANE_TILE + gg] = row.astype(ut_ref.dtype)


def _from_group_chunks(yt_ref, tile_ref):
    n_chunks = yt_ref.shape[1]
    for j in range(SSM_CH // LANES):
        for t in range(S5_Q):
            lo = t * SSM_GROUP_CH
            piece = jnp.concatenate(
                [yt_ref[j * GROUPS_PER_LANE_TILE + gg, :, lo:lo + SSM_GROUP_CH]
                 for gg in range(GROUPS_PER_LANE_TILE)], axis=1)
            tile_ref[j, pl.ds(t, n_chunks, stride=S5_Q), :] = piece
    return jnp.concatenate([tile_ref[j] for j in range(SSM_CH // LANES)], axis=1)


def _inproj_kernel(x_ref, mod_ref, g_ref, w_ref, v_ref, u_ref, ut_ref, tile_ref):
    x = x_ref[...]
    sh = mod_ref[0, 0:1, :]
    sc = mod_ref[0, 1:2, :]
    h = _rms(x, g_ref[...]) * (1.0 + sc) + sh
    z = jnp.dot(h.astype(BF16), w_ref[...], preferred_element_type=F32)
    v_ref[...] = z[:, :CONV_CH] * jax.nn.sigmoid(z[:, CONV_CH:2 * CONV_CH])
    u = z[:, 2 * CONV_CH:]
    u_ref[...] = u
    _to_group_chunks(u, tile_ref, ut_ref)


def _inproj(x2, mod, g_pre, w_in_bf):
    tiles_per_seq = SEQ // TM_IN
    return pl.pallas_call(
        _inproj_kernel,
        grid=(N_TOK // TM_IN,),
        in_specs=[pl.BlockSpec((TM_IN, D_MODEL), lambda i: (i, 0)),
                  pl.BlockSpec((1, 8, D_MODEL), lambda i: (i // tiles_per_seq, 0, 0)),
                  pl.BlockSpec((1, D_MODEL), lambda i: (0, 0)),
                  pl.BlockSpec((D_MODEL, D_IN), lambda i: (0, 0))],
        out_specs=[pl.BlockSpec((TM_IN, CONV_CH), lambda i: (i, 0)),
                   pl.BlockSpec((TM_IN, SSM_CH), lambda i: (i, 0)),
                   pl.BlockSpec((SSM_GROUPS, TM_IN // S5_Q, S5_QH), lambda i: (0, i, 0))],
        out_shape=[jax.ShapeDtypeStruct((N_TOK, CONV_CH), F32),
                   jax.ShapeDtypeStruct((N_TOK, SSM_CH), F32),
                   jax.ShapeDtypeStruct((SSM_GROUPS, S5_CHUNKS, S5_QH), BF16)],
        scratch_shapes=[pltpu.VMEM((SSM_CH // LANES, TM_IN, LANES), F32)],
        compiler_params=_cparams(("parallel",)),
        name="in_proj",
    )(x2, mod, g_pre, w_in_bf)


def _conv_kernel(vc_ref, vp_ref, w_ref, cb_ref, lg_ref, lb_ref, go_ref, o_ref, sh_ref):
    i = pl.program_id(1)
    keep = (i > 0).astype(F32)
    n_ext = TL_CONV + CONV_HALO
    sh_ref[0, 0:CONV_HALO, :] = vp_ref[0] * keep
    sh_ref[0, CONV_HALO:, :] = vc_ref[0]
    for s in range(1, SUBLANES):
        sh_ref[s, 0:n_ext - s, :] = sh_ref[0, s:n_ext, :]
    off = CONV_HALO - (CONV_WIDTH - 1)
    for r in range(TL_CONV // CONV_ROWS):
        acc = None
        for j in range(CONV_WIDTH):
            s = (off + j) % SUBLANES
            al = r * CONV_ROWS + (off + j) - s
            term = w_ref[j:j + 1, :] * sh_ref[s, al:al + CONV_ROWS, :]
            acc = term if acc is None else acc + term
        y = acc + cb_ref[...]
        mu = jnp.mean(y, axis=-1, keepdims=True)
        d = y - mu
        var = jnp.mean(d * d, axis=-1, keepdims=True)
        yn = d * lax.rsqrt(var + NORM_EPS) * lg_ref[...] + lb_ref[...]
        a = yn * jax.nn.sigmoid(yn)
        o_ref[0, r * CONV_ROWS:(r + 1) * CONV_ROWS, :] = _rms(a, go_ref[...]).astype(BF16)


def _conv(v3, conv_w, conv_b, ln_g, ln_b, g_out):
    halo_per_tile = TL_CONV // CONV_HALO
    vec = pl.BlockSpec((1, CONV_CH), lambda b, i: (0, 0))
    return pl.pallas_call(
        _conv_kernel,
        grid=(BATCH, SEQ // TL_CONV),
        in_specs=[pl.BlockSpec((1, TL_CONV, CONV_CH), lambda b, i: (b, i, 0)),
                  pl.BlockSpec((1, CONV_HALO, CONV_CH),
                               lambda b, i: (b, jnp.maximum(i * halo_per_tile - 1, 0), 0)),
                  pl.BlockSpec((CONV_WIDTH + 1, CONV_CH), lambda b, i: (0, 0)),
                  vec, vec, vec, vec],
        out_specs=pl.BlockSpec((1, TL_CONV, CONV_CH), lambda b, i: (b, i, 0)),
        out_shape=jax.ShapeDtypeStruct((BATCH, SEQ, CONV_CH), BF16),
        scratch_shapes=[pltpu.VMEM((SUBLANES, TL_CONV + CONV_HALO, CONV_CH), F32)],
        compiler_params=_cparams(("parallel", "arbitrary")),
        name="conv_module",
    )(v3, v3, conv_w, conv_b, ln_g, ln_b, g_out)


S5_GROUP_ROWS = S5_CHUNKS + 8


S5_POW_ROWS = (S5_Q + 1 + SUBLANES - 1) // SUBLANES * SUBLANES
(S5_BB_RI, S5_BB_NIR, S5_BB_IR, S5_BB_RNI, S5_CC_RI, S5_CC_NIR, S5_N_PARAM) = range(7)


def _s5_kernel(ut_ref, pwr_ref, pwi_ref, par_ref, a_ref, bq_ref, bs_ref, yt_ref, sin_s, sp_s):
    phase = pl.program_id(0)
    g = pl.program_id(1)
    q = S5_Q
    n = 2 * SSM_STATE
    row0 = pl.multiple_of(g * S5_GROUP_ROWS, 8)

    def lam_pow(j):
        return pwr_ref[0, j:j + 1, :], pwi_ref[0, j:j + 1, :]

    @pl.when(phase == 0)
    def _():
        bb_ri, bb_nir = par_ref[0, S5_BB_RI], par_ref[0, S5_BB_NIR]
        bb_ir, bb_rni = par_ref[0, S5_BB_IR], par_ref[0, S5_BB_RNI]
        blk_q, blk_s = [], []
        for t in range(q):
            pr, pi_ = lam_pow(q - 1 - t)
            blk_q.append(pr * bb_ri + pi_ * bb_nir)
            blk_s.append(pr * bb_ir + pi_ * bb_rni)
        wst = jnp.concatenate([jnp.concatenate(blk_q, axis=0), jnp.concatenate(blk_s, axis=0)], axis=1)
        r = jnp.dot(ut_ref[0], wst.astype(BF16), preferred_element_type=F32)
        sin_s[0, pl.ds(row0, S5_CHUNKS), :] = r[:, :n]
        sin_s[1, pl.ds(row0, S5_CHUNKS), :] = r[:, n:]

    @pl.when((phase == 1) & (g == 0))
    def _():
        a = a_ref[...]
        bq = bq_ref[...]
        bs = bs_ref[...]

        def body(c, carry):
            nxt = []
            for b in range(BATCH):
                x, xs = carry[b]
                rows = pl.ds(b * S5_CHUNKS_PER_SEQ + c, SSM_GROUPS, stride=S5_GROUP_ROWS)
                sp_s[rows, :] = x
                nxt.append((a * x + bq * xs + sin_s[0, rows, :], a * xs + bs * x + sin_s[1, rows, :]))
            return tuple(nxt)

        z = jnp.zeros((SSM_GROUPS, n), F32)
        lax.fori_loop(0, S5_CHUNKS_PER_SEQ, body, tuple((z, z) for _ in range(BATCH)))

    @pl.when(phase == 1)
    def _():
        cc_ri, cc_nir = par_ref[0, S5_CC_RI], par_ref[0, S5_CC_NIR]
        cl = []
        for j in range(q + 1):
            pr, pi_ = lam_pow(j)
            cl.append(pr * cc_ri + pi_ * cc_nir)
        cl_lo = jnp.concatenate(cl[:q], axis=0)
        cl_hi = jnp.concatenate(cl[1:], axis=0)
        lane = lax.broadcasted_iota(jnp.int32, (1, n), 1)
        vgt = (cl_hi * jnp.where(lane < SSM_STATE, 1.0, -1.0)).astype(BF16)
        kt = lax.dot_general(par_ref[0, S5_BB_RNI], cl_lo, (((1,), (1,)), ((), ())),
                             preferred_element_type=F32, precision=lax.Precision.HIGHEST)
        padded = jnp.concatenate([jnp.zeros_like(kt), kt], axis=1)
        tg = jnp.concatenate(
            [padded[:, (q - t) * SSM_GROUP_CH:(q - t) * SSM_GROUP_CH + S5_QH] for t in range(q)],
            axis=0).astype(BF16)
        sp = sp_s[pl.ds(row0, S5_CHUNKS), :]
        y = jnp.dot(ut_ref[0], tg, preferred_element_type=F32)
        yt_ref[0] = y + lax.dot_general(sp.astype(BF16), vgt, (((1,), (1,)), ((), ())),
                                        preferred_element_type=F32)


def _s5(ut, pwr, pwi, params, a_cat, b_q, b_s):
    vec = pl.BlockSpec((SSM_GROUPS, 2 * SSM_STATE), lambda p, g: (0, 0))
    powers = pl.BlockSpec((1, S5_POW_ROWS, 2 * SSM_STATE), lambda p, g: (g, 0, 0))
    return pl.pallas_call(
        _s5_kernel,
        grid=(2, SSM_GROUPS),
        in_specs=[pl.BlockSpec((1, S5_CHUNKS, S5_QH), lambda p, g: (g, 0, 0)),
                  powers, powers,
                  pl.BlockSpec((1, S5_N_PARAM, SSM_GROUP_CH, 2 * SSM_STATE), lambda p, g: (g, 0, 0, 0)),
                  vec, vec, vec],
        out_specs=pl.BlockSpec((1, S5_CHUNKS, S5_QH), lambda p, g: (g * p, 0, 0)),
        out_shape=jax.ShapeDtypeStruct((SSM_GROUPS, S5_CHUNKS, S5_QH), F32),
        scratch_shapes=[pltpu.VMEM((2, SSM_GROUPS * S5_GROUP_ROWS, 2 * SSM_STATE), F32),
                        pltpu.VMEM((SSM_GROUPS * S5_GROUP_ROWS, 2 * SSM_STATE), F32)],
        compiler_params=_cparams(("arbitrary", "arbitrary")),
        name="s5_chunked",
    )(ut, pwr, pwi, params, a_cat, b_q, b_s)


def _s5_operators(a_re, a_im, log_dt, b_re, b_im, c_re, c_im):
    q = S5_Q
    dt = jnp.exp(log_dt)[:, None]
    ar, ai = a_re, a_im
    mag = jnp.exp(ar * dt)
    lr = mag * jnp.cos(ai * dt)
    li = mag * jnp.sin(ai * dt)
    den = ar * ar + ai * ai
    nr = lr - 1.0
    kr = (nr * ar + li * ai) / den
    ki = (li * ar - nr * ai) / den
    bbr = kr[..., None] * b_re - ki[..., None] * b_im
    bbi = kr[..., None] * b_im + ki[..., None] * b_re
    j = jnp.arange(q + 1, dtype=F32)[None, :, None]
    pmag = jnp.exp(ar[:, None, :] * dt[:, :, None] * j)
    pang = ai[:, None, :] * dt[:, :, None] * j
    pr = pmag * jnp.cos(pang)
    pi_ = pmag * jnp.sin(pang)
    pad = ((0, 0), (0, S5_POW_ROWS - (q + 1)), (0, 0))
    pwr = jnp.pad(jnp.concatenate([pr, pr], axis=-1), pad)
    pwi = jnp.pad(jnp.concatenate([pi_, pi_], axis=-1), pad)
    br_t = bbr.transpose(0, 2, 1)
    bi_t = bbi.transpose(0, 2, 1)
    cat = lambda a, b: jnp.concatenate([a, b], axis=-1)
    stack = [None] * S5_N_PARAM
    stack[S5_BB_RI] = cat(br_t, bi_t)
    stack[S5_BB_NIR] = cat(-bi_t, br_t)
    stack[S5_BB_IR] = cat(bi_t, br_t)
    stack[S5_BB_RNI] = cat(br_t, -bi_t)
    stack[S5_CC_RI] = cat(c_re, c_im)
    stack[S5_CC_NIR] = cat(-c_im, c_re)
    params = jnp.stack(stack, axis=1)
    aq_r, aq_i = pr[:, q], pi_[:, q]
    a_cat = cat(aq_r, aq_r)
    b_q = cat(-aq_i, aq_i)
    b_s = cat(aq_i, -aq_i)
    return pwr, pwi, params, a_cat, b_q, b_s


def _gelu_tanh(x):
    return 0.5 * x * (1.0 + jnp.tanh(math.sqrt(2.0 / math.pi) * (x + 0.044715 * (x * x * x))))


def _mix_kernel(an_ref, yt_ref, u_ref, x_ref, mod_ref, d_ref, wglu_ref, bglu_ref, gos_ref,
                woa_ref, wob_ref, gpm_ref, gpf_ref, wr_ref, br_ref, tri_ref,
                x1_ref, h2_ref, eidx_ref, rank_ref, gw_ref, cnt_ref, run_ref, tile_ref):
    i = pl.program_id(0)
    tm = TM_MIX

    @pl.when(i == 0)
    def _():
        run_ref[...] = jnp.zeros_like(run_ref)

    gt_m = mod_ref[0, 2:3, :]
    sh_f = mod_ref[0, 3:4, :]
    sc_f = mod_ref[0, 4:5, :]

    yy = _from_group_chunks(yt_ref, tile_ref) + d_ref[...] * u_ref[...]
    g = _gelu_tanh(yy)
    gl = jnp.dot(g.astype(BF16), wglu_ref[...], preferred_element_type=F32) + bglu_ref[...]
    ob = g * jax.nn.sigmoid(gl)
    bn = _rms(ob, gos_ref[...]).astype(BF16)
    o = (jnp.dot(an_ref[...], woa_ref[...], preferred_element_type=F32)
         + jnp.dot(bn, wob_ref[...], preferred_element_type=F32))
    x1 = x_ref[...] + gt_m * _rms(o, gpm_ref[...])
    x1_ref[...] = x1
    h2 = _rms(x1, gpf_ref[...]) * (1.0 + sc_f) + sh_f
    h2_ref[...] = _pack_halves(h2)

    logits = lax.dot_general(wr_ref[...], h2, (((1,), (1,)), ((), ())),
                             preferred_element_type=F32, precision=lax.Precision.HIGHEST)
    scores = jax.nn.sigmoid(logits)
    biased = scores + br_ref[...]
    ng = N_ROUTE_GROUPS
    gsz = N_EXPERTS // ng
    b3 = biased.reshape(ng, gsz, tm)
    s3 = scores.reshape(ng, gsz, tm)
    sub = lax.broadcasted_iota(jnp.int32, (ng, gsz, tm), 1).astype(F32)
    grp = lax.broadcasted_iota(jnp.int32, (ng, gsz, tm), 0).astype(F32)
    eid = grp * gsz + sub
    neg = -jnp.inf
    m1 = jnp.max(b3, axis=1, keepdims=True)
    i1 = jnp.min(jnp.where(b3 == m1, sub, float(gsz)), axis=1, keepdims=True)
    m2 = jnp.max(jnp.where(sub == i1, neg, b3), axis=1, keepdims=True)
    gs = m1 + m2
    gi = lax.broadcasted_iota(jnp.int32, (ng, 1, tm), 0)
    beaten = jnp.zeros((ng, 1, tm), F32)
    for gp in range(ng):
        o_ = gs[gp:gp + 1]
        beats = (o_ > gs) | ((o_ == gs) & (gi > gp))
        beaten = beaten + beats.astype(F32)
    gmask = beaten < float(TOPK_ROUTE_GROUPS)
    masked = jnp.where(gmask, b3, neg)

    sels = []
    picked = jnp.zeros((ng, gsz, tm), F32)
    for k in range(TOP_K):
        m = jnp.max(jnp.max(masked, axis=0, keepdims=True), axis=1, keepdims=True)
        cand = jnp.where(masked == m, eid, float(N_EXPERTS))
        sel = jnp.min(jnp.min(cand, axis=0, keepdims=True), axis=1, keepdims=True)
        oh = eid == sel
        masked = jnp.where(oh, neg, masked)
        picked = jnp.where(oh, 1.0, picked)
        sels.append(sel)

    pm = picked.reshape(N_EXPERTS, tm)
    prefix = jnp.dot(pm.astype(BF16), tri_ref[...], preferred_element_type=F32) + run_ref[:, 0:1]
    p3 = prefix.reshape(ng, gsz, tm)
    run_new = run_ref[...] + jnp.sum(pm, axis=1, keepdims=True)
    run_ref[...] = run_new
    cnt_ref[...] = run_new

    sc_rows = []
    for k in range(TOP_K):
        oh = eid == sels[k]
        sc_k = jnp.sum(jnp.sum(jnp.where(oh, s3, 0.0), axis=0, keepdims=True), axis=1, keepdims=True)
        rk_k = jnp.sum(jnp.sum(jnp.where(oh, p3, 0.0), axis=0, keepdims=True), axis=1, keepdims=True)
        sc_rows.append(sc_k)
        eidx_ref[k:k + 1, :] = sels[k].reshape(1, tm).astype(jnp.int32)
        rank_ref[k:k + 1, :] = rk_k.reshape(1, tm).astype(jnp.int32)
    tot = sc_rows[0]
    for k in range(1, TOP_K):
        tot = tot + sc_rows[k]
    inv = ROUTED_SCALE / (tot + 1e-20)
    for k in range(TOP_K):
        gw_ref[k:k + 1, :] = (sc_rows[k] * inv).reshape(1, tm)


def _mix(half, a_n, yt, u2, x2, mod, d_skip, wglu_bf, b_glu, g_out_ssm, wo_a, wo_b, g_post_mix,
         g_pre_ffn, w_router_t, b_router_col, tri):
    tm = TM_MIX
    t0 = half * (HALF_TOK // tm)
    row = lambda n: pl.BlockSpec((1, n), lambda i: (0, 0))
    full = lambda a, b: pl.BlockSpec((a, b), lambda i: (0, 0))
    tok_in = lambda n: pl.BlockSpec((tm, n), lambda i: (t0 + i, 0))
    tok = lambda n: pl.BlockSpec((tm, n), lambda i: (i, 0))
    col = pl.BlockSpec((TOP_K, tm), lambda i: (0, i))
    return pl.pallas_call(
        _mix_kernel,
        grid=(HALF_TOK // tm,),
        in_specs=[tok_in(CONV_CH),
                  pl.BlockSpec((SSM_GROUPS, tm // S5_Q, S5_QH), lambda i: (0, t0 + i, 0)),
                  tok_in(SSM_CH), tok_in(D_MODEL),
                  pl.BlockSpec((1, 8, D_MODEL), lambda i: (half, 0, 0)),
                  row(SSM_CH), full(SSM_CH, SSM_CH), row(SSM_CH), row(SSM_CH),
                  full(CONV_CH, D_MODEL), full(SSM_CH, D_MODEL), row(D_MODEL), row(D_MODEL),
                  full(N_EXPERTS, D_MODEL), full(N_EXPERTS, 1), full(tm, tm)],
        out_specs=[tok(D_MODEL), tok(D_MODEL // 2), col, col, col,
                   pl.BlockSpec((N_EXPERTS, 128), lambda i: (0, 0))],
        out_shape=[jax.ShapeDtypeStruct((HALF_TOK, D_MODEL), F32),
                   jax.ShapeDtypeStruct((HALF_TOK, D_MODEL // 2), jnp.uint32),
                   jax.ShapeDtypeStruct((TOP_K, HALF_TOK), jnp.int32),
                   jax.ShapeDtypeStruct((TOP_K, HALF_TOK), jnp.int32),
                   jax.ShapeDtypeStruct((TOP_K, HALF_TOK), F32),
                   jax.ShapeDtypeStruct((N_EXPERTS, 128), F32)],
        scratch_shapes=[pltpu.VMEM((N_EXPERTS, 128), F32),
                        pltpu.VMEM((SSM_CH // LANES, tm, LANES), F32)],
        compiler_params=_cparams(("arbitrary",)),
        name="mix_out_router",
    )(a_n, yt, u2, x2, mod, d_skip, wglu_bf, b_glu, g_out_ssm, wo_a, wo_b, g_post_mix,
      g_pre_ffn, w_router_t, b_router_col, tri)


def _expert_kernel(blk0_ref, nblk_ref, xs_hbm, wg_ref, wu_ref, wd_ref, ys_hbm,
                   xbuf, ybuf, wgu_s, wd_s, sem_in, sem_out):
    e = pl.program_id(0)
    n = nblk_ref[e]
    b0 = blk0_ref[e]
    n_all = blk0_ref[N_EXPERTS - 1] + nblk_ref[N_EXPERTS - 1]

    def rows(b):
        return pl.ds(pl.multiple_of(b * ROW_BLOCK, ROW_BLOCK), ROW_BLOCK)

    def in_copy(b, slot):
        return pltpu.make_async_copy(xs_hbm.at[rows(b)], xbuf.at[slot], sem_in.at[slot])

    def out_copy(b, slot):
        return pltpu.make_async_copy(ybuf.at[slot], ys_hbm.at[rows(b)], sem_out.at[slot])

    for b in range(EXPERT_AHEAD):
        @pl.when((e == 0) & (b < n_all))
        def _():
            in_copy(b, b).start()

    wgu_s[:, :D_EXPERT] = wg_ref[0].astype(BF16)
    wgu_s[:, D_EXPERT:] = wu_ref[0].astype(BF16)
    wd_s[...] = wd_ref[0].astype(BF16)

    def admit(b):
        in_copy(b, b % EXPERT_SLOTS).wait()
        ahead = b + EXPERT_AHEAD

        @pl.when(ahead < n_all)
        def _():
            in_copy(ahead, ahead % EXPERT_SLOTS).start()

        @pl.when(b >= EXPERT_SLOTS)
        def _():
            out_copy(b - EXPERT_SLOTS, b % EXPERT_SLOTS).wait()

    def compute(b):
        slot = b % EXPERT_SLOTS
        x_lo, x_hi = _unpack_halves(xbuf[slot])
        x = jnp.concatenate([x_lo.astype(BF16), x_hi.astype(BF16)], axis=1)
        h = jnp.dot(x, wgu_s[...], preferred_element_type=F32)
        hg = h[:, :D_EXPERT]
        act = hg * jax.nn.sigmoid(hg) * h[:, D_EXPERT:]
        ybuf[slot] = _pack_halves(jnp.dot(act.astype(BF16), wd_s[...], preferred_element_type=F32))

    def run(blocks):
        for b in blocks:
            admit(b)
        for b in blocks:
            compute(b)
        for b in blocks:
            out_copy(b, b % EXPERT_SLOTS).start(priority=1)

    def pair(i, carry):
        b = b0 + 2 * i
        run([b, b + 1])
        return carry

    lax.fori_loop(0, n // 2, pair, 0)

    @pl.when(n % 2 == 1)
    def _():
        run([b0 + n - 1])

    @pl.when(e == N_EXPERTS - 1)
    def _():
        for j in range(1, EXPERT_SLOTS + 1):
            @pl.when(n_all >= j)
            def _():
                out_copy(n_all - j, (n_all - j) % EXPERT_SLOTS).wait()


def _experts(blk0, nblk, xs, we_gate, we_up, we_down):
    any_spec = pl.BlockSpec(memory_space=pl.ANY)
    grid_spec = pltpu.PrefetchScalarGridSpec(
        num_scalar_prefetch=2,
        grid=(N_EXPERTS,),
        in_specs=[any_spec,
                  pl.BlockSpec((1, D_MODEL, D_EXPERT), lambda e, b0, nb: (e, 0, 0)),
                  pl.BlockSpec((1, D_MODEL, D_EXPERT), lambda e, b0, nb: (e, 0, 0)),
                  pl.BlockSpec((1, D_EXPERT, D_MODEL), lambda e, b0, nb: (e, 0, 0))],
        out_specs=any_spec,
        scratch_shapes=[pltpu.VMEM((EXPERT_SLOTS, ROW_BLOCK, D_MODEL // 2), jnp.uint32),
                        pltpu.VMEM((EXPERT_SLOTS, ROW_BLOCK, D_MODEL // 2), jnp.uint32),
                        pltpu.VMEM((D_MODEL, 2 * D_EXPERT), BF16),
                        pltpu.VMEM((D_EXPERT, D_MODEL), BF16),
                        pltpu.SemaphoreType.DMA((EXPERT_SLOTS,)),
                        pltpu.SemaphoreType.DMA((EXPERT_SLOTS,))],
    )
    return pl.pallas_call(
        _expert_kernel,
        grid_spec=grid_spec,
        out_shape=jax.ShapeDtypeStruct((N_ROWS, D_MODEL // 2), jnp.uint32),
        compiler_params=_cparams(("arbitrary",)),
        name="routed_experts",
    )(blk0, nblk, xs, we_gate, we_up, we_down)


def _final_kernel(h2_ref, yg_ref, gw_ref, x1_ref, mod_ref, wgu_ref, wd_ref, g_ref, *rest):
    o_ref = rest[-1]
    half = D_MODEL // 2
    gt_f = mod_ref[0, 5:6, :]
    x_lo, x_hi = _unpack_halves(h2_ref[...])
    h = (jnp.dot(x_lo.astype(BF16), wgu_ref[:half, :], preferred_element_type=F32)
         + jnp.dot(x_hi.astype(BF16), wgu_ref[half:, :], preferred_element_type=F32))
    hg = h[:, :D_EXPERT]
    act = hg * jax.nn.sigmoid(hg) * h[:, D_EXPERT:]
    shared = jnp.dot(act.astype(BF16), wd_ref[...], preferred_element_type=F32)
    y_lo = shared[:, :half]
    y_hi = shared[:, half:]
    for k in range(TOP_K):
        r_lo, r_hi = _unpack_halves(yg_ref[k])
        w = gw_ref[:, k:k + 1]
        y_lo = y_lo + w * r_lo
        y_hi = y_hi + w * r_hi
    ms = (jnp.sum(y_lo * y_lo, axis=-1, keepdims=True)
          + jnp.sum(y_hi * y_hi, axis=-1, keepdims=True)) * (1.0 / D_MODEL)
    inv = lax.rsqrt(ms + NORM_EPS)
    o_ref[:, :half] = x1_ref[:, :half] + gt_f[:, :half] * (y_lo * inv * g_ref[:, :half])
    o_ref[:, half:] = x1_ref[:, half:] + gt_f[:, half:] * (y_hi * inv * g_ref[:, half:])


def _final(half, out_prev, h2p, yg, gw_t, x1, mod, ws_gu, ws_d, g_post_ffn):
    tm = TM_OUT
    t0 = half * (HALF_TOK // tm)
    tok = pl.BlockSpec((tm, D_MODEL), lambda i: (i, 0))
    in_specs = [pl.BlockSpec((tm, D_MODEL // 2), lambda i: (i, 0)),
                pl.BlockSpec((TOP_K, tm, D_MODEL // 2), lambda i: (0, i, 0)),
                pl.BlockSpec((tm, TOP_K), lambda i: (i, 0)),
                tok,
                pl.BlockSpec((1, 8, D_MODEL), lambda i: (half, 0, 0)),
                pl.BlockSpec((D_MODEL, 2 * D_EXPERT), lambda i: (0, 0)),
                pl.BlockSpec((D_EXPERT, D_MODEL), lambda i: (0, 0)),
                pl.BlockSpec((1, D_MODEL), lambda i: (0, 0))]
    args = [h2p, yg, gw_t, x1, mod, ws_gu, ws_d, g_post_ffn]
    aliases = {}
    if out_prev is not None:
        aliases = {len(args): 0}
        in_specs.append(pl.BlockSpec(memory_space=pl.ANY))
        args.append(out_prev)
    return pl.pallas_call(
        _final_kernel,
        grid=(HALF_TOK // tm,),
        in_specs=in_specs,
        out_specs=pl.BlockSpec((tm, D_MODEL), lambda i: (t0 + i, 0)),
        out_shape=jax.ShapeDtypeStruct((N_TOK, D_MODEL), F32),
        input_output_aliases=aliases,
        compiler_params=_cparams(("parallel",)),
        name="shared_final",
    )(*args)


def _sc_worker_id():
    return lax.axis_index("s") * SC_CORES + lax.axis_index("c")


def _dispatch_body(h_hbm, dest_hbm, xs_hbm, idx_v, rows_v, sem_l, sem_s):
    n = SC_CHUNKS_PER_WORKER
    c0 = _sc_worker_id() * n

    def load(i, b):
        return pltpu.async_copy(h_hbm.at[pl.ds((c0 + i) * SC_W, SC_W)], rows_v.at[b], sem_l.at[b])

    loads = [None] * n
    scat = [None] * n
    loads[0] = load(0, 0)
    for i in range(n):
        b = i % 2
        pltpu.sync_copy(dest_hbm.at[c0 + i], idx_v.at[b])
        loads[i].wait()
        if i + 1 < n:
            if i >= 1:
                for d in scat[i - 1]:
                    d.wait()
            loads[i + 1] = load(i + 1, 1 - b)
        scat[i] = [pltpu.async_copy(rows_v.at[b], xs_hbm.at[idx_v.at[b].at[k]], sem_s.at[b])
                   for k in range(TOP_K)]
    for i in (n - 2, n - 1):
        for d in scat[i]:
            d.wait()


def _sc_dispatch(h2p, dest3):
    mesh = plsc.VectorSubcoreMesh(core_axis_name="c", subcore_axis_name="s")
    return pl.kernel(
        _dispatch_body, mesh=mesh,
        out_type=jax.ShapeDtypeStruct((N_ROWS, D_MODEL // 2), jnp.uint32),
        scratch_types=[pltpu.VMEM((2, TOP_K, SC_W), jnp.int32),
                       pltpu.VMEM((2, SC_W, D_MODEL // 2), jnp.uint32),
                       pltpu.SemaphoreType.DMA((2,)), pltpu.SemaphoreType.DMA((2,))],
    )(h2p, dest3)


def _combine_body(ys_hbm, dest_hbm, yg_hbm, idx_v, rows_v, sem_g, sem_w):
    c0 = _sc_worker_id() * SC_CHUNKS_PER_WORKER

    @pl.loop(0, SC_CHUNKS_PER_WORKER)
    def _(i):
        c = c0 + i
        pltpu.sync_copy(dest_hbm.at[c], idx_v)
        g = [None] * TOP_K
        w = [None] * TOP_K
        g[0] = pltpu.async_copy(ys_hbm.at[idx_v.at[0]], rows_v.at[0], sem_g.at[0])
        for k in range(TOP_K):
            b = k % 2
            g[k].wait()
            if k + 1 < TOP_K:
                if k >= 1:
                    w[k - 1].wait()
                g[k + 1] = pltpu.async_copy(ys_hbm.at[idx_v.at[k + 1]], rows_v.at[1 - b], sem_g.at[1 - b])
            w[k] = pltpu.async_copy(rows_v.at[b], yg_hbm.at[k].at[pl.ds(c * SC_W, SC_W)], sem_w.at[b])
        w[TOP_K - 2].wait()
        w[TOP_K - 1].wait()


def _sc_combine(ysp, dest3):
    mesh = plsc.VectorSubcoreMesh(core_axis_name="c", subcore_axis_name="s")
    return pl.kernel(
        _combine_body, mesh=mesh,
        out_type=jax.ShapeDtypeStruct((TOP_K, HALF_TOK, D_MODEL // 2), jnp.uint32),
        scratch_types=[pltpu.VMEM((TOP_K, SC_W), jnp.int32),
                       pltpu.VMEM((2, SC_W, D_MODEL // 2), jnp.uint32),
                       pltpu.SemaphoreType.DMA((2,)), pltpu.SemaphoreType.DMA((2,))],
    )(ysp, dest3)


def kernel(x, c, w_ada, b_ada, g_pre_mix, g_post_mix, w_in, conv_w, conv_b, conv_ln_g, conv_ln_b,
           ssm_a_re, ssm_a_im, ssm_log_dt, ssm_b_re, ssm_b_im, ssm_c_re, ssm_c_im, ssm_d,
           ssm_w_glu, ssm_b_glu, g_out_conv, g_out_ssm, w_out, g_pre_ffn, g_post_ffn,
           w_router, b_router, we_gate, we_up, we_down, ws_gate, ws_up, ws_down):
    l = 0
    x2 = x.reshape(N_TOK, D_MODEL)
    r1 = lambda a: a.reshape(1, -1)

    c_pad = jnp.zeros((8, D_MODEL), F32).at[:BATCH].set(c)
    mod = _ada(c_pad, w_ada[l], r1(b_ada[l]))[:BATCH].reshape(BATCH, 6, D_MODEL)
    mod = jnp.concatenate([mod, jnp.zeros((BATCH, 2, D_MODEL), F32)], axis=1)

    v, u, ut = _inproj(x2, mod, r1(g_pre_mix[l]), w_in[l].astype(BF16))
    cw = jnp.concatenate([conv_w[l].reshape(CONV_WIDTH, CONV_CH), jnp.zeros((1, CONV_CH), F32)], axis=0)
    a_n = _conv(v.reshape(BATCH, SEQ, CONV_CH), cw, r1(conv_b[l]), r1(conv_ln_g[l]),
                r1(conv_ln_b[l]), r1(g_out_conv[l])).reshape(N_TOK, CONV_CH)

    pwr, pwi, s5_params, a_cat, b_q, b_s = _s5_operators(
        ssm_a_re[l], ssm_a_im[l], ssm_log_dt[l], ssm_b_re[l], ssm_b_im[l], ssm_c_re[l], ssm_c_im[l])
    yt = _s5(ut, pwr, pwi, s5_params, a_cat, b_q, b_s)

    tm = TM_MIX
    tri = (jnp.arange(tm)[:, None] < jnp.arange(tm)[None, :]).astype(BF16)
    wo = w_out[l].astype(BF16)
    mix_params = (r1(ssm_d[l]), ssm_w_glu[l].astype(BF16), r1(ssm_b_glu[l]), r1(g_out_ssm[l]),
                  wo[:CONV_CH], wo[CONV_CH:], r1(g_post_mix[l]), r1(g_pre_ffn[l]),
                  w_router[l].T, b_router[l].reshape(N_EXPERTS, 1), tri)
    ws_gu = jnp.concatenate([ws_gate[l], ws_up[l]], axis=1).astype(BF16)
    ws_d = ws_down[l].astype(BF16)
    e_ids = jnp.arange(N_EXPERTS, dtype=jnp.int32)

    out = None
    for half in range(N_HALVES):
        x1, h2, eidx, rank, gw, cnt = _mix(half, a_n, yt, u, x2, mod, *mix_params)
        counts = cnt[:, 0].astype(jnp.int32)
        padded = (counts + ROW_BLOCK - 1) // ROW_BLOCK * ROW_BLOCK
        pstart = jnp.cumsum(padded) - padded
        dest = rank + jnp.sum(jnp.where(eidx[..., None] == e_ids, pstart, 0), axis=-1)
        dest3 = dest.reshape(TOP_K, HALF_TOK // SC_W, SC_W).transpose(1, 0, 2)

        xs = _sc_dispatch(h2, dest3)
        ys = _experts(pstart // ROW_BLOCK, padded // ROW_BLOCK, xs, we_gate[l], we_up[l], we_down[l])
        yg = _sc_combine(ys, dest3)
        out = _final(half, out, h2, yg, gw.T, x1, mod, ws_gu, ws_d, r1(g_post_ffn[l]))
    return out.reshape(BATCH, SEQ, D_MODEL)
```

```python
import functools
import math

import jax
import jax.numpy as jnp
from jax import lax
from jax.experimental import pallas as pl
from jax.experimental.pallas import tpu as pltpu
from jax.experimental.pallas import tpu_sc as plsc

F32 = jnp.float32
BF16 = jnp.bfloat16

D_MODEL = 1024
BATCH = 2
SEQ = 8192
N_TOK = BATCH * SEQ
CONV_CH = 512
CONV_WIDTH = 31
SSM_CH = 512
SSM_GROUP_CH = 16
SSM_GROUPS = 32
SSM_STATE = 64
D_IN = 2 * CONV_CH + SSM_CH
N_EXPERTS = 64
TOP_K = 8
N_ROUTE_GROUPS = 8
TOPK_ROUTE_GROUPS = 4
D_EXPERT = 256
ROUTED_SCALE = 2.5
NORM_EPS = 1e-6

SUBLANES = 8
LANES = 128

TM_IN = 512
IN_SUBTILES = 2
TL_CONV = 512
CONV_HALO = 32
CONV_ROWS = 64
S5_Q = 32
S5_QH = S5_Q * SSM_GROUP_CH
S5_CHUNKS = N_TOK // S5_Q
S5_CHUNKS_PER_SEQ = SEQ // S5_Q
TM_MIX = 1024
MIX_SUBTILES = 1
ROW_BLOCK = 256
EXPERT_AHEAD = 2
EXPERT_SLOTS = 4
HALF_TOK = SEQ
N_HALVES = N_TOK // HALF_TOK
N_BLOCKS = HALF_TOK * TOP_K // ROW_BLOCK + N_EXPERTS
N_ROWS = N_BLOCKS * ROW_BLOCK
TM_OUT = 512
SC_CORES = 2
SC_SUBCORES = 16
SC_WORKERS = SC_CORES * SC_SUBCORES
SC_W = 64
SC_CHUNKS_PER_WORKER = HALF_TOK // (SC_WORKERS * SC_W)
VMEM_LIMIT = 48 * 1024 * 1024


def _cparams(sem):
    return pltpu.CompilerParams(dimension_semantics=sem, vmem_limit_bytes=VMEM_LIMIT)


def _pack_halves(x):
    n = x.shape[-1] // 2
    lo = lax.bitcast_convert_type(x[:, :n].astype(BF16).astype(F32), jnp.uint32)
    hi = lax.bitcast_convert_type(x[:, n:].astype(BF16).astype(F32), jnp.uint32)
    return hi | (lo >> 16)


def _unpack_halves(p):
    lo = lax.bitcast_convert_type(p << 16, F32)
    hi = lax.bitcast_convert_type(p & jnp.uint32(0xFFFF0000), F32)
    return lo, hi


def _rms(x, g):
    return x * lax.rsqrt(jnp.mean(x * x, axis=-1, keepdims=True) + NORM_EPS) * g


def _ada_kernel(c_ref, w_ref, b_ref, o_ref):
    c = c_ref[...]
    a = c * jax.nn.sigmoid(c)
    o_ref[...] = jnp.dot(a, w_ref[...], preferred_element_type=F32,
                         precision=lax.Precision.HIGHEST) + b_ref[...]


def _ada(c_pad, w_ada, b_ada):
    n = w_ada.shape[1]
    bn = 1536
    return pl.pallas_call(
        _ada_kernel,
        grid=(n // bn,),
        in_specs=[pl.BlockSpec((8, D_MODEL), lambda j: (0, 0)),
                  pl.BlockSpec((D_MODEL, bn), lambda j: (0, j)),
                  pl.BlockSpec((1, bn), lambda j: (0, j))],
        out_specs=pl.BlockSpec((8, bn), lambda j: (0, j)),
        out_shape=jax.ShapeDtypeStruct((8, n), F32),
        compiler_params=_cparams(("arbitrary",)),
        name="ada_mod",
    )(c_pad, w_ada, b_ada)


GROUPS_PER_LANE_TILE = LANES // SSM_GROUP_CH


def _to_group_chunks(u, tile_ref, ut_ref):
    n_chunks = u.shape[0] // S5_Q
    for j in range(SSM_CH // LANES):
        tile_ref[j] = u[:, LANES * j:LANES * (j + 1)]
    for j in range(SSM_CH // LANES):
        rows_t = [tile_ref[j, pl.ds(t, n_chunks, stride=S5_Q), :] for t in range(S5_Q)]
        for gg in range(GROUPS_PER_LANE_TILE):
            lo = gg * SSM_GROUP_CH
            row = jnp.concatenate([r[:, lo:lo + SSM_GROUP_CH] for r in rows_t], axis=1)
            ut_ref[j * GROUPS_PER_LANE_TILE + gg] = row.astype(ut_ref.dtype)


def _from_group_chunks(yt_ref, tile_ref):
    n_chunks = yt_ref.shape[1]
    for j in range(SSM_CH // LANES):
        for t in range(S5_Q):
            lo = t * SSM_GROUP_CH
            piece = jnp.concatenate(
                [yt_ref[j * GROUPS_PER_LANE_TILE + gg, :, lo:lo + SSM_GROUP_CH]
                 for gg in range(GROUPS_PER_LANE_TILE)], axis=1)
            tile_ref[j, pl.ds(t, n_chunks, stride=S5_Q), :] = piece
    return jnp.concatenate([tile_ref[j] for j in range(SSM_CH // LANES)], axis=1)


def _inproj_kernel(x_ref, mod_ref, g_ref, w_ref, v_ref, u_ref, ut_ref, tile_ref):
    sh = mod_ref[0, 0:1, :]
    sc = mod_ref[0, 1:2, :]
    sub = TM_IN // IN_SUBTILES
    sub_chunks = sub // S5_Q
    for s in range(IN_SUBTILES):
        r = slice(s * sub, (s + 1) * sub)
        h = _rms(x_ref[r, :], g_ref[...]) * (1.0 + sc) + sh
        z = jnp.dot(h.astype(BF16), w_ref[...], preferred_element_type=F32)
        v_ref[r, :] = z[:, :CONV_CH] * jax.nn.sigmoid(z[:, CONV_CH:2 * CONV_CH])
        u = z[:, 2 * CONV_CH:]
        u_ref[r, :] = u
        _to_group_chunks(u, tile_ref.at[s], ut_ref.at[:, s * sub_chunks:(s + 1) * sub_chunks, :])


def _inproj(x2, mod, g_pre, w_in_bf):
    tiles_per_seq = SEQ // TM_IN
    return pl.pallas_call(
        _inproj_kernel,
        grid=(N_TOK // TM_IN,),
        in_specs=[pl.BlockSpec((TM_IN, D_MODEL), lambda i: (i, 0)),
                  pl.BlockSpec((1, 8, D_MODEL), lambda i: (i // tiles_per_seq, 0, 0)),
                  pl.BlockSpec((1, D_MODEL), lambda i: (0, 0)),
                  pl.BlockSpec((D_MODEL, D_IN), lambda i: (0, 0))],
        out_specs=[pl.BlockSpec((TM_IN, CONV_CH), lambda i: (i, 0)),
                   pl.BlockSpec((TM_IN, SSM_CH), lambda i: (i, 0)),
                   pl.BlockSpec((SSM_GROUPS, TM_IN // S5_Q, S5_QH), lambda i: (0, i, 0))],
        out_shape=[jax.ShapeDtypeStruct((N_TOK, CONV_CH), F32),
                   jax.ShapeDtypeStruct((N_TOK, SSM_CH), F32),
                   jax.ShapeDtypeStruct((SSM_GROUPS, S5_CHUNKS, S5_QH), BF16)],
        scratch_shapes=[pltpu.VMEM((IN_SUBTILES, SSM_CH // LANES, TM_IN // IN_SUBTILES, LANES), F32)],
        compiler_params=_cparams(("parallel",)),
        name="in_proj",
    )(x2, mod, g_pre, w_in_bf)


def _conv_kernel(vc_ref, vp_ref, w_ref, cb_ref, lg_ref, lb_ref, go_ref, o_ref, sh_ref):
    i = pl.program_id(1)
    keep = (i > 0).astype(F32)
    n_ext = TL_CONV + CONV_HALO
    sh_ref[0, 0:CONV_HALO, :] = vp_ref[0] * keep
    sh_ref[0, CONV_HALO:, :] = vc_ref[0]
    for s in range(1, SUBLANES):
        sh_ref[s, 0:n_ext - s, :] = sh_ref[0, s:n_ext, :]
    off = CONV_HALO - (CONV_WIDTH - 1)
    for r in range(TL_CONV // CONV_ROWS):
        acc = None
        for j in range(CONV_WIDTH):
            s = (off + j) % SUBLANES
            al = r * CONV_ROWS + (off + j) - s
            term = w_ref[j:j + 1, :] * sh_ref[s, al:al + CONV_ROWS, :]
            acc = term if acc is None else acc + term
        y = acc + cb_ref[...]
        mu = jnp.mean(y, axis=-1, keepdims=True)
        d = y - mu
        var = jnp.mean(d * d, axis=-1, keepdims=True)
        yn = d * lax.rsqrt(var + NORM_EPS) * lg_ref[...] + lb_ref[...]
        a = yn * jax.nn.sigmoid(yn)
        o_ref[0, r * CONV_ROWS:(r + 1) * CONV_ROWS, :] = _rms(a, go_ref[...]).astype(BF16)


def _conv(v3, conv_w, conv_b, ln_g, ln_b, g_out):
    halo_per_tile = TL_CONV // CONV_HALO
    vec = pl.BlockSpec((1, CONV_CH), lambda b, i: (0, 0))
    return pl.pallas_call(
        _conv_kernel,
        grid=(BATCH, SEQ // TL_CONV),
        in_specs=[pl.BlockSpec((1, TL_CONV, CONV_CH), lambda b, i: (b, i, 0)),
                  pl.BlockSpec((1, CONV_HALO, CONV_CH),
                               lambda b, i: (b, jnp.maximum(i * halo_per_tile - 1, 0), 0)),
                  pl.BlockSpec((CONV_WIDTH + 1, CONV_CH), lambda b, i: (0, 0)),
                  vec, vec, vec, vec],
        out_specs=pl.BlockSpec((1, TL_CONV, CONV_CH), lambda b, i: (b, i, 0)),
        out_shape=jax.ShapeDtypeStruct((BATCH, SEQ, CONV_CH), BF16),
        scratch_shapes=[pltpu.VMEM((SUBLANES, TL_CONV + CONV_HALO, CONV_CH), F32)],
        compiler_params=_cparams(("parallel", "arbitrary")),
        name="conv_module",
    )(v3, v3, conv_w, conv_b, ln_g, ln_b, g_out)


S5_GROUP_ROWS = S5_CHUNKS + 8


S5_POW_ROWS = (S5_Q + 1 + SUBLANES - 1) // SUBLANES * SUBLANES
(S5_BB_RI, S5_BB_NIR, S5_BB_IR, S5_BB_RNI, S5_CC_RI, S5_CC_NIR, S5_N_PARAM) = range(7)


def _s5_kernel(ut_ref, pwr_ref, pwi_ref, par_ref, a_ref, bq_ref, bs_ref, yt_ref, sin_s, sp_s):
    phase = pl.program_id(0)
    g = pl.program_id(1)
    q = S5_Q
    n = 2 * SSM_STATE
    row0 = pl.multiple_of(g * S5_GROUP_ROWS, 8)

    def lam_pow(j):
        return pwr_ref[0, j:j + 1, :], pwi_ref[0, j:j + 1, :]

    @pl.when(phase == 0)
    def _():
        bb_ri, bb_nir = par_ref[0, S5_BB_RI], par_ref[0, S5_BB_NIR]
        bb_ir, bb_rni = par_ref[0, S5_BB_IR], par_ref[0, S5_BB_RNI]
        blk_q, blk_s = [], []
        for t in range(q):
            pr, pi_ = lam_pow(q - 1 - t)
            blk_q.append(pr * bb_ri + pi_ * bb_nir)
            blk_s.append(pr * bb_ir + pi_ * bb_rni)
        wst = jnp.concatenate([jnp.concatenate(blk_q, axis=0), jnp.concatenate(blk_s, axis=0)], axis=1)
        r = jnp.dot(ut_ref[0], wst.astype(BF16), preferred_element_type=F32)
        sin_s[0, pl.ds(row0, S5_CHUNKS), :] = r[:, :n]
        sin_s[1, pl.ds(row0, S5_CHUNKS), :] = r[:, n:]

    @pl.when((phase == 1) & (g == 0))
    def _():
        a = a_ref[...]
        bq = bq_ref[...]
        bs = bs_ref[...]

        def body(c, carry):
            nxt = []
            for b in range(BATCH):
                x, xs = carry[b]
                rows = pl.ds(b * S5_CHUNKS_PER_SEQ + c, SSM_GROUPS, stride=S5_GROUP_ROWS)
                sp_s[rows, :] = x
                nxt.append((a * x + bq * xs + sin_s[0, rows, :], a * xs + bs * x + sin_s[1, rows, :]))
            return tuple(nxt)

        z = jnp.zeros((SSM_GROUPS, n), F32)
        lax.fori_loop(0, S5_CHUNKS_PER_SEQ, body, tuple((z, z) for _ in range(BATCH)))

    @pl.when(phase == 1)
    def _():
        cc_ri, cc_nir = par_ref[0, S5_CC_RI], par_ref[0, S5_CC_NIR]
        cl = []
        for j in range(q + 1):
            pr, pi_ = lam_pow(j)
            cl.append(pr * cc_ri + pi_ * cc_nir)
        cl_lo = jnp.concatenate(cl[:q], axis=0)
        cl_hi = jnp.concatenate(cl[1:], axis=0)
        lane = lax.broadcasted_iota(jnp.int32, (1, n), 1)
        vgt = (cl_hi * jnp.where(lane < SSM_STATE, 1.0, -1.0)).astype(BF16)
        kt = lax.dot_general(par_ref[0, S5_BB_RNI], cl_lo, (((1,), (1,)), ((), ())),
                             preferred_element_type=F32, precision=lax.Precision.HIGHEST)
        padded = jnp.concatenate([jnp.zeros_like(kt), kt], axis=1)
        tg = jnp.concatenate(
            [padded[:, (q - t) * SSM_GROUP_CH:(q - t) * SSM_GROUP_CH + S5_QH] for t in range(q)],
            axis=0).astype(BF16)
        sp = sp_s[pl.ds(row0, S5_CHUNKS), :]
        y = jnp.dot(ut_ref[0], tg, preferred_element_type=F32)
        yt_ref[0] = y + lax.dot_general(sp.astype(BF16), vgt, (((1,), (1,)), ((), ())),
                                        preferred_element_type=F32)


def _s5(ut, pwr, pwi, params, a_cat, b_q, b_s):
    vec = pl.BlockSpec((SSM_GROUPS, 2 * SSM_STATE), lambda p, g: (0, 0))
    powers = pl.BlockSpec((1, S5_POW_ROWS, 2 * SSM_STATE), lambda p, g: (g, 0, 0))
    return pl.pallas_call(
        _s5_kernel,
        grid=(2, SSM_GROUPS),
        in_specs=[pl.BlockSpec((1, S5_CHUNKS, S5_QH), lambda p, g: (g, 0, 0)),
                  powers, powers,
                  pl.BlockSpec((1, S5_N_PARAM, SSM_GROUP_CH, 2 * SSM_STATE), lambda p, g: (g, 0, 0, 0)),
                  vec, vec, vec],
        out_specs=pl.BlockSpec((1, S5_CHUNKS, S5_QH), lambda p, g: (g * p, 0, 0)),
        out_shape=jax.ShapeDtypeStruct((SSM_GROUPS, S5_CHUNKS, S5_QH), F32),
        scratch_shapes=[pltpu.VMEM((2, SSM_GROUPS * S5_GROUP_ROWS, 2 * SSM_STATE), F32),
                        pltpu.VMEM((SSM_GROUPS * S5_GROUP_ROWS, 2 * SSM_STATE), F32)],
        compiler_params=_cparams(("arbitrary", "arbitrary")),
        name="s5_chunked",
    )(ut, pwr, pwi, params, a_cat, b_q, b_s)


def _s5_operators(a_re, a_im, log_dt, b_re, b_im, c_re, c_im):
    q = S5_Q
    dt = jnp.exp(log_dt)[:, None]
    ar, ai = a_re, a_im
    mag = jnp.exp(ar * dt)
    lr = mag * jnp.cos(ai * dt)
    li = mag * jnp.sin(ai * dt)
    den = ar * ar + ai * ai
    nr = lr - 1.0
    kr = (nr * ar + li * ai) / den
    ki = (li * ar - nr * ai) / den
    bbr = kr[..., None] * b_re - ki[..., None] * b_im
    bbi = kr[..., None] * b_im + ki[..., None] * b_re
    j = jnp.arange(q + 1, dtype=F32)[None, :, None]
    pmag = jnp.exp(ar[:, None, :] * dt[:, :, None] * j)
    pang = ai[:, None, :] * dt[:, :, None] * j
    pr = pmag * jnp.cos(pang)
    pi_ = pmag * jnp.sin(pang)
    pad = ((0, 0), (0, S5_POW_ROWS - (q + 1)), (0, 0))
    pwr = jnp.pad(jnp.concatenate([pr, pr], axis=-1), pad)
    pwi = jnp.pad(jnp.concatenate([pi_, pi_], axis=-1), pad)
    br_t = bbr.transpose(0, 2, 1)
    bi_t = bbi.transpose(0, 2, 1)
    cat = lambda a, b: jnp.concatenate([a, b], axis=-1)
    stack = [None] * S5_N_PARAM
    stack[S5_BB_RI] = cat(br_t, bi_t)
    stack[S5_BB_NIR] = cat(-bi_t, br_t)
    stack[S5_BB_IR] = cat(bi_t, br_t)
    stack[S5_BB_RNI] = cat(br_t, -bi_t)
    stack[S5_CC_RI] = cat(c_re, c_im)
    stack[S5_CC_NIR] = cat(-c_im, c_re)
    params = jnp.stack(stack, axis=1)
    aq_r, aq_i = pr[:, q], pi_[:, q]
    a_cat = cat(aq_r, aq_r)
    b_q = cat(-aq_i, aq_i)
    b_s = cat(aq_i, -aq_i)
    return pwr, pwi, params, a_cat, b_q, b_s


def _gelu_tanh(x):
    return 0.5 * x * (1.0 + jnp.tanh(math.sqrt(2.0 / math.pi) * (x + 0.044715 * (x * x * x))))


def _mix_kernel(an_ref, yt_ref, u_ref, x_ref, mod_ref, d_ref, wglu_ref, bglu_ref, gos_ref,
                woa_ref, wob_ref, gpm_ref, gpf_ref, wr_ref, br_ref, tri_ref,
                x1_ref, h2_ref, eidx_ref, rank_ref, gw_ref, cnt_ref, run_ref, tile_ref):
    i = pl.program_id(0)
    tm = TM_MIX

    @pl.when(i == 0)
    def _():
        run_ref[...] = jnp.zeros_like(run_ref)

    gt_m = mod_ref[0, 2:3, :]
    sh_f = mod_ref[0, 3:4, :]
    sc_f = mod_ref[0, 4:5, :]
    ng = N_ROUTE_GROUPS
    gsz = N_EXPERTS // ng
    neg = -jnp.inf

    ts = tm // MIX_SUBTILES
    run = run_ref[:, 0:1]
    for s in range(MIX_SUBTILES):
        r = slice(s * ts, (s + 1) * ts)
        chunks = slice(s * (ts // S5_Q), (s + 1) * (ts // S5_Q))
        yy = _from_group_chunks(yt_ref.at[:, chunks, :], tile_ref.at[s]) + d_ref[...] * u_ref[r, :]
        g = _gelu_tanh(yy)
        gl = jnp.dot(g.astype(BF16), wglu_ref[...], preferred_element_type=F32) + bglu_ref[...]
        ob = g * jax.nn.sigmoid(gl)
        bn = _rms(ob, gos_ref[...]).astype(BF16)
        o = (jnp.dot(an_ref[r, :], woa_ref[...], preferred_element_type=F32)
             + jnp.dot(bn, wob_ref[...], preferred_element_type=F32))
        x1 = x_ref[r, :] + gt_m * _rms(o, gpm_ref[...])
        x1_ref[r, :] = x1
        h2 = _rms(x1, gpf_ref[...]) * (1.0 + sc_f) + sh_f
        h2_ref[r, :] = _pack_halves(h2)

        logits = lax.dot_general(wr_ref[...], h2, (((1,), (1,)), ((), ())),
                                 preferred_element_type=F32, precision=lax.Precision.HIGHEST)
        scores = jax.nn.sigmoid(logits)
        biased = scores + br_ref[...]
        b3 = biased.reshape(ng, gsz, ts)
        s3 = scores.reshape(ng, gsz, ts)
        sub = lax.broadcasted_iota(jnp.int32, (ng, gsz, ts), 1).astype(F32)
        grp = lax.broadcasted_iota(jnp.int32, (ng, gsz, ts), 0).astype(F32)
        eid = grp * gsz + sub
        m1 = jnp.max(b3, axis=1, keepdims=True)
        i1 = jnp.min(jnp.where(b3 == m1, sub, float(gsz)), axis=1, keepdims=True)
        m2 = jnp.max(jnp.where(sub == i1, neg, b3), axis=1, keepdims=True)
        gs = m1 + m2
        gi = lax.broadcasted_iota(jnp.int32, (ng, 1, ts), 0)
        beaten = jnp.zeros((ng, 1, ts), F32)
        for gp in range(ng):
            o_ = gs[gp:gp + 1]
            beats = (o_ > gs) | ((o_ == gs) & (gi > gp))
            beaten = beaten + beats.astype(F32)
        gmask = beaten < float(TOPK_ROUTE_GROUPS)
        masked = jnp.where(gmask, b3, neg)

        sels = []
        picked = jnp.zeros((ng, gsz, ts), F32)
        for k in range(TOP_K):
            m = jnp.max(jnp.max(masked, axis=0, keepdims=True), axis=1, keepdims=True)
            cand = jnp.where(masked == m, eid, float(N_EXPERTS))
            sel = jnp.min(jnp.min(cand, axis=0, keepdims=True), axis=1, keepdims=True)
            oh = eid == sel
            masked = jnp.where(oh, neg, masked)
            picked = jnp.where(oh, 1.0, picked)
            sels.append(sel)

        pm = picked.reshape(N_EXPERTS, ts)
        prefix = jnp.dot(pm.astype(BF16), tri_ref[:ts, :ts], preferred_element_type=F32) + run
        p3 = prefix.reshape(ng, gsz, ts)
        run = run + jnp.sum(pm, axis=1, keepdims=True)

        sc_rows = []
        for k in range(TOP_K):
            oh = eid == sels[k]
            sc_k = jnp.sum(jnp.sum(jnp.where(oh, s3, 0.0), axis=0, keepdims=True), axis=1, keepdims=True)
            rk_k = jnp.sum(jnp.sum(jnp.where(oh, p3, 0.0), axis=0, keepdims=True), axis=1, keepdims=True)
            sc_rows.append(sc_k)
            eidx_ref[k:k + 1, r] = sels[k].reshape(1, ts).astype(jnp.int32)
            rank_ref[k:k + 1, r] = rk_k.reshape(1, ts).astype(jnp.int32)
        tot = sc_rows[0]
        for k in range(1, TOP_K):
            tot = tot + sc_rows[k]
        inv = ROUTED_SCALE / (tot + 1e-20)
        for k in range(TOP_K):
            gw_ref[k:k + 1, r] = (sc_rows[k] * inv).reshape(1, ts)

    run_new = jnp.broadcast_to(run, run_ref.shape)
    run_ref[...] = run_new
    cnt_ref[...] = run_new


def _mix(half, a_n, yt, u2, x2, mod, d_skip, wglu_bf, b_glu, g_out_ssm, wo_a, wo_b, g_post_mix,
         g_pre_ffn, w_router_t, b_router_col, tri):
    tm = TM_MIX
    t0 = half * (HALF_TOK // tm)
    row = lambda n: pl.BlockSpec((1, n), lambda i: (0, 0))
    full = lambda a, b: pl.BlockSpec((a, b), lambda i: (0, 0))
    tok_in = lambda n: pl.BlockSpec((tm, n), lambda i: (t0 + i, 0))
    tok = lambda n: pl.BlockSpec((tm, n), lambda i: (i, 0))
    col = pl.BlockSpec((TOP_K, tm), lambda i: (0, i))
    return pl.pallas_call(
        _mix_kernel,
        grid=(HALF_TOK // tm,),
        in_specs=[tok_in(CONV_CH),
                  pl.BlockSpec((SSM_GROUPS, tm // S5_Q, S5_QH), lambda i: (0, t0 + i, 0)),
                  tok_in(SSM_CH), tok_in(D_MODEL),
                  pl.BlockSpec((1, 8, D_MODEL), lambda i: (half, 0, 0)),
                  row(SSM_CH), full(SSM_CH, SSM_CH), row(SSM_CH), row(SSM_CH),
                  full(CONV_CH, D_MODEL), full(SSM_CH, D_MODEL), row(D_MODEL), row(D_MODEL),
                  full(N_EXPERTS, D_MODEL), full(N_EXPERTS, 1), full(tm, tm)],
        out_specs=[tok(D_MODEL), tok(D_MODEL // 2), col, col, col,
                   pl.BlockSpec((N_EXPERTS, 128), lambda i: (0, 0))],
        out_shape=[jax.ShapeDtypeStruct((HALF_TOK, D_MODEL), F32),
                   jax.ShapeDtypeStruct((HALF_TOK, D_MODEL // 2), jnp.uint32),
                   jax.ShapeDtypeStruct((TOP_K, HALF_TOK), jnp.int32),
                   jax.ShapeDtypeStruct((TOP_K, HALF_TOK), jnp.int32),
                   jax.ShapeDtypeStruct((TOP_K, HALF_TOK), F32),
                   jax.ShapeDtypeStruct((N_EXPERTS, 128), F32)],
        scratch_shapes=[pltpu.VMEM((N_EXPERTS, 128), F32),
                        pltpu.VMEM((MIX_SUBTILES, SSM_CH // LANES, tm // MIX_SUBTILES, LANES), F32)],
        compiler_params=_cparams(("arbitrary",)),
        name="mix_out_router",
    )(a_n, yt, u2, x2, mod, d_skip, wglu_bf, b_glu, g_out_ssm, wo_a, wo_b, g_post_mix,
      g_pre_ffn, w_router_t, b_router_col, tri)


def _expert_kernel(blk0_ref, nblk_ref, xs_hbm, wg_ref, wu_ref, wd_ref, ys_hbm,
                   xbuf, ybuf, wgu_s, wd_s, sem_in, sem_out):
    e = pl.program_id(0)
    n = nblk_ref[e]
    b0 = blk0_ref[e]
    n_all = blk0_ref[N_EXPERTS - 1] + nblk_ref[N_EXPERTS - 1]

    def rows(b):
        return pl.ds(pl.multiple_of(b * ROW_BLOCK, ROW_BLOCK), ROW_BLOCK)

    def in_copy(b, slot):
        return pltpu.make_async_copy(xs_hbm.at[rows(b)], xbuf.at[slot], sem_in.at[slot])

    def out_copy(b, slot):
        return pltpu.make_async_copy(ybuf.at[slot], ys_hbm.at[rows(b)], sem_out.at[slot])

    for b in range(EXPERT_AHEAD):
        @pl.when((e == 0) & (b < n_all))
        def _():
            in_copy(b, b).start()

    wgu_s[:, :D_EXPERT] = wg_ref[0].astype(BF16)
    wgu_s[:, D_EXPERT:] = wu_ref[0].astype(BF16)
    wd_s[...] = wd_ref[0].astype(BF16)

    def admit(b):
        in_copy(b, b % EXPERT_SLOTS).wait()
        ahead = b + EXPERT_AHEAD

        @pl.when(ahead < n_all)
        def _():
            in_copy(ahead, ahead % EXPERT_SLOTS).start()

        @pl.when(b >= EXPERT_SLOTS)
        def _():
            out_copy(b - EXPERT_SLOTS, b % EXPERT_SLOTS).wait()

    def compute(b):
        slot = b % EXPERT_SLOTS
        x_lo, x_hi = _unpack_halves(xbuf[slot])
        x = jnp.concatenate([x_lo.astype(BF16), x_hi.astype(BF16)], axis=1)
        h = jnp.dot(x, wgu_s[...], preferred_element_type=F32)
        hg = h[:, :D_EXPERT]
        act = hg * jax.nn.sigmoid(hg) * h[:, D_EXPERT:]
        ybuf[slot] = _pack_halves(jnp.dot(act.astype(BF16), wd_s[...], preferred_element_type=F32))

    def run(blocks):
        for b in blocks:
            admit(b)
        for b in blocks:
            compute(b)
        for b in blocks:
            out_copy(b, b % EXPERT_SLOTS).start(priority=1)

    def pair(i, carry):
        b = b0 + 2 * i
        run([b, b + 1])
        return carry

    lax.fori_loop(0, n // 2, pair, 0)

    @pl.when(n % 2 == 1)
    def _():
        run([b0 + n - 1])

    @pl.when(e == N_EXPERTS - 1)
    def _():
        for j in range(1, EXPERT_SLOTS + 1):
            @pl.when(n_all >= j)
            def _():
                out_copy(n_all - j, (n_all - j) % EXPERT_SLOTS).wait()


def _experts(blk0, nblk, xs, we_gate, we_up, we_down):
    any_spec = pl.BlockSpec(memory_space=pl.ANY)
    grid_spec = pltpu.PrefetchScalarGridSpec(
        num_scalar_prefetch=2,
        grid=(N_EXPERTS,),
        in_specs=[any_spec,
                  pl.BlockSpec((1, D_MODEL, D_EXPERT), lambda e, b0, nb: (e, 0, 0)),
                  pl.BlockSpec((1, D_MODEL, D_EXPERT), lambda e, b0, nb: (e, 0, 0)),
                  pl.BlockSpec((1, D_EXPERT, D_MODEL), lambda e, b0, nb: (e, 0, 0))],
        out_specs=any_spec,
        scratch_shapes=[pltpu.VMEM((EXPERT_SLOTS, ROW_BLOCK, D_MODEL // 2), jnp.uint32),
                        pltpu.VMEM((EXPERT_SLOTS, ROW_BLOCK, D_MODEL // 2), jnp.uint32),
                        pltpu.VMEM((D_MODEL, 2 * D_EXPERT), BF16),
                        pltpu.VMEM((D_EXPERT, D_MODEL), BF16),
                        pltpu.SemaphoreType.DMA((EXPERT_SLOTS,)),
                        pltpu.SemaphoreType.DMA((EXPERT_SLOTS,))],
    )
    return pl.pallas_call(
        _expert_kernel,
        grid_spec=grid_spec,
        out_shape=jax.ShapeDtypeStruct((N_ROWS, D_MODEL // 2), jnp.uint32),
        compiler_params=_cparams(("arbitrary",)),
        name="routed_experts",
    )(blk0, nblk, xs, we_gate, we_up, we_down)


def _final_kernel(h2_ref, yg_ref, gw_ref, x1_ref, mod_ref, wgu_ref, wd_ref, g_ref, *rest):
    o_ref = rest[-1]
    half = D_MODEL // 2
    gt_f = mod_ref[0, 5:6, :]
    x_lo, x_hi = _unpack_halves(h2_ref[...])
    h = (jnp.dot(x_lo.astype(BF16), wgu_ref[:half, :], preferred_element_type=F32)
         + jnp.dot(x_hi.astype(BF16), wgu_ref[half:, :], preferred_element_type=F32))
    hg = h[:, :D_EXPERT]
    act = hg * jax.nn.sigmoid(hg) * h[:, D_EXPERT:]
    shared = jnp.dot(act.astype(BF16), wd_ref[...], preferred_element_type=F32)
    y_lo = shared[:, :half]
    y_hi = shared[:, half:]
    for k in range(TOP_K):
        r_lo, r_hi = _unpack_halves(yg_ref[k])
        w = gw_ref[:, k:k + 1]
        y_lo = y_lo + w * r_lo
        y_hi = y_hi + w * r_hi
    ms = (jnp.sum(y_lo * y_lo, axis=-1, keepdims=True)
          + jnp.sum(y_hi * y_hi, axis=-1, keepdims=True)) * (1.0 / D_MODEL)
    inv = lax.rsqrt(ms + NORM_EPS)
    o_ref[:, :half] = x1_ref[:, :half] + gt_f[:, :half] * (y_lo * inv * g_ref[:, :half])
    o_ref[:, half:] = x1_ref[:, half:] + gt_f[:, half:] * (y_hi * inv * g_ref[:, half:])


def _final(half, out_prev, h2p, yg, gw_t, x1, mod, ws_gu, ws_d, g_post_ffn):
    tm = TM_OUT
    t0 = half * (HALF_TOK // tm)
    tok = pl.BlockSpec((tm, D_MODEL), lambda i: (i, 0))
    in_specs = [pl.BlockSpec((tm, D_MODEL // 2), lambda i: (i, 0)),
                pl.BlockSpec((TOP_K, tm, D_MODEL // 2), lambda i: (0, i, 0)),
                pl.BlockSpec((tm, TOP_K), lambda i: (i, 0)),
                tok,
                pl.BlockSpec((1, 8, D_MODEL), lambda i: (half, 0, 0)),
                pl.BlockSpec((D_MODEL, 2 * D_EXPERT), lambda i: (0, 0)),
                pl.BlockSpec((D_EXPERT, D_MODEL), lambda i: (0, 0)),
                pl.BlockSpec((1, D_MODEL), lambda i: (0, 0))]
    args = [h2p, yg, gw_t, x1, mod, ws_gu, ws_d, g_post_ffn]
    aliases = {}
    if out_prev is not None:
        aliases = {len(args): 0}
        in_specs.append(pl.BlockSpec(memory_space=pl.ANY))
        args.append(out_prev)
    return pl.pallas_call(
        _final_kernel,
        grid=(HALF_TOK // tm,),
        in_specs=in_specs,
        out_specs=pl.BlockSpec((tm, D_MODEL), lambda i: (t0 + i, 0)),
        out_shape=jax.ShapeDtypeStruct((N_TOK, D_MODEL), F32),
        input_output_aliases=aliases,
        compiler_params=_cparams(("parallel",)),
        name="shared_final",
    )(*args)


def _sc_worker_id():
    return lax.axis_index("s") * SC_CORES + lax.axis_index("c")


def _dispatch_body(h_hbm, dest_hbm, xs_hbm, idx_v, rows_v, sem_l, sem_s):
    n = SC_CHUNKS_PER_WORKER
    c0 = _sc_worker_id() * n

    def load(i, b):
        return pltpu.async_copy(h_hbm.at[pl.ds((c0 + i) * SC_W, SC_W)], rows_v.at[b], sem_l.at[b])

    loads = [None] * n
    scat = [None] * n
    loads[0] = load(0, 0)
    for i in range(n):
        b = i % 2
        pltpu.sync_copy(dest_hbm.at[c0 + i], idx_v.at[b])
        loads[i].wait()
        if i + 1 < n:
            if i >= 1:
                for d in scat[i - 1]:
                    d.wait()
            loads[i + 1] = load(i + 1, 1 - b)
        scat[i] = [pltpu.async_copy(rows_v.at[b], xs_hbm.at[idx_v.at[b].at[k]], sem_s.at[b])
                   for k in range(TOP_K)]
    for i in (n - 2, n - 1):
        for d in scat[i]:
            d.wait()


def _sc_dispatch(h2p, dest3):
    mesh = plsc.VectorSubcoreMesh(core_axis_name="c", subcore_axis_name="s")
    return pl.kernel(
        _dispatch_body, mesh=mesh,
        out_type=jax.ShapeDtypeStruct((N_ROWS, D_MODEL // 2), jnp.uint32),
        scratch_types=[pltpu.VMEM((2, TOP_K, SC_W), jnp.int32),
                       pltpu.VMEM((2, SC_W, D_MODEL // 2), jnp.uint32),
                       pltpu.SemaphoreType.DMA((2,)), pltpu.SemaphoreType.DMA((2,))],
    )(h2p, dest3)


def _combine_body(ys_hbm, dest_hbm, yg_hbm, idx_v, rows_v, sem_g, sem_w):
    c0 = _sc_worker_id() * SC_CHUNKS_PER_WORKER

    @pl.loop(0, SC_CHUNKS_PER_WORKER)
    def _(i):
        c = c0 + i
        pltpu.sync_copy(dest_hbm.at[c], idx_v)
        g = [None] * TOP_K
        w = [None] * TOP_K
        g[0] = pltpu.async_copy(ys_hbm.at[idx_v.at[0]], rows_v.at[0], sem_g.at[0])
        for k in range(TOP_K):
            b = k % 2
            g[k].wait()
            if k + 1 < TOP_K:
                if k >= 1:
                    w[k - 1].wait()
                g[k + 1] = pltpu.async_copy(ys_hbm.at[idx_v.at[k + 1]], rows_v.at[1 - b], sem_g.at[1 - b])
            w[k] = pltpu.async_copy(rows_v.at[b], yg_hbm.at[k].at[pl.ds(c * SC_W, SC_W)], sem_w.at[b])
        w[TOP_K - 2].wait()
        w[TOP_K - 1].wait()


def _sc_combine(ysp, dest3):
    mesh = plsc.VectorSubcoreMesh(core_axis_name="c", subcore_axis_name="s")
    return pl.kernel(
        _combine_body, mesh=mesh,
        out_type=jax.ShapeDtypeStruct((TOP_K, HALF_TOK, D_MODEL // 2), jnp.uint32),
        scratch_types=[pltpu.VMEM((TOP_K, SC_W), jnp.int32),
                       pltpu.VMEM((2, SC_W, D_MODEL // 2), jnp.uint32),
                       pltpu.SemaphoreType.DMA((2,)), pltpu.SemaphoreType.DMA((2,))],
    )(ysp, dest3)


def kernel(x, c, w_ada, b_ada, g_pre_mix, g_post_mix, w_in, conv_w, conv_b, conv_ln_g, conv_ln_b,
           ssm_a_re, ssm_a_im, ssm_log_dt, ssm_b_re, ssm_b_im, ssm_c_re, ssm_c_im, ssm_d,
           ssm_w_glu, ssm_b_glu, g_out_conv, g_out_ssm, w_out, g_pre_ffn, g_post_ffn,
           w_router, b_router, we_gate, we_up, we_down, ws_gate, ws_up, ws_down):
    l = 0
    x2 = x.reshape(N_TOK, D_MODEL)
    r1 = lambda a: a.reshape(1, -1)

    c_pad = jnp.zeros((8, D_MODEL), F32).at[:BATCH].set(c)
    mod = _ada(c_pad, w_ada[l], r1(b_ada[l]))[:BATCH].reshape(BATCH, 6, D_MODEL)
    mod = jnp.concatenate([mod, jnp.zeros((BATCH, 2, D_MODEL), F32)], axis=1)

    v, u, ut = _inproj(x2, mod, r1(g_pre_mix[l]), w_in[l].astype(BF16))
    cw = jnp.concatenate([conv_w[l].reshape(CONV_WIDTH, CONV_CH), jnp.zeros((1, CONV_CH), F32)], axis=0)
    a_n = _conv(v.reshape(BATCH, SEQ, CONV_CH), cw, r1(conv_b[l]), r1(conv_ln_g[l]),
                r1(conv_ln_b[l]), r1(g_out_conv[l])).reshape(N_TOK, CONV_CH)

    pwr, pwi, s5_params, a_cat, b_q, b_s = _s5_operators(
        ssm_a_re[l], ssm_a_im[l], ssm_log_dt[l], ssm_b_re[l], ssm_b_im[l], ssm_c_re[l], ssm_c_im[l])
    yt = _s5(ut, pwr, pwi, s5_params, a_cat, b_q, b_s)

    tm = TM_MIX
    tri = (jnp.arange(tm)[:, None] < jnp.arange(tm)[None, :]).astype(BF16)
    wo = w_out[l].astype(BF16)
    mix_params = (r1(ssm_d[l]), ssm_w_glu[l].astype(BF16), r1(ssm_b_glu[l]), r1(g_out_ssm[l]),
                  wo[:CONV_CH], wo[CONV_CH:], r1(g_post_mix[l]), r1(g_pre_ffn[l]),
                  w_router[l].T, b_router[l].reshape(N_EXPERTS, 1), tri)
    ws_gu = jnp.concatenate([ws_gate[l], ws_up[l]], axis=1).astype(BF16)
    ws_d = ws_down[l].astype(BF16)
    e_ids = jnp.arange(N_EXPERTS, dtype=jnp.int32)

    out = None
    for half in range(N_HALVES):
        x1, h2, eidx, rank, gw, cnt = _mix(half, a_n, yt, u, x2, mod, *mix_params)
        counts = cnt[:, 0].astype(jnp.int32)
        padded = (counts + ROW_BLOCK - 1) // ROW_BLOCK * ROW_BLOCK
        pstart = jnp.cumsum(padded) - padded
        dest = rank + jnp.sum(jnp.where(eidx[..., None] == e_ids, pstart, 0), axis=-1)
        dest3 = dest.reshape(TOP_K, HALF_TOK // SC_W, SC_W).transpose(1, 0, 2)

        xs = _sc_dispatch(h2, dest3)
        ys = _experts(pstart // ROW_BLOCK, padded // ROW_BLOCK, xs, we_gate[l], we_up[l], we_down[l])
        yg = _sc_combine(ys, dest3)
        out = _final(half, out, h2, yg, gw.T, x1, mod, ws_gu, ws_d, r1(g_post_ffn[l]))
    return out.reshape(BATCH, SEQ, D_MODEL)
```

```python
import functools
import math

import jax
import jax.numpy as jnp
from jax import lax
from jax.experimental import pallas as pl
from jax.experimental.pallas import tpu as pltpu
from jax.experimental.pallas import tpu_sc as plsc

F32 = jnp.float32
BF16 = jnp.bfloat16

D_MODEL = 1024
BATCH = 2
SEQ = 8192
N_TOK = BATCH * SEQ
CONV_CH = 512
CONV_WIDTH = 31
SSM_CH = 512
SSM_GROUP_CH = 16
SSM_GROUPS = 32
SSM_STATE = 64
D_IN = 2 * CONV_CH + SSM_CH
N_EXPERTS = 64
TOP_K = 8
N_ROUTE_GROUPS = 8
TOPK_ROUTE_GROUPS = 4
D_EXPERT = 256
ROUTED_SCALE = 2.5
NORM_EPS = 1e-6

SUBLANES = 8
LANES = 128

TM_IN = 512
IN_SUBTILES = 2
TL_CONV = 512
CONV_HALO = 32
CONV_ROWS = 64
S5_Q = 32
S5_QH = S5_Q * SSM_GROUP_CH
S5_CHUNKS = N_TOK // S5_Q
S5_CHUNKS_PER_SEQ = SEQ // S5_Q
TM_MIX = 512
MIX_SUBTILES = 1
ROW_BLOCK = 512
EXPERT_AHEAD = 4
EXPERT_SLOTS = 6
HALF_TOK = SEQ
N_HALVES = N_TOK // HALF_TOK
N_BLOCKS = HALF_TOK * TOP_K // ROW_BLOCK + N_EXPERTS
N_ROWS = N_BLOCKS * ROW_BLOCK
TM_OUT = 512
SC_CORES = 2
SC_SUBCORES = 16
SC_WORKERS = SC_CORES * SC_SUBCORES
SC_W = 64
SC_CHUNKS_PER_WORKER = HALF_TOK // (SC_WORKERS * SC_W)
VMEM_LIMIT = 48 * 1024 * 1024


def _cparams(sem):
    return pltpu.CompilerParams(dimension_semantics=sem, vmem_limit_bytes=VMEM_LIMIT)


def _pack_halves(x):
    n = x.shape[-1] // 2
    lo = lax.bitcast_convert_type(x[:, :n].astype(BF16).astype(F32), jnp.uint32)
    hi = lax.bitcast_convert_type(x[:, n:].astype(BF16).astype(F32), jnp.uint32)
    return hi | (lo >> 16)


def _unpack_halves(p):
    lo = lax.bitcast_convert_type(p << 16, F32)
    hi = lax.bitcast_convert_type(p & jnp.uint32(0xFFFF0000), F32)
    return lo, hi


def _rms(x, g):
    return x * lax.rsqrt(jnp.mean(x * x, axis=-1, keepdims=True) + NORM_EPS) * g


def _ada_kernel(c_ref, w_ref, b_ref, o_ref):
    c = c_ref[...]
    a = c * jax.nn.sigmoid(c)
    o_ref[...] = jnp.dot(a, w_ref[...], preferred_element_type=F32,
                         precision=lax.Precision.HIGHEST) + b_ref[...]


def _ada(c_pad, w_ada, b_ada):
    n = w_ada.shape[1]
    bn = 1536
    return pl.pallas_call(
        _ada_kernel,
        grid=(n // bn,),
        in_specs=[pl.BlockSpec((8, D_MODEL), lambda j: (0, 0)),
                  pl.BlockSpec((D_MODEL, bn), lambda j: (0, j)),
                  pl.BlockSpec((1, bn), lambda j: (0, j))],
        out_specs=pl.BlockSpec((8, bn), lambda j: (0, j)),
        out_shape=jax.ShapeDtypeStruct((8, n), F32),
        compiler_params=_cparams(("arbitrary",)),
        name="ada_mod",
    )(c_pad, w_ada, b_ada)


GROUPS_PER_LANE_TILE = LANES // SSM_GROUP_CH


def _to_group_chunks(u, tile_ref, ut_ref):
    n_chunks = u.shape[0] // S5_Q
    for j in range(SSM_CH // LANES):
        tile_ref[j] = u[:, LANES * j:LANES * (j + 1)]
    for j in range(SSM_CH // LANES):
        rows_t = [tile_ref[j, pl.ds(t, n_chunks, stride=S5_Q), :] for t in range(S5_Q)]
        for gg in range(GROUPS_PER_LANE_TILE):
            lo = gg * SSM_GROUP_CH
            row = jnp.concatenate([r[:, lo:lo + SSM_GROUP_CH] for r in rows_t], axis=1)
            ut_ref[j * GROUPS_PER_LANE_TILE + gg] = row.astype(ut_ref.dtype)


def _from_group_chunks(yt_ref, tile_ref):
    n_chunks = yt_ref.shape[1]
    for j in range(SSM_CH // LANES):
        for t in range(S5_Q):
            lo = t * SSM_GROUP_CH
            piece = jnp.concatenate(
                [yt_ref[j * GROUPS_PER_LANE_TILE + gg, :, lo:lo + SSM_GROUP_CH]
                 for gg in range(GROUPS_PER_LANE_TILE)], axis=1)
            tile_ref[j, pl.ds(t, n_chunks, stride=S5_Q), :] = piece
    return jnp.concatenate([tile_ref[j] for j in range(SSM_CH // LANES)], axis=1)


def _inproj_kernel(x_ref, mod_ref, g_ref, w_ref, v_ref, u_ref, ut_ref, tile_ref):
    sh = mod_ref[0, 0:1, :]
    sc = mod_ref[0, 1:2, :]
    sub = TM_IN // IN_SUBTILES
    sub_chunks = sub // S5_Q
    for s in range(IN_SUBTILES):
        r = slice(s * sub, (s + 1) * sub)
        h = _rms(x_ref[r, :], g_ref[...]) * (1.0 + sc) + sh
        z = jnp.dot(h.astype(BF16), w_ref[...], preferred_element_type=F32)
        v_ref[r, :] = z[:, :CONV_CH] * jax.nn.sigmoid(z[:, CONV_CH:2 * CONV_CH])
        u = z[:, 2 * CONV_CH:]
        u_ref[r, :] = u
        _to_group_chunks(u, tile_ref.at[s], ut_ref.at[:, s * sub_chunks:(s + 1) * sub_chunks, :])


def _inproj(x2, mod, g_pre, w_in_bf):
    tiles_per_seq = SEQ // TM_IN
    return pl.pallas_call(
        _inproj_kernel,
        grid=(N_TOK // TM_IN,),
        in_specs=[pl.BlockSpec((TM_IN, D_MODEL), lambda i: (i, 0)),
                  pl.BlockSpec((1, 8, D_MODEL), lambda i: (i // tiles_per_seq, 0, 0)),
                  pl.BlockSpec((1, D_MODEL), lambda i: (0, 0)),
                  pl.BlockSpec((D_MODEL, D_IN), lambda i: (0, 0))],
        out_specs=[pl.BlockSpec((TM_IN, CONV_CH), lambda i: (i, 0)),
                   pl.BlockSpec((TM_IN, SSM_CH), lambda i: (i, 0)),
                   pl.BlockSpec((SSM_GROUPS, TM_IN // S5_Q, S5_QH), lambda i: (0, i, 0))],
        out_shape=[jax.ShapeDtypeStruct((N_TOK, CONV_CH), F32),
                   jax.ShapeDtypeStruct((N_TOK, SSM_CH), F32),
                   jax.ShapeDtypeStruct((SSM_GROUPS, S5_CHUNKS, S5_QH), BF16)],
        scratch_shapes=[pltpu.VMEM((IN_SUBTILES, SSM_CH // LANES, TM_IN // IN_SUBTILES, LANES), F32)],
        compiler_params=_cparams(("parallel",)),
        name="in_proj",
    )(x2, mod, g_pre, w_in_bf)


def _conv_kernel(vc_ref, vp_ref, w_ref, cb_ref, lg_ref, lb_ref, go_ref, o_ref, sh_ref):
    i = pl.program_id(1)
    keep = (i > 0).astype(F32)
    n_ext = TL_CONV + CONV_HALO
    sh_ref[0, 0:CONV_HALO, :] = vp_ref[0] * keep
    sh_ref[0, CONV_HALO:, :] = vc_ref[0]
    for s in range(1, SUBLANES):
        sh_ref[s, 0:n_ext - s, :] = sh_ref[0, s:n_ext, :]
    off = CONV_HALO - (CONV_WIDTH - 1)
    for r in range(TL_CONV // CONV_ROWS):
        acc = None
        for j in range(CONV_WIDTH):
            s = (off + j) % SUBLANES
            al = r * CONV_ROWS + (off + j) - s
            term = w_ref[j:j + 1, :] * sh_ref[s, al:al + CONV_ROWS, :]
            acc = term if acc is None else acc + term
        y = acc + cb_ref[...]
        mu = jnp.mean(y, axis=-1, keepdims=True)
        d = y - mu
        var = jnp.mean(d * d, axis=-1, keepdims=True)
        yn = d * lax.rsqrt(var + NORM_EPS) * lg_ref[...] + lb_ref[...]
        a = yn * jax.nn.sigmoid(yn)
        o_ref[0, r * CONV_ROWS:(r + 1) * CONV_ROWS, :] = _rms(a, go_ref[...]).astype(BF16)


def _conv(v3, conv_w, conv_b, ln_g, ln_b, g_out):
    halo_per_tile = TL_CONV // CONV_HALO
    vec = pl.BlockSpec((1, CONV_CH), lambda b, i: (0, 0))
    return pl.pallas_call(
        _conv_kernel,
        grid=(BATCH, SEQ // TL_CONV),
        in_specs=[pl.BlockSpec((1, TL_CONV, CONV_CH), lambda b, i: (b, i, 0)),
                  pl.BlockSpec((1, CONV_HALO, CONV_CH),
                               lambda b, i: (b, jnp.maximum(i * halo_per_tile - 1, 0), 0)),
                  pl.BlockSpec((CONV_WIDTH + 1, CONV_CH), lambda b, i: (0, 0)),
                  vec, vec, vec, vec],
        out_specs=pl.BlockSpec((1, TL_CONV, CONV_CH), lambda b, i: (b, i, 0)),
        out_shape=jax.ShapeDtypeStruct((BATCH, SEQ, CONV_CH), BF16),
        scratch_shapes=[pltpu.VMEM((SUBLANES, TL_CONV + CONV_HALO, CONV_CH), F32)],
        compiler_params=_cparams(("parallel", "arbitrary")),
        name="conv_module",
    )(v3, v3, conv_w, conv_b, ln_g, ln_b, g_out)


S5_GROUP_ROWS = S5_CHUNKS + 8


S5_POW_ROWS = (S5_Q + 1 + SUBLANES - 1) // SUBLANES * SUBLANES
(S5_BB_RI, S5_BB_NIR, S5_BB_IR, S5_BB_RNI, S5_CC_RI, S5_CC_NIR, S5_N_PARAM) = range(7)


def _s5_kernel(ut_ref, pwr_ref, pwi_ref, par_ref, a_ref, bq_ref, bs_ref, yt_ref, sin_s, sp_s):
    phase = pl.program_id(0)
    g = pl.program_id(1)
    q = S5_Q
    n = 2 * SSM_STATE
    row0 = pl.multiple_of(g * S5_GROUP_ROWS, 8)

    def lam_pow(j):
        return pwr_ref[0, j:j + 1, :], pwi_ref[0, j:j + 1, :]

    @pl.when(phase == 0)
    def _():
        bb_ri, bb_nir = par_ref[0, S5_BB_RI], par_ref[0, S5_BB_NIR]
        bb_ir, bb_rni = par_ref[0, S5_BB_IR], par_ref[0, S5_BB_RNI]
        blk_q, blk_s = [], []
        for t in range(q):
            pr, pi_ = lam_pow(q - 1 - t)
            blk_q.append(pr * bb_ri + pi_ * bb_nir)
            blk_s.append(pr * bb_ir + pi_ * bb_rni)
        wst = jnp.concatenate([jnp.concatenate(blk_q, axis=0), jnp.concatenate(blk_s, axis=0)], axis=1)
        r = jnp.dot(ut_ref[0], wst.astype(BF16), preferred_element_type=F32)
        sin_s[0, pl.ds(row0, S5_CHUNKS), :] = r[:, :n]
        sin_s[1, pl.ds(row0, S5_CHUNKS), :] = r[:, n:]

    @pl.when((phase == 1) & (g == 0))
    def _():
        a = a_ref[...]
        bq = bq_ref[...]
        bs = bs_ref[...]

        def body(c, carry):
            nxt = []
            for b in range(BATCH):
                x, xs = carry[b]
                rows = pl.ds(b * S5_CHUNKS_PER_SEQ + c, SSM_GROUPS, stride=S5_GROUP_ROWS)
                sp_s[rows, :] = x
                nxt.append((a * x + bq * xs + sin_s[0, rows, :], a * xs + bs * x + sin_s[1, rows, :]))
            return tuple(nxt)

        z = jnp.zeros((SSM_GROUPS, n), F32)
        lax.fori_loop(0, S5_CHUNKS_PER_SEQ, body, tuple((z, z) for _ in range(BATCH)))

    @pl.when(phase == 1)
    def _():
        cc_ri, cc_nir = par_ref[0, S5_CC_RI], par_ref[0, S5_CC_NIR]
        cl = []
        for j in range(q + 1):
            pr, pi_ = lam_pow(j)
            cl.append(pr * cc_ri + pi_ * cc_nir)
        cl_lo = jnp.concatenate(cl[:q], axis=0)
        cl_hi = jnp.concatenate(cl[1:], axis=0)
        lane = lax.broadcasted_iota(jnp.int32, (1, n), 1)
        vgt = (cl_hi * jnp.where(lane < SSM_STATE, 1.0, -1.0)).astype(BF16)
        kt = lax.dot_general(par_ref[0, S5_BB_RNI], cl_lo, (((1,), (1,)), ((), ())),
                             preferred_element_type=F32, precision=lax.Precision.HIGHEST)
        padded = jnp.concatenate([jnp.zeros_like(kt), kt], axis=1)
        tg = jnp.concatenate(
            [padded[:, (q - t) * SSM_GROUP_CH:(q - t) * SSM_GROUP_CH + S5_QH] for t in range(q)],
            axis=0).astype(BF16)
        sp = sp_s[pl.ds(row0, S5_CHUNKS), :]
        y = jnp.dot(ut_ref[0], tg, preferred_element_type=F32)
        yt_ref[0] = y + lax.dot_general(sp.astype(BF16), vgt, (((1,), (1,)), ((), ())),
                                        preferred_element_type=F32)


def _s5(ut, pwr, pwi, params, a_cat, b_q, b_s):
    vec = pl.BlockSpec((SSM_GROUPS, 2 * SSM_STATE), lambda p, g: (0, 0))
    powers = pl.BlockSpec((1, S5_POW_ROWS, 2 * SSM_STATE), lambda p, g: (g, 0, 0))
    return pl.pallas_call(
        _s5_kernel,
        grid=(2, SSM_GROUPS),
        in_specs=[pl.BlockSpec((1, S5_CHUNKS, S5_QH), lambda p, g: (g, 0, 0)),
                  powers, powers,
                  pl.BlockSpec((1, S5_N_PARAM, SSM_GROUP_CH, 2 * SSM_STATE), lambda p, g: (g, 0, 0, 0)),
                  vec, vec, vec],
        out_specs=pl.BlockSpec((1, S5_CHUNKS, S5_QH), lambda p, g: (g * p, 0, 0)),
        out_shape=jax.ShapeDtypeStruct((SSM_GROUPS, S5_CHUNKS, S5_QH), F32),
        scratch_shapes=[pltpu.VMEM((2, SSM_GROUPS * S5_GROUP_ROWS, 2 * SSM_STATE), F32),
                        pltpu.VMEM((SSM_GROUPS * S5_GROUP_ROWS, 2 * SSM_STATE), F32)],
        compiler_params=_cparams(("arbitrary", "arbitrary")),
        name="s5_chunked",
    )(ut, pwr, pwi, params, a_cat, b_q, b_s)


def _s5_operators(a_re, a_im, log_dt, b_re, b_im, c_re, c_im):
    q = S5_Q
    dt = jnp.exp(log_dt)[:, None]
    ar, ai = a_re, a_im
    mag = jnp.exp(ar * dt)
    lr = mag * jnp.cos(ai * dt)
    li = mag * jnp.sin(ai * dt)
    den = ar * ar + ai * ai
    nr = lr - 1.0
    kr = (nr * ar + li * ai) / den
    ki = (li * ar - nr * ai) / den
    bbr = kr[..., None] * b_re - ki[..., None] * b_im
    bbi = kr[..., None] * b_im + ki[..., None] * b_re
    j = jnp.arange(q + 1, dtype=F32)[None, :, None]
    pmag = jnp.exp(ar[:, None, :] * dt[:, :, None] * j)
    pang = ai[:, None, :] * dt[:, :, None] * j
    pr = pmag * jnp.cos(pang)
    pi_ = pmag * jnp.sin(pang)
    pad = ((0, 0), (0, S5_POW_ROWS - (q + 1)), (0, 0))
    pwr = jnp.pad(jnp.concatenate([pr, pr], axis=-1), pad)
    pwi = jnp.pad(jnp.concatenate([pi_, pi_], axis=-1), pad)
    br_t = bbr.transpose(0, 2, 1)
    bi_t = bbi.transpose(0, 2, 1)
    cat = lambda a, b: jnp.concatenate([a, b], axis=-1)
    stack = [None] * S5_N_PARAM
    stack[S5_BB_RI] = cat(br_t, bi_t)
    stack[S5_BB_NIR] = cat(-bi_t, br_t)
    stack[S5_BB_IR] = cat(bi_t, br_t)
    stack[S5_BB_RNI] = cat(br_t, -bi_t)
    stack[S5_CC_RI] = cat(c_re, c_im)
    stack[S5_CC_NIR] = cat(-c_im, c_re)
    params = jnp.stack(stack, axis=1)
    aq_r, aq_i = pr[:, q], pi_[:, q]
    a_cat = cat(aq_r, aq_r)
    b_q = cat(-aq_i, aq_i)
    b_s = cat(aq_i, -aq_i)
    return pwr, pwi, params, a_cat, b_q, b_s


def _gelu_tanh(x):
    return 0.5 * x * (1.0 + jnp.tanh(math.sqrt(2.0 / math.pi) * (x + 0.044715 * (x * x * x))))


def _mix_kernel(an_ref, yt_ref, u_ref, x_ref, mod_ref, d_ref, wglu_ref, bglu_ref, gos_ref,
                woa_ref, wob_ref, gpm_ref, gpf_ref, wr_ref, br_ref, tri_ref,
                x1_ref, h2_ref, eidx_ref, rank_ref, gw_ref, cnt_ref, run_ref, tile_ref):
    i = pl.program_id(0)
    tm = TM_MIX

    @pl.when(i == 0)
    def _():
        run_ref[...] = jnp.zeros_like(run_ref)

    gt_m = mod_ref[0, 2:3, :]
    sh_f = mod_ref[0, 3:4, :]
    sc_f = mod_ref[0, 4:5, :]
    ng = N_ROUTE_GROUPS
    gsz = N_EXPERTS // ng
    neg = -jnp.inf

    ts = tm // MIX_SUBTILES
    run = run_ref[:, 0:1]
    for s in range(MIX_SUBTILES):
        r = slice(s * ts, (s + 1) * ts)
        chunks = slice(s * (ts // S5_Q), (s + 1) * (ts // S5_Q))
        yy = _from_group_chunks(yt_ref.at[:, chunks, :], tile_ref.at[s]) + d_ref[...] * u_ref[r, :]
        g = _gelu_tanh(yy)
        gl = jnp.dot(g.astype(BF16), wglu_ref[...], preferred_element_type=F32) + bglu_ref[...]
        ob = g * jax.nn.sigmoid(gl)
        bn = _rms(ob, gos_ref[...]).astype(BF16)
        o = (jnp.dot(an_ref[r, :], woa_ref[...], preferred_element_type=F32)
             + jnp.dot(bn, wob_ref[...], preferred_element_type=F32))
        x1 = x_ref[r, :] + gt_m * _rms(o, gpm_ref[...])
        x1_ref[r, :] = x1
        h2 = _rms(x1, gpf_ref[...]) * (1.0 + sc_f) + sh_f
        h2_ref[r, :] = _pack_halves(h2)

        logits = lax.dot_general(wr_ref[...], h2, (((1,), (1,)), ((), ())),
                                 preferred_element_type=F32, precision=lax.Precision.HIGHEST)
        scores = jax.nn.sigmoid(logits)
        biased = scores + br_ref[...]
        b3 = biased.reshape(ng, gsz, ts)
        s3 = scores.reshape(ng, gsz, ts)
        sub = lax.broadcasted_iota(jnp.int32, (ng, gsz, ts), 1).astype(F32)
        grp = lax.broadcasted_iota(jnp.int32, (ng, gsz, ts), 0).astype(F32)
        eid = grp * gsz + sub
        m1 = jnp.max(b3, axis=1, keepdims=True)
        i1 = jnp.min(jnp.where(b3 == m1, sub, float(gsz)), axis=1, keepdims=True)
        m2 = jnp.max(jnp.where(sub == i1, neg, b3), axis=1, keepdims=True)
        gs = m1 + m2
        gi = lax.broadcasted_iota(jnp.int32, (ng, 1, ts), 0)
        beaten = jnp.zeros((ng, 1, ts), F32)
        for gp in range(ng):
            o_ = gs[gp:gp + 1]
            beats = (o_ > gs) | ((o_ == gs) & (gi > gp))
            beaten = beaten + beats.astype(F32)
        gmask = beaten < float(TOPK_ROUTE_GROUPS)
        masked = jnp.where(gmask, b3, neg)

        sels = []
        picked = jnp.zeros((ng, gsz, ts), F32)
        for k in range(TOP_K):
            m = jnp.max(jnp.max(masked, axis=0, keepdims=True), axis=1, keepdims=True)
            cand = jnp.where(masked == m, eid, float(N_EXPERTS))
            sel = jnp.min(jnp.min(cand, axis=0, keepdims=True), axis=1, keepdims=True)
            oh = eid == sel
            masked = jnp.where(oh, neg, masked)
            picked = jnp.where(oh, 1.0, picked)
            sels.append(sel)

        pm = picked.reshape(N_EXPERTS, ts)
        prefix = jnp.dot(pm.astype(BF16), tri_ref[:ts, :ts], preferred_element_type=F32) + run
        p3 = prefix.reshape(ng, gsz, ts)
        run = run + jnp.sum(pm, axis=1, keepdims=True)

        sc_rows = []
        for k in range(TOP_K):
            oh = eid == sels[k]
            sc_k = jnp.sum(jnp.sum(jnp.where(oh, s3, 0.0), axis=0, keepdims=True), axis=1, keepdims=True)
            rk_k = jnp.sum(jnp.sum(jnp.where(oh, p3, 0.0), axis=0, keepdims=True), axis=1, keepdims=True)
            sc_rows.append(sc_k)
            eidx_ref[k:k + 1, r] = sels[k].reshape(1, ts).astype(jnp.int32)
            rank_ref[k:k + 1, r] = rk_k.reshape(1, ts).astype(jnp.int32)
        tot = sc_rows[0]
        for k in range(1, TOP_K):
            tot = tot + sc_rows[k]
        inv = ROUTED_SCALE / (tot + 1e-20)
        for k in range(TOP_K):
            gw_ref[k:k + 1, r] = (sc_rows[k] * inv).reshape(1, ts)

    run_new = jnp.broadcast_to(run, run_ref.shape)
    run_ref[...] = run_new
    cnt_ref[...] = run_new


def _mix(half, a_n, yt, u2, x2, mod, d_skip, wglu_bf, b_glu, g_out_ssm, wo_a, wo_b, g_post_mix,
         g_pre_ffn, w_router_t, b_router_col, tri):
    tm = TM_MIX
    t0 = half * (HALF_TOK // tm)
    row = lambda n: pl.BlockSpec((1, n), lambda i: (0, 0))
    full = lambda a, b: pl.BlockSpec((a, b), lambda i: (0, 0))
    tok_in = lambda n: pl.BlockSpec((tm, n), lambda i: (t0 + i, 0))
    tok = lambda n: pl.BlockSpec((tm, n), lambda i: (i, 0))
    col = pl.BlockSpec((TOP_K, tm), lambda i: (0, i))
    return pl.pallas_call(
        _mix_kernel,
        grid=(HALF_TOK // tm,),
        in_specs=[tok_in(CONV_CH),
                  pl.BlockSpec((SSM_GROUPS, tm // S5_Q, S5_QH), lambda i: (0, t0 + i, 0)),
                  tok_in(SSM_CH), tok_in(D_MODEL),
                  pl.BlockSpec((1, 8, D_MODEL), lambda i: (half, 0, 0)),
                  row(SSM_CH), full(SSM_CH, SSM_CH), row(SSM_CH), row(SSM_CH),
                  full(CONV_CH, D_MODEL), full(SSM_CH, D_MODEL), row(D_MODEL), row(D_MODEL),
                  full(N_EXPERTS, D_MODEL), full(N_EXPERTS, 1), full(tm, tm)],
        out_specs=[tok(D_MODEL), tok(D_MODEL // 2), col, col, col,
                   pl.BlockSpec((N_EXPERTS, 128), lambda i: (0, 0))],
        out_shape=[jax.ShapeDtypeStruct((HALF_TOK, D_MODEL), F32),
                   jax.ShapeDtypeStruct((HALF_TOK, D_MODEL // 2), jnp.uint32),
                   jax.ShapeDtypeStruct((TOP_K, HALF_TOK), jnp.int32),
                   jax.ShapeDtypeStruct((TOP_K, HALF_TOK), jnp.int32),
                   jax.ShapeDtypeStruct((TOP_K, HALF_TOK), F32),
                   jax.ShapeDtypeStruct((N_EXPERTS, 128), F32)],
        scratch_shapes=[pltpu.VMEM((N_EXPERTS, 128), F32),
                        pltpu.VMEM((MIX_SUBTILES, SSM_CH // LANES, tm // MIX_SUBTILES, LANES), F32)],
        compiler_params=_cparams(("arbitrary",)),
        name="mix_out_router",
    )(a_n, yt, u2, x2, mod, d_skip, wglu_bf, b_glu, g_out_ssm, wo_a, wo_b, g_post_mix,
      g_pre_ffn, w_router_t, b_router_col, tri)


def _expert_kernel(blk0_ref, nblk_ref, xs_hbm, wg_ref, wu_ref, wd_ref, ys_hbm,
                   xbuf, ybuf, wgu_s, wd_s, sem_in, sem_out):
    e = pl.program_id(0)
    n = nblk_ref[e]
    b0 = blk0_ref[e]
    n_all = blk0_ref[N_EXPERTS - 1] + nblk_ref[N_EXPERTS - 1]

    def rows(b):
        return pl.ds(pl.multiple_of(b * ROW_BLOCK, ROW_BLOCK), ROW_BLOCK)

    def in_copy(b, slot):
        return pltpu.make_async_copy(xs_hbm.at[rows(b)], xbuf.at[slot], sem_in.at[slot])

    def out_copy(b, slot):
        return pltpu.make_async_copy(ybuf.at[slot], ys_hbm.at[rows(b)], sem_out.at[slot])

    for b in range(EXPERT_AHEAD):
        @pl.when((e == 0) & (b < n_all))
        def _():
            in_copy(b, b).start()

    wgu_s[:, :D_EXPERT] = wg_ref[0].astype(BF16)
    wgu_s[:, D_EXPERT:] = wu_ref[0].astype(BF16)
    wd_s[...] = wd_ref[0].astype(BF16)

    def admit(b):
        in_copy(b, b % EXPERT_SLOTS).wait()
        ahead = b + EXPERT_AHEAD

        @pl.when(ahead < n_all)
        def _():
            in_copy(ahead, ahead % EXPERT_SLOTS).start()

        @pl.when(b >= EXPERT_SLOTS)
        def _():
            out_copy(b - EXPERT_SLOTS, b % EXPERT_SLOTS).wait()

    def compute(b):
        slot = b % EXPERT_SLOTS
        x_lo, x_hi = _unpack_halves(xbuf[slot])
        x = jnp.concatenate([x_lo.astype(BF16), x_hi.astype(BF16)], axis=1)
        h = jnp.dot(x, wgu_s[...], preferred_element_type=F32)
        hg = h[:, :D_EXPERT]
        act = hg * jax.nn.sigmoid(hg) * h[:, D_EXPERT:]
        ybuf[slot] = _pack_halves(jnp.dot(act.astype(BF16), wd_s[...], preferred_element_type=F32))

    def run(blocks):
        for b in blocks:
            admit(b)
        for b in blocks:
            compute(b)
        for b in blocks:
            out_copy(b, b % EXPERT_SLOTS).start(priority=1)

    def pair(i, carry):
        b = b0 + 2 * i
        run([b, b + 1])
        return carry

    lax.fori_loop(0, n // 2, pair, 0)

    @pl.when(n % 2 == 1)
    def _():
        run([b0 + n - 1])

    @pl.when(e == N_EXPERTS - 1)
    def _():
        for j in range(1, EXPERT_SLOTS + 1):
            @pl.when(n_all >= j)
            def _():
                out_copy(n_all - j, (n_all - j) % EXPERT_SLOTS).wait()


def _experts(blk0, nblk, xs, we_gate, we_up, we_down):
    any_spec = pl.BlockSpec(memory_space=pl.ANY)
    grid_spec = pltpu.PrefetchScalarGridSpec(
        num_scalar_prefetch=2,
        grid=(N_EXPERTS,),
        in_specs=[any_spec,
                  pl.BlockSpec((1, D_MODEL, D_EXPERT), lambda e, b0, nb: (e, 0, 0)),
                  pl.BlockSpec((1, D_MODEL, D_EXPERT), lambda e, b0, nb: (e, 0, 0)),
                  pl.BlockSpec((1, D_EXPERT, D_MODEL), lambda e, b0, nb: (e, 0, 0))],
        out_specs=any_spec,
        scratch_shapes=[pltpu.VMEM((EXPERT_SLOTS, ROW_BLOCK, D_MODEL // 2), jnp.uint32),
                        pltpu.VMEM((EXPERT_SLOTS, ROW_BLOCK, D_MODEL // 2), jnp.uint32),
                        pltpu.VMEM((D_MODEL, 2 * D_EXPERT), BF16),
                        pltpu.VMEM((D_EXPERT, D_MODEL), BF16),
                        pltpu.SemaphoreType.DMA((EXPERT_SLOTS,)),
                        pltpu.SemaphoreType.DMA((EXPERT_SLOTS,))],
    )
    return pl.pallas_call(
        _expert_kernel,
        grid_spec=grid_spec,
        out_shape=jax.ShapeDtypeStruct((N_ROWS, D_MODEL // 2), jnp.uint32),
        compiler_params=_cparams(("arbitrary",)),
        name="routed_experts",
    )(blk0, nblk, xs, we_gate, we_up, we_down)


def _final_kernel(h2_ref, yg_ref, gw_ref, x1_ref, mod_ref, wgu_ref, wd_ref, g_ref, *rest):
    o_ref = rest[-1]
    half = D_MODEL // 2
    gt_f = mod_ref[0, 5:6, :]
    x_lo, x_hi = _unpack_halves(h2_ref[...])
    h = (jnp.dot(x_lo.astype(BF16), wgu_ref[:half, :], preferred_element_type=F32)
         + jnp.dot(x_hi.astype(BF16), wgu_ref[half:, :], preferred_element_type=F32))
    hg = h[:, :D_EXPERT]
    act = hg * jax.nn.sigmoid(hg) * h[:, D_EXPERT:]
    shared = jnp.dot(act.astype(BF16), wd_ref[...], preferred_element_type=F32)
    y_lo = shared[:, :half]
    y_hi = shared[:, half:]
    for k in range(TOP_K):
        r_lo, r_hi = _unpack_halves(yg_ref[k])
        w = gw_ref[:, k:k + 1]
        y_lo = y_lo + w * r_lo
        y_hi = y_hi + w * r_hi
    ms = (jnp.sum(y_lo * y_lo, axis=-1, keepdims=True)
          + jnp.sum(y_hi * y_hi, axis=-1, keepdims=True)) * (1.0 / D_MODEL)
    inv = lax.rsqrt(ms + NORM_EPS)
    o_ref[:, :half] = x1_ref[:, :half] + gt_f[:, :half] * (y_lo * inv * g_ref[:, :half])
    o_ref[:, half:] = x1_ref[:, half:] + gt_f[:, half:] * (y_hi * inv * g_ref[:, half:])


def _final(half, out_prev, h2p, yg, gw_t, x1, mod, ws_gu, ws_d, g_post_ffn):
    tm = TM_OUT
    t0 = half * (HALF_TOK // tm)
    tok = pl.BlockSpec((tm, D_MODEL), lambda i: (i, 0))
    in_specs = [pl.BlockSpec((tm, D_MODEL // 2), lambda i: (i, 0)),
                pl.BlockSpec((TOP_K, tm, D_MODEL // 2), lambda i: (0, i, 0)),
                pl.BlockSpec((tm, TOP_K), lambda i: (i, 0)),
                tok,
                pl.BlockSpec((1, 8, D_MODEL), lambda i: (half, 0, 0)),
                pl.BlockSpec((D_MODEL, 2 * D_EXPERT), lambda i: (0, 0)),
                pl.BlockSpec((D_EXPERT, D_MODEL), lambda i: (0, 0)),
                pl.BlockSpec((1, D_MODEL), lambda i: (0, 0))]
    args = [h2p, yg, gw_t, x1, mod, ws_gu, ws_d, g_post_ffn]
    aliases = {}
    if out_prev is not None:
        aliases = {len(args): 0}
        in_specs.append(pl.BlockSpec(memory_space=pl.ANY))
        args.append(out_prev)
    return pl.pallas_call(
        _final_kernel,
        grid=(HALF_TOK // tm,),
        in_specs=in_specs,
        out_specs=pl.BlockSpec((tm, D_MODEL), lambda i: (t0 + i, 0)),
        out_shape=jax.ShapeDtypeStruct((N_TOK, D_MODEL), F32),
        input_output_aliases=aliases,
        compiler_params=_cparams(("parallel",)),
        name="shared_final",
    )(*args)


def _sc_worker_id():
    return lax.axis_index("s") * SC_CORES + lax.axis_index("c")


def _dispatch_body(h_hbm, dest_hbm, xs_hbm, idx_v, rows_v, sem_l, sem_s):
    n = SC_CHUNKS_PER_WORKER
    c0 = _sc_worker_id() * n

    def load(i, b):
        return pltpu.async_copy(h_hbm.at[pl.ds((c0 + i) * SC_W, SC_W)], rows_v.at[b], sem_l.at[b])

    loads = [None] * n
    scat = [None] * n
    loads[0] = load(0, 0)
    for i in range(n):
        b = i % 2
        pltpu.sync_copy(dest_hbm.at[c0 + i], idx_v.at[b])
        loads[i].wait()
        if i + 1 < n:
            if i >= 1:
                for d in scat[i - 1]:
                    d.wait()
            loads[i + 1] = load(i + 1, 1 - b)
        scat[i] = [pltpu.async_copy(rows_v.at[b], xs_hbm.at[idx_v.at[b].at[k]], sem_s.at[b])
                   for k in range(TOP_K)]
    for i in (n - 2, n - 1):
        for d in scat[i]:
            d.wait()


def _sc_dispatch(h2p, dest3):
    mesh = plsc.VectorSubcoreMesh(core_axis_name="c", subcore_axis_name="s")
    return pl.kernel(
        _dispatch_body, mesh=mesh,
        out_type=jax.ShapeDtypeStruct((N_ROWS, D_MODEL // 2), jnp.uint32),
        scratch_types=[pltpu.VMEM((2, TOP_K, SC_W), jnp.int32),
                       pltpu.VMEM((2, SC_W, D_MODEL // 2), jnp.uint32),
                       pltpu.SemaphoreType.DMA((2,)), pltpu.SemaphoreType.DMA((2,))],
    )(h2p, dest3)


def _combine_body(ys_hbm, dest_hbm, yg_hbm, idx_v, rows_v, sem_g, sem_w):
    c0 = _sc_worker_id() * SC_CHUNKS_PER_WORKER

    @pl.loop(0, SC_CHUNKS_PER_WORKER)
    def _(i):
        c = c0 + i
        pltpu.sync_copy(dest_hbm.at[c], idx_v)
        g = [None] * TOP_K
        w = [None] * TOP_K
        g[0] = pltpu.async_copy(ys_hbm.at[idx_v.at[0]], rows_v.at[0], sem_g.at[0])
        for k in range(TOP_K):
            b = k % 2
            g[k].wait()
            if k + 1 < TOP_K:
                if k >= 1:
                    w[k - 1].wait()
                g[k + 1] = pltpu.async_copy(ys_hbm.at[idx_v.at[k + 1]], rows_v.at[1 - b], sem_g.at[1 - b])
            w[k] = pltpu.async_copy(rows_v.at[b], yg_hbm.at[k].at[pl.ds(c * SC_W, SC_W)], sem_w.at[b])
        w[TOP_K - 2].wait()
        w[TOP_K - 1].wait()


def _sc_combine(ysp, dest3):
    mesh = plsc.VectorSubcoreMesh(core_axis_name="c", subcore_axis_name="s")
    return pl.kernel(
        _combine_body, mesh=mesh,
        out_type=jax.ShapeDtypeStruct((TOP_K, HALF_TOK, D_MODEL // 2), jnp.uint32),
        scratch_types=[pltpu.VMEM((TOP_K, SC_W), jnp.int32),
                       pltpu.VMEM((2, SC_W, D_MODEL // 2), jnp.uint32),
                       pltpu.SemaphoreType.DMA((2,)), pltpu.SemaphoreType.DMA((2,))],
    )(ysp, dest3)


def kernel(x, c, w_ada, b_ada, g_pre_mix, g_post_mix, w_in, conv_w, conv_b, conv_ln_g, conv_ln_b,
           ssm_a_re, ssm_a_im, ssm_log_dt, ssm_b_re, ssm_b_im, ssm_c_re, ssm_c_im, ssm_d,
           ssm_w_glu, ssm_b_glu, g_out_conv, g_out_ssm, w_out, g_pre_ffn, g_post_ffn,
           w_router, b_router, we_gate, we_up, we_down, ws_gate, ws_up, ws_down):
    l = 0
    x2 = x.reshape(N_TOK, D_MODEL)
    r1 = lambda a: a.reshape(1, -1)

    c_pad = jnp.zeros((8, D_MODEL), F32).at[:BATCH].set(c)
    mod = _ada(c_pad, w_ada[l], r1(b_ada[l]))[:BATCH].reshape(BATCH, 6, D_MODEL)
    mod = jnp.concatenate([mod, jnp.zeros((BATCH, 2, D_MODEL), F32)], axis=1)

    v, u, ut = _inproj(x2, mod, r1(g_pre_mix[l]), w_in[l].astype(BF16))
    cw = jnp.concatenate([conv_w[l].reshape(CONV_WIDTH, CONV_CH), jnp.zeros((1, CONV_CH), F32)], axis=0)
    a_n = _conv(v.reshape(BATCH, SEQ, CONV_CH), cw, r1(conv_b[l]), r1(conv_ln_g[l]),
                r1(conv_ln_b[l]), r1(g_out_conv[l])).reshape(N_TOK, CONV_CH)

    pwr, pwi, s5_params, a_cat, b_q, b_s = _s5_operators(
        ssm_a_re[l], ssm_a_im[l], ssm_log_dt[l], ssm_b_re[l], ssm_b_im[l], ssm_c_re[l], ssm_c_im[l])
    yt = _s5(ut, pwr, pwi, s5_params, a_cat, b_q, b_s)

    tm = TM_MIX
    tri = (jnp.arange(tm)[:, None] < jnp.arange(tm)[None, :]).astype(BF16)
    wo = w_out[l].astype(BF16)
    mix_params = (r1(ssm_d[l]), ssm_w_glu[l].astype(BF16), r1(ssm_b_glu[l]), r1(g_out_ssm[l]),
                  wo[:CONV_CH], wo[CONV_CH:], r1(g_post_mix[l]), r1(g_pre_ffn[l]),
                  w_router[l].T, b_router[l].reshape(N_EXPERTS, 1), tri)
    ws_gu = jnp.concatenate([ws_gate[l], ws_up[l]], axis=1).astype(BF16)
    ws_d = ws_down[l].astype(BF16)
    e_ids = jnp.arange(N_EXPERTS, dtype=jnp.int32)

    out = None
    for half in range(N_HALVES):
        x1, h2, eidx, rank, gw, cnt = _mix(half, a_n, yt, u, x2, mod, *mix_params)
        counts = cnt[:, 0].astype(jnp.int32)
        padded = (counts + ROW_BLOCK - 1) // ROW_BLOCK * ROW_BLOCK
        pstart = jnp.cumsum(padded) - padded
        dest = rank + jnp.sum(jnp.where(eidx[..., None] == e_ids, pstart, 0), axis=-1)
        dest3 = dest.reshape(TOP_K, HALF_TOK // SC_W, SC_W).transpose(1, 0, 2)

        xs = _sc_dispatch(h2, dest3)
        ys = _experts(pstart // ROW_BLOCK, padded // ROW_BLOCK, xs, we_gate[l], we_up[l], we_down[l])
        yg = _sc_combine(ys, dest3)
        out = _final(half, out, h2, yg, gw.T, x1, mod, ws_gu, ws_d, r1(g_post_ffn[l]))
    return out.reshape(BATCH, SEQ, D_MODEL)
```

```python
import functools
import math

import jax
import jax.numpy as jnp
from jax import lax
from jax.experimental import pallas as pl
from jax.experimental.pallas import tpu as pltpu
from jax.experimental.pallas import tpu_sc as plsc

F32 = jnp.float32
BF16 = jnp.bfloat16

D_MODEL = 1024
BATCH = 2
SEQ = 8192
N_TOK = BATCH * SEQ
CONV_CH = 512
CONV_WIDTH = 31
SSM_CH = 512
SSM_GROUP_CH = 16
SSM_GROUPS = 32
SSM_STATE = 64
D_IN = 2 * CONV_CH + SSM_CH
N_EXPERTS = 64
TOP_K = 8
N_ROUTE_GROUPS = 8
TOPK_ROUTE_GROUPS = 4
D_EXPERT = 256
ROUTED_SCALE = 2.5
NORM_EPS = 1e-6

SUBLANES = 8
LANES = 128

TM_IN = 512
IN_SUBTILES = 2
TL_CONV = 512
CONV_HALO = 32
CONV_ROWS = 64
S5_Q = 32
S5_QH = S5_Q * SSM_GROUP_CH
S5_CHUNKS = N_TOK // S5_Q
S5_CHUNKS_PER_SEQ = SEQ // S5_Q
TM_MIX = 512
MIX_SUBTILES = 1
ROW_BLOCK = 512
EXPERT_AHEAD = 6
EXPERT_SLOTS = 8
HALF_TOK = SEQ
N_HALVES = N_TOK // HALF_TOK
N_BLOCKS = HALF_TOK * TOP_K // ROW_BLOCK + N_EXPERTS
N_ROWS = N_BLOCKS * ROW_BLOCK
TM_OUT = 512
SC_CORES = 2
SC_SUBCORES = 16
SC_WORKERS = SC_CORES * SC_SUBCORES
SC_W = 64
SC_CHUNKS_PER_WORKER = HALF_TOK // (SC_WORKERS * SC_W)
VMEM_LIMIT = 48 * 1024 * 1024


def _cparams(sem):
    return pltpu.CompilerParams(dimension_semantics=sem, vmem_limit_bytes=VMEM_LIMIT)


def _pack_halves(x):
    n = x.shape[-1] // 2
    lo = lax.bitcast_convert_type(x[:, :n].astype(BF16).astype(F32), jnp.uint32)
    hi = lax.bitcast_convert_type(x[:, n:].astype(BF16).astype(F32), jnp.uint32)
    return hi | (lo >> 16)


def _unpack_halves(p):
    lo = lax.bitcast_convert_type(p << 16, F32)
    hi = lax.bitcast_convert_type(p & jnp.uint32(0xFFFF0000), F32)
    return lo, hi


def _rms(x, g):
    return x * lax.rsqrt(jnp.mean(x * x, axis=-1, keepdims=True) + NORM_EPS) * g


def _ada_kernel(c_ref, w_ref, b_ref, o_ref):
    c = c_ref[...]
    a = c * jax.nn.sigmoid(c)
    o_ref[...] = jnp.dot(a, w_ref[...], preferred_element_type=F32,
                         precision=lax.Precision.HIGHEST) + b_ref[...]


def _ada(c_pad, w_ada, b_ada):
    n = w_ada.shape[1]
    bn = 1536
    return pl.pallas_call(
        _ada_kernel,
        grid=(n // bn,),
        in_specs=[pl.BlockSpec((8, D_MODEL), lambda j: (0, 0)),
                  pl.BlockSpec((D_MODEL, bn), lambda j: (0, j)),
                  pl.BlockSpec((1, bn), lambda j: (0, j))],
        out_specs=pl.BlockSpec((8, bn), lambda j: (0, j)),
        out_shape=jax.ShapeDtypeStruct((8, n), F32),
        compiler_params=_cparams(("arbitrary",)),
        name="ada_mod",
    )(c_pad, w_ada, b_ada)


GROUPS_PER_LANE_TILE = LANES // SSM_GROUP_CH


def _to_group_chunks(u, tile_ref, ut_ref):
    n_chunks = u.shape[0] // S5_Q
    for j in range(SSM_CH // LANES):
        tile_ref[j] = u[:, LANES * j:LANES * (j + 1)]
    for j in range(SSM_CH // LANES):
        rows_t = [tile_ref[j, pl.ds(t, n_chunks, stride=S5_Q), :] for t in range(S5_Q)]
        for gg in range(GROUPS_PER_LANE_TILE):
            lo = gg * SSM_GROUP_CH
            row = jnp.concatenate([r[:, lo:lo + SSM_GROUP_CH] for r in rows_t], axis=1)
            ut_ref[j * GROUPS_PER_LANE_TILE + gg] = row.astype(ut_ref.dtype)


def _from_group_chunks(yt_ref, tile_ref):
    n_chunks = yt_ref.shape[1]
    for j in range(SSM_CH // LANES):
        for t in range(S5_Q):
            lo = t * SSM_GROUP_CH
            piece = jnp.concatenate(
                [yt_ref[j * GROUPS_PER_LANE_TILE + gg, :, lo:lo + SSM_GROUP_CH]
                 for gg in range(GROUPS_PER_LANE_TILE)], axis=1)
            tile_ref[j, pl.ds(t, n_chunks, stride=S5_Q), :] = piece
    return jnp.concatenate([tile_ref[j] for j in range(SSM_CH // LANES)], axis=1)


def _inproj_kernel(x_ref, mod_ref, g_ref, w_ref, v_ref, u_ref, ut_ref, tile_ref):
    sh = mod_ref[0, 0:1, :]
    sc = mod_ref[0, 1:2, :]
    sub = TM_IN // IN_SUBTILES
    sub_chunks = sub // S5_Q
    for s in range(IN_SUBTILES):
        r = slice(s * sub, (s + 1) * sub)
        h = _rms(x_ref[r, :], g_ref[...]) * (1.0 + sc) + sh
        z = jnp.dot(h.astype(BF16), w_ref[...], preferred_element_type=F32)
        v_ref[r, :] = z[:, :CONV_CH] * jax.nn.sigmoid(z[:, CONV_CH:2 * CONV_CH])
        u = z[:, 2 * CONV_CH:]
        u_ref[r, :] = u
        _to_group_chunks(u, tile_ref.at[s], ut_ref.at[:, s * sub_chunks:(s + 1) * sub_chunks, :])


def _inproj(x2, mod, g_pre, w_in_bf):
    tiles_per_seq = SEQ // TM_IN
    return pl.pallas_call(
        _inproj_kernel,
        grid=(N_TOK // TM_IN,),
        in_specs=[pl.BlockSpec((TM_IN, D_MODEL), lambda i: (i, 0)),
                  pl.BlockSpec((1, 8, D_MODEL), lambda i: (i // tiles_per_seq, 0, 0)),
                  pl.BlockSpec((1, D_MODEL), lambda i: (0, 0)),
                  pl.BlockSpec((D_MODEL, D_IN), lambda i: (0, 0))],
        out_specs=[pl.BlockSpec((TM_IN, CONV_CH), lambda i: (i, 0)),
                   pl.BlockSpec((TM_IN, SSM_CH), lambda i: (i, 0)),
                   pl.BlockSpec((SSM_GROUPS, TM_IN // S5_Q, S5_QH), lambda i: (0, i, 0))],
        out_shape=[jax.ShapeDtypeStruct((N_TOK, CONV_CH), F32),
                   jax.ShapeDtypeStruct((N_TOK, SSM_CH), F32),
                   jax.ShapeDtypeStruct((SSM_GROUPS, S5_CHUNKS, S5_QH), BF16)],
        scratch_shapes=[pltpu.VMEM((IN_SUBTILES, SSM_CH // LANES, TM_IN // IN_SUBTILES, LANES), F32)],
        compiler_params=_cparams(("parallel",)),
        name="in_proj",
    )(x2, mod, g_pre, w_in_bf)


def _conv_kernel(vc_ref, vp_ref, w_ref, cb_ref, lg_ref, lb_ref, go_ref, o_ref, sh_ref):
    i = pl.program_id(1)
    keep = (i > 0).astype(F32)
    n_ext = TL_CONV + CONV_HALO
    sh_ref[0, 0:CONV_HALO, :] = vp_ref[0] * keep
    sh_ref[0, CONV_HALO:, :] = vc_ref[0]
    for s in range(1, SUBLANES):
        sh_ref[s, 0:n_ext - s, :] = sh_ref[0, s:n_ext, :]
    off = CONV_HALO - (CONV_WIDTH - 1)
    for r in range(TL_CONV // CONV_ROWS):
        acc = None
        for j in range(CONV_WIDTH):
            s = (off + j) % SUBLANES
            al = r * CONV_ROWS + (off + j) - s
            term = w_ref[j:j + 1, :] * sh_ref[s, al:al + CONV_ROWS, :]
            acc = term if acc is None else acc + term
        y = acc + cb_ref[...]
        mu = jnp.mean(y, axis=-1, keepdims=True)
        d = y - mu
        var = jnp.mean(d * d, axis=-1, keepdims=True)
        yn = d * lax.rsqrt(var + NORM_EPS) * lg_ref[...] + lb_ref[...]
        a = yn * jax.nn.sigmoid(yn)
        o_ref[0, r * CONV_ROWS:(r + 1) * CONV_ROWS, :] = _rms(a, go_ref[...]).astype(BF16)


def _conv(v3, conv_w, conv_b, ln_g, ln_b, g_out):
    halo_per_tile = TL_CONV // CONV_HALO
    vec = pl.BlockSpec((1, CONV_CH), lambda b, i: (0, 0))
    return pl.pallas_call(
        _conv_kernel,
        grid=(BATCH, SEQ // TL_CONV),
        in_specs=[pl.BlockSpec((1, TL_CONV, CONV_CH), lambda b, i: (b, i, 0)),
                  pl.BlockSpec((1, CONV_HALO, CONV_CH),
                               lambda b, i: (b, jnp.maximum(i * halo_per_tile - 1, 0), 0)),
                  pl.BlockSpec((CONV_WIDTH + 1, CONV_CH), lambda b, i: (0, 0)),
                  vec, vec, vec, vec],
        out_specs=pl.BlockSpec((1, TL_CONV, CONV_CH), lambda b, i: (b, i, 0)),
        out_shape=jax.ShapeDtypeStruct((BATCH, SEQ, CONV_CH), BF16),
        scratch_shapes=[pltpu.VMEM((SUBLANES, TL_CONV + CONV_HALO, CONV_CH), F32)],
        compiler_params=_cparams(("parallel", "arbitrary")),
        name="conv_module",
    )(v3, v3, conv_w, conv_b, ln_g, ln_b, g_out)


S5_GROUP_ROWS = S5_CHUNKS + 8


S5_POW_ROWS = (S5_Q + 1 + SUBLANES - 1) // SUBLANES * SUBLANES
(S5_BB_RI, S5_BB_NIR, S5_BB_IR, S5_BB_RNI, S5_CC_RI, S5_CC_NIR, S5_N_PARAM) = range(7)


def _s5_kernel(ut_ref, pwr_ref, pwi_ref, par_ref, a_ref, bq_ref, bs_ref, yt_ref, sin_s, sp_s):
    phase = pl.program_id(0)
    g = pl.program_id(1)
    q = S5_Q
    n = 2 * SSM_STATE
    row0 = pl.multiple_of(g * S5_GROUP_ROWS, 8)

    def lam_pow(j):
        return pwr_ref[0, j:j + 1, :], pwi_ref[0, j:j + 1, :]

    @pl.when(phase == 0)
    def _():
        bb_ri, bb_nir = par_ref[0, S5_BB_RI], par_ref[0, S5_BB_NIR]
        bb_ir, bb_rni = par_ref[0, S5_BB_IR], par_ref[0, S5_BB_RNI]
        blk_q, blk_s = [], []
        for t in range(q):
            pr, pi_ = lam_pow(q - 1 - t)
            blk_q.append(pr * bb_ri + pi_ * bb_nir)
            blk_s.append(pr * bb_ir + pi_ * bb_rni)
        wst = jnp.concatenate([jnp.concatenate(blk_q, axis=0), jnp.concatenate(blk_s, axis=0)], axis=1)
        r = jnp.dot(ut_ref[0], wst.astype(BF16), preferred_element_type=F32)
        sin_s[0, pl.ds(row0, S5_CHUNKS), :] = r[:, :n]
        sin_s[1, pl.ds(row0, S5_CHUNKS), :] = r[:, n:]

    @pl.when((phase == 1) & (g == 0))
    def _():
        a = a_ref[...]
        bq = bq_ref[...]
        bs = bs_ref[...]

        def body(c, carry):
            nxt = []
            for b in range(BATCH):
                x, xs = carry[b]
                rows = pl.ds(b * S5_CHUNKS_PER_SEQ + c, SSM_GROUPS, stride=S5_GROUP_ROWS)
                sp_s[rows, :] = x
                nxt.append((a * x + bq * xs + sin_s[0, rows, :], a * xs + bs * x + sin_s[1, rows, :]))
            return tuple(nxt)

        z = jnp.zeros((SSM_GROUPS, n), F32)
        lax.fori_loop(0, S5_CHUNKS_PER_SEQ, body, tuple((z, z) for _ in range(BATCH)))

    @pl.when(phase == 1)
    def _():
        cc_ri, cc_nir = par_ref[0, S5_CC_RI], par_ref[0, S5_CC_NIR]
        cl = []
        for j in range(q + 1):
            pr, pi_ = lam_pow(j)
            cl.append(pr * cc_ri + pi_ * cc_nir)
        cl_lo = jnp.concatenate(cl[:q], axis=0)
        cl_hi = jnp.concatenate(cl[1:], axis=0)
        lane = lax.broadcasted_iota(jnp.int32, (1, n), 1)
        vgt = (cl_hi * jnp.where(lane < SSM_STATE, 1.0, -1.0)).astype(BF16)
        kt = lax.dot_general(par_ref[0, S5_BB_RNI], cl_lo, (((1,), (1,)), ((), ())),
                             preferred_element_type=F32, precision=lax.Precision.HIGHEST)
        padded = jnp.concatenate([jnp.zeros_like(kt), kt], axis=1)
        tg = jnp.concatenate(
            [padded[:, (q - t) * SSM_GROUP_CH:(q - t) * SSM_GROUP_CH + S5_QH] for t in range(q)],
            axis=0).astype(BF16)
        sp = sp_s[pl.ds(row0, S5_CHUNKS), :]
        y = jnp.dot(ut_ref[0], tg, preferred_element_type=F32)
        yt_ref[0] = y + lax.dot_general(sp.astype(BF16), vgt, (((1,), (1,)), ((), ())),
                                        preferred_element_type=F32)


def _s5(ut, pwr, pwi, params, a_cat, b_q, b_s):
    vec = pl.BlockSpec((SSM_GROUPS, 2 * SSM_STATE), lambda p, g: (0, 0))
    powers = pl.BlockSpec((1, S5_POW_ROWS, 2 * SSM_STATE), lambda p, g: (g, 0, 0))
    return pl.pallas_call(
        _s5_kernel,
        grid=(2, SSM_GROUPS),
        in_specs=[pl.BlockSpec((1, S5_CHUNKS, S5_QH), lambda p, g: (g, 0, 0)),
                  powers, powers,
                  pl.BlockSpec((1, S5_N_PARAM, SSM_GROUP_CH, 2 * SSM_STATE), lambda p, g: (g, 0, 0, 0)),
                  vec, vec, vec],
        out_specs=pl.BlockSpec((1, S5_CHUNKS, S5_QH), lambda p, g: (g * p, 0, 0)),
        out_shape=jax.ShapeDtypeStruct((SSM_GROUPS, S5_CHUNKS, S5_QH), F32),
        scratch_shapes=[pltpu.VMEM((2, SSM_GROUPS * S5_GROUP_ROWS, 2 * SSM_STATE), F32),
                        pltpu.VMEM((SSM_GROUPS * S5_GROUP_ROWS, 2 * SSM_STATE), F32)],
        compiler_params=_cparams(("arbitrary", "arbitrary")),
        name="s5_chunked",
    )(ut, pwr, pwi, params, a_cat, b_q, b_s)


def _s5_operators(a_re, a_im, log_dt, b_re, b_im, c_re, c_im):
    q = S5_Q
    dt = jnp.exp(log_dt)[:, None]
    ar, ai = a_re, a_im
    mag = jnp.exp(ar * dt)
    lr = mag * jnp.cos(ai * dt)
    li = mag * jnp.sin(ai * dt)
    den = ar * ar + ai * ai
    nr = lr - 1.0
    kr = (nr * ar + li * ai) / den
    ki = (li * ar - nr * ai) / den
    bbr = kr[..., None] * b_re - ki[..., None] * b_im
    bbi = kr[..., None] * b_im + ki[..., None] * b_re
    j = jnp.arange(q + 1, dtype=F32)[None, :, None]
    pmag = jnp.exp(ar[:, None, :] * dt[:, :, None] * j)
    pang = ai[:, None, :] * dt[:, :, None] * j
    pr = pmag * jnp.cos(pang)
    pi_ = pmag * jnp.sin(pang)
    pad = ((0, 0), (0, S5_POW_ROWS - (q + 1)), (0, 0))
    pwr = jnp.pad(jnp.concatenate([pr, pr], axis=-1), pad)
    pwi = jnp.pad(jnp.concatenate([pi_, pi_], axis=-1), pad)
    br_t = bbr.transpose(0, 2, 1)
    bi_t = bbi.transpose(0, 2, 1)
    cat = lambda a, b: jnp.concatenate([a, b], axis=-1)
    stack = [None] * S5_N_PARAM
    stack[S5_BB_RI] = cat(br_t, bi_t)
    stack[S5_BB_NIR] = cat(-bi_t, br_t)
    stack[S5_BB_IR] = cat(bi_t, br_t)
    stack[S5_BB_RNI] = cat(br_t, -bi_t)
    stack[S5_CC_RI] = cat(c_re, c_im)
    stack[S5_CC_NIR] = cat(-c_im, c_re)
    params = jnp.stack(stack, axis=1)
    aq_r, aq_i = pr[:, q], pi_[:, q]
    a_cat = cat(aq_r, aq_r)
    b_q = cat(-aq_i, aq_i)
    b_s = cat(aq_i, -aq_i)
    return pwr, pwi, params, a_cat, b_q, b_s


def _gelu_tanh(x):
    return 0.5 * x * (1.0 + jnp.tanh(math.sqrt(2.0 / math.pi) * (x + 0.044715 * (x * x * x))))


def _mix_kernel(an_ref, yt_ref, u_ref, x_ref, mod_ref, d_ref, wglu_ref, bglu_ref, gos_ref,
                woa_ref, wob_ref, gpm_ref, gpf_ref, wr_ref, br_ref, tri_ref,
                x1_ref, h2_ref, eidx_ref, rank_ref, gw_ref, cnt_ref, run_ref, tile_ref):
    i = pl.program_id(0)
    tm = TM_MIX

    @pl.when(i == 0)
    def _():
        run_ref[...] = jnp.zeros_like(run_ref)

    gt_m = mod_ref[0, 2:3, :]
    sh_f = mod_ref[0, 3:4, :]
    sc_f = mod_ref[0, 4:5, :]
    ng = N_ROUTE_GROUPS
    gsz = N_EXPERTS // ng
    neg = -jnp.inf

    ts = tm // MIX_SUBTILES
    run = run_ref[:, 0:1]
    for s in range(MIX_SUBTILES):
        r = slice(s * ts, (s + 1) * ts)
        chunks = slice(s * (ts // S5_Q), (s + 1) * (ts // S5_Q))
        yy = _from_group_chunks(yt_ref.at[:, chunks, :], tile_ref.at[s]) + d_ref[...] * u_ref[r, :]
        g = _gelu_tanh(yy)
        gl = jnp.dot(g.astype(BF16), wglu_ref[...], preferred_element_type=F32) + bglu_ref[...]
        ob = g * jax.nn.sigmoid(gl)
        bn = _rms(ob, gos_ref[...]).astype(BF16)
        o = (jnp.dot(an_ref[r, :], woa_ref[...], preferred_element_type=F32)
             + jnp.dot(bn, wob_ref[...], preferred_element_type=F32))
        x1 = x_ref[r, :] + gt_m * _rms(o, gpm_ref[...])
        x1_ref[r, :] = x1
        h2 = _rms(x1, gpf_ref[...]) * (1.0 + sc_f) + sh_f
        h2_ref[r, :] = _pack_halves(h2)

        logits = lax.dot_general(wr_ref[...], h2, (((1,), (1,)), ((), ())),
                                 preferred_element_type=F32, precision=lax.Precision.HIGHEST)
        scores = jax.nn.sigmoid(logits)
        biased = scores + br_ref[...]
        b3 = biased.reshape(ng, gsz, ts)
        s3 = scores.reshape(ng, gsz, ts)
        sub = lax.broadcasted_iota(jnp.int32, (ng, gsz, ts), 1).astype(F32)
        grp = lax.broadcasted_iota(jnp.int32, (ng, gsz, ts), 0).astype(F32)
        eid = grp * gsz + sub
        m1 = jnp.max(b3, axis=1, keepdims=True)
        i1 = jnp.min(jnp.where(b3 == m1, sub, float(gsz)), axis=1, keepdims=True)
        m2 = jnp.max(jnp.where(sub == i1, neg, b3), axis=1, keepdims=True)
        gs = m1 + m2
        gi = lax.broadcasted_iota(jnp.int32, (ng, 1, ts), 0)
        beaten = jnp.zeros((ng, 1, ts), F32)
        for gp in range(ng):
            o_ = gs[gp:gp + 1]
            beats = (o_ > gs) | ((o_ == gs) & (gi > gp))
            beaten = beaten + beats.astype(F32)
        gmask = beaten < float(TOPK_ROUTE_GROUPS)
        masked = jnp.where(gmask, b3, neg)

        sels = []
        picked = jnp.zeros((ng, gsz, ts), F32)
        for k in range(TOP_K):
            m = jnp.max(jnp.max(masked, axis=0, keepdims=True), axis=1, keepdims=True)
            cand = jnp.where(masked == m, eid, float(N_EXPERTS))
            sel = jnp.min(jnp.min(cand, axis=0, keepdims=True), axis=1, keepdims=True)
            oh = eid == sel
            masked = jnp.where(oh, neg, masked)
            picked = jnp.where(oh, 1.0, picked)
            sels.append(sel)

        pm = picked.reshape(N_EXPERTS, ts)
        prefix = jnp.dot(pm.astype(BF16), tri_ref[:ts, :ts], preferred_element_type=F32) + run
        p3 = prefix.reshape(ng, gsz, ts)
        run = run + jnp.sum(pm, axis=1, keepdims=True)

        sc_rows = []
        for k in range(TOP_K):
            oh = eid == sels[k]
            sc_k = jnp.sum(jnp.sum(jnp.where(oh, s3, 0.0), axis=0, keepdims=True), axis=1, keepdims=True)
            rk_k = jnp.sum(jnp.sum(jnp.where(oh, p3, 0.0), axis=0, keepdims=True), axis=1, keepdims=True)
            sc_rows.append(sc_k)
            eidx_ref[k:k + 1, r] = sels[k].reshape(1, ts).astype(jnp.int32)
            rank_ref[k:k + 1, r] = rk_k.reshape(1, ts).astype(jnp.int32)
        tot = sc_rows[0]
        for k in range(1, TOP_K):
            tot = tot + sc_rows[k]
        inv = ROUTED_SCALE / (tot + 1e-20)
        for k in range(TOP_K):
            gw_ref[k:k + 1, r] = (sc_rows[k] * inv).reshape(1, ts)

    run_new = jnp.broadcast_to(run, run_ref.shape)
    run_ref[...] = run_new
    cnt_ref[...] = run_new


def _mix(half, a_n, yt, u2, x2, mod, d_skip, wglu_bf, b_glu, g_out_ssm, wo_a, wo_b, g_post_mix,
         g_pre_ffn, w_router_t, b_router_col, tri):
    tm = TM_MIX
    t0 = half * (HALF_TOK // tm)
    row = lambda n: pl.BlockSpec((1, n), lambda i: (0, 0))
    full = lambda a, b: pl.BlockSpec((a, b), lambda i: (0, 0))
    tok_in = lambda n: pl.BlockSpec((tm, n), lambda i: (t0 + i, 0))
    tok = lambda n: pl.BlockSpec((tm, n), lambda i: (i, 0))
    col = pl.BlockSpec((TOP_K, tm), lambda i: (0, i))
    return pl.pallas_call(
        _mix_kernel,
        grid=(HALF_TOK // tm,),
        in_specs=[tok_in(CONV_CH),
                  pl.BlockSpec((SSM_GROUPS, tm // S5_Q, S5_QH), lambda i: (0, t0 + i, 0)),
                  tok_in(SSM_CH), tok_in(D_MODEL),
                  pl.BlockSpec((1, 8, D_MODEL), lambda i: (half, 0, 0)),
                  row(SSM_CH), full(SSM_CH, SSM_CH), row(SSM_CH), row(SSM_CH),
                  full(CONV_CH, D_MODEL), full(SSM_CH, D_MODEL), row(D_MODEL), row(D_MODEL),
                  full(N_EXPERTS, D_MODEL), full(N_EXPERTS, 1), full(tm, tm)],
        out_specs=[tok(D_MODEL), tok(D_MODEL // 2), col, col, col,
                   pl.BlockSpec((N_EXPERTS, 128), lambda i: (0, 0))],
        out_shape=[jax.ShapeDtypeStruct((HALF_TOK, D_MODEL), F32),
                   jax.ShapeDtypeStruct((HALF_TOK, D_MODEL // 2), jnp.uint32),
                   jax.ShapeDtypeStruct((TOP_K, HALF_TOK), jnp.int32),
                   jax.ShapeDtypeStruct((TOP_K, HALF_TOK), jnp.int32),
                   jax.ShapeDtypeStruct((TOP_K, HALF_TOK), F32),
                   jax.ShapeDtypeStruct((N_EXPERTS, 128), F32)],
        scratch_shapes=[pltpu.VMEM((N_EXPERTS, 128), F32),
                        pltpu.VMEM((MIX_SUBTILES, SSM_CH // LANES, tm // MIX_SUBTILES, LANES), F32)],
        compiler_params=_cparams(("arbitrary",)),
        name="mix_out_router",
    )(a_n, yt, u2, x2, mod, d_skip, wglu_bf, b_glu, g_out_ssm, wo_a, wo_b, g_post_mix,
      g_pre_ffn, w_router_t, b_router_col, tri)


def _expert_kernel(blk0_ref, nblk_ref, xs_hbm, wg_ref, wu_ref, wd_ref, ys_hbm,
                   xbuf, ybuf, wgu_s, wd_s, sem_in, sem_out):
    e = pl.program_id(0)
    n = nblk_ref[e]
    b0 = blk0_ref[e]
    n_all = blk0_ref[N_EXPERTS - 1] + nblk_ref[N_EXPERTS - 1]

    def rows(b):
        return pl.ds(pl.multiple_of(b * ROW_BLOCK, ROW_BLOCK), ROW_BLOCK)

    def in_copy(b, slot):
        return pltpu.make_async_copy(xs_hbm.at[rows(b)], xbuf.at[slot], sem_in.at[slot])

    def out_copy(b, slot):
        return pltpu.make_async_copy(ybuf.at[slot], ys_hbm.at[rows(b)], sem_out.at[slot])

    for b in range(EXPERT_AHEAD):
        @pl.when((e == 0) & (b < n_all))
        def _():
            in_copy(b, b).start()

    wgu_s[:, :D_EXPERT] = wg_ref[0].astype(BF16)
    wgu_s[:, D_EXPERT:] = wu_ref[0].astype(BF16)
    wd_s[...] = wd_ref[0].astype(BF16)

    def admit(b):
        in_copy(b, b % EXPERT_SLOTS).wait()
        ahead = b + EXPERT_AHEAD

        @pl.when(ahead < n_all)
        def _():
            in_copy(ahead, ahead % EXPERT_SLOTS).start()

        @pl.when(b >= EXPERT_SLOTS)
        def _():
            out_copy(b - EXPERT_SLOTS, b % EXPERT_SLOTS).wait()

    def compute(b):
        slot = b % EXPERT_SLOTS
        x_lo, x_hi = _unpack_halves(xbuf[slot])
        x = jnp.concatenate([x_lo.astype(BF16), x_hi.astype(BF16)], axis=1)
        h = jnp.dot(x, wgu_s[...], preferred_element_type=F32)
        hg = h[:, :D_EXPERT]
        act = hg * jax.nn.sigmoid(hg) * h[:, D_EXPERT:]
        ybuf[slot] = _pack_halves(jnp.dot(act.astype(BF16), wd_s[...], preferred_element_type=F32))

    def run(blocks):
        for b in blocks:
            admit(b)
        for b in blocks:
            compute(b)
        for b in blocks:
            out_copy(b, b % EXPERT_SLOTS).start(priority=1)

    def pair(i, carry):
        b = b0 + 2 * i
        run([b, b + 1])
        return carry

    lax.fori_loop(0, n // 2, pair, 0)

    @pl.when(n % 2 == 1)
    def _():
        run([b0 + n - 1])

    @pl.when(e == N_EXPERTS - 1)
    def _():
        for j in range(1, EXPERT_SLOTS + 1):
            @pl.when(n_all >= j)
            def _():
                out_copy(n_all - j, (n_all - j) % EXPERT_SLOTS).wait()


def _experts(blk0, nblk, xs, we_gate, we_up, we_down):
    any_spec = pl.BlockSpec(memory_space=pl.ANY)
    grid_spec = pltpu.PrefetchScalarGridSpec(
        num_scalar_prefetch=2,
        grid=(N_EXPERTS,),
        in_specs=[any_spec,
                  pl.BlockSpec((1, D_MODEL, D_EXPERT), lambda e, b0, nb: (e, 0, 0)),
                  pl.BlockSpec((1, D_MODEL, D_EXPERT), lambda e, b0, nb: (e, 0, 0)),
                  pl.BlockSpec((1, D_EXPERT, D_MODEL), lambda e, b0, nb: (e, 0, 0))],
        out_specs=any_spec,
        scratch_shapes=[pltpu.VMEM((EXPERT_SLOTS, ROW_BLOCK, D_MODEL // 2), jnp.uint32),
                        pltpu.VMEM((EXPERT_SLOTS, ROW_BLOCK, D_MODEL // 2), jnp.uint32),
                        pltpu.VMEM((D_MODEL, 2 * D_EXPERT), BF16),
                        pltpu.VMEM((D_EXPERT, D_MODEL), BF16),
                        pltpu.SemaphoreType.DMA((EXPERT_SLOTS,)),
                        pltpu.SemaphoreType.DMA((EXPERT_SLOTS,))],
    )
    return pl.pallas_call(
        _expert_kernel,
        grid_spec=grid_spec,
        out_shape=jax.ShapeDtypeStruct((N_ROWS, D_MODEL // 2), jnp.uint32),
        compiler_params=_cparams(("arbitrary",)),
        name="routed_experts",
    )(blk0, nblk, xs, we_gate, we_up, we_down)


def _final_kernel(h2_ref, yg_ref, gw_ref, x1_ref, mod_ref, wgu_ref, wd_ref, g_ref, *rest):
    o_ref = rest[-1]
    half = D_MODEL // 2
    gt_f = mod_ref[0, 5:6, :]
    x_lo, x_hi = _unpack_halves(h2_ref[...])
    h = (jnp.dot(x_lo.astype(BF16), wgu_ref[:half, :], preferred_element_type=F32)
         + jnp.dot(x_hi.astype(BF16), wgu_ref[half:, :], preferred_element_type=F32))
    hg = h[:, :D_EXPERT]
    act = hg * jax.nn.sigmoid(hg) * h[:, D_EXPERT:]
    shared = jnp.dot(act.astype(BF16), wd_ref[...], preferred_element_type=F32)
    y_lo = shared[:, :half]
    y_hi = shared[:, half:]
    for k in range(TOP_K):
        r_lo, r_hi = _unpack_halves(yg_ref[k])
        w = gw_ref[:, k:k + 1]
        y_lo = y_lo + w * r_lo
        y_hi = y_hi + w * r_hi
    ms = (jnp.sum(y_lo * y_lo, axis=-1, keepdims=True)
          + jnp.sum(y_hi * y_hi, axis=-1, keepdims=True)) * (1.0 / D_MODEL)
    inv = lax.rsqrt(ms + NORM_EPS)
    o_ref[:, :half] = x1_ref[:, :half] + gt_f[:, :half] * (y_lo * inv * g_ref[:, :half])
    o_ref[:, half:] = x1_ref[:, half:] + gt_f[:, half:] * (y_hi * inv * g_ref[:, half:])


def _final(half, out_prev, h2p, yg, gw_t, x1, mod, ws_gu, ws_d, g_post_ffn):
    tm = TM_OUT
    t0 = half * (HALF_TOK // tm)
    tok = pl.BlockSpec((tm, D_MODEL), lambda i: (i, 0))
    in_specs = [pl.BlockSpec((tm, D_MODEL // 2), lambda i: (i, 0)),
                pl.BlockSpec((TOP_K, tm, D_MODEL // 2), lambda i: (0, i, 0)),
                pl.BlockSpec((tm, TOP_K), lambda i: (i, 0)),
                tok,
                pl.BlockSpec((1, 8, D_MODEL), lambda i: (half, 0, 0)),
                pl.BlockSpec((D_MODEL, 2 * D_EXPERT), lambda i: (0, 0)),
                pl.BlockSpec((D_EXPERT, D_MODEL), lambda i: (0, 0)),
                pl.BlockSpec((1, D_MODEL), lambda i: (0, 0))]
    args = [h2p, yg, gw_t, x1, mod, ws_gu, ws_d, g_post_ffn]
    aliases = {}
    if out_prev is not None:
        aliases = {len(args): 0}
        in_specs.append(pl.BlockSpec(memory_space=pl.ANY))
        args.append(out_prev)
    return pl.pallas_call(
        _final_kernel,
        grid=(HALF_TOK // tm,),
        in_specs=in_specs,
        out_specs=pl.BlockSpec((tm, D_MODEL), lambda i: (t0 + i, 0)),
        out_shape=jax.ShapeDtypeStruct((N_TOK, D_MODEL), F32),
        input_output_aliases=aliases,
        compiler_params=_cparams(("parallel",)),
        name="shared_final",
    )(*args)


def _sc_worker_id():
    return lax.axis_index("s") * SC_CORES + lax.axis_index("c")


def _dispatch_body(h_hbm, dest_hbm, xs_hbm, idx_v, rows_v, sem_l, sem_s):
    n = SC_CHUNKS_PER_WORKER
    c0 = _sc_worker_id() * n

    def load(i, b):
        return pltpu.async_copy(h_hbm.at[pl.ds((c0 + i) * SC_W, SC_W)], rows_v.at[b], sem_l.at[b])

    loads = [None] * n
    scat = [None] * n
    loads[0] = load(0, 0)
    for i in range(n):
        b = i % 2
        pltpu.sync_copy(dest_hbm.at[c0 + i], idx_v.at[b])
        loads[i].wait()
        if i + 1 < n:
            if i >= 1:
                for d in scat[i - 1]:
                    d.wait()
            loads[i + 1] = load(i + 1, 1 - b)
        scat[i] = [pltpu.async_copy(rows_v.at[b], xs_hbm.at[idx_v.at[b].at[k]], sem_s.at[b])
                   for k in range(TOP_K)]
    for i in (n - 2, n - 1):
        for d in scat[i]:
            d.wait()


def _sc_dispatch(h2p, dest3):
    mesh = plsc.VectorSubcoreMesh(core_axis_name="c", subcore_axis_name="s")
    return pl.kernel(
        _dispatch_body, mesh=mesh,
        out_type=jax.ShapeDtypeStruct((N_ROWS, D_MODEL // 2), jnp.uint32),
        scratch_types=[pltpu.VMEM((2, TOP_K, SC_W), jnp.int32),
                       pltpu.VMEM((2, SC_W, D_MODEL // 2), jnp.uint32),
                       pltpu.SemaphoreType.DMA((2,)), pltpu.SemaphoreType.DMA((2,))],
    )(h2p, dest3)


def _combine_body(ys_hbm, dest_hbm, yg_hbm, idx_v, rows_v, sem_g, sem_w):
    c0 = _sc_worker_id() * SC_CHUNKS_PER_WORKER

    @pl.loop(0, SC_CHUNKS_PER_WORKER)
    def _(i):
        c = c0 + i
        pltpu.sync_copy(dest_hbm.at[c], idx_v)
        g = [None] * TOP_K
        w = [None] * TOP_K
        g[0] = pltpu.async_copy(ys_hbm.at[idx_v.at[0]], rows_v.at[0], sem_g.at[0])
        for k in range(TOP_K):
            b = k % 2
            g[k].wait()
            if k + 1 < TOP_K:
                if k >= 1:
                    w[k - 1].wait()
                g[k + 1] = pltpu.async_copy(ys_hbm.at[idx_v.at[k + 1]], rows_v.at[1 - b], sem_g.at[1 - b])
            w[k] = pltpu.async_copy(rows_v.at[b], yg_hbm.at[k].at[pl.ds(c * SC_W, SC_W)], sem_w.at[b])
        w[TOP_K - 2].wait()
        w[TOP_K - 1].wait()


def _sc_combine(ysp, dest3):
    mesh = plsc.VectorSubcoreMesh(core_axis_name="c", subcore_axis_name="s")
    return pl.kernel(
        _combine_body, mesh=mesh,
        out_type=jax.ShapeDtypeStruct((TOP_K, HALF_TOK, D_MODEL // 2), jnp.uint32),
        scratch_types=[pltpu.VMEM((TOP_K, SC_W), jnp.int32),
                       pltpu.VMEM((2, SC_W, D_MODEL // 2), jnp.uint32),
                       pltpu.SemaphoreType.DMA((2,)), pltpu.SemaphoreType.DMA((2,))],
    )(ysp, dest3)


def kernel(x, c, w_ada, b_ada, g_pre_mix, g_post_mix, w_in, conv_w, conv_b, conv_ln_g, conv_ln_b,
           ssm_a_re, ssm_a_im, ssm_log_dt, ssm_b_re, ssm_b_im, ssm_c_re, ssm_c_im, ssm_d,
           ssm_w_glu, ssm_b_glu, g_out_conv, g_out_ssm, w_out, g_pre_ffn, g_post_ffn,
           w_router, b_router, we_gate, we_up, we_down, ws_gate, ws_up, ws_down):
    l = 0
    x2 = x.reshape(N_TOK, D_MODEL)
    r1 = lambda a: a.reshape(1, -1)

    c_pad = jnp.zeros((8, D_MODEL), F32).at[:BATCH].set(c)
    mod = _ada(c_pad, w_ada[l], r1(b_ada[l]))[:BATCH].reshape(BATCH, 6, D_MODEL)
    mod = jnp.concatenate([mod, jnp.zeros((BATCH, 2, D_MODEL), F32)], axis=1)

    v, u, ut = _inproj(x2, mod, r1(g_pre_mix[l]), w_in[l].astype(BF16))
    cw = jnp.concatenate([conv_w[l].reshape(CONV_WIDTH, CONV_CH), jnp.zeros((1, CONV_CH), F32)], axis=0)
    a_n = _conv(v.reshape(BATCH, SEQ, CONV_CH), cw, r1(conv_b[l]), r1(conv_ln_g[l]),
                r1(conv_ln_b[l]), r1(g_out_conv[l])).reshape(N_TOK, CONV_CH)

    pwr, pwi, s5_params, a_cat, b_q, b_s = _s5_operators(
        ssm_a_re[l], ssm_a_im[l], ssm_log_dt[l], ssm_b_re[l], ssm_b_im[l], ssm_c_re[l], ssm_c_im[l])
    yt = _s5(ut, pwr, pwi, s5_params, a_cat, b_q, b_s)

    tm = TM_MIX
    tri = (jnp.arange(tm)[:, None] < jnp.arange(tm)[None, :]).astype(BF16)
    wo = w_out[l].astype(BF16)
    mix_params = (r1(ssm_d[l]), ssm_w_glu[l].astype(BF16), r1(ssm_b_glu[l]), r1(g_out_ssm[l]),
                  wo[:CONV_CH], wo[CONV_CH:], r1(g_post_mix[l]), r1(g_pre_ffn[l]),
                  w_router[l].T, b_router[l].reshape(N_EXPERTS, 1), tri)
    ws_gu = jnp.concatenate([ws_gate[l], ws_up[l]], axis=1).astype(BF16)
    ws_d = ws_down[l].astype(BF16)
    e_ids = jnp.arange(N_EXPERTS, dtype=jnp.int32)

    out = None
    for half in range(N_HALVES):
        x1, h2, eidx, rank, gw, cnt = _mix(half, a_n, yt, u, x2, mod, *mix_params)
        counts = cnt[:, 0].astype(jnp.int32)
        padded = (counts + ROW_BLOCK - 1) // ROW_BLOCK * ROW_BLOCK
        pstart = jnp.cumsum(padded) - padded
        dest = rank + jnp.sum(jnp.where(eidx[..., None] == e_ids, pstart, 0), axis=-1)
        dest3 = dest.reshape(TOP_K, HALF_TOK // SC_W, SC_W).transpose(1, 0, 2)

        xs = _sc_dispatch(h2, dest3)
        ys = _experts(pstart // ROW_BLOCK, padded // ROW_BLOCK, xs, we_gate[l], we_up[l], we_down[l])
        yg = _sc_combine(ys, dest3)
        out = _final(half, out, h2, yg, gw.T, x1, mod, ws_gu, ws_d, r1(g_post_ffn[l]))
    return out.reshape(BATCH, SEQ, D_MODEL)
```

```python
import functools
import math

import jax
import jax.numpy as jnp
from jax import lax
from jax.experimental import pallas as pl
from jax.experimental.pallas import tpu as pltpu
from jax.experimental.pallas import tpu_sc as plsc

F32 = jnp.float32
BF16 = jnp.bfloat16

D_MODEL = 1024
BATCH = 2
SEQ = 8192
N_TOK = BATCH * SEQ
CONV_CH = 512
CONV_WIDTH = 31
SSM_CH = 512
SSM_GROUP_CH = 16
SSM_GROUPS = 32
SSM_STATE = 64
D_IN = 2 * CONV_CH + SSM_CH
N_EXPERTS = 64
TOP_K = 8
N_ROUTE_GROUPS = 8
TOPK_ROUTE_GROUPS = 4
D_EXPERT = 256
ROUTED_SCALE = 2.5
NORM_EPS = 1e-6

SUBLANES = 8
LANES = 128

TM_IN = 512
IN_SUBTILES = 2
TL_CONV = 512
CONV_HALO = 32
CONV_ROWS = 64
EXPERTS_PER_CONV_STEP = N_EXPERTS * TL_CONV // N_TOK
assert EXPERTS_PER_CONV_STEP * N_TOK == N_EXPERTS * TL_CONV
S5_Q = 32
S5_QH = S5_Q * SSM_GROUP_CH
S5_CHUNKS = N_TOK // S5_Q
S5_CHUNKS_PER_SEQ = SEQ // S5_Q
TM_MIX = 512
MIX_SUBTILES = 1
ROW_BLOCK = 512
EXPERT_AHEAD = 4
EXPERT_SLOTS = 6
HALF_TOK = SEQ
N_HALVES = N_TOK // HALF_TOK
N_BLOCKS = HALF_TOK * TOP_K // ROW_BLOCK + N_EXPERTS
N_ROWS = N_BLOCKS * ROW_BLOCK
TM_OUT = 512
SC_CORES = 2
SC_SUBCORES = 16
SC_WORKERS = SC_CORES * SC_SUBCORES
SC_W = 64
SC_CHUNKS_PER_WORKER = HALF_TOK // (SC_WORKERS * SC_W)
VMEM_LIMIT = 48 * 1024 * 1024


def _cparams(sem):
    return pltpu.CompilerParams(dimension_semantics=sem, vmem_limit_bytes=VMEM_LIMIT)


def _pack_halves(x):
    n = x.shape[-1] // 2
    lo = lax.bitcast_convert_type(x[:, :n].astype(BF16).astype(F32), jnp.uint32)
    hi = lax.bitcast_convert_type(x[:, n:].astype(BF16).astype(F32), jnp.uint32)
    return hi | (lo >> 16)


def _unpack_halves(p):
    lo = lax.bitcast_convert_type(p << 16, F32)
    hi = lax.bitcast_convert_type(p & jnp.uint32(0xFFFF0000), F32)
    return lo, hi


def _rms(x, g):
    return x * lax.rsqrt(jnp.mean(x * x, axis=-1, keepdims=True) + NORM_EPS) * g


def _ada_kernel(c_ref, w_ref, b_ref, o_ref):
    c = c_ref[...]
    a = c * jax.nn.sigmoid(c)
    o_ref[...] = jnp.dot(a, w_ref[...], preferred_element_type=F32,
                         precision=lax.Precision.HIGHEST) + b_ref[...]


def _ada(c_pad, w_ada, b_ada):
    n = w_ada.shape[1]
    bn = 1536
    return pl.pallas_call(
        _ada_kernel,
        grid=(n // bn,),
        in_specs=[pl.BlockSpec((8, D_MODEL), lambda j: (0, 0)),
                  pl.BlockSpec((D_MODEL, bn), lambda j: (0, j)),
                  pl.BlockSpec((1, bn), lambda j: (0, j))],
        out_specs=pl.BlockSpec((8, bn), lambda j: (0, j)),
        out_shape=jax.ShapeDtypeStruct((8, n), F32),
        compiler_params=_cparams(("arbitrary",)),
        name="ada_mod",
    )(c_pad, w_ada, b_ada)


GROUPS_PER_LANE_TILE = LANES // SSM_GROUP_CH


def _to_group_chunks(u, tile_ref, ut_ref):
    n_chunks = u.shape[0] // S5_Q
    for j in range(SSM_CH // LANES):
        tile_ref[j] = u[:, LANES * j:LANES * (j + 1)]
    for j in range(SSM_CH // LANES):
        rows_t = [tile_ref[j, pl.ds(t, n_chunks, stride=S5_Q), :] for t in range(S5_Q)]
        for gg in range(GROUPS_PER_LANE_TILE):
            lo = gg * SSM_GROUP_CH
            row = jnp.concatenate([r[:, lo:lo + SSM_GROUP_CH] for r in rows_t], axis=1)
            ut_ref[j * GROUPS_PER_LANE_TILE + gg] = row.astype(ut_ref.dtype)


def _from_group_chunks(yt_ref, tile_ref):
    n_chunks = yt_ref.shape[1]
    for j in range(SSM_CH // LANES):
        for t in range(S5_Q):
            lo = t * SSM_GROUP_CH
            piece = jnp.concatenate(
                [yt_ref[j * GROUPS_PER_LANE_TILE + gg, :, lo:lo + SSM_GROUP_CH]
                 for gg in range(GROUPS_PER_LANE_TILE)], axis=1)
            tile_ref[j, pl.ds(t, n_chunks, stride=S5_Q), :] = piece
    return jnp.concatenate([tile_ref[j] for j in range(SSM_CH // LANES)], axis=1)


def _inproj_kernel(x_ref, mod_ref, g_ref, w_ref, v_ref, u_ref, ut_ref, tile_ref):
    sh = mod_ref[0, 0:1, :]
    sc = mod_ref[0, 1:2, :]
    sub = TM_IN // IN_SUBTILES
    sub_chunks = sub // S5_Q
    for s in range(IN_SUBTILES):
        r = slice(s * sub, (s + 1) * sub)
        h = _rms(x_ref[r, :], g_ref[...]) * (1.0 + sc) + sh
        z = jnp.dot(h.astype(BF16), w_ref[...], preferred_element_type=F32)
        v_ref[r, :] = z[:, :CONV_CH] * jax.nn.sigmoid(z[:, CONV_CH:2 * CONV_CH])
        u = z[:, 2 * CONV_CH:]
        u_ref[r, :] = u
        _to_group_chunks(u, tile_ref.at[s], ut_ref.at[:, s * sub_chunks:(s + 1) * sub_chunks, :])


def _inproj(x2, mod, g_pre, w_in_bf):
    tiles_per_seq = SEQ // TM_IN
    return pl.pallas_call(
        _inproj_kernel,
        grid=(N_TOK // TM_IN,),
        in_specs=[pl.BlockSpec((TM_IN, D_MODEL), lambda i: (i, 0)),
                  pl.BlockSpec((1, 8, D_MODEL), lambda i: (i // tiles_per_seq, 0, 0)),
                  pl.BlockSpec((1, D_MODEL), lambda i: (0, 0)),
                  pl.BlockSpec((D_MODEL, D_IN), lambda i: (0, 0))],
        out_specs=[pl.BlockSpec((TM_IN, CONV_CH), lambda i: (i, 0)),
                   pl.BlockSpec((TM_IN, SSM_CH), lambda i: (i, 0)),
                   pl.BlockSpec((SSM_GROUPS, TM_IN // S5_Q, S5_QH), lambda i: (0, i, 0))],
        out_shape=[jax.ShapeDtypeStruct((N_TOK, CONV_CH), F32),
                   jax.ShapeDtypeStruct((N_TOK, SSM_CH), F32),
                   jax.ShapeDtypeStruct((SSM_GROUPS, S5_CHUNKS, S5_QH), BF16)],
        scratch_shapes=[pltpu.VMEM((IN_SUBTILES, SSM_CH // LANES, TM_IN // IN_SUBTILES, LANES), F32)],
        compiler_params=_cparams(("parallel",)),
        name="in_proj",
    )(x2, mod, g_pre, w_in_bf)


def _conv_kernel(vc_ref, vp_ref, w_ref, cb_ref, lg_ref, lb_ref, go_ref, wg_ref, wu_ref, wd_ref,
                 o_ref, wgu_o, wd_o, sh_ref):
    for q in range(EXPERTS_PER_CONV_STEP):
        wgu_o[q, :, :D_EXPERT] = wg_ref[q].astype(BF16)
        wgu_o[q, :, D_EXPERT:] = wu_ref[q].astype(BF16)
        wd_o[q] = wd_ref[q].astype(BF16)

    i = pl.program_id(1)
    keep = (i > 0).astype(F32)
    n_ext = TL_CONV + CONV_HALO
    sh_ref[0, 0:CONV_HALO, :] = vp_ref[0] * keep
    sh_ref[0, CONV_HALO:, :] = vc_ref[0]
    for s in range(1, SUBLANES):
        sh_ref[s, 0:n_ext - s, :] = sh_ref[0, s:n_ext, :]
    off = CONV_HALO - (CONV_WIDTH - 1)
    for r in range(TL_CONV // CONV_ROWS):
        acc = None
        for j in range(CONV_WIDTH):
            s = (off + j) % SUBLANES
            al = r * CONV_ROWS + (off + j) - s
            term = w_ref[j:j + 1, :] * sh_ref[s, al:al + CONV_ROWS, :]
            acc = term if acc is None else acc + term
        y = acc + cb_ref[...]
        mu = jnp.mean(y, axis=-1, keepdims=True)
        d = y - mu
        var = jnp.mean(d * d, axis=-1, keepdims=True)
        yn = d * lax.rsqrt(var + NORM_EPS) * lg_ref[...] + lb_ref[...]
        a = yn * jax.nn.sigmoid(yn)
        o_ref[0, r * CONV_ROWS:(r + 1) * CONV_ROWS, :] = _rms(a, go_ref[...]).astype(BF16)


def _conv(v3, conv_w, conv_b, ln_g, ln_b, g_out, we_gate, we_up, we_down):
    halo_per_tile = TL_CONV // CONV_HALO
    steps_per_seq = SEQ // TL_CONV
    vec = pl.BlockSpec((1, CONV_CH), lambda b, i: (0, 0))
    ex = EXPERTS_PER_CONV_STEP
    w_in = pl.BlockSpec((ex, D_MODEL, D_EXPERT), lambda b, i: (b * steps_per_seq + i, 0, 0))
    return pl.pallas_call(
        _conv_kernel,
        grid=(BATCH, steps_per_seq),
        in_specs=[pl.BlockSpec((1, TL_CONV, CONV_CH), lambda b, i: (b, i, 0)),
                  pl.BlockSpec((1, CONV_HALO, CONV_CH),
                               lambda b, i: (b, jnp.maximum(i * halo_per_tile - 1, 0), 0)),
                  pl.BlockSpec((CONV_WIDTH + 1, CONV_CH), lambda b, i: (0, 0)),
                  vec, vec, vec, vec,
                  w_in, w_in,
                  pl.BlockSpec((ex, D_EXPERT, D_MODEL), lambda b, i: (b * steps_per_seq + i, 0, 0))],
        out_specs=[pl.BlockSpec((1, TL_CONV, CONV_CH), lambda b, i: (b, i, 0)),
                   pl.BlockSpec((ex, D_MODEL, 2 * D_EXPERT), lambda b, i: (b * steps_per_seq + i, 0, 0)),
                   pl.BlockSpec((ex, D_EXPERT, D_MODEL), lambda b, i: (b * steps_per_seq + i, 0, 0))],
        out_shape=[jax.ShapeDtypeStruct((BATCH, SEQ, CONV_CH), BF16),
                   jax.ShapeDtypeStruct((N_EXPERTS, D_MODEL, 2 * D_EXPERT), BF16),
                   jax.ShapeDtypeStruct((N_EXPERTS, D_EXPERT, D_MODEL), BF16)],
        scratch_shapes=[pltpu.VMEM((SUBLANES, TL_CONV + CONV_HALO, CONV_CH), F32)],
        compiler_params=_cparams(("parallel", "arbitrary")),
        name="conv_module",
    )(v3, v3, conv_w, conv_b, ln_g, ln_b, g_out, we_gate, we_up, we_down)


S5_GROUP_ROWS = S5_CHUNKS + 8


S5_POW_ROWS = (S5_Q + 1 + SUBLANES - 1) // SUBLANES * SUBLANES
(S5_BB_RI, S5_BB_NIR, S5_BB_IR, S5_BB_RNI, S5_CC_RI, S5_CC_NIR, S5_N_PARAM) = range(7)


def _s5_kernel(ut_ref, pwr_ref, pwi_ref, par_ref, a_ref, bq_ref, bs_ref, yt_ref, sin_s, sp_s):
    phase = pl.program_id(0)
    g = pl.program_id(1)
    q = S5_Q
    n = 2 * SSM_STATE
    row0 = pl.multiple_of(g * S5_GROUP_ROWS, 8)

    def lam_pow(j):
        return pwr_ref[0, j:j + 1, :], pwi_ref[0, j:j + 1, :]

    @pl.when(phase == 0)
    def _():
        bb_ri, bb_nir = par_ref[0, S5_BB_RI], par_ref[0, S5_BB_NIR]
        bb_ir, bb_rni = par_ref[0, S5_BB_IR], par_ref[0, S5_BB_RNI]
        blk_q, blk_s = [], []
        for t in range(q):
            pr, pi_ = lam_pow(q - 1 - t)
            blk_q.append(pr * bb_ri + pi_ * bb_nir)
            blk_s.append(pr * bb_ir + pi_ * bb_rni)
        wst = jnp.concatenate([jnp.concatenate(blk_q, axis=0), jnp.concatenate(blk_s, axis=0)], axis=1)
        r = jnp.dot(ut_ref[0], wst.astype(BF16), preferred_element_type=F32)
        sin_s[0, pl.ds(row0, S5_CHUNKS), :] = r[:, :n]
        sin_s[1, pl.ds(row0, S5_CHUNKS), :] = r[:, n:]

    @pl.when((phase == 1) & (g == 0))
    def _():
        a = a_ref[...]
        bq = bq_ref[...]
        bs = bs_ref[...]

        def body(c, carry):
            nxt = []
            for b in range(BATCH):
                x, xs = carry[b]
                rows = pl.ds(b * S5_CHUNKS_PER_SEQ + c, SSM_GROUPS, stride=S5_GROUP_ROWS)
                sp_s[rows, :] = x
                nxt.append((a * x + bq * xs + sin_s[0, rows, :], a * xs + bs * x + sin_s[1, rows, :]))
            return tuple(nxt)

        z = jnp.zeros((SSM_GROUPS, n), F32)
        lax.fori_loop(0, S5_CHUNKS_PER_SEQ, body, tuple((z, z) for _ in range(BATCH)))

    @pl.when(phase == 1)
    def _():
        cc_ri, cc_nir = par_ref[0, S5_CC_RI], par_ref[0, S5_CC_NIR]
        cl = []
        for j in range(q + 1):
            pr, pi_ = lam_pow(j)
            cl.append(pr * cc_ri + pi_ * cc_nir)
        cl_lo = jnp.concatenate(cl[:q], axis=0)
        cl_hi = jnp.concatenate(cl[1:], axis=0)
        lane = lax.broadcasted_iota(jnp.int32, (1, n), 1)
        vgt = (cl_hi * jnp.where(lane < SSM_STATE, 1.0, -1.0)).astype(BF16)
        kt = lax.dot_general(par_ref[0, S5_BB_RNI], cl_lo, (((1,), (1,)), ((), ())),
                             preferred_element_type=F32, precision=lax.Precision.HIGHEST)
        padded = jnp.concatenate([jnp.zeros_like(kt), kt], axis=1)
        tg = jnp.concatenate(
            [padded[:, (q - t) * SSM_GROUP_CH:(q - t) * SSM_GROUP_CH + S5_QH] for t in range(q)],
            axis=0).astype(BF16)
        sp = sp_s[pl.ds(row0, S5_CHUNKS), :]
        y = jnp.dot(ut_ref[0], tg, preferred_element_type=F32)
        yt_ref[0] = y + lax.dot_general(sp.astype(BF16), vgt, (((1,), (1,)), ((), ())),
                                        preferred_element_type=F32)


def _s5(ut, pwr, pwi, params, a_cat, b_q, b_s):
    vec = pl.BlockSpec((SSM_GROUPS, 2 * SSM_STATE), lambda p, g: (0, 0))
    powers = pl.BlockSpec((1, S5_POW_ROWS, 2 * SSM_STATE), lambda p, g: (g, 0, 0))
    return pl.pallas_call(
        _s5_kernel,
        grid=(2, SSM_GROUPS),
        in_specs=[pl.BlockSpec((1, S5_CHUNKS, S5_QH), lambda p, g: (g, 0, 0)),
                  powers, powers,
                  pl.BlockSpec((1, S5_N_PARAM, SSM_GROUP_CH, 2 * SSM_STATE), lambda p, g: (g, 0, 0, 0)),
                  vec, vec, vec],
        out_specs=pl.BlockSpec((1, S5_CHUNKS, S5_QH), lambda p, g: (g * p, 0, 0)),
        out_shape=jax.ShapeDtypeStruct((SSM_GROUPS, S5_CHUNKS, S5_QH), F32),
        scratch_shapes=[pltpu.VMEM((2, SSM_GROUPS * S5_GROUP_ROWS, 2 * SSM_STATE), F32),
                        pltpu.VMEM((SSM_GROUPS * S5_GROUP_ROWS, 2 * SSM_STATE), F32)],
        compiler_params=_cparams(("arbitrary", "arbitrary")),
        name="s5_chunked",
    )(ut, pwr, pwi, params, a_cat, b_q, b_s)


def _s5_operators(a_re, a_im, log_dt, b_re, b_im, c_re, c_im):
    q = S5_Q
    dt = jnp.exp(log_dt)[:, None]
    ar, ai = a_re, a_im
    mag = jnp.exp(ar * dt)
    lr = mag * jnp.cos(ai * dt)
    li = mag * jnp.sin(ai * dt)
    den = ar * ar + ai * ai
    nr = lr - 1.0
    kr = (nr * ar + li * ai) / den
    ki = (li * ar - nr * ai) / den
    bbr = kr[..., None] * b_re - ki[..., None] * b_im
    bbi = kr[..., None] * b_im + ki[..., None] * b_re
    j = jnp.arange(q + 1, dtype=F32)[None, :, None]
    pmag = jnp.exp(ar[:, None, :] * dt[:, :, None] * j)
    pang = ai[:, None, :] * dt[:, :, None] * j
    pr = pmag * jnp.cos(pang)
    pi_ = pmag * jnp.sin(pang)
    pad = ((0, 0), (0, S5_POW_ROWS - (q + 1)), (0, 0))
    pwr = jnp.pad(jnp.concatenate([pr, pr], axis=-1), pad)
    pwi = jnp.pad(jnp.concatenate([pi_, pi_], axis=-1), pad)
    br_t = bbr.transpose(0, 2, 1)
    bi_t = bbi.transpose(0, 2, 1)
    cat = lambda a, b: jnp.concatenate([a, b], axis=-1)
    stack = [None] * S5_N_PARAM
    stack[S5_BB_RI] = cat(br_t, bi_t)
    stack[S5_BB_NIR] = cat(-bi_t, br_t)
    stack[S5_BB_IR] = cat(bi_t, br_t)
    stack[S5_BB_RNI] = cat(br_t, -bi_t)
    stack[S5_CC_RI] = cat(c_re, c_im)
    stack[S5_CC_NIR] = cat(-c_im, c_re)
    params = jnp.stack(stack, axis=1)
    aq_r, aq_i = pr[:, q], pi_[:, q]
    a_cat = cat(aq_r, aq_r)
    b_q = cat(-aq_i, aq_i)
    b_s = cat(aq_i, -aq_i)
    return pwr, pwi, params, a_cat, b_q, b_s


def _gelu_tanh(x):
    return 0.5 * x * (1.0 + jnp.tanh(math.sqrt(2.0 / math.pi) * (x + 0.044715 * (x * x * x))))


def _mix_kernel(an_ref, yt_ref, u_ref, x_ref, mod_ref, d_ref, wglu_ref, bglu_ref, gos_ref,
                woa_ref, wob_ref, gpm_ref, gpf_ref, wr_ref, br_ref, tri_ref,
                x1_ref, h2_ref, eidx_ref, rank_ref, gw_ref, cnt_ref, run_ref, tile_ref):
    i = pl.program_id(0)
    tm = TM_MIX

    @pl.when(i == 0)
    def _():
        run_ref[...] = jnp.zeros_like(run_ref)

    gt_m = mod_ref[0, 2:3, :]
    sh_f = mod_ref[0, 3:4, :]
    sc_f = mod_ref[0, 4:5, :]
    ng = N_ROUTE_GROUPS
    gsz = N_EXPERTS // ng
    neg = -jnp.inf

    ts = tm // MIX_SUBTILES
    run = run_ref[:, 0:1]
    for s in range(MIX_SUBTILES):
        r = slice(s * ts, (s + 1) * ts)
        chunks = slice(s * (ts // S5_Q), (s + 1) * (ts // S5_Q))
        yy = _from_group_chunks(yt_ref.at[:, chunks, :], tile_ref.at[s]) + d_ref[...] * u_ref[r, :]
        g = _gelu_tanh(yy)
        gl = jnp.dot(g.astype(BF16), wglu_ref[...], preferred_element_type=F32) + bglu_ref[...]
        ob = g * jax.nn.sigmoid(gl)
        bn = _rms(ob, gos_ref[...]).astype(BF16)
        o = (jnp.dot(an_ref[r, :], woa_ref[...], preferred_element_type=F32)
             + jnp.dot(bn, wob_ref[...], preferred_element_type=F32))
        x1 = x_ref[r, :] + gt_m * _rms(o, gpm_ref[...])
        x1_ref[r, :] = x1
        h2 = _rms(x1, gpf_ref[...]) * (1.0 + sc_f) + sh_f
        h2_ref[r, :] = _pack_halves(h2)

        logits = lax.dot_general(wr_ref[...], h2, (((1,), (1,)), ((), ())),
                                 preferred_element_type=F32, precision=lax.Precision.HIGHEST)
        scores = jax.nn.sigmoid(logits)
        biased = scores + br_ref[...]
        b3 = biased.reshape(ng, gsz, ts)
        s3 = scores.reshape(ng, gsz, ts)
        sub = lax.broadcasted_iota(jnp.int32, (ng, gsz, ts), 1).astype(F32)
        grp = lax.broadcasted_iota(jnp.int32, (ng, gsz, ts), 0).astype(F32)
        eid = grp * gsz + sub
        m1 = jnp.max(b3, axis=1, keepdims=True)
        i1 = jnp.min(jnp.where(b3 == m1, sub, float(gsz)), axis=1, keepdims=True)
        m2 = jnp.max(jnp.where(sub == i1, neg, b3), axis=1, keepdims=True)
        gs = m1 + m2
        gi = lax.broadcasted_iota(jnp.int32, (ng, 1, ts), 0)
        beaten = jnp.zeros((ng, 1, ts), F32)
        for gp in range(ng):
            o_ = gs[gp:gp + 1]
            beats = (o_ > gs) | ((o_ == gs) & (gi > gp))
            beaten = beaten + beats.astype(F32)
        gmask = beaten < float(TOPK_ROUTE_GROUPS)
        masked = jnp.where(gmask, b3, neg)

        sels = []
        picked = jnp.zeros((ng, gsz, ts), F32)
        for k in range(TOP_K):
            m = jnp.max(jnp.max(masked, axis=0, keepdims=True), axis=1, keepdims=True)
            cand = jnp.where(masked == m, eid, float(N_EXPERTS))
            sel = jnp.min(jnp.min(cand, axis=0, keepdims=True), axis=1, keepdims=True)
            oh = eid == sel
            masked = jnp.where(oh, neg, masked)
            picked = jnp.where(oh, 1.0, picked)
            sels.append(sel)

        pm = picked.reshape(N_EXPERTS, ts)
        prefix = jnp.dot(pm.astype(BF16), tri_ref[:ts, :ts], preferred_element_type=F32) + run
        p3 = prefix.reshape(ng, gsz, ts)
        run = run + jnp.sum(pm, axis=1, keepdims=True)

        sc_rows = []
        for k in range(TOP_K):
            oh = eid == sels[k]
            sc_k = jnp.sum(jnp.sum(jnp.where(oh, s3, 0.0), axis=0, keepdims=True), axis=1, keepdims=True)
            rk_k = jnp.sum(jnp.sum(jnp.where(oh, p3, 0.0), axis=0, keepdims=True), axis=1, keepdims=True)
            sc_rows.append(sc_k)
            eidx_ref[k:k + 1, r] = sels[k].reshape(1, ts).astype(jnp.int32)
            rank_ref[k:k + 1, r] = rk_k.reshape(1, ts).astype(jnp.int32)
        tot = sc_rows[0]
        for k in range(1, TOP_K):
            tot = tot + sc_rows[k]
        inv = ROUTED_SCALE / (tot + 1e-20)
        for k in range(TOP_K):
            gw_ref[k:k + 1, r] = (sc_rows[k] * inv).reshape(1, ts)

    run_new = jnp.broadcast_to(run, run_ref.shape)
    run_ref[...] = run_new
    cnt_ref[...] = run_new


def _mix(half, a_n, yt, u2, x2, mod, d_skip, wglu_bf, b_glu, g_out_ssm, wo_a, wo_b, g_post_mix,
         g_pre_ffn, w_router_t, b_router_col, tri):
    tm = TM_MIX
    t0 = half * (HALF_TOK // tm)
    row = lambda n: pl.BlockSpec((1, n), lambda i: (0, 0))
    full = lambda a, b: pl.BlockSpec((a, b), lambda i: (0, 0))
    tok_in = lambda n: pl.BlockSpec((tm, n), lambda i: (t0 + i, 0))
    tok = lambda n: pl.BlockSpec((tm, n), lambda i: (i, 0))
    col = pl.BlockSpec((TOP_K, tm), lambda i: (0, i))
    return pl.pallas_call(
        _mix_kernel,
        grid=(HALF_TOK // tm,),
        in_specs=[tok_in(CONV_CH),
                  pl.BlockSpec((SSM_GROUPS, tm // S5_Q, S5_QH), lambda i: (0, t0 + i, 0)),
                  tok_in(SSM_CH), tok_in(D_MODEL),
                  pl.BlockSpec((1, 8, D_MODEL), lambda i: (half, 0, 0)),
                  row(SSM_CH), full(SSM_CH, SSM_CH), row(SSM_CH), row(SSM_CH),
                  full(CONV_CH, D_MODEL), full(SSM_CH, D_MODEL), row(D_MODEL), row(D_MODEL),
                  full(N_EXPERTS, D_MODEL), full(N_EXPERTS, 1), full(tm, tm)],
        out_specs=[tok(D_MODEL), tok(D_MODEL // 2), col, col, col,
                   pl.BlockSpec((N_EXPERTS, 128), lambda i: (0, 0))],
        out_shape=[jax.ShapeDtypeStruct((HALF_TOK, D_MODEL), F32),
                   jax.ShapeDtypeStruct((HALF_TOK, D_MODEL // 2), jnp.uint32),
                   jax.ShapeDtypeStruct((TOP_K, HALF_TOK), jnp.int32),
                   jax.ShapeDtypeStruct((TOP_K, HALF_TOK), jnp.int32),
                   jax.ShapeDtypeStruct((TOP_K, HALF_TOK), F32),
                   jax.ShapeDtypeStruct((N_EXPERTS, 128), F32)],
        scratch_shapes=[pltpu.VMEM((N_EXPERTS, 128), F32),
                        pltpu.VMEM((MIX_SUBTILES, SSM_CH // LANES, tm // MIX_SUBTILES, LANES), F32)],
        compiler_params=_cparams(("arbitrary",)),
        name="mix_out_router",
    )(a_n, yt, u2, x2, mod, d_skip, wglu_bf, b_glu, g_out_ssm, wo_a, wo_b, g_post_mix,
      g_pre_ffn, w_router_t, b_router_col, tri)


def _expert_kernel(blk0_ref, nblk_ref, xs_hbm, wgu_ref, wd_ref, ys_hbm, xbuf, ybuf, sem_in, sem_out):
    e = pl.program_id(0)
    n = nblk_ref[e]
    b0 = blk0_ref[e]
    n_all = blk0_ref[N_EXPERTS - 1] + nblk_ref[N_EXPERTS - 1]

    def rows(b):
        return pl.ds(pl.multiple_of(b * ROW_BLOCK, ROW_BLOCK), ROW_BLOCK)

    def in_copy(b, slot):
        return pltpu.make_async_copy(xs_hbm.at[rows(b)], xbuf.at[slot], sem_in.at[slot])

    def out_copy(b, slot):
        return pltpu.make_async_copy(ybuf.at[slot], ys_hbm.at[rows(b)], sem_out.at[slot])

    for b in range(EXPERT_AHEAD):
        @pl.when((e == 0) & (b < n_all))
        def _():
            in_copy(b, b).start()

    def admit(b):
        in_copy(b, b % EXPERT_SLOTS).wait()
        ahead = b + EXPERT_AHEAD

        @pl.when(ahead < n_all)
        def _():
            in_copy(ahead, ahead % EXPERT_SLOTS).start()

        @pl.when(b >= EXPERT_SLOTS)
        def _():
            out_copy(b - EXPERT_SLOTS, b % EXPERT_SLOTS).wait()

    def compute(b):
        slot = b % EXPERT_SLOTS
        x_lo, x_hi = _unpack_halves(xbuf[slot])
        x = jnp.concatenate([x_lo.astype(BF16), x_hi.astype(BF16)], axis=1)
        h = jnp.dot(x, wgu_ref[0], preferred_element_type=F32)
        hg = h[:, :D_EXPERT]
        act = hg * jax.nn.sigmoid(hg) * h[:, D_EXPERT:]
        ybuf[slot] = _pack_halves(jnp.dot(act.astype(BF16), wd_ref[0], preferred_element_type=F32))

    def run(blocks):
        for b in blocks:
            admit(b)
        for b in blocks:
            compute(b)
        for b in blocks:
            out_copy(b, b % EXPERT_SLOTS).start(priority=1)

    def pair(i, carry):
        b = b0 + 2 * i
        run([b, b + 1])
        return carry

    lax.fori_loop(0, n // 2, pair, 0)

    @pl.when(n % 2 == 1)
    def _():
        run([b0 + n - 1])

    @pl.when(e == N_EXPERTS - 1)
    def _():
        for j in range(1, EXPERT_SLOTS + 1):
            @pl.when(n_all >= j)
            def _():
                out_copy(n_all - j, (n_all - j) % EXPERT_SLOTS).wait()


def _experts(blk0, nblk, xs, we_gu, we_d):
    any_spec = pl.BlockSpec(memory_space=pl.ANY)
    grid_spec = pltpu.PrefetchScalarGridSpec(
        num_scalar_prefetch=2,
        grid=(N_EXPERTS,),
        in_specs=[any_spec,
                  pl.BlockSpec((1, D_MODEL, 2 * D_EXPERT), lambda e, b0, nb: (e, 0, 0)),
                  pl.BlockSpec((1, D_EXPERT, D_MODEL), lambda e, b0, nb: (e, 0, 0))],
        out_specs=any_spec,
        scratch_shapes=[pltpu.VMEM((EXPERT_SLOTS, ROW_BLOCK, D_MODEL // 2), jnp.uint32),
                        pltpu.VMEM((EXPERT_SLOTS, ROW_BLOCK, D_MODEL // 2), jnp.uint32),
                        pltpu.SemaphoreType.DMA((EXPERT_SLOTS,)),
                        pltpu.SemaphoreType.DMA((EXPERT_SLOTS,))],
    )
    return pl.pallas_call(
        _expert_kernel,
        grid_spec=grid_spec,
        out_shape=jax.ShapeDtypeStruct((N_ROWS, D_MODEL // 2), jnp.uint32),
        compiler_params=_cparams(("arbitrary",)),
        name="routed_experts",
    )(blk0, nblk, xs, we_gu, we_d)


def _final_kernel(h2_ref, yg_ref, gw_ref, x1_ref, mod_ref, wgu_ref, wd_ref, g_ref, *rest):
    o_ref = rest[-1]
    half = D_MODEL // 2
    gt_f = mod_ref[0, 5:6, :]
    x_lo, x_hi = _unpack_halves(h2_ref[...])
    h = (jnp.dot(x_lo.astype(BF16), wgu_ref[:half, :], preferred_element_type=F32)
         + jnp.dot(x_hi.astype(BF16), wgu_ref[half:, :], preferred_element_type=F32))
    hg = h[:, :D_EXPERT]
    act = hg * jax.nn.sigmoid(hg) * h[:, D_EXPERT:]
    shared = jnp.dot(act.astype(BF16), wd_ref[...], preferred_element_type=F32)
    y_lo = shared[:, :half]
    y_hi = shared[:, half:]
    for k in range(TOP_K):
        r_lo, r_hi = _unpack_halves(yg_ref[k])
        w = gw_ref[:, k:k + 1]
        y_lo = y_lo + w * r_lo
        y_hi = y_hi + w * r_hi
    ms = (jnp.sum(y_lo * y_lo, axis=-1, keepdims=True)
          + jnp.sum(y_hi * y_hi, axis=-1, keepdims=True)) * (1.0 / D_MODEL)
    inv = lax.rsqrt(ms + NORM_EPS)
    o_ref[:, :half] = x1_ref[:, :half] + gt_f[:, :half] * (y_lo * inv * g_ref[:, :half])
    o_ref[:, half:] = x1_ref[:, half:] + gt_f[:, half:] * (y_hi * inv * g_ref[:, half:])


def _final(half, out_prev, h2p, yg, gw_t, x1, mod, ws_gu, ws_d, g_post_ffn):
    tm = TM_OUT
    t0 = half * (HALF_TOK // tm)
    tok = pl.BlockSpec((tm, D_MODEL), lambda i: (i, 0))
    in_specs = [pl.BlockSpec((tm, D_MODEL // 2), lambda i: (i, 0)),
                pl.BlockSpec((TOP_K, tm, D_MODEL // 2), lambda i: (0, i, 0)),
                pl.BlockSpec((tm, TOP_K), lambda i: (i, 0)),
                tok,
                pl.BlockSpec((1, 8, D_MODEL), lambda i: (half, 0, 0)),
                pl.BlockSpec((D_MODEL, 2 * D_EXPERT), lambda i: (0, 0)),
                pl.BlockSpec((D_EXPERT, D_MODEL), lambda i: (0, 0)),
                pl.BlockSpec((1, D_MODEL), lambda i: (0, 0))]
    args = [h2p, yg, gw_t, x1, mod, ws_gu, ws_d, g_post_ffn]
    aliases = {}
    if out_prev is not None:
        aliases = {len(args): 0}
        in_specs.append(pl.BlockSpec(memory_space=pl.ANY))
        args.append(out_prev)
    return pl.pallas_call(
        _final_kernel,
        grid=(HALF_TOK // tm,),
        in_specs=in_specs,
        out_specs=pl.BlockSpec((tm, D_MODEL), lambda i: (t0 + i, 0)),
        out_shape=jax.ShapeDtypeStruct((N_TOK, D_MODEL), F32),
        input_output_aliases=aliases,
        compiler_params=_cparams(("parallel",)),
        name="shared_final",
    )(*args)


def _sc_worker_id():
    return lax.axis_index("s") * SC_CORES + lax.axis_index("c")


def _dispatch_body(h_hbm, dest_hbm, xs_hbm, idx_v, rows_v, sem_l, sem_s):
    n = SC_CHUNKS_PER_WORKER
    c0 = _sc_worker_id() * n

    def load(i, b):
        return pltpu.async_copy(h_hbm.at[pl.ds((c0 + i) * SC_W, SC_W)], rows_v.at[b], sem_l.at[b])

    loads = [None] * n
    scat = [None] * n
    loads[0] = load(0, 0)
    for i in range(n):
        b = i % 2
        pltpu.sync_copy(dest_hbm.at[c0 + i], idx_v.at[b])
        loads[i].wait()
        if i + 1 < n:
            if i >= 1:
                for d in scat[i - 1]:
                    d.wait()
            loads[i + 1] = load(i + 1, 1 - b)
        scat[i] = [pltpu.async_copy(rows_v.at[b], xs_hbm.at[idx_v.at[b].at[k]], sem_s.at[b])
                   for k in range(TOP_K)]
    for i in (n - 2, n - 1):
        for d in scat[i]:
            d.wait()


def _sc_dispatch(h2p, dest3):
    mesh = plsc.VectorSubcoreMesh(core_axis_name="c", subcore_axis_name="s")
    return pl.kernel(
        _dispatch_body, mesh=mesh,
        out_type=jax.ShapeDtypeStruct((N_ROWS, D_MODEL // 2), jnp.uint32),
        scratch_types=[pltpu.VMEM((2, TOP_K, SC_W), jnp.int32),
                       pltpu.VMEM((2, SC_W, D_MODEL // 2), jnp.uint32),
                       pltpu.SemaphoreType.DMA((2,)), pltpu.SemaphoreType.DMA((2,))],
    )(h2p, dest3)


def _combine_body(ys_hbm, dest_hbm, yg_hbm, idx_v, rows_v, sem_g, sem_w):
    c0 = _sc_worker_id() * SC_CHUNKS_PER_WORKER

    @pl.loop(0, SC_CHUNKS_PER_WORKER)
    def _(i):
        c = c0 + i
        pltpu.sync_copy(dest_hbm.at[c], idx_v)
        g = [None] * TOP_K
        w = [None] * TOP_K
        g[0] = pltpu.async_copy(ys_hbm.at[idx_v.at[0]], rows_v.at[0], sem_g.at[0])
        for k in range(TOP_K):
            b = k % 2
            g[k].wait()
            if k + 1 < TOP_K:
                if k >= 1:
                    w[k - 1].wait()
                g[k + 1] = pltpu.async_copy(ys_hbm.at[idx_v.at[k + 1]], rows_v.at[1 - b], sem_g.at[1 - b])
            w[k] = pltpu.async_copy(rows_v.at[b], yg_hbm.at[k].at[pl.ds(c * SC_W, SC_W)], sem_w.at[b])
        w[TOP_K - 2].wait()
        w[TOP_K - 1].wait()


def _sc_combine(ysp, dest3):
    mesh = plsc.VectorSubcoreMesh(core_axis_name="c", subcore_axis_name="s")
    return pl.kernel(
        _combine_body, mesh=mesh,
        out_type=jax.ShapeDtypeStruct((TOP_K, HALF_TOK, D_MODEL // 2), jnp.uint32),
        scratch_types=[pltpu.VMEM((TOP_K, SC_W), jnp.int32),
                       pltpu.VMEM((2, SC_W, D_MODEL // 2), jnp.uint32),
                       pltpu.SemaphoreType.DMA((2,)), pltpu.SemaphoreType.DMA((2,))],
    )(ysp, dest3)


def kernel(x, c, w_ada, b_ada, g_pre_mix, g_post_mix, w_in, conv_w, conv_b, conv_ln_g, conv_ln_b,
           ssm_a_re, ssm_a_im, ssm_log_dt, ssm_b_re, ssm_b_im, ssm_c_re, ssm_c_im, ssm_d,
           ssm_w_glu, ssm_b_glu, g_out_conv, g_out_ssm, w_out, g_pre_ffn, g_post_ffn,
           w_router, b_router, we_gate, we_up, we_down, ws_gate, ws_up, ws_down):
    l = 0
    x2 = x.reshape(N_TOK, D_MODEL)
    r1 = lambda a: a.reshape(1, -1)

    c_pad = jnp.zeros((8, D_MODEL), F32).at[:BATCH].set(c)
    mod = _ada(c_pad, w_ada[l], r1(b_ada[l]))[:BATCH].reshape(BATCH, 6, D_MODEL)
    mod = jnp.concatenate([mod, jnp.zeros((BATCH, 2, D_MODEL), F32)], axis=1)

    v, u, ut = _inproj(x2, mod, r1(g_pre_mix[l]), w_in[l].astype(BF16))
    cw = jnp.concatenate([conv_w[l].reshape(CONV_WIDTH, CONV_CH), jnp.zeros((1, CONV_CH), F32)], axis=0)
    a_n, we_gu, we_d = _conv(v.reshape(BATCH, SEQ, CONV_CH), cw, r1(conv_b[l]), r1(conv_ln_g[l]),
                             r1(conv_ln_b[l]), r1(g_out_conv[l]), we_gate[l], we_up[l], we_down[l])
    a_n = a_n.reshape(N_TOK, CONV_CH)

    pwr, pwi, s5_params, a_cat, b_q, b_s = _s5_operators(
        ssm_a_re[l], ssm_a_im[l], ssm_log_dt[l], ssm_b_re[l], ssm_b_im[l], ssm_c_re[l], ssm_c_im[l])
    yt = _s5(ut, pwr, pwi, s5_params, a_cat, b_q, b_s)

    tm = TM_MIX
    tri = (jnp.arange(tm)[:, None] < jnp.arange(tm)[None, :]).astype(BF16)
    wo = w_out[l].astype(BF16)
    mix_params = (r1(ssm_d[l]), ssm_w_glu[l].astype(BF16), r1(ssm_b_glu[l]), r1(g_out_ssm[l]),
                  wo[:CONV_CH], wo[CONV_CH:], r1(g_post_mix[l]), r1(g_pre_ffn[l]),
                  w_router[l].T, b_router[l].reshape(N_EXPERTS, 1), tri)
    ws_gu = jnp.concatenate([ws_gate[l], ws_up[l]], axis=1).astype(BF16)
    ws_d = ws_down[l].astype(BF16)
    e_ids = jnp.arange(N_EXPERTS, dtype=jnp.int32)

    out = None
    for half in range(N_HALVES):
        x1, h2, eidx, rank, gw, cnt = _mix(half, a_n, yt, u, x2, mod, *mix_params)
        counts = cnt[:, 0].astype(jnp.int32)
        padded = (counts + ROW_BLOCK - 1) // ROW_BLOCK * ROW_BLOCK
        pstart = jnp.cumsum(padded) - padded
        dest = rank + jnp.sum(jnp.where(eidx[..., None] == e_ids, pstart, 0), axis=-1)
        dest3 = dest.reshape(TOP_K, HALF_TOK // SC_W, SC_W).transpose(1, 0, 2)

        xs = _sc_dispatch(h2, dest3)
        ys = _experts(pstart // ROW_BLOCK, padded // ROW_BLOCK, xs, we_gu, we_d)
        yg = _sc_combine(ys, dest3)
        out = _final(half, out, h2, yg, gw.T, x1, mod, ws_gu, ws_d, r1(g_post_ffn[l]))
    return out.reshape(BATCH, SEQ, D_MODEL)
```

```python
import functools
import math

import jax
import jax.numpy as jnp
from jax import lax
from jax.experimental import pallas as pl
from jax.experimental.pallas import tpu as pltpu
from jax.experimental.pallas import tpu_sc as plsc

F32 = jnp.float32
BF16 = jnp.bfloat16

D_MODEL = 1024
BATCH = 2
SEQ = 8192
N_TOK = BATCH * SEQ
CONV_CH = 512
CONV_WIDTH = 31
SSM_CH = 512
SSM_GROUP_CH = 16
SSM_GROUPS = 32
SSM_STATE = 64
D_IN = 2 * CONV_CH + SSM_CH
N_EXPERTS = 64
TOP_K = 8
N_ROUTE_GROUPS = 8
TOPK_ROUTE_GROUPS = 4
D_EXPERT = 256
ROUTED_SCALE = 2.5
NORM_EPS = 1e-6

SUBLANES = 8
LANES = 128

TM_IN = 512
IN_SUBTILES = 2
TL_CONV = 512
CONV_HALO = 32
CONV_ROWS = 64
S5_Q = 32
S5_QH = S5_Q * SSM_GROUP_CH
S5_CHUNKS = N_TOK // S5_Q
S5_CHUNKS_PER_SEQ = SEQ // S5_Q
TM_MIX = 512
MIX_SUBTILES = 1
ROW_BLOCK = 512
EXPERT_AHEAD = 4
EXPERT_SLOTS = 6
HALF_TOK = SEQ
N_HALVES = N_TOK // HALF_TOK
N_BLOCKS = HALF_TOK * TOP_K // ROW_BLOCK + N_EXPERTS
N_ROWS = N_BLOCKS * ROW_BLOCK
TM_OUT = 512
SC_CORES = 2
SC_SUBCORES = 16
SC_WORKERS = SC_CORES * SC_SUBCORES
SC_W = 64
SC_CHUNKS_PER_WORKER = HALF_TOK // (SC_WORKERS * SC_W)
VMEM_LIMIT = 48 * 1024 * 1024


def _cparams(sem):
    return pltpu.CompilerParams(dimension_semantics=sem, vmem_limit_bytes=VMEM_LIMIT)


def _pack_halves(x):
    n = x.shape[-1] // 2
    lo = lax.bitcast_convert_type(x[:, :n].astype(BF16).astype(F32), jnp.uint32)
    hi = lax.bitcast_convert_type(x[:, n:].astype(BF16).astype(F32), jnp.uint32)
    return hi | (lo >> 16)


def _unpack_halves(p):
    lo = lax.bitcast_convert_type(p << 16, F32)
    hi = lax.bitcast_convert_type(p & jnp.uint32(0xFFFF0000), F32)
    return lo, hi


def _rms(x, g):
    return x * lax.rsqrt(jnp.mean(x * x, axis=-1, keepdims=True) + NORM_EPS) * g


def _ada_kernel(c_ref, w_ref, b_ref, o_ref):
    c = c_ref[...]
    a = c * jax.nn.sigmoid(c)
    o_ref[...] = jnp.dot(a, w_ref[...], preferred_element_type=F32,
                         precision=lax.Precision.HIGHEST) + b_ref[...]


def _ada(c_pad, w_ada, b_ada):
    n = w_ada.shape[1]
    bn = 1536
    return pl.pallas_call(
        _ada_kernel,
        grid=(n // bn,),
        in_specs=[pl.BlockSpec((8, D_MODEL), lambda j: (0, 0)),
                  pl.BlockSpec((D_MODEL, bn), lambda j: (0, j)),
                  pl.BlockSpec((1, bn), lambda j: (0, j))],
        out_specs=pl.BlockSpec((8, bn), lambda j: (0, j)),
        out_shape=jax.ShapeDtypeStruct((8, n), F32),
        compiler_params=_cparams(("arbitrary",)),
        name="ada_mod",
    )(c_pad, w_ada, b_ada)


GROUPS_PER_LANE_TILE = LANES // SSM_GROUP_CH


def _to_group_chunks(u, tile_ref, ut_ref):
    n_chunks = u.shape[0] // S5_Q
    for j in range(SSM_CH // LANES):
        tile_ref[j] = u[:, LANES * j:LANES * (j + 1)]
    for j in range(SSM_CH // LANES):
        rows_t = [tile_ref[j, pl.ds(t, n_chunks, stride=S5_Q), :] for t in range(S5_Q)]
        for gg in range(GROUPS_PER_LANE_TILE):
            lo = gg * SSM_GROUP_CH
            row = jnp.concatenate([r[:, lo:lo + SSM_GROUP_CH] for r in rows_t], axis=1)
            ut_ref[j * GROUPS_PER_LANE_TILE + gg] = row.astype(ut_ref.dtype)


def _from_group_chunks(yt_ref, tile_ref):
    n_chunks = yt_ref.shape[1]
    for j in range(SSM_CH // LANES):
        for t in range(S5_Q):
            lo = t * SSM_GROUP_CH
            piece = jnp.concatenate(
                [yt_ref[j * GROUPS_PER_LANE_TILE + gg, :, lo:lo + SSM_GROUP_CH]
                 for gg in range(GROUPS_PER_LANE_TILE)], axis=1)
            tile_ref[j, pl.ds(t, n_chunks, stride=S5_Q), :] = piece
    return jnp.concatenate([tile_ref[j] for j in range(SSM_CH // LANES)], axis=1)


def _inproj_kernel(x_ref, mod_ref, g_ref, w_ref, v_ref, u_ref, ut_ref, tile_ref):
    sh = mod_ref[0, 0:1, :]
    sc = mod_ref[0, 1:2, :]
    sub = TM_IN // IN_SUBTILES
    sub_chunks = sub // S5_Q
    for s in range(IN_SUBTILES):
        r = slice(s * sub, (s + 1) * sub)
        h = _rms(x_ref[r, :], g_ref[...]) * (1.0 + sc) + sh
        z = jnp.dot(h.astype(BF16), w_ref[...], preferred_element_type=F32)
        v_ref[r, :] = z[:, :CONV_CH] * jax.nn.sigmoid(z[:, CONV_CH:2 * CONV_CH])
        u = z[:, 2 * CONV_CH:]
        u_ref[r, :] = u
        _to_group_chunks(u, tile_ref.at[s], ut_ref.at[:, s * sub_chunks:(s + 1) * sub_chunks, :])


def _inproj(x2, mod, g_pre, w_in_bf):
    tiles_per_seq = SEQ // TM_IN
    return pl.pallas_call(
        _inproj_kernel,
        grid=(N_TOK // TM_IN,),
        in_specs=[pl.BlockSpec((TM_IN, D_MODEL), lambda i: (i, 0)),
                  pl.BlockSpec((1, 8, D_MODEL), lambda i: (i // tiles_per_seq, 0, 0)),
                  pl.BlockSpec((1, D_MODEL), lambda i: (0, 0)),
                  pl.BlockSpec((D_MODEL, D_IN), lambda i: (0, 0))],
        out_specs=[pl.BlockSpec((TM_IN, CONV_CH), lambda i: (i, 0)),
                   pl.BlockSpec((TM_IN, SSM_CH), lambda i: (i, 0)),
                   pl.BlockSpec((SSM_GROUPS, TM_IN // S5_Q, S5_QH), lambda i: (0, i, 0))],
        out_shape=[jax.ShapeDtypeStruct((N_TOK, CONV_CH), F32),
                   jax.ShapeDtypeStruct((N_TOK, SSM_CH), F32),
                   jax.ShapeDtypeStruct((SSM_GROUPS, S5_CHUNKS, S5_QH), BF16)],
        scratch_shapes=[pltpu.VMEM((IN_SUBTILES, SSM_CH // LANES, TM_IN // IN_SUBTILES, LANES), F32)],
        compiler_params=_cparams(("parallel",)),
        name="in_proj",
    )(x2, mod, g_pre, w_in_bf)


def _conv_kernel(vc_ref, vp_ref, w_ref, cb_ref, lg_ref, lb_ref, go_ref, o_ref, sh_ref):
    i = pl.program_id(1)
    keep = (i > 0).astype(F32)
    n_ext = TL_CONV + CONV_HALO
    sh_ref[0, 0:CONV_HALO, :] = vp_ref[0] * keep
    sh_ref[0, CONV_HALO:, :] = vc_ref[0]
    for s in range(1, SUBLANES):
        sh_ref[s, 0:n_ext - s, :] = sh_ref[0, s:n_ext, :]
    off = CONV_HALO - (CONV_WIDTH - 1)
    for r in range(TL_CONV // CONV_ROWS):
        acc = None
        for j in range(CONV_WIDTH):
            s = (off + j) % SUBLANES
            al = r * CONV_ROWS + (off + j) - s
            term = w_ref[j:j + 1, :] * sh_ref[s, al:al + CONV_ROWS, :]
            acc = term if acc is None else acc + term
        y = acc + cb_ref[...]
        mu = jnp.mean(y, axis=-1, keepdims=True)
        d = y - mu
        var = jnp.mean(d * d, axis=-1, keepdims=True)
        yn = d * lax.rsqrt(var + NORM_EPS) * lg_ref[...] + lb_ref[...]
        a = yn * jax.nn.sigmoid(yn)
        o_ref[0, r * CONV_ROWS:(r + 1) * CONV_ROWS, :] = _rms(a, go_ref[...]).astype(BF16)


def _conv(v3, conv_w, conv_b, ln_g, ln_b, g_out):
    halo_per_tile = TL_CONV // CONV_HALO
    vec = pl.BlockSpec((1, CONV_CH), lambda b, i: (0, 0))
    return pl.pallas_call(
        _conv_kernel,
        grid=(BATCH, SEQ // TL_CONV),
        in_specs=[pl.BlockSpec((1, TL_CONV, CONV_CH), lambda b, i: (b, i, 0)),
                  pl.BlockSpec((1, CONV_HALO, CONV_CH),
                               lambda b, i: (b, jnp.maximum(i * halo_per_tile - 1, 0), 0)),
                  pl.BlockSpec((CONV_WIDTH + 1, CONV_CH), lambda b, i: (0, 0)),
                  vec, vec, vec, vec],
        out_specs=pl.BlockSpec((1, TL_CONV, CONV_CH), lambda b, i: (b, i, 0)),
        out_shape=jax.ShapeDtypeStruct((BATCH, SEQ, CONV_CH), BF16),
        scratch_shapes=[pltpu.VMEM((SUBLANES, TL_CONV + CONV_HALO, CONV_CH), F32)],
        compiler_params=_cparams(("parallel", "arbitrary")),
        name="conv_module",
    )(v3, v3, conv_w, conv_b, ln_g, ln_b, g_out)


S5_GROUP_ROWS = S5_CHUNKS + 8


S5_POW_ROWS = (S5_Q + 1 + SUBLANES - 1) // SUBLANES * SUBLANES
(S5_BB_RI, S5_BB_NIR, S5_BB_IR, S5_BB_RNI, S5_CC_RI, S5_CC_NIR, S5_N_PARAM) = range(7)


def _s5_kernel(ut_ref, pwr_ref, pwi_ref, par_ref, a_ref, bq_ref, bs_ref, wg_ref, wu_ref, wd_ref,
               yt_ref, wgu_o, wd_o, sin_s, sp_s):
    phase = pl.program_id(0)
    g = pl.program_id(1)
    q = S5_Q
    n = 2 * SSM_STATE
    row0 = pl.multiple_of(g * S5_GROUP_ROWS, 8)

    wgu_o[0, :, :D_EXPERT] = wg_ref[0].astype(BF16)
    wgu_o[0, :, D_EXPERT:] = wu_ref[0].astype(BF16)
    wd_o[0] = wd_ref[0].astype(BF16)

    def lam_pow(j):
        return pwr_ref[0, j:j + 1, :], pwi_ref[0, j:j + 1, :]

    @pl.when(phase == 0)
    def _():
        bb_ri, bb_nir = par_ref[0, S5_BB_RI], par_ref[0, S5_BB_NIR]
        bb_ir, bb_rni = par_ref[0, S5_BB_IR], par_ref[0, S5_BB_RNI]
        blk_q, blk_s = [], []
        for t in range(q):
            pr, pi_ = lam_pow(q - 1 - t)
            blk_q.append(pr * bb_ri + pi_ * bb_nir)
            blk_s.append(pr * bb_ir + pi_ * bb_rni)
        wst = jnp.concatenate([jnp.concatenate(blk_q, axis=0), jnp.concatenate(blk_s, axis=0)], axis=1)
        r = jnp.dot(ut_ref[0], wst.astype(BF16), preferred_element_type=F32)
        sin_s[0, pl.ds(row0, S5_CHUNKS), :] = r[:, :n]
        sin_s[1, pl.ds(row0, S5_CHUNKS), :] = r[:, n:]

    @pl.when((phase == 1) & (g == 0))
    def _():
        a = a_ref[...]
        bq = bq_ref[...]
        bs = bs_ref[...]

        def body(c, carry):
            nxt = []
            for b in range(BATCH):
                x, xs = carry[b]
                rows = pl.ds(b * S5_CHUNKS_PER_SEQ + c, SSM_GROUPS, stride=S5_GROUP_ROWS)
                sp_s[rows, :] = x
                nxt.append((a * x + bq * xs + sin_s[0, rows, :], a * xs + bs * x + sin_s[1, rows, :]))
            return tuple(nxt)

        z = jnp.zeros((SSM_GROUPS, n), F32)
        lax.fori_loop(0, S5_CHUNKS_PER_SEQ, body, tuple((z, z) for _ in range(BATCH)))

    @pl.when(phase == 1)
    def _():
        cc_ri, cc_nir = par_ref[0, S5_CC_RI], par_ref[0, S5_CC_NIR]
        cl = []
        for j in range(q + 1):
            pr, pi_ = lam_pow(j)
            cl.append(pr * cc_ri + pi_ * cc_nir)
        cl_lo = jnp.concatenate(cl[:q], axis=0)
        cl_hi = jnp.concatenate(cl[1:], axis=0)
        lane = lax.broadcasted_iota(jnp.int32, (1, n), 1)
        vgt = (cl_hi * jnp.where(lane < SSM_STATE, 1.0, -1.0)).astype(BF16)
        kt = lax.dot_general(par_ref[0, S5_BB_RNI], cl_lo, (((1,), (1,)), ((), ())),
                             preferred_element_type=F32, precision=lax.Precision.HIGHEST)
        padded = jnp.concatenate([jnp.zeros_like(kt), kt], axis=1)
        tg = jnp.concatenate(
            [padded[:, (q - t) * SSM_GROUP_CH:(q - t) * SSM_GROUP_CH + S5_QH] for t in range(q)],
            axis=0).astype(BF16)
        sp = sp_s[pl.ds(row0, S5_CHUNKS), :]
        y = jnp.dot(ut_ref[0], tg, preferred_element_type=F32)
        yt_ref[0] = y + lax.dot_general(sp.astype(BF16), vgt, (((1,), (1,)), ((), ())),
                                        preferred_element_type=F32)


def _s5(ut, pwr, pwi, params, a_cat, b_q, b_s, we_gate, we_up, we_down):
    assert 2 * SSM_GROUPS == N_EXPERTS
    vec = pl.BlockSpec((SSM_GROUPS, 2 * SSM_STATE), lambda p, g: (0, 0))
    powers = pl.BlockSpec((1, S5_POW_ROWS, 2 * SSM_STATE), lambda p, g: (g, 0, 0))
    expert = lambda a, b: pl.BlockSpec((1, a, b), lambda p, g: (p * SSM_GROUPS + g, 0, 0))
    return pl.pallas_call(
        _s5_kernel,
        grid=(2, SSM_GROUPS),
        in_specs=[pl.BlockSpec((1, S5_CHUNKS, S5_QH), lambda p, g: (g, 0, 0)),
                  powers, powers,
                  pl.BlockSpec((1, S5_N_PARAM, SSM_GROUP_CH, 2 * SSM_STATE), lambda p, g: (g, 0, 0, 0)),
                  vec, vec, vec,
                  expert(D_MODEL, D_EXPERT), expert(D_MODEL, D_EXPERT), expert(D_EXPERT, D_MODEL)],
        out_specs=[pl.BlockSpec((1, S5_CHUNKS, S5_QH), lambda p, g: (g * p, 0, 0)),
                   expert(D_MODEL, 2 * D_EXPERT), expert(D_EXPERT, D_MODEL)],
        out_shape=[jax.ShapeDtypeStruct((SSM_GROUPS, S5_CHUNKS, S5_QH), F32),
                   jax.ShapeDtypeStruct((N_EXPERTS, D_MODEL, 2 * D_EXPERT), BF16),
                   jax.ShapeDtypeStruct((N_EXPERTS, D_EXPERT, D_MODEL), BF16)],
        scratch_shapes=[pltpu.VMEM((2, SSM_GROUPS * S5_GROUP_ROWS, 2 * SSM_STATE), F32),
                        pltpu.VMEM((SSM_GROUPS * S5_GROUP_ROWS, 2 * SSM_STATE), F32)],
        compiler_params=_cparams(("arbitrary", "arbitrary")),
        name="s5_chunked",
    )(ut, pwr, pwi, params, a_cat, b_q, b_s, we_gate, we_up, we_down)


def _s5_operators(a_re, a_im, log_dt, b_re, b_im, c_re, c_im):
    q = S5_Q
    dt = jnp.exp(log_dt)[:, None]
    ar, ai = a_re, a_im
    mag = jnp.exp(ar * dt)
    lr = mag * jnp.cos(ai * dt)
    li = mag * jnp.sin(ai * dt)
    den = ar * ar + ai * ai
    nr = lr - 1.0
    kr = (nr * ar + li * ai) / den
    ki = (li * ar - nr * ai) / den
    bbr = kr[..., None] * b_re - ki[..., None] * b_im
    bbi = kr[..., None] * b_im + ki[..., None] * b_re
    j = jnp.arange(q + 1, dtype=F32)[None, :, None]
    pmag = jnp.exp(ar[:, None, :] * dt[:, :, None] * j)
    pang = ai[:, None, :] * dt[:, :, None] * j
    pr = pmag * jnp.cos(pang)
    pi_ = pmag * jnp.sin(pang)
    pad = ((0, 0), (0, S5_POW_ROWS - (q + 1)), (0, 0))
    pwr = jnp.pad(jnp.concatenate([pr, pr], axis=-1), pad)
    pwi = jnp.pad(jnp.concatenate([pi_, pi_], axis=-1), pad)
    br_t = bbr.transpose(0, 2, 1)
    bi_t = bbi.transpose(0, 2, 1)
    cat = lambda a, b: jnp.concatenate([a, b], axis=-1)
    stack = [None] * S5_N_PARAM
    stack[S5_BB_RI] = cat(br_t, bi_t)
    stack[S5_BB_NIR] = cat(-bi_t, br_t)
    stack[S5_BB_IR] = cat(bi_t, br_t)
    stack[S5_BB_RNI] = cat(br_t, -bi_t)
    stack[S5_CC_RI] = cat(c_re, c_im)
    stack[S5_CC_NIR] = cat(-c_im, c_re)
    params = jnp.stack(stack, axis=1)
    aq_r, aq_i = pr[:, q], pi_[:, q]
    a_cat = cat(aq_r, aq_r)
    b_q = cat(-aq_i, aq_i)
    b_s = cat(aq_i, -aq_i)
    return pwr, pwi, params, a_cat, b_q, b_s


def _gelu_tanh(x):
    return 0.5 * x * (1.0 + jnp.tanh(math.sqrt(2.0 / math.pi) * (x + 0.044715 * (x * x * x))))


def _mix_kernel(an_ref, yt_ref, u_ref, x_ref, mod_ref, d_ref, wglu_ref, bglu_ref, gos_ref,
                woa_ref, wob_ref, gpm_ref, gpf_ref, wr_ref, br_ref, tri_ref,
                x1_ref, h2_ref, eidx_ref, rank_ref, gw_ref, cnt_ref, run_ref, tile_ref):
    i = pl.program_id(0)
    tm = TM_MIX

    @pl.when(i == 0)
    def _():
        run_ref[...] = jnp.zeros_like(run_ref)

    gt_m = mod_ref[0, 2:3, :]
    sh_f = mod_ref[0, 3:4, :]
    sc_f = mod_ref[0, 4:5, :]
    ng = N_ROUTE_GROUPS
    gsz = N_EXPERTS // ng
    neg = -jnp.inf

    ts = tm // MIX_SUBTILES
    run = run_ref[:, 0:1]
    for s in range(MIX_SUBTILES):
        r = slice(s * ts, (s + 1) * ts)
        chunks = slice(s * (ts // S5_Q), (s + 1) * (ts // S5_Q))
        yy = _from_group_chunks(yt_ref.at[:, chunks, :], tile_ref.at[s]) + d_ref[...] * u_ref[r, :]
        g = _gelu_tanh(yy)
        gl = jnp.dot(g.astype(BF16), wglu_ref[...], preferred_element_type=F32) + bglu_ref[...]
        ob = g * jax.nn.sigmoid(gl)
        bn = _rms(ob, gos_ref[...]).astype(BF16)
        o = (jnp.dot(an_ref[r, :], woa_ref[...], preferred_element_type=F32)
             + jnp.dot(bn, wob_ref[...], preferred_element_type=F32))
        x1 = x_ref[r, :] + gt_m * _rms(o, gpm_ref[...])
        x1_ref[r, :] = x1
        h2 = _rms(x1, gpf_ref[...]) * (1.0 + sc_f) + sh_f
        h2_ref[r, :] = _pack_halves(h2)

        logits = lax.dot_general(wr_ref[...], h2, (((1,), (1,)), ((), ())),
                                 preferred_element_type=F32, precision=lax.Precision.HIGHEST)
        scores = jax.nn.sigmoid(logits)
        biased = scores + br_ref[...]
        b3 = biased.reshape(ng, gsz, ts)
        s3 = scores.reshape(ng, gsz, ts)
        sub = lax.broadcasted_iota(jnp.int32, (ng, gsz, ts), 1).astype(F32)
        grp = lax.broadcasted_iota(jnp.int32, (ng, gsz, ts), 0).astype(F32)
        eid = grp * gsz + sub
        m1 = jnp.max(b3, axis=1, keepdims=True)
        i1 = jnp.min(jnp.where(b3 == m1, sub, float(gsz)), axis=1, keepdims=True)
        m2 = jnp.max(jnp.where(sub == i1, neg, b3), axis=1, keepdims=True)
        gs = m1 + m2
        gi = lax.broadcasted_iota(jnp.int32, (ng, 1, ts), 0)
        beaten = jnp.zeros((ng, 1, ts), F32)
        for gp in range(ng):
            o_ = gs[gp:gp + 1]
            beats = (o_ > gs) | ((o_ == gs) & (gi > gp))
            beaten = beaten + beats.astype(F32)
        gmask = beaten < float(TOPK_ROUTE_GROUPS)
        masked = jnp.where(gmask, b3, neg)

        sels = []
        picked = jnp.zeros((ng, gsz, ts), F32)
        for k in range(TOP_K):
            m = jnp.max(jnp.max(masked, axis=0, keepdims=True), axis=1, keepdims=True)
            cand = jnp.where(masked == m, eid, float(N_EXPERTS))
            sel = jnp.min(jnp.min(cand, axis=0, keepdims=True), axis=1, keepdims=True)
            oh = eid == sel
            masked = jnp.where(oh, neg, masked)
            picked = jnp.where(oh, 1.0, picked)
            sels.append(sel)

        pm = picked.reshape(N_EXPERTS, ts)
        prefix = jnp.dot(pm.astype(BF16), tri_ref[:ts, :ts], preferred_element_type=F32) + run
        p3 = prefix.reshape(ng, gsz, ts)
        run = run + jnp.sum(pm, axis=1, keepdims=True)

        sc_rows = []
        for k in range(TOP_K):
            oh = eid == sels[k]
            sc_k = jnp.sum(jnp.sum(jnp.where(oh, s3, 0.0), axis=0, keepdims=True), axis=1, keepdims=True)
            rk_k = jnp.sum(jnp.sum(jnp.where(oh, p3, 0.0), axis=0, keepdims=True), axis=1, keepdims=True)
            sc_rows.append(sc_k)
            eidx_ref[k:k + 1, r] = sels[k].reshape(1, ts).astype(jnp.int32)
            rank_ref[k:k + 1, r] = rk_k.reshape(1, ts).astype(jnp.int32)
        tot = sc_rows[0]
        for k in range(1, TOP_K):
            tot = tot + sc_rows[k]
        inv = ROUTED_SCALE / (tot + 1e-20)
        for k in range(TOP_K):
            gw_ref[k:k + 1, r] = (sc_rows[k] * inv).reshape(1, ts)

    run_new = jnp.broadcast_to(run, run_ref.shape)
    run_ref[...] = run_new
    cnt_ref[...] = run_new


def _mix(half, a_n, yt, u2, x2, mod, d_skip, wglu_bf, b_glu, g_out_ssm, wo_a, wo_b, g_post_mix,
         g_pre_ffn, w_router_t, b_router_col, tri):
    tm = TM_MIX
    t0 = half * (HALF_TOK // tm)
    row = lambda n: pl.BlockSpec((1, n), lambda i: (0, 0))
    full = lambda a, b: pl.BlockSpec((a, b), lambda i: (0, 0))
    tok_in = lambda n: pl.BlockSpec((tm, n), lambda i: (t0 + i, 0))
    tok = lambda n: pl.BlockSpec((tm, n), lambda i: (i, 0))
    col = pl.BlockSpec((TOP_K, tm), lambda i: (0, i))
    return pl.pallas_call(
        _mix_kernel,
        grid=(HALF_TOK // tm,),
        in_specs=[tok_in(CONV_CH),
                  pl.BlockSpec((SSM_GROUPS, tm // S5_Q, S5_QH), lambda i: (0, t0 + i, 0)),
                  tok_in(SSM_CH), tok_in(D_MODEL),
                  pl.BlockSpec((1, 8, D_MODEL), lambda i: (half, 0, 0)),
                  row(SSM_CH), full(SSM_CH, SSM_CH), row(SSM_CH), row(SSM_CH),
                  full(CONV_CH, D_MODEL), full(SSM_CH, D_MODEL), row(D_MODEL), row(D_MODEL),
                  full(N_EXPERTS, D_MODEL), full(N_EXPERTS, 1), full(tm, tm)],
        out_specs=[tok(D_MODEL), tok(D_MODEL // 2), col, col, col,
                   pl.BlockSpec((N_EXPERTS, 128), lambda i: (0, 0))],
        out_shape=[jax.ShapeDtypeStruct((HALF_TOK, D_MODEL), F32),
                   jax.ShapeDtypeStruct((HALF_TOK, D_MODEL // 2), jnp.uint32),
                   jax.ShapeDtypeStruct((TOP_K, HALF_TOK), jnp.int32),
                   jax.ShapeDtypeStruct((TOP_K, HALF_TOK), jnp.int32),
                   jax.ShapeDtypeStruct((TOP_K, HALF_TOK), F32),
                   jax.ShapeDtypeStruct((N_EXPERTS, 128), F32)],
        scratch_shapes=[pltpu.VMEM((N_EXPERTS, 128), F32),
                        pltpu.VMEM((MIX_SUBTILES, SSM_CH // LANES, tm // MIX_SUBTILES, LANES), F32)],
        compiler_params=_cparams(("arbitrary",)),
        name="mix_out_router",
    )(a_n, yt, u2, x2, mod, d_skip, wglu_bf, b_glu, g_out_ssm, wo_a, wo_b, g_post_mix,
      g_pre_ffn, w_router_t, b_router_col, tri)


def _expert_kernel(blk0_ref, nblk_ref, xs_hbm, wgu_ref, wd_ref, ys_hbm, xbuf, ybuf, sem_in, sem_out):
    e = pl.program_id(0)
    n = nblk_ref[e]
    b0 = blk0_ref[e]
    n_all = blk0_ref[N_EXPERTS - 1] + nblk_ref[N_EXPERTS - 1]

    def rows(b):
        return pl.ds(pl.multiple_of(b * ROW_BLOCK, ROW_BLOCK), ROW_BLOCK)

    def in_copy(b, slot):
        return pltpu.make_async_copy(xs_hbm.at[rows(b)], xbuf.at[slot], sem_in.at[slot])

    def out_copy(b, slot):
        return pltpu.make_async_copy(ybuf.at[slot], ys_hbm.at[rows(b)], sem_out.at[slot])

    for b in range(EXPERT_AHEAD):
        @pl.when((e == 0) & (b < n_all))
        def _():
            in_copy(b, b).start()

    def admit(b):
        in_copy(b, b % EXPERT_SLOTS).wait()
        ahead = b + EXPERT_AHEAD

        @pl.when(ahead < n_all)
        def _():
            in_copy(ahead, ahead % EXPERT_SLOTS).start()

        @pl.when(b >= EXPERT_SLOTS)
        def _():
            out_copy(b - EXPERT_SLOTS, b % EXPERT_SLOTS).wait()

    def compute(b):
        slot = b % EXPERT_SLOTS
        x_lo, x_hi = _unpack_halves(xbuf[slot])
        x = jnp.concatenate([x_lo.astype(BF16), x_hi.astype(BF16)], axis=1)
        h = jnp.dot(x, wgu_ref[0], preferred_element_type=F32)
        hg = h[:, :D_EXPERT]
        act = hg * jax.nn.sigmoid(hg) * h[:, D_EXPERT:]
        ybuf[slot] = _pack_halves(jnp.dot(act.astype(BF16), wd_ref[0], preferred_element_type=F32))

    def run(blocks):
        for b in blocks:
            admit(b)
        for b in blocks:
            compute(b)
        for b in blocks:
            out_copy(b, b % EXPERT_SLOTS).start(priority=1)

    def pair(i, carry):
        b = b0 + 2 * i
        run([b, b + 1])
        return carry

    lax.fori_loop(0, n // 2, pair, 0)

    @pl.when(n % 2 == 1)
    def _():
        run([b0 + n - 1])

    @pl.when(e == N_EXPERTS - 1)
    def _():
        for j in range(1, EXPERT_SLOTS + 1):
            @pl.when(n_all >= j)
            def _():
                out_copy(n_all - j, (n_all - j) % EXPERT_SLOTS).wait()


def _experts(blk0, nblk, xs, we_gu, we_d):
    any_spec = pl.BlockSpec(memory_space=pl.ANY)
    grid_spec = pltpu.PrefetchScalarGridSpec(
        num_scalar_prefetch=2,
        grid=(N_EXPERTS,),
        in_specs=[any_spec,
                  pl.BlockSpec((1, D_MODEL, 2 * D_EXPERT), lambda e, b0, nb: (e, 0, 0)),
                  pl.BlockSpec((1, D_EXPERT, D_MODEL), lambda e, b0, nb: (e, 0, 0))],
        out_specs=any_spec,
        scratch_shapes=[pltpu.VMEM((EXPERT_SLOTS, ROW_BLOCK, D_MODEL // 2), jnp.uint32),
                        pltpu.VMEM((EXPERT_SLOTS, ROW_BLOCK, D_MODEL // 2), jnp.uint32),
                        pltpu.SemaphoreType.DMA((EXPERT_SLOTS,)),
                        pltpu.SemaphoreType.DMA((EXPERT_SLOTS,))],
    )
    return pl.pallas_call(
        _expert_kernel,
        grid_spec=grid_spec,
        out_shape=jax.ShapeDtypeStruct((N_ROWS, D_MODEL // 2), jnp.uint32),
        compiler_params=_cparams(("arbitrary",)),
        name="routed_experts",
    )(blk0, nblk, xs, we_gu, we_d)


def _final_kernel(h2_ref, yg_ref, gw_ref, x1_ref, mod_ref, wgu_ref, wd_ref, g_ref, *rest):
    o_ref = rest[-1]
    half = D_MODEL // 2
    gt_f = mod_ref[0, 5:6, :]
    x_lo, x_hi = _unpack_halves(h2_ref[...])
    h = (jnp.dot(x_lo.astype(BF16), wgu_ref[:half, :], preferred_element_type=F32)
         + jnp.dot(x_hi.astype(BF16), wgu_ref[half:, :], preferred_element_type=F32))
    hg = h[:, :D_EXPERT]
    act = hg * jax.nn.sigmoid(hg) * h[:, D_EXPERT:]
    shared = jnp.dot(act.astype(BF16), wd_ref[...], preferred_element_type=F32)
    y_lo = shared[:, :half]
    y_hi = shared[:, half:]
    for k in range(TOP_K):
        r_lo, r_hi = _unpack_halves(yg_ref[k])
        w = gw_ref[:, k:k + 1]
        y_lo = y_lo + w * r_lo
        y_hi = y_hi + w * r_hi
    ms = (jnp.sum(y_lo * y_lo, axis=-1, keepdims=True)
          + jnp.sum(y_hi * y_hi, axis=-1, keepdims=True)) * (1.0 / D_MODEL)
    inv = lax.rsqrt(ms + NORM_EPS)
    o_ref[:, :half] = x1_ref[:, :half] + gt_f[:, :half] * (y_lo * inv * g_ref[:, :half])
    o_ref[:, half:] = x1_ref[:, half:] + gt_f[:, half:] * (y_hi * inv * g_ref[:, half:])


def _final(half, out_prev, h2p, yg, gw_t, x1, mod, ws_gu, ws_d, g_post_ffn):
    tm = TM_OUT
    t0 = half * (HALF_TOK // tm)
    tok = pl.BlockSpec((tm, D_MODEL), lambda i: (i, 0))
    in_specs = [pl.BlockSpec((tm, D_MODEL // 2), lambda i: (i, 0)),
                pl.BlockSpec((TOP_K, tm, D_MODEL // 2), lambda i: (0, i, 0)),
                pl.BlockSpec((tm, TOP_K), lambda i: (i, 0)),
                tok,
                pl.BlockSpec((1, 8, D_MODEL), lambda i: (half, 0, 0)),
                pl.BlockSpec((D_MODEL, 2 * D_EXPERT), lambda i: (0, 0)),
                pl.BlockSpec((D_EXPERT, D_MODEL), lambda i: (0, 0)),
                pl.BlockSpec((1, D_MODEL), lambda i: (0, 0))]
    args = [h2p, yg, gw_t, x1, mod, ws_gu, ws_d, g_post_ffn]
    aliases = {}
    if out_prev is not None:
        aliases = {len(args): 0}
        in_specs.append(pl.BlockSpec(memory_space=pl.ANY))
        args.append(out_prev)
    return pl.pallas_call(
        _final_kernel,
        grid=(HALF_TOK // tm,),
        in_specs=in_specs,
        out_specs=pl.BlockSpec((tm, D_MODEL), lambda i: (t0 + i, 0)),
        out_shape=jax.ShapeDtypeStruct((N_TOK, D_MODEL), F32),
        input_output_aliases=aliases,
        compiler_params=_cparams(("parallel",)),
        name="shared_final",
    )(*args)


def _sc_worker_id():
    return lax.axis_index("s") * SC_CORES + lax.axis_index("c")


def _dispatch_body(h_hbm, dest_hbm, xs_hbm, idx_v, rows_v, sem_l, sem_s):
    n = SC_CHUNKS_PER_WORKER
    c0 = _sc_worker_id() * n

    def load(i, b):
        return pltpu.async_copy(h_hbm.at[pl.ds((c0 + i) * SC_W, SC_W)], rows_v.at[b], sem_l.at[b])

    loads = [None] * n
    scat = [None] * n
    loads[0] = load(0, 0)
    for i in range(n):
        b = i % 2
        pltpu.sync_copy(dest_hbm.at[c0 + i], idx_v.at[b])
        loads[i].wait()
        if i + 1 < n:
            if i >= 1:
                for d in scat[i - 1]:
                    d.wait()
            loads[i + 1] = load(i + 1, 1 - b)
        scat[i] = [pltpu.async_copy(rows_v.at[b], xs_hbm.at[idx_v.at[b].at[k]], sem_s.at[b])
                   for k in range(TOP_K)]
    for i in (n - 2, n - 1):
        for d in scat[i]:
            d.wait()


def _sc_dispatch(h2p, dest3):
    mesh = plsc.VectorSubcoreMesh(core_axis_name="c", subcore_axis_name="s")
    return pl.kernel(
        _dispatch_body, mesh=mesh,
        out_type=jax.ShapeDtypeStruct((N_ROWS, D_MODEL // 2), jnp.uint32),
        scratch_types=[pltpu.VMEM((2, TOP_K, SC_W), jnp.int32),
                       pltpu.VMEM((2, SC_W, D_MODEL // 2), jnp.uint32),
                       pltpu.SemaphoreType.DMA((2,)), pltpu.SemaphoreType.DMA((2,))],
    )(h2p, dest3)


def _combine_body(ys_hbm, dest_hbm, yg_hbm, idx_v, rows_v, sem_g, sem_w):
    c0 = _sc_worker_id() * SC_CHUNKS_PER_WORKER

    @pl.loop(0, SC_CHUNKS_PER_WORKER)
    def _(i):
        c = c0 + i
        pltpu.sync_copy(dest_hbm.at[c], idx_v)
        g = [None] * TOP_K
        w = [None] * TOP_K
        g[0] = pltpu.async_copy(ys_hbm.at[idx_v.at[0]], rows_v.at[0], sem_g.at[0])
        for k in range(TOP_K):
            b = k % 2
            g[k].wait()
            if k + 1 < TOP_K:
                if k >= 1:
                    w[k - 1].wait()
                g[k + 1] = pltpu.async_copy(ys_hbm.at[idx_v.at[k + 1]], rows_v.at[1 - b], sem_g.at[1 - b])
            w[k] = pltpu.async_copy(rows_v.at[b], yg_hbm.at[k].at[pl.ds(c * SC_W, SC_W)], sem_w.at[b])
        w[TOP_K - 2].wait()
        w[TOP_K - 1].wait()


def _sc_combine(ysp, dest3):
    mesh = plsc.VectorSubcoreMesh(core_axis_name="c", subcore_axis_name="s")
    return pl.kernel(
        _combine_body, mesh=mesh,
        out_type=jax.ShapeDtypeStruct((TOP_K, HALF_TOK, D_MODEL // 2), jnp.uint32),
        scratch_types=[pltpu.VMEM((TOP_K, SC_W), jnp.int32),
                       pltpu.VMEM((2, SC_W, D_MODEL // 2), jnp.uint32),
                       pltpu.SemaphoreType.DMA((2,)), pltpu.SemaphoreType.DMA((2,))],
    )(ysp, dest3)


def kernel(x, c, w_ada, b_ada, g_pre_mix, g_post_mix, w_in, conv_w, conv_b, conv_ln_g, conv_ln_b,
           ssm_a_re, ssm_a_im, ssm_log_dt, ssm_b_re, ssm_b_im, ssm_c_re, ssm_c_im, ssm_d,
           ssm_w_glu, ssm_b_glu, g_out_conv, g_out_ssm, w_out, g_pre_ffn, g_post_ffn,
           w_router, b_router, we_gate, we_up, we_down, ws_gate, ws_up, ws_down):
    l = 0
    x2 = x.reshape(N_TOK, D_MODEL)
    r1 = lambda a: a.reshape(1, -1)

    c_pad = jnp.zeros((8, D_MODEL), F32).at[:BATCH].set(c)
    mod = _ada(c_pad, w_ada[l], r1(b_ada[l]))[:BATCH].reshape(BATCH, 6, D_MODEL)
    mod = jnp.concatenate([mod, jnp.zeros((BATCH, 2, D_MODEL), F32)], axis=1)

    v, u, ut = _inproj(x2, mod, r1(g_pre_mix[l]), w_in[l].astype(BF16))
    cw = jnp.concatenate([conv_w[l].reshape(CONV_WIDTH, CONV_CH), jnp.zeros((1, CONV_CH), F32)], axis=0)
    a_n = _conv(v.reshape(BATCH, SEQ, CONV_CH), cw, r1(conv_b[l]), r1(conv_ln_g[l]),
                r1(conv_ln_b[l]), r1(g_out_conv[l])).reshape(N_TOK, CONV_CH)

    pwr, pwi, s5_params, a_cat, b_q, b_s = _s5_operators(
        ssm_a_re[l], ssm_a_im[l], ssm_log_dt[l], ssm_b_re[l], ssm_b_im[l], ssm_c_re[l], ssm_c_im[l])
    yt, we_gu, we_d = _s5(ut, pwr, pwi, s5_params, a_cat, b_q, b_s,
                          we_gate[l], we_up[l], we_down[l])

    tm = TM_MIX
    tri = (jnp.arange(tm)[:, None] < jnp.arange(tm)[None, :]).astype(BF16)
    wo = w_out[l].astype(BF16)
    mix_params = (r1(ssm_d[l]), ssm_w_glu[l].astype(BF16), r1(ssm_b_glu[l]), r1(g_out_ssm[l]),
                  wo[:CONV_CH], wo[CONV_CH:], r1(g_post_mix[l]), r1(g_pre_ffn[l]),
                  w_router[l].T, b_router[l].reshape(N_EXPERTS, 1), tri)
    ws_gu = jnp.concatenate([ws_gate[l], ws_up[l]], axis=1).astype(BF16)
    ws_d = ws_down[l].astype(BF16)
    e_ids = jnp.arange(N_EXPERTS, dtype=jnp.int32)

    out = None
    for half in range(N_HALVES):
        x1, h2, eidx, rank, gw, cnt = _mix(half, a_n, yt, u, x2, mod, *mix_params)
        counts = cnt[:, 0].astype(jnp.int32)
        padded = (counts + ROW_BLOCK - 1) // ROW_BLOCK * ROW_BLOCK
        pstart = jnp.cumsum(padded) - padded
        dest = rank + jnp.sum(jnp.where(eidx[..., None] == e_ids, pstart, 0), axis=-1)
        dest3 = dest.reshape(TOP_K, HALF_TOK // SC_W, SC_W).transpose(1, 0, 2)

        xs = _sc_dispatch(h2, dest3)
        ys = _experts(pstart // ROW_BLOCK, padded // ROW_BLOCK, xs, we_gu, we_d)
        yg = _sc_combine(ys, dest3)
        out = _final(half, out, h2, yg, gw.T, x1, mod, ws_gu, ws_d, r1(g_post_ffn[l]))
    return out.reshape(BATCH, SEQ, D_MODEL)
```

```python
import functools
import math

import jax
import jax.numpy as jnp
from jax import lax
from jax.experimental import pallas as pl
from jax.experimental.pallas import tpu as pltpu
from jax.experimental.pallas import tpu_sc as plsc

F32 = jnp.float32
BF16 = jnp.bfloat16

D_MODEL = 1024
BATCH = 2
SEQ = 8192
N_TOK = BATCH * SEQ
CONV_CH = 512
CONV_WIDTH = 31
SSM_CH = 512
SSM_GROUP_CH = 16
SSM_GROUPS = 32
SSM_STATE = 64
D_IN = 2 * CONV_CH + SSM_CH
N_EXPERTS = 64
TOP_K = 8
N_ROUTE_GROUPS = 8
TOPK_ROUTE_GROUPS = 4
D_EXPERT = 256
ROUTED_SCALE = 2.5
NORM_EPS = 1e-6

SUBLANES = 8
LANES = 128

TM_IN = 512
IN_SUBTILES = 2
TL_CONV = 512
CONV_HALO = 32
CONV_ROWS = 64
EXPERTS_PER_CONV_STEP = N_EXPERTS * TL_CONV // N_TOK
assert EXPERTS_PER_CONV_STEP * N_TOK == N_EXPERTS * TL_CONV
S5_Q = 32
S5_QH = S5_Q * SSM_GROUP_CH
S5_CHUNKS = N_TOK // S5_Q
S5_CHUNKS_PER_SEQ = SEQ // S5_Q
TM_MIX = 512
ROW_BLOCK = 512
EXPERT_AHEAD = 4
EXPERT_SLOTS = 6
HALF_TOK = SEQ
N_HALVES = N_TOK // HALF_TOK
N_BLOCKS = HALF_TOK * TOP_K // ROW_BLOCK + N_EXPERTS
N_ROWS = N_BLOCKS * ROW_BLOCK
TM_OUT = 512
SC_CORES = 2
SC_SUBCORES = 16
SC_WORKERS = SC_CORES * SC_SUBCORES
SC_W = 64
SC_CHUNKS_PER_WORKER = HALF_TOK // (SC_WORKERS * SC_W)
VMEM_LIMIT = 48 * 1024 * 1024


def _cparams(sem):
    return pltpu.CompilerParams(dimension_semantics=sem, vmem_limit_bytes=VMEM_LIMIT)


def _pack_rounded_halves(xr):
    n = xr.shape[-1] // 2
    lo = lax.bitcast_convert_type(xr[:, :n], jnp.uint32)
    hi = lax.bitcast_convert_type(xr[:, n:], jnp.uint32)
    return hi | (lo >> 16)


def _pack_halves(x):
    return _pack_rounded_halves(x.astype(BF16).astype(F32))


def _unpack_halves(p):
    lo = lax.bitcast_convert_type(p << 16, F32)
    hi = lax.bitcast_convert_type(p & jnp.uint32(0xFFFF0000), F32)
    return lo, hi


def _rms(x, g):
    return x * lax.rsqrt(jnp.mean(x * x, axis=-1, keepdims=True) + NORM_EPS) * g


def _ada_kernel(c_ref, w_ref, b_ref, o_ref):
    c = c_ref[...]
    a = c * jax.nn.sigmoid(c)
    o_ref[...] = jnp.dot(a, w_ref[...], preferred_element_type=F32,
                         precision=lax.Precision.HIGHEST) + b_ref[...]


def _ada(c_pad, w_ada, b_ada):
    n = w_ada.shape[1]
    bn = 1536
    return pl.pallas_call(
        _ada_kernel,
        grid=(n // bn,),
        in_specs=[pl.BlockSpec((8, D_MODEL), lambda j: (0, 0)),
                  pl.BlockSpec((D_MODEL, bn), lambda j: (0, j)),
                  pl.BlockSpec((1, bn), lambda j: (0, j))],
        out_specs=pl.BlockSpec((8, bn), lambda j: (0, j)),
        out_shape=jax.ShapeDtypeStruct((8, n), F32),
        compiler_params=_cparams(("arbitrary",)),
        name="ada_mod",
    )(c_pad, w_ada, b_ada)


GROUPS_PER_LANE_TILE = LANES // SSM_GROUP_CH


def _to_group_chunks(u, tile_ref, ut_ref):
    n_chunks = u.shape[0] // S5_Q
    for j in range(SSM_CH // LANES):
        tile_ref[j] = u[:, LANES * j:LANES * (j + 1)]
    for j in range(SSM_CH // LANES):
        rows_t = [tile_ref[j, pl.ds(t, n_chunks, stride=S5_Q), :] for t in range(S5_Q)]
        for gg in range(GROUPS_PER_LANE_TILE):
            lo = gg * SSM_GROUP_CH
            row = jnp.concatenate([r[:, lo:lo + SSM_GROUP_CH] for r in rows_t], axis=1)
            ut_ref[j * GROUPS_PER_LANE_TILE + gg] = row.astype(ut_ref.dtype)


def _from_group_chunks(yt_ref, tile_ref):
    n_chunks = yt_ref.shape[1]
    for j in range(SSM_CH // LANES):
        for t in range(S5_Q):
            lo = t * SSM_GROUP_CH
            piece = jnp.concatenate(
                [yt_ref[j * GROUPS_PER_LANE_TILE + gg, :, lo:lo + SSM_GROUP_CH]
                 for gg in range(GROUPS_PER_LANE_TILE)], axis=1)
            tile_ref[j, pl.ds(t, n_chunks, stride=S5_Q), :] = piece
    return jnp.concatenate([tile_ref[j] for j in range(SSM_CH // LANES)], axis=1)


def _inproj_kernel(x_ref, mod_ref, g_ref, w_ref, v_ref, u_ref, ut_ref, tile_ref):
    sh = mod_ref[0, 0:1, :]
    sc = mod_ref[0, 1:2, :]
    sub = TM_IN // IN_SUBTILES
    sub_chunks = sub // S5_Q
    for s in range(IN_SUBTILES):
        r = slice(s * sub, (s + 1) * sub)
        h = _rms(x_ref[r, :], g_ref[...]) * (1.0 + sc) + sh
        z = jnp.dot(h.astype(BF16), w_ref[...], preferred_element_type=F32)
        v_ref[r, :] = z[:, :CONV_CH] * jax.nn.sigmoid(z[:, CONV_CH:2 * CONV_CH])
        u = z[:, 2 * CONV_CH:]
        u_ref[r, :] = u
        _to_group_chunks(u, tile_ref.at[s], ut_ref.at[:, s * sub_chunks:(s + 1) * sub_chunks, :])


def _inproj(x2, mod, g_pre, w_in_bf):
    tiles_per_seq = SEQ // TM_IN
    return pl.pallas_call(
        _inproj_kernel,
        grid=(N_TOK // TM_IN,),
        in_specs=[pl.BlockSpec((TM_IN, D_MODEL), lambda i: (i, 0)),
                  pl.BlockSpec((1, 8, D_MODEL), lambda i: (i // tiles_per_seq, 0, 0)),
                  pl.BlockSpec((1, D_MODEL), lambda i: (0, 0)),
                  pl.BlockSpec((D_MODEL, D_IN), lambda i: (0, 0))],
        out_specs=[pl.BlockSpec((TM_IN, CONV_CH), lambda i: (i, 0)),
                   pl.BlockSpec((TM_IN, SSM_CH), lambda i: (i, 0)),
                   pl.BlockSpec((SSM_GROUPS, TM_IN // S5_Q, S5_QH), lambda i: (0, i, 0))],
        out_shape=[jax.ShapeDtypeStruct((N_TOK, CONV_CH), F32),
                   jax.ShapeDtypeStruct((N_TOK, SSM_CH), F32),
                   jax.ShapeDtypeStruct((SSM_GROUPS, S5_CHUNKS, S5_QH), BF16)],
        scratch_shapes=[pltpu.VMEM((IN_SUBTILES, SSM_CH // LANES, TM_IN // IN_SUBTILES, LANES), F32)],
        compiler_params=_cparams(("parallel",)),
        name="in_proj",
    )(x2, mod, g_pre, w_in_bf)


def _conv_kernel(vc_ref, vp_ref, w_ref, cb_ref, lg_ref, lb_ref, go_ref, wg_ref, wu_ref, wd_ref,
                 o_ref, wgu_o, wd_o, sh_ref):
    for q in range(EXPERTS_PER_CONV_STEP):
        wgu_o[q, :, :D_EXPERT] = wg_ref[q].astype(BF16)
        wgu_o[q, :, D_EXPERT:] = wu_ref[q].astype(BF16)
        wd_o[q] = wd_ref[q].astype(BF16)

    i = pl.program_id(1)
    keep = (i > 0).astype(F32)
    n_ext = TL_CONV + CONV_HALO
    sh_ref[0, 0:CONV_HALO, :] = vp_ref[0] * keep
    sh_ref[0, CONV_HALO:, :] = vc_ref[0]
    for s in range(1, SUBLANES):
        sh_ref[s, 0:n_ext - s, :] = sh_ref[0, s:n_ext, :]
    off = CONV_HALO - (CONV_WIDTH - 1)
    for r in range(TL_CONV // CONV_ROWS):
        acc = None
        for j in range(CONV_WIDTH):
            s = (off + j) % SUBLANES
            al = r * CONV_ROWS + (off + j) - s
            term = w_ref[j:j + 1, :] * sh_ref[s, al:al + CONV_ROWS, :]
            acc = term if acc is None else acc + term
        y = acc + cb_ref[...]
        mu = jnp.mean(y, axis=-1, keepdims=True)
        d = y - mu
        var = jnp.mean(d * d, axis=-1, keepdims=True)
        yn = d * lax.rsqrt(var + NORM_EPS) * lg_ref[...] + lb_ref[...]
        a = yn * jax.nn.sigmoid(yn)
        o_ref[0, r * CONV_ROWS:(r + 1) * CONV_ROWS, :] = _rms(a, go_ref[...]).astype(BF16)


def _conv(v3, conv_w, conv_b, ln_g, ln_b, g_out, we_gate, we_up, we_down):
    halo_per_tile = TL_CONV // CONV_HALO
    steps_per_seq = SEQ // TL_CONV
    vec = pl.BlockSpec((1, CONV_CH), lambda b, i: (0, 0))
    ex = EXPERTS_PER_CONV_STEP
    w_in = pl.BlockSpec((ex, D_MODEL, D_EXPERT), lambda b, i: (b * steps_per_seq + i, 0, 0))
    return pl.pallas_call(
        _conv_kernel,
        grid=(BATCH, steps_per_seq),
        in_specs=[pl.BlockSpec((1, TL_CONV, CONV_CH), lambda b, i: (b, i, 0)),
                  pl.BlockSpec((1, CONV_HALO, CONV_CH),
                               lambda b, i: (b, jnp.maximum(i * halo_per_tile - 1, 0), 0)),
                  pl.BlockSpec((CONV_WIDTH + 1, CONV_CH), lambda b, i: (0, 0)),
                  vec, vec, vec, vec,
                  w_in, w_in,
                  pl.BlockSpec((ex, D_EXPERT, D_MODEL), lambda b, i: (b * steps_per_seq + i, 0, 0))],
        out_specs=[pl.BlockSpec((1, TL_CONV, CONV_CH), lambda b, i: (b, i, 0)),
                   pl.BlockSpec((ex, D_MODEL, 2 * D_EXPERT), lambda b, i: (b * steps_per_seq + i, 0, 0)),
                   pl.BlockSpec((ex, D_EXPERT, D_MODEL), lambda b, i: (b * steps_per_seq + i, 0, 0))],
        out_shape=[jax.ShapeDtypeStruct((BATCH, SEQ, CONV_CH), BF16),
                   jax.ShapeDtypeStruct((N_EXPERTS, D_MODEL, 2 * D_EXPERT), BF16),
                   jax.ShapeDtypeStruct((N_EXPERTS, D_EXPERT, D_MODEL), BF16)],
        scratch_shapes=[pltpu.VMEM((SUBLANES, TL_CONV + CONV_HALO, CONV_CH), F32)],
        compiler_params=_cparams(("parallel", "arbitrary")),
        name="conv_module",
    )(v3, v3, conv_w, conv_b, ln_g, ln_b, g_out, we_gate, we_up, we_down)


S5_GROUP_ROWS = S5_CHUNKS + 8


S5_POW_ROWS = (S5_Q + 1 + SUBLANES - 1) // SUBLANES * SUBLANES
(S5_BB_RI, S5_BB_NIR, S5_BB_IR, S5_BB_RNI, S5_CC_RI, S5_CC_NIR, S5_N_PARAM) = range(7)


def _s5_kernel(ut_ref, pwr_ref, pwi_ref, par_ref, a_ref, bq_ref, bs_ref, yt_ref, sin_s, sp_s):
    phase = pl.program_id(0)
    g = pl.program_id(1)
    q = S5_Q
    n = 2 * SSM_STATE
    row0 = pl.multiple_of(g * S5_GROUP_ROWS, 8)

    def lam_pow(j):
        return pwr_ref[0, j:j + 1, :], pwi_ref[0, j:j + 1, :]

    @pl.when(phase == 0)
    def _():
        bb_ri, bb_nir = par_ref[0, S5_BB_RI], par_ref[0, S5_BB_NIR]
        bb_ir, bb_rni = par_ref[0, S5_BB_IR], par_ref[0, S5_BB_RNI]
        blk_q, blk_s = [], []
        for t in range(q):
            pr, pi_ = lam_pow(q - 1 - t)
            blk_q.append(pr * bb_ri + pi_ * bb_nir)
            blk_s.append(pr * bb_ir + pi_ * bb_rni)
        wst = jnp.concatenate([jnp.concatenate(blk_q, axis=0), jnp.concatenate(blk_s, axis=0)], axis=1)
        r = jnp.dot(ut_ref[0], wst.astype(BF16), preferred_element_type=F32)
        sin_s[0, pl.ds(row0, S5_CHUNKS), :] = r[:, :n]
        sin_s[1, pl.ds(row0, S5_CHUNKS), :] = r[:, n:]

    @pl.when((phase == 1) & (g == 0))
    def _():
        a = a_ref[...]
        bq = bq_ref[...]
        bs = bs_ref[...]

        def body(c, carry):
            nxt = []
            for b in range(BATCH):
                x, xs = carry[b]
                rows = pl.ds(b * S5_CHUNKS_PER_SEQ + c, SSM_GROUPS, stride=S5_GROUP_ROWS)
                sp_s[rows, :] = x
                nxt.append((a * x + bq * xs + sin_s[0, rows, :], a * xs + bs * x + sin_s[1, rows, :]))
            return tuple(nxt)

        z = jnp.zeros((SSM_GROUPS, n), F32)
        lax.fori_loop(0, S5_CHUNKS_PER_SEQ, body, tuple((z, z) for _ in range(BATCH)))

    @pl.when(phase == 1)
    def _():
        cc_ri, cc_nir = par_ref[0, S5_CC_RI], par_ref[0, S5_CC_NIR]
        cl = []
        for j in range(q + 1):
            pr, pi_ = lam_pow(j)
            cl.append(pr * cc_ri + pi_ * cc_nir)
        cl_lo = jnp.concatenate(cl[:q], axis=0)
        cl_hi = jnp.concatenate(cl[1:], axis=0)
        lane = lax.broadcasted_iota(jnp.int32, (1, n), 1)
        vgt = (cl_hi * jnp.where(lane < SSM_STATE, 1.0, -1.0)).astype(BF16)
        kt = lax.dot_general(par_ref[0, S5_BB_RNI], cl_lo, (((1,), (1,)), ((), ())),
                             preferred_element_type=F32, precision=lax.Precision.HIGHEST)
        padded = jnp.concatenate([jnp.zeros_like(kt), kt], axis=1)
        tg = jnp.concatenate(
            [padded[:, (q - t) * SSM_GROUP_CH:(q - t) * SSM_GROUP_CH + S5_QH] for t in range(q)],
            axis=0).astype(BF16)
        sp = sp_s[pl.ds(row0, S5_CHUNKS), :]
        y = jnp.dot(ut_ref[0], tg, preferred_element_type=F32)
        yt_ref[0] = y + lax.dot_general(sp.astype(BF16), vgt, (((1,), (1,)), ((), ())),
                                        preferred_element_type=F32)


def _s5(ut, pwr, pwi, params, a_cat, b_q, b_s):
    vec = pl.BlockSpec((SSM_GROUPS, 2 * SSM_STATE), lambda p, g: (0, 0))
    powers = pl.BlockSpec((1, S5_POW_ROWS, 2 * SSM_STATE), lambda p, g: (g, 0, 0))
    return pl.pallas_call(
        _s5_kernel,
        grid=(2, SSM_GROUPS),
        in_specs=[pl.BlockSpec((1, S5_CHUNKS, S5_QH), lambda p, g: (g, 0, 0)),
                  powers, powers,
                  pl.BlockSpec((1, S5_N_PARAM, SSM_GROUP_CH, 2 * SSM_STATE), lambda p, g: (g, 0, 0, 0)),
                  vec, vec, vec],
        out_specs=pl.BlockSpec((1, S5_CHUNKS, S5_QH), lambda p, g: (g * p, 0, 0)),
        out_shape=jax.ShapeDtypeStruct((SSM_GROUPS, S5_CHUNKS, S5_QH), F32),
        scratch_shapes=[pltpu.VMEM((2, SSM_GROUPS * S5_GROUP_ROWS, 2 * SSM_STATE), F32),
                        pltpu.VMEM((SSM_GROUPS * S5_GROUP_ROWS, 2 * SSM_STATE), F32)],
        compiler_params=_cparams(("arbitrary", "arbitrary")),
        name="s5_chunked",
    )(ut, pwr, pwi, params, a_cat, b_q, b_s)


def _s5_operators(a_re, a_im, log_dt, b_re, b_im, c_re, c_im):
    q = S5_Q
    dt = jnp.exp(log_dt)[:, None]
    ar, ai = a_re, a_im
    mag = jnp.exp(ar * dt)
    lr = mag * jnp.cos(ai * dt)
    li = mag * jnp.sin(ai * dt)
    den = ar * ar + ai * ai
    nr = lr - 1.0
    kr = (nr * ar + li * ai) / den
    ki = (li * ar - nr * ai) / den
    bbr = kr[..., None] * b_re - ki[..., None] * b_im
    bbi = kr[..., None] * b_im + ki[..., None] * b_re
    j = jnp.arange(q + 1, dtype=F32)[None, :, None]
    pmag = jnp.exp(ar[:, None, :] * dt[:, :, None] * j)
    pang = ai[:, None, :] * dt[:, :, None] * j
    pr = pmag * jnp.cos(pang)
    pi_ = pmag * jnp.sin(pang)
    pad = ((0, 0), (0, S5_POW_ROWS - (q + 1)), (0, 0))
    pwr = jnp.pad(jnp.concatenate([pr, pr], axis=-1), pad)
    pwi = jnp.pad(jnp.concatenate([pi_, pi_], axis=-1), pad)
    br_t = bbr.transpose(0, 2, 1)
    bi_t = bbi.transpose(0, 2, 1)
    cat = lambda a, b: jnp.concatenate([a, b], axis=-1)
    stack = [None] * S5_N_PARAM
    stack[S5_BB_RI] = cat(br_t, bi_t)
    stack[S5_BB_NIR] = cat(-bi_t, br_t)
    stack[S5_BB_IR] = cat(bi_t, br_t)
    stack[S5_BB_RNI] = cat(br_t, -bi_t)
    stack[S5_CC_RI] = cat(c_re, c_im)
    stack[S5_CC_NIR] = cat(-c_im, c_re)
    params = jnp.stack(stack, axis=1)
    aq_r, aq_i = pr[:, q], pi_[:, q]
    a_cat = cat(aq_r, aq_r)
    b_q = cat(-aq_i, aq_i)
    b_s = cat(aq_i, -aq_i)
    return pwr, pwi, params, a_cat, b_q, b_s


def _gelu_tanh(x):
    return 0.5 * x * (1.0 + jnp.tanh(math.sqrt(2.0 / math.pi) * (x + 0.044715 * (x * x * x))))


def _mix_kernel(an_ref, yt_ref, u_ref, x_ref, mod_ref, d_ref, wglu_ref, bglu_ref, gos_ref,
                woa_ref, wob_ref, gpm_ref, gpf_ref, wr_ref, br_ref, tri_ref,
                x1_ref, h2_ref, eidx_ref, rank_ref, gw_ref, cnt_ref, run_ref, tile_ref):
    i = pl.program_id(0)
    tm = TM_MIX

    @pl.when(i == 0)
    def _():
        run_ref[...] = jnp.zeros_like(run_ref)

    gt_m = mod_ref[0, 2:3, :]
    sh_f = mod_ref[0, 3:4, :]
    sc_f = mod_ref[0, 4:5, :]

    yy = _from_group_chunks(yt_ref, tile_ref) + d_ref[...] * u_ref[...]
    g = _gelu_tanh(yy)
    gl = jnp.dot(g.astype(BF16), wglu_ref[...], preferred_element_type=F32) + bglu_ref[...]
    ob = g * jax.nn.sigmoid(gl)
    bn = _rms(ob, gos_ref[...]).astype(BF16)
    o = (jnp.dot(an_ref[...], woa_ref[...], preferred_element_type=F32)
         + jnp.dot(bn, wob_ref[...], preferred_element_type=F32))
    x1 = x_ref[...] + gt_m * _rms(o, gpm_ref[...])
    x1_ref[...] = x1
    h2 = _rms(x1, gpf_ref[...]) * (1.0 + sc_f) + sh_f
    h2_hi = h2.astype(BF16)
    h2_hi32 = h2_hi.astype(F32)
    h2_ref[...] = _pack_rounded_halves(h2_hi32)

    h2_lo = (h2 - h2_hi32).astype(BF16)
    nt = (((1,), (1,)), ((), ()))
    both = lax.dot_general(wr_ref[...], h2_hi, nt, preferred_element_type=F32)
    logits = (both[:N_EXPERTS] + both[N_EXPERTS:]
              + lax.dot_general(wr_ref[:N_EXPERTS, :], h2_lo, nt, preferred_element_type=F32))
    scores = jax.nn.sigmoid(logits)
    biased = scores + br_ref[...]
    ng = N_ROUTE_GROUPS
    gsz = N_EXPERTS // ng
    b3 = biased.reshape(ng, gsz, tm)
    s3 = scores.reshape(ng, gsz, tm)
    sub = lax.broadcasted_iota(jnp.int32, (ng, gsz, tm), 1).astype(F32)
    grp = lax.broadcasted_iota(jnp.int32, (ng, gsz, tm), 0).astype(F32)
    eid = grp * gsz + sub
    neg = -jnp.inf
    m1 = jnp.max(b3, axis=1, keepdims=True)
    i1 = jnp.min(jnp.where(b3 == m1, sub, float(gsz)), axis=1, keepdims=True)
    m2 = jnp.max(jnp.where(sub == i1, neg, b3), axis=1, keepdims=True)
    gs = m1 + m2
    gi = lax.broadcasted_iota(jnp.int32, (ng, 1, tm), 0)
    beaten = jnp.zeros((ng, 1, tm), F32)
    for gp in range(ng):
        o_ = gs[gp:gp + 1]
        beats = (o_ > gs) | ((o_ == gs) & (gi > gp))
        beaten = beaten + beats.astype(F32)
    gmask = beaten < float(TOPK_ROUTE_GROUPS)
    masked = jnp.where(gmask, b3, neg)

    sels = []
    picked = jnp.zeros((ng, gsz, tm), F32)
    for k in range(TOP_K):
        m = jnp.max(jnp.max(masked, axis=0, keepdims=True), axis=1, keepdims=True)
        cand = jnp.where(masked == m, eid, float(N_EXPERTS))
        sel = jnp.min(jnp.min(cand, axis=0, keepdims=True), axis=1, keepdims=True)
        oh = eid == sel
        masked = jnp.where(oh, neg, masked)
        picked = jnp.where(oh, 1.0, picked)
        sels.append(sel)

    pm = picked.reshape(N_EXPERTS, tm)
    prefix = jnp.dot(pm.astype(BF16), tri_ref[...], preferred_element_type=F32) + run_ref[:, 0:1]
    p3 = prefix.reshape(ng, gsz, tm)
    run_new = run_ref[...] + jnp.sum(pm, axis=1, keepdims=True)
    run_ref[...] = run_new
    cnt_ref[...] = run_new

    sc_rows = []
    for k in range(TOP_K):
        oh = eid == sels[k]
        sc_k = jnp.sum(jnp.sum(jnp.where(oh, s3, 0.0), axis=0, keepdims=True), axis=1, keepdims=True)
        rk_k = jnp.sum(jnp.sum(jnp.where(oh, p3, 0.0), axis=0, keepdims=True), axis=1, keepdims=True)
        sc_rows.append(sc_k)
        eidx_ref[k:k + 1, :] = sels[k].reshape(1, tm).astype(jnp.int32)
        rank_ref[k:k + 1, :] = rk_k.reshape(1, tm).astype(jnp.int32)
    tot = sc_rows[0]
    for k in range(1, TOP_K):
        tot = tot + sc_rows[k]
    inv = ROUTED_SCALE / (tot + 1e-20)
    for k in range(TOP_K):
        gw_ref[k:k + 1, :] = (sc_rows[k] * inv).reshape(1, tm)


def _mix(half, a_n, yt, u2, x2, mod, d_skip, wglu_bf, b_glu, g_out_ssm, wo_a, wo_b, g_post_mix,
         g_pre_ffn, w_router_t, b_router_col, tri):
    tm = TM_MIX
    t0 = half * (HALF_TOK // tm)
    row = lambda n: pl.BlockSpec((1, n), lambda i: (0, 0))
    full = lambda a, b: pl.BlockSpec((a, b), lambda i: (0, 0))
    tok_in = lambda n: pl.BlockSpec((tm, n), lambda i: (t0 + i, 0))
    tok = lambda n: pl.BlockSpec((tm, n), lambda i: (i, 0))
    col = pl.BlockSpec((TOP_K, tm), lambda i: (0, i))
    return pl.pallas_call(
        _mix_kernel,
        grid=(HALF_TOK // tm,),
        in_specs=[tok_in(CONV_CH),
                  pl.BlockSpec((SSM_GROUPS, tm // S5_Q, S5_QH), lambda i: (0, t0 + i, 0)),
                  tok_in(SSM_CH), tok_in(D_MODEL),
                  pl.BlockSpec((1, 8, D_MODEL), lambda i: (half, 0, 0)),
                  row(SSM_CH), full(SSM_CH, SSM_CH), row(SSM_CH), row(SSM_CH),
                  full(CONV_CH, D_MODEL), full(SSM_CH, D_MODEL), row(D_MODEL), row(D_MODEL),
                  full(2 * N_EXPERTS, D_MODEL), full(N_EXPERTS, 1), full(tm, tm)],
        out_specs=[tok(D_MODEL), tok(D_MODEL // 2), col, col, col,
                   pl.BlockSpec((N_EXPERTS, 128), lambda i: (0, 0))],
        out_shape=[jax.ShapeDtypeStruct((HALF_TOK, D_MODEL), F32),
                   jax.ShapeDtypeStruct((HALF_TOK, D_MODEL // 2), jnp.uint32),
                   jax.ShapeDtypeStruct((TOP_K, HALF_TOK), jnp.int32),
                   jax.ShapeDtypeStruct((TOP_K, HALF_TOK), jnp.int32),
                   jax.ShapeDtypeStruct((TOP_K, HALF_TOK), F32),
                   jax.ShapeDtypeStruct((N_EXPERTS, 128), F32)],
        scratch_shapes=[pltpu.VMEM((N_EXPERTS, 128), F32),
                        pltpu.VMEM((SSM_CH // LANES, tm, LANES), F32)],
        compiler_params=_cparams(("arbitrary",)),
        name="mix_out_router",
    )(a_n, yt, u2, x2, mod, d_skip, wglu_bf, b_glu, g_out_ssm, wo_a, wo_b, g_post_mix,
      g_pre_ffn, w_router_t, b_router_col, tri)


def _expert_kernel(blk0_ref, nblk_ref, xs_hbm, wgu_ref, wd_ref, ys_hbm, xbuf, ybuf, sem_in, sem_out):
    e = pl.program_id(0)
    n = nblk_ref[e]
    b0 = blk0_ref[e]
    n_all = blk0_ref[N_EXPERTS - 1] + nblk_ref[N_EXPERTS - 1]

    def rows(b):
        return pl.ds(pl.multiple_of(b * ROW_BLOCK, ROW_BLOCK), ROW_BLOCK)

    def in_copy(b, slot):
        return pltpu.make_async_copy(xs_hbm.at[rows(b)], xbuf.at[slot], sem_in.at[slot])

    def out_copy(b, slot):
        return pltpu.make_async_copy(ybuf.at[slot], ys_hbm.at[rows(b)], sem_out.at[slot])

    for b in range(EXPERT_AHEAD):
        @pl.when((e == 0) & (b < n_all))
        def _():
            in_copy(b, b).start()

    def admit(b):
        in_copy(b, b % EXPERT_SLOTS).wait()
        ahead = b + EXPERT_AHEAD

        @pl.when(ahead < n_all)
        def _():
            in_copy(ahead, ahead % EXPERT_SLOTS).start()

        @pl.when(b >= EXPERT_SLOTS)
        def _():
            out_copy(b - EXPERT_SLOTS, b % EXPERT_SLOTS).wait()

    def compute(b):
        slot = b % EXPERT_SLOTS
        x_lo, x_hi = _unpack_halves(xbuf[slot])
        x = jnp.concatenate([x_lo.astype(BF16), x_hi.astype(BF16)], axis=1)
        h = jnp.dot(x, wgu_ref[0], preferred_element_type=F32)
        hg = h[:, :D_EXPERT]
        act = hg * jax.nn.sigmoid(hg) * h[:, D_EXPERT:]
        ybuf[slot] = _pack_halves(jnp.dot(act.astype(BF16), wd_ref[0], preferred_element_type=F32))

    def run(blocks):
        for b in blocks:
            admit(b)
        for b in blocks:
            compute(b)
        for b in blocks:
            out_copy(b, b % EXPERT_SLOTS).start(priority=1)

    def pair(i, carry):
        b = b0 + 2 * i
        run([b, b + 1])
        return carry

    lax.fori_loop(0, n // 2, pair, 0)

    @pl.when(n % 2 == 1)
    def _():
        run([b0 + n - 1])

    @pl.when(e == N_EXPERTS - 1)
    def _():
        for j in range(1, EXPERT_SLOTS + 1):
            @pl.when(n_all >= j)
            def _():
                out_copy(n_all - j, (n_all - j) % EXPERT_SLOTS).wait()


def _experts(blk0, nblk, xs, we_gu, we_d):
    any_spec = pl.BlockSpec(memory_space=pl.ANY)
    grid_spec = pltpu.PrefetchScalarGridSpec(
        num_scalar_prefetch=2,
        grid=(N_EXPERTS,),
        in_specs=[any_spec,
                  pl.BlockSpec((1, D_MODEL, 2 * D_EXPERT), lambda e, b0, nb: (e, 0, 0)),
                  pl.BlockSpec((1, D_EXPERT, D_MODEL), lambda e, b0, nb: (e, 0, 0))],
        out_specs=any_spec,
        scratch_shapes=[pltpu.VMEM((EXPERT_SLOTS, ROW_BLOCK, D_MODEL // 2), jnp.uint32),
                        pltpu.VMEM((EXPERT_SLOTS, ROW_BLOCK, D_MODEL // 2), jnp.uint32),
                        pltpu.SemaphoreType.DMA((EXPERT_SLOTS,)),
                        pltpu.SemaphoreType.DMA((EXPERT_SLOTS,))],
    )
    return pl.pallas_call(
        _expert_kernel,
        grid_spec=grid_spec,
        out_shape=jax.ShapeDtypeStruct((N_ROWS, D_MODEL // 2), jnp.uint32),
        compiler_params=_cparams(("arbitrary",)),
        name="routed_experts",
    )(blk0, nblk, xs, we_gu, we_d)


def _final_kernel(h2_ref, yg_ref, gw_ref, x1_ref, mod_ref, wgu_ref, wd_ref, g_ref, *rest):
    o_ref = rest[-1]
    half = D_MODEL // 2
    gt_f = mod_ref[0, 5:6, :]
    x_lo, x_hi = _unpack_halves(h2_ref[...])
    h = (jnp.dot(x_lo.astype(BF16), wgu_ref[:half, :], preferred_element_type=F32)
         + jnp.dot(x_hi.astype(BF16), wgu_ref[half:, :], preferred_element_type=F32))
    hg = h[:, :D_EXPERT]
    act = hg * jax.nn.sigmoid(hg) * h[:, D_EXPERT:]
    shared = jnp.dot(act.astype(BF16), wd_ref[...], preferred_element_type=F32)
    y_lo = shared[:, :half]
    y_hi = shared[:, half:]
    for k in range(TOP_K):
        r_lo, r_hi = _unpack_halves(yg_ref[k])
        w = gw_ref[:, k:k + 1]
        y_lo = y_lo + w * r_lo
        y_hi = y_hi + w * r_hi
    ms = (jnp.sum(y_lo * y_lo, axis=-1, keepdims=True)
          + jnp.sum(y_hi * y_hi, axis=-1, keepdims=True)) * (1.0 / D_MODEL)
    inv = lax.rsqrt(ms + NORM_EPS)
    o_ref[:, :half] = x1_ref[:, :half] + gt_f[:, :half] * (y_lo * inv * g_ref[:, :half])
    o_ref[:, half:] = x1_ref[:, half:] + gt_f[:, half:] * (y_hi * inv * g_ref[:, half:])


def _final(half, out_prev, h2p, yg, gw_t, x1, mod, ws_gu, ws_d, g_post_ffn):
    tm = TM_OUT
    t0 = half * (HALF_TOK // tm)
    tok = pl.BlockSpec((tm, D_MODEL), lambda i: (i, 0))
    in_specs = [pl.BlockSpec((tm, D_MODEL // 2), lambda i: (i, 0)),
                pl.BlockSpec((TOP_K, tm, D_MODEL // 2), lambda i: (0, i, 0)),
                pl.BlockSpec((tm, TOP_K), lambda i: (i, 0)),
                tok,
                pl.BlockSpec((1, 8, D_MODEL), lambda i: (half, 0, 0)),
                pl.BlockSpec((D_MODEL, 2 * D_EXPERT), lambda i: (0, 0)),
                pl.BlockSpec((D_EXPERT, D_MODEL), lambda i: (0, 0)),
                pl.BlockSpec((1, D_MODEL), lambda i: (0, 0))]
    args = [h2p, yg, gw_t, x1, mod, ws_gu, ws_d, g_post_ffn]
    aliases = {}
    if out_prev is not None:
        aliases = {len(args): 0}
        in_specs.append(pl.BlockSpec(memory_space=pl.ANY))
        args.append(out_prev)
    return pl.pallas_call(
        _final_kernel,
        grid=(HALF_TOK // tm,),
        in_specs=in_specs,
        out_specs=pl.BlockSpec((tm, D_MODEL), lambda i: (t0 + i, 0)),
        out_shape=jax.ShapeDtypeStruct((N_TOK, D_MODEL), F32),
        input_output_aliases=aliases,
        compiler_params=_cparams(("parallel",)),
        name="shared_final",
    )(*args)


def _sc_worker_id():
    return lax.axis_index("s") * SC_CORES + lax.axis_index("c")


def _dispatch_body(h_hbm, dest_hbm, xs_hbm, idx_v, rows_v, sem_l, sem_s):
    n = SC_CHUNKS_PER_WORKER
    c0 = _sc_worker_id() * n

    def load(i, b):
        return pltpu.async_copy(h_hbm.at[pl.ds((c0 + i) * SC_W, SC_W)], rows_v.at[b], sem_l.at[b])

    loads = [None] * n
    scat = [None] * n
    loads[0] = load(0, 0)
    for i in range(n):
        b = i % 2
        pltpu.sync_copy(dest_hbm.at[c0 + i], idx_v.at[b])
        loads[i].wait()
        if i + 1 < n:
            if i >= 1:
                for d in scat[i - 1]:
                    d.wait()
            loads[i + 1] = load(i + 1, 1 - b)
        scat[i] = [pltpu.async_copy(rows_v.at[b], xs_hbm.at[idx_v.at[b].at[k]], sem_s.at[b])
                   for k in range(TOP_K)]
    for i in (n - 2, n - 1):
        for d in scat[i]:
            d.wait()


def _sc_dispatch(h2p, dest3):
    mesh = plsc.VectorSubcoreMesh(core_axis_name="c", subcore_axis_name="s")
    return pl.kernel(
        _dispatch_body, mesh=mesh,
        out_type=jax.ShapeDtypeStruct((N_ROWS, D_MODEL // 2), jnp.uint32),
        scratch_types=[pltpu.VMEM((2, TOP_K, SC_W), jnp.int32),
                       pltpu.VMEM((2, SC_W, D_MODEL // 2), jnp.uint32),
                       pltpu.SemaphoreType.DMA((2,)), pltpu.SemaphoreType.DMA((2,))],
    )(h2p, dest3)


def _combine_body(ys_hbm, dest_hbm, yg_hbm, idx_v, rows_v, sem_g, sem_w):
    c0 = _sc_worker_id() * SC_CHUNKS_PER_WORKER

    @pl.loop(0, SC_CHUNKS_PER_WORKER)
    def _(i):
        c = c0 + i
        pltpu.sync_copy(dest_hbm.at[c], idx_v)
        g = [None] * TOP_K
        w = [None] * TOP_K
        g[0] = pltpu.async_copy(ys_hbm.at[idx_v.at[0]], rows_v.at[0], sem_g.at[0])
        for k in range(TOP_K):
            b = k % 2
            g[k].wait()
            if k + 1 < TOP_K:
                if k >= 1:
                    w[k - 1].wait()
                g[k + 1] = pltpu.async_copy(ys_hbm.at[idx_v.at[k + 1]], rows_v.at[1 - b], sem_g.at[1 - b])
            w[k] = pltpu.async_copy(rows_v.at[b], yg_hbm.at[k].at[pl.ds(c * SC_W, SC_W)], sem_w.at[b])
        w[TOP_K - 2].wait()
        w[TOP_K - 1].wait()


def _sc_combine(ysp, dest3):
    mesh = plsc.VectorSubcoreMesh(core_axis_name="c", subcore_axis_name="s")
    return pl.kernel(
        _combine_body, mesh=mesh,
        out_type=jax.ShapeDtypeStruct((TOP_K, HALF_TOK, D_MODEL // 2), jnp.uint32),
        scratch_types=[pltpu.VMEM((TOP_K, SC_W), jnp.int32),
                       pltpu.VMEM((2, SC_W, D_MODEL // 2), jnp.uint32),
                       pltpu.SemaphoreType.DMA((2,)), pltpu.SemaphoreType.DMA((2,))],
    )(ysp, dest3)


def kernel(x, c, w_ada, b_ada, g_pre_mix, g_post_mix, w_in, conv_w, conv_b, conv_ln_g, conv_ln_b,
           ssm_a_re, ssm_a_im, ssm_log_dt, ssm_b_re, ssm_b_im, ssm_c_re, ssm_c_im, ssm_d,
           ssm_w_glu, ssm_b_glu, g_out_conv, g_out_ssm, w_out, g_pre_ffn, g_post_ffn,
           w_router, b_router, we_gate, we_up, we_down, ws_gate, ws_up, ws_down):
    l = 0
    x2 = x.reshape(N_TOK, D_MODEL)
    r1 = lambda a: a.reshape(1, -1)

    c_pad = jnp.zeros((8, D_MODEL), F32).at[:BATCH].set(c)
    mod = _ada(c_pad, w_ada[l], r1(b_ada[l]))[:BATCH].reshape(BATCH, 6, D_MODEL)
    mod = jnp.concatenate([mod, jnp.zeros((BATCH, 2, D_MODEL), F32)], axis=1)

    v, u, ut = _inproj(x2, mod, r1(g_pre_mix[l]), w_in[l].astype(BF16))
    cw = jnp.concatenate([conv_w[l].reshape(CONV_WIDTH, CONV_CH), jnp.zeros((1, CONV_CH), F32)], axis=0)
    a_n, we_gu, we_d = _conv(v.reshape(BATCH, SEQ, CONV_CH), cw, r1(conv_b[l]), r1(conv_ln_g[l]),
                             r1(conv_ln_b[l]), r1(g_out_conv[l]), we_gate[l], we_up[l], we_down[l])
    a_n = a_n.reshape(N_TOK, CONV_CH)

    pwr, pwi, s5_params, a_cat, b_q, b_s = _s5_operators(
        ssm_a_re[l], ssm_a_im[l], ssm_log_dt[l], ssm_b_re[l], ssm_b_im[l], ssm_c_re[l], ssm_c_im[l])
    yt = _s5(ut, pwr, pwi, s5_params, a_cat, b_q, b_s)

    tm = TM_MIX
    tri = (jnp.arange(tm)[:, None] < jnp.arange(tm)[None, :]).astype(BF16)
    wo = w_out[l].astype(BF16)
    wr_t = w_router[l].T
    wr_hi = wr_t.astype(BF16)
    wr_split = jnp.concatenate([wr_hi, (wr_t - wr_hi.astype(F32)).astype(BF16)], axis=0)
    mix_params = (r1(ssm_d[l]), ssm_w_glu[l].astype(BF16), r1(ssm_b_glu[l]), r1(g_out_ssm[l]),
                  wo[:CONV_CH], wo[CONV_CH:], r1(g_post_mix[l]), r1(g_pre_ffn[l]),
                  wr_split, b_router[l].reshape(N_EXPERTS, 1), tri)
    ws_gu = jnp.concatenate([ws_gate[l], ws_up[l]], axis=1).astype(BF16)
    ws_d = ws_down[l].astype(BF16)
    e_ids = jnp.arange(N_EXPERTS, dtype=jnp.int32)

    out = None
    for half in range(N_HALVES):
        x1, h2, eidx, rank, gw, cnt = _mix(half, a_n, yt, u, x2, mod, *mix_params)
        counts = cnt[:, 0].astype(jnp.int32)
        padded = (counts + ROW_BLOCK - 1) // ROW_BLOCK * ROW_BLOCK
        pstart = jnp.cumsum(padded) - padded
        dest = rank + jnp.sum(jnp.where(eidx[..., None] == e_ids, pstart, 0), axis=-1)
        dest3 = dest.reshape(TOP_K, HALF_TOK // SC_W, SC_W).transpose(1, 0, 2)

        xs = _sc_dispatch(h2, dest3)
        ys = _experts(pstart // ROW_BLOCK, padded // ROW_BLOCK, xs, we_gu, we_d)
        yg = _sc_combine(ys, dest3)
        out = _final(half, out, h2, yg, gw.T, x1, mod, ws_gu, ws_d, r1(g_post_ffn[l]))
    return out.reshape(BATCH, SEQ, D_MODEL)
```

```python
import functools
import math

import jax
import jax.numpy as jnp
from jax import lax
from jax.experimental import pallas as pl
from jax.experimental.pallas import tpu as pltpu
from jax.experimental.pallas import tpu_sc as plsc

F32 = jnp.float32
BF16 = jnp.bfloat16

D_MODEL = 1024
BATCH = 2
SEQ = 8192
N_TOK = BATCH * SEQ
CONV_CH = 512
CONV_WIDTH = 31
SSM_CH = 512
SSM_GROUP_CH = 16
SSM_GROUPS = 32
SSM_STATE = 64
D_IN = 2 * CONV_CH + SSM_CH
N_EXPERTS = 64
TOP_K = 8
N_ROUTE_GROUPS = 8
TOPK_ROUTE_GROUPS = 4
D_EXPERT = 256
ROUTED_SCALE = 2.5
NORM_EPS = 1e-6

SUBLANES = 8
LANES = 128

TM_IN = 512
IN_SUBTILES = 2
TL_CONV = 512
CONV_HALO = 32
CONV_ROWS = 64
EXPERTS_PER_CONV_STEP = N_EXPERTS * TL_CONV // N_TOK
assert EXPERTS_PER_CONV_STEP * N_TOK == N_EXPERTS * TL_CONV
S5_Q = 32
S5_QH = S5_Q * SSM_GROUP_CH
S5_CHUNKS = N_TOK // S5_Q
S5_CHUNKS_PER_SEQ = SEQ // S5_Q
TM_MIX = 512
ROW_BLOCK = 512
EXPERT_AHEAD = 4
EXPERT_SLOTS = 6
HALF_TOK = SEQ
N_HALVES = N_TOK // HALF_TOK
N_BLOCKS = HALF_TOK * TOP_K // ROW_BLOCK + N_EXPERTS
N_ROWS = N_BLOCKS * ROW_BLOCK
TM_OUT = 512
SC_CORES = 2
SC_SUBCORES = 16
SC_WORKERS = SC_CORES * SC_SUBCORES
SC_W = 64
SC_CHUNKS_PER_WORKER = HALF_TOK // (SC_WORKERS * SC_W)
VMEM_LIMIT = 48 * 1024 * 1024


def _cparams(sem):
    return pltpu.CompilerParams(dimension_semantics=sem, vmem_limit_bytes=VMEM_LIMIT)


def _pack_rounded_halves(xr):
    n = xr.shape[-1] // 2
    lo = lax.bitcast_convert_type(xr[:, :n], jnp.uint32)
    hi = lax.bitcast_convert_type(xr[:, n:], jnp.uint32)
    return hi | (lo >> 16)


def _pack_halves(x):
    return _pack_rounded_halves(x.astype(BF16).astype(F32))


def _unpack_halves(p):
    lo = lax.bitcast_convert_type(p << 16, F32)
    hi = lax.bitcast_convert_type(p & jnp.uint32(0xFFFF0000), F32)
    return lo, hi


def _rms(x, g):
    return x * lax.rsqrt(jnp.mean(x * x, axis=-1, keepdims=True) + NORM_EPS) * g


def _split_bf16(x):
    hi = x.astype(BF16)
    return hi, (x - hi.astype(F32)).astype(BF16)


def _dot_nt_split(a, b):
    nt = (((1,), (1,)), ((), ()))
    a_hi, a_lo = _split_bf16(a)
    b_hi, b_lo = _split_bf16(b)
    m = a.shape[0]
    both = lax.dot_general(jnp.concatenate([a_hi, a_lo], axis=0), b_hi, nt, preferred_element_type=F32)
    return both[:m] + both[m:] + lax.dot_general(a_hi, b_lo, nt, preferred_element_type=F32)


def _ada_kernel(c_ref, w_ref, b_ref, o_ref):
    c = c_ref[...]
    a = c * jax.nn.sigmoid(c)
    o_ref[...] = jnp.dot(a, w_ref[...], preferred_element_type=F32,
                         precision=lax.Precision.HIGHEST) + b_ref[...]


def _ada(c_pad, w_ada, b_ada):
    n = w_ada.shape[1]
    bn = 1536
    return pl.pallas_call(
        _ada_kernel,
        grid=(n // bn,),
        in_specs=[pl.BlockSpec((8, D_MODEL), lambda j: (0, 0)),
                  pl.BlockSpec((D_MODEL, bn), lambda j: (0, j)),
                  pl.BlockSpec((1, bn), lambda j: (0, j))],
        out_specs=pl.BlockSpec((8, bn), lambda j: (0, j)),
        out_shape=jax.ShapeDtypeStruct((8, n), F32),
        compiler_params=_cparams(("arbitrary",)),
        name="ada_mod",
    )(c_pad, w_ada, b_ada)


GROUPS_PER_LANE_TILE = LANES // SSM_GROUP_CH


def _to_group_chunks(u, tile_ref, ut_ref):
    n_chunks = u.shape[0] // S5_Q
    for j in range(SSM_CH // LANES):
        tile_ref[j] = u[:, LANES * j:LANES * (j + 1)]
    for j in range(SSM_CH // LANES):
        rows_t = [tile_ref[j, pl.ds(t, n_chunks, stride=S5_Q), :] for t in range(S5_Q)]
        for gg in range(GROUPS_PER_LANE_TILE):
            lo = gg * SSM_GROUP_CH
            row = jnp.concatenate([r[:, lo:lo + SSM_GROUP_CH] for r in rows_t], axis=1)
            ut_ref[j * GROUPS_PER_LANE_TILE + gg] = row.astype(ut_ref.dtype)


def _from_group_chunks(yt_ref, tile_ref):
    n_chunks = yt_ref.shape[1]
    for j in range(SSM_CH // LANES):
        for t in range(S5_Q):
            lo = t * SSM_GROUP_CH
            piece = jnp.concatenate(
                [yt_ref[j * GROUPS_PER_LANE_TILE + gg, :, lo:lo + SSM_GROUP_CH]
                 for gg in range(GROUPS_PER_LANE_TILE)], axis=1)
            tile_ref[j, pl.ds(t, n_chunks, stride=S5_Q), :] = piece
    return jnp.concatenate([tile_ref[j] for j in range(SSM_CH // LANES)], axis=1)


def _inproj_kernel(x_ref, mod_ref, g_ref, w_ref, v_ref, u_ref, ut_ref, tile_ref):
    sh = mod_ref[0, 0:1, :]
    sc = mod_ref[0, 1:2, :]
    sub = TM_IN // IN_SUBTILES
    sub_chunks = sub // S5_Q
    for s in range(IN_SUBTILES):
        r = slice(s * sub, (s + 1) * sub)
        h = _rms(x_ref[r, :], g_ref[...]) * (1.0 + sc) + sh
        z = jnp.dot(h.astype(BF16), w_ref[...], preferred_element_type=F32)
        v_ref[r, :] = z[:, :CONV_CH] * jax.nn.sigmoid(z[:, CONV_CH:2 * CONV_CH])
        u = z[:, 2 * CONV_CH:]
        u_ref[r, :] = u
        _to_group_chunks(u, tile_ref.at[s], ut_ref.at[:, s * sub_chunks:(s + 1) * sub_chunks, :])


def _inproj(x2, mod, g_pre, w_in_bf):
    tiles_per_seq = SEQ // TM_IN
    return pl.pallas_call(
        _inproj_kernel,
        grid=(N_TOK // TM_IN,),
        in_specs=[pl.BlockSpec((TM_IN, D_MODEL), lambda i: (i, 0)),
                  pl.BlockSpec((1, 8, D_MODEL), lambda i: (i // tiles_per_seq, 0, 0)),
                  pl.BlockSpec((1, D_MODEL), lambda i: (0, 0)),
                  pl.BlockSpec((D_MODEL, D_IN), lambda i: (0, 0))],
        out_specs=[pl.BlockSpec((TM_IN, CONV_CH), lambda i: (i, 0)),
                   pl.BlockSpec((TM_IN, SSM_CH), lambda i: (i, 0)),
                   pl.BlockSpec((SSM_GROUPS, TM_IN // S5_Q, S5_QH), lambda i: (0, i, 0))],
        out_shape=[jax.ShapeDtypeStruct((N_TOK, CONV_CH), F32),
                   jax.ShapeDtypeStruct((N_TOK, SSM_CH), F32),
                   jax.ShapeDtypeStruct((SSM_GROUPS, S5_CHUNKS, S5_QH), BF16)],
        scratch_shapes=[pltpu.VMEM((IN_SUBTILES, SSM_CH // LANES, TM_IN // IN_SUBTILES, LANES), F32)],
        compiler_params=_cparams(("parallel",)),
        name="in_proj",
    )(x2, mod, g_pre, w_in_bf)


def _conv_kernel(vc_ref, vp_ref, w_ref, cb_ref, lg_ref, lb_ref, go_ref, wg_ref, wu_ref, wd_ref,
                 o_ref, wgu_o, wd_o, sh_ref):
    for q in range(EXPERTS_PER_CONV_STEP):
        wgu_o[q, :, :D_EXPERT] = wg_ref[q].astype(BF16)
        wgu_o[q, :, D_EXPERT:] = wu_ref[q].astype(BF16)
        wd_o[q] = wd_ref[q].astype(BF16)

    i = pl.program_id(1)
    keep = (i > 0).astype(F32)
    n_ext = TL_CONV + CONV_HALO
    sh_ref[0, 0:CONV_HALO, :] = vp_ref[0] * keep
    sh_ref[0, CONV_HALO:, :] = vc_ref[0]
    ext = sh_ref[0]
    for s in range(1, SUBLANES):
        sh_ref[s] = pltpu.roll(ext, n_ext - s, axis=0)
    off = CONV_HALO - (CONV_WIDTH - 1)
    for r in range(TL_CONV // CONV_ROWS):
        acc = None
        for j in range(CONV_WIDTH):
            s = (off + j) % SUBLANES
            al = r * CONV_ROWS + (off + j) - s
            term = w_ref[j:j + 1, :] * sh_ref[s, al:al + CONV_ROWS, :]
            acc = term if acc is None else acc + term
        y = acc + cb_ref[...]
        mu = jnp.mean(y, axis=-1, keepdims=True)
        d = y - mu
        var = jnp.mean(d * d, axis=-1, keepdims=True)
        yn = d * lax.rsqrt(var + NORM_EPS) * lg_ref[...] + lb_ref[...]
        a = yn * jax.nn.sigmoid(yn)
        o_ref[0, r * CONV_ROWS:(r + 1) * CONV_ROWS, :] = _rms(a, go_ref[...]).astype(BF16)


def _conv(v3, conv_w, conv_b, ln_g, ln_b, g_out, we_gate, we_up, we_down):
    halo_per_tile = TL_CONV // CONV_HALO
    steps_per_seq = SEQ // TL_CONV
    vec = pl.BlockSpec((1, CONV_CH), lambda b, i: (0, 0))
    ex = EXPERTS_PER_CONV_STEP
    w_in = pl.BlockSpec((ex, D_MODEL, D_EXPERT), lambda b, i: (b * steps_per_seq + i, 0, 0))
    return pl.pallas_call(
        _conv_kernel,
        grid=(BATCH, steps_per_seq),
        in_specs=[pl.BlockSpec((1, TL_CONV, CONV_CH), lambda b, i: (b, i, 0)),
                  pl.BlockSpec((1, CONV_HALO, CONV_CH),
                               lambda b, i: (b, jnp.maximum(i * halo_per_tile - 1, 0), 0)),
                  pl.BlockSpec((CONV_WIDTH + 1, CONV_CH), lambda b, i: (0, 0)),
                  vec, vec, vec, vec,
                  w_in, w_in,
                  pl.BlockSpec((ex, D_EXPERT, D_MODEL), lambda b, i: (b * steps_per_seq + i, 0, 0))],
        out_specs=[pl.BlockSpec((1, TL_CONV, CONV_CH), lambda b, i: (b, i, 0)),
                   pl.BlockSpec((ex, D_MODEL, 2 * D_EXPERT), lambda b, i: (b * steps_per_seq + i, 0, 0)),
                   pl.BlockSpec((ex, D_EXPERT, D_MODEL), lambda b, i: (b * steps_per_seq + i, 0, 0))],
        out_shape=[jax.ShapeDtypeStruct((BATCH, SEQ, CONV_CH), BF16),
                   jax.ShapeDtypeStruct((N_EXPERTS, D_MODEL, 2 * D_EXPERT), BF16),
                   jax.ShapeDtypeStruct((N_EXPERTS, D_EXPERT, D_MODEL), BF16)],
        scratch_shapes=[pltpu.VMEM((SUBLANES, TL_CONV + CONV_HALO, CONV_CH), F32)],
        compiler_params=_cparams(("parallel", "arbitrary")),
        name="conv_module",
    )(v3, v3, conv_w, conv_b, ln_g, ln_b, g_out, we_gate, we_up, we_down)


S5_GROUP_ROWS = S5_CHUNKS + 8


S5_POW_ROWS = (S5_Q + 1 + SUBLANES - 1) // SUBLANES * SUBLANES
(S5_BB_RI, S5_BB_NIR, S5_BB_IR, S5_BB_RNI, S5_CC_RI, S5_CC_NIR, S5_N_PARAM) = range(7)


def _s5_kernel(ut_ref, pwr_ref, pwi_ref, par_ref, a_ref, bq_ref, bs_ref, yt_ref, sin_s, sp_s):
    phase = pl.program_id(0)
    g = pl.program_id(1)
    q = S5_Q
    n = 2 * SSM_STATE
    row0 = pl.multiple_of(g * S5_GROUP_ROWS, 8)

    def lam_pow(j):
        return pwr_ref[0, j:j + 1, :], pwi_ref[0, j:j + 1, :]

    @pl.when(phase == 0)
    def _():
        bb_ri, bb_nir = par_ref[0, S5_BB_RI], par_ref[0, S5_BB_NIR]
        bb_ir, bb_rni = par_ref[0, S5_BB_IR], par_ref[0, S5_BB_RNI]
        blk_q, blk_s = [], []
        for t in range(q):
            pr, pi_ = lam_pow(q - 1 - t)
            blk_q.append(pr * bb_ri + pi_ * bb_nir)
            blk_s.append(pr * bb_ir + pi_ * bb_rni)
        wst = jnp.concatenate([jnp.concatenate(blk_q, axis=0), jnp.concatenate(blk_s, axis=0)], axis=1)
        r = jnp.dot(ut_ref[0], wst.astype(BF16), preferred_element_type=F32)
        sin_s[0, pl.ds(row0, S5_CHUNKS), :] = r[:, :n]
        sin_s[1, pl.ds(row0, S5_CHUNKS), :] = r[:, n:]

    @pl.when((phase == 1) & (g == 0))
    def _():
        a = a_ref[...]
        bq = bq_ref[...]
        bs = bs_ref[...]

        def body(c, carry):
            nxt = []
            for b in range(BATCH):
                x, xs = carry[b]
                rows = pl.ds(b * S5_CHUNKS_PER_SEQ + c, SSM_GROUPS, stride=S5_GROUP_ROWS)
                sp_s[rows, :] = x
                nxt.append((a * x + bq * xs + sin_s[0, rows, :], a * xs + bs * x + sin_s[1, rows, :]))
            return tuple(nxt)

        z = jnp.zeros((SSM_GROUPS, n), F32)
        lax.fori_loop(0, S5_CHUNKS_PER_SEQ, body, tuple((z, z) for _ in range(BATCH)))

    @pl.when(phase == 1)
    def _():
        cc_ri, cc_nir = par_ref[0, S5_CC_RI], par_ref[0, S5_CC_NIR]
        cl = []
        for j in range(q + 1):
            pr, pi_ = lam_pow(j)
            cl.append(pr * cc_ri + pi_ * cc_nir)
        cl_lo = jnp.concatenate(cl[:q], axis=0)
        cl_hi = jnp.concatenate(cl[1:], axis=0)
        lane = lax.broadcasted_iota(jnp.int32, (1, n), 1)
        vgt = (cl_hi * jnp.where(lane < SSM_STATE, 1.0, -1.0)).astype(BF16)
        kt = _dot_nt_split(par_ref[0, S5_BB_RNI], cl_lo)
        padded = jnp.concatenate([jnp.zeros_like(kt), kt], axis=1)
        tg = jnp.concatenate(
            [padded[:, (q - t) * SSM_GROUP_CH:(q - t) * SSM_GROUP_CH + S5_QH] for t in range(q)],
            axis=0).astype(BF16)
        sp = sp_s[pl.ds(row0, S5_CHUNKS), :]
        y = jnp.dot(ut_ref[0], tg, preferred_element_type=F32)
        yt_ref[0] = y + lax.dot_general(sp.astype(BF16), vgt, (((1,), (1,)), ((), ())),
                                        preferred_element_type=F32)


def _s5(ut, pwr, pwi, params, a_cat, b_q, b_s):
    vec = pl.BlockSpec((SSM_GROUPS, 2 * SSM_STATE), lambda p, g: (0, 0))
    powers = pl.BlockSpec((1, S5_POW_ROWS, 2 * SSM_STATE), lambda p, g: (g, 0, 0))
    return pl.pallas_call(
        _s5_kernel,
        grid=(2, SSM_GROUPS),
        in_specs=[pl.BlockSpec((1, S5_CHUNKS, S5_QH), lambda p, g: (g, 0, 0)),
                  powers, powers,
                  pl.BlockSpec((1, S5_N_PARAM, SSM_GROUP_CH, 2 * SSM_STATE), lambda p, g: (g, 0, 0, 0)),
                  vec, vec, vec],
        out_specs=pl.BlockSpec((1, S5_CHUNKS, S5_QH), lambda p, g: (g * p, 0, 0)),
        out_shape=jax.ShapeDtypeStruct((SSM_GROUPS, S5_CHUNKS, S5_QH), F32),
        scratch_shapes=[pltpu.VMEM((2, SSM_GROUPS * S5_GROUP_ROWS, 2 * SSM_STATE), F32),
                        pltpu.VMEM((SSM_GROUPS * S5_GROUP_ROWS, 2 * SSM_STATE), F32)],
        compiler_params=_cparams(("arbitrary", "arbitrary")),
        name="s5_chunked",
    )(ut, pwr, pwi, params, a_cat, b_q, b_s)


def _s5_operators(a_re, a_im, log_dt, b_re, b_im, c_re, c_im):
    q = S5_Q
    dt = jnp.exp(log_dt)[:, None]
    ar, ai = a_re, a_im
    mag = jnp.exp(ar * dt)
    lr = mag * jnp.cos(ai * dt)
    li = mag * jnp.sin(ai * dt)
    den = ar * ar + ai * ai
    nr = lr - 1.0
    kr = (nr * ar + li * ai) / den
    ki = (li * ar - nr * ai) / den
    bbr = kr[..., None] * b_re - ki[..., None] * b_im
    bbi = kr[..., None] * b_im + ki[..., None] * b_re
    j = jnp.arange(q + 1, dtype=F32)[None, :, None]
    pmag = jnp.exp(ar[:, None, :] * dt[:, :, None] * j)
    pang = ai[:, None, :] * dt[:, :, None] * j
    pr = pmag * jnp.cos(pang)
    pi_ = pmag * jnp.sin(pang)
    pad = ((0, 0), (0, S5_POW_ROWS - (q + 1)), (0, 0))
    pwr = jnp.pad(jnp.concatenate([pr, pr], axis=-1), pad)
    pwi = jnp.pad(jnp.concatenate([pi_, pi_], axis=-1), pad)
    br_t = bbr.transpose(0, 2, 1)
    bi_t = bbi.transpose(0, 2, 1)
    cat = lambda a, b: jnp.concatenate([a, b], axis=-1)
    stack = [None] * S5_N_PARAM
    stack[S5_BB_RI] = cat(br_t, bi_t)
    stack[S5_BB_NIR] = cat(-bi_t, br_t)
    stack[S5_BB_IR] = cat(bi_t, br_t)
    stack[S5_BB_RNI] = cat(br_t, -bi_t)
    stack[S5_CC_RI] = cat(c_re, c_im)
    stack[S5_CC_NIR] = cat(-c_im, c_re)
    params = jnp.stack(stack, axis=1)
    aq_r, aq_i = pr[:, q], pi_[:, q]
    a_cat = cat(aq_r, aq_r)
    b_q = cat(-aq_i, aq_i)
    b_s = cat(aq_i, -aq_i)
    return pwr, pwi, params, a_cat, b_q, b_s


def _gelu_tanh(x):
    return 0.5 * x * (1.0 + jnp.tanh(math.sqrt(2.0 / math.pi) * (x + 0.044715 * (x * x * x))))


def _mix_kernel(an_ref, yt_ref, u_ref, x_ref, mod_ref, d_ref, wglu_ref, bglu_ref, gos_ref,
                woa_ref, wob_ref, gpm_ref, gpf_ref, wr_ref, br_ref, tri_ref,
                x1_ref, h2_ref, eidx_ref, rank_ref, gw_ref, cnt_ref, run_ref, tile_ref):
    i = pl.program_id(0)
    tm = TM_MIX

    @pl.when(i == 0)
    def _():
        run_ref[...] = jnp.zeros_like(run_ref)

    gt_m = mod_ref[0, 2:3, :]
    sh_f = mod_ref[0, 3:4, :]
    sc_f = mod_ref[0, 4:5, :]

    yy = _from_group_chunks(yt_ref, tile_ref) + d_ref[...] * u_ref[...]
    g = _gelu_tanh(yy)
    gl = jnp.dot(g.astype(BF16), wglu_ref[...], preferred_element_type=F32) + bglu_ref[...]
    ob = g * jax.nn.sigmoid(gl)
    bn = _rms(ob, gos_ref[...]).astype(BF16)
    o = (jnp.dot(an_ref[...], woa_ref[...], preferred_element_type=F32)
         + jnp.dot(bn, wob_ref[...], preferred_element_type=F32))
    x1 = x_ref[...] + gt_m * _rms(o, gpm_ref[...])
    x1_ref[...] = x1
    h2 = _rms(x1, gpf_ref[...]) * (1.0 + sc_f) + sh_f
    h2_hi = h2.astype(BF16)
    h2_hi32 = h2_hi.astype(F32)
    h2_ref[...] = _pack_rounded_halves(h2_hi32)

    h2_lo = (h2 - h2_hi32).astype(BF16)
    nt = (((1,), (1,)), ((), ()))
    both = lax.dot_general(wr_ref[...], h2_hi, nt, preferred_element_type=F32)
    logits = (both[:N_EXPERTS] + both[N_EXPERTS:]
              + lax.dot_general(wr_ref[:N_EXPERTS, :], h2_lo, nt, preferred_element_type=F32))
    scores = jax.nn.sigmoid(logits)
    biased = scores + br_ref[...]
    ng = N_ROUTE_GROUPS
    gsz = N_EXPERTS // ng
    b3 = biased.reshape(ng, gsz, tm)
    s3 = scores.reshape(ng, gsz, tm)
    sub = lax.broadcasted_iota(jnp.int32, (ng, gsz, tm), 1).astype(F32)
    grp = lax.broadcasted_iota(jnp.int32, (ng, gsz, tm), 0).astype(F32)
    eid = grp * gsz + sub
    neg = -jnp.inf
    m1 = jnp.max(b3, axis=1, keepdims=True)
    i1 = jnp.min(jnp.where(b3 == m1, sub, float(gsz)), axis=1, keepdims=True)
    m2 = jnp.max(jnp.where(sub == i1, neg, b3), axis=1, keepdims=True)
    gs = m1 + m2
    gi = lax.broadcasted_iota(jnp.int32, (ng, 1, tm), 0)
    beaten = jnp.zeros((ng, 1, tm), F32)
    for gp in range(ng):
        o_ = gs[gp:gp + 1]
        beats = (o_ > gs) | ((o_ == gs) & (gi > gp))
        beaten = beaten + beats.astype(F32)
    gmask = beaten < float(TOPK_ROUTE_GROUPS)
    masked = jnp.where(gmask, b3, neg)

    sels = []
    picked = jnp.zeros((ng, gsz, tm), F32)
    for k in range(TOP_K):
        m = jnp.max(jnp.max(masked, axis=0, keepdims=True), axis=1, keepdims=True)
        cand = jnp.where(masked == m, eid, float(N_EXPERTS))
        sel = jnp.min(jnp.min(cand, axis=0, keepdims=True), axis=1, keepdims=True)
        oh = eid == sel
        masked = jnp.where(oh, neg, masked)
        picked = jnp.where(oh, 1.0, picked)
        sels.append(sel)

    pm = picked.reshape(N_EXPERTS, tm)
    prefix = jnp.dot(pm.astype(BF16), tri_ref[...], preferred_element_type=F32) + run_ref[:, 0:1]
    p3 = prefix.reshape(ng, gsz, tm)
    run_new = run_ref[...] + jnp.sum(pm, axis=1, keepdims=True)
    run_ref[...] = run_new
    cnt_ref[...] = run_new

    sc_rows = []
    for k in range(TOP_K):
        oh = eid == sels[k]
        sc_k = jnp.sum(jnp.sum(jnp.where(oh, s3, 0.0), axis=0, keepdims=True), axis=1, keepdims=True)
        rk_k = jnp.sum(jnp.sum(jnp.where(oh, p3, 0.0), axis=0, keepdims=True), axis=1, keepdims=True)
        sc_rows.append(sc_k)
        eidx_ref[k:k + 1, :] = sels[k].reshape(1, tm).astype(jnp.int32)
        rank_ref[k:k + 1, :] = rk_k.reshape(1, tm).astype(jnp.int32)
    tot = sc_rows[0]
    for k in range(1, TOP_K):
        tot = tot + sc_rows[k]
    inv = ROUTED_SCALE / (tot + 1e-20)
    for k in range(TOP_K):
        gw_ref[k:k + 1, :] = (sc_rows[k] * inv).reshape(1, tm)


def _mix(half, a_n, yt, u2, x2, mod, d_skip, wglu_bf, b_glu, g_out_ssm, wo_a, wo_b, g_post_mix,
         g_pre_ffn, w_router_t, b_router_col, tri):
    tm = TM_MIX
    t0 = half * (HALF_TOK // tm)
    row = lambda n: pl.BlockSpec((1, n), lambda i: (0, 0))
    full = lambda a, b: pl.BlockSpec((a, b), lambda i: (0, 0))
    tok_in = lambda n: pl.BlockSpec((tm, n), lambda i: (t0 + i, 0))
    tok = lambda n: pl.BlockSpec((tm, n), lambda i: (i, 0))
    col = pl.BlockSpec((TOP_K, tm), lambda i: (0, i))
    return pl.pallas_call(
        _mix_kernel,
        grid=(HALF_TOK // tm,),
        in_specs=[tok_in(CONV_CH),
                  pl.BlockSpec((SSM_GROUPS, tm // S5_Q, S5_QH), lambda i: (0, t0 + i, 0)),
                  tok_in(SSM_CH), tok_in(D_MODEL),
                  pl.BlockSpec((1, 8, D_MODEL), lambda i: (half, 0, 0)),
                  row(SSM_CH), full(SSM_CH, SSM_CH), row(SSM_CH), row(SSM_CH),
                  full(CONV_CH, D_MODEL), full(SSM_CH, D_MODEL), row(D_MODEL), row(D_MODEL),
                  full(2 * N_EXPERTS, D_MODEL), full(N_EXPERTS, 1), full(tm, tm)],
        out_specs=[tok(D_MODEL), tok(D_MODEL // 2), col, col, col,
                   pl.BlockSpec((N_EXPERTS, 128), lambda i: (0, 0))],
        out_shape=[jax.ShapeDtypeStruct((HALF_TOK, D_MODEL), F32),
                   jax.ShapeDtypeStruct((HALF_TOK, D_MODEL // 2), jnp.uint32),
                   jax.ShapeDtypeStruct((TOP_K, HALF_TOK), jnp.int32),
                   jax.ShapeDtypeStruct((TOP_K, HALF_TOK), jnp.int32),
                   jax.ShapeDtypeStruct((TOP_K, HALF_TOK), F32),
                   jax.ShapeDtypeStruct((N_EXPERTS, 128), F32)],
        scratch_shapes=[pltpu.VMEM((N_EXPERTS, 128), F32),
                        pltpu.VMEM((SSM_CH // LANES, tm, LANES), F32)],
        compiler_params=_cparams(("arbitrary",)),
        name="mix_out_router",
    )(a_n, yt, u2, x2, mod, d_skip, wglu_bf, b_glu, g_out_ssm, wo_a, wo_b, g_post_mix,
      g_pre_ffn, w_router_t, b_router_col, tri)


def _expert_kernel(blk0_ref, nblk_ref, xs_hbm, wgu_ref, wd_ref, ys_hbm, xbuf, ybuf, sem_in, sem_out):
    e = pl.program_id(0)
    n = nblk_ref[e]
    b0 = blk0_ref[e]
    n_all = blk0_ref[N_EXPERTS - 1] + nblk_ref[N_EXPERTS - 1]

    def rows(b):
        return pl.ds(pl.multiple_of(b * ROW_BLOCK, ROW_BLOCK), ROW_BLOCK)

    def in_copy(b, slot):
        return pltpu.make_async_copy(xs_hbm.at[rows(b)], xbuf.at[slot], sem_in.at[slot])

    def out_copy(b, slot):
        return pltpu.make_async_copy(ybuf.at[slot], ys_hbm.at[rows(b)], sem_out.at[slot])

    for b in range(EXPERT_AHEAD):
        @pl.when((e == 0) & (b < n_all))
        def _():
            in_copy(b, b).start()

    def admit(b):
        in_copy(b, b % EXPERT_SLOTS).wait()
        ahead = b + EXPERT_AHEAD

        @pl.when(ahead < n_all)
        def _():
            in_copy(ahead, ahead % EXPERT_SLOTS).start()

        @pl.when(b >= EXPERT_SLOTS)
        def _():
            out_copy(b - EXPERT_SLOTS, b % EXPERT_SLOTS).wait()

    def compute(b):
        slot = b % EXPERT_SLOTS
        x_lo, x_hi = _unpack_halves(xbuf[slot])
        x = jnp.concatenate([x_lo.astype(BF16), x_hi.astype(BF16)], axis=1)
        h = jnp.dot(x, wgu_ref[0], preferred_element_type=F32)
        hg = h[:, :D_EXPERT]
        act = hg * jax.nn.sigmoid(hg) * h[:, D_EXPERT:]
        ybuf[slot] = _pack_halves(jnp.dot(act.astype(BF16), wd_ref[0], preferred_element_type=F32))

    def run(blocks):
        for b in blocks:
            admit(b)
        for b in blocks:
            compute(b)
        for b in blocks:
            out_copy(b, b % EXPERT_SLOTS).start(priority=1)

    def pair(i, carry):
        b = b0 + 2 * i
        run([b, b + 1])
        return carry

    lax.fori_loop(0, n // 2, pair, 0)

    @pl.when(n % 2 == 1)
    def _():
        run([b0 + n - 1])

    @pl.when(e == N_EXPERTS - 1)
    def _():
        for j in range(1, EXPERT_SLOTS + 1):
            @pl.when(n_all >= j)
            def _():
                out_copy(n_all - j, (n_all - j) % EXPERT_SLOTS).wait()


def _experts(blk0, nblk, xs, we_gu, we_d):
    any_spec = pl.BlockSpec(memory_space=pl.ANY)
    grid_spec = pltpu.PrefetchScalarGridSpec(
        num_scalar_prefetch=2,
        grid=(N_EXPERTS,),
        in_specs=[any_spec,
                  pl.BlockSpec((1, D_MODEL, 2 * D_EXPERT), lambda e, b0, nb: (e, 0, 0)),
                  pl.BlockSpec((1, D_EXPERT, D_MODEL), lambda e, b0, nb: (e, 0, 0))],
        out_specs=any_spec,
        scratch_shapes=[pltpu.VMEM((EXPERT_SLOTS, ROW_BLOCK, D_MODEL // 2), jnp.uint32),
                        pltpu.VMEM((EXPERT_SLOTS, ROW_BLOCK, D_MODEL // 2), jnp.uint32),
                        pltpu.SemaphoreType.DMA((EXPERT_SLOTS,)),
                        pltpu.SemaphoreType.DMA((EXPERT_SLOTS,))],
    )
    return pl.pallas_call(
        _expert_kernel,
        grid_spec=grid_spec,
        out_shape=jax.ShapeDtypeStruct((N_ROWS, D_MODEL // 2), jnp.uint32),
        compiler_params=_cparams(("arbitrary",)),
        name="routed_experts",
    )(blk0, nblk, xs, we_gu, we_d)


def _final_kernel(h2_ref, yg_ref, gw_ref, x1_ref, mod_ref, wgu_ref, wd_ref, g_ref, *rest):
    o_ref = rest[-1]
    half = D_MODEL // 2
    gt_f = mod_ref[0, 5:6, :]
    x_lo, x_hi = _unpack_halves(h2_ref[...])
    h = (jnp.dot(x_lo.astype(BF16), wgu_ref[:half, :], preferred_element_type=F32)
         + jnp.dot(x_hi.astype(BF16), wgu_ref[half:, :], preferred_element_type=F32))
    hg = h[:, :D_EXPERT]
    act = hg * jax.nn.sigmoid(hg) * h[:, D_EXPERT:]
    shared = jnp.dot(act.astype(BF16), wd_ref[...], preferred_element_type=F32)
    y_lo = shared[:, :half]
    y_hi = shared[:, half:]
    for k in range(TOP_K):
        r_lo, r_hi = _unpack_halves(yg_ref[k])
        w = gw_ref[:, k:k + 1]
        y_lo = y_lo + w * r_lo
        y_hi = y_hi + w * r_hi
    ms = (jnp.sum(y_lo * y_lo, axis=-1, keepdims=True)
          + jnp.sum(y_hi * y_hi, axis=-1, keepdims=True)) * (1.0 / D_MODEL)
    inv = lax.rsqrt(ms + NORM_EPS)
    o_ref[:, :half] = x1_ref[:, :half] + gt_f[:, :half] * (y_lo * inv * g_ref[:, :half])
    o_ref[:, half:] = x1_ref[:, half:] + gt_f[:, half:] * (y_hi * inv * g_ref[:, half:])


def _final(half, out_prev, h2p, yg, gw_t, x1, mod, ws_gu, ws_d, g_post_ffn):
    tm = TM_OUT
    t0 = half * (HALF_TOK // tm)
    tok = pl.BlockSpec((tm, D_MODEL), lambda i: (i, 0))
    in_specs = [pl.BlockSpec((tm, D_MODEL // 2), lambda i: (i, 0)),
                pl.BlockSpec((TOP_K, tm, D_MODEL // 2), lambda i: (0, i, 0)),
                pl.BlockSpec((tm, TOP_K), lambda i: (i, 0)),
                tok,
                pl.BlockSpec((1, 8, D_MODEL), lambda i: (half, 0, 0)),
                pl.BlockSpec((D_MODEL, 2 * D_EXPERT), lambda i: (0, 0)),
                pl.BlockSpec((D_EXPERT, D_MODEL), lambda i: (0, 0)),
                pl.BlockSpec((1, D_MODEL), lambda i: (0, 0))]
    args = [h2p, yg, gw_t, x1, mod, ws_gu, ws_d, g_post_ffn]
    aliases = {}
    if out_prev is not None:
        aliases = {len(args): 0}
        in_specs.append(pl.BlockSpec(memory_space=pl.ANY))
        args.append(out_prev)
    return pl.pallas_call(
        _final_kernel,
        grid=(HALF_TOK // tm,),
        in_specs=in_specs,
        out_specs=pl.BlockSpec((tm, D_MODEL), lambda i: (t0 + i, 0)),
        out_shape=jax.ShapeDtypeStruct((N_TOK, D_MODEL), F32),
        input_output_aliases=aliases,
        compiler_params=_cparams(("parallel",)),
        name="shared_final",
    )(*args)


def _sc_worker_id():
    return lax.axis_index("s") * SC_CORES + lax.axis_index("c")


def _dispatch_body(h_hbm, dest_hbm, xs_hbm, idx_v, rows_v, sem_l, sem_s):
    n = SC_CHUNKS_PER_WORKER
    c0 = _sc_worker_id() * n

    def load(i, b):
        return pltpu.async_copy(h_hbm.at[pl.ds((c0 + i) * SC_W, SC_W)], rows_v.at[b], sem_l.at[b])

    loads = [None] * n
    scat = [None] * n
    loads[0] = load(0, 0)
    for i in range(n):
        b = i % 2
        pltpu.sync_copy(dest_hbm.at[c0 + i], idx_v.at[b])
        loads[i].wait()
        if i + 1 < n:
            if i >= 1:
                for d in scat[i - 1]:
                    d.wait()
            loads[i + 1] = load(i + 1, 1 - b)
        scat[i] = [pltpu.async_copy(rows_v.at[b], xs_hbm.at[idx_v.at[b].at[k]], sem_s.at[b])
                   for k in range(TOP_K)]
    for i in (n - 2, n - 1):
        for d in scat[i]:
            d.wait()


def _sc_dispatch(h2p, dest3):
    mesh = plsc.VectorSubcoreMesh(core_axis_name="c", subcore_axis_name="s")
    return pl.kernel(
        _dispatch_body, mesh=mesh,
        out_type=jax.ShapeDtypeStruct((N_ROWS, D_MODEL // 2), jnp.uint32),
        scratch_types=[pltpu.VMEM((2, TOP_K, SC_W), jnp.int32),
                       pltpu.VMEM((2, SC_W, D_MODEL // 2), jnp.uint32),
                       pltpu.SemaphoreType.DMA((2,)), pltpu.SemaphoreType.DMA((2,))],
    )(h2p, dest3)


def _combine_body(ys_hbm, dest_hbm, yg_hbm, idx_v, rows_v, sem_g, sem_w):
    c0 = _sc_worker_id() * SC_CHUNKS_PER_WORKER

    @pl.loop(0, SC_CHUNKS_PER_WORKER)
    def _(i):
        c = c0 + i
        pltpu.sync_copy(dest_hbm.at[c], idx_v)
        g = [None] * TOP_K
        w = [None] * TOP_K
        g[0] = pltpu.async_copy(ys_hbm.at[idx_v.at[0]], rows_v.at[0], sem_g.at[0])
        for k in range(TOP_K):
            b = k % 2
            g[k].wait()
            if k + 1 < TOP_K:
                if k >= 1:
                    w[k - 1].wait()
                g[k + 1] = pltpu.async_copy(ys_hbm.at[idx_v.at[k + 1]], rows_v.at[1 - b], sem_g.at[1 - b])
            w[k] = pltpu.async_copy(rows_v.at[b], yg_hbm.at[k].at[pl.ds(c * SC_W, SC_W)], sem_w.at[b])
        w[TOP_K - 2].wait()
        w[TOP_K - 1].wait()


def _sc_combine(ysp, dest3):
    mesh = plsc.VectorSubcoreMesh(core_axis_name="c", subcore_axis_name="s")
    return pl.kernel(
        _combine_body, mesh=mesh,
        out_type=jax.ShapeDtypeStruct((TOP_K, HALF_TOK, D_MODEL // 2), jnp.uint32),
        scratch_types=[pltpu.VMEM((TOP_K, SC_W), jnp.int32),
                       pltpu.VMEM((2, SC_W, D_MODEL // 2), jnp.uint32),
                       pltpu.SemaphoreType.DMA((2,)), pltpu.SemaphoreType.DMA((2,))],
    )(ysp, dest3)


def kernel(x, c, w_ada, b_ada, g_pre_mix, g_post_mix, w_in, conv_w, conv_b, conv_ln_g, conv_ln_b,
           ssm_a_re, ssm_a_im, ssm_log_dt, ssm_b_re, ssm_b_im, ssm_c_re, ssm_c_im, ssm_d,
           ssm_w_glu, ssm_b_glu, g_out_conv, g_out_ssm, w_out, g_pre_ffn, g_post_ffn,
           w_router, b_router, we_gate, we_up, we_down, ws_gate, ws_up, ws_down):
    l = 0
    x2 = x.reshape(N_TOK, D_MODEL)
    r1 = lambda a: a.reshape(1, -1)

    c_pad = jnp.zeros((8, D_MODEL), F32).at[:BATCH].set(c)
    mod = _ada(c_pad, w_ada[l], r1(b_ada[l]))[:BATCH].reshape(BATCH, 6, D_MODEL)
    mod = jnp.concatenate([mod, jnp.zeros((BATCH, 2, D_MODEL), F32)], axis=1)

    v, u, ut = _inproj(x2, mod, r1(g_pre_mix[l]), w_in[l].astype(BF16))
    cw = jnp.concatenate([conv_w[l].reshape(CONV_WIDTH, CONV_CH), jnp.zeros((1, CONV_CH), F32)], axis=0)
    a_n, we_gu, we_d = _conv(v.reshape(BATCH, SEQ, CONV_CH), cw, r1(conv_b[l]), r1(conv_ln_g[l]),
                             r1(conv_ln_b[l]), r1(g_out_conv[l]), we_gate[l], we_up[l], we_down[l])
    a_n = a_n.reshape(N_TOK, CONV_CH)

    pwr, pwi, s5_params, a_cat, b_q, b_s = _s5_operators(
        ssm_a_re[l], ssm_a_im[l], ssm_log_dt[l], ssm_b_re[l], ssm_b_im[l], ssm_c_re[l], ssm_c_im[l])
    yt = _s5(ut, pwr, pwi, s5_params, a_cat, b_q, b_s)

    tm = TM_MIX
    tri = (jnp.arange(tm)[:, None] < jnp.arange(tm)[None, :]).astype(BF16)
    wo = w_out[l].astype(BF16)
    wr_t = w_router[l].T
    wr_hi = wr_t.astype(BF16)
    wr_split = jnp.concatenate([wr_hi, (wr_t - wr_hi.astype(F32)).astype(BF16)], axis=0)
    mix_params = (r1(ssm_d[l]), ssm_w_glu[l].astype(BF16), r1(ssm_b_glu[l]), r1(g_out_ssm[l]),
                  wo[:CONV_CH], wo[CONV_CH:], r1(g_post_mix[l]), r1(g_pre_ffn[l]),
                  wr_split, b_router[l].reshape(N_EXPERTS, 1), tri)
    ws_gu = jnp.concatenate([ws_gate[l], ws_up[l]], axis=1).astype(BF16)
    ws_d = ws_down[l].astype(BF16)
    e_ids = jnp.arange(N_EXPERTS, dtype=jnp.int32)

    out = None
    for half in range(N_HALVES):
        x1, h2, eidx, rank, gw, cnt = _mix(half, a_n, yt, u, x2, mod, *mix_params)
        counts = cnt[:, 0].astype(jnp.int32)
        padded = (counts + ROW_BLOCK - 1) // ROW_BLOCK * ROW_BLOCK
        pstart = jnp.cumsum(padded) - padded
        dest = rank + jnp.sum(jnp.where(eidx[..., None] == e_ids, pstart, 0), axis=-1)
        dest3 = dest.reshape(TOP_K, HALF_TOK // SC_W, SC_W).transpose(1, 0, 2)

        xs = _sc_dispatch(h2, dest3)
        ys = _experts(pstart // ROW_BLOCK, padded // ROW_BLOCK, xs, we_gu, we_d)
        yg = _sc_combine(ys, dest3)
        out = _final(half, out, h2, yg, gw.T, x1, mod, ws_gu, ws_d, r1(g_post_ffn[l]))
    return out.reshape(BATCH, SEQ, D_MODEL)
```

```python
import math

import jax
import jax.numpy as jnp
from jax import lax
from jax.experimental import pallas as pl
from jax.experimental.pallas import tpu as pltpu
from jax.experimental.pallas import tpu_sc as plsc

F32 = jnp.float32
BF16 = jnp.bfloat16

D_MODEL = 1024
BATCH = 2
SEQ = 8192
N_TOK = BATCH * SEQ
CONV_CH = 512
CONV_WIDTH = 31
SSM_CH = 512
SSM_GROUP_CH = 16
SSM_GROUPS = 32
SSM_STATE = 64
D_IN = 2 * CONV_CH + SSM_CH
N_EXPERTS = 64
TOP_K = 8
N_ROUTE_GROUPS = 8
TOPK_ROUTE_GROUPS = 4
D_EXPERT = 256
ROUTED_SCALE = 2.5
NORM_EPS = 1e-6

SUBLANES = 8
LANES = 128

N_MOD = 6
MOD_ROWS = SUBLANES
ADA_COLS = 1536
TM_IN = 512
IN_SUBTILES = 2
TL_CONV = 512
CONV_HALO = 32
CONV_ROWS = 64
EXPERTS_PER_CONV_STEP = N_EXPERTS * TL_CONV // N_TOK
assert EXPERTS_PER_CONV_STEP * N_TOK == N_EXPERTS * TL_CONV
S5_Q = 32
S5_QH = S5_Q * SSM_GROUP_CH
S5_CHUNKS = N_TOK // S5_Q
S5_CHUNKS_PER_SEQ = SEQ // S5_Q
TM_MIX = 512
ROW_BLOCK = 512
EXPERT_AHEAD = 4
EXPERT_SLOTS = EXPERT_AHEAD + 1
HALF_TOK = SEQ
N_HALVES = N_TOK // HALF_TOK
N_BLOCKS = HALF_TOK * TOP_K // ROW_BLOCK + N_EXPERTS
N_ROWS = N_BLOCKS * ROW_BLOCK
TM_OUT = 512
SC_CORES = 2
SC_SUBCORES = 16
SC_WORKERS = SC_CORES * SC_SUBCORES
SC_W = 64
SC_CHUNKS_PER_WORKER = HALF_TOK // (SC_WORKERS * SC_W)
VMEM_LIMIT = 48 * 1024 * 1024


def _cparams(sem):
    return pltpu.CompilerParams(dimension_semantics=sem, vmem_limit_bytes=VMEM_LIMIT)


def _pack_rounded_halves(xr):
    n = xr.shape[-1] // 2
    lo = lax.bitcast_convert_type(xr[:, :n], jnp.uint32)
    hi = lax.bitcast_convert_type(xr[:, n:], jnp.uint32)
    return hi | (lo >> 16)


def _pack_halves(x):
    return _pack_rounded_halves(x.astype(BF16).astype(F32))


def _unpack_halves(p):
    lo = lax.bitcast_convert_type(p << 16, F32)
    hi = lax.bitcast_convert_type(p & jnp.uint32(0xFFFF0000), F32)
    return lo, hi


def _rms(x, g):
    return x * lax.rsqrt(jnp.mean(x * x, axis=-1, keepdims=True) + NORM_EPS) * g


def _split_bf16(x):
    hi = x.astype(BF16)
    return hi, (x - hi.astype(F32)).astype(BF16)


def _dot_nt_split(a, b):
    nt = (((1,), (1,)), ((), ()))
    a_hi, a_lo = _split_bf16(a)
    b_hi, b_lo = _split_bf16(b)
    m = a.shape[0]
    both = lax.dot_general(jnp.concatenate([a_hi, a_lo], axis=0), b_hi, nt, preferred_element_type=F32)
    return both[:m] + both[m:] + lax.dot_general(a_hi, b_lo, nt, preferred_element_type=F32)


def _ada_kernel(c_ref, w_ref, b_ref, o_ref):
    c = c_ref[...]
    a = c * jax.nn.sigmoid(c)
    o_ref[...] = jnp.dot(a, w_ref[...], preferred_element_type=F32,
                         precision=lax.Precision.HIGHEST) + b_ref[...]


def _ada(c_pad, w_ada, b_ada):
    n = w_ada.shape[1]
    bn = ADA_COLS
    return pl.pallas_call(
        _ada_kernel,
        grid=(n // bn,),
        in_specs=[pl.BlockSpec((SUBLANES, D_MODEL), lambda j: (0, 0)),
                  pl.BlockSpec((D_MODEL, bn), lambda j: (0, j)),
                  pl.BlockSpec((1, bn), lambda j: (0, j))],
        out_specs=pl.BlockSpec((SUBLANES, bn), lambda j: (0, j)),
        out_shape=jax.ShapeDtypeStruct((SUBLANES, n), F32),
        compiler_params=_cparams(("arbitrary",)),
        name="ada_mod",
    )(c_pad, w_ada, b_ada)


GROUPS_PER_LANE_TILE = LANES // SSM_GROUP_CH


def _to_group_chunks(u, tile_ref, ut_ref):
    n_chunks = u.shape[0] // S5_Q
    for j in range(SSM_CH // LANES):
        tile_ref[j] = u[:, LANES * j:LANES * (j + 1)]
    for j in range(SSM_CH // LANES):
        rows_t = [tile_ref[j, pl.ds(t, n_chunks, stride=S5_Q), :] for t in range(S5_Q)]
        for gg in range(GROUPS_PER_LANE_TILE):
            lo = gg * SSM_GROUP_CH
            row = jnp.concatenate([r[:, lo:lo + SSM_GROUP_CH] for r in rows_t], axis=1)
            ut_ref[j * GROUPS_PER_LANE_TILE + gg] = row.astype(ut_ref.dtype)


def _from_group_chunks(yt_ref, tile_ref):
    n_chunks = yt_ref.shape[1]
    for j in range(SSM_CH // LANES):
        for t in range(S5_Q):
            lo = t * SSM_GROUP_CH
            piece = jnp.concatenate(
                [yt_ref[j * GROUPS_PER_LANE_TILE + gg, :, lo:lo + SSM_GROUP_CH]
                 for gg in range(GROUPS_PER_LANE_TILE)], axis=1)
            tile_ref[j, pl.ds(t, n_chunks, stride=S5_Q), :] = piece
    return jnp.concatenate([tile_ref[j] for j in range(SSM_CH // LANES)], axis=1)


def _inproj_kernel(x_ref, mod_ref, g_ref, w_ref, v_ref, u_ref, ut_ref, tile_ref):
    sh = mod_ref[0, 0:1, :]
    sc = mod_ref[0, 1:2, :]
    sub = TM_IN // IN_SUBTILES
    sub_chunks = sub // S5_Q
    for s in range(IN_SUBTILES):
        r = slice(s * sub, (s + 1) * sub)
        h = _rms(x_ref[r, :], g_ref[...]) * (1.0 + sc) + sh
        z = jnp.dot(h.astype(BF16), w_ref[...], preferred_element_type=F32)
        v_ref[r, :] = z[:, :CONV_CH] * jax.nn.sigmoid(z[:, CONV_CH:2 * CONV_CH])
        u = z[:, 2 * CONV_CH:]
        u_ref[r, :] = u
        _to_group_chunks(u, tile_ref.at[s], ut_ref.at[:, s * sub_chunks:(s + 1) * sub_chunks, :])


def _inproj(x2, mod, g_pre, w_in_bf):
    tiles_per_seq = SEQ // TM_IN
    return pl.pallas_call(
        _inproj_kernel,
        grid=(N_TOK // TM_IN,),
        in_specs=[pl.BlockSpec((TM_IN, D_MODEL), lambda i: (i, 0)),
                  pl.BlockSpec((1, MOD_ROWS, D_MODEL), lambda i: (i // tiles_per_seq, 0, 0)),
                  pl.BlockSpec((1, D_MODEL), lambda i: (0, 0)),
                  pl.BlockSpec((D_MODEL, D_IN), lambda i: (0, 0))],
        out_specs=[pl.BlockSpec((TM_IN, CONV_CH), lambda i: (i, 0)),
                   pl.BlockSpec((TM_IN, SSM_CH), lambda i: (i, 0)),
                   pl.BlockSpec((SSM_GROUPS, TM_IN // S5_Q, S5_QH), lambda i: (0, i, 0))],
        out_shape=[jax.ShapeDtypeStruct((N_TOK, CONV_CH), F32),
                   jax.ShapeDtypeStruct((N_TOK, SSM_CH), F32),
                   jax.ShapeDtypeStruct((SSM_GROUPS, S5_CHUNKS, S5_QH), BF16)],
        scratch_shapes=[pltpu.VMEM((IN_SUBTILES, SSM_CH // LANES, TM_IN // IN_SUBTILES, LANES), F32)],
        compiler_params=_cparams(("parallel",)),
        name="in_proj",
    )(x2, mod, g_pre, w_in_bf)


def _conv_kernel(vc_ref, vp_ref, w_ref, cb_ref, lg_ref, lb_ref, go_ref, wg_ref, wu_ref, wd_ref,
                 o_ref, wgu_o, wd_o, sh_ref):
    for q in range(EXPERTS_PER_CONV_STEP):
        wgu_o[q, :, :D_EXPERT] = wg_ref[q].astype(BF16)
        wgu_o[q, :, D_EXPERT:] = wu_ref[q].astype(BF16)
        wd_o[q] = wd_ref[q].astype(BF16)

    i = pl.program_id(1)
    keep = (i > 0).astype(F32)
    n_ext = TL_CONV + CONV_HALO
    sh_ref[0, 0:CONV_HALO, :] = vp_ref[0] * keep
    sh_ref[0, CONV_HALO:, :] = vc_ref[0]
    ext = sh_ref[0]
    for s in range(1, SUBLANES):
        sh_ref[s] = pltpu.roll(ext, n_ext - s, axis=0)
    off = CONV_HALO - (CONV_WIDTH - 1)
    for r in range(TL_CONV // CONV_ROWS):
        acc = None
        for j in range(CONV_WIDTH):
            s = (off + j) % SUBLANES
            al = r * CONV_ROWS + (off + j) - s
            term = w_ref[j:j + 1, :] * sh_ref[s, al:al + CONV_ROWS, :]
            acc = term if acc is None else acc + term
        y = acc + cb_ref[...]
        mu = jnp.mean(y, axis=-1, keepdims=True)
        d = y - mu
        var = jnp.mean(d * d, axis=-1, keepdims=True)
        yn = d * lax.rsqrt(var + NORM_EPS) * lg_ref[...] + lb_ref[...]
        a = yn * jax.nn.sigmoid(yn)
        o_ref[0, r * CONV_ROWS:(r + 1) * CONV_ROWS, :] = _rms(a, go_ref[...]).astype(BF16)


def _conv(v3, conv_w, conv_b, ln_g, ln_b, g_out, we_gate, we_up, we_down):
    halo_per_tile = TL_CONV // CONV_HALO
    steps_per_seq = SEQ // TL_CONV
    vec = pl.BlockSpec((1, CONV_CH), lambda b, i: (0, 0))
    ex = EXPERTS_PER_CONV_STEP
    w_in = pl.BlockSpec((ex, D_MODEL, D_EXPERT), lambda b, i: (b * steps_per_seq + i, 0, 0))
    return pl.pallas_call(
        _conv_kernel,
        grid=(BATCH, steps_per_seq),
        in_specs=[pl.BlockSpec((1, TL_CONV, CONV_CH), lambda b, i: (b, i, 0)),
                  pl.BlockSpec((1, CONV_HALO, CONV_CH),
                               lambda b, i: (b, jnp.maximum(i * halo_per_tile - 1, 0), 0)),
                  pl.BlockSpec((CONV_WIDTH + 1, CONV_CH), lambda b, i: (0, 0)),
                  vec, vec, vec, vec,
                  w_in, w_in,
                  pl.BlockSpec((ex, D_EXPERT, D_MODEL), lambda b, i: (b * steps_per_seq + i, 0, 0))],
        out_specs=[pl.BlockSpec((1, TL_CONV, CONV_CH), lambda b, i: (b, i, 0)),
                   pl.BlockSpec((ex, D_MODEL, 2 * D_EXPERT), lambda b, i: (b * steps_per_seq + i, 0, 0)),
                   pl.BlockSpec((ex, D_EXPERT, D_MODEL), lambda b, i: (b * steps_per_seq + i, 0, 0))],
        out_shape=[jax.ShapeDtypeStruct((BATCH, SEQ, CONV_CH), BF16),
                   jax.ShapeDtypeStruct((N_EXPERTS, D_MODEL, 2 * D_EXPERT), BF16),
                   jax.ShapeDtypeStruct((N_EXPERTS, D_EXPERT, D_MODEL), BF16)],
        scratch_shapes=[pltpu.VMEM((SUBLANES, TL_CONV + CONV_HALO, CONV_CH), F32)],
        compiler_params=_cparams(("parallel", "arbitrary")),
        name="conv_module",
    )(v3, v3, conv_w, conv_b, ln_g, ln_b, g_out, we_gate, we_up, we_down)


S5_GROUP_ROWS = S5_CHUNKS + 8


S5_POW_ROWS = (S5_Q + 1 + SUBLANES - 1) // SUBLANES * SUBLANES
(S5_BB_RI, S5_BB_NIR, S5_BB_IR, S5_BB_RNI, S5_CC_RI, S5_CC_NIR, S5_N_PARAM) = range(7)


def _s5_kernel(ut_ref, pwr_ref, pwi_ref, par_ref, a_ref, bq_ref, bs_ref, yt_ref, sin_s, sp_s):
    phase = pl.program_id(0)
    g = pl.program_id(1)
    q = S5_Q
    n = 2 * SSM_STATE
    row0 = pl.multiple_of(g * S5_GROUP_ROWS, SUBLANES)

    def lam_pow(j):
        return pwr_ref[0, j:j + 1, :], pwi_ref[0, j:j + 1, :]

    @pl.when(phase == 0)
    def _():
        bb_ri, bb_nir = par_ref[0, S5_BB_RI], par_ref[0, S5_BB_NIR]
        bb_ir, bb_rni = par_ref[0, S5_BB_IR], par_ref[0, S5_BB_RNI]
        blk_q, blk_s = [], []
        for t in range(q):
            pr, pi_ = lam_pow(q - 1 - t)
            blk_q.append(pr * bb_ri + pi_ * bb_nir)
            blk_s.append(pr * bb_ir + pi_ * bb_rni)
        wst = jnp.concatenate([jnp.concatenate(blk_q, axis=0), jnp.concatenate(blk_s, axis=0)], axis=1)
        r = jnp.dot(ut_ref[0], wst.astype(BF16), preferred_element_type=F32)
        sin_s[0, pl.ds(row0, S5_CHUNKS), :] = r[:, :n]
        sin_s[1, pl.ds(row0, S5_CHUNKS), :] = r[:, n:]

    @pl.when((phase == 1) & (g == 0))
    def _():
        a = a_ref[...]
        bq = bq_ref[...]
        bs = bs_ref[...]

        def body(c, carry):
            nxt = []
            for b in range(BATCH):
                x, xs = carry[b]
                rows = pl.ds(b * S5_CHUNKS_PER_SEQ + c, SSM_GROUPS, stride=S5_GROUP_ROWS)
                sp_s[rows, :] = x
                nxt.append((a * x + bq * xs + sin_s[0, rows, :], a * xs + bs * x + sin_s[1, rows, :]))
            return tuple(nxt)

        z = jnp.zeros((SSM_GROUPS, n), F32)
        lax.fori_loop(0, S5_CHUNKS_PER_SEQ, body, tuple((z, z) for _ in range(BATCH)))

    @pl.when(phase == 1)
    def _():
        cc_ri, cc_nir = par_ref[0, S5_CC_RI], par_ref[0, S5_CC_NIR]
        cl = []
        for j in range(q + 1):
            pr, pi_ = lam_pow(j)
            cl.append(pr * cc_ri + pi_ * cc_nir)
        cl_lo = jnp.concatenate(cl[:q], axis=0)
        cl_hi = jnp.concatenate(cl[1:], axis=0)
        lane = lax.broadcasted_iota(jnp.int32, (1, n), 1)
        vgt = (cl_hi * jnp.where(lane < SSM_STATE, 1.0, -1.0)).astype(BF16)
        kt = _dot_nt_split(par_ref[0, S5_BB_RNI], cl_lo)
        padded = jnp.concatenate([jnp.zeros_like(kt), kt], axis=1)
        tg = jnp.concatenate(
            [padded[:, (q - t) * SSM_GROUP_CH:(q - t) * SSM_GROUP_CH + S5_QH] for t in range(q)],
            axis=0).astype(BF16)
        sp = sp_s[pl.ds(row0, S5_CHUNKS), :]
        y = jnp.dot(ut_ref[0], tg, preferred_element_type=F32)
        yt_ref[0] = y + lax.dot_general(sp.astype(BF16), vgt, (((1,), (1,)), ((), ())),
                                        preferred_element_type=F32)


def _s5(ut, pwr, pwi, params, a_cat, b_q, b_s):
    vec = pl.BlockSpec((SSM_GROUPS, 2 * SSM_STATE), lambda p, g: (0, 0))
    powers = pl.BlockSpec((1, S5_POW_ROWS, 2 * SSM_STATE), lambda p, g: (g, 0, 0))
    return pl.pallas_call(
        _s5_kernel,
        grid=(2, SSM_GROUPS),
        in_specs=[pl.BlockSpec((1, S5_CHUNKS, S5_QH), lambda p, g: (g, 0, 0)),
                  powers, powers,
                  pl.BlockSpec((1, S5_N_PARAM, SSM_GROUP_CH, 2 * SSM_STATE), lambda p, g: (g, 0, 0, 0)),
                  vec, vec, vec],
        out_specs=pl.BlockSpec((1, S5_CHUNKS, S5_QH), lambda p, g: (g * p, 0, 0)),
        out_shape=jax.ShapeDtypeStruct((SSM_GROUPS, S5_CHUNKS, S5_QH), F32),
        scratch_shapes=[pltpu.VMEM((2, SSM_GROUPS * S5_GROUP_ROWS, 2 * SSM_STATE), F32),
                        pltpu.VMEM((SSM_GROUPS * S5_GROUP_ROWS, 2 * SSM_STATE), F32)],
        compiler_params=_cparams(("arbitrary", "arbitrary")),
        name="s5_chunked",
    )(ut, pwr, pwi, params, a_cat, b_q, b_s)


def _s5_operators(a_re, a_im, log_dt, b_re, b_im, c_re, c_im):
    q = S5_Q
    dt = jnp.exp(log_dt)[:, None]
    ar, ai = a_re, a_im
    mag = jnp.exp(ar * dt)
    lr = mag * jnp.cos(ai * dt)
    li = mag * jnp.sin(ai * dt)
    den = ar * ar + ai * ai
    nr = lr - 1.0
    kr = (nr * ar + li * ai) / den
    ki = (li * ar - nr * ai) / den
    bbr = kr[..., None] * b_re - ki[..., None] * b_im
    bbi = kr[..., None] * b_im + ki[..., None] * b_re
    j = jnp.arange(q + 1, dtype=F32)[None, :, None]
    pmag = jnp.exp(ar[:, None, :] * dt[:, :, None] * j)
    pang = ai[:, None, :] * dt[:, :, None] * j
    pr = pmag * jnp.cos(pang)
    pi_ = pmag * jnp.sin(pang)
    pad = ((0, 0), (0, S5_POW_ROWS - (q + 1)), (0, 0))
    pwr = jnp.pad(jnp.concatenate([pr, pr], axis=-1), pad)
    pwi = jnp.pad(jnp.concatenate([pi_, pi_], axis=-1), pad)
    br_t = bbr.transpose(0, 2, 1)
    bi_t = bbi.transpose(0, 2, 1)
    cat = lambda a, b: jnp.concatenate([a, b], axis=-1)
    stack = [None] * S5_N_PARAM
    stack[S5_BB_RI] = cat(br_t, bi_t)
    stack[S5_BB_NIR] = cat(-bi_t, br_t)
    stack[S5_BB_IR] = cat(bi_t, br_t)
    stack[S5_BB_RNI] = cat(br_t, -bi_t)
    stack[S5_CC_RI] = cat(c_re, c_im)
    stack[S5_CC_NIR] = cat(-c_im, c_re)
    params = jnp.stack(stack, axis=1)
    aq_r, aq_i = pr[:, q], pi_[:, q]
    a_cat = cat(aq_r, aq_r)
    b_q = cat(-aq_i, aq_i)
    b_s = cat(aq_i, -aq_i)
    return pwr, pwi, params, a_cat, b_q, b_s


def _gelu_tanh(x):
    return 0.5 * x * (1.0 + jnp.tanh(math.sqrt(2.0 / math.pi) * (x + 0.044715 * (x * x * x))))


def _mix_kernel(an_ref, yt_ref, u_ref, x_ref, mod_ref, d_ref, wglu_ref, bglu_ref, gos_ref,
                woa_ref, wob_ref, gpm_ref, gpf_ref, wr_ref, br_ref, tri_ref,
                x1_ref, h2_ref, eidx_ref, rank_ref, gw_ref, cnt_ref, run_ref, tile_ref):
    i = pl.program_id(0)
    tm = TM_MIX

    @pl.when(i == 0)
    def _():
        run_ref[...] = jnp.zeros_like(run_ref)

    gt_m = mod_ref[0, 2:3, :]
    sh_f = mod_ref[0, 3:4, :]
    sc_f = mod_ref[0, 4:5, :]

    yy = _from_group_chunks(yt_ref, tile_ref) + d_ref[...] * u_ref[...]
    g = _gelu_tanh(yy)
    gl = jnp.dot(g.astype(BF16), wglu_ref[...], preferred_element_type=F32) + bglu_ref[...]
    ob = g * jax.nn.sigmoid(gl)
    bn = _rms(ob, gos_ref[...]).astype(BF16)
    o = (jnp.dot(an_ref[...], woa_ref[...], preferred_element_type=F32)
         + jnp.dot(bn, wob_ref[...], preferred_element_type=F32))
    x1 = x_ref[...] + gt_m * _rms(o, gpm_ref[...])
    x1_ref[...] = x1
    h2 = _rms(x1, gpf_ref[...]) * (1.0 + sc_f) + sh_f
    h2_hi = h2.astype(BF16)
    h2_hi32 = h2_hi.astype(F32)
    h2_ref[...] = _pack_rounded_halves(h2_hi32)

    h2_lo = (h2 - h2_hi32).astype(BF16)
    nt = (((1,), (1,)), ((), ()))
    both = lax.dot_general(wr_ref[...], h2_hi, nt, preferred_element_type=F32)
    logits = (both[:N_EXPERTS] + both[N_EXPERTS:]
              + lax.dot_general(wr_ref[:N_EXPERTS, :], h2_lo, nt, preferred_element_type=F32))
    scores = jax.nn.sigmoid(logits)
    biased = scores + br_ref[...]
    ng = N_ROUTE_GROUPS
    gsz = N_EXPERTS // ng
    b3 = biased.reshape(ng, gsz, tm)
    s3 = scores.reshape(ng, gsz, tm)
    sub = lax.broadcasted_iota(jnp.int32, (ng, gsz, tm), 1).astype(F32)
    grp = lax.broadcasted_iota(jnp.int32, (ng, gsz, tm), 0).astype(F32)
    eid = grp * gsz + sub
    neg = -jnp.inf
    m1 = jnp.max(b3, axis=1, keepdims=True)
    i1 = jnp.min(jnp.where(b3 == m1, sub, float(gsz)), axis=1, keepdims=True)
    m2 = jnp.max(jnp.where(sub == i1, neg, b3), axis=1, keepdims=True)
    gs = m1 + m2
    gi = lax.broadcasted_iota(jnp.int32, (ng, 1, tm), 0)
    beaten = jnp.zeros((ng, 1, tm), F32)
    for gp in range(ng):
        o_ = gs[gp:gp + 1]
        beats = (o_ > gs) | ((o_ == gs) & (gi > gp))
        beaten = beaten + beats.astype(F32)
    gmask = beaten < float(TOPK_ROUTE_GROUPS)
    masked = jnp.where(gmask, b3, neg)

    sels = []
    picked = jnp.zeros((ng, gsz, tm), F32)
    for k in range(TOP_K):
        m = jnp.max(jnp.max(masked, axis=0, keepdims=True), axis=1, keepdims=True)
        cand = jnp.where(masked == m, eid, float(N_EXPERTS))
        sel = jnp.min(jnp.min(cand, axis=0, keepdims=True), axis=1, keepdims=True)
        oh = eid == sel
        masked = jnp.where(oh, neg, masked)
        picked = jnp.where(oh, 1.0, picked)
        sels.append(sel)

    pm = picked.reshape(N_EXPERTS, tm)
    prefix = jnp.dot(pm.astype(BF16), tri_ref[...], preferred_element_type=F32) + run_ref[:, 0:1]
    p3 = prefix.reshape(ng, gsz, tm)
    run_new = run_ref[...] + jnp.sum(pm, axis=1, keepdims=True)
    run_ref[...] = run_new
    cnt_ref[...] = run_new

    sc_rows = []
    for k in range(TOP_K):
        oh = eid == sels[k]
        sc_k = jnp.sum(jnp.sum(jnp.where(oh, s3, 0.0), axis=0, keepdims=True), axis=1, keepdims=True)
        rk_k = jnp.sum(jnp.sum(jnp.where(oh, p3, 0.0), axis=0, keepdims=True), axis=1, keepdims=True)
        sc_rows.append(sc_k)
        eidx_ref[k:k + 1, :] = sels[k].reshape(1, tm).astype(jnp.int32)
        rank_ref[k:k + 1, :] = rk_k.reshape(1, tm).astype(jnp.int32)
    tot = sc_rows[0]
    for k in range(1, TOP_K):
        tot = tot + sc_rows[k]
    inv = ROUTED_SCALE / (tot + 1e-20)
    for k in range(TOP_K):
        gw_ref[k:k + 1, :] = (sc_rows[k] * inv).reshape(1, tm)


def _mix(half, a_n, yt, u2, x2, mod, d_skip, wglu_bf, b_glu, g_out_ssm, wo_a, wo_b, g_post_mix,
         g_pre_ffn, w_router_t, b_router_col, tri):
    tm = TM_MIX
    t0 = half * (HALF_TOK // tm)
    row = lambda n: pl.BlockSpec((1, n), lambda i: (0, 0))
    full = lambda a, b: pl.BlockSpec((a, b), lambda i: (0, 0))
    tok_in = lambda n: pl.BlockSpec((tm, n), lambda i: (t0 + i, 0))
    tok = lambda n: pl.BlockSpec((tm, n), lambda i: (i, 0))
    col = pl.BlockSpec((TOP_K, tm), lambda i: (0, i))
    return pl.pallas_call(
        _mix_kernel,
        grid=(HALF_TOK // tm,),
        in_specs=[tok_in(CONV_CH),
                  pl.BlockSpec((SSM_GROUPS, tm // S5_Q, S5_QH), lambda i: (0, t0 + i, 0)),
                  tok_in(SSM_CH), tok_in(D_MODEL),
                  pl.BlockSpec((1, MOD_ROWS, D_MODEL), lambda i: (half, 0, 0)),
                  row(SSM_CH), full(SSM_CH, SSM_CH), row(SSM_CH), row(SSM_CH),
                  full(CONV_CH, D_MODEL), full(SSM_CH, D_MODEL), row(D_MODEL), row(D_MODEL),
                  full(2 * N_EXPERTS, D_MODEL), full(N_EXPERTS, 1), full(tm, tm)],
        out_specs=[tok(D_MODEL), tok(D_MODEL // 2), col, col, col,
                   pl.BlockSpec((N_EXPERTS, LANES), lambda i: (0, 0))],
        out_shape=[jax.ShapeDtypeStruct((HALF_TOK, D_MODEL), F32),
                   jax.ShapeDtypeStruct((HALF_TOK, D_MODEL // 2), jnp.uint32),
                   jax.ShapeDtypeStruct((TOP_K, HALF_TOK), jnp.int32),
                   jax.ShapeDtypeStruct((TOP_K, HALF_TOK), jnp.int32),
                   jax.ShapeDtypeStruct((TOP_K, HALF_TOK), F32),
                   jax.ShapeDtypeStruct((N_EXPERTS, LANES), F32)],
        scratch_shapes=[pltpu.VMEM((N_EXPERTS, LANES), F32),
                        pltpu.VMEM((SSM_CH // LANES, tm, LANES), F32)],
        compiler_params=_cparams(("arbitrary",)),
        name="mix_out_router",
    )(a_n, yt, u2, x2, mod, d_skip, wglu_bf, b_glu, g_out_ssm, wo_a, wo_b, g_post_mix,
      g_pre_ffn, w_router_t, b_router_col, tri)


def _expert_kernel(blk0_ref, nblk_ref, xs_hbm, wgu_ref, wd_ref, ys_hbm, xbuf, ybuf, sem_in, sem_out):
    e = pl.program_id(0)
    n = nblk_ref[e]
    b0 = blk0_ref[e]
    n_all = blk0_ref[N_EXPERTS - 1] + nblk_ref[N_EXPERTS - 1]

    def rows(b):
        return pl.ds(pl.multiple_of(b * ROW_BLOCK, ROW_BLOCK), ROW_BLOCK)

    def in_copy(b, slot):
        return pltpu.make_async_copy(xs_hbm.at[rows(b)], xbuf.at[slot], sem_in.at[slot])

    def out_copy(b, slot):
        return pltpu.make_async_copy(ybuf.at[slot], ys_hbm.at[rows(b)], sem_out.at[slot])

    for b in range(EXPERT_AHEAD):
        @pl.when((e == 0) & (b < n_all))
        def _():
            in_copy(b, b).start()

    def admit(b):
        in_copy(b, b % EXPERT_SLOTS).wait()
        ahead = b + EXPERT_AHEAD

        @pl.when(ahead < n_all)
        def _():
            in_copy(ahead, ahead % EXPERT_SLOTS).start()

        @pl.when(b >= EXPERT_SLOTS)
        def _():
            out_copy(b - EXPERT_SLOTS, b % EXPERT_SLOTS).wait()

    def compute(b):
        slot = b % EXPERT_SLOTS
        x_lo, x_hi = _unpack_halves(xbuf[slot])
        x = jnp.concatenate([x_lo.astype(BF16), x_hi.astype(BF16)], axis=1)
        h = jnp.dot(x, wgu_ref[0], preferred_element_type=F32)
        hg = h[:, :D_EXPERT]
        act = hg * jax.nn.sigmoid(hg) * h[:, D_EXPERT:]
        ybuf[slot] = _pack_halves(jnp.dot(act.astype(BF16), wd_ref[0], preferred_element_type=F32))

    def block(b, carry):
        admit(b)
        compute(b)
        out_copy(b, b % EXPERT_SLOTS).start()
        return carry

    lax.fori_loop(b0, b0 + n, block, 0)

    @pl.when(e == N_EXPERTS - 1)
    def _():
        for j in range(1, EXPERT_SLOTS + 1):
            @pl.when(n_all >= j)
            def _():
                out_copy(n_all - j, (n_all - j) % EXPERT_SLOTS).wait()


def _experts(blk0, nblk, xs, we_gu, we_d):
    any_spec = pl.BlockSpec(memory_space=pl.ANY)
    grid_spec = pltpu.PrefetchScalarGridSpec(
        num_scalar_prefetch=2,
        grid=(N_EXPERTS,),
        in_specs=[any_spec,
                  pl.BlockSpec((1, D_MODEL, 2 * D_EXPERT), lambda e, b0, nb: (e, 0, 0)),
                  pl.BlockSpec((1, D_EXPERT, D_MODEL), lambda e, b0, nb: (e, 0, 0))],
        out_specs=any_spec,
        scratch_shapes=[pltpu.VMEM((EXPERT_SLOTS, ROW_BLOCK, D_MODEL // 2), jnp.uint32),
                        pltpu.VMEM((EXPERT_SLOTS, ROW_BLOCK, D_MODEL // 2), jnp.uint32),
                        pltpu.SemaphoreType.DMA((EXPERT_SLOTS,)),
                        pltpu.SemaphoreType.DMA((EXPERT_SLOTS,))],
    )
    return pl.pallas_call(
        _expert_kernel,
        grid_spec=grid_spec,
        out_shape=jax.ShapeDtypeStruct((N_ROWS, D_MODEL // 2), jnp.uint32),
        compiler_params=_cparams(("arbitrary",)),
        name="routed_experts",
    )(blk0, nblk, xs, we_gu, we_d)


def _final_kernel(h2_ref, yg_ref, gw_ref, x1_ref, mod_ref, wgu_ref, wd_ref, g_ref, *rest):
    o_ref = rest[-1]
    half = D_MODEL // 2
    gt_f = mod_ref[0, 5:6, :]
    x_lo, x_hi = _unpack_halves(h2_ref[...])
    h = (jnp.dot(x_lo.astype(BF16), wgu_ref[:half, :], preferred_element_type=F32)
         + jnp.dot(x_hi.astype(BF16), wgu_ref[half:, :], preferred_element_type=F32))
    hg = h[:, :D_EXPERT]
    act = hg * jax.nn.sigmoid(hg) * h[:, D_EXPERT:]
    shared = jnp.dot(act.astype(BF16), wd_ref[...], preferred_element_type=F32)
    y_lo = shared[:, :half]
    y_hi = shared[:, half:]
    for k in range(TOP_K):
        r_lo, r_hi = _unpack_halves(yg_ref[k])
        w = gw_ref[:, k:k + 1]
        y_lo = y_lo + w * r_lo
        y_hi = y_hi + w * r_hi
    ms = (jnp.sum(y_lo * y_lo, axis=-1, keepdims=True)
          + jnp.sum(y_hi * y_hi, axis=-1, keepdims=True)) * (1.0 / D_MODEL)
    inv = lax.rsqrt(ms + NORM_EPS)
    o_ref[:, :half] = x1_ref[:, :half] + gt_f[:, :half] * (y_lo * inv * g_ref[:, :half])
    o_ref[:, half:] = x1_ref[:, half:] + gt_f[:, half:] * (y_hi * inv * g_ref[:, half:])


def _final(half, out_prev, h2p, yg, gw_t, x1, mod, ws_gu, ws_d, g_post_ffn):
    tm = TM_OUT
    t0 = half * (HALF_TOK // tm)
    tok = pl.BlockSpec((tm, D_MODEL), lambda i: (i, 0))
    in_specs = [pl.BlockSpec((tm, D_MODEL // 2), lambda i: (i, 0)),
                pl.BlockSpec((TOP_K, tm, D_MODEL // 2), lambda i: (0, i, 0)),
                pl.BlockSpec((tm, TOP_K), lambda i: (i, 0)),
                tok,
                pl.BlockSpec((1, MOD_ROWS, D_MODEL), lambda i: (half, 0, 0)),
                pl.BlockSpec((D_MODEL, 2 * D_EXPERT), lambda i: (0, 0)),
                pl.BlockSpec((D_EXPERT, D_MODEL), lambda i: (0, 0)),
                pl.BlockSpec((1, D_MODEL), lambda i: (0, 0))]
    args = [h2p, yg, gw_t, x1, mod, ws_gu, ws_d, g_post_ffn]
    aliases = {}
    if out_prev is not None:
        aliases = {len(args): 0}
        in_specs.append(pl.BlockSpec(memory_space=pl.ANY))
        args.append(out_prev)
    return pl.pallas_call(
        _final_kernel,
        grid=(HALF_TOK // tm,),
        in_specs=in_specs,
        out_specs=pl.BlockSpec((tm, D_MODEL), lambda i: (t0 + i, 0)),
        out_shape=jax.ShapeDtypeStruct((N_TOK, D_MODEL), F32),
        input_output_aliases=aliases,
        compiler_params=_cparams(("parallel",)),
        name="shared_final",
    )(*args)


def _sc_worker_id():
    return lax.axis_index("s") * SC_CORES + lax.axis_index("c")


def _dispatch_body(h_hbm, dest_hbm, xs_hbm, idx_v, rows_v, sem_l, sem_s):
    n = SC_CHUNKS_PER_WORKER
    c0 = _sc_worker_id() * n

    def load(i, b):
        return pltpu.async_copy(h_hbm.at[pl.ds((c0 + i) * SC_W, SC_W)], rows_v.at[b], sem_l.at[b])

    loads = [None] * n
    scat = [None] * n
    loads[0] = load(0, 0)
    for i in range(n):
        b = i % 2
        pltpu.sync_copy(dest_hbm.at[c0 + i], idx_v.at[b])
        loads[i].wait()
        if i + 1 < n:
            if i >= 1:
                for d in scat[i - 1]:
                    d.wait()
            loads[i + 1] = load(i + 1, 1 - b)
        scat[i] = [pltpu.async_copy(rows_v.at[b], xs_hbm.at[idx_v.at[b].at[k]], sem_s.at[b])
                   for k in range(TOP_K)]
    for i in (n - 2, n - 1):
        for d in scat[i]:
            d.wait()


def _sc_dispatch(h2p, dest3):
    mesh = plsc.VectorSubcoreMesh(core_axis_name="c", subcore_axis_name="s")
    return pl.kernel(
        _dispatch_body, mesh=mesh,
        out_type=jax.ShapeDtypeStruct((N_ROWS, D_MODEL // 2), jnp.uint32),
        scratch_types=[pltpu.VMEM((2, TOP_K, SC_W), jnp.int32),
                       pltpu.VMEM((2, SC_W, D_MODEL // 2), jnp.uint32),
                       pltpu.SemaphoreType.DMA((2,)), pltpu.SemaphoreType.DMA((2,))],
    )(h2p, dest3)


def _combine_body(ys_hbm, dest_hbm, yg_hbm, idx_v, rows_v, sem_g, sem_w):
    c0 = _sc_worker_id() * SC_CHUNKS_PER_WORKER

    @pl.loop(0, SC_CHUNKS_PER_WORKER)
    def _(i):
        c = c0 + i
        pltpu.sync_copy(dest_hbm.at[c], idx_v)
        g = [None] * TOP_K
        w = [None] * TOP_K
        g[0] = pltpu.async_copy(ys_hbm.at[idx_v.at[0]], rows_v.at[0], sem_g.at[0])
        for k in range(TOP_K):
            b = k % 2
            g[k].wait()
            if k + 1 < TOP_K:
                if k >= 1:
                    w[k - 1].wait()
                g[k + 1] = pltpu.async_copy(ys_hbm.at[idx_v.at[k + 1]], rows_v.at[1 - b], sem_g.at[1 - b])
            w[k] = pltpu.async_copy(rows_v.at[b], yg_hbm.at[k].at[pl.ds(c * SC_W, SC_W)], sem_w.at[b])
        w[TOP_K - 2].wait()
        w[TOP_K - 1].wait()


def _sc_combine(ysp, dest3):
    mesh = plsc.VectorSubcoreMesh(core_axis_name="c", subcore_axis_name="s")
    return pl.kernel(
        _combine_body, mesh=mesh,
        out_type=jax.ShapeDtypeStruct((TOP_K, HALF_TOK, D_MODEL // 2), jnp.uint32),
        scratch_types=[pltpu.VMEM((TOP_K, SC_W), jnp.int32),
                       pltpu.VMEM((2, SC_W, D_MODEL // 2), jnp.uint32),
                       pltpu.SemaphoreType.DMA((2,)), pltpu.SemaphoreType.DMA((2,))],
    )(ysp, dest3)


def kernel(x, c, w_ada, b_ada, g_pre_mix, g_post_mix, w_in, conv_w, conv_b, conv_ln_g, conv_ln_b,
           ssm_a_re, ssm_a_im, ssm_log_dt, ssm_b_re, ssm_b_im, ssm_c_re, ssm_c_im, ssm_d,
           ssm_w_glu, ssm_b_glu, g_out_conv, g_out_ssm, w_out, g_pre_ffn, g_post_ffn,
           w_router, b_router, we_gate, we_up, we_down, ws_gate, ws_up, ws_down):
    l = 0
    x2 = x.reshape(N_TOK, D_MODEL)
    r1 = lambda a: a.reshape(1, -1)

    c_pad = jnp.zeros((SUBLANES, D_MODEL), F32).at[:BATCH].set(c)
    mod = _ada(c_pad, w_ada[l], r1(b_ada[l]))[:BATCH].reshape(BATCH, N_MOD, D_MODEL)
    mod = jnp.concatenate([mod, jnp.zeros((BATCH, MOD_ROWS - N_MOD, D_MODEL), F32)], axis=1)

    v, u, ut = _inproj(x2, mod, r1(g_pre_mix[l]), w_in[l].astype(BF16))
    cw = jnp.concatenate([conv_w[l].reshape(CONV_WIDTH, CONV_CH), jnp.zeros((1, CONV_CH), F32)], axis=0)
    a_n, we_gu, we_d = _conv(v.reshape(BATCH, SEQ, CONV_CH), cw, r1(conv_b[l]), r1(conv_ln_g[l]),
                             r1(conv_ln_b[l]), r1(g_out_conv[l]), we_gate[l], we_up[l], we_down[l])
    a_n = a_n.reshape(N_TOK, CONV_CH)

    pwr, pwi, s5_params, a_cat, b_q, b_s = _s5_operators(
        ssm_a_re[l], ssm_a_im[l], ssm_log_dt[l], ssm_b_re[l], ssm_b_im[l], ssm_c_re[l], ssm_c_im[l])
    yt = _s5(ut, pwr, pwi, s5_params, a_cat, b_q, b_s)

    tm = TM_MIX
    tri = (jnp.arange(tm)[:, None] < jnp.arange(tm)[None, :]).astype(BF16)
    wo = w_out[l].astype(BF16)
    wr_t = w_router[l].T
    wr_hi = wr_t.astype(BF16)
    wr_split = jnp.concatenate([wr_hi, (wr_t - wr_hi.astype(F32)).astype(BF16)], axis=0)
    mix_params = (r1(ssm_d[l]), ssm_w_glu[l].astype(BF16), r1(ssm_b_glu[l]), r1(g_out_ssm[l]),
                  wo[:CONV_CH], wo[CONV_CH:], r1(g_post_mix[l]), r1(g_pre_ffn[l]),
                  wr_split, b_router[l].reshape(N_EXPERTS, 1), tri)
    ws_gu = jnp.concatenate([ws_gate[l], ws_up[l]], axis=1).astype(BF16)
    ws_d = ws_down[l].astype(BF16)
    e_ids = jnp.arange(N_EXPERTS, dtype=jnp.int32)

    out = None
    for half in range(N_HALVES):
        x1, h2, eidx, rank, gw, cnt = _mix(half, a_n, yt, u, x2, mod, *mix_params)
        counts = cnt[:, 0].astype(jnp.int32)
        padded = (counts + ROW_BLOCK - 1) // ROW_BLOCK * ROW_BLOCK
        pstart = jnp.cumsum(padded) - padded
        dest = rank + jnp.sum(jnp.where(eidx[..., None] == e_ids, pstart, 0), axis=-1)
        dest3 = dest.reshape(TOP_K, HALF_TOK // SC_W, SC_W).transpose(1, 0, 2)

        xs = _sc_dispatch(h2, dest3)
        ys = _experts(pstart // ROW_BLOCK, padded // ROW_BLOCK, xs, we_gu, we_d)
        yg = _sc_combine(ys, dest3)
        out = _final(half, out, h2, yg, gw.T, x1, mod, ws_gu, ws_d, r1(g_post_ffn[l]))
    return out.reshape(BATCH, SEQ, D_MODEL)
```

```python
import math

import jax
import jax.numpy as jnp
from jax import lax
from jax.experimental import pallas as pl
from jax.experimental.pallas import tpu as pltpu
from jax.experimental.pallas import tpu_sc as plsc

F32 = jnp.float32
BF16 = jnp.bfloat16

D_MODEL = 1024
BATCH = 2
SEQ = 8192
N_TOK = BATCH * SEQ
CONV_CH = 512
CONV_WIDTH = 31
SSM_CH = 512
SSM_GROUP_CH = 16
SSM_GROUPS = 32
SSM_STATE = 64
D_IN = 2 * CONV_CH + SSM_CH
N_EXPERTS = 64
TOP_K = 8
N_ROUTE_GROUPS = 8
TOPK_ROUTE_GROUPS = 4
D_EXPERT = 256
ROUTED_SCALE = 2.5
NORM_EPS = 1e-6

SUBLANES = 8
LANES = 128

N_MOD = 6
MOD_ROWS = SUBLANES
ADA_COLS = 1536
TM_IN = 512
IN_SUBTILES = 2
TL_CONV = 512
CONV_HALO = 32
CONV_ROWS = 64
EXPERTS_PER_CONV_STEP = N_EXPERTS * TL_CONV // N_TOK
assert EXPERTS_PER_CONV_STEP * N_TOK == N_EXPERTS * TL_CONV
S5_Q = 32
S5_QH = S5_Q * SSM_GROUP_CH
S5_CHUNKS = N_TOK // S5_Q
S5_CHUNKS_PER_SEQ = SEQ // S5_Q
TM_MIX = 512
ROW_BLOCK = 384
EXPERT_AHEAD = 4
EXPERT_SLOTS = EXPERT_AHEAD + 1
HALF_TOK = SEQ
N_HALVES = N_TOK // HALF_TOK
N_BLOCKS = HALF_TOK * TOP_K // ROW_BLOCK + N_EXPERTS
N_ROWS = N_BLOCKS * ROW_BLOCK
TM_OUT = 512
SC_CORES = 2
SC_SUBCORES = 16
SC_WORKERS = SC_CORES * SC_SUBCORES
SC_W = 64
SC_CHUNKS_PER_WORKER = HALF_TOK // (SC_WORKERS * SC_W)
VMEM_LIMIT = 48 * 1024 * 1024


def _cparams(sem):
    return pltpu.CompilerParams(dimension_semantics=sem, vmem_limit_bytes=VMEM_LIMIT)


def _pack_rounded_halves(xr):
    n = xr.shape[-1] // 2
    lo = lax.bitcast_convert_type(xr[:, :n], jnp.uint32)
    hi = lax.bitcast_convert_type(xr[:, n:], jnp.uint32)
    return hi | (lo >> 16)


def _pack_halves(x):
    return _pack_rounded_halves(x.astype(BF16).astype(F32))


def _unpack_halves(p):
    lo = lax.bitcast_convert_type(p << 16, F32)
    hi = lax.bitcast_convert_type(p & jnp.uint32(0xFFFF0000), F32)
    return lo, hi


def _rms(x, g):
    return x * lax.rsqrt(jnp.mean(x * x, axis=-1, keepdims=True) + NORM_EPS) * g


def _split_bf16(x):
    hi = x.astype(BF16)
    return hi, (x - hi.astype(F32)).astype(BF16)


def _dot_nt_split(a, b):
    nt = (((1,), (1,)), ((), ()))
    a_hi, a_lo = _split_bf16(a)
    b_hi, b_lo = _split_bf16(b)
    m = a.shape[0]
    both = lax.dot_general(jnp.concatenate([a_hi, a_lo], axis=0), b_hi, nt, preferred_element_type=F32)
    return both[:m] + both[m:] + lax.dot_general(a_hi, b_lo, nt, preferred_element_type=F32)


def _ada_kernel(c_ref, w_ref, b_ref, o_ref):
    c = c_ref[...]
    a = c * jax.nn.sigmoid(c)
    o_ref[...] = jnp.dot(a, w_ref[...], preferred_element_type=F32,
                         precision=lax.Precision.HIGHEST) + b_ref[...]


def _ada(c_pad, w_ada, b_ada):
    n = w_ada.shape[1]
    bn = ADA_COLS
    return pl.pallas_call(
        _ada_kernel,
        grid=(n // bn,),
        in_specs=[pl.BlockSpec((SUBLANES, D_MODEL), lambda j: (0, 0)),
                  pl.BlockSpec((D_MODEL, bn), lambda j: (0, j)),
                  pl.BlockSpec((1, bn), lambda j: (0, j))],
        out_specs=pl.BlockSpec((SUBLANES, bn), lambda j: (0, j)),
        out_shape=jax.ShapeDtypeStruct((SUBLANES, n), F32),
        compiler_params=_cparams(("arbitrary",)),
        name="ada_mod",
    )(c_pad, w_ada, b_ada)


GROUPS_PER_LANE_TILE = LANES // SSM_GROUP_CH


def _to_group_chunks(u, tile_ref, ut_ref):
    n_chunks = u.shape[0] // S5_Q
    for j in range(SSM_CH // LANES):
        tile_ref[j] = u[:, LANES * j:LANES * (j + 1)]
    for j in range(SSM_CH // LANES):
        rows_t = [tile_ref[j, pl.ds(t, n_chunks, stride=S5_Q), :] for t in range(S5_Q)]
        for gg in range(GROUPS_PER_LANE_TILE):
            lo = gg * SSM_GROUP_CH
            row = jnp.concatenate([r[:, lo:lo + SSM_GROUP_CH] for r in rows_t], axis=1)
            ut_ref[j * GROUPS_PER_LANE_TILE + gg] = row.astype(ut_ref.dtype)


def _from_group_chunks(yt_ref, tile_ref):
    n_chunks = yt_ref.shape[1]
    for j in range(SSM_CH // LANES):
        for t in range(S5_Q):
            lo = t * SSM_GROUP_CH
            piece = jnp.concatenate(
                [yt_ref[j * GROUPS_PER_LANE_TILE + gg, :, lo:lo + SSM_GROUP_CH]
                 for gg in range(GROUPS_PER_LANE_TILE)], axis=1)
            tile_ref[j, pl.ds(t, n_chunks, stride=S5_Q), :] = piece
    return jnp.concatenate([tile_ref[j] for j in range(SSM_CH // LANES)], axis=1)


def _inproj_kernel(x_ref, mod_ref, g_ref, w_ref, v_ref, u_ref, ut_ref, tile_ref):
    sh = mod_ref[0, 0:1, :]
    sc = mod_ref[0, 1:2, :]
    sub = TM_IN // IN_SUBTILES
    sub_chunks = sub // S5_Q
    for s in range(IN_SUBTILES):
        r = slice(s * sub, (s + 1) * sub)
        h = _rms(x_ref[r, :], g_ref[...]) * (1.0 + sc) + sh
        z = jnp.dot(h.astype(BF16), w_ref[...], preferred_element_type=F32)
        v_ref[r, :] = z[:, :CONV_CH] * jax.nn.sigmoid(z[:, CONV_CH:2 * CONV_CH])
        u = z[:, 2 * CONV_CH:]
        u_ref[r, :] = u
        _to_group_chunks(u, tile_ref.at[s], ut_ref.at[:, s * sub_chunks:(s + 1) * sub_chunks, :])


def _inproj(x2, mod, g_pre, w_in_bf):
    tiles_per_seq = SEQ // TM_IN
    return pl.pallas_call(
        _inproj_kernel,
        grid=(N_TOK // TM_IN,),
        in_specs=[pl.BlockSpec((TM_IN, D_MODEL), lambda i: (i, 0)),
                  pl.BlockSpec((1, MOD_ROWS, D_MODEL), lambda i: (i // tiles_per_seq, 0, 0)),
                  pl.BlockSpec((1, D_MODEL), lambda i: (0, 0)),
                  pl.BlockSpec((D_MODEL, D_IN), lambda i: (0, 0))],
        out_specs=[pl.BlockSpec((TM_IN, CONV_CH), lambda i: (i, 0)),
                   pl.BlockSpec((TM_IN, SSM_CH), lambda i: (i, 0)),
                   pl.BlockSpec((SSM_GROUPS, TM_IN // S5_Q, S5_QH), lambda i: (0, i, 0))],
        out_shape=[jax.ShapeDtypeStruct((N_TOK, CONV_CH), F32),
                   jax.ShapeDtypeStruct((N_TOK, SSM_CH), F32),
                   jax.ShapeDtypeStruct((SSM_GROUPS, S5_CHUNKS, S5_QH), BF16)],
        scratch_shapes=[pltpu.VMEM((IN_SUBTILES, SSM_CH // LANES, TM_IN // IN_SUBTILES, LANES), F32)],
        compiler_params=_cparams(("parallel",)),
        name="in_proj",
    )(x2, mod, g_pre, w_in_bf)


def _conv_kernel(vc_ref, vp_ref, w_ref, cb_ref, lg_ref, lb_ref, go_ref, wg_ref, wu_ref, wd_ref,
                 o_ref, wgu_o, wd_o, sh_ref):
    for q in range(EXPERTS_PER_CONV_STEP):
        wgu_o[q, :, :D_EXPERT] = wg_ref[q].astype(BF16)
        wgu_o[q, :, D_EXPERT:] = wu_ref[q].astype(BF16)
        wd_o[q] = wd_ref[q].astype(BF16)

    i = pl.program_id(1)
    keep = (i > 0).astype(F32)
    n_ext = TL_CONV + CONV_HALO
    sh_ref[0, 0:CONV_HALO, :] = vp_ref[0] * keep
    sh_ref[0, CONV_HALO:, :] = vc_ref[0]
    ext = sh_ref[0]
    for s in range(1, SUBLANES):
        sh_ref[s] = pltpu.roll(ext, n_ext - s, axis=0)
    off = CONV_HALO - (CONV_WIDTH - 1)
    for r in range(TL_CONV // CONV_ROWS):
        acc = None
        for j in range(CONV_WIDTH):
            s = (off + j) % SUBLANES
            al = r * CONV_ROWS + (off + j) - s
            term = w_ref[j:j + 1, :] * sh_ref[s, al:al + CONV_ROWS, :]
            acc = term if acc is None else acc + term
        y = acc + cb_ref[...]
        mu = jnp.mean(y, axis=-1, keepdims=True)
        d = y - mu
        var = jnp.mean(d * d, axis=-1, keepdims=True)
        yn = d * lax.rsqrt(var + NORM_EPS) * lg_ref[...] + lb_ref[...]
        a = yn * jax.nn.sigmoid(yn)
        o_ref[0, r * CONV_ROWS:(r + 1) * CONV_ROWS, :] = _rms(a, go_ref[...]).astype(BF16)


def _conv(v3, conv_w, conv_b, ln_g, ln_b, g_out, we_gate, we_up, we_down):
    halo_per_tile = TL_CONV // CONV_HALO
    steps_per_seq = SEQ // TL_CONV
    vec = pl.BlockSpec((1, CONV_CH), lambda b, i: (0, 0))
    ex = EXPERTS_PER_CONV_STEP
    w_in = pl.BlockSpec((ex, D_MODEL, D_EXPERT), lambda b, i: (b * steps_per_seq + i, 0, 0))
    return pl.pallas_call(
        _conv_kernel,
        grid=(BATCH, steps_per_seq),
        in_specs=[pl.BlockSpec((1, TL_CONV, CONV_CH), lambda b, i: (b, i, 0)),
                  pl.BlockSpec((1, CONV_HALO, CONV_CH),
                               lambda b, i: (b, jnp.maximum(i * halo_per_tile - 1, 0), 0)),
                  pl.BlockSpec((CONV_WIDTH + 1, CONV_CH), lambda b, i: (0, 0)),
                  vec, vec, vec, vec,
                  w_in, w_in,
                  pl.BlockSpec((ex, D_EXPERT, D_MODEL), lambda b, i: (b * steps_per_seq + i, 0, 0))],
        out_specs=[pl.BlockSpec((1, TL_CONV, CONV_CH), lambda b, i: (b, i, 0)),
                   pl.BlockSpec((ex, D_MODEL, 2 * D_EXPERT), lambda b, i: (b * steps_per_seq + i, 0, 0)),
                   pl.BlockSpec((ex, D_EXPERT, D_MODEL), lambda b, i: (b * steps_per_seq + i, 0, 0))],
        out_shape=[jax.ShapeDtypeStruct((BATCH, SEQ, CONV_CH), BF16),
                   jax.ShapeDtypeStruct((N_EXPERTS, D_MODEL, 2 * D_EXPERT), BF16),
                   jax.ShapeDtypeStruct((N_EXPERTS, D_EXPERT, D_MODEL), BF16)],
        scratch_shapes=[pltpu.VMEM((SUBLANES, TL_CONV + CONV_HALO, CONV_CH), F32)],
        compiler_params=_cparams(("parallel", "arbitrary")),
        name="conv_module",
    )(v3, v3, conv_w, conv_b, ln_g, ln_b, g_out, we_gate, we_up, we_down)


S5_GROUP_ROWS = S5_CHUNKS + 8


S5_POW_ROWS = (S5_Q + 1 + SUBLANES - 1) // SUBLANES * SUBLANES
(S5_BB_RI, S5_BB_NIR, S5_BB_IR, S5_BB_RNI, S5_CC_RI, S5_CC_NIR, S5_N_PARAM) = range(7)


def _s5_kernel(ut_ref, pwr_ref, pwi_ref, par_ref, a_ref, bq_ref, bs_ref, yt_ref, sin_s, sp_s):
    phase = pl.program_id(0)
    g = pl.program_id(1)
    q = S5_Q
    n = 2 * SSM_STATE
    row0 = pl.multiple_of(g * S5_GROUP_ROWS, SUBLANES)

    def lam_pow(j):
        return pwr_ref[0, j:j + 1, :], pwi_ref[0, j:j + 1, :]

    @pl.when(phase == 0)
    def _():
        bb_ri, bb_nir = par_ref[0, S5_BB_RI], par_ref[0, S5_BB_NIR]
        bb_ir, bb_rni = par_ref[0, S5_BB_IR], par_ref[0, S5_BB_RNI]
        blk_q, blk_s = [], []
        for t in range(q):
            pr, pi_ = lam_pow(q - 1 - t)
            blk_q.append(pr * bb_ri + pi_ * bb_nir)
            blk_s.append(pr * bb_ir + pi_ * bb_rni)
        wst = jnp.concatenate([jnp.concatenate(blk_q, axis=0), jnp.concatenate(blk_s, axis=0)], axis=1)
        r = jnp.dot(ut_ref[0], wst.astype(BF16), preferred_element_type=F32)
        sin_s[0, pl.ds(row0, S5_CHUNKS), :] = r[:, :n]
        sin_s[1, pl.ds(row0, S5_CHUNKS), :] = r[:, n:]

    @pl.when((phase == 1) & (g == 0))
    def _():
        a = a_ref[...]
        bq = bq_ref[...]
        bs = bs_ref[...]

        def body(c, carry):
            nxt = []
            for b in range(BATCH):
                x, xs = carry[b]
                rows = pl.ds(b * S5_CHUNKS_PER_SEQ + c, SSM_GROUPS, stride=S5_GROUP_ROWS)
                sp_s[rows, :] = x
                nxt.append((a * x + bq * xs + sin_s[0, rows, :], a * xs + bs * x + sin_s[1, rows, :]))
            return tuple(nxt)

        z = jnp.zeros((SSM_GROUPS, n), F32)
        lax.fori_loop(0, S5_CHUNKS_PER_SEQ, body, tuple((z, z) for _ in range(BATCH)))

    @pl.when(phase == 1)
    def _():
        cc_ri, cc_nir = par_ref[0, S5_CC_RI], par_ref[0, S5_CC_NIR]
        cl = []
        for j in range(q + 1):
            pr, pi_ = lam_pow(j)
            cl.append(pr * cc_ri + pi_ * cc_nir)
        cl_lo = jnp.concatenate(cl[:q], axis=0)
        cl_hi = jnp.concatenate(cl[1:], axis=0)
        lane = lax.broadcasted_iota(jnp.int32, (1, n), 1)
        vgt = (cl_hi * jnp.where(lane < SSM_STATE, 1.0, -1.0)).astype(BF16)
        kt = _dot_nt_split(par_ref[0, S5_BB_RNI], cl_lo)
        padded = jnp.concatenate([jnp.zeros_like(kt), kt], axis=1)
        tg = jnp.concatenate(
            [padded[:, (q - t) * SSM_GROUP_CH:(q - t) * SSM_GROUP_CH + S5_QH] for t in range(q)],
            axis=0).astype(BF16)
        sp = sp_s[pl.ds(row0, S5_CHUNKS), :]
        y = jnp.dot(ut_ref[0], tg, preferred_element_type=F32)
        yt_ref[0] = y + lax.dot_general(sp.astype(BF16), vgt, (((1,), (1,)), ((), ())),
                                        preferred_element_type=F32)


def _s5(ut, pwr, pwi, params, a_cat, b_q, b_s):
    vec = pl.BlockSpec((SSM_GROUPS, 2 * SSM_STATE), lambda p, g: (0, 0))
    powers = pl.BlockSpec((1, S5_POW_ROWS, 2 * SSM_STATE), lambda p, g: (g, 0, 0))
    return pl.pallas_call(
        _s5_kernel,
        grid=(2, SSM_GROUPS),
        in_specs=[pl.BlockSpec((1, S5_CHUNKS, S5_QH), lambda p, g: (g, 0, 0)),
                  powers, powers,
                  pl.BlockSpec((1, S5_N_PARAM, SSM_GROUP_CH, 2 * SSM_STATE), lambda p, g: (g, 0, 0, 0)),
                  vec, vec, vec],
        out_specs=pl.BlockSpec((1, S5_CHUNKS, S5_QH), lambda p, g: (g * p, 0, 0)),
        out_shape=jax.ShapeDtypeStruct((SSM_GROUPS, S5_CHUNKS, S5_QH), F32),
        scratch_shapes=[pltpu.VMEM((2, SSM_GROUPS * S5_GROUP_ROWS, 2 * SSM_STATE), F32),
                        pltpu.VMEM((SSM_GROUPS * S5_GROUP_ROWS, 2 * SSM_STATE), F32)],
        compiler_params=_cparams(("arbitrary", "arbitrary")),
        name="s5_chunked",
    )(ut, pwr, pwi, params, a_cat, b_q, b_s)


def _s5_operators(a_re, a_im, log_dt, b_re, b_im, c_re, c_im):
    q = S5_Q
    dt = jnp.exp(log_dt)[:, None]
    ar, ai = a_re, a_im
    mag = jnp.exp(ar * dt)
    lr = mag * jnp.cos(ai * dt)
    li = mag * jnp.sin(ai * dt)
    den = ar * ar + ai * ai
    nr = lr - 1.0
    kr = (nr * ar + li * ai) / den
    ki = (li * ar - nr * ai) / den
    bbr = kr[..., None] * b_re - ki[..., None] * b_im
    bbi = kr[..., None] * b_im + ki[..., None] * b_re
    j = jnp.arange(q + 1, dtype=F32)[None, :, None]
    pmag = jnp.exp(ar[:, None, :] * dt[:, :, None] * j)
    pang = ai[:, None, :] * dt[:, :, None] * j
    pr = pmag * jnp.cos(pang)
    pi_ = pmag * jnp.sin(pang)
    pad = ((0, 0), (0, S5_POW_ROWS - (q + 1)), (0, 0))
    pwr = jnp.pad(jnp.concatenate([pr, pr], axis=-1), pad)
    pwi = jnp.pad(jnp.concatenate([pi_, pi_], axis=-1), pad)
    br_t = bbr.transpose(0, 2, 1)
    bi_t = bbi.transpose(0, 2, 1)
    cat = lambda a, b: jnp.concatenate([a, b], axis=-1)
    stack = [None] * S5_N_PARAM
    stack[S5_BB_RI] = cat(br_t, bi_t)
    stack[S5_BB_NIR] = cat(-bi_t, br_t)
    stack[S5_BB_IR] = cat(bi_t, br_t)
    stack[S5_BB_RNI] = cat(br_t, -bi_t)
    stack[S5_CC_RI] = cat(c_re, c_im)
    stack[S5_CC_NIR] = cat(-c_im, c_re)
    params = jnp.stack(stack, axis=1)
    aq_r, aq_i = pr[:, q], pi_[:, q]
    a_cat = cat(aq_r, aq_r)
    b_q = cat(-aq_i, aq_i)
    b_s = cat(aq_i, -aq_i)
    return pwr, pwi, params, a_cat, b_q, b_s


def _gelu_tanh(x):
    return 0.5 * x * (1.0 + jnp.tanh(math.sqrt(2.0 / math.pi) * (x + 0.044715 * (x * x * x))))


def _mix_kernel(an_ref, yt_ref, u_ref, x_ref, mod_ref, d_ref, wglu_ref, bglu_ref, gos_ref,
                woa_ref, wob_ref, gpm_ref, gpf_ref, wr_ref, br_ref, tri_ref,
                x1_ref, h2_ref, eidx_ref, rank_ref, gw_ref, cnt_ref, run_ref, tile_ref):
    i = pl.program_id(0)
    tm = TM_MIX

    @pl.when(i == 0)
    def _():
        run_ref[...] = jnp.zeros_like(run_ref)

    gt_m = mod_ref[0, 2:3, :]
    sh_f = mod_ref[0, 3:4, :]
    sc_f = mod_ref[0, 4:5, :]

    yy = _from_group_chunks(yt_ref, tile_ref) + d_ref[...] * u_ref[...]
    g = _gelu_tanh(yy)
    gl = jnp.dot(g.astype(BF16), wglu_ref[...], preferred_element_type=F32) + bglu_ref[...]
    ob = g * jax.nn.sigmoid(gl)
    bn = _rms(ob, gos_ref[...]).astype(BF16)
    o = (jnp.dot(an_ref[...], woa_ref[...], preferred_element_type=F32)
         + jnp.dot(bn, wob_ref[...], preferred_element_type=F32))
    x1 = x_ref[...] + gt_m * _rms(o, gpm_ref[...])
    x1_ref[...] = x1
    h2 = _rms(x1, gpf_ref[...]) * (1.0 + sc_f) + sh_f
    h2_hi = h2.astype(BF16)
    h2_hi32 = h2_hi.astype(F32)
    h2_ref[...] = _pack_rounded_halves(h2_hi32)

    h2_lo = (h2 - h2_hi32).astype(BF16)
    nt = (((1,), (1,)), ((), ()))
    both = lax.dot_general(wr_ref[...], h2_hi, nt, preferred_element_type=F32)
    logits = (both[:N_EXPERTS] + both[N_EXPERTS:]
              + lax.dot_general(wr_ref[:N_EXPERTS, :], h2_lo, nt, preferred_element_type=F32))
    scores = jax.nn.sigmoid(logits)
    biased = scores + br_ref[...]
    ng = N_ROUTE_GROUPS
    gsz = N_EXPERTS // ng
    b3 = biased.reshape(ng, gsz, tm)
    s3 = scores.reshape(ng, gsz, tm)
    sub = lax.broadcasted_iota(jnp.int32, (ng, gsz, tm), 1).astype(F32)
    grp = lax.broadcasted_iota(jnp.int32, (ng, gsz, tm), 0).astype(F32)
    eid = grp * gsz + sub
    neg = -jnp.inf
    m1 = jnp.max(b3, axis=1, keepdims=True)
    i1 = jnp.min(jnp.where(b3 == m1, sub, float(gsz)), axis=1, keepdims=True)
    m2 = jnp.max(jnp.where(sub == i1, neg, b3), axis=1, keepdims=True)
    gs = m1 + m2
    gi = lax.broadcasted_iota(jnp.int32, (ng, 1, tm), 0)
    beaten = jnp.zeros((ng, 1, tm), F32)
    for gp in range(ng):
        o_ = gs[gp:gp + 1]
        beats = (o_ > gs) | ((o_ == gs) & (gi > gp))
        beaten = beaten + beats.astype(F32)
    gmask = beaten < float(TOPK_ROUTE_GROUPS)
    masked = jnp.where(gmask, b3, neg)

    sels = []
    picked = jnp.zeros((ng, gsz, tm), F32)
    for k in range(TOP_K):
        m = jnp.max(jnp.max(masked, axis=0, keepdims=True), axis=1, keepdims=True)
        cand = jnp.where(masked == m, eid, float(N_EXPERTS))
        sel = jnp.min(jnp.min(cand, axis=0, keepdims=True), axis=1, keepdims=True)
        oh = eid == sel
        masked = jnp.where(oh, neg, masked)
        picked = jnp.where(oh, 1.0, picked)
        sels.append(sel)

    pm = picked.reshape(N_EXPERTS, tm)
    prefix = jnp.dot(pm.astype(BF16), tri_ref[...], preferred_element_type=F32) + run_ref[:, 0:1]
    p3 = prefix.reshape(ng, gsz, tm)
    run_new = run_ref[...] + jnp.sum(pm, axis=1, keepdims=True)
    run_ref[...] = run_new
    cnt_ref[...] = run_new

    sc_rows = []
    for k in range(TOP_K):
        oh = eid == sels[k]
        sc_k = jnp.sum(jnp.sum(jnp.where(oh, s3, 0.0), axis=0, keepdims=True), axis=1, keepdims=True)
        rk_k = jnp.sum(jnp.sum(jnp.where(oh, p3, 0.0), axis=0, keepdims=True), axis=1, keepdims=True)
        sc_rows.append(sc_k)
        eidx_ref[k:k + 1, :] = sels[k].reshape(1, tm).astype(jnp.int32)
        rank_ref[k:k + 1, :] = rk_k.reshape(1, tm).astype(jnp.int32)
    tot = sc_rows[0]
    for k in range(1, TOP_K):
        tot = tot + sc_rows[k]
    inv = ROUTED_SCALE / (tot + 1e-20)
    for k in range(TOP_K):
        gw_ref[k:k + 1, :] = (sc_rows[k] * inv).reshape(1, tm)


def _mix(half, a_n, yt, u2, x2, mod, d_skip, wglu_bf, b_glu, g_out_ssm, wo_a, wo_b, g_post_mix,
         g_pre_ffn, w_router_t, b_router_col, tri):
    tm = TM_MIX
    t0 = half * (HALF_TOK // tm)
    row = lambda n: pl.BlockSpec((1, n), lambda i: (0, 0))
    full = lambda a, b: pl.BlockSpec((a, b), lambda i: (0, 0))
    tok_in = lambda n: pl.BlockSpec((tm, n), lambda i: (t0 + i, 0))
    tok = lambda n: pl.BlockSpec((tm, n), lambda i: (i, 0))
    col = pl.BlockSpec((TOP_K, tm), lambda i: (0, i))
    return pl.pallas_call(
        _mix_kernel,
        grid=(HALF_TOK // tm,),
        in_specs=[tok_in(CONV_CH),
                  pl.BlockSpec((SSM_GROUPS, tm // S5_Q, S5_QH), lambda i: (0, t0 + i, 0)),
                  tok_in(SSM_CH), tok_in(D_MODEL),
                  pl.BlockSpec((1, MOD_ROWS, D_MODEL), lambda i: (half, 0, 0)),
                  row(SSM_CH), full(SSM_CH, SSM_CH), row(SSM_CH), row(SSM_CH),
                  full(CONV_CH, D_MODEL), full(SSM_CH, D_MODEL), row(D_MODEL), row(D_MODEL),
                  full(2 * N_EXPERTS, D_MODEL), full(N_EXPERTS, 1), full(tm, tm)],
        out_specs=[tok(D_MODEL), tok(D_MODEL // 2), col, col, col,
                   pl.BlockSpec((N_EXPERTS, LANES), lambda i: (0, 0))],
        out_shape=[jax.ShapeDtypeStruct((HALF_TOK, D_MODEL), F32),
                   jax.ShapeDtypeStruct((HALF_TOK, D_MODEL // 2), jnp.uint32),
                   jax.ShapeDtypeStruct((TOP_K, HALF_TOK), jnp.int32),
                   jax.ShapeDtypeStruct((TOP_K, HALF_TOK), jnp.int32),
                   jax.ShapeDtypeStruct((TOP_K, HALF_TOK), F32),
                   jax.ShapeDtypeStruct((N_EXPERTS, LANES), F32)],
        scratch_shapes=[pltpu.VMEM((N_EXPERTS, LANES), F32),
                        pltpu.VMEM((SSM_CH // LANES, tm, LANES), F32)],
        compiler_params=_cparams(("arbitrary",)),
        name="mix_out_router",
    )(a_n, yt, u2, x2, mod, d_skip, wglu_bf, b_glu, g_out_ssm, wo_a, wo_b, g_post_mix,
      g_pre_ffn, w_router_t, b_router_col, tri)


def _expert_kernel(blk0_ref, nblk_ref, xs_hbm, wgu_ref, wd_ref, ys_hbm, xbuf, ybuf, sem_in, sem_out):
    e = pl.program_id(0)
    n = nblk_ref[e]
    b0 = blk0_ref[e]
    n_all = blk0_ref[N_EXPERTS - 1] + nblk_ref[N_EXPERTS - 1]

    def rows(b):
        return pl.ds(pl.multiple_of(b * ROW_BLOCK, ROW_BLOCK), ROW_BLOCK)

    def in_copy(b, slot):
        return pltpu.make_async_copy(xs_hbm.at[rows(b)], xbuf.at[slot], sem_in.at[slot])

    def out_copy(b, slot):
        return pltpu.make_async_copy(ybuf.at[slot], ys_hbm.at[rows(b)], sem_out.at[slot])

    for b in range(EXPERT_AHEAD):
        @pl.when((e == 0) & (b < n_all))
        def _():
            in_copy(b, b).start()

    def admit(b):
        in_copy(b, b % EXPERT_SLOTS).wait()
        ahead = b + EXPERT_AHEAD

        @pl.when(ahead < n_all)
        def _():
            in_copy(ahead, ahead % EXPERT_SLOTS).start()

        @pl.when(b >= EXPERT_SLOTS)
        def _():
            out_copy(b - EXPERT_SLOTS, b % EXPERT_SLOTS).wait()

    def compute(b):
        slot = b % EXPERT_SLOTS
        x_lo, x_hi = _unpack_halves(xbuf[slot])
        x = jnp.concatenate([x_lo.astype(BF16), x_hi.astype(BF16)], axis=1)
        h = jnp.dot(x, wgu_ref[0], preferred_element_type=F32)
        hg = h[:, :D_EXPERT]
        act = hg * jax.nn.sigmoid(hg) * h[:, D_EXPERT:]
        ybuf[slot] = _pack_halves(jnp.dot(act.astype(BF16), wd_ref[0], preferred_element_type=F32))

    def block(b, carry):
        admit(b)
        compute(b)
        out_copy(b, b % EXPERT_SLOTS).start()
        return carry

    lax.fori_loop(b0, b0 + n, block, 0)

    @pl.when(e == N_EXPERTS - 1)
    def _():
        for j in range(1, EXPERT_SLOTS + 1):
            @pl.when(n_all >= j)
            def _():
                out_copy(n_all - j, (n_all - j) % EXPERT_SLOTS).wait()


def _experts(blk0, nblk, xs, we_gu, we_d):
    any_spec = pl.BlockSpec(memory_space=pl.ANY)
    grid_spec = pltpu.PrefetchScalarGridSpec(
        num_scalar_prefetch=2,
        grid=(N_EXPERTS,),
        in_specs=[any_spec,
                  pl.BlockSpec((1, D_MODEL, 2 * D_EXPERT), lambda e, b0, nb: (e, 0, 0)),
                  pl.BlockSpec((1, D_EXPERT, D_MODEL), lambda e, b0, nb: (e, 0, 0))],
        out_specs=any_spec,
        scratch_shapes=[pltpu.VMEM((EXPERT_SLOTS, ROW_BLOCK, D_MODEL // 2), jnp.uint32),
                        pltpu.VMEM((EXPERT_SLOTS, ROW_BLOCK, D_MODEL // 2), jnp.uint32),
                        pltpu.SemaphoreType.DMA((EXPERT_SLOTS,)),
                        pltpu.SemaphoreType.DMA((EXPERT_SLOTS,))],
    )
    return pl.pallas_call(
        _expert_kernel,
        grid_spec=grid_spec,
        out_shape=jax.ShapeDtypeStruct((N_ROWS, D_MODEL // 2), jnp.uint32),
        compiler_params=_cparams(("arbitrary",)),
        name="routed_experts",
    )(blk0, nblk, xs, we_gu, we_d)


def _final_kernel(h2_ref, yg_ref, gw_ref, x1_ref, mod_ref, wgu_ref, wd_ref, g_ref, *rest):
    o_ref = rest[-1]
    half = D_MODEL // 2
    gt_f = mod_ref[0, 5:6, :]
    x_lo, x_hi = _unpack_halves(h2_ref[...])
    h = (jnp.dot(x_lo.astype(BF16), wgu_ref[:half, :], preferred_element_type=F32)
         + jnp.dot(x_hi.astype(BF16), wgu_ref[half:, :], preferred_element_type=F32))
    hg = h[:, :D_EXPERT]
    act = hg * jax.nn.sigmoid(hg) * h[:, D_EXPERT:]
    shared = jnp.dot(act.astype(BF16), wd_ref[...], preferred_element_type=F32)
    y_lo = shared[:, :half]
    y_hi = shared[:, half:]
    for k in range(TOP_K):
        r_lo, r_hi = _unpack_halves(yg_ref[k])
        w = gw_ref[:, k:k + 1]
        y_lo = y_lo + w * r_lo
        y_hi = y_hi + w * r_hi
    ms = (jnp.sum(y_lo * y_lo, axis=-1, keepdims=True)
          + jnp.sum(y_hi * y_hi, axis=-1, keepdims=True)) * (1.0 / D_MODEL)
    inv = lax.rsqrt(ms + NORM_EPS)
    o_ref[:, :half] = x1_ref[:, :half] + gt_f[:, :half] * (y_lo * inv * g_ref[:, :half])
    o_ref[:, half:] = x1_ref[:, half:] + gt_f[:, half:] * (y_hi * inv * g_ref[:, half:])


def _final(half, out_prev, h2p, yg, gw_t, x1, mod, ws_gu, ws_d, g_post_ffn):
    tm = TM_OUT
    t0 = half * (HALF_TOK // tm)
    tok = pl.BlockSpec((tm, D_MODEL), lambda i: (i, 0))
    in_specs = [pl.BlockSpec((tm, D_MODEL // 2), lambda i: (i, 0)),
                pl.BlockSpec((TOP_K, tm, D_MODEL // 2), lambda i: (0, i, 0)),
                pl.BlockSpec((tm, TOP_K), lambda i: (i, 0)),
                tok,
                pl.BlockSpec((1, MOD_ROWS, D_MODEL), lambda i: (half, 0, 0)),
                pl.BlockSpec((D_MODEL, 2 * D_EXPERT), lambda i: (0, 0)),
                pl.BlockSpec((D_EXPERT, D_MODEL), lambda i: (0, 0)),
                pl.BlockSpec((1, D_MODEL), lambda i: (0, 0))]
    args = [h2p, yg, gw_t, x1, mod, ws_gu, ws_d, g_post_ffn]
    aliases = {}
    if out_prev is not None:
        aliases = {len(args): 0}
        in_specs.append(pl.BlockSpec(memory_space=pl.ANY))
        args.append(out_prev)
    return pl.pallas_call(
        _final_kernel,
        grid=(HALF_TOK // tm,),
        in_specs=in_specs,
        out_specs=pl.BlockSpec((tm, D_MODEL), lambda i: (t0 + i, 0)),
        out_shape=jax.ShapeDtypeStruct((N_TOK, D_MODEL), F32),
        input_output_aliases=aliases,
        compiler_params=_cparams(("parallel",)),
        name="shared_final",
    )(*args)


def _sc_worker_id():
    return lax.axis_index("s") * SC_CORES + lax.axis_index("c")


def _dispatch_body(h_hbm, dest_hbm, xs_hbm, idx_v, rows_v, sem_l, sem_s):
    n = SC_CHUNKS_PER_WORKER
    c0 = _sc_worker_id() * n

    def load(i, b):
        return pltpu.async_copy(h_hbm.at[pl.ds((c0 + i) * SC_W, SC_W)], rows_v.at[b], sem_l.at[b])

    loads = [None] * n
    scat = [None] * n
    loads[0] = load(0, 0)
    for i in range(n):
        b = i % 2
        pltpu.sync_copy(dest_hbm.at[c0 + i], idx_v.at[b])
        loads[i].wait()
        if i + 1 < n:
            if i >= 1:
                for d in scat[i - 1]:
                    d.wait()
            loads[i + 1] = load(i + 1, 1 - b)
        scat[i] = [pltpu.async_copy(rows_v.at[b], xs_hbm.at[idx_v.at[b].at[k]], sem_s.at[b])
                   for k in range(TOP_K)]
    for i in (n - 2, n - 1):
        for d in scat[i]:
            d.wait()


def _sc_dispatch(h2p, dest3):
    mesh = plsc.VectorSubcoreMesh(core_axis_name="c", subcore_axis_name="s")
    return pl.kernel(
        _dispatch_body, mesh=mesh,
        out_type=jax.ShapeDtypeStruct((N_ROWS, D_MODEL // 2), jnp.uint32),
        scratch_types=[pltpu.VMEM((2, TOP_K, SC_W), jnp.int32),
                       pltpu.VMEM((2, SC_W, D_MODEL // 2), jnp.uint32),
                       pltpu.SemaphoreType.DMA((2,)), pltpu.SemaphoreType.DMA((2,))],
    )(h2p, dest3)


def _combine_body(ys_hbm, dest_hbm, yg_hbm, idx_v, rows_v, sem_g, sem_w):
    c0 = _sc_worker_id() * SC_CHUNKS_PER_WORKER

    @pl.loop(0, SC_CHUNKS_PER_WORKER)
    def _(i):
        c = c0 + i
        pltpu.sync_copy(dest_hbm.at[c], idx_v)
        g = [None] * TOP_K
        w = [None] * TOP_K
        g[0] = pltpu.async_copy(ys_hbm.at[idx_v.at[0]], rows_v.at[0], sem_g.at[0])
        for k in range(TOP_K):
            b = k % 2
            g[k].wait()
            if k + 1 < TOP_K:
                if k >= 1:
                    w[k - 1].wait()
                g[k + 1] = pltpu.async_copy(ys_hbm.at[idx_v.at[k + 1]], rows_v.at[1 - b], sem_g.at[1 - b])
            w[k] = pltpu.async_copy(rows_v.at[b], yg_hbm.at[k].at[pl.ds(c * SC_W, SC_W)], sem_w.at[b])
        w[TOP_K - 2].wait()
        w[TOP_K - 1].wait()


def _sc_combine(ysp, dest3):
    mesh = plsc.VectorSubcoreMesh(core_axis_name="c", subcore_axis_name="s")
    return pl.kernel(
        _combine_body, mesh=mesh,
        out_type=jax.ShapeDtypeStruct((TOP_K, HALF_TOK, D_MODEL // 2), jnp.uint32),
        scratch_types=[pltpu.VMEM((TOP_K, SC_W), jnp.int32),
                       pltpu.VMEM((2, SC_W, D_MODEL // 2), jnp.uint32),
                       pltpu.SemaphoreType.DMA((2,)), pltpu.SemaphoreType.DMA((2,))],
    )(ysp, dest3)


def kernel(x, c, w_ada, b_ada, g_pre_mix, g_post_mix, w_in, conv_w, conv_b, conv_ln_g, conv_ln_b,
           ssm_a_re, ssm_a_im, ssm_log_dt, ssm_b_re, ssm_b_im, ssm_c_re, ssm_c_im, ssm_d,
           ssm_w_glu, ssm_b_glu, g_out_conv, g_out_ssm, w_out, g_pre_ffn, g_post_ffn,
           w_router, b_router, we_gate, we_up, we_down, ws_gate, ws_up, ws_down):
    l = 0
    x2 = x.reshape(N_TOK, D_MODEL)
    r1 = lambda a: a.reshape(1, -1)

    c_pad = jnp.zeros((SUBLANES, D_MODEL), F32).at[:BATCH].set(c)
    mod = _ada(c_pad, w_ada[l], r1(b_ada[l]))[:BATCH].reshape(BATCH, N_MOD, D_MODEL)
    mod = jnp.concatenate([mod, jnp.zeros((BATCH, MOD_ROWS - N_MOD, D_MODEL), F32)], axis=1)

    v, u, ut = _inproj(x2, mod, r1(g_pre_mix[l]), w_in[l].astype(BF16))
    cw = jnp.concatenate([conv_w[l].reshape(CONV_WIDTH, CONV_CH), jnp.zeros((1, CONV_CH), F32)], axis=0)
    a_n, we_gu, we_d = _conv(v.reshape(BATCH, SEQ, CONV_CH), cw, r1(conv_b[l]), r1(conv_ln_g[l]),
                             r1(conv_ln_b[l]), r1(g_out_conv[l]), we_gate[l], we_up[l], we_down[l])
    a_n = a_n.reshape(N_TOK, CONV_CH)

    pwr, pwi, s5_params, a_cat, b_q, b_s = _s5_operators(
        ssm_a_re[l], ssm_a_im[l], ssm_log_dt[l], ssm_b_re[l], ssm_b_im[l], ssm_c_re[l], ssm_c_im[l])
    yt = _s5(ut, pwr, pwi, s5_params, a_cat, b_q, b_s)

    tm = TM_MIX
    tri = (jnp.arange(tm)[:, None] < jnp.arange(tm)[None, :]).astype(BF16)
    wo = w_out[l].astype(BF16)
    wr_t = w_router[l].T
    wr_hi = wr_t.astype(BF16)
    wr_split = jnp.concatenate([wr_hi, (wr_t - wr_hi.astype(F32)).astype(BF16)], axis=0)
    mix_params = (r1(ssm_d[l]), ssm_w_glu[l].astype(BF16), r1(ssm_b_glu[l]), r1(g_out_ssm[l]),
                  wo[:CONV_CH], wo[CONV_CH:], r1(g_post_mix[l]), r1(g_pre_ffn[l]),
                  wr_split, b_router[l].reshape(N_EXPERTS, 1), tri)
    ws_gu = jnp.concatenate([ws_gate[l], ws_up[l]], axis=1).astype(BF16)
    ws_d = ws_down[l].astype(BF16)
    e_ids = jnp.arange(N_EXPERTS, dtype=jnp.int32)

    out = None
    for half in range(N_HALVES):
        x1, h2, eidx, rank, gw, cnt = _mix(half, a_n, yt, u, x2, mod, *mix_params)
        counts = cnt[:, 0].astype(jnp.int32)
        padded = (counts + ROW_BLOCK - 1) // ROW_BLOCK * ROW_BLOCK
        pstart = jnp.cumsum(padded) - padded
        dest = rank + jnp.sum(jnp.where(eidx[..., None] == e_ids, pstart, 0), axis=-1)
        dest3 = dest.reshape(TOP_K, HALF_TOK // SC_W, SC_W).transpose(1, 0, 2)

        xs = _sc_dispatch(h2, dest3)
        ys = _experts(pstart // ROW_BLOCK, padded // ROW_BLOCK, xs, we_gu, we_d)
        yg = _sc_combine(ys, dest3)
        out = _final(half, out, h2, yg, gw.T, x1, mod, ws_gu, ws_d, r1(g_post_ffn[l]))
    return out.reshape(BATCH, SEQ, D_MODEL)
```

```python
import math

import jax
import jax.numpy as jnp
from jax import lax
from jax.experimental import pallas as pl
from jax.experimental.pallas import tpu as pltpu
from jax.experimental.pallas import tpu_sc as plsc

F32 = jnp.float32
BF16 = jnp.bfloat16

D_MODEL = 1024
BATCH = 2
SEQ = 8192
N_TOK = BATCH * SEQ
CONV_CH = 512
CONV_WIDTH = 31
SSM_CH = 512
SSM_GROUP_CH = 16
SSM_GROUPS = 32
SSM_STATE = 64
D_IN = 2 * CONV_CH + SSM_CH
N_EXPERTS = 64
TOP_K = 8
N_ROUTE_GROUPS = 8
TOPK_ROUTE_GROUPS = 4
D_EXPERT = 256
ROUTED_SCALE = 2.5
NORM_EPS = 1e-6

SUBLANES = 8
LANES = 128

N_MOD = 6
MOD_ROWS = SUBLANES
ADA_COLS = 1536
TM_IN = 1024
IN_SUBTILES = 4
TL_CONV = 512
CONV_HALO = 32
CONV_ROWS = 64
EXPERTS_PER_CONV_STEP = N_EXPERTS * TL_CONV // N_TOK
assert EXPERTS_PER_CONV_STEP * N_TOK == N_EXPERTS * TL_CONV
S5_Q = 32
S5_QH = S5_Q * SSM_GROUP_CH
S5_CHUNKS = N_TOK // S5_Q
S5_CHUNKS_PER_SEQ = SEQ // S5_Q
TM_MIX = 512
ROW_BLOCK = 512
EXPERT_AHEAD = 4
EXPERT_SLOTS = EXPERT_AHEAD + 1
HALF_TOK = SEQ
N_HALVES = N_TOK // HALF_TOK
N_BLOCKS = HALF_TOK * TOP_K // ROW_BLOCK + N_EXPERTS
N_ROWS = N_BLOCKS * ROW_BLOCK
TM_OUT = 512
SC_CORES = 2
SC_SUBCORES = 16
SC_WORKERS = SC_CORES * SC_SUBCORES
SC_W = 64
SC_CHUNKS_PER_WORKER = HALF_TOK // (SC_WORKERS * SC_W)
VMEM_LIMIT = 48 * 1024 * 1024


def _cparams(sem):
    return pltpu.CompilerParams(dimension_semantics=sem, vmem_limit_bytes=VMEM_LIMIT)


def _pack_rounded_halves(xr):
    n = xr.shape[-1] // 2
    lo = lax.bitcast_convert_type(xr[:, :n], jnp.uint32)
    hi = lax.bitcast_convert_type(xr[:, n:], jnp.uint32)
    return hi | (lo >> 16)


def _pack_halves(x):
    return _pack_rounded_halves(x.astype(BF16).astype(F32))


def _unpack_halves(p):
    lo = lax.bitcast_convert_type(p << 16, F32)
    hi = lax.bitcast_convert_type(p & jnp.uint32(0xFFFF0000), F32)
    return lo, hi


def _rms(x, g):
    return x * lax.rsqrt(jnp.mean(x * x, axis=-1, keepdims=True) + NORM_EPS) * g


def _split_bf16(x):
    hi = x.astype(BF16)
    return hi, (x - hi.astype(F32)).astype(BF16)


def _dot_nt_split(a, b):
    nt = (((1,), (1,)), ((), ()))
    a_hi, a_lo = _split_bf16(a)
    b_hi, b_lo = _split_bf16(b)
    m = a.shape[0]
    both = lax.dot_general(jnp.concatenate([a_hi, a_lo], axis=0), b_hi, nt, preferred_element_type=F32)
    return both[:m] + both[m:] + lax.dot_general(a_hi, b_lo, nt, preferred_element_type=F32)


def _ada_kernel(c_ref, w_ref, b_ref, o_ref):
    c = c_ref[...]
    a = c * jax.nn.sigmoid(c)
    o_ref[...] = jnp.dot(a, w_ref[...], preferred_element_type=F32,
                         precision=lax.Precision.HIGHEST) + b_ref[...]


def _ada(c_pad, w_ada, b_ada):
    n = w_ada.shape[1]
    bn = ADA_COLS
    return pl.pallas_call(
        _ada_kernel,
        grid=(n // bn,),
        in_specs=[pl.BlockSpec((SUBLANES, D_MODEL), lambda j: (0, 0)),
                  pl.BlockSpec((D_MODEL, bn), lambda j: (0, j)),
                  pl.BlockSpec((1, bn), lambda j: (0, j))],
        out_specs=pl.BlockSpec((SUBLANES, bn), lambda j: (0, j)),
        out_shape=jax.ShapeDtypeStruct((SUBLANES, n), F32),
        compiler_params=_cparams(("arbitrary",)),
        name="ada_mod",
    )(c_pad, w_ada, b_ada)


GROUPS_PER_LANE_TILE = LANES // SSM_GROUP_CH


def _to_group_chunks(u, tile_ref, ut_ref):
    n_chunks = u.shape[0] // S5_Q
    for j in range(SSM_CH // LANES):
        tile_ref[j] = u[:, LANES * j:LANES * (j + 1)]
    for j in range(SSM_CH // LANES):
        rows_t = [tile_ref[j, pl.ds(t, n_chunks, stride=S5_Q), :] for t in range(S5_Q)]
        for gg in range(GROUPS_PER_LANE_TILE):
            lo = gg * SSM_GROUP_CH
            row = jnp.concatenate([r[:, lo:lo + SSM_GROUP_CH] for r in rows_t], axis=1)
            ut_ref[j * GROUPS_PER_LANE_TILE + gg] = row.astype(ut_ref.dtype)


def _from_group_chunks(yt_ref, tile_ref):
    n_chunks = yt_ref.shape[1]
    for j in range(SSM_CH // LANES):
        for t in range(S5_Q):
            lo = t * SSM_GROUP_CH
            piece = jnp.concatenate(
                [yt_ref[j * GROUPS_PER_LANE_TILE + gg, :, lo:lo + SSM_GROUP_CH]
                 for gg in range(GROUPS_PER_LANE_TILE)], axis=1)
            tile_ref[j, pl.ds(t, n_chunks, stride=S5_Q), :] = piece
    return jnp.concatenate([tile_ref[j] for j in range(SSM_CH // LANES)], axis=1)


def _inproj_kernel(x_ref, mod_ref, g_ref, w_ref, v_ref, u_ref, ut_ref, tile_ref):
    sh = mod_ref[0, 0:1, :]
    sc = mod_ref[0, 1:2, :]
    sub = TM_IN // IN_SUBTILES
    sub_chunks = sub // S5_Q
    for s in range(IN_SUBTILES):
        r = slice(s * sub, (s + 1) * sub)
        h = _rms(x_ref[r, :], g_ref[...]) * (1.0 + sc) + sh
        z = jnp.dot(h.astype(BF16), w_ref[...], preferred_element_type=F32)
        v_ref[r, :] = z[:, :CONV_CH] * jax.nn.sigmoid(z[:, CONV_CH:2 * CONV_CH])
        u = z[:, 2 * CONV_CH:]
        u_ref[r, :] = u
        _to_group_chunks(u, tile_ref.at[s], ut_ref.at[:, s * sub_chunks:(s + 1) * sub_chunks, :])


def _inproj(x2, mod, g_pre, w_in_bf):
    tiles_per_seq = SEQ // TM_IN
    return pl.pallas_call(
        _inproj_kernel,
        grid=(N_TOK // TM_IN,),
        in_specs=[pl.BlockSpec((TM_IN, D_MODEL), lambda i: (i, 0)),
                  pl.BlockSpec((1, MOD_ROWS, D_MODEL), lambda i: (i // tiles_per_seq, 0, 0)),
                  pl.BlockSpec((1, D_MODEL), lambda i: (0, 0)),
                  pl.BlockSpec((D_MODEL, D_IN), lambda i: (0, 0))],
        out_specs=[pl.BlockSpec((TM_IN, CONV_CH), lambda i: (i, 0)),
                   pl.BlockSpec((TM_IN, SSM_CH), lambda i: (i, 0)),
                   pl.BlockSpec((SSM_GROUPS, TM_IN // S5_Q, S5_QH), lambda i: (0, i, 0))],
        out_shape=[jax.ShapeDtypeStruct((N_TOK, CONV_CH), F32),
                   jax.ShapeDtypeStruct((N_TOK, SSM_CH), F32),
                   jax.ShapeDtypeStruct((SSM_GROUPS, S5_CHUNKS, S5_QH), BF16)],
        scratch_shapes=[pltpu.VMEM((IN_SUBTILES, SSM_CH // LANES, TM_IN // IN_SUBTILES, LANES), F32)],
        compiler_params=_cparams(("parallel",)),
        name="in_proj",
    )(x2, mod, g_pre, w_in_bf)


def _conv_kernel(vc_ref, vp_ref, w_ref, cb_ref, lg_ref, lb_ref, go_ref, wg_ref, wu_ref, wd_ref,
                 o_ref, wgu_o, wd_o, sh_ref):
    for q in range(EXPERTS_PER_CONV_STEP):
        wgu_o[q, :, :D_EXPERT] = wg_ref[q].astype(BF16)
        wgu_o[q, :, D_EXPERT:] = wu_ref[q].astype(BF16)
        wd_o[q] = wd_ref[q].astype(BF16)

    i = pl.program_id(1)
    keep = (i > 0).astype(F32)
    n_ext = TL_CONV + CONV_HALO
    sh_ref[0, 0:CONV_HALO, :] = vp_ref[0] * keep
    sh_ref[0, CONV_HALO:, :] = vc_ref[0]
    ext = sh_ref[0]
    for s in range(1, SUBLANES):
        sh_ref[s] = pltpu.roll(ext, n_ext - s, axis=0)
    off = CONV_HALO - (CONV_WIDTH - 1)
    for r in range(TL_CONV // CONV_ROWS):
        acc = None
        for j in range(CONV_WIDTH):
            s = (off + j) % SUBLANES
            al = r * CONV_ROWS + (off + j) - s
            term = w_ref[j:j + 1, :] * sh_ref[s, al:al + CONV_ROWS, :]
            acc = term if acc is None else acc + term
        y = acc + cb_ref[...]
        mu = jnp.mean(y, axis=-1, keepdims=True)
        d = y - mu
        var = jnp.mean(d * d, axis=-1, keepdims=True)
        yn = d * lax.rsqrt(var + NORM_EPS) * lg_ref[...] + lb_ref[...]
        a = yn * jax.nn.sigmoid(yn)
        o_ref[0, r * CONV_ROWS:(r + 1) * CONV_ROWS, :] = _rms(a, go_ref[...]).astype(BF16)


def _conv(v3, conv_w, conv_b, ln_g, ln_b, g_out, we_gate, we_up, we_down):
    halo_per_tile = TL_CONV // CONV_HALO
    steps_per_seq = SEQ // TL_CONV
    vec = pl.BlockSpec((1, CONV_CH), lambda b, i: (0, 0))
    ex = EXPERTS_PER_CONV_STEP
    w_in = pl.BlockSpec((ex, D_MODEL, D_EXPERT), lambda b, i: (b * steps_per_seq + i, 0, 0))
    return pl.pallas_call(
        _conv_kernel,
        grid=(BATCH, steps_per_seq),
        in_specs=[pl.BlockSpec((1, TL_CONV, CONV_CH), lambda b, i: (b, i, 0)),
                  pl.BlockSpec((1, CONV_HALO, CONV_CH),
                               lambda b, i: (b, jnp.maximum(i * halo_per_tile - 1, 0), 0)),
                  pl.BlockSpec((CONV_WIDTH + 1, CONV_CH), lambda b, i: (0, 0)),
                  vec, vec, vec, vec,
                  w_in, w_in,
                  pl.BlockSpec((ex, D_EXPERT, D_MODEL), lambda b, i: (b * steps_per_seq + i, 0, 0))],
        out_specs=[pl.BlockSpec((1, TL_CONV, CONV_CH), lambda b, i: (b, i, 0)),
                   pl.BlockSpec((ex, D_MODEL, 2 * D_EXPERT), lambda b, i: (b * steps_per_seq + i, 0, 0)),
                   pl.BlockSpec((ex, D_EXPERT, D_MODEL), lambda b, i: (b * steps_per_seq + i, 0, 0))],
        out_shape=[jax.ShapeDtypeStruct((BATCH, SEQ, CONV_CH), BF16),
                   jax.ShapeDtypeStruct((N_EXPERTS, D_MODEL, 2 * D_EXPERT), BF16),
                   jax.ShapeDtypeStruct((N_EXPERTS, D_EXPERT, D_MODEL), BF16)],
        scratch_shapes=[pltpu.VMEM((SUBLANES, TL_CONV + CONV_HALO, CONV_CH), F32)],
        compiler_params=_cparams(("parallel", "arbitrary")),
        name="conv_module",
    )(v3, v3, conv_w, conv_b, ln_g, ln_b, g_out, we_gate, we_up, we_down)


S5_GROUP_ROWS = S5_CHUNKS + 8


S5_POW_ROWS = (S5_Q + 1 + SUBLANES - 1) // SUBLANES * SUBLANES
(S5_BB_RI, S5_BB_NIR, S5_BB_IR, S5_BB_RNI, S5_CC_RI, S5_CC_NIR, S5_N_PARAM) = range(7)


def _s5_kernel(ut_ref, pwr_ref, pwi_ref, par_ref, a_ref, bq_ref, bs_ref, yt_ref, sin_s, sp_s):
    phase = pl.program_id(0)
    g = pl.program_id(1)
    q = S5_Q
    n = 2 * SSM_STATE
    row0 = pl.multiple_of(g * S5_GROUP_ROWS, SUBLANES)

    def lam_pow(j):
        return pwr_ref[0, j:j + 1, :], pwi_ref[0, j:j + 1, :]

    @pl.when(phase == 0)
    def _():
        bb_ri, bb_nir = par_ref[0, S5_BB_RI], par_ref[0, S5_BB_NIR]
        bb_ir, bb_rni = par_ref[0, S5_BB_IR], par_ref[0, S5_BB_RNI]
        blk_q, blk_s = [], []
        for t in range(q):
            pr, pi_ = lam_pow(q - 1 - t)
            blk_q.append(pr * bb_ri + pi_ * bb_nir)
            blk_s.append(pr * bb_ir + pi_ * bb_rni)
        wst = jnp.concatenate([jnp.concatenate(blk_q, axis=0), jnp.concatenate(blk_s, axis=0)], axis=1)
        r = jnp.dot(ut_ref[0], wst.astype(BF16), preferred_element_type=F32)
        sin_s[0, pl.ds(row0, S5_CHUNKS), :] = r[:, :n]
        sin_s[1, pl.ds(row0, S5_CHUNKS), :] = r[:, n:]

    @pl.when((phase == 1) & (g == 0))
    def _():
        a = a_ref[...]
        bq = bq_ref[...]
        bs = bs_ref[...]

        def body(c, carry):
            nxt = []
            for b in range(BATCH):
                x, xs = carry[b]
                rows = pl.ds(b * S5_CHUNKS_PER_SEQ + c, SSM_GROUPS, stride=S5_GROUP_ROWS)
                sp_s[rows, :] = x
                nxt.append((a * x + bq * xs + sin_s[0, rows, :], a * xs + bs * x + sin_s[1, rows, :]))
            return tuple(nxt)

        z = jnp.zeros((SSM_GROUPS, n), F32)
        lax.fori_loop(0, S5_CHUNKS_PER_SEQ, body, tuple((z, z) for _ in range(BATCH)))

    @pl.when(phase == 1)
    def _():
        cc_ri, cc_nir = par_ref[0, S5_CC_RI], par_ref[0, S5_CC_NIR]
        cl = []
        for j in range(q + 1):
            pr, pi_ = lam_pow(j)
            cl.append(pr * cc_ri + pi_ * cc_nir)
        cl_lo = jnp.concatenate(cl[:q], axis=0)
        cl_hi = jnp.concatenate(cl[1:], axis=0)
        lane = lax.broadcasted_iota(jnp.int32, (1, n), 1)
        vgt = (cl_hi * jnp.where(lane < SSM_STATE, 1.0, -1.0)).astype(BF16)
        kt = _dot_nt_split(par_ref[0, S5_BB_RNI], cl_lo)
        padded = jnp.concatenate([jnp.zeros_like(kt), kt], axis=1)
        tg = jnp.concatenate(
            [padded[:, (q - t) * SSM_GROUP_CH:(q - t) * SSM_GROUP_CH + S5_QH] for t in range(q)],
            axis=0).astype(BF16)
        sp = sp_s[pl.ds(row0, S5_CHUNKS), :]
        y = jnp.dot(ut_ref[0], tg, preferred_element_type=F32)
        yt_ref[0] = y + lax.dot_general(sp.astype(BF16), vgt, (((1,), (1,)), ((), ())),
                                        preferred_element_type=F32)


def _s5(ut, pwr, pwi, params, a_cat, b_q, b_s):
    vec = pl.BlockSpec((SSM_GROUPS, 2 * SSM_STATE), lambda p, g: (0, 0))
    powers = pl.BlockSpec((1, S5_POW_ROWS, 2 * SSM_STATE), lambda p, g: (g, 0, 0))
    return pl.pallas_call(
        _s5_kernel,
        grid=(2, SSM_GROUPS),
        in_specs=[pl.BlockSpec((1, S5_CHUNKS, S5_QH), lambda p, g: (g, 0, 0)),
                  powers, powers,
                  pl.BlockSpec((1, S5_N_PARAM, SSM_GROUP_CH, 2 * SSM_STATE), lambda p, g: (g, 0, 0, 0)),
                  vec, vec, vec],
        out_specs=pl.BlockSpec((1, S5_CHUNKS, S5_QH), lambda p, g: (g * p, 0, 0)),
        out_shape=jax.ShapeDtypeStruct((SSM_GROUPS, S5_CHUNKS, S5_QH), F32),
        scratch_shapes=[pltpu.VMEM((2, SSM_GROUPS * S5_GROUP_ROWS, 2 * SSM_STATE), F32),
                        pltpu.VMEM((SSM_GROUPS * S5_GROUP_ROWS, 2 * SSM_STATE), F32)],
        compiler_params=_cparams(("arbitrary", "arbitrary")),
        name="s5_chunked",
    )(ut, pwr, pwi, params, a_cat, b_q, b_s)


def _s5_operators(a_re, a_im, log_dt, b_re, b_im, c_re, c_im):
    q = S5_Q
    dt = jnp.exp(log_dt)[:, None]
    ar, ai = a_re, a_im
    mag = jnp.exp(ar * dt)
    lr = mag * jnp.cos(ai * dt)
    li = mag * jnp.sin(ai * dt)
    den = ar * ar + ai * ai
    nr = lr - 1.0
    kr = (nr * ar + li * ai) / den
    ki = (li * ar - nr * ai) / den
    bbr = kr[..., None] * b_re - ki[..., None] * b_im
    bbi = kr[..., None] * b_im + ki[..., None] * b_re
    j = jnp.arange(q + 1, dtype=F32)[None, :, None]
    pmag = jnp.exp(ar[:, None, :] * dt[:, :, None] * j)
    pang = ai[:, None, :] * dt[:, :, None] * j
    pr = pmag * jnp.cos(pang)
    pi_ = pmag * jnp.sin(pang)
    pad = ((0, 0), (0, S5_POW_ROWS - (q + 1)), (0, 0))
    pwr = jnp.pad(jnp.concatenate([pr, pr], axis=-1), pad)
    pwi = jnp.pad(jnp.concatenate([pi_, pi_], axis=-1), pad)
    br_t = bbr.transpose(0, 2, 1)
    bi_t = bbi.transpose(0, 2, 1)
    cat = lambda a, b: jnp.concatenate([a, b], axis=-1)
    stack = [None] * S5_N_PARAM
    stack[S5_BB_RI] = cat(br_t, bi_t)
    stack[S5_BB_NIR] = cat(-bi_t, br_t)
    stack[S5_BB_IR] = cat(bi_t, br_t)
    stack[S5_BB_RNI] = cat(br_t, -bi_t)
    stack[S5_CC_RI] = cat(c_re, c_im)
    stack[S5_CC_NIR] = cat(-c_im, c_re)
    params = jnp.stack(stack, axis=1)
    aq_r, aq_i = pr[:, q], pi_[:, q]
    a_cat = cat(aq_r, aq_r)
    b_q = cat(-aq_i, aq_i)
    b_s = cat(aq_i, -aq_i)
    return pwr, pwi, params, a_cat, b_q, b_s


def _gelu_tanh(x):
    return 0.5 * x * (1.0 + jnp.tanh(math.sqrt(2.0 / math.pi) * (x + 0.044715 * (x * x * x))))


def _mix_kernel(an_ref, yt_ref, u_ref, x_ref, mod_ref, d_ref, wglu_ref, bglu_ref, gos_ref,
                woa_ref, wob_ref, gpm_ref, gpf_ref, wr_ref, br_ref, tri_ref,
                x1_ref, h2_ref, eidx_ref, rank_ref, gw_ref, cnt_ref, run_ref, tile_ref):
    i = pl.program_id(0)
    tm = TM_MIX

    @pl.when(i == 0)
    def _():
        run_ref[...] = jnp.zeros_like(run_ref)

    gt_m = mod_ref[0, 2:3, :]
    sh_f = mod_ref[0, 3:4, :]
    sc_f = mod_ref[0, 4:5, :]

    yy = _from_group_chunks(yt_ref, tile_ref) + d_ref[...] * u_ref[...]
    g = _gelu_tanh(yy)
    gl = jnp.dot(g.astype(BF16), wglu_ref[...], preferred_element_type=F32) + bglu_ref[...]
    ob = g * jax.nn.sigmoid(gl)
    bn = _rms(ob, gos_ref[...]).astype(BF16)
    o = (jnp.dot(an_ref[...], woa_ref[...], preferred_element_type=F32)
         + jnp.dot(bn, wob_ref[...], preferred_element_type=F32))
    x1 = x_ref[...] + gt_m * _rms(o, gpm_ref[...])
    x1_ref[...] = x1
    h2 = _rms(x1, gpf_ref[...]) * (1.0 + sc_f) + sh_f
    h2_hi = h2.astype(BF16)
    h2_hi32 = h2_hi.astype(F32)
    h2_ref[...] = _pack_rounded_halves(h2_hi32)

    h2_lo = (h2 - h2_hi32).astype(BF16)
    nt = (((1,), (1,)), ((), ()))
    both = lax.dot_general(wr_ref[...], h2_hi, nt, preferred_element_type=F32)
    logits = (both[:N_EXPERTS] + both[N_EXPERTS:]
              + lax.dot_general(wr_ref[:N_EXPERTS, :], h2_lo, nt, preferred_element_type=F32))
    scores = jax.nn.sigmoid(logits)
    biased = scores + br_ref[...]
    ng = N_ROUTE_GROUPS
    gsz = N_EXPERTS // ng
    b3 = biased.reshape(ng, gsz, tm)
    s3 = scores.reshape(ng, gsz, tm)
    sub = lax.broadcasted_iota(jnp.int32, (ng, gsz, tm), 1).astype(F32)
    grp = lax.broadcasted_iota(jnp.int32, (ng, gsz, tm), 0).astype(F32)
    eid = grp * gsz + sub
    neg = -jnp.inf
    m1 = jnp.max(b3, axis=1, keepdims=True)
    i1 = jnp.min(jnp.where(b3 == m1, sub, float(gsz)), axis=1, keepdims=True)
    m2 = jnp.max(jnp.where(sub == i1, neg, b3), axis=1, keepdims=True)
    gs = m1 + m2
    gi = lax.broadcasted_iota(jnp.int32, (ng, 1, tm), 0)
    beaten = jnp.zeros((ng, 1, tm), F32)
    for gp in range(ng):
        o_ = gs[gp:gp + 1]
        beats = (o_ > gs) | ((o_ == gs) & (gi > gp))
        beaten = beaten + beats.astype(F32)
    gmask = beaten < float(TOPK_ROUTE_GROUPS)
    masked = jnp.where(gmask, b3, neg)

    sels = []
    picked = jnp.zeros((ng, gsz, tm), F32)
    for k in range(TOP_K):
        m = jnp.max(jnp.max(masked, axis=0, keepdims=True), axis=1, keepdims=True)
        cand = jnp.where(masked == m, eid, float(N_EXPERTS))
        sel = jnp.min(jnp.min(cand, axis=0, keepdims=True), axis=1, keepdims=True)
        oh = eid == sel
        masked = jnp.where(oh, neg, masked)
        picked = jnp.where(oh, 1.0, picked)
        sels.append(sel)

    pm = picked.reshape(N_EXPERTS, tm)
    prefix = jnp.dot(pm.astype(BF16), tri_ref[...], preferred_element_type=F32) + run_ref[:, 0:1]
    p3 = prefix.reshape(ng, gsz, tm)
    run_new = run_ref[...] + jnp.sum(pm, axis=1, keepdims=True)
    run_ref[...] = run_new
    cnt_ref[...] = run_new

    sc_rows = []
    for k in range(TOP_K):
        oh = eid == sels[k]
        sc_k = jnp.sum(jnp.sum(jnp.where(oh, s3, 0.0), axis=0, keepdims=True), axis=1, keepdims=True)
        rk_k = jnp.sum(jnp.sum(jnp.where(oh, p3, 0.0), axis=0, keepdims=True), axis=1, keepdims=True)
        sc_rows.append(sc_k)
        eidx_ref[k:k + 1, :] = sels[k].reshape(1, tm).astype(jnp.int32)
        rank_ref[k:k + 1, :] = rk_k.reshape(1, tm).astype(jnp.int32)
    tot = sc_rows[0]
    for k in range(1, TOP_K):
        tot = tot + sc_rows[k]
    inv = ROUTED_SCALE / (tot + 1e-20)
    for k in range(TOP_K):
        gw_ref[k:k + 1, :] = (sc_rows[k] * inv).reshape(1, tm)


def _mix(half, a_n, yt, u2, x2, mod, d_skip, wglu_bf, b_glu, g_out_ssm, wo_a, wo_b, g_post_mix,
         g_pre_ffn, w_router_split, b_router_col, tri):
    tm = TM_MIX
    t0 = half * (HALF_TOK // tm)
    row = lambda n: pl.BlockSpec((1, n), lambda i: (0, 0))
    full = lambda a, b: pl.BlockSpec((a, b), lambda i: (0, 0))
    tok_in = lambda n: pl.BlockSpec((tm, n), lambda i: (t0 + i, 0))
    tok = lambda n: pl.BlockSpec((tm, n), lambda i: (i, 0))
    col = pl.BlockSpec((TOP_K, tm), lambda i: (0, i))
    return pl.pallas_call(
        _mix_kernel,
        grid=(HALF_TOK // tm,),
        in_specs=[tok_in(CONV_CH),
                  pl.BlockSpec((SSM_GROUPS, tm // S5_Q, S5_QH), lambda i: (0, t0 + i, 0)),
                  tok_in(SSM_CH), tok_in(D_MODEL),
                  pl.BlockSpec((1, MOD_ROWS, D_MODEL), lambda i: (half, 0, 0)),
                  row(SSM_CH), full(SSM_CH, SSM_CH), row(SSM_CH), row(SSM_CH),
                  full(CONV_CH, D_MODEL), full(SSM_CH, D_MODEL), row(D_MODEL), row(D_MODEL),
                  full(2 * N_EXPERTS, D_MODEL), full(N_EXPERTS, 1), full(tm, tm)],
        out_specs=[tok(D_MODEL), tok(D_MODEL // 2), col, col, col,
                   pl.BlockSpec((N_EXPERTS, LANES), lambda i: (0, 0))],
        out_shape=[jax.ShapeDtypeStruct((HALF_TOK, D_MODEL), F32),
                   jax.ShapeDtypeStruct((HALF_TOK, D_MODEL // 2), jnp.uint32),
                   jax.ShapeDtypeStruct((TOP_K, HALF_TOK), jnp.int32),
                   jax.ShapeDtypeStruct((TOP_K, HALF_TOK), jnp.int32),
                   jax.ShapeDtypeStruct((TOP_K, HALF_TOK), F32),
                   jax.ShapeDtypeStruct((N_EXPERTS, LANES), F32)],
        scratch_shapes=[pltpu.VMEM((N_EXPERTS, LANES), F32),
                        pltpu.VMEM((SSM_CH // LANES, tm, LANES), F32)],
        compiler_params=_cparams(("arbitrary",)),
        name="mix_out_router",
    )(a_n, yt, u2, x2, mod, d_skip, wglu_bf, b_glu, g_out_ssm, wo_a, wo_b, g_post_mix,
      g_pre_ffn, w_router_split, b_router_col, tri)


def _expert_kernel(blk0_ref, nblk_ref, xs_hbm, wgu_ref, wd_ref, ys_hbm, xbuf, ybuf, sem_in, sem_out):
    e = pl.program_id(0)
    n = nblk_ref[e]
    b0 = blk0_ref[e]
    n_all = blk0_ref[N_EXPERTS - 1] + nblk_ref[N_EXPERTS - 1]

    def rows(b):
        return pl.ds(pl.multiple_of(b * ROW_BLOCK, ROW_BLOCK), ROW_BLOCK)

    def in_copy(b, slot):
        return pltpu.make_async_copy(xs_hbm.at[rows(b)], xbuf.at[slot], sem_in.at[slot])

    def out_copy(b, slot):
        return pltpu.make_async_copy(ybuf.at[slot], ys_hbm.at[rows(b)], sem_out.at[slot])

    for b in range(EXPERT_AHEAD):
        @pl.when((e == 0) & (b < n_all))
        def _():
            in_copy(b, b).start()

    def admit(b):
        in_copy(b, b % EXPERT_SLOTS).wait()
        ahead = b + EXPERT_AHEAD

        @pl.when(ahead < n_all)
        def _():
            in_copy(ahead, ahead % EXPERT_SLOTS).start()

        @pl.when(b >= EXPERT_SLOTS)
        def _():
            out_copy(b - EXPERT_SLOTS, b % EXPERT_SLOTS).wait()

    def compute(b):
        slot = b % EXPERT_SLOTS
        x_lo, x_hi = _unpack_halves(xbuf[slot])
        x = jnp.concatenate([x_lo.astype(BF16), x_hi.astype(BF16)], axis=1)
        h = jnp.dot(x, wgu_ref[0], preferred_element_type=F32)
        hg = h[:, :D_EXPERT]
        act = hg * jax.nn.sigmoid(hg) * h[:, D_EXPERT:]
        ybuf[slot] = _pack_halves(jnp.dot(act.astype(BF16), wd_ref[0], preferred_element_type=F32))

    def block(b, carry):
        admit(b)
        compute(b)
        out_copy(b, b % EXPERT_SLOTS).start()
        return carry

    lax.fori_loop(b0, b0 + n, block, 0)

    @pl.when(e == N_EXPERTS - 1)
    def _():
        for j in range(1, EXPERT_SLOTS + 1):
            @pl.when(n_all >= j)
            def _():
                out_copy(n_all - j, (n_all - j) % EXPERT_SLOTS).wait()


def _experts(blk0, nblk, xs, we_gu, we_d):
    any_spec = pl.BlockSpec(memory_space=pl.ANY)
    grid_spec = pltpu.PrefetchScalarGridSpec(
        num_scalar_prefetch=2,
        grid=(N_EXPERTS,),
        in_specs=[any_spec,
                  pl.BlockSpec((1, D_MODEL, 2 * D_EXPERT), lambda e, b0, nb: (e, 0, 0)),
                  pl.BlockSpec((1, D_EXPERT, D_MODEL), lambda e, b0, nb: (e, 0, 0))],
        out_specs=any_spec,
        scratch_shapes=[pltpu.VMEM((EXPERT_SLOTS, ROW_BLOCK, D_MODEL // 2), jnp.uint32),
                        pltpu.VMEM((EXPERT_SLOTS, ROW_BLOCK, D_MODEL // 2), jnp.uint32),
                        pltpu.SemaphoreType.DMA((EXPERT_SLOTS,)),
                        pltpu.SemaphoreType.DMA((EXPERT_SLOTS,))],
    )
    return pl.pallas_call(
        _expert_kernel,
        grid_spec=grid_spec,
        out_shape=jax.ShapeDtypeStruct((N_ROWS, D_MODEL // 2), jnp.uint32),
        compiler_params=_cparams(("arbitrary",)),
        name="routed_experts",
    )(blk0, nblk, xs, we_gu, we_d)


def _final_kernel(h2_ref, yg_ref, gw_ref, x1_ref, mod_ref, wgu_ref, wd_ref, g_ref, *rest):
    o_ref = rest[-1]
    half = D_MODEL // 2
    gt_f = mod_ref[0, 5:6, :]
    x_lo, x_hi = _unpack_halves(h2_ref[...])
    h = (jnp.dot(x_lo.astype(BF16), wgu_ref[:half, :], preferred_element_type=F32)
         + jnp.dot(x_hi.astype(BF16), wgu_ref[half:, :], preferred_element_type=F32))
    hg = h[:, :D_EXPERT]
    act = hg * jax.nn.sigmoid(hg) * h[:, D_EXPERT:]
    shared = jnp.dot(act.astype(BF16), wd_ref[...], preferred_element_type=F32)
    y_lo = shared[:, :half]
    y_hi = shared[:, half:]
    for k in range(TOP_K):
        r_lo, r_hi = _unpack_halves(yg_ref[k])
        w = gw_ref[:, k:k + 1]
        y_lo = y_lo + w * r_lo
        y_hi = y_hi + w * r_hi
    ms = (jnp.sum(y_lo * y_lo, axis=-1, keepdims=True)
          + jnp.sum(y_hi * y_hi, axis=-1, keepdims=True)) * (1.0 / D_MODEL)
    inv = lax.rsqrt(ms + NORM_EPS)
    o_ref[:, :half] = x1_ref[:, :half] + gt_f[:, :half] * (y_lo * inv * g_ref[:, :half])
    o_ref[:, half:] = x1_ref[:, half:] + gt_f[:, half:] * (y_hi * inv * g_ref[:, half:])


def _final(half, out_prev, h2p, yg, gw_t, x1, mod, ws_gu, ws_d, g_post_ffn):
    tm = TM_OUT
    t0 = half * (HALF_TOK // tm)
    tok = pl.BlockSpec((tm, D_MODEL), lambda i: (i, 0))
    in_specs = [pl.BlockSpec((tm, D_MODEL // 2), lambda i: (i, 0)),
                pl.BlockSpec((TOP_K, tm, D_MODEL // 2), lambda i: (0, i, 0)),
                pl.BlockSpec((tm, TOP_K), lambda i: (i, 0)),
                tok,
                pl.BlockSpec((1, MOD_ROWS, D_MODEL), lambda i: (half, 0, 0)),
                pl.BlockSpec((D_MODEL, 2 * D_EXPERT), lambda i: (0, 0)),
                pl.BlockSpec((D_EXPERT, D_MODEL), lambda i: (0, 0)),
                pl.BlockSpec((1, D_MODEL), lambda i: (0, 0))]
    args = [h2p, yg, gw_t, x1, mod, ws_gu, ws_d, g_post_ffn]
    aliases = {}
    if out_prev is not None:
        aliases = {len(args): 0}
        in_specs.append(pl.BlockSpec(memory_space=pl.ANY))
        args.append(out_prev)
    return pl.pallas_call(
        _final_kernel,
        grid=(HALF_TOK // tm,),
        in_specs=in_specs,
        out_specs=pl.BlockSpec((tm, D_MODEL), lambda i: (t0 + i, 0)),
        out_shape=jax.ShapeDtypeStruct((N_TOK, D_MODEL), F32),
        input_output_aliases=aliases,
        compiler_params=_cparams(("parallel",)),
        name="shared_final",
    )(*args)


def _sc_worker_id():
    return lax.axis_index("s") * SC_CORES + lax.axis_index("c")


def _dispatch_body(h_hbm, dest_hbm, xs_hbm, idx_v, rows_v, sem_l, sem_s):
    n = SC_CHUNKS_PER_WORKER
    c0 = _sc_worker_id() * n

    def load(i, b):
        return pltpu.async_copy(h_hbm.at[pl.ds((c0 + i) * SC_W, SC_W)], rows_v.at[b], sem_l.at[b])

    loads = [None] * n
    scat = [None] * n
    loads[0] = load(0, 0)
    for i in range(n):
        b = i % 2
        pltpu.sync_copy(dest_hbm.at[c0 + i], idx_v.at[b])
        loads[i].wait()
        if i + 1 < n:
            if i >= 1:
                for d in scat[i - 1]:
                    d.wait()
            loads[i + 1] = load(i + 1, 1 - b)
        scat[i] = [pltpu.async_copy(rows_v.at[b], xs_hbm.at[idx_v.at[b].at[k]], sem_s.at[b])
                   for k in range(TOP_K)]
    for i in (n - 2, n - 1):
        for d in scat[i]:
            d.wait()


def _sc_dispatch(h2p, dest3):
    mesh = plsc.VectorSubcoreMesh(core_axis_name="c", subcore_axis_name="s")
    return pl.kernel(
        _dispatch_body, mesh=mesh,
        out_type=jax.ShapeDtypeStruct((N_ROWS, D_MODEL // 2), jnp.uint32),
        scratch_types=[pltpu.VMEM((2, TOP_K, SC_W), jnp.int32),
                       pltpu.VMEM((2, SC_W, D_MODEL // 2), jnp.uint32),
                       pltpu.SemaphoreType.DMA((2,)), pltpu.SemaphoreType.DMA((2,))],
    )(h2p, dest3)


def _combine_body(ys_hbm, dest_hbm, yg_hbm, idx_v, rows_v, sem_g, sem_w):
    c0 = _sc_worker_id() * SC_CHUNKS_PER_WORKER

    @pl.loop(0, SC_CHUNKS_PER_WORKER)
    def _(i):
        c = c0 + i
        pltpu.sync_copy(dest_hbm.at[c], idx_v)
        g = [None] * TOP_K
        w = [None] * TOP_K
        g[0] = pltpu.async_copy(ys_hbm.at[idx_v.at[0]], rows_v.at[0], sem_g.at[0])
        for k in range(TOP_K):
            b = k % 2
            g[k].wait()
            if k + 1 < TOP_K:
                if k >= 1:
                    w[k - 1].wait()
                g[k + 1] = pltpu.async_copy(ys_hbm.at[idx_v.at[k + 1]], rows_v.at[1 - b], sem_g.at[1 - b])
            w[k] = pltpu.async_copy(rows_v.at[b], yg_hbm.at[k].at[pl.ds(c * SC_W, SC_W)], sem_w.at[b])
        w[TOP_K - 2].wait()
        w[TOP_K - 1].wait()


def _sc_combine(ysp, dest3):
    mesh = plsc.VectorSubcoreMesh(core_axis_name="c", subcore_axis_name="s")
    return pl.kernel(
        _combine_body, mesh=mesh,
        out_type=jax.ShapeDtypeStruct((TOP_K, HALF_TOK, D_MODEL // 2), jnp.uint32),
        scratch_types=[pltpu.VMEM((TOP_K, SC_W), jnp.int32),
                       pltpu.VMEM((2, SC_W, D_MODEL // 2), jnp.uint32),
                       pltpu.SemaphoreType.DMA((2,)), pltpu.SemaphoreType.DMA((2,))],
    )(ysp, dest3)


def kernel(x, c, w_ada, b_ada, g_pre_mix, g_post_mix, w_in, conv_w, conv_b, conv_ln_g, conv_ln_b,
           ssm_a_re, ssm_a_im, ssm_log_dt, ssm_b_re, ssm_b_im, ssm_c_re, ssm_c_im, ssm_d,
           ssm_w_glu, ssm_b_glu, g_out_conv, g_out_ssm, w_out, g_pre_ffn, g_post_ffn,
           w_router, b_router, we_gate, we_up, we_down, ws_gate, ws_up, ws_down):
    l = 0
    x2 = x.reshape(N_TOK, D_MODEL)
    r1 = lambda a: a.reshape(1, -1)

    c_pad = jnp.zeros((SUBLANES, D_MODEL), F32).at[:BATCH].set(c)
    mod = _ada(c_pad, w_ada[l], r1(b_ada[l]))[:BATCH].reshape(BATCH, N_MOD, D_MODEL)
    mod = jnp.concatenate([mod, jnp.zeros((BATCH, MOD_ROWS - N_MOD, D_MODEL), F32)], axis=1)

    v, u, ut = _inproj(x2, mod, r1(g_pre_mix[l]), w_in[l].astype(BF16))
    cw = jnp.concatenate([conv_w[l].reshape(CONV_WIDTH, CONV_CH), jnp.zeros((1, CONV_CH), F32)], axis=0)
    a_n, we_gu, we_d = _conv(v.reshape(BATCH, SEQ, CONV_CH), cw, r1(conv_b[l]), r1(conv_ln_g[l]),
                             r1(conv_ln_b[l]), r1(g_out_conv[l]), we_gate[l], we_up[l], we_down[l])
    a_n = a_n.reshape(N_TOK, CONV_CH)

    pwr, pwi, s5_params, a_cat, b_q, b_s = _s5_operators(
        ssm_a_re[l], ssm_a_im[l], ssm_log_dt[l], ssm_b_re[l], ssm_b_im[l], ssm_c_re[l], ssm_c_im[l])
    yt = _s5(ut, pwr, pwi, s5_params, a_cat, b_q, b_s)

    tm = TM_MIX
    tri = (jnp.arange(tm)[:, None] < jnp.arange(tm)[None, :]).astype(BF16)
    wo = w_out[l].astype(BF16)
    wr_t = w_router[l].T
    wr_hi = wr_t.astype(BF16)
    wr_split = jnp.concatenate([wr_hi, (wr_t - wr_hi.astype(F32)).astype(BF16)], axis=0)
    mix_params = (r1(ssm_d[l]), ssm_w_glu[l].astype(BF16), r1(ssm_b_glu[l]), r1(g_out_ssm[l]),
                  wo[:CONV_CH], wo[CONV_CH:], r1(g_post_mix[l]), r1(g_pre_ffn[l]),
                  wr_split, b_router[l].reshape(N_EXPERTS, 1), tri)
    ws_gu = jnp.concatenate([ws_gate[l], ws_up[l]], axis=1).astype(BF16)
    ws_d = ws_down[l].astype(BF16)
    e_ids = jnp.arange(N_EXPERTS, dtype=jnp.int32)

    out = None
    for half in range(N_HALVES):
        x1, h2, eidx, rank, gw, cnt = _mix(half, a_n, yt, u, x2, mod, *mix_params)
        counts = cnt[:, 0].astype(jnp.int32)
        padded = (counts + ROW_BLOCK - 1) // ROW_BLOCK * ROW_BLOCK
        pstart = jnp.cumsum(padded) - padded
        dest = rank + jnp.sum(jnp.where(eidx[..., None] == e_ids, pstart, 0), axis=-1)
        dest3 = dest.reshape(TOP_K, HALF_TOK // SC_W, SC_W).transpose(1, 0, 2)

        xs = _sc_dispatch(h2, dest3)
        ys = _experts(pstart // ROW_BLOCK, padded // ROW_BLOCK, xs, we_gu, we_d)
        yg = _sc_combine(ys, dest3)
        out = _final(half, out, h2, yg, gw.T, x1, mod, ws_gu, ws_d, r1(g_post_ffn[l]))
    return out.reshape(BATCH, SEQ, D_MODEL)
```

```python
import math

import jax
import jax.numpy as jnp
from jax import lax
from jax.experimental import pallas as pl
from jax.experimental.pallas import tpu as pltpu
from jax.experimental.pallas import tpu_sc as plsc

F32 = jnp.float32
BF16 = jnp.bfloat16

D_MODEL = 1024
BATCH = 2
SEQ = 8192
N_TOK = BATCH * SEQ
CONV_CH = 512
CONV_WIDTH = 31
SSM_CH = 512
SSM_GROUP_CH = 16
SSM_GROUPS = 32
SSM_STATE = 64
D_IN = 2 * CONV_CH + SSM_CH
N_EXPERTS = 64
TOP_K = 8
N_ROUTE_GROUPS = 8
TOPK_ROUTE_GROUPS = 4
D_EXPERT = 256
ROUTED_SCALE = 2.5
NORM_EPS = 1e-6

SUBLANES = 8
LANES = 128

N_MOD = 6
MOD_ROWS = SUBLANES
ADA_COLS = 1536
TM_IN = 1024
IN_SUBTILES = 4
TL_CONV = 512
CONV_HALO = 32
CONV_ROWS = 64
EXPERTS_PER_CONV_STEP = N_EXPERTS * TL_CONV // N_TOK
assert EXPERTS_PER_CONV_STEP * N_TOK == N_EXPERTS * TL_CONV
S5_Q = 32
S5_GROUPS_PER_STEP = 4
S5_QH = S5_Q * SSM_GROUP_CH
S5_CHUNKS = N_TOK // S5_Q
S5_CHUNKS_PER_SEQ = SEQ // S5_Q
TM_MIX = 512
ROW_BLOCK = 512
EXPERT_AHEAD = 4
EXPERT_SLOTS = EXPERT_AHEAD + 1
HALF_TOK = SEQ
N_HALVES = N_TOK // HALF_TOK
N_BLOCKS = HALF_TOK * TOP_K // ROW_BLOCK + N_EXPERTS
N_ROWS = N_BLOCKS * ROW_BLOCK
TM_OUT = 512
SC_CORES = 2
SC_SUBCORES = 16
SC_WORKERS = SC_CORES * SC_SUBCORES
SC_W = 64
SC_CHUNKS_PER_WORKER = HALF_TOK // (SC_WORKERS * SC_W)
VMEM_LIMIT = 48 * 1024 * 1024


def _cparams(sem):
    return pltpu.CompilerParams(dimension_semantics=sem, vmem_limit_bytes=VMEM_LIMIT)


def _pack_rounded_halves(xr):
    n = xr.shape[-1] // 2
    lo = lax.bitcast_convert_type(xr[:, :n], jnp.uint32)
    hi = lax.bitcast_convert_type(xr[:, n:], jnp.uint32)
    return hi | (lo >> 16)


def _pack_halves(x):
    return _pack_rounded_halves(x.astype(BF16).astype(F32))


def _unpack_halves(p):
    lo = lax.bitcast_convert_type(p << 16, F32)
    hi = lax.bitcast_convert_type(p & jnp.uint32(0xFFFF0000), F32)
    return lo, hi


def _rms(x, g):
    return x * lax.rsqrt(jnp.mean(x * x, axis=-1, keepdims=True) + NORM_EPS) * g


def _split_bf16(x):
    hi = x.astype(BF16)
    return hi, (x - hi.astype(F32)).astype(BF16)


def _dot_nt_split(a, b):
    nt = (((1,), (1,)), ((), ()))
    a_hi, a_lo = _split_bf16(a)
    b_hi, b_lo = _split_bf16(b)
    m = a.shape[0]
    both = lax.dot_general(jnp.concatenate([a_hi, a_lo], axis=0), b_hi, nt, preferred_element_type=F32)
    return both[:m] + both[m:] + lax.dot_general(a_hi, b_lo, nt, preferred_element_type=F32)


def _ada_kernel(c_ref, w_ref, b_ref, o_ref):
    c = c_ref[...]
    a = c * jax.nn.sigmoid(c)
    o_ref[...] = jnp.dot(a, w_ref[...], preferred_element_type=F32,
                         precision=lax.Precision.HIGHEST) + b_ref[...]


def _ada(c_pad, w_ada, b_ada):
    n = w_ada.shape[1]
    bn = ADA_COLS
    return pl.pallas_call(
        _ada_kernel,
        grid=(n // bn,),
        in_specs=[pl.BlockSpec((SUBLANES, D_MODEL), lambda j: (0, 0)),
                  pl.BlockSpec((D_MODEL, bn), lambda j: (0, j)),
                  pl.BlockSpec((1, bn), lambda j: (0, j))],
        out_specs=pl.BlockSpec((SUBLANES, bn), lambda j: (0, j)),
        out_shape=jax.ShapeDtypeStruct((SUBLANES, n), F32),
        compiler_params=_cparams(("arbitrary",)),
        name="ada_mod",
    )(c_pad, w_ada, b_ada)


GROUPS_PER_LANE_TILE = LANES // SSM_GROUP_CH


def _to_group_chunks(u, tile_ref, ut_ref):
    n_chunks = u.shape[0] // S5_Q
    for j in range(SSM_CH // LANES):
        tile_ref[j] = u[:, LANES * j:LANES * (j + 1)]
    for j in range(SSM_CH // LANES):
        rows_t = [tile_ref[j, pl.ds(t, n_chunks, stride=S5_Q), :] for t in range(S5_Q)]
        for gg in range(GROUPS_PER_LANE_TILE):
            lo = gg * SSM_GROUP_CH
            row = jnp.concatenate([r[:, lo:lo + SSM_GROUP_CH] for r in rows_t], axis=1)
            ut_ref[j * GROUPS_PER_LANE_TILE + gg] = row.astype(ut_ref.dtype)


def _from_group_chunks(yt_ref, tile_ref):
    n_chunks = yt_ref.shape[1]
    for j in range(SSM_CH // LANES):
        for t in range(S5_Q):
            lo = t * SSM_GROUP_CH
            piece = jnp.concatenate(
                [yt_ref[j * GROUPS_PER_LANE_TILE + gg, :, lo:lo + SSM_GROUP_CH]
                 for gg in range(GROUPS_PER_LANE_TILE)], axis=1)
            tile_ref[j, pl.ds(t, n_chunks, stride=S5_Q), :] = piece
    return jnp.concatenate([tile_ref[j] for j in range(SSM_CH // LANES)], axis=1)


def _inproj_kernel(x_ref, mod_ref, g_ref, w_ref, v_ref, u_ref, ut_ref, tile_ref):
    sh = mod_ref[0, 0:1, :]
    sc = mod_ref[0, 1:2, :]
    sub = TM_IN // IN_SUBTILES
    sub_chunks = sub // S5_Q
    for s in range(IN_SUBTILES):
        r = slice(s * sub, (s + 1) * sub)
        h = _rms(x_ref[r, :], g_ref[...]) * (1.0 + sc) + sh
        z = jnp.dot(h.astype(BF16), w_ref[...], preferred_element_type=F32)
        v_ref[r, :] = z[:, :CONV_CH] * jax.nn.sigmoid(z[:, CONV_CH:2 * CONV_CH])
        u = z[:, 2 * CONV_CH:]
        u_ref[r, :] = u
        _to_group_chunks(u, tile_ref.at[s], ut_ref.at[:, s * sub_chunks:(s + 1) * sub_chunks, :])


def _inproj(x2, mod, g_pre, w_in_bf):
    tiles_per_seq = SEQ // TM_IN
    return pl.pallas_call(
        _inproj_kernel,
        grid=(N_TOK // TM_IN,),
        in_specs=[pl.BlockSpec((TM_IN, D_MODEL), lambda i: (i, 0)),
                  pl.BlockSpec((1, MOD_ROWS, D_MODEL), lambda i: (i // tiles_per_seq, 0, 0)),
                  pl.BlockSpec((1, D_MODEL), lambda i: (0, 0)),
                  pl.BlockSpec((D_MODEL, D_IN), lambda i: (0, 0))],
        out_specs=[pl.BlockSpec((TM_IN, CONV_CH), lambda i: (i, 0)),
                   pl.BlockSpec((TM_IN, SSM_CH), lambda i: (i, 0)),
                   pl.BlockSpec((SSM_GROUPS, TM_IN // S5_Q, S5_QH), lambda i: (0, i, 0))],
        out_shape=[jax.ShapeDtypeStruct((N_TOK, CONV_CH), F32),
                   jax.ShapeDtypeStruct((N_TOK, SSM_CH), F32),
                   jax.ShapeDtypeStruct((SSM_GROUPS, S5_CHUNKS, S5_QH), BF16)],
        scratch_shapes=[pltpu.VMEM((IN_SUBTILES, SSM_CH // LANES, TM_IN // IN_SUBTILES, LANES), F32)],
        compiler_params=_cparams(("parallel",)),
        name="in_proj",
    )(x2, mod, g_pre, w_in_bf)


def _conv_kernel(vc_ref, vp_ref, w_ref, cb_ref, lg_ref, lb_ref, go_ref, wg_ref, wu_ref, wd_ref,
                 o_ref, wgu_o, wd_o, sh_ref):
    for q in range(EXPERTS_PER_CONV_STEP):
        wgu_o[q, :, :D_EXPERT] = wg_ref[q].astype(BF16)
        wgu_o[q, :, D_EXPERT:] = wu_ref[q].astype(BF16)
        wd_o[q] = wd_ref[q].astype(BF16)

    i = pl.program_id(1)
    keep = (i > 0).astype(F32)
    n_ext = TL_CONV + CONV_HALO
    sh_ref[0, 0:CONV_HALO, :] = vp_ref[0] * keep
    sh_ref[0, CONV_HALO:, :] = vc_ref[0]
    ext = sh_ref[0]
    for s in range(1, SUBLANES):
        sh_ref[s] = pltpu.roll(ext, n_ext - s, axis=0)
    off = CONV_HALO - (CONV_WIDTH - 1)
    for r in range(TL_CONV // CONV_ROWS):
        acc = None
        for j in range(CONV_WIDTH):
            s = (off + j) % SUBLANES
            al = r * CONV_ROWS + (off + j) - s
            term = w_ref[j:j + 1, :] * sh_ref[s, al:al + CONV_ROWS, :]
            acc = term if acc is None else acc + term
        y = acc + cb_ref[...]
        mu = jnp.mean(y, axis=-1, keepdims=True)
        d = y - mu
        var = jnp.mean(d * d, axis=-1, keepdims=True)
        yn = d * lax.rsqrt(var + NORM_EPS) * lg_ref[...] + lb_ref[...]
        a = yn * jax.nn.sigmoid(yn)
        o_ref[0, r * CONV_ROWS:(r + 1) * CONV_ROWS, :] = _rms(a, go_ref[...]).astype(BF16)


def _conv(v3, conv_w, conv_b, ln_g, ln_b, g_out, we_gate, we_up, we_down):
    halo_per_tile = TL_CONV // CONV_HALO
    steps_per_seq = SEQ // TL_CONV
    vec = pl.BlockSpec((1, CONV_CH), lambda b, i: (0, 0))
    ex = EXPERTS_PER_CONV_STEP
    w_in = pl.BlockSpec((ex, D_MODEL, D_EXPERT), lambda b, i: (b * steps_per_seq + i, 0, 0))
    return pl.pallas_call(
        _conv_kernel,
        grid=(BATCH, steps_per_seq),
        in_specs=[pl.BlockSpec((1, TL_CONV, CONV_CH), lambda b, i: (b, i, 0)),
                  pl.BlockSpec((1, CONV_HALO, CONV_CH),
                               lambda b, i: (b, jnp.maximum(i * halo_per_tile - 1, 0), 0)),
                  pl.BlockSpec((CONV_WIDTH + 1, CONV_CH), lambda b, i: (0, 0)),
                  vec, vec, vec, vec,
                  w_in, w_in,
                  pl.BlockSpec((ex, D_EXPERT, D_MODEL), lambda b, i: (b * steps_per_seq + i, 0, 0))],
        out_specs=[pl.BlockSpec((1, TL_CONV, CONV_CH), lambda b, i: (b, i, 0)),
                   pl.BlockSpec((ex, D_MODEL, 2 * D_EXPERT), lambda b, i: (b * steps_per_seq + i, 0, 0)),
                   pl.BlockSpec((ex, D_EXPERT, D_MODEL), lambda b, i: (b * steps_per_seq + i, 0, 0))],
        out_shape=[jax.ShapeDtypeStruct((BATCH, SEQ, CONV_CH), BF16),
                   jax.ShapeDtypeStruct((N_EXPERTS, D_MODEL, 2 * D_EXPERT), BF16),
                   jax.ShapeDtypeStruct((N_EXPERTS, D_EXPERT, D_MODEL), BF16)],
        scratch_shapes=[pltpu.VMEM((SUBLANES, TL_CONV + CONV_HALO, CONV_CH), F32)],
        compiler_params=_cparams(("parallel", "arbitrary")),
        name="conv_module",
    )(v3, v3, conv_w, conv_b, ln_g, ln_b, g_out, we_gate, we_up, we_down)


S5_GROUP_ROWS = S5_CHUNKS + 8


S5_POW_ROWS = (S5_Q + 1 + SUBLANES - 1) // SUBLANES * SUBLANES
(S5_BB_RI, S5_BB_NIR, S5_BB_IR, S5_BB_RNI, S5_CC_RI, S5_CC_NIR, S5_N_PARAM) = range(7)


def _s5_kernel(ut_ref, pwr_ref, pwi_ref, par_ref, a_ref, bq_ref, bs_ref, yt_ref, sin_s, sp_s):
    phase = pl.program_id(0)
    g = pl.program_id(1)
    q = S5_Q
    n = 2 * SSM_STATE

    def group_row0(k):
        return pl.multiple_of((g * S5_GROUPS_PER_STEP + k) * S5_GROUP_ROWS, SUBLANES)

    def lam_pow(k, j):
        return pwr_ref[k, j:j + 1, :], pwi_ref[k, j:j + 1, :]

    @pl.when(phase == 0)
    def _():
        for k in range(S5_GROUPS_PER_STEP):
            bb_ri, bb_nir = par_ref[k, S5_BB_RI], par_ref[k, S5_BB_NIR]
            bb_ir, bb_rni = par_ref[k, S5_BB_IR], par_ref[k, S5_BB_RNI]
            blk_q, blk_s = [], []
            for t in range(q):
                pr, pi_ = lam_pow(k, q - 1 - t)
                blk_q.append(pr * bb_ri + pi_ * bb_nir)
                blk_s.append(pr * bb_ir + pi_ * bb_rni)
            wst = jnp.concatenate([jnp.concatenate(blk_q, axis=0), jnp.concatenate(blk_s, axis=0)], axis=1)
            r = jnp.dot(ut_ref[k], wst.astype(BF16), preferred_element_type=F32)
            sin_s[0, pl.ds(group_row0(k), S5_CHUNKS), :] = r[:, :n]
            sin_s[1, pl.ds(group_row0(k), S5_CHUNKS), :] = r[:, n:]

    @pl.when((phase == 1) & (g == 0))
    def _():
        a = a_ref[...]
        bq = bq_ref[...]
        bs = bs_ref[...]

        def body(c, carry):
            nxt = []
            for b in range(BATCH):
                x, xs = carry[b]
                rows = pl.ds(b * S5_CHUNKS_PER_SEQ + c, SSM_GROUPS, stride=S5_GROUP_ROWS)
                sp_s[rows, :] = x
                nxt.append((a * x + bq * xs + sin_s[0, rows, :], a * xs + bs * x + sin_s[1, rows, :]))
            return tuple(nxt)

        z = jnp.zeros((SSM_GROUPS, n), F32)
        lax.fori_loop(0, S5_CHUNKS_PER_SEQ, body, tuple((z, z) for _ in range(BATCH)))

    @pl.when(phase == 1)
    def _():
        for k in range(S5_GROUPS_PER_STEP):
            cc_ri, cc_nir = par_ref[k, S5_CC_RI], par_ref[k, S5_CC_NIR]
            cl = []
            for j in range(q + 1):
                pr, pi_ = lam_pow(k, j)
                cl.append(pr * cc_ri + pi_ * cc_nir)
            cl_lo = jnp.concatenate(cl[:q], axis=0)
            cl_hi = jnp.concatenate(cl[1:], axis=0)
            lane = lax.broadcasted_iota(jnp.int32, (1, n), 1)
            vgt = (cl_hi * jnp.where(lane < SSM_STATE, 1.0, -1.0)).astype(BF16)
            kt = _dot_nt_split(par_ref[k, S5_BB_RNI], cl_lo)
            padded = jnp.concatenate([jnp.zeros_like(kt), kt], axis=1)
            tg = jnp.concatenate(
                [padded[:, (q - t) * SSM_GROUP_CH:(q - t) * SSM_GROUP_CH + S5_QH] for t in range(q)],
                axis=0).astype(BF16)
            sp = sp_s[pl.ds(group_row0(k), S5_CHUNKS), :]
            y = jnp.dot(ut_ref[k], tg, preferred_element_type=F32)
            yt_ref[k] = y + lax.dot_general(sp.astype(BF16), vgt, (((1,), (1,)), ((), ())),
                                            preferred_element_type=F32)


def _s5(ut, pwr, pwi, params, a_cat, b_q, b_s):
    vec = pl.BlockSpec((SSM_GROUPS, 2 * SSM_STATE), lambda p, g: (0, 0))
    gs = S5_GROUPS_PER_STEP
    powers = pl.BlockSpec((gs, S5_POW_ROWS, 2 * SSM_STATE), lambda p, g: (g, 0, 0))
    return pl.pallas_call(
        _s5_kernel,
        grid=(2, SSM_GROUPS // gs),
        in_specs=[pl.BlockSpec((gs, S5_CHUNKS, S5_QH), lambda p, g: (g, 0, 0)),
                  powers, powers,
                  pl.BlockSpec((gs, S5_N_PARAM, SSM_GROUP_CH, 2 * SSM_STATE), lambda p, g: (g, 0, 0, 0)),
                  vec, vec, vec],
        out_specs=pl.BlockSpec((gs, S5_CHUNKS, S5_QH), lambda p, g: (g * p, 0, 0)),
        out_shape=jax.ShapeDtypeStruct((SSM_GROUPS, S5_CHUNKS, S5_QH), F32),
        scratch_shapes=[pltpu.VMEM((2, SSM_GROUPS * S5_GROUP_ROWS, 2 * SSM_STATE), F32),
                        pltpu.VMEM((SSM_GROUPS * S5_GROUP_ROWS, 2 * SSM_STATE), F32)],
        compiler_params=_cparams(("arbitrary", "arbitrary")),
        name="s5_chunked",
    )(ut, pwr, pwi, params, a_cat, b_q, b_s)


def _s5_operators(a_re, a_im, log_dt, b_re, b_im, c_re, c_im):
    q = S5_Q
    dt = jnp.exp(log_dt)[:, None]
    ar, ai = a_re, a_im
    mag = jnp.exp(ar * dt)
    lr = mag * jnp.cos(ai * dt)
    li = mag * jnp.sin(ai * dt)
    den = ar * ar + ai * ai
    nr = lr - 1.0
    kr = (nr * ar + li * ai) / den
    ki = (li * ar - nr * ai) / den
    bbr = kr[..., None] * b_re - ki[..., None] * b_im
    bbi = kr[..., None] * b_im + ki[..., None] * b_re
    j = jnp.arange(q + 1, dtype=F32)[None, :, None]
    pmag = jnp.exp(ar[:, None, :] * dt[:, :, None] * j)
    pang = ai[:, None, :] * dt[:, :, None] * j
    pr = pmag * jnp.cos(pang)
    pi_ = pmag * jnp.sin(pang)
    pad = ((0, 0), (0, S5_POW_ROWS - (q + 1)), (0, 0))
    pwr = jnp.pad(jnp.concatenate([pr, pr], axis=-1), pad)
    pwi = jnp.pad(jnp.concatenate([pi_, pi_], axis=-1), pad)
    br_t = bbr.transpose(0, 2, 1)
    bi_t = bbi.transpose(0, 2, 1)
    cat = lambda a, b: jnp.concatenate([a, b], axis=-1)
    stack = [None] * S5_N_PARAM
    stack[S5_BB_RI] = cat(br_t, bi_t)
    stack[S5_BB_NIR] = cat(-bi_t, br_t)
    stack[S5_BB_IR] = cat(bi_t, br_t)
    stack[S5_BB_RNI] = cat(br_t, -bi_t)
    stack[S5_CC_RI] = cat(c_re, c_im)
    stack[S5_CC_NIR] = cat(-c_im, c_re)
    params = jnp.stack(stack, axis=1)
    aq_r, aq_i = pr[:, q], pi_[:, q]
    a_cat = cat(aq_r, aq_r)
    b_q = cat(-aq_i, aq_i)
    b_s = cat(aq_i, -aq_i)
    return pwr, pwi, params, a_cat, b_q, b_s


def _gelu_tanh(x):
    return 0.5 * x * (1.0 + jnp.tanh(math.sqrt(2.0 / math.pi) * (x + 0.044715 * (x * x * x))))


def _mix_kernel(an_ref, yt_ref, u_ref, x_ref, mod_ref, d_ref, wglu_ref, bglu_ref, gos_ref,
                woa_ref, wob_ref, gpm_ref, gpf_ref, wr_ref, br_ref, tri_ref,
                x1_ref, h2_ref, eidx_ref, rank_ref, gw_ref, cnt_ref, run_ref, tile_ref):
    i = pl.program_id(0)
    tm = TM_MIX

    @pl.when(i == 0)
    def _():
        run_ref[...] = jnp.zeros_like(run_ref)

    gt_m = mod_ref[0, 2:3, :]
    sh_f = mod_ref[0, 3:4, :]
    sc_f = mod_ref[0, 4:5, :]

    yy = _from_group_chunks(yt_ref, tile_ref) + d_ref[...] * u_ref[...]
    g = _gelu_tanh(yy)
    gl = jnp.dot(g.astype(BF16), wglu_ref[...], preferred_element_type=F32) + bglu_ref[...]
    ob = g * jax.nn.sigmoid(gl)
    bn = _rms(ob, gos_ref[...]).astype(BF16)
    o = (jnp.dot(an_ref[...], woa_ref[...], preferred_element_type=F32)
         + jnp.dot(bn, wob_ref[...], preferred_element_type=F32))
    x1 = x_ref[...] + gt_m * _rms(o, gpm_ref[...])
    x1_ref[...] = x1
    h2 = _rms(x1, gpf_ref[...]) * (1.0 + sc_f) + sh_f
    h2_hi = h2.astype(BF16)
    h2_hi32 = h2_hi.astype(F32)
    h2_ref[...] = _pack_rounded_halves(h2_hi32)

    h2_lo = (h2 - h2_hi32).astype(BF16)
    nt = (((1,), (1,)), ((), ()))
    both = lax.dot_general(wr_ref[...], h2_hi, nt, preferred_element_type=F32)
    logits = (both[:N_EXPERTS] + both[N_EXPERTS:]
              + lax.dot_general(wr_ref[:N_EXPERTS, :], h2_lo, nt, preferred_element_type=F32))
    scores = jax.nn.sigmoid(logits)
    biased = scores + br_ref[...]
    ng = N_ROUTE_GROUPS
    gsz = N_EXPERTS // ng
    b3 = biased.reshape(ng, gsz, tm)
    s3 = scores.reshape(ng, gsz, tm)
    sub = lax.broadcasted_iota(jnp.int32, (ng, gsz, tm), 1).astype(F32)
    grp = lax.broadcasted_iota(jnp.int32, (ng, gsz, tm), 0).astype(F32)
    eid = grp * gsz + sub
    neg = -jnp.inf
    m1 = jnp.max(b3, axis=1, keepdims=True)
    i1 = jnp.min(jnp.where(b3 == m1, sub, float(gsz)), axis=1, keepdims=True)
    m2 = jnp.max(jnp.where(sub == i1, neg, b3), axis=1, keepdims=True)
    gs = m1 + m2
    gi = lax.broadcasted_iota(jnp.int32, (ng, 1, tm), 0)
    beaten = jnp.zeros((ng, 1, tm), F32)
    for gp in range(ng):
        o_ = gs[gp:gp + 1]
        beats = (o_ > gs) | ((o_ == gs) & (gi > gp))
        beaten = beaten + beats.astype(F32)
    gmask = beaten < float(TOPK_ROUTE_GROUPS)
    masked = jnp.where(gmask, b3, neg)

    sels = []
    picked = jnp.zeros((ng, gsz, tm), F32)
    for k in range(TOP_K):
        m = jnp.max(jnp.max(masked, axis=0, keepdims=True), axis=1, keepdims=True)
        cand = jnp.where(masked == m, eid, float(N_EXPERTS))
        sel = jnp.min(jnp.min(cand, axis=0, keepdims=True), axis=1, keepdims=True)
        oh = eid == sel
        masked = jnp.where(oh, neg, masked)
        picked = jnp.where(oh, 1.0, picked)
        sels.append(sel)

    pm = picked.reshape(N_EXPERTS, tm)
    prefix = jnp.dot(pm.astype(BF16), tri_ref[...], preferred_element_type=F32) + run_ref[:, 0:1]
    p3 = prefix.reshape(ng, gsz, tm)
    run_new = run_ref[...] + jnp.sum(pm, axis=1, keepdims=True)
    run_ref[...] = run_new
    cnt_ref[...] = run_new

    sc_rows = []
    for k in range(TOP_K):
        oh = eid == sels[k]
        sc_k = jnp.sum(jnp.sum(jnp.where(oh, s3, 0.0), axis=0, keepdims=True), axis=1, keepdims=True)
        rk_k = jnp.sum(jnp.sum(jnp.where(oh, p3, 0.0), axis=0, keepdims=True), axis=1, keepdims=True)
        sc_rows.append(sc_k)
        eidx_ref[k:k + 1, :] = sels[k].reshape(1, tm).astype(jnp.int32)
        rank_ref[k:k + 1, :] = rk_k.reshape(1, tm).astype(jnp.int32)
    tot = sc_rows[0]
    for k in range(1, TOP_K):
        tot = tot + sc_rows[k]
    inv = ROUTED_SCALE / (tot + 1e-20)
    for k in range(TOP_K):
        gw_ref[k:k + 1, :] = (sc_rows[k] * inv).reshape(1, tm)


def _mix(half, a_n, yt, u2, x2, mod, d_skip, wglu_bf, b_glu, g_out_ssm, wo_a, wo_b, g_post_mix,
         g_pre_ffn, w_router_split, b_router_col, tri):
    tm = TM_MIX
    t0 = half * (HALF_TOK // tm)
    row = lambda n: pl.BlockSpec((1, n), lambda i: (0, 0))
    full = lambda a, b: pl.BlockSpec((a, b), lambda i: (0, 0))
    tok_in = lambda n: pl.BlockSpec((tm, n), lambda i: (t0 + i, 0))
    tok = lambda n: pl.BlockSpec((tm, n), lambda i: (i, 0))
    col = pl.BlockSpec((TOP_K, tm), lambda i: (0, i))
    return pl.pallas_call(
        _mix_kernel,
        grid=(HALF_TOK // tm,),
        in_specs=[tok_in(CONV_CH),
                  pl.BlockSpec((SSM_GROUPS, tm // S5_Q, S5_QH), lambda i: (0, t0 + i, 0)),
                  tok_in(SSM_CH), tok_in(D_MODEL),
                  pl.BlockSpec((1, MOD_ROWS, D_MODEL), lambda i: (half, 0, 0)),
                  row(SSM_CH), full(SSM_CH, SSM_CH), row(SSM_CH), row(SSM_CH),
                  full(CONV_CH, D_MODEL), full(SSM_CH, D_MODEL), row(D_MODEL), row(D_MODEL),
                  full(2 * N_EXPERTS, D_MODEL), full(N_EXPERTS, 1), full(tm, tm)],
        out_specs=[tok(D_MODEL), tok(D_MODEL // 2), col, col, col,
                   pl.BlockSpec((N_EXPERTS, LANES), lambda i: (0, 0))],
        out_shape=[jax.ShapeDtypeStruct((HALF_TOK, D_MODEL), F32),
                   jax.ShapeDtypeStruct((HALF_TOK, D_MODEL // 2), jnp.uint32),
                   jax.ShapeDtypeStruct((TOP_K, HALF_TOK), jnp.int32),
                   jax.ShapeDtypeStruct((TOP_K, HALF_TOK), jnp.int32),
                   jax.ShapeDtypeStruct((TOP_K, HALF_TOK), F32),
                   jax.ShapeDtypeStruct((N_EXPERTS, LANES), F32)],
        scratch_shapes=[pltpu.VMEM((N_EXPERTS, LANES), F32),
                        pltpu.VMEM((SSM_CH // LANES, tm, LANES), F32)],
        compiler_params=_cparams(("arbitrary",)),
        name="mix_out_router",
    )(a_n, yt, u2, x2, mod, d_skip, wglu_bf, b_glu, g_out_ssm, wo_a, wo_b, g_post_mix,
      g_pre_ffn, w_router_split, b_router_col, tri)


def _expert_kernel(blk0_ref, nblk_ref, xs_hbm, wgu_ref, wd_ref, ys_hbm, xbuf, ybuf, sem_in, sem_out):
    e = pl.program_id(0)
    n = nblk_ref[e]
    b0 = blk0_ref[e]
    n_all = blk0_ref[N_EXPERTS - 1] + nblk_ref[N_EXPERTS - 1]

    def rows(b):
        return pl.ds(pl.multiple_of(b * ROW_BLOCK, ROW_BLOCK), ROW_BLOCK)

    def in_copy(b, slot):
        return pltpu.make_async_copy(xs_hbm.at[rows(b)], xbuf.at[slot], sem_in.at[slot])

    def out_copy(b, slot):
        return pltpu.make_async_copy(ybuf.at[slot], ys_hbm.at[rows(b)], sem_out.at[slot])

    for b in range(EXPERT_AHEAD):
        @pl.when((e == 0) & (b < n_all))
        def _():
            in_copy(b, b).start()

    def admit(b):
        in_copy(b, b % EXPERT_SLOTS).wait()
        ahead = b + EXPERT_AHEAD

        @pl.when(ahead < n_all)
        def _():
            in_copy(ahead, ahead % EXPERT_SLOTS).start()

        @pl.when(b >= EXPERT_SLOTS)
        def _():
            out_copy(b - EXPERT_SLOTS, b % EXPERT_SLOTS).wait()

    def compute(b):
        slot = b % EXPERT_SLOTS
        x_lo, x_hi = _unpack_halves(xbuf[slot])
        x = jnp.concatenate([x_lo.astype(BF16), x_hi.astype(BF16)], axis=1)
        h = jnp.dot(x, wgu_ref[0], preferred_element_type=F32)
        hg = h[:, :D_EXPERT]
        act = hg * jax.nn.sigmoid(hg) * h[:, D_EXPERT:]
        ybuf[slot] = _pack_halves(jnp.dot(act.astype(BF16), wd_ref[0], preferred_element_type=F32))

    def block(b, carry):
        admit(b)
        compute(b)
        out_copy(b, b % EXPERT_SLOTS).start()
        return carry

    lax.fori_loop(b0, b0 + n, block, 0)

    @pl.when(e == N_EXPERTS - 1)
    def _():
        for j in range(1, EXPERT_SLOTS + 1):
            @pl.when(n_all >= j)
            def _():
                out_copy(n_all - j, (n_all - j) % EXPERT_SLOTS).wait()


def _experts(blk0, nblk, xs, we_gu, we_d):
    any_spec = pl.BlockSpec(memory_space=pl.ANY)
    grid_spec = pltpu.PrefetchScalarGridSpec(
        num_scalar_prefetch=2,
        grid=(N_EXPERTS,),
        in_specs=[any_spec,
                  pl.BlockSpec((1, D_MODEL, 2 * D_EXPERT), lambda e, b0, nb: (e, 0, 0)),
                  pl.BlockSpec((1, D_EXPERT, D_MODEL), lambda e, b0, nb: (e, 0, 0))],
        out_specs=any_spec,
        scratch_shapes=[pltpu.VMEM((EXPERT_SLOTS, ROW_BLOCK, D_MODEL // 2), jnp.uint32),
                        pltpu.VMEM((EXPERT_SLOTS, ROW_BLOCK, D_MODEL // 2), jnp.uint32),
                        pltpu.SemaphoreType.DMA((EXPERT_SLOTS,)),
                        pltpu.SemaphoreType.DMA((EXPERT_SLOTS,))],
    )
    return pl.pallas_call(
        _expert_kernel,
        grid_spec=grid_spec,
        out_shape=jax.ShapeDtypeStruct((N_ROWS, D_MODEL // 2), jnp.uint32),
        compiler_params=_cparams(("arbitrary",)),
        name="routed_experts",
    )(blk0, nblk, xs, we_gu, we_d)


def _final_kernel(h2_ref, yg_ref, gw_ref, x1_ref, mod_ref, wgu_ref, wd_ref, g_ref, *rest):
    o_ref = rest[-1]
    half = D_MODEL // 2
    gt_f = mod_ref[0, 5:6, :]
    x_lo, x_hi = _unpack_halves(h2_ref[...])
    h = (jnp.dot(x_lo.astype(BF16), wgu_ref[:half, :], preferred_element_type=F32)
         + jnp.dot(x_hi.astype(BF16), wgu_ref[half:, :], preferred_element_type=F32))
    hg = h[:, :D_EXPERT]
    act = hg * jax.nn.sigmoid(hg) * h[:, D_EXPERT:]
    shared = jnp.dot(act.astype(BF16), wd_ref[...], preferred_element_type=F32)
    y_lo = shared[:, :half]
    y_hi = shared[:, half:]
    for k in range(TOP_K):
        r_lo, r_hi = _unpack_halves(yg_ref[k])
        w = gw_ref[:, k:k + 1]
        y_lo = y_lo + w * r_lo
        y_hi = y_hi + w * r_hi
    ms = (jnp.sum(y_lo * y_lo, axis=-1, keepdims=True)
          + jnp.sum(y_hi * y_hi, axis=-1, keepdims=True)) * (1.0 / D_MODEL)
    inv = lax.rsqrt(ms + NORM_EPS)
    o_ref[:, :half] = x1_ref[:, :half] + gt_f[:, :half] * (y_lo * inv * g_ref[:, :half])
    o_ref[:, half:] = x1_ref[:, half:] + gt_f[:, half:] * (y_hi * inv * g_ref[:, half:])


def _final(half, out_prev, h2p, yg, gw_t, x1, mod, ws_gu, ws_d, g_post_ffn):
    tm = TM_OUT
    t0 = half * (HALF_TOK // tm)
    tok = pl.BlockSpec((tm, D_MODEL), lambda i: (i, 0))
    in_specs = [pl.BlockSpec((tm, D_MODEL // 2), lambda i: (i, 0)),
                pl.BlockSpec((TOP_K, tm, D_MODEL // 2), lambda i: (0, i, 0)),
                pl.BlockSpec((tm, TOP_K), lambda i: (i, 0)),
                tok,
                pl.BlockSpec((1, MOD_ROWS, D_MODEL), lambda i: (half, 0, 0)),
                pl.BlockSpec((D_MODEL, 2 * D_EXPERT), lambda i: (0, 0)),
                pl.BlockSpec((D_EXPERT, D_MODEL), lambda i: (0, 0)),
                pl.BlockSpec((1, D_MODEL), lambda i: (0, 0))]
    args = [h2p, yg, gw_t, x1, mod, ws_gu, ws_d, g_post_ffn]
    aliases = {}
    if out_prev is not None:
        aliases = {len(args): 0}
        in_specs.append(pl.BlockSpec(memory_space=pl.ANY))
        args.append(out_prev)
    return pl.pallas_call(
        _final_kernel,
        grid=(HALF_TOK // tm,),
        in_specs=in_specs,
        out_specs=pl.BlockSpec((tm, D_MODEL), lambda i: (t0 + i, 0)),
        out_shape=jax.ShapeDtypeStruct((N_TOK, D_MODEL), F32),
        input_output_aliases=aliases,
        compiler_params=_cparams(("parallel",)),
        name="shared_final",
    )(*args)


def _sc_worker_id():
    return lax.axis_index("s") * SC_CORES + lax.axis_index("c")


def _dispatch_body(h_hbm, dest_hbm, xs_hbm, idx_v, rows_v, sem_l, sem_s):
    n = SC_CHUNKS_PER_WORKER
    c0 = _sc_worker_id() * n

    def load(i, b):
        return pltpu.async_copy(h_hbm.at[pl.ds((c0 + i) * SC_W, SC_W)], rows_v.at[b], sem_l.at[b])

    loads = [None] * n
    scat = [None] * n
    loads[0] = load(0, 0)
    for i in range(n):
        b = i % 2
        pltpu.sync_copy(dest_hbm.at[c0 + i], idx_v.at[b])
        loads[i].wait()
        if i + 1 < n:
            if i >= 1:
                for d in scat[i - 1]:
                    d.wait()
            loads[i + 1] = load(i + 1, 1 - b)
        scat[i] = [pltpu.async_copy(rows_v.at[b], xs_hbm.at[idx_v.at[b].at[k]], sem_s.at[b])
                   for k in range(TOP_K)]
    for i in (n - 2, n - 1):
        for d in scat[i]:
            d.wait()


def _sc_dispatch(h2p, dest3):
    mesh = plsc.VectorSubcoreMesh(core_axis_name="c", subcore_axis_name="s")
    return pl.kernel(
        _dispatch_body, mesh=mesh,
        out_type=jax.ShapeDtypeStruct((N_ROWS, D_MODEL // 2), jnp.uint32),
        scratch_types=[pltpu.VMEM((2, TOP_K, SC_W), jnp.int32),
                       pltpu.VMEM((2, SC_W, D_MODEL // 2), jnp.uint32),
                       pltpu.SemaphoreType.DMA((2,)), pltpu.SemaphoreType.DMA((2,))],
    )(h2p, dest3)


def _combine_body(ys_hbm, dest_hbm, yg_hbm, idx_v, rows_v, sem_g, sem_w):
    c0 = _sc_worker_id() * SC_CHUNKS_PER_WORKER

    @pl.loop(0, SC_CHUNKS_PER_WORKER)
    def _(i):
        c = c0 + i
        pltpu.sync_copy(dest_hbm.at[c], idx_v)
        g = [None] * TOP_K
        w = [None] * TOP_K
        g[0] = pltpu.async_copy(ys_hbm.at[idx_v.at[0]], rows_v.at[0], sem_g.at[0])
        for k in range(TOP_K):
            b = k % 2
            g[k].wait()
            if k + 1 < TOP_K:
                if k >= 1:
                    w[k - 1].wait()
                g[k + 1] = pltpu.async_copy(ys_hbm.at[idx_v.at[k + 1]], rows_v.at[1 - b], sem_g.at[1 - b])
            w[k] = pltpu.async_copy(rows_v.at[b], yg_hbm.at[k].at[pl.ds(c * SC_W, SC_W)], sem_w.at[b])
        w[TOP_K - 2].wait()
        w[TOP_K - 1].wait()


def _sc_combine(ysp, dest3):
    mesh = plsc.VectorSubcoreMesh(core_axis_name="c", subcore_axis_name="s")
    return pl.kernel(
        _combine_body, mesh=mesh,
        out_type=jax.ShapeDtypeStruct((TOP_K, HALF_TOK, D_MODEL // 2), jnp.uint32),
        scratch_types=[pltpu.VMEM((TOP_K, SC_W), jnp.int32),
                       pltpu.VMEM((2, SC_W, D_MODEL // 2), jnp.uint32),
                       pltpu.SemaphoreType.DMA((2,)), pltpu.SemaphoreType.DMA((2,))],
    )(ysp, dest3)


def kernel(x, c, w_ada, b_ada, g_pre_mix, g_post_mix, w_in, conv_w, conv_b, conv_ln_g, conv_ln_b,
           ssm_a_re, ssm_a_im, ssm_log_dt, ssm_b_re, ssm_b_im, ssm_c_re, ssm_c_im, ssm_d,
           ssm_w_glu, ssm_b_glu, g_out_conv, g_out_ssm, w_out, g_pre_ffn, g_post_ffn,
           w_router, b_router, we_gate, we_up, we_down, ws_gate, ws_up, ws_down):
    l = 0
    x2 = x.reshape(N_TOK, D_MODEL)
    r1 = lambda a: a.reshape(1, -1)

    c_pad = jnp.zeros((SUBLANES, D_MODEL), F32).at[:BATCH].set(c)
    mod = _ada(c_pad, w_ada[l], r1(b_ada[l]))[:BATCH].reshape(BATCH, N_MOD, D_MODEL)
    mod = jnp.concatenate([mod, jnp.zeros((BATCH, MOD_ROWS - N_MOD, D_MODEL), F32)], axis=1)

    v, u, ut = _inproj(x2, mod, r1(g_pre_mix[l]), w_in[l].astype(BF16))
    cw = jnp.concatenate([conv_w[l].reshape(CONV_WIDTH, CONV_CH), jnp.zeros((1, CONV_CH), F32)], axis=0)
    a_n, we_gu, we_d = _conv(v.reshape(BATCH, SEQ, CONV_CH), cw, r1(conv_b[l]), r1(conv_ln_g[l]),
                             r1(conv_ln_b[l]), r1(g_out_conv[l]), we_gate[l], we_up[l], we_down[l])
    a_n = a_n.reshape(N_TOK, CONV_CH)

    pwr, pwi, s5_params, a_cat, b_q, b_s = _s5_operators(
        ssm_a_re[l], ssm_a_im[l], ssm_log_dt[l], ssm_b_re[l], ssm_b_im[l], ssm_c_re[l], ssm_c_im[l])
    yt = _s5(ut, pwr, pwi, s5_params, a_cat, b_q, b_s)

    tm = TM_MIX
    tri = (jnp.arange(tm)[:, None] < jnp.arange(tm)[None, :]).astype(BF16)
    wo = w_out[l].astype(BF16)
    wr_t = w_router[l].T
    wr_hi = wr_t.astype(BF16)
    wr_split = jnp.concatenate([wr_hi, (wr_t - wr_hi.astype(F32)).astype(BF16)], axis=0)
    mix_params = (r1(ssm_d[l]), ssm_w_glu[l].astype(BF16), r1(ssm_b_glu[l]), r1(g_out_ssm[l]),
                  wo[:CONV_CH], wo[CONV_CH:], r1(g_post_mix[l]), r1(g_pre_ffn[l]),
                  wr_split, b_router[l].reshape(N_EXPERTS, 1), tri)
    ws_gu = jnp.concatenate([ws_gate[l], ws_up[l]], axis=1).astype(BF16)
    ws_d = ws_down[l].astype(BF16)
    e_ids = jnp.arange(N_EXPERTS, dtype=jnp.int32)

    out = None
    for half in range(N_HALVES):
        x1, h2, eidx, rank, gw, cnt = _mix(half, a_n, yt, u, x2, mod, *mix_params)
        counts = cnt[:, 0].astype(jnp.int32)
        padded = (counts + ROW_BLOCK - 1) // ROW_BLOCK * ROW_BLOCK
        pstart = jnp.cumsum(padded) - padded
        dest = rank + jnp.sum(jnp.where(eidx[..., None] == e_ids, pstart, 0), axis=-1)
        dest3 = dest.reshape(TOP_K, HALF_TOK // SC_W, SC_W).transpose(1, 0, 2)

        xs = _sc_dispatch(h2, dest3)
        ys = _experts(pstart // ROW_BLOCK, padded // ROW_BLOCK, xs, we_gu, we_d)
        yg = _sc_combine(ys, dest3)
        out = _final(half, out, h2, yg, gw.T, x1, mod, ws_gu, ws_d, r1(g_post_ffn[l]))
    return out.reshape(BATCH, SEQ, D_MODEL)
```

```python
import math

import jax
import jax.numpy as jnp
from jax import lax
from jax.experimental import pallas as pl
from jax.experimental.pallas import tpu as pltpu
from jax.experimental.pallas import tpu_sc as plsc

F32 = jnp.float32
BF16 = jnp.bfloat16

D_MODEL = 1024
BATCH = 2
SEQ = 8192
N_TOK = BATCH * SEQ
CONV_CH = 512
CONV_WIDTH = 31
SSM_CH = 512
SSM_GROUP_CH = 16
SSM_GROUPS = 32
SSM_STATE = 64
D_IN = 2 * CONV_CH + SSM_CH
N_EXPERTS = 64
TOP_K = 8
N_ROUTE_GROUPS = 8
TOPK_ROUTE_GROUPS = 4
D_EXPERT = 256
ROUTED_SCALE = 2.5
NORM_EPS = 1e-6

SUBLANES = 8
LANES = 128

N_MOD = 6
MOD_ROWS = SUBLANES
ADA_COLS = 1536
TM_IN = 1024
IN_SUBTILES = 4
TL_CONV = 512
CONV_HALO = 32
CONV_ROWS = 64
EXPERTS_PER_CONV_STEP = N_EXPERTS * TL_CONV // N_TOK
assert EXPERTS_PER_CONV_STEP * N_TOK == N_EXPERTS * TL_CONV
S5_Q = 32
S5_GROUPS_PER_STEP = 4
S5_QH = S5_Q * SSM_GROUP_CH
S5_CHUNKS = N_TOK // S5_Q
S5_CHUNKS_PER_SEQ = SEQ // S5_Q
TM_MIX = 1024
ROW_BLOCK = 512
EXPERT_AHEAD = 4
EXPERT_SLOTS = EXPERT_AHEAD + 1
HALF_TOK = SEQ
N_HALVES = N_TOK // HALF_TOK
N_BLOCKS = HALF_TOK * TOP_K // ROW_BLOCK + N_EXPERTS
N_ROWS = N_BLOCKS * ROW_BLOCK
TM_OUT = 512
SC_CORES = 2
SC_SUBCORES = 16
SC_WORKERS = SC_CORES * SC_SUBCORES
SC_W = 64
SC_CHUNKS_PER_WORKER = HALF_TOK // (SC_WORKERS * SC_W)
VMEM_LIMIT = 48 * 1024 * 1024


def _cparams(sem):
    return pltpu.CompilerParams(dimension_semantics=sem, vmem_limit_bytes=VMEM_LIMIT)


def _pack_rounded_halves(xr):
    n = xr.shape[-1] // 2
    lo = lax.bitcast_convert_type(xr[:, :n], jnp.uint32)
    hi = lax.bitcast_convert_type(xr[:, n:], jnp.uint32)
    return hi | (lo >> 16)


def _pack_halves(x):
    return _pack_rounded_halves(x.astype(BF16).astype(F32))


def _unpack_halves(p):
    lo = lax.bitcast_convert_type(p << 16, F32)
    hi = lax.bitcast_convert_type(p & jnp.uint32(0xFFFF0000), F32)
    return lo, hi


def _rms(x, g):
    return x * lax.rsqrt(jnp.mean(x * x, axis=-1, keepdims=True) + NORM_EPS) * g


def _split_bf16(x):
    hi = x.astype(BF16)
    return hi, (x - hi.astype(F32)).astype(BF16)


def _dot_nt_split(a, b):
    nt = (((1,), (1,)), ((), ()))
    a_hi, a_lo = _split_bf16(a)
    b_hi, b_lo = _split_bf16(b)
    m = a.shape[0]
    both = lax.dot_general(jnp.concatenate([a_hi, a_lo], axis=0), b_hi, nt, preferred_element_type=F32)
    return both[:m] + both[m:] + lax.dot_general(a_hi, b_lo, nt, preferred_element_type=F32)


def _ada_kernel(c_ref, w_ref, b_ref, o_ref):
    c = c_ref[...]
    a = c * jax.nn.sigmoid(c)
    o_ref[...] = jnp.dot(a, w_ref[...], preferred_element_type=F32,
                         precision=lax.Precision.HIGHEST) + b_ref[...]


def _ada(c_pad, w_ada, b_ada):
    n = w_ada.shape[1]
    bn = ADA_COLS
    return pl.pallas_call(
        _ada_kernel,
        grid=(n // bn,),
        in_specs=[pl.BlockSpec((SUBLANES, D_MODEL), lambda j: (0, 0)),
                  pl.BlockSpec((D_MODEL, bn), lambda j: (0, j)),
                  pl.BlockSpec((1, bn), lambda j: (0, j))],
        out_specs=pl.BlockSpec((SUBLANES, bn), lambda j: (0, j)),
        out_shape=jax.ShapeDtypeStruct((SUBLANES, n), F32),
        compiler_params=_cparams(("arbitrary",)),
        name="ada_mod",
    )(c_pad, w_ada, b_ada)


GROUPS_PER_LANE_TILE = LANES // SSM_GROUP_CH


def _to_group_chunks(u, tile_ref, ut_ref):
    n_chunks = u.shape[0] // S5_Q
    for j in range(SSM_CH // LANES):
        tile_ref[j] = u[:, LANES * j:LANES * (j + 1)]
    for j in range(SSM_CH // LANES):
        rows_t = [tile_ref[j, pl.ds(t, n_chunks, stride=S5_Q), :] for t in range(S5_Q)]
        for gg in range(GROUPS_PER_LANE_TILE):
            lo = gg * SSM_GROUP_CH
            row = jnp.concatenate([r[:, lo:lo + SSM_GROUP_CH] for r in rows_t], axis=1)
            ut_ref[j * GROUPS_PER_LANE_TILE + gg] = row.astype(ut_ref.dtype)


def _from_group_chunks(yt_ref, tile_ref):
    n_chunks = yt_ref.shape[1]
    for j in range(SSM_CH // LANES):
        for t in range(S5_Q):
            lo = t * SSM_GROUP_CH
            piece = jnp.concatenate(
                [yt_ref[j * GROUPS_PER_LANE_TILE + gg, :, lo:lo + SSM_GROUP_CH]
                 for gg in range(GROUPS_PER_LANE_TILE)], axis=1)
            tile_ref[j, pl.ds(t, n_chunks, stride=S5_Q), :] = piece
    return jnp.concatenate([tile_ref[j] for j in range(SSM_CH // LANES)], axis=1)


def _inproj_kernel(x_ref, mod_ref, g_ref, w_ref, v_ref, u_ref, ut_ref, tile_ref):
    sh = mod_ref[0, 0:1, :]
    sc = mod_ref[0, 1:2, :]
    sub = TM_IN // IN_SUBTILES
    sub_chunks = sub // S5_Q
    for s in range(IN_SUBTILES):
        r = slice(s * sub, (s + 1) * sub)
        h = _rms(x_ref[r, :], g_ref[...]) * (1.0 + sc) + sh
        z = jnp.dot(h.astype(BF16), w_ref[...], preferred_element_type=F32)
        v_ref[r, :] = z[:, :CONV_CH] * jax.nn.sigmoid(z[:, CONV_CH:2 * CONV_CH])
        u = z[:, 2 * CONV_CH:]
        u_ref[r, :] = u
        _to_group_chunks(u, tile_ref.at[s], ut_ref.at[:, s * sub_chunks:(s + 1) * sub_chunks, :])


def _inproj(x2, mod, g_pre, w_in_bf):
    tiles_per_seq = SEQ // TM_IN
    return pl.pallas_call(
        _inproj_kernel,
        grid=(N_TOK // TM_IN,),
        in_specs=[pl.BlockSpec((TM_IN, D_MODEL), lambda i: (i, 0)),
                  pl.BlockSpec((1, MOD_ROWS, D_MODEL), lambda i: (i // tiles_per_seq, 0, 0)),
                  pl.BlockSpec((1, D_MODEL), lambda i: (0, 0)),
                  pl.BlockSpec((D_MODEL, D_IN), lambda i: (0, 0))],
        out_specs=[pl.BlockSpec((TM_IN, CONV_CH), lambda i: (i, 0)),
                   pl.BlockSpec((TM_IN, SSM_CH), lambda i: (i, 0)),
                   pl.BlockSpec((SSM_GROUPS, TM_IN // S5_Q, S5_QH), lambda i: (0, i, 0))],
        out_shape=[jax.ShapeDtypeStruct((N_TOK, CONV_CH), F32),
                   jax.ShapeDtypeStruct((N_TOK, SSM_CH), F32),
                   jax.ShapeDtypeStruct((SSM_GROUPS, S5_CHUNKS, S5_QH), BF16)],
        scratch_shapes=[pltpu.VMEM((IN_SUBTILES, SSM_CH // LANES, TM_IN // IN_SUBTILES, LANES), F32)],
        compiler_params=_cparams(("parallel",)),
        name="in_proj",
    )(x2, mod, g_pre, w_in_bf)


def _conv_kernel(vc_ref, vp_ref, w_ref, cb_ref, lg_ref, lb_ref, go_ref, wg_ref, wu_ref, wd_ref,
                 o_ref, wgu_o, wd_o, sh_ref):
    for q in range(EXPERTS_PER_CONV_STEP):
        wgu_o[q, :, :D_EXPERT] = wg_ref[q].astype(BF16)
        wgu_o[q, :, D_EXPERT:] = wu_ref[q].astype(BF16)
        wd_o[q] = wd_ref[q].astype(BF16)

    i = pl.program_id(1)
    keep = (i > 0).astype(F32)
    n_ext = TL_CONV + CONV_HALO
    sh_ref[0, 0:CONV_HALO, :] = vp_ref[0] * keep
    sh_ref[0, CONV_HALO:, :] = vc_ref[0]
    ext = sh_ref[0]
    for s in range(1, SUBLANES):
        sh_ref[s] = pltpu.roll(ext, n_ext - s, axis=0)
    off = CONV_HALO - (CONV_WIDTH - 1)
    for r in range(TL_CONV // CONV_ROWS):
        acc = None
        for j in range(CONV_WIDTH):
            s = (off + j) % SUBLANES
            al = r * CONV_ROWS + (off + j) - s
            term = w_ref[j:j + 1, :] * sh_ref[s, al:al + CONV_ROWS, :]
            acc = term if acc is None else acc + term
        y = acc + cb_ref[...]
        mu = jnp.mean(y, axis=-1, keepdims=True)
        d = y - mu
        var = jnp.mean(d * d, axis=-1, keepdims=True)
        yn = d * lax.rsqrt(var + NORM_EPS) * lg_ref[...] + lb_ref[...]
        a = yn * jax.nn.sigmoid(yn)
        o_ref[0, r * CONV_ROWS:(r + 1) * CONV_ROWS, :] = _rms(a, go_ref[...]).astype(BF16)


def _conv(v3, conv_w, conv_b, ln_g, ln_b, g_out, we_gate, we_up, we_down):
    halo_per_tile = TL_CONV // CONV_HALO
    steps_per_seq = SEQ // TL_CONV
    vec = pl.BlockSpec((1, CONV_CH), lambda b, i: (0, 0))
    ex = EXPERTS_PER_CONV_STEP
    w_in = pl.BlockSpec((ex, D_MODEL, D_EXPERT), lambda b, i: (b * steps_per_seq + i, 0, 0))
    return pl.pallas_call(
        _conv_kernel,
        grid=(BATCH, steps_per_seq),
        in_specs=[pl.BlockSpec((1, TL_CONV, CONV_CH), lambda b, i: (b, i, 0)),
                  pl.BlockSpec((1, CONV_HALO, CONV_CH),
                               lambda b, i: (b, jnp.maximum(i * halo_per_tile - 1, 0), 0)),
                  pl.BlockSpec((CONV_WIDTH + 1, CONV_CH), lambda b, i: (0, 0)),
                  vec, vec, vec, vec,
                  w_in, w_in,
                  pl.BlockSpec((ex, D_EXPERT, D_MODEL), lambda b, i: (b * steps_per_seq + i, 0, 0))],
        out_specs=[pl.BlockSpec((1, TL_CONV, CONV_CH), lambda b, i: (b, i, 0)),
                   pl.BlockSpec((ex, D_MODEL, 2 * D_EXPERT), lambda b, i: (b * steps_per_seq + i, 0, 0)),
                   pl.BlockSpec((ex, D_EXPERT, D_MODEL), lambda b, i: (b * steps_per_seq + i, 0, 0))],
        out_shape=[jax.ShapeDtypeStruct((BATCH, SEQ, CONV_CH), BF16),
                   jax.ShapeDtypeStruct((N_EXPERTS, D_MODEL, 2 * D_EXPERT), BF16),
                   jax.ShapeDtypeStruct((N_EXPERTS, D_EXPERT, D_MODEL), BF16)],
        scratch_shapes=[pltpu.VMEM((SUBLANES, TL_CONV + CONV_HALO, CONV_CH), F32)],
        compiler_params=_cparams(("parallel", "arbitrary")),
        name="conv_module",
    )(v3, v3, conv_w, conv_b, ln_g, ln_b, g_out, we_gate, we_up, we_down)


S5_GROUP_ROWS = S5_CHUNKS + SUBLANES


S5_POW_ROWS = (S5_Q + 1 + SUBLANES - 1) // SUBLANES * SUBLANES
(S5_BB_RI, S5_BB_NIR, S5_BB_IR, S5_BB_RNI, S5_CC_RI, S5_CC_NIR, S5_N_PARAM) = range(7)


def _s5_kernel(ut_ref, pwr_ref, pwi_ref, par_ref, a_ref, bq_ref, bs_ref, yt_ref, sin_s, sp_s):
    phase = pl.program_id(0)
    g = pl.program_id(1)
    q = S5_Q
    n = 2 * SSM_STATE

    def group_row0(k):
        return pl.multiple_of((g * S5_GROUPS_PER_STEP + k) * S5_GROUP_ROWS, SUBLANES)

    def lam_pow(k, j):
        return pwr_ref[k, j:j + 1, :], pwi_ref[k, j:j + 1, :]

    @pl.when(phase == 0)
    def _():
        for k in range(S5_GROUPS_PER_STEP):
            bb_ri, bb_nir = par_ref[k, S5_BB_RI], par_ref[k, S5_BB_NIR]
            bb_ir, bb_rni = par_ref[k, S5_BB_IR], par_ref[k, S5_BB_RNI]
            blk_q, blk_s = [], []
            for t in range(q):
                pr, pi_ = lam_pow(k, q - 1 - t)
                blk_q.append(pr * bb_ri + pi_ * bb_nir)
                blk_s.append(pr * bb_ir + pi_ * bb_rni)
            wst = jnp.concatenate([jnp.concatenate(blk_q, axis=0), jnp.concatenate(blk_s, axis=0)], axis=1)
            r = jnp.dot(ut_ref[k], wst.astype(BF16), preferred_element_type=F32)
            sin_s[0, pl.ds(group_row0(k), S5_CHUNKS), :] = r[:, :n]
            sin_s[1, pl.ds(group_row0(k), S5_CHUNKS), :] = r[:, n:]

    @pl.when((phase == 1) & (g == 0))
    def _():
        a = a_ref[...]
        bq = bq_ref[...]
        bs = bs_ref[...]

        def body(c, carry):
            nxt = []
            for b in range(BATCH):
                x, xs = carry[b]
                rows = pl.ds(b * S5_CHUNKS_PER_SEQ + c, SSM_GROUPS, stride=S5_GROUP_ROWS)
                sp_s[rows, :] = x
                nxt.append((a * x + bq * xs + sin_s[0, rows, :], a * xs + bs * x + sin_s[1, rows, :]))
            return tuple(nxt)

        z = jnp.zeros((SSM_GROUPS, n), F32)
        lax.fori_loop(0, S5_CHUNKS_PER_SEQ, body, tuple((z, z) for _ in range(BATCH)))

    @pl.when(phase == 1)
    def _():
        for k in range(S5_GROUPS_PER_STEP):
            cc_ri, cc_nir = par_ref[k, S5_CC_RI], par_ref[k, S5_CC_NIR]
            cl = []
            for j in range(q + 1):
                pr, pi_ = lam_pow(k, j)
                cl.append(pr * cc_ri + pi_ * cc_nir)
            cl_lo = jnp.concatenate(cl[:q], axis=0)
            cl_hi = jnp.concatenate(cl[1:], axis=0)
            lane = lax.broadcasted_iota(jnp.int32, (1, n), 1)
            vgt = (cl_hi * jnp.where(lane < SSM_STATE, 1.0, -1.0)).astype(BF16)
            kt = _dot_nt_split(par_ref[k, S5_BB_RNI], cl_lo)
            padded = jnp.concatenate([jnp.zeros_like(kt), kt], axis=1)
            tg = jnp.concatenate(
                [padded[:, (q - t) * SSM_GROUP_CH:(q - t) * SSM_GROUP_CH + S5_QH] for t in range(q)],
                axis=0).astype(BF16)
            sp = sp_s[pl.ds(group_row0(k), S5_CHUNKS), :]
            y = jnp.dot(ut_ref[k], tg, preferred_element_type=F32)
            yt_ref[k] = y + lax.dot_general(sp.astype(BF16), vgt, (((1,), (1,)), ((), ())),
                                            preferred_element_type=F32)


def _s5(ut, pwr, pwi, params, a_cat, b_q, b_s):
    vec = pl.BlockSpec((SSM_GROUPS, 2 * SSM_STATE), lambda p, g: (0, 0))
    gs = S5_GROUPS_PER_STEP
    powers = pl.BlockSpec((gs, S5_POW_ROWS, 2 * SSM_STATE), lambda p, g: (g, 0, 0))
    return pl.pallas_call(
        _s5_kernel,
        grid=(2, SSM_GROUPS // gs),
        in_specs=[pl.BlockSpec((gs, S5_CHUNKS, S5_QH), lambda p, g: (g, 0, 0)),
                  powers, powers,
                  pl.BlockSpec((gs, S5_N_PARAM, SSM_GROUP_CH, 2 * SSM_STATE), lambda p, g: (g, 0, 0, 0)),
                  vec, vec, vec],
        out_specs=pl.BlockSpec((gs, S5_CHUNKS, S5_QH), lambda p, g: (g * p, 0, 0)),
        out_shape=jax.ShapeDtypeStruct((SSM_GROUPS, S5_CHUNKS, S5_QH), F32),
        scratch_shapes=[pltpu.VMEM((2, SSM_GROUPS * S5_GROUP_ROWS, 2 * SSM_STATE), F32),
                        pltpu.VMEM((SSM_GROUPS * S5_GROUP_ROWS, 2 * SSM_STATE), F32)],
        compiler_params=_cparams(("arbitrary", "arbitrary")),
        name="s5_chunked",
    )(ut, pwr, pwi, params, a_cat, b_q, b_s)


def _s5_operators(a_re, a_im, log_dt, b_re, b_im, c_re, c_im):
    q = S5_Q
    dt = jnp.exp(log_dt)[:, None]
    ar, ai = a_re, a_im
    mag = jnp.exp(ar * dt)
    lr = mag * jnp.cos(ai * dt)
    li = mag * jnp.sin(ai * dt)
    den = ar * ar + ai * ai
    nr = lr - 1.0
    kr = (nr * ar + li * ai) / den
    ki = (li * ar - nr * ai) / den
    bbr = kr[..., None] * b_re - ki[..., None] * b_im
    bbi = kr[..., None] * b_im + ki[..., None] * b_re
    j = jnp.arange(q + 1, dtype=F32)[None, :, None]
    pmag = jnp.exp(ar[:, None, :] * dt[:, :, None] * j)
    pang = ai[:, None, :] * dt[:, :, None] * j
    pr = pmag * jnp.cos(pang)
    pi_ = pmag * jnp.sin(pang)
    pad = ((0, 0), (0, S5_POW_ROWS - (q + 1)), (0, 0))
    pwr = jnp.pad(jnp.concatenate([pr, pr], axis=-1), pad)
    pwi = jnp.pad(jnp.concatenate([pi_, pi_], axis=-1), pad)
    br_t = bbr.transpose(0, 2, 1)
    bi_t = bbi.transpose(0, 2, 1)
    cat = lambda a, b: jnp.concatenate([a, b], axis=-1)
    stack = [None] * S5_N_PARAM
    stack[S5_BB_RI] = cat(br_t, bi_t)
    stack[S5_BB_NIR] = cat(-bi_t, br_t)
    stack[S5_BB_IR] = cat(bi_t, br_t)
    stack[S5_BB_RNI] = cat(br_t, -bi_t)
    stack[S5_CC_RI] = cat(c_re, c_im)
    stack[S5_CC_NIR] = cat(-c_im, c_re)
    params = jnp.stack(stack, axis=1)
    aq_r, aq_i = pr[:, q], pi_[:, q]
    a_cat = cat(aq_r, aq_r)
    b_q = cat(-aq_i, aq_i)
    b_s = cat(aq_i, -aq_i)
    return pwr, pwi, params, a_cat, b_q, b_s


def _gelu_tanh(x):
    return 0.5 * x * (1.0 + jnp.tanh(math.sqrt(2.0 / math.pi) * (x + 0.044715 * (x * x * x))))


def _mix_kernel(an_ref, yt_ref, u_ref, x_ref, mod_ref, d_ref, wglu_ref, bglu_ref, gos_ref,
                woa_ref, wob_ref, gpm_ref, gpf_ref, wr_ref, br_ref, tri_ref,
                x1_ref, h2_ref, eidx_ref, rank_ref, gw_ref, cnt_ref, run_ref, tile_ref):
    i = pl.program_id(0)
    tm = TM_MIX

    @pl.when(i == 0)
    def _():
        run_ref[...] = jnp.zeros_like(run_ref)

    gt_m = mod_ref[0, 2:3, :]
    sh_f = mod_ref[0, 3:4, :]
    sc_f = mod_ref[0, 4:5, :]

    yy = _from_group_chunks(yt_ref, tile_ref) + d_ref[...] * u_ref[...]
    g = _gelu_tanh(yy)
    gl = jnp.dot(g.astype(BF16), wglu_ref[...], preferred_element_type=F32) + bglu_ref[...]
    ob = g * jax.nn.sigmoid(gl)
    bn = _rms(ob, gos_ref[...]).astype(BF16)
    o = (jnp.dot(an_ref[...], woa_ref[...], preferred_element_type=F32)
         + jnp.dot(bn, wob_ref[...], preferred_element_type=F32))
    x1 = x_ref[...] + gt_m * _rms(o, gpm_ref[...])
    x1_ref[...] = x1
    h2 = _rms(x1, gpf_ref[...]) * (1.0 + sc_f) + sh_f
    h2_hi = h2.astype(BF16)
    h2_hi32 = h2_hi.astype(F32)
    h2_ref[...] = _pack_rounded_halves(h2_hi32)

    h2_lo = (h2 - h2_hi32).astype(BF16)
    nt = (((1,), (1,)), ((), ()))
    both = lax.dot_general(wr_ref[...], h2_hi, nt, preferred_element_type=F32)
    logits = (both[:N_EXPERTS] + both[N_EXPERTS:]
              + lax.dot_general(wr_ref[:N_EXPERTS, :], h2_lo, nt, preferred_element_type=F32))
    scores = jax.nn.sigmoid(logits)
    biased = scores + br_ref[...]
    ng = N_ROUTE_GROUPS
    gsz = N_EXPERTS // ng
    b3 = biased.reshape(ng, gsz, tm)
    s3 = scores.reshape(ng, gsz, tm)
    sub = lax.broadcasted_iota(jnp.int32, (ng, gsz, tm), 1).astype(F32)
    grp = lax.broadcasted_iota(jnp.int32, (ng, gsz, tm), 0).astype(F32)
    eid = grp * gsz + sub
    neg = -jnp.inf
    m1 = jnp.max(b3, axis=1, keepdims=True)
    i1 = jnp.min(jnp.where(b3 == m1, sub, float(gsz)), axis=1, keepdims=True)
    m2 = jnp.max(jnp.where(sub == i1, neg, b3), axis=1, keepdims=True)
    gs = m1 + m2
    gi = lax.broadcasted_iota(jnp.int32, (ng, 1, tm), 0)
    beaten = jnp.zeros((ng, 1, tm), F32)
    for gp in range(ng):
        o_ = gs[gp:gp + 1]
        beats = (o_ > gs) | ((o_ == gs) & (gi > gp))
        beaten = beaten + beats.astype(F32)
    gmask = beaten < float(TOPK_ROUTE_GROUPS)
    masked = jnp.where(gmask, b3, neg)

    sels = []
    picked = jnp.zeros((ng, gsz, tm), F32)
    for k in range(TOP_K):
        m = jnp.max(jnp.max(masked, axis=0, keepdims=True), axis=1, keepdims=True)
        cand = jnp.where(masked == m, eid, float(N_EXPERTS))
        sel = jnp.min(jnp.min(cand, axis=0, keepdims=True), axis=1, keepdims=True)
        oh = eid == sel
        masked = jnp.where(oh, neg, masked)
        picked = jnp.where(oh, 1.0, picked)
        sels.append(sel)

    pm = picked.reshape(N_EXPERTS, tm)
    prefix = jnp.dot(pm.astype(BF16), tri_ref[...], preferred_element_type=F32) + run_ref[:, 0:1]
    p3 = prefix.reshape(ng, gsz, tm)
    run_new = run_ref[...] + jnp.sum(pm, axis=1, keepdims=True)
    run_ref[...] = run_new
    cnt_ref[...] = run_new

    sc_rows = []
    for k in range(TOP_K):
        oh = eid == sels[k]
        sc_k = jnp.sum(jnp.sum(jnp.where(oh, s3, 0.0), axis=0, keepdims=True), axis=1, keepdims=True)
        rk_k = jnp.sum(jnp.sum(jnp.where(oh, p3, 0.0), axis=0, keepdims=True), axis=1, keepdims=True)
        sc_rows.append(sc_k)
        eidx_ref[k:k + 1, :] = sels[k].reshape(1, tm).astype(jnp.int32)
        rank_ref[k:k + 1, :] = rk_k.reshape(1, tm).astype(jnp.int32)
    tot = sc_rows[0]
    for k in range(1, TOP_K):
        tot = tot + sc_rows[k]
    inv = ROUTED_SCALE / (tot + 1e-20)
    for k in range(TOP_K):
        gw_ref[k:k + 1, :] = (sc_rows[k] * inv).reshape(1, tm)


def _mix(half, a_n, yt, u2, x2, mod, d_skip, wglu_bf, b_glu, g_out_ssm, wo_a, wo_b, g_post_mix,
         g_pre_ffn, w_router_split, b_router_col, tri):
    tm = TM_MIX
    t0 = half * (HALF_TOK // tm)
    row = lambda n: pl.BlockSpec((1, n), lambda i: (0, 0))
    full = lambda a, b: pl.BlockSpec((a, b), lambda i: (0, 0))
    tok_in = lambda n: pl.BlockSpec((tm, n), lambda i: (t0 + i, 0))
    tok = lambda n: pl.BlockSpec((tm, n), lambda i: (i, 0))
    col = pl.BlockSpec((TOP_K, tm), lambda i: (0, i))
    return pl.pallas_call(
        _mix_kernel,
        grid=(HALF_TOK // tm,),
        in_specs=[tok_in(CONV_CH),
                  pl.BlockSpec((SSM_GROUPS, tm // S5_Q, S5_QH), lambda i: (0, t0 + i, 0)),
                  tok_in(SSM_CH), tok_in(D_MODEL),
                  pl.BlockSpec((1, MOD_ROWS, D_MODEL), lambda i: (half, 0, 0)),
                  row(SSM_CH), full(SSM_CH, SSM_CH), row(SSM_CH), row(SSM_CH),
                  full(CONV_CH, D_MODEL), full(SSM_CH, D_MODEL), row(D_MODEL), row(D_MODEL),
                  full(2 * N_EXPERTS, D_MODEL), full(N_EXPERTS, 1), full(tm, tm)],
        out_specs=[tok(D_MODEL), tok(D_MODEL // 2), col, col, col,
                   pl.BlockSpec((N_EXPERTS, LANES), lambda i: (0, 0))],
        out_shape=[jax.ShapeDtypeStruct((HALF_TOK, D_MODEL), F32),
                   jax.ShapeDtypeStruct((HALF_TOK, D_MODEL // 2), jnp.uint32),
                   jax.ShapeDtypeStruct((TOP_K, HALF_TOK), jnp.int32),
                   jax.ShapeDtypeStruct((TOP_K, HALF_TOK), jnp.int32),
                   jax.ShapeDtypeStruct((TOP_K, HALF_TOK), F32),
                   jax.ShapeDtypeStruct((N_EXPERTS, LANES), F32)],
        scratch_shapes=[pltpu.VMEM((N_EXPERTS, LANES), F32),
                        pltpu.VMEM((SSM_CH // LANES, tm, LANES), F32)],
        compiler_params=_cparams(("arbitrary",)),
        name="mix_out_router",
    )(a_n, yt, u2, x2, mod, d_skip, wglu_bf, b_glu, g_out_ssm, wo_a, wo_b, g_post_mix,
      g_pre_ffn, w_router_split, b_router_col, tri)


def _expert_kernel(blk0_ref, nblk_ref, xs_hbm, wgu_ref, wd_ref, ys_hbm, xbuf, ybuf, sem_in, sem_out):
    e = pl.program_id(0)
    n = nblk_ref[e]
    b0 = blk0_ref[e]
    n_all = blk0_ref[N_EXPERTS - 1] + nblk_ref[N_EXPERTS - 1]

    def rows(b):
        return pl.ds(pl.multiple_of(b * ROW_BLOCK, ROW_BLOCK), ROW_BLOCK)

    def in_copy(b, slot):
        return pltpu.make_async_copy(xs_hbm.at[rows(b)], xbuf.at[slot], sem_in.at[slot])

    def out_copy(b, slot):
        return pltpu.make_async_copy(ybuf.at[slot], ys_hbm.at[rows(b)], sem_out.at[slot])

    for b in range(EXPERT_AHEAD):
        @pl.when((e == 0) & (b < n_all))
        def _():
            in_copy(b, b).start()

    def admit(b):
        in_copy(b, b % EXPERT_SLOTS).wait()
        ahead = b + EXPERT_AHEAD

        @pl.when(ahead < n_all)
        def _():
            in_copy(ahead, ahead % EXPERT_SLOTS).start()

        @pl.when(b >= EXPERT_SLOTS)
        def _():
            out_copy(b - EXPERT_SLOTS, b % EXPERT_SLOTS).wait()

    def compute(b):
        slot = b % EXPERT_SLOTS
        x_lo, x_hi = _unpack_halves(xbuf[slot])
        x = jnp.concatenate([x_lo.astype(BF16), x_hi.astype(BF16)], axis=1)
        h = jnp.dot(x, wgu_ref[0], preferred_element_type=F32)
        hg = h[:, :D_EXPERT]
        act = hg * jax.nn.sigmoid(hg) * h[:, D_EXPERT:]
        ybuf[slot] = _pack_halves(jnp.dot(act.astype(BF16), wd_ref[0], preferred_element_type=F32))

    def block(b, carry):
        admit(b)
        compute(b)
        out_copy(b, b % EXPERT_SLOTS).start()
        return carry

    lax.fori_loop(b0, b0 + n, block, 0)

    @pl.when(e == N_EXPERTS - 1)
    def _():
        for j in range(1, EXPERT_SLOTS + 1):
            @pl.when(n_all >= j)
            def _():
                out_copy(n_all - j, (n_all - j) % EXPERT_SLOTS).wait()


def _experts(blk0, nblk, xs, we_gu, we_d):
    any_spec = pl.BlockSpec(memory_space=pl.ANY)
    grid_spec = pltpu.PrefetchScalarGridSpec(
        num_scalar_prefetch=2,
        grid=(N_EXPERTS,),
        in_specs=[any_spec,
                  pl.BlockSpec((1, D_MODEL, 2 * D_EXPERT), lambda e, b0, nb: (e, 0, 0)),
                  pl.BlockSpec((1, D_EXPERT, D_MODEL), lambda e, b0, nb: (e, 0, 0))],
        out_specs=any_spec,
        scratch_shapes=[pltpu.VMEM((EXPERT_SLOTS, ROW_BLOCK, D_MODEL // 2), jnp.uint32),
                        pltpu.VMEM((EXPERT_SLOTS, ROW_BLOCK, D_MODEL // 2), jnp.uint32),
                        pltpu.SemaphoreType.DMA((EXPERT_SLOTS,)),
                        pltpu.SemaphoreType.DMA((EXPERT_SLOTS,))],
    )
    return pl.pallas_call(
        _expert_kernel,
        grid_spec=grid_spec,
        out_shape=jax.ShapeDtypeStruct((N_ROWS, D_MODEL // 2), jnp.uint32),
        compiler_params=_cparams(("arbitrary",)),
        name="routed_experts",
    )(blk0, nblk, xs, we_gu, we_d)


def _final_kernel(h2_ref, yg_ref, gw_ref, x1_ref, mod_ref, wgu_ref, wd_ref, g_ref, *rest):
    o_ref = rest[-1]
    half = D_MODEL // 2
    gt_f = mod_ref[0, 5:6, :]
    x_lo, x_hi = _unpack_halves(h2_ref[...])
    h = (jnp.dot(x_lo.astype(BF16), wgu_ref[:half, :], preferred_element_type=F32)
         + jnp.dot(x_hi.astype(BF16), wgu_ref[half:, :], preferred_element_type=F32))
    hg = h[:, :D_EXPERT]
    act = hg * jax.nn.sigmoid(hg) * h[:, D_EXPERT:]
    shared = jnp.dot(act.astype(BF16), wd_ref[...], preferred_element_type=F32)
    y_lo = shared[:, :half]
    y_hi = shared[:, half:]
    for k in range(TOP_K):
        r_lo, r_hi = _unpack_halves(yg_ref[k])
        w = gw_ref[:, k:k + 1]
        y_lo = y_lo + w * r_lo
        y_hi = y_hi + w * r_hi
    ms = (jnp.sum(y_lo * y_lo, axis=-1, keepdims=True)
          + jnp.sum(y_hi * y_hi, axis=-1, keepdims=True)) * (1.0 / D_MODEL)
    inv = lax.rsqrt(ms + NORM_EPS)
    o_ref[:, :half] = x1_ref[:, :half] + gt_f[:, :half] * (y_lo * inv * g_ref[:, :half])
    o_ref[:, half:] = x1_ref[:, half:] + gt_f[:, half:] * (y_hi * inv * g_ref[:, half:])


def _final(half, out_prev, h2p, yg, gw_t, x1, mod, ws_gu, ws_d, g_post_ffn):
    tm = TM_OUT
    t0 = half * (HALF_TOK // tm)
    tok = pl.BlockSpec((tm, D_MODEL), lambda i: (i, 0))
    in_specs = [pl.BlockSpec((tm, D_MODEL // 2), lambda i: (i, 0)),
                pl.BlockSpec((TOP_K, tm, D_MODEL // 2), lambda i: (0, i, 0)),
                pl.BlockSpec((tm, TOP_K), lambda i: (i, 0)),
                tok,
                pl.BlockSpec((1, MOD_ROWS, D_MODEL), lambda i: (half, 0, 0)),
                pl.BlockSpec((D_MODEL, 2 * D_EXPERT), lambda i: (0, 0)),
                pl.BlockSpec((D_EXPERT, D_MODEL), lambda i: (0, 0)),
                pl.BlockSpec((1, D_MODEL), lambda i: (0, 0))]
    args = [h2p, yg, gw_t, x1, mod, ws_gu, ws_d, g_post_ffn]
    aliases = {}
    if out_prev is not None:
        aliases = {len(args): 0}
        in_specs.append(pl.BlockSpec(memory_space=pl.ANY))
        args.append(out_prev)
    return pl.pallas_call(
        _final_kernel,
        grid=(HALF_TOK // tm,),
        in_specs=in_specs,
        out_specs=pl.BlockSpec((tm, D_MODEL), lambda i: (t0 + i, 0)),
        out_shape=jax.ShapeDtypeStruct((N_TOK, D_MODEL), F32),
        input_output_aliases=aliases,
        compiler_params=_cparams(("parallel",)),
        name="shared_final",
    )(*args)


def _sc_worker_id():
    return lax.axis_index("s") * SC_CORES + lax.axis_index("c")


def _dispatch_body(h_hbm, dest_hbm, xs_hbm, idx_v, rows_v, sem_l, sem_s):
    n = SC_CHUNKS_PER_WORKER
    c0 = _sc_worker_id() * n

    def load(i, b):
        return pltpu.async_copy(h_hbm.at[pl.ds((c0 + i) * SC_W, SC_W)], rows_v.at[b], sem_l.at[b])

    loads = [None] * n
    scat = [None] * n
    loads[0] = load(0, 0)
    for i in range(n):
        b = i % 2
        pltpu.sync_copy(dest_hbm.at[c0 + i], idx_v.at[b])
        loads[i].wait()
        if i + 1 < n:
            if i >= 1:
                for d in scat[i - 1]:
                    d.wait()
            loads[i + 1] = load(i + 1, 1 - b)
        scat[i] = [pltpu.async_copy(rows_v.at[b], xs_hbm.at[idx_v.at[b].at[k]], sem_s.at[b])
                   for k in range(TOP_K)]
    for i in (n - 2, n - 1):
        for d in scat[i]:
            d.wait()


def _sc_dispatch(h2p, dest3):
    mesh = plsc.VectorSubcoreMesh(core_axis_name="c", subcore_axis_name="s")
    return pl.kernel(
        _dispatch_body, mesh=mesh,
        out_type=jax.ShapeDtypeStruct((N_ROWS, D_MODEL // 2), jnp.uint32),
        scratch_types=[pltpu.VMEM((2, TOP_K, SC_W), jnp.int32),
                       pltpu.VMEM((2, SC_W, D_MODEL // 2), jnp.uint32),
                       pltpu.SemaphoreType.DMA((2,)), pltpu.SemaphoreType.DMA((2,))],
    )(h2p, dest3)


def _combine_body(ys_hbm, dest_hbm, yg_hbm, idx_v, rows_v, sem_g, sem_w):
    c0 = _sc_worker_id() * SC_CHUNKS_PER_WORKER

    @pl.loop(0, SC_CHUNKS_PER_WORKER)
    def _(i):
        c = c0 + i
        pltpu.sync_copy(dest_hbm.at[c], idx_v)
        g = [None] * TOP_K
        w = [None] * TOP_K
        g[0] = pltpu.async_copy(ys_hbm.at[idx_v.at[0]], rows_v.at[0], sem_g.at[0])
        for k in range(TOP_K):
            b = k % 2
            g[k].wait()
            if k + 1 < TOP_K:
                if k >= 1:
                    w[k - 1].wait()
                g[k + 1] = pltpu.async_copy(ys_hbm.at[idx_v.at[k + 1]], rows_v.at[1 - b], sem_g.at[1 - b])
            w[k] = pltpu.async_copy(rows_v.at[b], yg_hbm.at[k].at[pl.ds(c * SC_W, SC_W)], sem_w.at[b])
        w[TOP_K - 2].wait()
        w[TOP_K - 1].wait()


def _sc_combine(ysp, dest3):
    mesh = plsc.VectorSubcoreMesh(core_axis_name="c", subcore_axis_name="s")
    return pl.kernel(
        _combine_body, mesh=mesh,
        out_type=jax.ShapeDtypeStruct((TOP_K, HALF_TOK, D_MODEL // 2), jnp.uint32),
        scratch_types=[pltpu.VMEM((TOP_K, SC_W), jnp.int32),
                       pltpu.VMEM((2, SC_W, D_MODEL // 2), jnp.uint32),
                       pltpu.SemaphoreType.DMA((2,)), pltpu.SemaphoreType.DMA((2,))],
    )(ysp, dest3)


def kernel(x, c, w_ada, b_ada, g_pre_mix, g_post_mix, w_in, conv_w, conv_b, conv_ln_g, conv_ln_b,
           ssm_a_re, ssm_a_im, ssm_log_dt, ssm_b_re, ssm_b_im, ssm_c_re, ssm_c_im, ssm_d,
           ssm_w_glu, ssm_b_glu, g_out_conv, g_out_ssm, w_out, g_pre_ffn, g_post_ffn,
           w_router, b_router, we_gate, we_up, we_down, ws_gate, ws_up, ws_down):
    l = 0
    x2 = x.reshape(N_TOK, D_MODEL)
    r1 = lambda a: a.reshape(1, -1)

    c_pad = jnp.zeros((SUBLANES, D_MODEL), F32).at[:BATCH].set(c)
    mod = _ada(c_pad, w_ada[l], r1(b_ada[l]))[:BATCH].reshape(BATCH, N_MOD, D_MODEL)
    mod = jnp.concatenate([mod, jnp.zeros((BATCH, MOD_ROWS - N_MOD, D_MODEL), F32)], axis=1)

    v, u, ut = _inproj(x2, mod, r1(g_pre_mix[l]), w_in[l].astype(BF16))
    cw = jnp.concatenate([conv_w[l].reshape(CONV_WIDTH, CONV_CH), jnp.zeros((1, CONV_CH), F32)], axis=0)
    a_n, we_gu, we_d = _conv(v.reshape(BATCH, SEQ, CONV_CH), cw, r1(conv_b[l]), r1(conv_ln_g[l]),
                             r1(conv_ln_b[l]), r1(g_out_conv[l]), we_gate[l], we_up[l], we_down[l])
    a_n = a_n.reshape(N_TOK, CONV_CH)

    pwr, pwi, s5_params, a_cat, b_q, b_s = _s5_operators(
        ssm_a_re[l], ssm_a_im[l], ssm_log_dt[l], ssm_b_re[l], ssm_b_im[l], ssm_c_re[l], ssm_c_im[l])
    yt = _s5(ut, pwr, pwi, s5_params, a_cat, b_q, b_s)

    tm = TM_MIX
    tri = (jnp.arange(tm)[:, None] < jnp.arange(tm)[None, :]).astype(BF16)
    wo = w_out[l].astype(BF16)
    wr_t = w_router[l].T
    wr_hi = wr_t.astype(BF16)
    wr_split = jnp.concatenate([wr_hi, (wr_t - wr_hi.astype(F32)).astype(BF16)], axis=0)
    mix_params = (r1(ssm_d[l]), ssm_w_glu[l].astype(BF16), r1(ssm_b_glu[l]), r1(g_out_ssm[l]),
                  wo[:CONV_CH], wo[CONV_CH:], r1(g_post_mix[l]), r1(g_pre_ffn[l]),
                  wr_split, b_router[l].reshape(N_EXPERTS, 1), tri)
    ws_gu = jnp.concatenate([ws_gate[l], ws_up[l]], axis=1).astype(BF16)
    ws_d = ws_down[l].astype(BF16)
    e_ids = jnp.arange(N_EXPERTS, dtype=jnp.int32)

    out = None
    for half in range(N_HALVES):
        x1, h2, eidx, rank, gw, cnt = _mix(half, a_n, yt, u, x2, mod, *mix_params)
        counts = cnt[:, 0].astype(jnp.int32)
        padded = (counts + ROW_BLOCK - 1) // ROW_BLOCK * ROW_BLOCK
        pstart = jnp.cumsum(padded) - padded
        dest = rank + jnp.sum(jnp.where(eidx[..., None] == e_ids, pstart, 0), axis=-1)
        dest3 = dest.reshape(TOP_K, HALF_TOK // SC_W, SC_W).transpose(1, 0, 2)

        xs = _sc_dispatch(h2, dest3)
        ys = _experts(pstart // ROW_BLOCK, padded // ROW_BLOCK, xs, we_gu, we_d)
        yg = _sc_combine(ys, dest3)
        out = _final(half, out, h2, yg, gw.T, x1, mod, ws_gu, ws_d, r1(g_post_ffn[l]))
    return out.reshape(BATCH, SEQ, D_MODEL)
```

```python
import math

import jax
import jax.numpy as jnp
from jax import lax
from jax.experimental import pallas as pl
from jax.experimental.pallas import tpu as pltpu
from jax.experimental.pallas import tpu_sc as plsc

F32 = jnp.float32
BF16 = jnp.bfloat16

D_MODEL = 1024
BATCH = 2
SEQ = 8192
N_TOK = BATCH * SEQ
CONV_CH = 512
CONV_WIDTH = 31
SSM_CH = 512
SSM_GROUP_CH = 16
SSM_GROUPS = 32
SSM_STATE = 64
D_IN = 2 * CONV_CH + SSM_CH
N_EXPERTS = 64
TOP_K = 8
N_ROUTE_GROUPS = 8
TOPK_ROUTE_GROUPS = 4
D_EXPERT = 256
ROUTED_SCALE = 2.5
NORM_EPS = 1e-6

SUBLANES = 8
LANES = 128

N_MOD = 6
MOD_ROWS = SUBLANES
ADA_COLS = 1536
TM_IN = 1024
IN_SUBTILES = 4
TL_CONV = 512
CONV_HALO = 32
CONV_ROWS = 64
EXPERTS_PER_CONV_STEP = N_EXPERTS * TL_CONV // N_TOK
assert EXPERTS_PER_CONV_STEP * N_TOK == N_EXPERTS * TL_CONV
S5_Q = 32
S5_GROUPS_PER_STEP = 4
S5_QH = S5_Q * SSM_GROUP_CH
S5_CHUNKS = N_TOK // S5_Q
S5_CHUNKS_PER_SEQ = SEQ // S5_Q
TM_MIX = 1024
ROW_BLOCK = 512
ROW_UNIT = ROW_BLOCK // 2
EXPERT_AHEAD = 4
EXPERT_SLOTS = EXPERT_AHEAD + 1
HALF_TOK = SEQ
N_HALVES = N_TOK // HALF_TOK
N_UNITS = HALF_TOK * TOP_K // ROW_UNIT + N_EXPERTS
N_BLOCKS = (N_UNITS + N_EXPERTS) // 2
N_ROWS = N_UNITS * ROW_UNIT
TM_OUT = 512
SC_CORES = 2
SC_SUBCORES = 16
SC_WORKERS = SC_CORES * SC_SUBCORES
SC_W = 64
SC_CHUNKS_PER_WORKER = HALF_TOK // (SC_WORKERS * SC_W)
VMEM_LIMIT = 48 * 1024 * 1024


def _cparams(sem):
    return pltpu.CompilerParams(dimension_semantics=sem, vmem_limit_bytes=VMEM_LIMIT)


def _pack_rounded_halves(xr):
    n = xr.shape[-1] // 2
    lo = lax.bitcast_convert_type(xr[:, :n], jnp.uint32)
    hi = lax.bitcast_convert_type(xr[:, n:], jnp.uint32)
    return hi | (lo >> 16)


def _pack_halves(x):
    return _pack_rounded_halves(x.astype(BF16).astype(F32))


def _unpack_halves(p):
    lo = lax.bitcast_convert_type(p << 16, F32)
    hi = lax.bitcast_convert_type(p & jnp.uint32(0xFFFF0000), F32)
    return lo, hi


def _rms(x, g):
    return x * lax.rsqrt(jnp.mean(x * x, axis=-1, keepdims=True) + NORM_EPS) * g


def _split_bf16(x):
    hi = x.astype(BF16)
    return hi, (x - hi.astype(F32)).astype(BF16)


def _dot_nt_split(a, b):
    nt = (((1,), (1,)), ((), ()))
    a_hi, a_lo = _split_bf16(a)
    b_hi, b_lo = _split_bf16(b)
    m = a.shape[0]
    both = lax.dot_general(jnp.concatenate([a_hi, a_lo], axis=0), b_hi, nt, preferred_element_type=F32)
    return both[:m] + both[m:] + lax.dot_general(a_hi, b_lo, nt, preferred_element_type=F32)


def _ada_kernel(c_ref, w_ref, b_ref, o_ref):
    c = c_ref[...]
    a = c * jax.nn.sigmoid(c)
    o_ref[...] = jnp.dot(a, w_ref[...], preferred_element_type=F32,
                         precision=lax.Precision.HIGHEST) + b_ref[...]


def _ada(c_pad, w_ada, b_ada):
    n = w_ada.shape[1]
    bn = ADA_COLS
    return pl.pallas_call(
        _ada_kernel,
        grid=(n // bn,),
        in_specs=[pl.BlockSpec((SUBLANES, D_MODEL), lambda j: (0, 0)),
                  pl.BlockSpec((D_MODEL, bn), lambda j: (0, j)),
                  pl.BlockSpec((1, bn), lambda j: (0, j))],
        out_specs=pl.BlockSpec((SUBLANES, bn), lambda j: (0, j)),
        out_shape=jax.ShapeDtypeStruct((SUBLANES, n), F32),
        compiler_params=_cparams(("arbitrary",)),
        name="ada_mod",
    )(c_pad, w_ada, b_ada)


GROUPS_PER_LANE_TILE = LANES // SSM_GROUP_CH


def _to_group_chunks(u, tile_ref, ut_ref):
    n_chunks = u.shape[0] // S5_Q
    for j in range(SSM_CH // LANES):
        tile_ref[j] = u[:, LANES * j:LANES * (j + 1)]
    for j in range(SSM_CH // LANES):
        rows_t = [tile_ref[j, pl.ds(t, n_chunks, stride=S5_Q), :] for t in range(S5_Q)]
        for gg in range(GROUPS_PER_LANE_TILE):
            lo = gg * SSM_GROUP_CH
            row = jnp.concatenate([r[:, lo:lo + SSM_GROUP_CH] for r in rows_t], axis=1)
            ut_ref[j * GROUPS_PER_LANE_TILE + gg] = row.astype(ut_ref.dtype)


def _from_group_chunks(yt_ref, tile_ref):
    n_chunks = yt_ref.shape[1]
    for j in range(SSM_CH // LANES):
        for t in range(S5_Q):
            lo = t * SSM_GROUP_CH
            piece = jnp.concatenate(
                [yt_ref[j * GROUPS_PER_LANE_TILE + gg, :, lo:lo + SSM_GROUP_CH]
                 for gg in range(GROUPS_PER_LANE_TILE)], axis=1)
            tile_ref[j, pl.ds(t, n_chunks, stride=S5_Q), :] = piece
    return jnp.concatenate([tile_ref[j] for j in range(SSM_CH // LANES)], axis=1)


def _inproj_kernel(x_ref, mod_ref, g_ref, w_ref, v_ref, u_ref, ut_ref, tile_ref):
    sh = mod_ref[0, 0:1, :]
    sc = mod_ref[0, 1:2, :]
    sub = TM_IN // IN_SUBTILES
    sub_chunks = sub // S5_Q
    for s in range(IN_SUBTILES):
        r = slice(s * sub, (s + 1) * sub)
        h = _rms(x_ref[r, :], g_ref[...]) * (1.0 + sc) + sh
        z = jnp.dot(h.astype(BF16), w_ref[...], preferred_element_type=F32)
        v_ref[r, :] = z[:, :CONV_CH] * jax.nn.sigmoid(z[:, CONV_CH:2 * CONV_CH])
        u = z[:, 2 * CONV_CH:]
        u_ref[r, :] = u
        _to_group_chunks(u, tile_ref.at[s], ut_ref.at[:, s * sub_chunks:(s + 1) * sub_chunks, :])


def _inproj(x2, mod, g_pre, w_in_bf):
    tiles_per_seq = SEQ // TM_IN
    return pl.pallas_call(
        _inproj_kernel,
        grid=(N_TOK // TM_IN,),
        in_specs=[pl.BlockSpec((TM_IN, D_MODEL), lambda i: (i, 0)),
                  pl.BlockSpec((1, MOD_ROWS, D_MODEL), lambda i: (i // tiles_per_seq, 0, 0)),
                  pl.BlockSpec((1, D_MODEL), lambda i: (0, 0)),
                  pl.BlockSpec((D_MODEL, D_IN), lambda i: (0, 0))],
        out_specs=[pl.BlockSpec((TM_IN, CONV_CH), lambda i: (i, 0)),
                   pl.BlockSpec((TM_IN, SSM_CH), lambda i: (i, 0)),
                   pl.BlockSpec((SSM_GROUPS, TM_IN // S5_Q, S5_QH), lambda i: (0, i, 0))],
        out_shape=[jax.ShapeDtypeStruct((N_TOK, CONV_CH), F32),
                   jax.ShapeDtypeStruct((N_TOK, SSM_CH), F32),
                   jax.ShapeDtypeStruct((SSM_GROUPS, S5_CHUNKS, S5_QH), BF16)],
        scratch_shapes=[pltpu.VMEM((IN_SUBTILES, SSM_CH // LANES, TM_IN // IN_SUBTILES, LANES), F32)],
        compiler_params=_cparams(("parallel",)),
        name="in_proj",
    )(x2, mod, g_pre, w_in_bf)


def _conv_kernel(vc_ref, vp_ref, w_ref, cb_ref, lg_ref, lb_ref, go_ref, wg_ref, wu_ref, wd_ref,
                 o_ref, wgu_o, wd_o, sh_ref):
    for q in range(EXPERTS_PER_CONV_STEP):
        wgu_o[q, :, :D_EXPERT] = wg_ref[q].astype(BF16)
        wgu_o[q, :, D_EXPERT:] = wu_ref[q].astype(BF16)
        wd_o[q] = wd_ref[q].astype(BF16)

    i = pl.program_id(1)
    keep = (i > 0).astype(F32)
    n_ext = TL_CONV + CONV_HALO
    sh_ref[0, 0:CONV_HALO, :] = vp_ref[0] * keep
    sh_ref[0, CONV_HALO:, :] = vc_ref[0]
    ext = sh_ref[0]
    for s in range(1, SUBLANES):
        sh_ref[s] = pltpu.roll(ext, n_ext - s, axis=0)
    off = CONV_HALO - (CONV_WIDTH - 1)
    for r in range(TL_CONV // CONV_ROWS):
        acc = None
        for j in range(CONV_WIDTH):
            s = (off + j) % SUBLANES
            al = r * CONV_ROWS + (off + j) - s
            term = w_ref[j:j + 1, :] * sh_ref[s, al:al + CONV_ROWS, :]
            acc = term if acc is None else acc + term
        y = acc + cb_ref[...]
        mu = jnp.mean(y, axis=-1, keepdims=True)
        d = y - mu
        var = jnp.mean(d * d, axis=-1, keepdims=True)
        yn = d * lax.rsqrt(var + NORM_EPS) * lg_ref[...] + lb_ref[...]
        a = yn * jax.nn.sigmoid(yn)
        o_ref[0, r * CONV_ROWS:(r + 1) * CONV_ROWS, :] = _rms(a, go_ref[...]).astype(BF16)


def _conv(v3, conv_w, conv_b, ln_g, ln_b, g_out, we_gate, we_up, we_down):
    halo_per_tile = TL_CONV // CONV_HALO
    steps_per_seq = SEQ // TL_CONV
    vec = pl.BlockSpec((1, CONV_CH), lambda b, i: (0, 0))
    ex = EXPERTS_PER_CONV_STEP
    w_in = pl.BlockSpec((ex, D_MODEL, D_EXPERT), lambda b, i: (b * steps_per_seq + i, 0, 0))
    return pl.pallas_call(
        _conv_kernel,
        grid=(BATCH, steps_per_seq),
        in_specs=[pl.BlockSpec((1, TL_CONV, CONV_CH), lambda b, i: (b, i, 0)),
                  pl.BlockSpec((1, CONV_HALO, CONV_CH),
                               lambda b, i: (b, jnp.maximum(i * halo_per_tile - 1, 0), 0)),
                  pl.BlockSpec((CONV_WIDTH + 1, CONV_CH), lambda b, i: (0, 0)),
                  vec, vec, vec, vec,
                  w_in, w_in,
                  pl.BlockSpec((ex, D_EXPERT, D_MODEL), lambda b, i: (b * steps_per_seq + i, 0, 0))],
        out_specs=[pl.BlockSpec((1, TL_CONV, CONV_CH), lambda b, i: (b, i, 0)),
                   pl.BlockSpec((ex, D_MODEL, 2 * D_EXPERT), lambda b, i: (b * steps_per_seq + i, 0, 0)),
                   pl.BlockSpec((ex, D_EXPERT, D_MODEL), lambda b, i: (b * steps_per_seq + i, 0, 0))],
        out_shape=[jax.ShapeDtypeStruct((BATCH, SEQ, CONV_CH), BF16),
                   jax.ShapeDtypeStruct((N_EXPERTS, D_MODEL, 2 * D_EXPERT), BF16),
                   jax.ShapeDtypeStruct((N_EXPERTS, D_EXPERT, D_MODEL), BF16)],
        scratch_shapes=[pltpu.VMEM((SUBLANES, TL_CONV + CONV_HALO, CONV_CH), F32)],
        compiler_params=_cparams(("parallel", "arbitrary")),
        name="conv_module",
    )(v3, v3, conv_w, conv_b, ln_g, ln_b, g_out, we_gate, we_up, we_down)


S5_GROUP_ROWS = S5_CHUNKS + SUBLANES


S5_POW_ROWS = (S5_Q + 1 + SUBLANES - 1) // SUBLANES * SUBLANES
(S5_BB_RI, S5_BB_NIR, S5_BB_IR, S5_BB_RNI, S5_CC_RI, S5_CC_NIR, S5_N_PARAM) = range(7)


def _s5_kernel(ut_ref, pwr_ref, pwi_ref, par_ref, a_ref, bq_ref, bs_ref, yt_ref, sin_s, sp_s):
    phase = pl.program_id(0)
    g = pl.program_id(1)
    q = S5_Q
    n = 2 * SSM_STATE

    def group_row0(k):
        return pl.multiple_of((g * S5_GROUPS_PER_STEP + k) * S5_GROUP_ROWS, SUBLANES)

    def lam_pow(k, j):
        return pwr_ref[k, j:j + 1, :], pwi_ref[k, j:j + 1, :]

    @pl.when(phase == 0)
    def _():
        for k in range(S5_GROUPS_PER_STEP):
            bb_ri, bb_nir = par_ref[k, S5_BB_RI], par_ref[k, S5_BB_NIR]
            bb_ir, bb_rni = par_ref[k, S5_BB_IR], par_ref[k, S5_BB_RNI]
            blk_q, blk_s = [], []
            for t in range(q):
                pr, pi_ = lam_pow(k, q - 1 - t)
                blk_q.append(pr * bb_ri + pi_ * bb_nir)
                blk_s.append(pr * bb_ir + pi_ * bb_rni)
            wst = jnp.concatenate([jnp.concatenate(blk_q, axis=0), jnp.concatenate(blk_s, axis=0)], axis=1)
            r = jnp.dot(ut_ref[k], wst.astype(BF16), preferred_element_type=F32)
            sin_s[0, pl.ds(group_row0(k), S5_CHUNKS), :] = r[:, :n]
            sin_s[1, pl.ds(group_row0(k), S5_CHUNKS), :] = r[:, n:]

    @pl.when((phase == 1) & (g == 0))
    def _():
        a = a_ref[...]
        bq = bq_ref[...]
        bs = bs_ref[...]

        def body(c, carry):
            nxt = []
            for b in range(BATCH):
                x, xs = carry[b]
                rows = pl.ds(b * S5_CHUNKS_PER_SEQ + c, SSM_GROUPS, stride=S5_GROUP_ROWS)
                sp_s[rows, :] = x
                nxt.append((a * x + bq * xs + sin_s[0, rows, :], a * xs + bs * x + sin_s[1, rows, :]))
            return tuple(nxt)

        z = jnp.zeros((SSM_GROUPS, n), F32)
        lax.fori_loop(0, S5_CHUNKS_PER_SEQ, body, tuple((z, z) for _ in range(BATCH)))

    @pl.when(phase == 1)
    def _():
        for k in range(S5_GROUPS_PER_STEP):
            cc_ri, cc_nir = par_ref[k, S5_CC_RI], par_ref[k, S5_CC_NIR]
            cl = []
            for j in range(q + 1):
                pr, pi_ = lam_pow(k, j)
                cl.append(pr * cc_ri + pi_ * cc_nir)
            cl_lo = jnp.concatenate(cl[:q], axis=0)
            cl_hi = jnp.concatenate(cl[1:], axis=0)
            lane = lax.broadcasted_iota(jnp.int32, (1, n), 1)
            vgt = (cl_hi * jnp.where(lane < SSM_STATE, 1.0, -1.0)).astype(BF16)
            kt = _dot_nt_split(par_ref[k, S5_BB_RNI], cl_lo)
            padded = jnp.concatenate([jnp.zeros_like(kt), kt], axis=1)
            tg = jnp.concatenate(
                [padded[:, (q - t) * SSM_GROUP_CH:(q - t) * SSM_GROUP_CH + S5_QH] for t in range(q)],
                axis=0).astype(BF16)
            sp = sp_s[pl.ds(group_row0(k), S5_CHUNKS), :]
            y = jnp.dot(ut_ref[k], tg, preferred_element_type=F32)
            yt_ref[k] = y + lax.dot_general(sp.astype(BF16), vgt, (((1,), (1,)), ((), ())),
                                            preferred_element_type=F32)


def _s5(ut, pwr, pwi, params, a_cat, b_q, b_s):
    vec = pl.BlockSpec((SSM_GROUPS, 2 * SSM_STATE), lambda p, g: (0, 0))
    gs = S5_GROUPS_PER_STEP
    powers = pl.BlockSpec((gs, S5_POW_ROWS, 2 * SSM_STATE), lambda p, g: (g, 0, 0))
    return pl.pallas_call(
        _s5_kernel,
        grid=(2, SSM_GROUPS // gs),
        in_specs=[pl.BlockSpec((gs, S5_CHUNKS, S5_QH), lambda p, g: (g, 0, 0)),
                  powers, powers,
                  pl.BlockSpec((gs, S5_N_PARAM, SSM_GROUP_CH, 2 * SSM_STATE), lambda p, g: (g, 0, 0, 0)),
                  vec, vec, vec],
        out_specs=pl.BlockSpec((gs, S5_CHUNKS, S5_QH), lambda p, g: (g * p, 0, 0)),
        out_shape=jax.ShapeDtypeStruct((SSM_GROUPS, S5_CHUNKS, S5_QH), F32),
        scratch_shapes=[pltpu.VMEM((2, SSM_GROUPS * S5_GROUP_ROWS, 2 * SSM_STATE), F32),
                        pltpu.VMEM((SSM_GROUPS * S5_GROUP_ROWS, 2 * SSM_STATE), F32)],
        compiler_params=_cparams(("arbitrary", "arbitrary")),
        name="s5_chunked",
    )(ut, pwr, pwi, params, a_cat, b_q, b_s)


def _s5_operators(a_re, a_im, log_dt, b_re, b_im, c_re, c_im):
    q = S5_Q
    dt = jnp.exp(log_dt)[:, None]
    ar, ai = a_re, a_im
    mag = jnp.exp(ar * dt)
    lr = mag * jnp.cos(ai * dt)
    li = mag * jnp.sin(ai * dt)
    den = ar * ar + ai * ai
    nr = lr - 1.0
    kr = (nr * ar + li * ai) / den
    ki = (li * ar - nr * ai) / den
    bbr = kr[..., None] * b_re - ki[..., None] * b_im
    bbi = kr[..., None] * b_im + ki[..., None] * b_re
    j = jnp.arange(q + 1, dtype=F32)[None, :, None]
    pmag = jnp.exp(ar[:, None, :] * dt[:, :, None] * j)
    pang = ai[:, None, :] * dt[:, :, None] * j
    pr = pmag * jnp.cos(pang)
    pi_ = pmag * jnp.sin(pang)
    pad = ((0, 0), (0, S5_POW_ROWS - (q + 1)), (0, 0))
    pwr = jnp.pad(jnp.concatenate([pr, pr], axis=-1), pad)
    pwi = jnp.pad(jnp.concatenate([pi_, pi_], axis=-1), pad)
    br_t = bbr.transpose(0, 2, 1)
    bi_t = bbi.transpose(0, 2, 1)
    cat = lambda a, b: jnp.concatenate([a, b], axis=-1)
    stack = [None] * S5_N_PARAM
    stack[S5_BB_RI] = cat(br_t, bi_t)
    stack[S5_BB_NIR] = cat(-bi_t, br_t)
    stack[S5_BB_IR] = cat(bi_t, br_t)
    stack[S5_BB_RNI] = cat(br_t, -bi_t)
    stack[S5_CC_RI] = cat(c_re, c_im)
    stack[S5_CC_NIR] = cat(-c_im, c_re)
    params = jnp.stack(stack, axis=1)
    aq_r, aq_i = pr[:, q], pi_[:, q]
    a_cat = cat(aq_r, aq_r)
    b_q = cat(-aq_i, aq_i)
    b_s = cat(aq_i, -aq_i)
    return pwr, pwi, params, a_cat, b_q, b_s


def _gelu_tanh(x):
    return 0.5 * x * (1.0 + jnp.tanh(math.sqrt(2.0 / math.pi) * (x + 0.044715 * (x * x * x))))


def _mix_kernel(an_ref, yt_ref, u_ref, x_ref, mod_ref, d_ref, wglu_ref, bglu_ref, gos_ref,
                woa_ref, wob_ref, gpm_ref, gpf_ref, wr_ref, br_ref, tri_ref,
                x1_ref, h2_ref, eidx_ref, rank_ref, gw_ref, cnt_ref, run_ref, tile_ref):
    i = pl.program_id(0)
    tm = TM_MIX

    @pl.when(i == 0)
    def _():
        run_ref[...] = jnp.zeros_like(run_ref)

    gt_m = mod_ref[0, 2:3, :]
    sh_f = mod_ref[0, 3:4, :]
    sc_f = mod_ref[0, 4:5, :]

    yy = _from_group_chunks(yt_ref, tile_ref) + d_ref[...] * u_ref[...]
    g = _gelu_tanh(yy)
    gl = jnp.dot(g.astype(BF16), wglu_ref[...], preferred_element_type=F32) + bglu_ref[...]
    ob = g * jax.nn.sigmoid(gl)
    bn = _rms(ob, gos_ref[...]).astype(BF16)
    o = (jnp.dot(an_ref[...], woa_ref[...], preferred_element_type=F32)
         + jnp.dot(bn, wob_ref[...], preferred_element_type=F32))
    x1 = x_ref[...] + gt_m * _rms(o, gpm_ref[...])
    x1_ref[...] = x1
    h2 = _rms(x1, gpf_ref[...]) * (1.0 + sc_f) + sh_f
    h2_hi = h2.astype(BF16)
    h2_hi32 = h2_hi.astype(F32)
    h2_ref[...] = _pack_rounded_halves(h2_hi32)

    h2_lo = (h2 - h2_hi32).astype(BF16)
    nt = (((1,), (1,)), ((), ()))
    both = lax.dot_general(wr_ref[...], h2_hi, nt, preferred_element_type=F32)
    logits = (both[:N_EXPERTS] + both[N_EXPERTS:]
              + lax.dot_general(wr_ref[:N_EXPERTS, :], h2_lo, nt, preferred_element_type=F32))
    scores = jax.nn.sigmoid(logits)
    biased = scores + br_ref[...]
    ng = N_ROUTE_GROUPS
    gsz = N_EXPERTS // ng
    b3 = biased.reshape(ng, gsz, tm)
    s3 = scores.reshape(ng, gsz, tm)
    sub = lax.broadcasted_iota(jnp.int32, (ng, gsz, tm), 1).astype(F32)
    grp = lax.broadcasted_iota(jnp.int32, (ng, gsz, tm), 0).astype(F32)
    eid = grp * gsz + sub
    neg = -jnp.inf
    m1 = jnp.max(b3, axis=1, keepdims=True)
    i1 = jnp.min(jnp.where(b3 == m1, sub, float(gsz)), axis=1, keepdims=True)
    m2 = jnp.max(jnp.where(sub == i1, neg, b3), axis=1, keepdims=True)
    gs = m1 + m2
    gi = lax.broadcasted_iota(jnp.int32, (ng, 1, tm), 0)
    beaten = jnp.zeros((ng, 1, tm), F32)
    for gp in range(ng):
        o_ = gs[gp:gp + 1]
        beats = (o_ > gs) | ((o_ == gs) & (gi > gp))
        beaten = beaten + beats.astype(F32)
    gmask = beaten < float(TOPK_ROUTE_GROUPS)
    masked = jnp.where(gmask, b3, neg)

    sels = []
    picked = jnp.zeros((ng, gsz, tm), F32)
    for k in range(TOP_K):
        m = jnp.max(jnp.max(masked, axis=0, keepdims=True), axis=1, keepdims=True)
        cand = jnp.where(masked == m, eid, float(N_EXPERTS))
        sel = jnp.min(jnp.min(cand, axis=0, keepdims=True), axis=1, keepdims=True)
        oh = eid == sel
        masked = jnp.where(oh, neg, masked)
        picked = jnp.where(oh, 1.0, picked)
        sels.append(sel)

    pm = picked.reshape(N_EXPERTS, tm)
    prefix = jnp.dot(pm.astype(BF16), tri_ref[...], preferred_element_type=F32) + run_ref[:, 0:1]
    p3 = prefix.reshape(ng, gsz, tm)
    run_new = run_ref[...] + jnp.sum(pm, axis=1, keepdims=True)
    run_ref[...] = run_new
    cnt_ref[...] = run_new

    sc_rows = []
    for k in range(TOP_K):
        oh = eid == sels[k]
        sc_k = jnp.sum(jnp.sum(jnp.where(oh, s3, 0.0), axis=0, keepdims=True), axis=1, keepdims=True)
        rk_k = jnp.sum(jnp.sum(jnp.where(oh, p3, 0.0), axis=0, keepdims=True), axis=1, keepdims=True)
        sc_rows.append(sc_k)
        eidx_ref[k:k + 1, :] = sels[k].reshape(1, tm).astype(jnp.int32)
        rank_ref[k:k + 1, :] = rk_k.reshape(1, tm).astype(jnp.int32)
    tot = sc_rows[0]
    for k in range(1, TOP_K):
        tot = tot + sc_rows[k]
    inv = ROUTED_SCALE / (tot + 1e-20)
    for k in range(TOP_K):
        gw_ref[k:k + 1, :] = (sc_rows[k] * inv).reshape(1, tm)


def _mix(half, a_n, yt, u2, x2, mod, d_skip, wglu_bf, b_glu, g_out_ssm, wo_a, wo_b, g_post_mix,
         g_pre_ffn, w_router_split, b_router_col, tri):
    tm = TM_MIX
    t0 = half * (HALF_TOK // tm)
    row = lambda n: pl.BlockSpec((1, n), lambda i: (0, 0))
    full = lambda a, b: pl.BlockSpec((a, b), lambda i: (0, 0))
    tok_in = lambda n: pl.BlockSpec((tm, n), lambda i: (t0 + i, 0))
    tok = lambda n: pl.BlockSpec((tm, n), lambda i: (i, 0))
    col = pl.BlockSpec((TOP_K, tm), lambda i: (0, i))
    return pl.pallas_call(
        _mix_kernel,
        grid=(HALF_TOK // tm,),
        in_specs=[tok_in(CONV_CH),
                  pl.BlockSpec((SSM_GROUPS, tm // S5_Q, S5_QH), lambda i: (0, t0 + i, 0)),
                  tok_in(SSM_CH), tok_in(D_MODEL),
                  pl.BlockSpec((1, MOD_ROWS, D_MODEL), lambda i: (half, 0, 0)),
                  row(SSM_CH), full(SSM_CH, SSM_CH), row(SSM_CH), row(SSM_CH),
                  full(CONV_CH, D_MODEL), full(SSM_CH, D_MODEL), row(D_MODEL), row(D_MODEL),
                  full(2 * N_EXPERTS, D_MODEL), full(N_EXPERTS, 1), full(tm, tm)],
        out_specs=[tok(D_MODEL), tok(D_MODEL // 2), col, col, col,
                   pl.BlockSpec((N_EXPERTS, LANES), lambda i: (0, 0))],
        out_shape=[jax.ShapeDtypeStruct((HALF_TOK, D_MODEL), F32),
                   jax.ShapeDtypeStruct((HALF_TOK, D_MODEL // 2), jnp.uint32),
                   jax.ShapeDtypeStruct((TOP_K, HALF_TOK), jnp.int32),
                   jax.ShapeDtypeStruct((TOP_K, HALF_TOK), jnp.int32),
                   jax.ShapeDtypeStruct((TOP_K, HALF_TOK), F32),
                   jax.ShapeDtypeStruct((N_EXPERTS, LANES), F32)],
        scratch_shapes=[pltpu.VMEM((N_EXPERTS, LANES), F32),
                        pltpu.VMEM((SSM_CH // LANES, tm, LANES), F32)],
        compiler_params=_cparams(("arbitrary",)),
        name="mix_out_router",
    )(a_n, yt, u2, x2, mod, d_skip, wglu_bf, b_glu, g_out_ssm, wo_a, wo_b, g_post_mix,
      g_pre_ffn, w_router_split, b_router_col, tri)


def _expert_kernel(blk0_ref, nblk_ref, bunit_ref, bfull_ref, xs_hbm, wgu_ref, wd_ref, ys_hbm,
                   xbuf, ybuf, sem_in, sem_out):
    e = pl.program_id(0)
    n = nblk_ref[e]
    b0 = blk0_ref[e]
    n_all = blk0_ref[N_EXPERTS - 1] + nblk_ref[N_EXPERTS - 1]

    def copies(b, slot, rows):
        hbm = pl.ds(pl.multiple_of(bunit_ref[b] * ROW_UNIT, ROW_UNIT), rows)
        buf = pl.ds(0, rows)
        return (pltpu.make_async_copy(xs_hbm.at[hbm], xbuf.at[slot, buf], sem_in.at[slot]),
                pltpu.make_async_copy(ybuf.at[slot, buf], ys_hbm.at[hbm], sem_out.at[slot]))

    def for_size(b, fn):
        full = bfull_ref[b] == 1

        @pl.when(full)
        def _():
            fn(ROW_BLOCK)

        @pl.when(jnp.logical_not(full))
        def _():
            fn(ROW_UNIT)

    for b in range(EXPERT_AHEAD):
        @pl.when((e == 0) & (b < n_all))
        def _():
            for_size(b, lambda rows: copies(b, b, rows)[0].start())

    def compute(slot, rows):
        x_lo, x_hi = _unpack_halves(xbuf[slot, pl.ds(0, rows)])
        x = jnp.concatenate([x_lo.astype(BF16), x_hi.astype(BF16)], axis=1)
        h = jnp.dot(x, wgu_ref[0], preferred_element_type=F32)
        hg = h[:, :D_EXPERT]
        act = hg * jax.nn.sigmoid(hg) * h[:, D_EXPERT:]
        ybuf[slot, pl.ds(0, rows)] = _pack_halves(
            jnp.dot(act.astype(BF16), wd_ref[0], preferred_element_type=F32))

    def block(b, carry):
        slot = b % EXPERT_SLOTS
        ahead = b + EXPERT_AHEAD

        @pl.when(ahead < n_all)
        def _():
            for_size(ahead, lambda rows: copies(ahead, ahead % EXPERT_SLOTS, rows)[0].start())

        @pl.when(b >= EXPERT_SLOTS)
        def _():
            for_size(b - EXPERT_SLOTS, lambda rows: copies(b - EXPERT_SLOTS, slot, rows)[1].wait())

        def work(rows):
            cp_in, cp_out = copies(b, slot, rows)
            cp_in.wait()
            compute(slot, rows)
            cp_out.start()

        for_size(b, work)
        return carry

    lax.fori_loop(b0, b0 + n, block, 0)

    @pl.when(e == N_EXPERTS - 1)
    def _():
        for j in range(1, EXPERT_SLOTS + 1):
            @pl.when(n_all >= j)
            def _():
                last = n_all - j
                for_size(last, lambda rows: copies(last, last % EXPERT_SLOTS, rows)[1].wait())


def _experts(blk0, nblk, bunit, bfull, xs, we_gu, we_d):
    any_spec = pl.BlockSpec(memory_space=pl.ANY)
    grid_spec = pltpu.PrefetchScalarGridSpec(
        num_scalar_prefetch=4,
        grid=(N_EXPERTS,),
        in_specs=[any_spec,
                  pl.BlockSpec((1, D_MODEL, 2 * D_EXPERT), lambda e, *_: (e, 0, 0)),
                  pl.BlockSpec((1, D_EXPERT, D_MODEL), lambda e, *_: (e, 0, 0))],
        out_specs=any_spec,
        scratch_shapes=[pltpu.VMEM((EXPERT_SLOTS, ROW_BLOCK, D_MODEL // 2), jnp.uint32),
                        pltpu.VMEM((EXPERT_SLOTS, ROW_BLOCK, D_MODEL // 2), jnp.uint32),
                        pltpu.SemaphoreType.DMA((EXPERT_SLOTS,)),
                        pltpu.SemaphoreType.DMA((EXPERT_SLOTS,))],
    )
    return pl.pallas_call(
        _expert_kernel,
        grid_spec=grid_spec,
        out_shape=jax.ShapeDtypeStruct((N_ROWS, D_MODEL // 2), jnp.uint32),
        compiler_params=_cparams(("arbitrary",)),
        name="routed_experts",
    )(blk0, nblk, bunit, bfull, xs, we_gu, we_d)


def _final_kernel(h2_ref, yg_ref, gw_ref, x1_ref, mod_ref, wgu_ref, wd_ref, g_ref, *rest):
    o_ref = rest[-1]
    half = D_MODEL // 2
    gt_f = mod_ref[0, 5:6, :]
    x_lo, x_hi = _unpack_halves(h2_ref[...])
    h = (jnp.dot(x_lo.astype(BF16), wgu_ref[:half, :], preferred_element_type=F32)
         + jnp.dot(x_hi.astype(BF16), wgu_ref[half:, :], preferred_element_type=F32))
    hg = h[:, :D_EXPERT]
    act = hg * jax.nn.sigmoid(hg) * h[:, D_EXPERT:]
    shared = jnp.dot(act.astype(BF16), wd_ref[...], preferred_element_type=F32)
    y_lo = shared[:, :half]
    y_hi = shared[:, half:]
    for k in range(TOP_K):
        r_lo, r_hi = _unpack_halves(yg_ref[k])
        w = gw_ref[:, k:k + 1]
        y_lo = y_lo + w * r_lo
        y_hi = y_hi + w * r_hi
    ms = (jnp.sum(y_lo * y_lo, axis=-1, keepdims=True)
          + jnp.sum(y_hi * y_hi, axis=-1, keepdims=True)) * (1.0 / D_MODEL)
    inv = lax.rsqrt(ms + NORM_EPS)
    o_ref[:, :half] = x1_ref[:, :half] + gt_f[:, :half] * (y_lo * inv * g_ref[:, :half])
    o_ref[:, half:] = x1_ref[:, half:] + gt_f[:, half:] * (y_hi * inv * g_ref[:, half:])


def _final(half, out_prev, h2p, yg, gw_t, x1, mod, ws_gu, ws_d, g_post_ffn):
    tm = TM_OUT
    t0 = half * (HALF_TOK // tm)
    tok = pl.BlockSpec((tm, D_MODEL), lambda i: (i, 0))
    in_specs = [pl.BlockSpec((tm, D_MODEL // 2), lambda i: (i, 0)),
                pl.BlockSpec((TOP_K, tm, D_MODEL // 2), lambda i: (0, i, 0)),
                pl.BlockSpec((tm, TOP_K), lambda i: (i, 0)),
                tok,
                pl.BlockSpec((1, MOD_ROWS, D_MODEL), lambda i: (half, 0, 0)),
                pl.BlockSpec((D_MODEL, 2 * D_EXPERT), lambda i: (0, 0)),
                pl.BlockSpec((D_EXPERT, D_MODEL), lambda i: (0, 0)),
                pl.BlockSpec((1, D_MODEL), lambda i: (0, 0))]
    args = [h2p, yg, gw_t, x1, mod, ws_gu, ws_d, g_post_ffn]
    aliases = {}
    if out_prev is not None:
        aliases = {len(args): 0}
        in_specs.append(pl.BlockSpec(memory_space=pl.ANY))
        args.append(out_prev)
    return pl.pallas_call(
        _final_kernel,
        grid=(HALF_TOK // tm,),
        in_specs=in_specs,
        out_specs=pl.BlockSpec((tm, D_MODEL), lambda i: (t0 + i, 0)),
        out_shape=jax.ShapeDtypeStruct((N_TOK, D_MODEL), F32),
        input_output_aliases=aliases,
        compiler_params=_cparams(("parallel",)),
        name="shared_final",
    )(*args)


def _sc_worker_id():
    return lax.axis_index("s") * SC_CORES + lax.axis_index("c")


def _dispatch_body(h_hbm, dest_hbm, xs_hbm, idx_v, rows_v, sem_l, sem_s):
    n = SC_CHUNKS_PER_WORKER
    c0 = _sc_worker_id() * n

    def load(i, b):
        return pltpu.async_copy(h_hbm.at[pl.ds((c0 + i) * SC_W, SC_W)], rows_v.at[b], sem_l.at[b])

    loads = [None] * n
    scat = [None] * n
    loads[0] = load(0, 0)
    for i in range(n):
        b = i % 2
        pltpu.sync_copy(dest_hbm.at[c0 + i], idx_v.at[b])
        loads[i].wait()
        if i + 1 < n:
            if i >= 1:
                for d in scat[i - 1]:
                    d.wait()
            loads[i + 1] = load(i + 1, 1 - b)
        scat[i] = [pltpu.async_copy(rows_v.at[b], xs_hbm.at[idx_v.at[b].at[k]], sem_s.at[b])
                   for k in range(TOP_K)]
    for i in (n - 2, n - 1):
        for d in scat[i]:
            d.wait()


def _sc_dispatch(h2p, dest3):
    mesh = plsc.VectorSubcoreMesh(core_axis_name="c", subcore_axis_name="s")
    return pl.kernel(
        _dispatch_body, mesh=mesh,
        out_type=jax.ShapeDtypeStruct((N_ROWS, D_MODEL // 2), jnp.uint32),
        scratch_types=[pltpu.VMEM((2, TOP_K, SC_W), jnp.int32),
                       pltpu.VMEM((2, SC_W, D_MODEL // 2), jnp.uint32),
                       pltpu.SemaphoreType.DMA((2,)), pltpu.SemaphoreType.DMA((2,))],
    )(h2p, dest3)


def _combine_body(ys_hbm, dest_hbm, yg_hbm, idx_v, rows_v, sem_g, sem_w):
    c0 = _sc_worker_id() * SC_CHUNKS_PER_WORKER

    @pl.loop(0, SC_CHUNKS_PER_WORKER)
    def _(i):
        c = c0 + i
        pltpu.sync_copy(dest_hbm.at[c], idx_v)
        g = [None] * TOP_K
        w = [None] * TOP_K
        g[0] = pltpu.async_copy(ys_hbm.at[idx_v.at[0]], rows_v.at[0], sem_g.at[0])
        for k in range(TOP_K):
            b = k % 2
            g[k].wait()
            if k + 1 < TOP_K:
                if k >= 1:
                    w[k - 1].wait()
                g[k + 1] = pltpu.async_copy(ys_hbm.at[idx_v.at[k + 1]], rows_v.at[1 - b], sem_g.at[1 - b])
            w[k] = pltpu.async_copy(rows_v.at[b], yg_hbm.at[k].at[pl.ds(c * SC_W, SC_W)], sem_w.at[b])
        w[TOP_K - 2].wait()
        w[TOP_K - 1].wait()


def _sc_combine(ysp, dest3):
    mesh = plsc.VectorSubcoreMesh(core_axis_name="c", subcore_axis_name="s")
    return pl.kernel(
        _combine_body, mesh=mesh,
        out_type=jax.ShapeDtypeStruct((TOP_K, HALF_TOK, D_MODEL // 2), jnp.uint32),
        scratch_types=[pltpu.VMEM((TOP_K, SC_W), jnp.int32),
                       pltpu.VMEM((2, SC_W, D_MODEL // 2), jnp.uint32),
                       pltpu.SemaphoreType.DMA((2,)), pltpu.SemaphoreType.DMA((2,))],
    )(ysp, dest3)


def kernel(x, c, w_ada, b_ada, g_pre_mix, g_post_mix, w_in, conv_w, conv_b, conv_ln_g, conv_ln_b,
           ssm_a_re, ssm_a_im, ssm_log_dt, ssm_b_re, ssm_b_im, ssm_c_re, ssm_c_im, ssm_d,
           ssm_w_glu, ssm_b_glu, g_out_conv, g_out_ssm, w_out, g_pre_ffn, g_post_ffn,
           w_router, b_router, we_gate, we_up, we_down, ws_gate, ws_up, ws_down):
    l = 0
    x2 = x.reshape(N_TOK, D_MODEL)
    r1 = lambda a: a.reshape(1, -1)

    c_pad = jnp.zeros((SUBLANES, D_MODEL), F32).at[:BATCH].set(c)
    mod = _ada(c_pad, w_ada[l], r1(b_ada[l]))[:BATCH].reshape(BATCH, N_MOD, D_MODEL)
    mod = jnp.concatenate([mod, jnp.zeros((BATCH, MOD_ROWS - N_MOD, D_MODEL), F32)], axis=1)

    v, u, ut = _inproj(x2, mod, r1(g_pre_mix[l]), w_in[l].astype(BF16))
    cw = jnp.concatenate([conv_w[l].reshape(CONV_WIDTH, CONV_CH), jnp.zeros((1, CONV_CH), F32)], axis=0)
    a_n, we_gu, we_d = _conv(v.reshape(BATCH, SEQ, CONV_CH), cw, r1(conv_b[l]), r1(conv_ln_g[l]),
                             r1(conv_ln_b[l]), r1(g_out_conv[l]), we_gate[l], we_up[l], we_down[l])
    a_n = a_n.reshape(N_TOK, CONV_CH)

    pwr, pwi, s5_params, a_cat, b_q, b_s = _s5_operators(
        ssm_a_re[l], ssm_a_im[l], ssm_log_dt[l], ssm_b_re[l], ssm_b_im[l], ssm_c_re[l], ssm_c_im[l])
    yt = _s5(ut, pwr, pwi, s5_params, a_cat, b_q, b_s)

    tm = TM_MIX
    tri = (jnp.arange(tm)[:, None] < jnp.arange(tm)[None, :]).astype(BF16)
    wo = w_out[l].astype(BF16)
    wr_t = w_router[l].T
    wr_hi = wr_t.astype(BF16)
    wr_split = jnp.concatenate([wr_hi, (wr_t - wr_hi.astype(F32)).astype(BF16)], axis=0)
    mix_params = (r1(ssm_d[l]), ssm_w_glu[l].astype(BF16), r1(ssm_b_glu[l]), r1(g_out_ssm[l]),
                  wo[:CONV_CH], wo[CONV_CH:], r1(g_post_mix[l]), r1(g_pre_ffn[l]),
                  wr_split, b_router[l].reshape(N_EXPERTS, 1), tri)
    ws_gu = jnp.concatenate([ws_gate[l], ws_up[l]], axis=1).astype(BF16)
    ws_d = ws_down[l].astype(BF16)
    e_ids = jnp.arange(N_EXPERTS, dtype=jnp.int32)

    out = None
    for half in range(N_HALVES):
        x1, h2, eidx, rank, gw, cnt = _mix(half, a_n, yt, u, x2, mod, *mix_params)
        counts = cnt[:, 0].astype(jnp.int32)
        units = (counts + ROW_UNIT - 1) // ROW_UNIT
        unit0 = jnp.cumsum(units) - units
        nblk = (units + 1) // 2
        blk0 = jnp.cumsum(nblk) - nblk
        dest = rank + jnp.sum(jnp.where(eidx[..., None] == e_ids, unit0 * ROW_UNIT, 0), axis=-1)
        dest3 = dest.reshape(TOP_K, HALF_TOK // SC_W, SC_W).transpose(1, 0, 2)
        b_ids = jnp.arange(N_BLOCKS, dtype=jnp.int32)
        owner = (b_ids[:, None] >= blk0[None, :]) & (b_ids[:, None] < (blk0 + nblk)[None, :])
        pick = lambda v: jnp.sum(jnp.where(owner, v[None, :], 0), axis=1)
        local = b_ids - pick(blk0)
        bunit = pick(unit0) + 2 * local
        bfull = (2 * local + 2 <= pick(units)).astype(jnp.int32)

        xs = _sc_dispatch(h2, dest3)
        ys = _experts(blk0, nblk, bunit, bfull, xs, we_gu, we_d)
        yg = _sc_combine(ys, dest3)
        out = _final(half, out, h2, yg, gw.T, x1, mod, ws_gu, ws_d, r1(g_post_ffn[l]))
    return out.reshape(BATCH, SEQ, D_MODEL)
```

```python
import math

import jax
import jax.numpy as jnp
from jax import lax
from jax.experimental import pallas as pl
from jax.experimental.pallas import tpu as pltpu
from jax.experimental.pallas import tpu_sc as plsc

F32 = jnp.float32
BF16 = jnp.bfloat16

D_MODEL = 1024
BATCH = 2
SEQ = 8192
N_TOK = BATCH * SEQ
CONV_CH = 512
CONV_WIDTH = 31
SSM_CH = 512
SSM_GROUP_CH = 16
SSM_GROUPS = 32
SSM_STATE = 64
D_IN = 2 * CONV_CH + SSM_CH
N_EXPERTS = 64
TOP_K = 8
N_ROUTE_GROUPS = 8
TOPK_ROUTE_GROUPS = 4
D_EXPERT = 256
ROUTED_SCALE = 2.5
NORM_EPS = 1e-6

SUBLANES = 8
LANES = 128

N_MOD = 6
MOD_ROWS = SUBLANES
ADA_COLS = 1536
TM_IN = 1024
IN_SUBTILES = 4
TL_CONV = 512
CONV_HALO = 32
CONV_ROWS = 64
EXPERTS_PER_CONV_STEP = N_EXPERTS * TL_CONV // N_TOK
assert EXPERTS_PER_CONV_STEP * N_TOK == N_EXPERTS * TL_CONV
S5_Q = 32
S5_GROUPS_PER_STEP = 4
S5_QH = S5_Q * SSM_GROUP_CH
S5_CHUNKS = N_TOK // S5_Q
S5_CHUNKS_PER_SEQ = SEQ // S5_Q
TM_MIX = 1024
ROW_BLOCK = 1024
ROW_UNIT = ROW_BLOCK // 2
EXPERT_AHEAD = 4
EXPERT_SLOTS = EXPERT_AHEAD + 1
HALF_TOK = SEQ
N_HALVES = N_TOK // HALF_TOK
N_UNITS = HALF_TOK * TOP_K // ROW_UNIT + N_EXPERTS
N_BLOCKS = (N_UNITS + N_EXPERTS) // 2
N_ROWS = N_UNITS * ROW_UNIT
TM_OUT = 512
SC_CORES = 2
SC_SUBCORES = 16
SC_WORKERS = SC_CORES * SC_SUBCORES
SC_W = 64
SC_CHUNKS_PER_WORKER = HALF_TOK // (SC_WORKERS * SC_W)
VMEM_LIMIT = 48 * 1024 * 1024


def _cparams(sem):
    return pltpu.CompilerParams(dimension_semantics=sem, vmem_limit_bytes=VMEM_LIMIT)


def _pack_rounded_halves(xr):
    n = xr.shape[-1] // 2
    lo = lax.bitcast_convert_type(xr[:, :n], jnp.uint32)
    hi = lax.bitcast_convert_type(xr[:, n:], jnp.uint32)
    return hi | (lo >> 16)


def _pack_halves(x):
    return _pack_rounded_halves(x.astype(BF16).astype(F32))


def _unpack_halves(p):
    lo = lax.bitcast_convert_type(p << 16, F32)
    hi = lax.bitcast_convert_type(p & jnp.uint32(0xFFFF0000), F32)
    return lo, hi


def _rms(x, g):
    return x * lax.rsqrt(jnp.mean(x * x, axis=-1, keepdims=True) + NORM_EPS) * g


def _split_bf16(x):
    hi = x.astype(BF16)
    return hi, (x - hi.astype(F32)).astype(BF16)


def _dot_nt_split(a, b):
    nt = (((1,), (1,)), ((), ()))
    a_hi, a_lo = _split_bf16(a)
    b_hi, b_lo = _split_bf16(b)
    m = a.shape[0]
    both = lax.dot_general(jnp.concatenate([a_hi, a_lo], axis=0), b_hi, nt, preferred_element_type=F32)
    return both[:m] + both[m:] + lax.dot_general(a_hi, b_lo, nt, preferred_element_type=F32)


def _ada_kernel(c_ref, w_ref, b_ref, o_ref):
    c = c_ref[...]
    a = c * jax.nn.sigmoid(c)
    o_ref[...] = jnp.dot(a, w_ref[...], preferred_element_type=F32,
                         precision=lax.Precision.HIGHEST) + b_ref[...]


def _ada(c_pad, w_ada, b_ada):
    n = w_ada.shape[1]
    bn = ADA_COLS
    return pl.pallas_call(
        _ada_kernel,
        grid=(n // bn,),
        in_specs=[pl.BlockSpec((SUBLANES, D_MODEL), lambda j: (0, 0)),
                  pl.BlockSpec((D_MODEL, bn), lambda j: (0, j)),
                  pl.BlockSpec((1, bn), lambda j: (0, j))],
        out_specs=pl.BlockSpec((SUBLANES, bn), lambda j: (0, j)),
        out_shape=jax.ShapeDtypeStruct((SUBLANES, n), F32),
        compiler_params=_cparams(("arbitrary",)),
        name="ada_mod",
    )(c_pad, w_ada, b_ada)


GROUPS_PER_LANE_TILE = LANES // SSM_GROUP_CH


def _to_group_chunks(u, tile_ref, ut_ref):
    n_chunks = u.shape[0] // S5_Q
    for j in range(SSM_CH // LANES):
        tile_ref[j] = u[:, LANES * j:LANES * (j + 1)]
    for j in range(SSM_CH // LANES):
        rows_t = [tile_ref[j, pl.ds(t, n_chunks, stride=S5_Q), :] for t in range(S5_Q)]
        for gg in range(GROUPS_PER_LANE_TILE):
            lo = gg * SSM_GROUP_CH
            row = jnp.concatenate([r[:, lo:lo + SSM_GROUP_CH] for r in rows_t], axis=1)
            ut_ref[j * GROUPS_PER_LANE_TILE + gg] = row.astype(ut_ref.dtype)


def _from_group_chunks(yt_ref, tile_ref):
    n_chunks = yt_ref.shape[1]
    for j in range(SSM_CH // LANES):
        for t in range(S5_Q):
            lo = t * SSM_GROUP_CH
            piece = jnp.concatenate(
                [yt_ref[j * GROUPS_PER_LANE_TILE + gg, :, lo:lo + SSM_GROUP_CH]
                 for gg in range(GROUPS_PER_LANE_TILE)], axis=1)
            tile_ref[j, pl.ds(t, n_chunks, stride=S5_Q), :] = piece
    return jnp.concatenate([tile_ref[j] for j in range(SSM_CH // LANES)], axis=1)


def _inproj_kernel(x_ref, mod_ref, g_ref, w_ref, v_ref, u_ref, ut_ref, tile_ref):
    sh = mod_ref[0, 0:1, :]
    sc = mod_ref[0, 1:2, :]
    sub = TM_IN // IN_SUBTILES
    sub_chunks = sub // S5_Q
    for s in range(IN_SUBTILES):
        r = slice(s * sub, (s + 1) * sub)
        h = _rms(x_ref[r, :], g_ref[...]) * (1.0 + sc) + sh
        z = jnp.dot(h.astype(BF16), w_ref[...], preferred_element_type=F32)
        v_ref[r, :] = z[:, :CONV_CH] * jax.nn.sigmoid(z[:, CONV_CH:2 * CONV_CH])
        u = z[:, 2 * CONV_CH:]
        u_ref[r, :] = u
        _to_group_chunks(u, tile_ref.at[s], ut_ref.at[:, s * sub_chunks:(s + 1) * sub_chunks, :])


def _inproj(x2, mod, g_pre, w_in_bf):
    tiles_per_seq = SEQ // TM_IN
    return pl.pallas_call(
        _inproj_kernel,
        grid=(N_TOK // TM_IN,),
        in_specs=[pl.BlockSpec((TM_IN, D_MODEL), lambda i: (i, 0)),
                  pl.BlockSpec((1, MOD_ROWS, D_MODEL), lambda i: (i // tiles_per_seq, 0, 0)),
                  pl.BlockSpec((1, D_MODEL), lambda i: (0, 0)),
                  pl.BlockSpec((D_MODEL, D_IN), lambda i: (0, 0))],
        out_specs=[pl.BlockSpec((TM_IN, CONV_CH), lambda i: (i, 0)),
                   pl.BlockSpec((TM_IN, SSM_CH), lambda i: (i, 0)),
                   pl.BlockSpec((SSM_GROUPS, TM_IN // S5_Q, S5_QH), lambda i: (0, i, 0))],
        out_shape=[jax.ShapeDtypeStruct((N_TOK, CONV_CH), F32),
                   jax.ShapeDtypeStruct((N_TOK, SSM_CH), F32),
                   jax.ShapeDtypeStruct((SSM_GROUPS, S5_CHUNKS, S5_QH), BF16)],
        scratch_shapes=[pltpu.VMEM((IN_SUBTILES, SSM_CH // LANES, TM_IN // IN_SUBTILES, LANES), F32)],
        compiler_params=_cparams(("parallel",)),
        name="in_proj",
    )(x2, mod, g_pre, w_in_bf)


def _conv_kernel(vc_ref, vp_ref, w_ref, cb_ref, lg_ref, lb_ref, go_ref, wg_ref, wu_ref, wd_ref,
                 o_ref, wgu_o, wd_o, sh_ref):
    for q in range(EXPERTS_PER_CONV_STEP):
        wgu_o[q, :, :D_EXPERT] = wg_ref[q].astype(BF16)
        wgu_o[q, :, D_EXPERT:] = wu_ref[q].astype(BF16)
        wd_o[q] = wd_ref[q].astype(BF16)

    i = pl.program_id(1)
    keep = (i > 0).astype(F32)
    n_ext = TL_CONV + CONV_HALO
    sh_ref[0, 0:CONV_HALO, :] = vp_ref[0] * keep
    sh_ref[0, CONV_HALO:, :] = vc_ref[0]
    ext = sh_ref[0]
    for s in range(1, SUBLANES):
        sh_ref[s] = pltpu.roll(ext, n_ext - s, axis=0)
    off = CONV_HALO - (CONV_WIDTH - 1)
    for r in range(TL_CONV // CONV_ROWS):
        acc = None
        for j in range(CONV_WIDTH):
            s = (off + j) % SUBLANES
            al = r * CONV_ROWS + (off + j) - s
            term = w_ref[j:j + 1, :] * sh_ref[s, al:al + CONV_ROWS, :]
            acc = term if acc is None else acc + term
        y = acc + cb_ref[...]
        mu = jnp.mean(y, axis=-1, keepdims=True)
        d = y - mu
        var = jnp.mean(d * d, axis=-1, keepdims=True)
        yn = d * lax.rsqrt(var + NORM_EPS) * lg_ref[...] + lb_ref[...]
        a = yn * jax.nn.sigmoid(yn)
        o_ref[0, r * CONV_ROWS:(r + 1) * CONV_ROWS, :] = _rms(a, go_ref[...]).astype(BF16)


def _conv(v3, conv_w, conv_b, ln_g, ln_b, g_out, we_gate, we_up, we_down):
    halo_per_tile = TL_CONV // CONV_HALO
    steps_per_seq = SEQ // TL_CONV
    vec = pl.BlockSpec((1, CONV_CH), lambda b, i: (0, 0))
    ex = EXPERTS_PER_CONV_STEP
    w_in = pl.BlockSpec((ex, D_MODEL, D_EXPERT), lambda b, i: (b * steps_per_seq + i, 0, 0))
    return pl.pallas_call(
        _conv_kernel,
        grid=(BATCH, steps_per_seq),
        in_specs=[pl.BlockSpec((1, TL_CONV, CONV_CH), lambda b, i: (b, i, 0)),
                  pl.BlockSpec((1, CONV_HALO, CONV_CH),
                               lambda b, i: (b, jnp.maximum(i * halo_per_tile - 1, 0), 0)),
                  pl.BlockSpec((CONV_WIDTH + 1, CONV_CH), lambda b, i: (0, 0)),
                  vec, vec, vec, vec,
                  w_in, w_in,
                  pl.BlockSpec((ex, D_EXPERT, D_MODEL), lambda b, i: (b * steps_per_seq + i, 0, 0))],
        out_specs=[pl.BlockSpec((1, TL_CONV, CONV_CH), lambda b, i: (b, i, 0)),
                   pl.BlockSpec((ex, D_MODEL, 2 * D_EXPERT), lambda b, i: (b * steps_per_seq + i, 0, 0)),
                   pl.BlockSpec((ex, D_EXPERT, D_MODEL), lambda b, i: (b * steps_per_seq + i, 0, 0))],
        out_shape=[jax.ShapeDtypeStruct((BATCH, SEQ, CONV_CH), BF16),
                   jax.ShapeDtypeStruct((N_EXPERTS, D_MODEL, 2 * D_EXPERT), BF16),
                   jax.ShapeDtypeStruct((N_EXPERTS, D_EXPERT, D_MODEL), BF16)],
        scratch_shapes=[pltpu.VMEM((SUBLANES, TL_CONV + CONV_HALO, CONV_CH), F32)],
        compiler_params=_cparams(("parallel", "arbitrary")),
        name="conv_module",
    )(v3, v3, conv_w, conv_b, ln_g, ln_b, g_out, we_gate, we_up, we_down)


S5_GROUP_ROWS = S5_CHUNKS + SUBLANES


S5_POW_ROWS = (S5_Q + 1 + SUBLANES - 1) // SUBLANES * SUBLANES
(S5_BB_RI, S5_BB_NIR, S5_BB_IR, S5_BB_RNI, S5_CC_RI, S5_CC_NIR, S5_N_PARAM) = range(7)


def _s5_kernel(ut_ref, pwr_ref, pwi_ref, par_ref, a_ref, bq_ref, bs_ref, yt_ref, sin_s, sp_s):
    phase = pl.program_id(0)
    g = pl.program_id(1)
    q = S5_Q
    n = 2 * SSM_STATE

    def group_row0(k):
        return pl.multiple_of((g * S5_GROUPS_PER_STEP + k) * S5_GROUP_ROWS, SUBLANES)

    def lam_pow(k, j):
        return pwr_ref[k, j:j + 1, :], pwi_ref[k, j:j + 1, :]

    @pl.when(phase == 0)
    def _():
        for k in range(S5_GROUPS_PER_STEP):
            bb_ri, bb_nir = par_ref[k, S5_BB_RI], par_ref[k, S5_BB_NIR]
            bb_ir, bb_rni = par_ref[k, S5_BB_IR], par_ref[k, S5_BB_RNI]
            blk_q, blk_s = [], []
            for t in range(q):
                pr, pi_ = lam_pow(k, q - 1 - t)
                blk_q.append(pr * bb_ri + pi_ * bb_nir)
                blk_s.append(pr * bb_ir + pi_ * bb_rni)
            wst = jnp.concatenate([jnp.concatenate(blk_q, axis=0), jnp.concatenate(blk_s, axis=0)], axis=1)
            r = jnp.dot(ut_ref[k], wst.astype(BF16), preferred_element_type=F32)
            sin_s[0, pl.ds(group_row0(k), S5_CHUNKS), :] = r[:, :n]
            sin_s[1, pl.ds(group_row0(k), S5_CHUNKS), :] = r[:, n:]

    @pl.when((phase == 1) & (g == 0))
    def _():
        a = a_ref[...]
        bq = bq_ref[...]
        bs = bs_ref[...]

        def body(c, carry):
            nxt = []
            for b in range(BATCH):
                x, xs = carry[b]
                rows = pl.ds(b * S5_CHUNKS_PER_SEQ + c, SSM_GROUPS, stride=S5_GROUP_ROWS)
                sp_s[rows, :] = x
                nxt.append((a * x + bq * xs + sin_s[0, rows, :], a * xs + bs * x + sin_s[1, rows, :]))
            return tuple(nxt)

        z = jnp.zeros((SSM_GROUPS, n), F32)
        lax.fori_loop(0, S5_CHUNKS_PER_SEQ, body, tuple((z, z) for _ in range(BATCH)))

    @pl.when(phase == 1)
    def _():
        for k in range(S5_GROUPS_PER_STEP):
            cc_ri, cc_nir = par_ref[k, S5_CC_RI], par_ref[k, S5_CC_NIR]
            cl = []
            for j in range(q + 1):
                pr, pi_ = lam_pow(k, j)
                cl.append(pr * cc_ri + pi_ * cc_nir)
            cl_lo = jnp.concatenate(cl[:q], axis=0)
            cl_hi = jnp.concatenate(cl[1:], axis=0)
            lane = lax.broadcasted_iota(jnp.int32, (1, n), 1)
            vgt = (cl_hi * jnp.where(lane < SSM_STATE, 1.0, -1.0)).astype(BF16)
            kt = _dot_nt_split(par_ref[k, S5_BB_RNI], cl_lo)
            padded = jnp.concatenate([jnp.zeros_like(kt), kt], axis=1)
            tg = jnp.concatenate(
                [padded[:, (q - t) * SSM_GROUP_CH:(q - t) * SSM_GROUP_CH + S5_QH] for t in range(q)],
                axis=0).astype(BF16)
            sp = sp_s[pl.ds(group_row0(k), S5_CHUNKS), :]
            y = jnp.dot(ut_ref[k], tg, preferred_element_type=F32)
            yt_ref[k] = y + lax.dot_general(sp.astype(BF16), vgt, (((1,), (1,)), ((), ())),
                                            preferred_element_type=F32)


def _s5(ut, pwr, pwi, params, a_cat, b_q, b_s):
    vec = pl.BlockSpec((SSM_GROUPS, 2 * SSM_STATE), lambda p, g: (0, 0))
    gs = S5_GROUPS_PER_STEP
    powers = pl.BlockSpec((gs, S5_POW_ROWS, 2 * SSM_STATE), lambda p, g: (g, 0, 0))
    return pl.pallas_call(
        _s5_kernel,
        grid=(2, SSM_GROUPS // gs),
        in_specs=[pl.BlockSpec((gs, S5_CHUNKS, S5_QH), lambda p, g: (g, 0, 0)),
                  powers, powers,
                  pl.BlockSpec((gs, S5_N_PARAM, SSM_GROUP_CH, 2 * SSM_STATE), lambda p, g: (g, 0, 0, 0)),
                  vec, vec, vec],
        out_specs=pl.BlockSpec((gs, S5_CHUNKS, S5_QH), lambda p, g: (g * p, 0, 0)),
        out_shape=jax.ShapeDtypeStruct((SSM_GROUPS, S5_CHUNKS, S5_QH), F32),
        scratch_shapes=[pltpu.VMEM((2, SSM_GROUPS * S5_GROUP_ROWS, 2 * SSM_STATE), F32),
                        pltpu.VMEM((SSM_GROUPS * S5_GROUP_ROWS, 2 * SSM_STATE), F32)],
        compiler_params=_cparams(("arbitrary", "arbitrary")),
        name="s5_chunked",
    )(ut, pwr, pwi, params, a_cat, b_q, b_s)


def _s5_operators(a_re, a_im, log_dt, b_re, b_im, c_re, c_im):
    q = S5_Q
    dt = jnp.exp(log_dt)[:, None]
    ar, ai = a_re, a_im
    mag = jnp.exp(ar * dt)
    lr = mag * jnp.cos(ai * dt)
    li = mag * jnp.sin(ai * dt)
    den = ar * ar + ai * ai
    nr = lr - 1.0
    kr = (nr * ar + li * ai) / den
    ki = (li * ar - nr * ai) / den
    bbr = kr[..., None] * b_re - ki[..., None] * b_im
    bbi = kr[..., None] * b_im + ki[..., None] * b_re
    j = jnp.arange(q + 1, dtype=F32)[None, :, None]
    pmag = jnp.exp(ar[:, None, :] * dt[:, :, None] * j)
    pang = ai[:, None, :] * dt[:, :, None] * j
    pr = pmag * jnp.cos(pang)
    pi_ = pmag * jnp.sin(pang)
    pad = ((0, 0), (0, S5_POW_ROWS - (q + 1)), (0, 0))
    pwr = jnp.pad(jnp.concatenate([pr, pr], axis=-1), pad)
    pwi = jnp.pad(jnp.concatenate([pi_, pi_], axis=-1), pad)
    br_t = bbr.transpose(0, 2, 1)
    bi_t = bbi.transpose(0, 2, 1)
    cat = lambda a, b: jnp.concatenate([a, b], axis=-1)
    stack = [None] * S5_N_PARAM
    stack[S5_BB_RI] = cat(br_t, bi_t)
    stack[S5_BB_NIR] = cat(-bi_t, br_t)
    stack[S5_BB_IR] = cat(bi_t, br_t)
    stack[S5_BB_RNI] = cat(br_t, -bi_t)
    stack[S5_CC_RI] = cat(c_re, c_im)
    stack[S5_CC_NIR] = cat(-c_im, c_re)
    params = jnp.stack(stack, axis=1)
    aq_r, aq_i = pr[:, q], pi_[:, q]
    a_cat = cat(aq_r, aq_r)
    b_q = cat(-aq_i, aq_i)
    b_s = cat(aq_i, -aq_i)
    return pwr, pwi, params, a_cat, b_q, b_s


def _gelu_tanh(x):
    return 0.5 * x * (1.0 + jnp.tanh(math.sqrt(2.0 / math.pi) * (x + 0.044715 * (x * x * x))))


def _mix_kernel(an_ref, yt_ref, u_ref, x_ref, mod_ref, d_ref, wglu_ref, bglu_ref, gos_ref,
                woa_ref, wob_ref, gpm_ref, gpf_ref, wr_ref, br_ref, tri_ref,
                x1_ref, h2_ref, eidx_ref, rank_ref, gw_ref, cnt_ref, run_ref, tile_ref):
    i = pl.program_id(0)
    tm = TM_MIX

    @pl.when(i == 0)
    def _():
        run_ref[...] = jnp.zeros_like(run_ref)

    gt_m = mod_ref[0, 2:3, :]
    sh_f = mod_ref[0, 3:4, :]
    sc_f = mod_ref[0, 4:5, :]

    yy = _from_group_chunks(yt_ref, tile_ref) + d_ref[...] * u_ref[...]
    g = _gelu_tanh(yy)
    gl = jnp.dot(g.astype(BF16), wglu_ref[...], preferred_element_type=F32) + bglu_ref[...]
    ob = g * jax.nn.sigmoid(gl)
    bn = _rms(ob, gos_ref[...]).astype(BF16)
    o = (jnp.dot(an_ref[...], woa_ref[...], preferred_element_type=F32)
         + jnp.dot(bn, wob_ref[...], preferred_element_type=F32))
    x1 = x_ref[...] + gt_m * _rms(o, gpm_ref[...])
    x1_ref[...] = x1
    h2 = _rms(x1, gpf_ref[...]) * (1.0 + sc_f) + sh_f
    h2_hi = h2.astype(BF16)
    h2_hi32 = h2_hi.astype(F32)
    h2_ref[...] = _pack_rounded_halves(h2_hi32)

    h2_lo = (h2 - h2_hi32).astype(BF16)
    nt = (((1,), (1,)), ((), ()))
    both = lax.dot_general(wr_ref[...], h2_hi, nt, preferred_element_type=F32)
    logits = (both[:N_EXPERTS] + both[N_EXPERTS:]
              + lax.dot_general(wr_ref[:N_EXPERTS, :], h2_lo, nt, preferred_element_type=F32))
    scores = jax.nn.sigmoid(logits)
    biased = scores + br_ref[...]
    ng = N_ROUTE_GROUPS
    gsz = N_EXPERTS // ng
    b3 = biased.reshape(ng, gsz, tm)
    s3 = scores.reshape(ng, gsz, tm)
    sub = lax.broadcasted_iota(jnp.int32, (ng, gsz, tm), 1).astype(F32)
    grp = lax.broadcasted_iota(jnp.int32, (ng, gsz, tm), 0).astype(F32)
    eid = grp * gsz + sub
    neg = -jnp.inf
    m1 = jnp.max(b3, axis=1, keepdims=True)
    i1 = jnp.min(jnp.where(b3 == m1, sub, float(gsz)), axis=1, keepdims=True)
    m2 = jnp.max(jnp.where(sub == i1, neg, b3), axis=1, keepdims=True)
    gs = m1 + m2
    gi = lax.broadcasted_iota(jnp.int32, (ng, 1, tm), 0)
    beaten = jnp.zeros((ng, 1, tm), F32)
    for gp in range(ng):
        o_ = gs[gp:gp + 1]
        beats = (o_ > gs) | ((o_ == gs) & (gi > gp))
        beaten = beaten + beats.astype(F32)
    gmask = beaten < float(TOPK_ROUTE_GROUPS)
    masked = jnp.where(gmask, b3, neg)

    sels = []
    picked = jnp.zeros((ng, gsz, tm), F32)
    for k in range(TOP_K):
        m = jnp.max(jnp.max(masked, axis=0, keepdims=True), axis=1, keepdims=True)
        cand = jnp.where(masked == m, eid, float(N_EXPERTS))
        sel = jnp.min(jnp.min(cand, axis=0, keepdims=True), axis=1, keepdims=True)
        oh = eid == sel
        masked = jnp.where(oh, neg, masked)
        picked = jnp.where(oh, 1.0, picked)
        sels.append(sel)

    pm = picked.reshape(N_EXPERTS, tm)
    prefix = jnp.dot(pm.astype(BF16), tri_ref[...], preferred_element_type=F32) + run_ref[:, 0:1]
    p3 = prefix.reshape(ng, gsz, tm)
    run_new = run_ref[...] + jnp.sum(pm, axis=1, keepdims=True)
    run_ref[...] = run_new
    cnt_ref[...] = run_new

    sc_rows = []
    for k in range(TOP_K):
        oh = eid == sels[k]
        sc_k = jnp.sum(jnp.sum(jnp.where(oh, s3, 0.0), axis=0, keepdims=True), axis=1, keepdims=True)
        rk_k = jnp.sum(jnp.sum(jnp.where(oh, p3, 0.0), axis=0, keepdims=True), axis=1, keepdims=True)
        sc_rows.append(sc_k)
        eidx_ref[k:k + 1, :] = sels[k].reshape(1, tm).astype(jnp.int32)
        rank_ref[k:k + 1, :] = rk_k.reshape(1, tm).astype(jnp.int32)
    tot = sc_rows[0]
    for k in range(1, TOP_K):
        tot = tot + sc_rows[k]
    inv = ROUTED_SCALE / (tot + 1e-20)
    for k in range(TOP_K):
        gw_ref[k:k + 1, :] = (sc_rows[k] * inv).reshape(1, tm)


def _mix(half, a_n, yt, u2, x2, mod, d_skip, wglu_bf, b_glu, g_out_ssm, wo_a, wo_b, g_post_mix,
         g_pre_ffn, w_router_split, b_router_col, tri):
    tm = TM_MIX
    t0 = half * (HALF_TOK // tm)
    row = lambda n: pl.BlockSpec((1, n), lambda i: (0, 0))
    full = lambda a, b: pl.BlockSpec((a, b), lambda i: (0, 0))
    tok_in = lambda n: pl.BlockSpec((tm, n), lambda i: (t0 + i, 0))
    tok = lambda n: pl.BlockSpec((tm, n), lambda i: (i, 0))
    col = pl.BlockSpec((TOP_K, tm), lambda i: (0, i))
    return pl.pallas_call(
        _mix_kernel,
        grid=(HALF_TOK // tm,),
        in_specs=[tok_in(CONV_CH),
                  pl.BlockSpec((SSM_GROUPS, tm // S5_Q, S5_QH), lambda i: (0, t0 + i, 0)),
                  tok_in(SSM_CH), tok_in(D_MODEL),
                  pl.BlockSpec((1, MOD_ROWS, D_MODEL), lambda i: (half, 0, 0)),
                  row(SSM_CH), full(SSM_CH, SSM_CH), row(SSM_CH), row(SSM_CH),
                  full(CONV_CH, D_MODEL), full(SSM_CH, D_MODEL), row(D_MODEL), row(D_MODEL),
                  full(2 * N_EXPERTS, D_MODEL), full(N_EXPERTS, 1), full(tm, tm)],
        out_specs=[tok(D_MODEL), tok(D_MODEL // 2), col, col, col,
                   pl.BlockSpec((N_EXPERTS, LANES), lambda i: (0, 0))],
        out_shape=[jax.ShapeDtypeStruct((HALF_TOK, D_MODEL), F32),
                   jax.ShapeDtypeStruct((HALF_TOK, D_MODEL // 2), jnp.uint32),
                   jax.ShapeDtypeStruct((TOP_K, HALF_TOK), jnp.int32),
                   jax.ShapeDtypeStruct((TOP_K, HALF_TOK), jnp.int32),
                   jax.ShapeDtypeStruct((TOP_K, HALF_TOK), F32),
                   jax.ShapeDtypeStruct((N_EXPERTS, LANES), F32)],
        scratch_shapes=[pltpu.VMEM((N_EXPERTS, LANES), F32),
                        pltpu.VMEM((SSM_CH // LANES, tm, LANES), F32)],
        compiler_params=_cparams(("arbitrary",)),
        name="mix_out_router",
    )(a_n, yt, u2, x2, mod, d_skip, wglu_bf, b_glu, g_out_ssm, wo_a, wo_b, g_post_mix,
      g_pre_ffn, w_router_split, b_router_col, tri)


def _expert_kernel(blk0_ref, nblk_ref, bunit_ref, bfull_ref, xs_hbm, wgu_ref, wd_ref, ys_hbm,
                   xbuf, ybuf, sem_in, sem_out):
    e = pl.program_id(0)
    n = nblk_ref[e]
    b0 = blk0_ref[e]
    n_all = blk0_ref[N_EXPERTS - 1] + nblk_ref[N_EXPERTS - 1]

    def copies(b, slot, rows):
        hbm = pl.ds(pl.multiple_of(bunit_ref[b] * ROW_UNIT, ROW_UNIT), rows)
        buf = pl.ds(0, rows)
        return (pltpu.make_async_copy(xs_hbm.at[hbm], xbuf.at[slot, buf], sem_in.at[slot]),
                pltpu.make_async_copy(ybuf.at[slot, buf], ys_hbm.at[hbm], sem_out.at[slot]))

    def for_size(b, fn):
        full = bfull_ref[b] == 1

        @pl.when(full)
        def _():
            fn(ROW_BLOCK)

        @pl.when(jnp.logical_not(full))
        def _():
            fn(ROW_UNIT)

    for b in range(EXPERT_AHEAD):
        @pl.when((e == 0) & (b < n_all))
        def _():
            for_size(b, lambda rows: copies(b, b, rows)[0].start())

    def compute(slot, rows):
        x_lo, x_hi = _unpack_halves(xbuf[slot, pl.ds(0, rows)])
        x = jnp.concatenate([x_lo.astype(BF16), x_hi.astype(BF16)], axis=1)
        h = jnp.dot(x, wgu_ref[0], preferred_element_type=F32)
        hg = h[:, :D_EXPERT]
        act = hg * jax.nn.sigmoid(hg) * h[:, D_EXPERT:]
        ybuf[slot, pl.ds(0, rows)] = _pack_halves(
            jnp.dot(act.astype(BF16), wd_ref[0], preferred_element_type=F32))

    def block(b, carry):
        slot = b % EXPERT_SLOTS
        ahead = b + EXPERT_AHEAD

        @pl.when(ahead < n_all)
        def _():
            for_size(ahead, lambda rows: copies(ahead, ahead % EXPERT_SLOTS, rows)[0].start())

        @pl.when(b >= EXPERT_SLOTS)
        def _():
            for_size(b - EXPERT_SLOTS, lambda rows: copies(b - EXPERT_SLOTS, slot, rows)[1].wait())

        def work(rows):
            cp_in, cp_out = copies(b, slot, rows)
            cp_in.wait()
            compute(slot, rows)
            cp_out.start()

        for_size(b, work)
        return carry

    lax.fori_loop(b0, b0 + n, block, 0)

    @pl.when(e == N_EXPERTS - 1)
    def _():
        for j in range(1, EXPERT_SLOTS + 1):
            @pl.when(n_all >= j)
            def _():
                last = n_all - j
                for_size(last, lambda rows: copies(last, last % EXPERT_SLOTS, rows)[1].wait())


def _experts(blk0, nblk, bunit, bfull, xs, we_gu, we_d):
    any_spec = pl.BlockSpec(memory_space=pl.ANY)
    grid_spec = pltpu.PrefetchScalarGridSpec(
        num_scalar_prefetch=4,
        grid=(N_EXPERTS,),
        in_specs=[any_spec,
                  pl.BlockSpec((1, D_MODEL, 2 * D_EXPERT), lambda e, *_: (e, 0, 0)),
                  pl.BlockSpec((1, D_EXPERT, D_MODEL), lambda e, *_: (e, 0, 0))],
        out_specs=any_spec,
        scratch_shapes=[pltpu.VMEM((EXPERT_SLOTS, ROW_BLOCK, D_MODEL // 2), jnp.uint32),
                        pltpu.VMEM((EXPERT_SLOTS, ROW_BLOCK, D_MODEL // 2), jnp.uint32),
                        pltpu.SemaphoreType.DMA((EXPERT_SLOTS,)),
                        pltpu.SemaphoreType.DMA((EXPERT_SLOTS,))],
    )
    return pl.pallas_call(
        _expert_kernel,
        grid_spec=grid_spec,
        out_shape=jax.ShapeDtypeStruct((N_ROWS, D_MODEL // 2), jnp.uint32),
        compiler_params=_cparams(("arbitrary",)),
        name="routed_experts",
    )(blk0, nblk, bunit, bfull, xs, we_gu, we_d)


def _final_kernel(h2_ref, yg_ref, gw_ref, x1_ref, mod_ref, wgu_ref, wd_ref, g_ref, *rest):
    o_ref = rest[-1]
    half = D_MODEL // 2
    gt_f = mod_ref[0, 5:6, :]
    x_lo, x_hi = _unpack_halves(h2_ref[...])
    h = (jnp.dot(x_lo.astype(BF16), wgu_ref[:half, :], preferred_element_type=F32)
         + jnp.dot(x_hi.astype(BF16), wgu_ref[half:, :], preferred_element_type=F32))
    hg = h[:, :D_EXPERT]
    act = hg * jax.nn.sigmoid(hg) * h[:, D_EXPERT:]
    shared = jnp.dot(act.astype(BF16), wd_ref[...], preferred_element_type=F32)
    y_lo = shared[:, :half]
    y_hi = shared[:, half:]
    for k in range(TOP_K):
        r_lo, r_hi = _unpack_halves(yg_ref[k])
        w = gw_ref[:, k:k + 1]
        y_lo = y_lo + w * r_lo
        y_hi = y_hi + w * r_hi
    ms = (jnp.sum(y_lo * y_lo, axis=-1, keepdims=True)
          + jnp.sum(y_hi * y_hi, axis=-1, keepdims=True)) * (1.0 / D_MODEL)
    inv = lax.rsqrt(ms + NORM_EPS)
    o_ref[:, :half] = x1_ref[:, :half] + gt_f[:, :half] * (y_lo * inv * g_ref[:, :half])
    o_ref[:, half:] = x1_ref[:, half:] + gt_f[:, half:] * (y_hi * inv * g_ref[:, half:])


def _final(half, out_prev, h2p, yg, gw_t, x1, mod, ws_gu, ws_d, g_post_ffn):
    tm = TM_OUT
    t0 = half * (HALF_TOK // tm)
    tok = pl.BlockSpec((tm, D_MODEL), lambda i: (i, 0))
    in_specs = [pl.BlockSpec((tm, D_MODEL // 2), lambda i: (i, 0)),
                pl.BlockSpec((TOP_K, tm, D_MODEL // 2), lambda i: (0, i, 0)),
                pl.BlockSpec((tm, TOP_K), lambda i: (i, 0)),
                tok,
                pl.BlockSpec((1, MOD_ROWS, D_MODEL), lambda i: (half, 0, 0)),
                pl.BlockSpec((D_MODEL, 2 * D_EXPERT), lambda i: (0, 0)),
                pl.BlockSpec((D_EXPERT, D_MODEL), lambda i: (0, 0)),
                pl.BlockSpec((1, D_MODEL), lambda i: (0, 0))]
    args = [h2p, yg, gw_t, x1, mod, ws_gu, ws_d, g_post_ffn]
    aliases = {}
    if out_prev is not None:
        aliases = {len(args): 0}
        in_specs.append(pl.BlockSpec(memory_space=pl.ANY))
        args.append(out_prev)
    return pl.pallas_call(
        _final_kernel,
        grid=(HALF_TOK // tm,),
        in_specs=in_specs,
        out_specs=pl.BlockSpec((tm, D_MODEL), lambda i: (t0 + i, 0)),
        out_shape=jax.ShapeDtypeStruct((N_TOK, D_MODEL), F32),
        input_output_aliases=aliases,
        compiler_params=_cparams(("parallel",)),
        name="shared_final",
    )(*args)


def _sc_worker_id():
    return lax.axis_index("s") * SC_CORES + lax.axis_index("c")


def _dispatch_body(h_hbm, dest_hbm, xs_hbm, idx_v, rows_v, sem_l, sem_s):
    n = SC_CHUNKS_PER_WORKER
    c0 = _sc_worker_id() * n

    def load(i, b):
        return pltpu.async_copy(h_hbm.at[pl.ds((c0 + i) * SC_W, SC_W)], rows_v.at[b], sem_l.at[b])

    loads = [None] * n
    scat = [None] * n
    loads[0] = load(0, 0)
    for i in range(n):
        b = i % 2
        pltpu.sync_copy(dest_hbm.at[c0 + i], idx_v.at[b])
        loads[i].wait()
        if i + 1 < n:
            if i >= 1:
                for d in scat[i - 1]:
                    d.wait()
            loads[i + 1] = load(i + 1, 1 - b)
        scat[i] = [pltpu.async_copy(rows_v.at[b], xs_hbm.at[idx_v.at[b].at[k]], sem_s.at[b])
                   for k in range(TOP_K)]
    for i in (n - 2, n - 1):
        for d in scat[i]:
            d.wait()


def _sc_dispatch(h2p, dest3):
    mesh = plsc.VectorSubcoreMesh(core_axis_name="c", subcore_axis_name="s")
    return pl.kernel(
        _dispatch_body, mesh=mesh,
        out_type=jax.ShapeDtypeStruct((N_ROWS, D_MODEL // 2), jnp.uint32),
        scratch_types=[pltpu.VMEM((2, TOP_K, SC_W), jnp.int32),
                       pltpu.VMEM((2, SC_W, D_MODEL // 2), jnp.uint32),
                       pltpu.SemaphoreType.DMA((2,)), pltpu.SemaphoreType.DMA((2,))],
    )(h2p, dest3)


def _combine_body(ys_hbm, dest_hbm, yg_hbm, idx_v, rows_v, sem_g, sem_w):
    c0 = _sc_worker_id() * SC_CHUNKS_PER_WORKER

    @pl.loop(0, SC_CHUNKS_PER_WORKER)
    def _(i):
        c = c0 + i
        pltpu.sync_copy(dest_hbm.at[c], idx_v)
        g = [None] * TOP_K
        w = [None] * TOP_K
        g[0] = pltpu.async_copy(ys_hbm.at[idx_v.at[0]], rows_v.at[0], sem_g.at[0])
        for k in range(TOP_K):
            b = k % 2
            g[k].wait()
            if k + 1 < TOP_K:
                if k >= 1:
                    w[k - 1].wait()
                g[k + 1] = pltpu.async_copy(ys_hbm.at[idx_v.at[k + 1]], rows_v.at[1 - b], sem_g.at[1 - b])
            w[k] = pltpu.async_copy(rows_v.at[b], yg_hbm.at[k].at[pl.ds(c * SC_W, SC_W)], sem_w.at[b])
        w[TOP_K - 2].wait()
        w[TOP_K - 1].wait()


def _sc_combine(ysp, dest3):
    mesh = plsc.VectorSubcoreMesh(core_axis_name="c", subcore_axis_name="s")
    return pl.kernel(
        _combine_body, mesh=mesh,
        out_type=jax.ShapeDtypeStruct((TOP_K, HALF_TOK, D_MODEL // 2), jnp.uint32),
        scratch_types=[pltpu.VMEM((TOP_K, SC_W), jnp.int32),
                       pltpu.VMEM((2, SC_W, D_MODEL // 2), jnp.uint32),
                       pltpu.SemaphoreType.DMA((2,)), pltpu.SemaphoreType.DMA((2,))],
    )(ysp, dest3)


def kernel(x, c, w_ada, b_ada, g_pre_mix, g_post_mix, w_in, conv_w, conv_b, conv_ln_g, conv_ln_b,
           ssm_a_re, ssm_a_im, ssm_log_dt, ssm_b_re, ssm_b_im, ssm_c_re, ssm_c_im, ssm_d,
           ssm_w_glu, ssm_b_glu, g_out_conv, g_out_ssm, w_out, g_pre_ffn, g_post_ffn,
           w_router, b_router, we_gate, we_up, we_down, ws_gate, ws_up, ws_down):
    l = 0
    x2 = x.reshape(N_TOK, D_MODEL)
    r1 = lambda a: a.reshape(1, -1)

    c_pad = jnp.zeros((SUBLANES, D_MODEL), F32).at[:BATCH].set(c)
    mod = _ada(c_pad, w_ada[l], r1(b_ada[l]))[:BATCH].reshape(BATCH, N_MOD, D_MODEL)
    mod = jnp.concatenate([mod, jnp.zeros((BATCH, MOD_ROWS - N_MOD, D_MODEL), F32)], axis=1)

    v, u, ut = _inproj(x2, mod, r1(g_pre_mix[l]), w_in[l].astype(BF16))
    cw = jnp.concatenate([conv_w[l].reshape(CONV_WIDTH, CONV_CH), jnp.zeros((1, CONV_CH), F32)], axis=0)
    a_n, we_gu, we_d = _conv(v.reshape(BATCH, SEQ, CONV_CH), cw, r1(conv_b[l]), r1(conv_ln_g[l]),
                             r1(conv_ln_b[l]), r1(g_out_conv[l]), we_gate[l], we_up[l], we_down[l])
    a_n = a_n.reshape(N_TOK, CONV_CH)

    pwr, pwi, s5_params, a_cat, b_q, b_s = _s5_operators(
        ssm_a_re[l], ssm_a_im[l], ssm_log_dt[l], ssm_b_re[l], ssm_b_im[l], ssm_c_re[l], ssm_c_im[l])
    yt = _s5(ut, pwr, pwi, s5_params, a_cat, b_q, b_s)

    tm = TM_MIX
    tri = (jnp.arange(tm)[:, None] < jnp.arange(tm)[None, :]).astype(BF16)
    wo = w_out[l].astype(BF16)
    wr_t = w_router[l].T
    wr_hi = wr_t.astype(BF16)
    wr_split = jnp.concatenate([wr_hi, (wr_t - wr_hi.astype(F32)).astype(BF16)], axis=0)
    mix_params = (r1(ssm_d[l]), ssm_w_glu[l].astype(BF16), r1(ssm_b_glu[l]), r1(g_out_ssm[l]),
                  wo[:CONV_CH], wo[CONV_CH:], r1(g_post_mix[l]), r1(g_pre_ffn[l]),
                  wr_split, b_router[l].reshape(N_EXPERTS, 1), tri)
    ws_gu = jnp.concatenate([ws_gate[l], ws_up[l]], axis=1).astype(BF16)
    ws_d = ws_down[l].astype(BF16)
    e_ids = jnp.arange(N_EXPERTS, dtype=jnp.int32)

    out = None
    for half in range(N_HALVES):
        x1, h2, eidx, rank, gw, cnt = _mix(half, a_n, yt, u, x2, mod, *mix_params)
        counts = cnt[:, 0].astype(jnp.int32)
        units = (counts + ROW_UNIT - 1) // ROW_UNIT
        unit0 = jnp.cumsum(units) - units
        nblk = (units + 1) // 2
        blk0 = jnp.cumsum(nblk) - nblk
        dest = rank + jnp.sum(jnp.where(eidx[..., None] == e_ids, unit0 * ROW_UNIT, 0), axis=-1)
        dest3 = dest.reshape(TOP_K, HALF_TOK // SC_W, SC_W).transpose(1, 0, 2)
        b_ids = jnp.arange(N_BLOCKS, dtype=jnp.int32)
        owner = (b_ids[:, None] >= blk0[None, :]) & (b_ids[:, None] < (blk0 + nblk)[None, :])
        pick = lambda v: jnp.sum(jnp.where(owner, v[None, :], 0), axis=1)
        local = b_ids - pick(blk0)
        bunit = pick(unit0) + 2 * local
        bfull = (2 * local + 2 <= pick(units)).astype(jnp.int32)

        xs = _sc_dispatch(h2, dest3)
        ys = _experts(blk0, nblk, bunit, bfull, xs, we_gu, we_d)
        yg = _sc_combine(ys, dest3)
        out = _final(half, out, h2, yg, gw.T, x1, mod, ws_gu, ws_d, r1(g_post_ffn[l]))
    return out.reshape(BATCH, SEQ, D_MODEL)
```

```python
import math

import jax
import jax.numpy as jnp
from jax import lax
from jax.experimental import pallas as pl
from jax.experimental.pallas import tpu as pltpu
from jax.experimental.pallas import tpu_sc as plsc

F32 = jnp.float32
BF16 = jnp.bfloat16

D_MODEL = 1024
BATCH = 2
SEQ = 8192
N_TOK = BATCH * SEQ
CONV_CH = 512
CONV_WIDTH = 31
SSM_CH = 512
SSM_GROUP_CH = 16
SSM_GROUPS = 32
SSM_STATE = 64
D_IN = 2 * CONV_CH + SSM_CH
N_EXPERTS = 64
TOP_K = 8
N_ROUTE_GROUPS = 8
TOPK_ROUTE_GROUPS = 4
D_EXPERT = 256
ROUTED_SCALE = 2.5
NORM_EPS = 1e-6

SUBLANES = 8
LANES = 128

N_MOD = 6
MOD_ROWS = SUBLANES
ADA_COLS = 1536
TM_IN = 2048
IN_SUBTILES = 8
TL_CONV = 512
CONV_HALO = 32
CONV_ROWS = 256
EXPERTS_PER_CONV_STEP = N_EXPERTS * TL_CONV // N_TOK
assert EXPERTS_PER_CONV_STEP * N_TOK == N_EXPERTS * TL_CONV
S5_Q = 32
S5_GROUPS_PER_STEP = 8
S5_QH = S5_Q * SSM_GROUP_CH
S5_CHUNKS = N_TOK // S5_Q
S5_CHUNKS_PER_SEQ = SEQ // S5_Q
TM_MIX = 1024
ROW_BLOCK = 512
ROW_UNIT = ROW_BLOCK // 2
EXPERT_AHEAD = 4
EXPERT_SLOTS = EXPERT_AHEAD + 1
HALF_TOK = SEQ
N_HALVES = N_TOK // HALF_TOK
N_UNITS = HALF_TOK * TOP_K // ROW_UNIT + N_EXPERTS
N_BLOCKS = (N_UNITS + N_EXPERTS) // 2
N_ROWS = N_UNITS * ROW_UNIT
TM_OUT = 512
SC_CORES = 2
SC_SUBCORES = 16
SC_WORKERS = SC_CORES * SC_SUBCORES
SC_W = 64
SC_CHUNKS_PER_WORKER = HALF_TOK // (SC_WORKERS * SC_W)
VMEM_LIMIT = 56 * 1024 * 1024


def _cparams(sem):
    return pltpu.CompilerParams(dimension_semantics=sem, vmem_limit_bytes=VMEM_LIMIT)


def _pack_rounded_halves(xr):
    n = xr.shape[-1] // 2
    lo = lax.bitcast_convert_type(xr[:, :n], jnp.uint32)
    hi = lax.bitcast_convert_type(xr[:, n:], jnp.uint32)
    return hi | (lo >> 16)


def _pack_halves(x):
    return _pack_rounded_halves(x.astype(BF16).astype(F32))


def _unpack_halves(p):
    lo = lax.bitcast_convert_type(p << 16, F32)
    hi = lax.bitcast_convert_type(p & jnp.uint32(0xFFFF0000), F32)
    return lo, hi


def _rms(x, g):
    return x * lax.rsqrt(jnp.mean(x * x, axis=-1, keepdims=True) + NORM_EPS) * g


def _split_bf16(x):
    hi = x.astype(BF16)
    return hi, (x - hi.astype(F32)).astype(BF16)


def _dot_nt_split(a, b):
    nt = (((1,), (1,)), ((), ()))
    a_hi, a_lo = _split_bf16(a)
    b_hi, b_lo = _split_bf16(b)
    m = a.shape[0]
    both = lax.dot_general(jnp.concatenate([a_hi, a_lo], axis=0), b_hi, nt, preferred_element_type=F32)
    return both[:m] + both[m:] + lax.dot_general(a_hi, b_lo, nt, preferred_element_type=F32)


def _ada_kernel(c_ref, w_ref, b_ref, o_ref):
    c = c_ref[...]
    a = c * jax.nn.sigmoid(c)
    o_ref[...] = jnp.dot(a, w_ref[...], preferred_element_type=F32,
                         precision=lax.Precision.HIGHEST) + b_ref[...]


def _ada(c_pad, w_ada, b_ada):
    n = w_ada.shape[1]
    bn = ADA_COLS
    return pl.pallas_call(
        _ada_kernel,
        grid=(n // bn,),
        in_specs=[pl.BlockSpec((SUBLANES, D_MODEL), lambda j: (0, 0)),
                  pl.BlockSpec((D_MODEL, bn), lambda j: (0, j)),
                  pl.BlockSpec((1, bn), lambda j: (0, j))],
        out_specs=pl.BlockSpec((SUBLANES, bn), lambda j: (0, j)),
        out_shape=jax.ShapeDtypeStruct((SUBLANES, n), F32),
        compiler_params=_cparams(("arbitrary",)),
        name="ada_mod",
    )(c_pad, w_ada, b_ada)


GROUPS_PER_LANE_TILE = LANES // SSM_GROUP_CH


def _to_group_chunks(u, tile_ref, ut_ref):
    n_chunks = u.shape[0] // S5_Q
    for j in range(SSM_CH // LANES):
        tile_ref[j] = u[:, LANES * j:LANES * (j + 1)]
    for j in range(SSM_CH // LANES):
        rows_t = [tile_ref[j, pl.ds(t, n_chunks, stride=S5_Q), :] for t in range(S5_Q)]
        for gg in range(GROUPS_PER_LANE_TILE):
            lo = gg * SSM_GROUP_CH
            row = jnp.concatenate([r[:, lo:lo + SSM_GROUP_CH] for r in rows_t], axis=1)
            ut_ref[j * GROUPS_PER_LANE_TILE + gg] = row.astype(ut_ref.dtype)


def _from_group_chunks(yt_ref, tile_ref):
    n_chunks = yt_ref.shape[1]
    for j in range(SSM_CH // LANES):
        for t in range(S5_Q):
            lo = t * SSM_GROUP_CH
            piece = jnp.concatenate(
                [yt_ref[j * GROUPS_PER_LANE_TILE + gg, :, lo:lo + SSM_GROUP_CH]
                 for gg in range(GROUPS_PER_LANE_TILE)], axis=1)
            tile_ref[j, pl.ds(t, n_chunks, stride=S5_Q), :] = piece
    return jnp.concatenate([tile_ref[j] for j in range(SSM_CH // LANES)], axis=1)


def _inproj_kernel(x_ref, mod_ref, g_ref, w_ref, v_ref, u_ref, ut_ref, tile_ref):
    sh = mod_ref[0, 0:1, :]
    sc = mod_ref[0, 1:2, :]
    sub = TM_IN // IN_SUBTILES
    sub_chunks = sub // S5_Q
    for s in range(IN_SUBTILES):
        r = slice(s * sub, (s + 1) * sub)
        h = _rms(x_ref[r, :], g_ref[...]) * (1.0 + sc) + sh
        z = jnp.dot(h.astype(BF16), w_ref[...], preferred_element_type=F32)
        v_ref[r, :] = z[:, :CONV_CH] * jax.nn.sigmoid(z[:, CONV_CH:2 * CONV_CH])
        u = z[:, 2 * CONV_CH:]
        u_ref[r, :] = u
        _to_group_chunks(u, tile_ref.at[s], ut_ref.at[:, s * sub_chunks:(s + 1) * sub_chunks, :])


def _inproj(x2, mod, g_pre, w_in_bf):
    tiles_per_seq = SEQ // TM_IN
    return pl.pallas_call(
        _inproj_kernel,
        grid=(N_TOK // TM_IN,),
        in_specs=[pl.BlockSpec((TM_IN, D_MODEL), lambda i: (i, 0)),
                  pl.BlockSpec((1, MOD_ROWS, D_MODEL), lambda i: (i // tiles_per_seq, 0, 0)),
                  pl.BlockSpec((1, D_MODEL), lambda i: (0, 0)),
                  pl.BlockSpec((D_MODEL, D_IN), lambda i: (0, 0))],
        out_specs=[pl.BlockSpec((TM_IN, CONV_CH), lambda i: (i, 0)),
                   pl.BlockSpec((TM_IN, SSM_CH), lambda i: (i, 0)),
                   pl.BlockSpec((SSM_GROUPS, TM_IN // S5_Q, S5_QH), lambda i: (0, i, 0))],
        out_shape=[jax.ShapeDtypeStruct((N_TOK, CONV_CH), F32),
                   jax.ShapeDtypeStruct((N_TOK, SSM_CH), F32),
                   jax.ShapeDtypeStruct((SSM_GROUPS, S5_CHUNKS, S5_QH), BF16)],
        scratch_shapes=[pltpu.VMEM((IN_SUBTILES, SSM_CH // LANES, TM_IN // IN_SUBTILES, LANES), F32)],
        compiler_params=_cparams(("parallel",)),
        name="in_proj",
    )(x2, mod, g_pre, w_in_bf)


def _conv_kernel(vc_ref, vp_ref, w_ref, cb_ref, lg_ref, lb_ref, go_ref, wg_ref, wu_ref, wd_ref,
                 o_ref, wgu_o, wd_o, sh_ref):
    for q in range(EXPERTS_PER_CONV_STEP):
        wgu_o[q, :, :D_EXPERT] = wg_ref[q].astype(BF16)
        wgu_o[q, :, D_EXPERT:] = wu_ref[q].astype(BF16)
        wd_o[q] = wd_ref[q].astype(BF16)

    i = pl.program_id(1)
    keep = (i > 0).astype(F32)
    n_ext = TL_CONV + CONV_HALO
    sh_ref[0, 0:CONV_HALO, :] = vp_ref[0] * keep
    sh_ref[0, CONV_HALO:, :] = vc_ref[0]
    ext = sh_ref[0]
    for s in range(1, SUBLANES):
        sh_ref[s] = pltpu.roll(ext, n_ext - s, axis=0)
    off = CONV_HALO - (CONV_WIDTH - 1)
    for r in range(TL_CONV // CONV_ROWS):
        acc = None
        for j in range(CONV_WIDTH):
            s = (off + j) % SUBLANES
            al = r * CONV_ROWS + (off + j) - s
            term = w_ref[j:j + 1, :] * sh_ref[s, al:al + CONV_ROWS, :]
            acc = term if acc is None else acc + term
        y = acc + cb_ref[...]
        mu = jnp.mean(y, axis=-1, keepdims=True)
        d = y - mu
        var = jnp.mean(d * d, axis=-1, keepdims=True)
        yn = d * lax.rsqrt(var + NORM_EPS) * lg_ref[...] + lb_ref[...]
        a = yn * jax.nn.sigmoid(yn)
        o_ref[0, r * CONV_ROWS:(r + 1) * CONV_ROWS, :] = _rms(a, go_ref[...]).astype(BF16)


def _conv(v3, conv_w, conv_b, ln_g, ln_b, g_out, we_gate, we_up, we_down):
    halo_per_tile = TL_CONV // CONV_HALO
    steps_per_seq = SEQ // TL_CONV
    vec = pl.BlockSpec((1, CONV_CH), lambda b, i: (0, 0))
    ex = EXPERTS_PER_CONV_STEP
    w_in = pl.BlockSpec((ex, D_MODEL, D_EXPERT), lambda b, i: (b * steps_per_seq + i, 0, 0))
    return pl.pallas_call(
        _conv_kernel,
        grid=(BATCH, steps_per_seq),
        in_specs=[pl.BlockSpec((1, TL_CONV, CONV_CH), lambda b, i: (b, i, 0)),
                  pl.BlockSpec((1, CONV_HALO, CONV_CH),
                               lambda b, i: (b, jnp.maximum(i * halo_per_tile - 1, 0), 0)),
                  pl.BlockSpec((CONV_WIDTH + 1, CONV_CH), lambda b, i: (0, 0)),
                  vec, vec, vec, vec,
                  w_in, w_in,
                  pl.BlockSpec((ex, D_EXPERT, D_MODEL), lambda b, i: (b * steps_per_seq + i, 0, 0))],
        out_specs=[pl.BlockSpec((1, TL_CONV, CONV_CH), lambda b, i: (b, i, 0)),
                   pl.BlockSpec((ex, D_MODEL, 2 * D_EXPERT), lambda b, i: (b * steps_per_seq + i, 0, 0)),
                   pl.BlockSpec((ex, D_EXPERT, D_MODEL), lambda b, i: (b * steps_per_seq + i, 0, 0))],
        out_shape=[jax.ShapeDtypeStruct((BATCH, SEQ, CONV_CH), BF16),
                   jax.ShapeDtypeStruct((N_EXPERTS, D_MODEL, 2 * D_EXPERT), BF16),
                   jax.ShapeDtypeStruct((N_EXPERTS, D_EXPERT, D_MODEL), BF16)],
        scratch_shapes=[pltpu.VMEM((SUBLANES, TL_CONV + CONV_HALO, CONV_CH), F32)],
        compiler_params=_cparams(("parallel", "arbitrary")),
        name="conv_module",
    )(v3, v3, conv_w, conv_b, ln_g, ln_b, g_out, we_gate, we_up, we_down)


S5_GROUP_ROWS = S5_CHUNKS + SUBLANES


S5_POW_ROWS = (S5_Q + 1 + SUBLANES - 1) // SUBLANES * SUBLANES
(S5_BB_RI, S5_BB_NIR, S5_BB_IR, S5_BB_RNI, S5_CC_RI, S5_CC_NIR, S5_N_PARAM) = range(7)


def _s5_kernel(ut_ref, pwr_ref, pwi_ref, par_ref, a_ref, bq_ref, bs_ref, yt_ref, sin_s, sp_s):
    phase = pl.program_id(0)
    g = pl.program_id(1)
    q = S5_Q
    n = 2 * SSM_STATE

    def group_row0(k):
        return pl.multiple_of((g * S5_GROUPS_PER_STEP + k) * S5_GROUP_ROWS, SUBLANES)

    def lam_pow(k, j):
        return pwr_ref[k, j:j + 1, :], pwi_ref[k, j:j + 1, :]

    @pl.when(phase == 0)
    def _():
        for k in range(S5_GROUPS_PER_STEP):
            bb_ri, bb_nir = par_ref[k, S5_BB_RI], par_ref[k, S5_BB_NIR]
            bb_ir, bb_rni = par_ref[k, S5_BB_IR], par_ref[k, S5_BB_RNI]
            blk_q, blk_s = [], []
            for t in range(q):
                pr, pi_ = lam_pow(k, q - 1 - t)
                blk_q.append(pr * bb_ri + pi_ * bb_nir)
                blk_s.append(pr * bb_ir + pi_ * bb_rni)
            wst = jnp.concatenate([jnp.concatenate(blk_q, axis=0), jnp.concatenate(blk_s, axis=0)], axis=1)
            r = jnp.dot(ut_ref[k], wst.astype(BF16), preferred_element_type=F32)
            sin_s[0, pl.ds(group_row0(k), S5_CHUNKS), :] = r[:, :n]
            sin_s[1, pl.ds(group_row0(k), S5_CHUNKS), :] = r[:, n:]

    @pl.when((phase == 1) & (g == 0))
    def _():
        a = a_ref[...]
        bq = bq_ref[...]
        bs = bs_ref[...]

        def body(c, carry):
            nxt = []
            for b in range(BATCH):
                x, xs = carry[b]
                rows = pl.ds(b * S5_CHUNKS_PER_SEQ + c, SSM_GROUPS, stride=S5_GROUP_ROWS)
                sp_s[rows, :] = x
                nxt.append((a * x + bq * xs + sin_s[0, rows, :], a * xs + bs * x + sin_s[1, rows, :]))
            return tuple(nxt)

        z = jnp.zeros((SSM_GROUPS, n), F32)
        lax.fori_loop(0, S5_CHUNKS_PER_SEQ, body, tuple((z, z) for _ in range(BATCH)))

    @pl.when(phase == 1)
    def _():
        for k in range(S5_GROUPS_PER_STEP):
            cc_ri, cc_nir = par_ref[k, S5_CC_RI], par_ref[k, S5_CC_NIR]
            cl = []
            for j in range(q + 1):
                pr, pi_ = lam_pow(k, j)
                cl.append(pr * cc_ri + pi_ * cc_nir)
            cl_lo = jnp.concatenate(cl[:q], axis=0)
            cl_hi = jnp.concatenate(cl[1:], axis=0)
            lane = lax.broadcasted_iota(jnp.int32, (1, n), 1)
            vgt = (cl_hi * jnp.where(lane < SSM_STATE, 1.0, -1.0)).astype(BF16)
            kt = _dot_nt_split(par_ref[k, S5_BB_RNI], cl_lo)
            padded = jnp.concatenate([jnp.zeros_like(kt), kt], axis=1)
            tg = jnp.concatenate(
                [padded[:, (q - t) * SSM_GROUP_CH:(q - t) * SSM_GROUP_CH + S5_QH] for t in range(q)],
                axis=0).astype(BF16)
            sp = sp_s[pl.ds(group_row0(k), S5_CHUNKS), :]
            y = jnp.dot(ut_ref[k], tg, preferred_element_type=F32)
            yt_ref[k] = y + lax.dot_general(sp.astype(BF16), vgt, (((1,), (1,)), ((), ())),
                                            preferred_element_type=F32)


def _s5(ut, pwr, pwi, params, a_cat, b_q, b_s):
    vec = pl.BlockSpec((SSM_GROUPS, 2 * SSM_STATE), lambda p, g: (0, 0))
    gs = S5_GROUPS_PER_STEP
    powers = pl.BlockSpec((gs, S5_POW_ROWS, 2 * SSM_STATE), lambda p, g: (g, 0, 0))
    return pl.pallas_call(
        _s5_kernel,
        grid=(2, SSM_GROUPS // gs),
        in_specs=[pl.BlockSpec((gs, S5_CHUNKS, S5_QH), lambda p, g: (g, 0, 0)),
                  powers, powers,
                  pl.BlockSpec((gs, S5_N_PARAM, SSM_GROUP_CH, 2 * SSM_STATE), lambda p, g: (g, 0, 0, 0)),
                  vec, vec, vec],
        out_specs=pl.BlockSpec((gs, S5_CHUNKS, S5_QH), lambda p, g: (g * p, 0, 0)),
        out_shape=jax.ShapeDtypeStruct((SSM_GROUPS, S5_CHUNKS, S5_QH), F32),
        scratch_shapes=[pltpu.VMEM((2, SSM_GROUPS * S5_GROUP_ROWS, 2 * SSM_STATE), F32),
                        pltpu.VMEM((SSM_GROUPS * S5_GROUP_ROWS, 2 * SSM_STATE), F32)],
        compiler_params=_cparams(("arbitrary", "arbitrary")),
        name="s5_chunked",
    )(ut, pwr, pwi, params, a_cat, b_q, b_s)


def _s5_operators(a_re, a_im, log_dt, b_re, b_im, c_re, c_im):
    q = S5_Q
    dt = jnp.exp(log_dt)[:, None]
    ar, ai = a_re, a_im
    mag = jnp.exp(ar * dt)
    lr = mag * jnp.cos(ai * dt)
    li = mag * jnp.sin(ai * dt)
    den = ar * ar + ai * ai
    nr = lr - 1.0
    kr = (nr * ar + li * ai) / den
    ki = (li * ar - nr * ai) / den
    bbr = kr[..., None] * b_re - ki[..., None] * b_im
    bbi = kr[..., None] * b_im + ki[..., None] * b_re
    j = jnp.arange(q + 1, dtype=F32)[None, :, None]
    pmag = jnp.exp(ar[:, None, :] * dt[:, :, None] * j)
    pang = ai[:, None, :] * dt[:, :, None] * j
    pr = pmag * jnp.cos(pang)
    pi_ = pmag * jnp.sin(pang)
    pad = ((0, 0), (0, S5_POW_ROWS - (q + 1)), (0, 0))
    pwr = jnp.pad(jnp.concatenate([pr, pr], axis=-1), pad)
    pwi = jnp.pad(jnp.concatenate([pi_, pi_], axis=-1), pad)
    br_t = bbr.transpose(0, 2, 1)
    bi_t = bbi.transpose(0, 2, 1)
    cat = lambda a, b: jnp.concatenate([a, b], axis=-1)
    stack = [None] * S5_N_PARAM
    stack[S5_BB_RI] = cat(br_t, bi_t)
    stack[S5_BB_NIR] = cat(-bi_t, br_t)
    stack[S5_BB_IR] = cat(bi_t, br_t)
    stack[S5_BB_RNI] = cat(br_t, -bi_t)
    stack[S5_CC_RI] = cat(c_re, c_im)
    stack[S5_CC_NIR] = cat(-c_im, c_re)
    params = jnp.stack(stack, axis=1)
    aq_r, aq_i = pr[:, q], pi_[:, q]
    a_cat = cat(aq_r, aq_r)
    b_q = cat(-aq_i, aq_i)
    b_s = cat(aq_i, -aq_i)
    return pwr, pwi, params, a_cat, b_q, b_s


def _gelu_tanh(x):
    return 0.5 * x * (1.0 + jnp.tanh(math.sqrt(2.0 / math.pi) * (x + 0.044715 * (x * x * x))))


def _mix_kernel(an_ref, yt_ref, u_ref, x_ref, mod_ref, d_ref, wglu_ref, bglu_ref, gos_ref,
                woa_ref, wob_ref, gpm_ref, gpf_ref, wr_ref, br_ref, tri_ref,
                x1_ref, h2_ref, eidx_ref, rank_ref, gw_ref, cnt_ref, run_ref, tile_ref):
    i = pl.program_id(0)
    tm = TM_MIX

    @pl.when(i == 0)
    def _():
        run_ref[...] = jnp.zeros_like(run_ref)

    gt_m = mod_ref[0, 2:3, :]
    sh_f = mod_ref[0, 3:4, :]
    sc_f = mod_ref[0, 4:5, :]

    yy = _from_group_chunks(yt_ref, tile_ref) + d_ref[...] * u_ref[...]
    g = _gelu_tanh(yy)
    gl = jnp.dot(g.astype(BF16), wglu_ref[...], preferred_element_type=F32) + bglu_ref[...]
    ob = g * jax.nn.sigmoid(gl)
    bn = _rms(ob, gos_ref[...]).astype(BF16)
    o = (jnp.dot(an_ref[...], woa_ref[...], preferred_element_type=F32)
         + jnp.dot(bn, wob_ref[...], preferred_element_type=F32))
    x1 = x_ref[...] + gt_m * _rms(o, gpm_ref[...])
    x1_ref[...] = x1
    h2 = _rms(x1, gpf_ref[...]) * (1.0 + sc_f) + sh_f
    h2_hi = h2.astype(BF16)
    h2_hi32 = h2_hi.astype(F32)
    h2_ref[...] = _pack_rounded_halves(h2_hi32)

    h2_lo = (h2 - h2_hi32).astype(BF16)
    nt = (((1,), (1,)), ((), ()))
    both = lax.dot_general(wr_ref[...], h2_hi, nt, preferred_element_type=F32)
    logits = (both[:N_EXPERTS] + both[N_EXPERTS:]
              + lax.dot_general(wr_ref[:N_EXPERTS, :], h2_lo, nt, preferred_element_type=F32))
    scores = jax.nn.sigmoid(logits)
    biased = scores + br_ref[...]
    ng = N_ROUTE_GROUPS
    gsz = N_EXPERTS // ng
    b3 = biased.reshape(ng, gsz, tm)
    s3 = scores.reshape(ng, gsz, tm)
    sub = lax.broadcasted_iota(jnp.int32, (ng, gsz, tm), 1).astype(F32)
    grp = lax.broadcasted_iota(jnp.int32, (ng, gsz, tm), 0).astype(F32)
    eid = grp * gsz + sub
    neg = -jnp.inf
    m1 = jnp.max(b3, axis=1, keepdims=True)
    i1 = jnp.min(jnp.where(b3 == m1, sub, float(gsz)), axis=1, keepdims=True)
    m2 = jnp.max(jnp.where(sub == i1, neg, b3), axis=1, keepdims=True)
    gs = m1 + m2
    gi = lax.broadcasted_iota(jnp.int32, (ng, 1, tm), 0)
    beaten = jnp.zeros((ng, 1, tm), F32)
    for gp in range(ng):
        o_ = gs[gp:gp + 1]
        beats = (o_ > gs) | ((o_ == gs) & (gi > gp))
        beaten = beaten + beats.astype(F32)
    gmask = beaten < float(TOPK_ROUTE_GROUPS)
    masked = jnp.where(gmask, b3, neg)

    sels = []
    picked = jnp.zeros((ng, gsz, tm), F32)
    for k in range(TOP_K):
        m = jnp.max(jnp.max(masked, axis=0, keepdims=True), axis=1, keepdims=True)
        cand = jnp.where(masked == m, eid, float(N_EXPERTS))
        sel = jnp.min(jnp.min(cand, axis=0, keepdims=True), axis=1, keepdims=True)
        oh = eid == sel
        masked = jnp.where(oh, neg, masked)
        picked = jnp.where(oh, 1.0, picked)
        sels.append(sel)

    pm = picked.reshape(N_EXPERTS, tm)
    prefix = jnp.dot(pm.astype(BF16), tri_ref[...], preferred_element_type=F32) + run_ref[:, 0:1]
    p3 = prefix.reshape(ng, gsz, tm)
    run_new = run_ref[...] + jnp.sum(pm, axis=1, keepdims=True)
    run_ref[...] = run_new
    cnt_ref[...] = run_new

    sc_rows = []
    for k in range(TOP_K):
        oh = eid == sels[k]
        sc_k = jnp.sum(jnp.sum(jnp.where(oh, s3, 0.0), axis=0, keepdims=True), axis=1, keepdims=True)
        rk_k = jnp.sum(jnp.sum(jnp.where(oh, p3, 0.0), axis=0, keepdims=True), axis=1, keepdims=True)
        sc_rows.append(sc_k)
        eidx_ref[k:k + 1, :] = sels[k].reshape(1, tm).astype(jnp.int32)
        rank_ref[k:k + 1, :] = rk_k.reshape(1, tm).astype(jnp.int32)
    tot = sc_rows[0]
    for k in range(1, TOP_K):
        tot = tot + sc_rows[k]
    inv = ROUTED_SCALE / (tot + 1e-20)
    for k in range(TOP_K):
        gw_ref[k:k + 1, :] = (sc_rows[k] * inv).reshape(1, tm)


def _mix(half, a_n, yt, u2, x2, mod, d_skip, wglu_bf, b_glu, g_out_ssm, wo_a, wo_b, g_post_mix,
         g_pre_ffn, w_router_split, b_router_col, tri):
    tm = TM_MIX
    t0 = half * (HALF_TOK // tm)
    row = lambda n: pl.BlockSpec((1, n), lambda i: (0, 0))
    full = lambda a, b: pl.BlockSpec((a, b), lambda i: (0, 0))
    tok_in = lambda n: pl.BlockSpec((tm, n), lambda i: (t0 + i, 0))
    tok = lambda n: pl.BlockSpec((tm, n), lambda i: (i, 0))
    col = pl.BlockSpec((TOP_K, tm), lambda i: (0, i))
    return pl.pallas_call(
        _mix_kernel,
        grid=(HALF_TOK // tm,),
        in_specs=[tok_in(CONV_CH),
                  pl.BlockSpec((SSM_GROUPS, tm // S5_Q, S5_QH), lambda i: (0, t0 + i, 0)),
                  tok_in(SSM_CH), tok_in(D_MODEL),
                  pl.BlockSpec((1, MOD_ROWS, D_MODEL), lambda i: (half, 0, 0)),
                  row(SSM_CH), full(SSM_CH, SSM_CH), row(SSM_CH), row(SSM_CH),
                  full(CONV_CH, D_MODEL), full(SSM_CH, D_MODEL), row(D_MODEL), row(D_MODEL),
                  full(2 * N_EXPERTS, D_MODEL), full(N_EXPERTS, 1), full(tm, tm)],
        out_specs=[tok(D_MODEL), tok(D_MODEL // 2), col, col, col,
                   pl.BlockSpec((N_EXPERTS, LANES), lambda i: (0, 0))],
        out_shape=[jax.ShapeDtypeStruct((HALF_TOK, D_MODEL), F32),
                   jax.ShapeDtypeStruct((HALF_TOK, D_MODEL // 2), jnp.uint32),
                   jax.ShapeDtypeStruct((TOP_K, HALF_TOK), jnp.int32),
                   jax.ShapeDtypeStruct((TOP_K, HALF_TOK), jnp.int32),
                   jax.ShapeDtypeStruct((TOP_K, HALF_TOK), F32),
                   jax.ShapeDtypeStruct((N_EXPERTS, LANES), F32)],
        scratch_shapes=[pltpu.VMEM((N_EXPERTS, LANES), F32),
                        pltpu.VMEM((SSM_CH // LANES, tm, LANES), F32)],
        compiler_params=_cparams(("arbitrary",)),
        name="mix_out_router",
    )(a_n, yt, u2, x2, mod, d_skip, wglu_bf, b_glu, g_out_ssm, wo_a, wo_b, g_post_mix,
      g_pre_ffn, w_router_split, b_router_col, tri)


def _expert_kernel(blk0_ref, nblk_ref, bunit_ref, bfull_ref, xs_hbm, wgu_ref, wd_ref, ys_hbm,
                   xbuf, ybuf, sem_in, sem_out):
    e = pl.program_id(0)
    n = nblk_ref[e]
    b0 = blk0_ref[e]
    n_all = blk0_ref[N_EXPERTS - 1] + nblk_ref[N_EXPERTS - 1]

    def copies(b, slot, rows):
        hbm = pl.ds(pl.multiple_of(bunit_ref[b] * ROW_UNIT, ROW_UNIT), rows)
        buf = pl.ds(0, rows)
        return (pltpu.make_async_copy(xs_hbm.at[hbm], xbuf.at[slot, buf], sem_in.at[slot]),
                pltpu.make_async_copy(ybuf.at[slot, buf], ys_hbm.at[hbm], sem_out.at[slot]))

    def for_size(b, fn):
        full = bfull_ref[b] == 1

        @pl.when(full)
        def _():
            fn(ROW_BLOCK)

        @pl.when(jnp.logical_not(full))
        def _():
            fn(ROW_UNIT)

    for b in range(EXPERT_AHEAD):
        @pl.when((e == 0) & (b < n_all))
        def _():
            for_size(b, lambda rows: copies(b, b, rows)[0].start())

    def compute(slot, rows):
        x_lo, x_hi = _unpack_halves(xbuf[slot, pl.ds(0, rows)])
        x = jnp.concatenate([x_lo.astype(BF16), x_hi.astype(BF16)], axis=1)
        h = jnp.dot(x, wgu_ref[0], preferred_element_type=F32)
        hg = h[:, :D_EXPERT]
        act = hg * jax.nn.sigmoid(hg) * h[:, D_EXPERT:]
        ybuf[slot, pl.ds(0, rows)] = _pack_halves(
            jnp.dot(act.astype(BF16), wd_ref[0], preferred_element_type=F32))

    def block(b, carry):
        slot = b % EXPERT_SLOTS
        ahead = b + EXPERT_AHEAD

        @pl.when(ahead < n_all)
        def _():
            for_size(ahead, lambda rows: copies(ahead, ahead % EXPERT_SLOTS, rows)[0].start())

        @pl.when(b >= EXPERT_SLOTS)
        def _():
            for_size(b - EXPERT_SLOTS, lambda rows: copies(b - EXPERT_SLOTS, slot, rows)[1].wait())

        def work(rows):
            cp_in, cp_out = copies(b, slot, rows)
            cp_in.wait()
            compute(slot, rows)
            cp_out.start()

        for_size(b, work)
        return carry

    lax.fori_loop(b0, b0 + n, block, 0)

    @pl.when(e == N_EXPERTS - 1)
    def _():
        for j in range(1, EXPERT_SLOTS + 1):
            @pl.when(n_all >= j)
            def _():
                last = n_all - j
                for_size(last, lambda rows: copies(last, last % EXPERT_SLOTS, rows)[1].wait())


def _experts(blk0, nblk, bunit, bfull, xs, we_gu, we_d):
    any_spec = pl.BlockSpec(memory_space=pl.ANY)
    grid_spec = pltpu.PrefetchScalarGridSpec(
        num_scalar_prefetch=4,
        grid=(N_EXPERTS,),
        in_specs=[any_spec,
                  pl.BlockSpec((1, D_MODEL, 2 * D_EXPERT), lambda e, *_: (e, 0, 0)),
                  pl.BlockSpec((1, D_EXPERT, D_MODEL), lambda e, *_: (e, 0, 0))],
        out_specs=any_spec,
        scratch_shapes=[pltpu.VMEM((EXPERT_SLOTS, ROW_BLOCK, D_MODEL // 2), jnp.uint32),
                        pltpu.VMEM((EXPERT_SLOTS, ROW_BLOCK, D_MODEL // 2), jnp.uint32),
                        pltpu.SemaphoreType.DMA((EXPERT_SLOTS,)),
                        pltpu.SemaphoreType.DMA((EXPERT_SLOTS,))],
    )
    return pl.pallas_call(
        _expert_kernel,
        grid_spec=grid_spec,
        out_shape=jax.ShapeDtypeStruct((N_ROWS, D_MODEL // 2), jnp.uint32),
        compiler_params=_cparams(("arbitrary",)),
        name="routed_experts",
    )(blk0, nblk, bunit, bfull, xs, we_gu, we_d)


def _final_kernel(h2_ref, yg_ref, gw_ref, x1_ref, mod_ref, wgu_ref, wd_ref, g_ref, *rest):
    o_ref = rest[-1]
    half = D_MODEL // 2
    gt_f = mod_ref[0, 5:6, :]
    x_lo, x_hi = _unpack_halves(h2_ref[...])
    h = (jnp.dot(x_lo.astype(BF16), wgu_ref[:half, :], preferred_element_type=F32)
         + jnp.dot(x_hi.astype(BF16), wgu_ref[half:, :], preferred_element_type=F32))
    hg = h[:, :D_EXPERT]
    act = hg * jax.nn.sigmoid(hg) * h[:, D_EXPERT:]
    shared = jnp.dot(act.astype(BF16), wd_ref[...], preferred_element_type=F32)
    y_lo = shared[:, :half]
    y_hi = shared[:, half:]
    for k in range(TOP_K):
        r_lo, r_hi = _unpack_halves(yg_ref[k])
        w = gw_ref[:, k:k + 1]
        y_lo = y_lo + w * r_lo
        y_hi = y_hi + w * r_hi
    ms = (jnp.sum(y_lo * y_lo, axis=-1, keepdims=True)
          + jnp.sum(y_hi * y_hi, axis=-1, keepdims=True)) * (1.0 / D_MODEL)
    inv = lax.rsqrt(ms + NORM_EPS)
    o_ref[:, :half] = x1_ref[:, :half] + gt_f[:, :half] * (y_lo * inv * g_ref[:, :half])
    o_ref[:, half:] = x1_ref[:, half:] + gt_f[:, half:] * (y_hi * inv * g_ref[:, half:])


def _final(half, out_prev, h2p, yg, gw_t, x1, mod, ws_gu, ws_d, g_post_ffn):
    tm = TM_OUT
    t0 = half * (HALF_TOK // tm)
    tok = pl.BlockSpec((tm, D_MODEL), lambda i: (i, 0))
    in_specs = [pl.BlockSpec((tm, D_MODEL // 2), lambda i: (i, 0)),
                pl.BlockSpec((TOP_K, tm, D_MODEL // 2), lambda i: (0, i, 0)),
                pl.BlockSpec((tm, TOP_K), lambda i: (i, 0)),
                tok,
                pl.BlockSpec((1, MOD_ROWS, D_MODEL), lambda i: (half, 0, 0)),
                pl.BlockSpec((D_MODEL, 2 * D_EXPERT), lambda i: (0, 0)),
                pl.BlockSpec((D_EXPERT, D_MODEL), lambda i: (0, 0)),
                pl.BlockSpec((1, D_MODEL), lambda i: (0, 0))]
    args = [h2p, yg, gw_t, x1, mod, ws_gu, ws_d, g_post_ffn]
    aliases = {}
    if out_prev is not None:
        aliases = {len(args): 0}
        in_specs.append(pl.BlockSpec(memory_space=pl.ANY))
        args.append(out_prev)
    return pl.pallas_call(
        _final_kernel,
        grid=(HALF_TOK // tm,),
        in_specs=in_specs,
        out_specs=pl.BlockSpec((tm, D_MODEL), lambda i: (t0 + i, 0)),
        out_shape=jax.ShapeDtypeStruct((N_TOK, D_MODEL), F32),
        input_output_aliases=aliases,
        compiler_params=_cparams(("parallel",)),
        name="shared_final",
    )(*args)


def _sc_worker_id():
    return lax.axis_index("s") * SC_CORES + lax.axis_index("c")


def _dispatch_body(h_hbm, dest_hbm, xs_hbm, idx_v, rows_v, sem_l, sem_s):
    n = SC_CHUNKS_PER_WORKER
    c0 = _sc_worker_id() * n

    def load(i, b):
        return pltpu.async_copy(h_hbm.at[pl.ds((c0 + i) * SC_W, SC_W)], rows_v.at[b], sem_l.at[b])

    loads = [None] * n
    scat = [None] * n
    loads[0] = load(0, 0)
    for i in range(n):
        b = i % 2
        pltpu.sync_copy(dest_hbm.at[c0 + i], idx_v.at[b])
        loads[i].wait()
        if i + 1 < n:
            if i >= 1:
                for d in scat[i - 1]:
                    d.wait()
            loads[i + 1] = load(i + 1, 1 - b)
        scat[i] = [pltpu.async_copy(rows_v.at[b], xs_hbm.at[idx_v.at[b].at[k]], sem_s.at[b])
                   for k in range(TOP_K)]
    for i in (n - 2, n - 1):
        for d in scat[i]:
            d.wait()


def _sc_dispatch(h2p, dest3):
    mesh = plsc.VectorSubcoreMesh(core_axis_name="c", subcore_axis_name="s")
    return pl.kernel(
        _dispatch_body, mesh=mesh,
        out_type=jax.ShapeDtypeStruct((N_ROWS, D_MODEL // 2), jnp.uint32),
        scratch_types=[pltpu.VMEM((2, TOP_K, SC_W), jnp.int32),
                       pltpu.VMEM((2, SC_W, D_MODEL // 2), jnp.uint32),
                       pltpu.SemaphoreType.DMA((2,)), pltpu.SemaphoreType.DMA((2,))],
    )(h2p, dest3)


def _combine_body(ys_hbm, dest_hbm, yg_hbm, idx_v, rows_v, sem_g, sem_w):
    c0 = _sc_worker_id() * SC_CHUNKS_PER_WORKER

    @pl.loop(0, SC_CHUNKS_PER_WORKER)
    def _(i):
        c = c0 + i
        pltpu.sync_copy(dest_hbm.at[c], idx_v)
        g = [None] * TOP_K
        w = [None] * TOP_K
        g[0] = pltpu.async_copy(ys_hbm.at[idx_v.at[0]], rows_v.at[0], sem_g.at[0])
        for k in range(TOP_K):
            b = k % 2
            g[k].wait()
            if k + 1 < TOP_K:
                if k >= 1:
                    w[k - 1].wait()
                g[k + 1] = pltpu.async_copy(ys_hbm.at[idx_v.at[k + 1]], rows_v.at[1 - b], sem_g.at[1 - b])
            w[k] = pltpu.async_copy(rows_v.at[b], yg_hbm.at[k].at[pl.ds(c * SC_W, SC_W)], sem_w.at[b])
        w[TOP_K - 2].wait()
        w[TOP_K - 1].wait()


def _sc_combine(ysp, dest3):
    mesh = plsc.VectorSubcoreMesh(core_axis_name="c", subcore_axis_name="s")
    return pl.kernel(
        _combine_body, mesh=mesh,
        out_type=jax.ShapeDtypeStruct((TOP_K, HALF_TOK, D_MODEL // 2), jnp.uint32),
        scratch_types=[pltpu.VMEM((TOP_K, SC_W), jnp.int32),
                       pltpu.VMEM((2, SC_W, D_MODEL // 2), jnp.uint32),
                       pltpu.SemaphoreType.DMA((2,)), pltpu.SemaphoreType.DMA((2,))],
    )(ysp, dest3)


def kernel(x, c, w_ada, b_ada, g_pre_mix, g_post_mix, w_in, conv_w, conv_b, conv_ln_g, conv_ln_b,
           ssm_a_re, ssm_a_im, ssm_log_dt, ssm_b_re, ssm_b_im, ssm_c_re, ssm_c_im, ssm_d,
           ssm_w_glu, ssm_b_glu, g_out_conv, g_out_ssm, w_out, g_pre_ffn, g_post_ffn,
           w_router, b_router, we_gate, we_up, we_down, ws_gate, ws_up, ws_down):
    l = 0
    x2 = x.reshape(N_TOK, D_MODEL)
    r1 = lambda a: a.reshape(1, -1)

    c_pad = jnp.zeros((SUBLANES, D_MODEL), F32).at[:BATCH].set(c)
    mod = _ada(c_pad, w_ada[l], r1(b_ada[l]))[:BATCH].reshape(BATCH, N_MOD, D_MODEL)
    mod = jnp.concatenate([mod, jnp.zeros((BATCH, MOD_ROWS - N_MOD, D_MODEL), F32)], axis=1)

    v, u, ut = _inproj(x2, mod, r1(g_pre_mix[l]), w_in[l].astype(BF16))
    cw = jnp.concatenate([conv_w[l].reshape(CONV_WIDTH, CONV_CH), jnp.zeros((1, CONV_CH), F32)], axis=0)
    a_n, we_gu, we_d = _conv(v.reshape(BATCH, SEQ, CONV_CH), cw, r1(conv_b[l]), r1(conv_ln_g[l]),
                             r1(conv_ln_b[l]), r1(g_out_conv[l]), we_gate[l], we_up[l], we_down[l])
    a_n = a_n.reshape(N_TOK, CONV_CH)

    pwr, pwi, s5_params, a_cat, b_q, b_s = _s5_operators(
        ssm_a_re[l], ssm_a_im[l], ssm_log_dt[l], ssm_b_re[l], ssm_b_im[l], ssm_c_re[l], ssm_c_im[l])
    yt = _s5(ut, pwr, pwi, s5_params, a_cat, b_q, b_s)

    tm = TM_MIX
    tri = (jnp.arange(tm)[:, None] < jnp.arange(tm)[None, :]).astype(BF16)
    wo = w_out[l].astype(BF16)
    wr_t = w_router[l].T
    wr_hi = wr_t.astype(BF16)
    wr_split = jnp.concatenate([wr_hi, (wr_t - wr_hi.astype(F32)).astype(BF16)], axis=0)
    mix_params = (r1(ssm_d[l]), ssm_w_glu[l].astype(BF16), r1(ssm_b_glu[l]), r1(g_out_ssm[l]),
                  wo[:CONV_CH], wo[CONV_CH:], r1(g_post_mix[l]), r1(g_pre_ffn[l]),
                  wr_split, b_router[l].reshape(N_EXPERTS, 1), tri)
    ws_gu = jnp.concatenate([ws_gate[l], ws_up[l]], axis=1).astype(BF16)
    ws_d = ws_down[l].astype(BF16)
    e_ids = jnp.arange(N_EXPERTS, dtype=jnp.int32)

    out = None
    for half in range(N_HALVES):
        x1, h2, eidx, rank, gw, cnt = _mix(half, a_n, yt, u, x2, mod, *mix_params)
        counts = cnt[:, 0].astype(jnp.int32)
        units = (counts + ROW_UNIT - 1) // ROW_UNIT
        unit0 = jnp.cumsum(units) - units
        nblk = (units + 1) // 2
        blk0 = jnp.cumsum(nblk) - nblk
        dest = rank + jnp.sum(jnp.where(eidx[..., None] == e_ids, unit0 * ROW_UNIT, 0), axis=-1)
        dest3 = dest.reshape(TOP_K, HALF_TOK // SC_W, SC_W).transpose(1, 0, 2)
        b_ids = jnp.arange(N_BLOCKS, dtype=jnp.int32)
        owner = (b_ids[:, None] >= blk0[None, :]) & (b_ids[:, None] < (blk0 + nblk)[None, :])
        pick = lambda v: jnp.sum(jnp.where(owner, v[None, :], 0), axis=1)
        local = b_ids - pick(blk0)
        bunit = pick(unit0) + 2 * local
        bfull = (2 * local + 2 <= pick(units)).astype(jnp.int32)

        xs = _sc_dispatch(h2, dest3)
        ys = _experts(blk0, nblk, bunit, bfull, xs, we_gu, we_d)
        yg = _sc_combine(ys, dest3)
        out = _final(half, out, h2, yg, gw.T, x1, mod, ws_gu, ws_d, r1(g_post_ffn[l]))
    return out.reshape(BATCH, SEQ, D_MODEL)
```

```python
import math

import jax
import jax.numpy as jnp
from jax import lax
from jax.experimental import pallas as pl
from jax.experimental.pallas import tpu as pltpu
from jax.experimental.pallas import tpu_sc as plsc

F32 = jnp.float32
BF16 = jnp.bfloat16

D_MODEL = 1024
BATCH = 2
SEQ = 8192
N_TOK = BATCH * SEQ
CONV_CH = 512
CONV_WIDTH = 31
SSM_CH = 512
SSM_GROUP_CH = 16
SSM_GROUPS = 32
SSM_STATE = 64
D_IN = 2 * CONV_CH + SSM_CH
N_EXPERTS = 64
TOP_K = 8
N_ROUTE_GROUPS = 8
TOPK_ROUTE_GROUPS = 4
D_EXPERT = 256
ROUTED_SCALE = 2.5
NORM_EPS = 1e-6

SUBLANES = 8
LANES = 128

N_MOD = 6
MOD_ROWS = SUBLANES
ADA_COLS = 1536
TM_IN = 1024
IN_SUBTILES = 4
TL_CONV = 512
CONV_HALO = 32
CONV_ROWS = 256
EXPERTS_PER_CONV_STEP = N_EXPERTS * TL_CONV // N_TOK
assert EXPERTS_PER_CONV_STEP * N_TOK == N_EXPERTS * TL_CONV
S5_Q = 32
S5_GROUPS_PER_STEP = 4
S5_QH = S5_Q * SSM_GROUP_CH
S5_CHUNKS = N_TOK // S5_Q
S5_CHUNKS_PER_SEQ = SEQ // S5_Q
TM_MIX = 1024
ROW_BLOCK = 512
ROW_UNIT = ROW_BLOCK // 2
EXPERT_AHEAD = 4
EXPERT_SLOTS = EXPERT_AHEAD + 1
HALF_TOK = SEQ
N_HALVES = N_TOK // HALF_TOK
N_UNITS = HALF_TOK * TOP_K // ROW_UNIT + N_EXPERTS
N_BLOCKS = (N_UNITS + N_EXPERTS) // 2
N_ROWS = N_UNITS * ROW_UNIT
TM_OUT = 512
SC_CORES = 2
SC_SUBCORES = 16
SC_WORKERS = SC_CORES * SC_SUBCORES
SC_W = 64
SC_CHUNKS_PER_WORKER = HALF_TOK // (SC_WORKERS * SC_W)
MIB = 1024 * 1024


def _cparams(sem, vmem_mib):
    return pltpu.CompilerParams(dimension_semantics=sem, vmem_limit_bytes=vmem_mib * MIB)


def _pack_rounded_halves(xr):
    n = xr.shape[-1] // 2
    lo = lax.bitcast_convert_type(xr[:, :n], jnp.uint32)
    hi = lax.bitcast_convert_type(xr[:, n:], jnp.uint32)
    return hi | (lo >> 16)


def _pack_halves(x):
    return _pack_rounded_halves(x.astype(BF16).astype(F32))


def _unpack_halves(p):
    lo = lax.bitcast_convert_type(p << 16, F32)
    hi = lax.bitcast_convert_type(p & jnp.uint32(0xFFFF0000), F32)
    return lo, hi


def _rms(x, g):
    return x * lax.rsqrt(jnp.mean(x * x, axis=-1, keepdims=True) + NORM_EPS) * g


def _split_bf16(x):
    hi = x.astype(BF16)
    return hi, (x - hi.astype(F32)).astype(BF16)


def _dot_nt_split(a, b):
    nt = (((1,), (1,)), ((), ()))
    a_hi, a_lo = _split_bf16(a)
    b_hi, b_lo = _split_bf16(b)
    m = a.shape[0]
    both = lax.dot_general(jnp.concatenate([a_hi, a_lo], axis=0), b_hi, nt, preferred_element_type=F32)
    return both[:m] + both[m:] + lax.dot_general(a_hi, b_lo, nt, preferred_element_type=F32)


def _ada_kernel(c_ref, w_ref, b_ref, o_ref):
    c = c_ref[...]
    a = c * jax.nn.sigmoid(c)
    o_ref[...] = jnp.dot(a, w_ref[...], preferred_element_type=F32,
                         precision=lax.Precision.HIGHEST) + b_ref[...]


def _ada(c_pad, w_ada, b_ada):
    n = w_ada.shape[1]
    bn = ADA_COLS
    return pl.pallas_call(
        _ada_kernel,
        grid=(n // bn,),
        in_specs=[pl.BlockSpec((SUBLANES, D_MODEL), lambda j: (0, 0)),
                  pl.BlockSpec((D_MODEL, bn), lambda j: (0, j)),
                  pl.BlockSpec((1, bn), lambda j: (0, j))],
        out_specs=pl.BlockSpec((SUBLANES, bn), lambda j: (0, j)),
        out_shape=jax.ShapeDtypeStruct((SUBLANES, n), F32),
        compiler_params=_cparams(("arbitrary",), 28),
        name="ada_mod",
    )(c_pad, w_ada, b_ada)


GROUPS_PER_LANE_TILE = LANES // SSM_GROUP_CH


def _to_group_chunks(u, tile_ref, ut_ref):
    n_chunks = u.shape[0] // S5_Q
    for j in range(SSM_CH // LANES):
        tile_ref[j] = u[:, LANES * j:LANES * (j + 1)]
    for j in range(SSM_CH // LANES):
        rows_t = [tile_ref[j, pl.ds(t, n_chunks, stride=S5_Q), :] for t in range(S5_Q)]
        for gg in range(GROUPS_PER_LANE_TILE):
            lo = gg * SSM_GROUP_CH
            row = jnp.concatenate([r[:, lo:lo + SSM_GROUP_CH] for r in rows_t], axis=1)
            ut_ref[j * GROUPS_PER_LANE_TILE + gg] = row.astype(ut_ref.dtype)


def _from_group_chunks(yt_ref, tile_ref):
    n_chunks = yt_ref.shape[1]
    for j in range(SSM_CH // LANES):
        for t in range(S5_Q):
            lo = t * SSM_GROUP_CH
            piece = jnp.concatenate(
                [yt_ref[j * GROUPS_PER_LANE_TILE + gg, :, lo:lo + SSM_GROUP_CH]
                 for gg in range(GROUPS_PER_LANE_TILE)], axis=1)
            tile_ref[j, pl.ds(t, n_chunks, stride=S5_Q), :] = piece
    return jnp.concatenate([tile_ref[j] for j in range(SSM_CH // LANES)], axis=1)


def _inproj_kernel(x_ref, mod_ref, g_ref, w_ref, v_ref, u_ref, ut_ref, tile_ref):
    sh = mod_ref[0, 0:1, :]
    sc = mod_ref[0, 1:2, :]
    sub = TM_IN // IN_SUBTILES
    sub_chunks = sub // S5_Q
    for s in range(IN_SUBTILES):
        r = slice(s * sub, (s + 1) * sub)
        h = _rms(x_ref[r, :], g_ref[...]) * (1.0 + sc) + sh
        z = jnp.dot(h.astype(BF16), w_ref[...], preferred_element_type=F32)
        v_ref[r, :] = z[:, :CONV_CH] * jax.nn.sigmoid(z[:, CONV_CH:2 * CONV_CH])
        u = z[:, 2 * CONV_CH:]
        u_ref[r, :] = u
        _to_group_chunks(u, tile_ref.at[s], ut_ref.at[:, s * sub_chunks:(s + 1) * sub_chunks, :])


def _inproj(x2, mod, g_pre, w_in_bf):
    tiles_per_seq = SEQ // TM_IN
    return pl.pallas_call(
        _inproj_kernel,
        grid=(N_TOK // TM_IN,),
        in_specs=[pl.BlockSpec((TM_IN, D_MODEL), lambda i: (i, 0)),
                  pl.BlockSpec((1, MOD_ROWS, D_MODEL), lambda i: (i // tiles_per_seq, 0, 0)),
                  pl.BlockSpec((1, D_MODEL), lambda i: (0, 0)),
                  pl.BlockSpec((D_MODEL, D_IN), lambda i: (0, 0))],
        out_specs=[pl.BlockSpec((TM_IN, CONV_CH), lambda i: (i, 0)),
                   pl.BlockSpec((TM_IN, SSM_CH), lambda i: (i, 0)),
                   pl.BlockSpec((SSM_GROUPS, TM_IN // S5_Q, S5_QH), lambda i: (0, i, 0))],
        out_shape=[jax.ShapeDtypeStruct((N_TOK, CONV_CH), F32),
                   jax.ShapeDtypeStruct((N_TOK, SSM_CH), F32),
                   jax.ShapeDtypeStruct((SSM_GROUPS, S5_CHUNKS, S5_QH), BF16)],
        scratch_shapes=[pltpu.VMEM((IN_SUBTILES, SSM_CH // LANES, TM_IN // IN_SUBTILES, LANES), F32)],
        compiler_params=_cparams(("parallel",), 40),
        name="in_proj",
    )(x2, mod, g_pre, w_in_bf)


def _conv_kernel(vc_ref, vp_ref, w_ref, cb_ref, lg_ref, lb_ref, go_ref, wg_ref, wu_ref, wd_ref,
                 o_ref, wgu_o, wd_o, sh_ref):
    for q in range(EXPERTS_PER_CONV_STEP):
        wgu_o[q, :, :D_EXPERT] = wg_ref[q].astype(BF16)
        wgu_o[q, :, D_EXPERT:] = wu_ref[q].astype(BF16)
        wd_o[q] = wd_ref[q].astype(BF16)

    i = pl.program_id(1)
    keep = (i > 0).astype(F32)
    n_ext = TL_CONV + CONV_HALO
    sh_ref[0, 0:CONV_HALO, :] = vp_ref[0] * keep
    sh_ref[0, CONV_HALO:, :] = vc_ref[0]
    ext = sh_ref[0]
    for s in range(1, SUBLANES):
        sh_ref[s] = pltpu.roll(ext, n_ext - s, axis=0)
    off = CONV_HALO - (CONV_WIDTH - 1)
    for r in range(TL_CONV // CONV_ROWS):
        acc = None
        for j in range(CONV_WIDTH):
            s = (off + j) % SUBLANES
            al = r * CONV_ROWS + (off + j) - s
            term = w_ref[j:j + 1, :] * sh_ref[s, al:al + CONV_ROWS, :]
            acc = term if acc is None else acc + term
        y = acc + cb_ref[...]
        mu = jnp.mean(y, axis=-1, keepdims=True)
        d = y - mu
        var = jnp.mean(d * d, axis=-1, keepdims=True)
        yn = d * lax.rsqrt(var + NORM_EPS) * lg_ref[...] + lb_ref[...]
        a = yn * jax.nn.sigmoid(yn)
        o_ref[0, r * CONV_ROWS:(r + 1) * CONV_ROWS, :] = _rms(a, go_ref[...]).astype(BF16)


def _conv(v3, conv_w, conv_b, ln_g, ln_b, g_out, we_gate, we_up, we_down):
    halo_per_tile = TL_CONV // CONV_HALO
    steps_per_seq = SEQ // TL_CONV
    vec = pl.BlockSpec((1, CONV_CH), lambda b, i: (0, 0))
    ex = EXPERTS_PER_CONV_STEP
    w_in = pl.BlockSpec((ex, D_MODEL, D_EXPERT), lambda b, i: (b * steps_per_seq + i, 0, 0))
    return pl.pallas_call(
        _conv_kernel,
        grid=(BATCH, steps_per_seq),
        in_specs=[pl.BlockSpec((1, TL_CONV, CONV_CH), lambda b, i: (b, i, 0)),
                  pl.BlockSpec((1, CONV_HALO, CONV_CH),
                               lambda b, i: (b, jnp.maximum(i * halo_per_tile - 1, 0), 0)),
                  pl.BlockSpec((CONV_WIDTH + 1, CONV_CH), lambda b, i: (0, 0)),
                  vec, vec, vec, vec,
                  w_in, w_in,
                  pl.BlockSpec((ex, D_EXPERT, D_MODEL), lambda b, i: (b * steps_per_seq + i, 0, 0))],
        out_specs=[pl.BlockSpec((1, TL_CONV, CONV_CH), lambda b, i: (b, i, 0)),
                   pl.BlockSpec((ex, D_MODEL, 2 * D_EXPERT), lambda b, i: (b * steps_per_seq + i, 0, 0)),
                   pl.BlockSpec((ex, D_EXPERT, D_MODEL), lambda b, i: (b * steps_per_seq + i, 0, 0))],
        out_shape=[jax.ShapeDtypeStruct((BATCH, SEQ, CONV_CH), BF16),
                   jax.ShapeDtypeStruct((N_EXPERTS, D_MODEL, 2 * D_EXPERT), BF16),
                   jax.ShapeDtypeStruct((N_EXPERTS, D_EXPERT, D_MODEL), BF16)],
        scratch_shapes=[pltpu.VMEM((SUBLANES, TL_CONV + CONV_HALO, CONV_CH), F32)],
        compiler_params=_cparams(("parallel", "arbitrary"), 40),
        name="conv_module",
    )(v3, v3, conv_w, conv_b, ln_g, ln_b, g_out, we_gate, we_up, we_down)


S5_GROUP_ROWS = S5_CHUNKS + SUBLANES


S5_POW_ROWS = (S5_Q + 1 + SUBLANES - 1) // SUBLANES * SUBLANES
(S5_BB_RI, S5_BB_NIR, S5_BB_IR, S5_BB_RNI, S5_CC_RI, S5_CC_NIR, S5_N_PARAM) = range(7)


def _s5_kernel(ut_ref, pwr_ref, pwi_ref, par_ref, a_ref, bq_ref, bs_ref, yt_ref, sin_s, sp_s):
    phase = pl.program_id(0)
    g = pl.program_id(1)
    q = S5_Q
    n = 2 * SSM_STATE

    def group_row0(k):
        return pl.multiple_of((g * S5_GROUPS_PER_STEP + k) * S5_GROUP_ROWS, SUBLANES)

    def lam_pow(k, j):
        return pwr_ref[k, j:j + 1, :], pwi_ref[k, j:j + 1, :]

    @pl.when(phase == 0)
    def _():
        for k in range(S5_GROUPS_PER_STEP):
            bb_ri, bb_nir = par_ref[k, S5_BB_RI], par_ref[k, S5_BB_NIR]
            bb_ir, bb_rni = par_ref[k, S5_BB_IR], par_ref[k, S5_BB_RNI]
            blk_q, blk_s = [], []
            for t in range(q):
                pr, pi_ = lam_pow(k, q - 1 - t)
                blk_q.append(pr * bb_ri + pi_ * bb_nir)
                blk_s.append(pr * bb_ir + pi_ * bb_rni)
            wst = jnp.concatenate([jnp.concatenate(blk_q, axis=0), jnp.concatenate(blk_s, axis=0)], axis=1)
            r = jnp.dot(ut_ref[k], wst.astype(BF16), preferred_element_type=F32)
            sin_s[0, pl.ds(group_row0(k), S5_CHUNKS), :] = r[:, :n]
            sin_s[1, pl.ds(group_row0(k), S5_CHUNKS), :] = r[:, n:]

    @pl.when((phase == 1) & (g == 0))
    def _():
        a = a_ref[...]
        bq = bq_ref[...]
        bs = bs_ref[...]

        def body(c, carry):
            nxt = []
            for b in range(BATCH):
                x, xs = carry[b]
                rows = pl.ds(b * S5_CHUNKS_PER_SEQ + c, SSM_GROUPS, stride=S5_GROUP_ROWS)
                sp_s[rows, :] = x
                nxt.append((a * x + bq * xs + sin_s[0, rows, :], a * xs + bs * x + sin_s[1, rows, :]))
            return tuple(nxt)

        z = jnp.zeros((SSM_GROUPS, n), F32)
        lax.fori_loop(0, S5_CHUNKS_PER_SEQ, body, tuple((z, z) for _ in range(BATCH)))

    @pl.when(phase == 1)
    def _():
        for k in range(S5_GROUPS_PER_STEP):
            cc_ri, cc_nir = par_ref[k, S5_CC_RI], par_ref[k, S5_CC_NIR]
            cl = []
            for j in range(q + 1):
                pr, pi_ = lam_pow(k, j)
                cl.append(pr * cc_ri + pi_ * cc_nir)
            cl_lo = jnp.concatenate(cl[:q], axis=0)
            cl_hi = jnp.concatenate(cl[1:], axis=0)
            lane = lax.broadcasted_iota(jnp.int32, (1, n), 1)
            vgt = (cl_hi * jnp.where(lane < SSM_STATE, 1.0, -1.0)).astype(BF16)
            kt = _dot_nt_split(par_ref[k, S5_BB_RNI], cl_lo)
            padded = jnp.concatenate([jnp.zeros_like(kt), kt], axis=1)
            tg = jnp.concatenate(
                [padded[:, (q - t) * SSM_GROUP_CH:(q - t) * SSM_GROUP_CH + S5_QH] for t in range(q)],
                axis=0).astype(BF16)
            sp = sp_s[pl.ds(group_row0(k), S5_CHUNKS), :]
            y = jnp.dot(ut_ref[k], tg, preferred_element_type=F32)
            yt_ref[k] = y + lax.dot_general(sp.astype(BF16), vgt, (((1,), (1,)), ((), ())),
                                            preferred_element_type=F32)


def _s5(ut, pwr, pwi, params, a_cat, b_q, b_s):
    vec = pl.BlockSpec((SSM_GROUPS, 2 * SSM_STATE), lambda p, g: (0, 0))
    gs = S5_GROUPS_PER_STEP
    powers = pl.BlockSpec((gs, S5_POW_ROWS, 2 * SSM_STATE), lambda p, g: (g, 0, 0))
    return pl.pallas_call(
        _s5_kernel,
        grid=(2, SSM_GROUPS // gs),
        in_specs=[pl.BlockSpec((gs, S5_CHUNKS, S5_QH), lambda p, g: (g, 0, 0)),
                  powers, powers,
                  pl.BlockSpec((gs, S5_N_PARAM, SSM_GROUP_CH, 2 * SSM_STATE), lambda p, g: (g, 0, 0, 0)),
                  vec, vec, vec],
        out_specs=pl.BlockSpec((gs, S5_CHUNKS, S5_QH), lambda p, g: (g * p, 0, 0)),
        out_shape=jax.ShapeDtypeStruct((SSM_GROUPS, S5_CHUNKS, S5_QH), F32),
        scratch_shapes=[pltpu.VMEM((2, SSM_GROUPS * S5_GROUP_ROWS, 2 * SSM_STATE), F32),
                        pltpu.VMEM((SSM_GROUPS * S5_GROUP_ROWS, 2 * SSM_STATE), F32)],
        compiler_params=_cparams(("arbitrary", "arbitrary"), 46),
        name="s5_chunked",
    )(ut, pwr, pwi, params, a_cat, b_q, b_s)


def _s5_operators(a_re, a_im, log_dt, b_re, b_im, c_re, c_im):
    q = S5_Q
    dt = jnp.exp(log_dt)[:, None]
    ar, ai = a_re, a_im
    mag = jnp.exp(ar * dt)
    lr = mag * jnp.cos(ai * dt)
    li = mag * jnp.sin(ai * dt)
    den = ar * ar + ai * ai
    nr = lr - 1.0
    kr = (nr * ar + li * ai) / den
    ki = (li * ar - nr * ai) / den
    bbr = kr[..., None] * b_re - ki[..., None] * b_im
    bbi = kr[..., None] * b_im + ki[..., None] * b_re
    j = jnp.arange(q + 1, dtype=F32)[None, :, None]
    pmag = jnp.exp(ar[:, None, :] * dt[:, :, None] * j)
    pang = ai[:, None, :] * dt[:, :, None] * j
    pr = pmag * jnp.cos(pang)
    pi_ = pmag * jnp.sin(pang)
    pad = ((0, 0), (0, S5_POW_ROWS - (q + 1)), (0, 0))
    pwr = jnp.pad(jnp.concatenate([pr, pr], axis=-1), pad)
    pwi = jnp.pad(jnp.concatenate([pi_, pi_], axis=-1), pad)
    br_t = bbr.transpose(0, 2, 1)
    bi_t = bbi.transpose(0, 2, 1)
    cat = lambda a, b: jnp.concatenate([a, b], axis=-1)
    stack = [None] * S5_N_PARAM
    stack[S5_BB_RI] = cat(br_t, bi_t)
    stack[S5_BB_NIR] = cat(-bi_t, br_t)
    stack[S5_BB_IR] = cat(bi_t, br_t)
    stack[S5_BB_RNI] = cat(br_t, -bi_t)
    stack[S5_CC_RI] = cat(c_re, c_im)
    stack[S5_CC_NIR] = cat(-c_im, c_re)
    params = jnp.stack(stack, axis=1)
    aq_r, aq_i = pr[:, q], pi_[:, q]
    a_cat = cat(aq_r, aq_r)
    b_q = cat(-aq_i, aq_i)
    b_s = cat(aq_i, -aq_i)
    return pwr, pwi, params, a_cat, b_q, b_s


def _gelu_tanh(x):
    return 0.5 * x * (1.0 + jnp.tanh(math.sqrt(2.0 / math.pi) * (x + 0.044715 * (x * x * x))))


def _mix_kernel(an_ref, yt_ref, u_ref, x_ref, mod_ref, d_ref, wglu_ref, bglu_ref, gos_ref,
                woa_ref, wob_ref, gpm_ref, gpf_ref, wr_ref, br_ref, tri_ref,
                x1_ref, h2_ref, eidx_ref, rank_ref, gw_ref, cnt_ref, run_ref, tile_ref):
    i = pl.program_id(0)
    tm = TM_MIX

    @pl.when(i == 0)
    def _():
        run_ref[...] = jnp.zeros_like(run_ref)

    gt_m = mod_ref[0, 2:3, :]
    sh_f = mod_ref[0, 3:4, :]
    sc_f = mod_ref[0, 4:5, :]

    yy = _from_group_chunks(yt_ref, tile_ref) + d_ref[...] * u_ref[...]
    g = _gelu_tanh(yy)
    gl = jnp.dot(g.astype(BF16), wglu_ref[...], preferred_element_type=F32) + bglu_ref[...]
    ob = g * jax.nn.sigmoid(gl)
    bn = _rms(ob, gos_ref[...]).astype(BF16)
    o = (jnp.dot(an_ref[...], woa_ref[...], preferred_element_type=F32)
         + jnp.dot(bn, wob_ref[...], preferred_element_type=F32))
    x1 = x_ref[...] + gt_m * _rms(o, gpm_ref[...])
    x1_ref[...] = x1
    h2 = _rms(x1, gpf_ref[...]) * (1.0 + sc_f) + sh_f
    h2_hi = h2.astype(BF16)
    h2_hi32 = h2_hi.astype(F32)
    h2_ref[...] = _pack_rounded_halves(h2_hi32)

    h2_lo = (h2 - h2_hi32).astype(BF16)
    nt = (((1,), (1,)), ((), ()))
    both = lax.dot_general(wr_ref[...], h2_hi, nt, preferred_element_type=F32)
    logits = (both[:N_EXPERTS] + both[N_EXPERTS:]
              + lax.dot_general(wr_ref[:N_EXPERTS, :], h2_lo, nt, preferred_element_type=F32))
    scores = jax.nn.sigmoid(logits)
    biased = scores + br_ref[...]
    ng = N_ROUTE_GROUPS
    gsz = N_EXPERTS // ng
    b3 = biased.reshape(ng, gsz, tm)
    s3 = scores.reshape(ng, gsz, tm)
    sub = lax.broadcasted_iota(jnp.int32, (ng, gsz, tm), 1).astype(F32)
    grp = lax.broadcasted_iota(jnp.int32, (ng, gsz, tm), 0).astype(F32)
    eid = grp * gsz + sub
    neg = -jnp.inf
    m1 = jnp.max(b3, axis=1, keepdims=True)
    i1 = jnp.min(jnp.where(b3 == m1, sub, float(gsz)), axis=1, keepdims=True)
    m2 = jnp.max(jnp.where(sub == i1, neg, b3), axis=1, keepdims=True)
    gs = m1 + m2
    gi = lax.broadcasted_iota(jnp.int32, (ng, 1, tm), 0)
    beaten = jnp.zeros((ng, 1, tm), F32)
    for gp in range(ng):
        o_ = gs[gp:gp + 1]
        beats = (o_ > gs) | ((o_ == gs) & (gi > gp))
        beaten = beaten + beats.astype(F32)
    gmask = beaten < float(TOPK_ROUTE_GROUPS)
    masked = jnp.where(gmask, b3, neg)

    sels = []
    picked = jnp.zeros((ng, gsz, tm), F32)
    for k in range(TOP_K):
        m = jnp.max(jnp.max(masked, axis=0, keepdims=True), axis=1, keepdims=True)
        cand = jnp.where(masked == m, eid, float(N_EXPERTS))
        sel = jnp.min(jnp.min(cand, axis=0, keepdims=True), axis=1, keepdims=True)
        oh = eid == sel
        masked = jnp.where(oh, neg, masked)
        picked = jnp.where(oh, 1.0, picked)
        sels.append(sel)

    pm = picked.reshape(N_EXPERTS, tm)
    prefix = jnp.dot(pm.astype(BF16), tri_ref[...], preferred_element_type=F32) + run_ref[:, 0:1]
    p3 = prefix.reshape(ng, gsz, tm)
    run_new = run_ref[...] + jnp.sum(pm, axis=1, keepdims=True)
    run_ref[...] = run_new
    cnt_ref[...] = run_new

    sc_rows = []
    for k in range(TOP_K):
        oh = eid == sels[k]
        sc_k = jnp.sum(jnp.sum(jnp.where(oh, s3, 0.0), axis=0, keepdims=True), axis=1, keepdims=True)
        rk_k = jnp.sum(jnp.sum(jnp.where(oh, p3, 0.0), axis=0, keepdims=True), axis=1, keepdims=True)
        sc_rows.append(sc_k)
        eidx_ref[k:k + 1, :] = sels[k].reshape(1, tm).astype(jnp.int32)
        rank_ref[k:k + 1, :] = rk_k.reshape(1, tm).astype(jnp.int32)
    tot = sc_rows[0]
    for k in range(1, TOP_K):
        tot = tot + sc_rows[k]
    inv = ROUTED_SCALE / (tot + 1e-20)
    for k in range(TOP_K):
        gw_ref[k:k + 1, :] = (sc_rows[k] * inv).reshape(1, tm)


def _mix(half, a_n, yt, u2, x2, mod, d_skip, wglu_bf, b_glu, g_out_ssm, wo_a, wo_b, g_post_mix,
         g_pre_ffn, w_router_split, b_router_col, tri):
    tm = TM_MIX
    t0 = half * (HALF_TOK // tm)
    row = lambda n: pl.BlockSpec((1, n), lambda i: (0, 0))
    full = lambda a, b: pl.BlockSpec((a, b), lambda i: (0, 0))
    tok_in = lambda n: pl.BlockSpec((tm, n), lambda i: (t0 + i, 0))
    tok = lambda n: pl.BlockSpec((tm, n), lambda i: (i, 0))
    col = pl.BlockSpec((TOP_K, tm), lambda i: (0, i))
    return pl.pallas_call(
        _mix_kernel,
        grid=(HALF_TOK // tm,),
        in_specs=[tok_in(CONV_CH),
                  pl.BlockSpec((SSM_GROUPS, tm // S5_Q, S5_QH), lambda i: (0, t0 + i, 0)),
                  tok_in(SSM_CH), tok_in(D_MODEL),
                  pl.BlockSpec((1, MOD_ROWS, D_MODEL), lambda i: (half, 0, 0)),
                  row(SSM_CH), full(SSM_CH, SSM_CH), row(SSM_CH), row(SSM_CH),
                  full(CONV_CH, D_MODEL), full(SSM_CH, D_MODEL), row(D_MODEL), row(D_MODEL),
                  full(2 * N_EXPERTS, D_MODEL), full(N_EXPERTS, 1), full(tm, tm)],
        out_specs=[tok(D_MODEL), tok(D_MODEL // 2), col, col, col,
                   pl.BlockSpec((N_EXPERTS, LANES), lambda i: (0, 0))],
        out_shape=[jax.ShapeDtypeStruct((HALF_TOK, D_MODEL), F32),
                   jax.ShapeDtypeStruct((HALF_TOK, D_MODEL // 2), jnp.uint32),
                   jax.ShapeDtypeStruct((TOP_K, HALF_TOK), jnp.int32),
                   jax.ShapeDtypeStruct((TOP_K, HALF_TOK), jnp.int32),
                   jax.ShapeDtypeStruct((TOP_K, HALF_TOK), F32),
                   jax.ShapeDtypeStruct((N_EXPERTS, LANES), F32)],
        scratch_shapes=[pltpu.VMEM((N_EXPERTS, LANES), F32),
                        pltpu.VMEM((SSM_CH // LANES, tm, LANES), F32)],
        compiler_params=_cparams(("arbitrary",), 48),
        name="mix_out_router",
    )(a_n, yt, u2, x2, mod, d_skip, wglu_bf, b_glu, g_out_ssm, wo_a, wo_b, g_post_mix,
      g_pre_ffn, w_router_split, b_router_col, tri)


def _expert_kernel(blk0_ref, nblk_ref, bunit_ref, bfull_ref, xs_hbm, wgu_ref, wd_ref, ys_hbm,
                   xbuf, ybuf, sem_in, sem_out):
    e = pl.program_id(0)
    n = nblk_ref[e]
    b0 = blk0_ref[e]
    n_all = blk0_ref[N_EXPERTS - 1] + nblk_ref[N_EXPERTS - 1]

    def copies(b, slot, rows):
        hbm = pl.ds(pl.multiple_of(bunit_ref[b] * ROW_UNIT, ROW_UNIT), rows)
        buf = pl.ds(0, rows)
        return (pltpu.make_async_copy(xs_hbm.at[hbm], xbuf.at[slot, buf], sem_in.at[slot]),
                pltpu.make_async_copy(ybuf.at[slot, buf], ys_hbm.at[hbm], sem_out.at[slot]))

    def for_size(b, fn):
        full = bfull_ref[b] == 1

        @pl.when(full)
        def _():
            fn(ROW_BLOCK)

        @pl.when(jnp.logical_not(full))
        def _():
            fn(ROW_UNIT)

    for b in range(EXPERT_AHEAD):
        @pl.when((e == 0) & (b < n_all))
        def _():
            for_size(b, lambda rows: copies(b, b, rows)[0].start())

    def compute(slot, rows):
        x_lo, x_hi = _unpack_halves(xbuf[slot, pl.ds(0, rows)])
        x = jnp.concatenate([x_lo.astype(BF16), x_hi.astype(BF16)], axis=1)
        h = jnp.dot(x, wgu_ref[0], preferred_element_type=F32)
        hg = h[:, :D_EXPERT]
        act = hg * jax.nn.sigmoid(hg) * h[:, D_EXPERT:]
        ybuf[slot, pl.ds(0, rows)] = _pack_halves(
            jnp.dot(act.astype(BF16), wd_ref[0], preferred_element_type=F32))

    def block(b, carry):
        slot = b % EXPERT_SLOTS
        ahead = b + EXPERT_AHEAD

        @pl.when(ahead < n_all)
        def _():
            for_size(ahead, lambda rows: copies(ahead, ahead % EXPERT_SLOTS, rows)[0].start())

        @pl.when(b >= EXPERT_SLOTS)
        def _():
            for_size(b - EXPERT_SLOTS, lambda rows: copies(b - EXPERT_SLOTS, slot, rows)[1].wait())

        def work(rows):
            cp_in, cp_out = copies(b, slot, rows)
            cp_in.wait()
            compute(slot, rows)
            cp_out.start()

        for_size(b, work)
        return carry

    lax.fori_loop(b0, b0 + n, block, 0)

    @pl.when(e == N_EXPERTS - 1)
    def _():
        for j in range(1, EXPERT_SLOTS + 1):
            @pl.when(n_all >= j)
            def _():
                last = n_all - j
                for_size(last, lambda rows: copies(last, last % EXPERT_SLOTS, rows)[1].wait())


def _experts(blk0, nblk, bunit, bfull, xs, we_gu, we_d):
    any_spec = pl.BlockSpec(memory_space=pl.ANY)
    grid_spec = pltpu.PrefetchScalarGridSpec(
        num_scalar_prefetch=4,
        grid=(N_EXPERTS,),
        in_specs=[any_spec,
                  pl.BlockSpec((1, D_MODEL, 2 * D_EXPERT), lambda e, *_: (e, 0, 0)),
                  pl.BlockSpec((1, D_EXPERT, D_MODEL), lambda e, *_: (e, 0, 0))],
        out_specs=any_spec,
        scratch_shapes=[pltpu.VMEM((EXPERT_SLOTS, ROW_BLOCK, D_MODEL // 2), jnp.uint32),
                        pltpu.VMEM((EXPERT_SLOTS, ROW_BLOCK, D_MODEL // 2), jnp.uint32),
                        pltpu.SemaphoreType.DMA((EXPERT_SLOTS,)),
                        pltpu.SemaphoreType.DMA((EXPERT_SLOTS,))],
    )
    return pl.pallas_call(
        _expert_kernel,
        grid_spec=grid_spec,
        out_shape=jax.ShapeDtypeStruct((N_ROWS, D_MODEL // 2), jnp.uint32),
        compiler_params=_cparams(("arbitrary",), 28),
        name="routed_experts",
    )(blk0, nblk, bunit, bfull, xs, we_gu, we_d)


def _final_kernel(h2_ref, yg_ref, gw_ref, x1_ref, mod_ref, wgu_ref, wd_ref, g_ref, *rest):
    o_ref = rest[-1]
    half = D_MODEL // 2
    gt_f = mod_ref[0, 5:6, :]
    x_lo, x_hi = _unpack_halves(h2_ref[...])
    h = (jnp.dot(x_lo.astype(BF16), wgu_ref[:half, :], preferred_element_type=F32)
         + jnp.dot(x_hi.astype(BF16), wgu_ref[half:, :], preferred_element_type=F32))
    hg = h[:, :D_EXPERT]
    act = hg * jax.nn.sigmoid(hg) * h[:, D_EXPERT:]
    shared = jnp.dot(act.astype(BF16), wd_ref[...], preferred_element_type=F32)
    y_lo = shared[:, :half]
    y_hi = shared[:, half:]
    for k in range(TOP_K):
        r_lo, r_hi = _unpack_halves(yg_ref[k])
        w = gw_ref[:, k:k + 1]
        y_lo = y_lo + w * r_lo
        y_hi = y_hi + w * r_hi
    ms = (jnp.sum(y_lo * y_lo, axis=-1, keepdims=True)
          + jnp.sum(y_hi * y_hi, axis=-1, keepdims=True)) * (1.0 / D_MODEL)
    inv = lax.rsqrt(ms + NORM_EPS)
    o_ref[:, :half] = x1_ref[:, :half] + gt_f[:, :half] * (y_lo * inv * g_ref[:, :half])
    o_ref[:, half:] = x1_ref[:, half:] + gt_f[:, half:] * (y_hi * inv * g_ref[:, half:])


def _final(half, out_prev, h2p, yg, gw_t, x1, mod, ws_gu, ws_d, g_post_ffn):
    tm = TM_OUT
    t0 = half * (HALF_TOK // tm)
    tok = pl.BlockSpec((tm, D_MODEL), lambda i: (i, 0))
    in_specs = [pl.BlockSpec((tm, D_MODEL // 2), lambda i: (i, 0)),
                pl.BlockSpec((TOP_K, tm, D_MODEL // 2), lambda i: (0, i, 0)),
                pl.BlockSpec((tm, TOP_K), lambda i: (i, 0)),
                tok,
                pl.BlockSpec((1, MOD_ROWS, D_MODEL), lambda i: (half, 0, 0)),
                pl.BlockSpec((D_MODEL, 2 * D_EXPERT), lambda i: (0, 0)),
                pl.BlockSpec((D_EXPERT, D_MODEL), lambda i: (0, 0)),
                pl.BlockSpec((1, D_MODEL), lambda i: (0, 0))]
    args = [h2p, yg, gw_t, x1, mod, ws_gu, ws_d, g_post_ffn]
    aliases = {}
    if out_prev is not None:
        aliases = {len(args): 0}
        in_specs.append(pl.BlockSpec(memory_space=pl.ANY))
        args.append(out_prev)
    return pl.pallas_call(
        _final_kernel,
        grid=(HALF_TOK // tm,),
        in_specs=in_specs,
        out_specs=pl.BlockSpec((tm, D_MODEL), lambda i: (t0 + i, 0)),
        out_shape=jax.ShapeDtypeStruct((N_TOK, D_MODEL), F32),
        input_output_aliases=aliases,
        compiler_params=_cparams(("parallel",), 40),
        name="shared_final",
    )(*args)


def _sc_worker_id():
    return lax.axis_index("s") * SC_CORES + lax.axis_index("c")


def _dispatch_body(h_hbm, dest_hbm, xs_hbm, idx_v, rows_v, sem_l, sem_s):
    n = SC_CHUNKS_PER_WORKER
    c0 = _sc_worker_id() * n

    def load(i, b):
        return pltpu.async_copy(h_hbm.at[pl.ds((c0 + i) * SC_W, SC_W)], rows_v.at[b], sem_l.at[b])

    loads = [None] * n
    scat = [None] * n
    loads[0] = load(0, 0)
    for i in range(n):
        b = i % 2
        pltpu.sync_copy(dest_hbm.at[c0 + i], idx_v.at[b])
        loads[i].wait()
        if i + 1 < n:
            if i >= 1:
                for d in scat[i - 1]:
                    d.wait()
            loads[i + 1] = load(i + 1, 1 - b)
        scat[i] = [pltpu.async_copy(rows_v.at[b], xs_hbm.at[idx_v.at[b].at[k]], sem_s.at[b])
                   for k in range(TOP_K)]
    for i in (n - 2, n - 1):
        for d in scat[i]:
            d.wait()


def _sc_dispatch(h2p, dest3):
    mesh = plsc.VectorSubcoreMesh(core_axis_name="c", subcore_axis_name="s")
    return pl.kernel(
        _dispatch_body, mesh=mesh,
        out_type=jax.ShapeDtypeStruct((N_ROWS, D_MODEL // 2), jnp.uint32),
        scratch_types=[pltpu.VMEM((2, TOP_K, SC_W), jnp.int32),
                       pltpu.VMEM((2, SC_W, D_MODEL // 2), jnp.uint32),
                       pltpu.SemaphoreType.DMA((2,)), pltpu.SemaphoreType.DMA((2,))],
    )(h2p, dest3)


def _combine_body(ys_hbm, dest_hbm, yg_hbm, idx_v, rows_v, sem_g, sem_w):
    c0 = _sc_worker_id() * SC_CHUNKS_PER_WORKER

    @pl.loop(0, SC_CHUNKS_PER_WORKER)
    def _(i):
        c = c0 + i
        pltpu.sync_copy(dest_hbm.at[c], idx_v)
        g = [None] * TOP_K
        w = [None] * TOP_K
        g[0] = pltpu.async_copy(ys_hbm.at[idx_v.at[0]], rows_v.at[0], sem_g.at[0])
        for k in range(TOP_K):
            b = k % 2
            g[k].wait()
            if k + 1 < TOP_K:
                if k >= 1:
                    w[k - 1].wait()
                g[k + 1] = pltpu.async_copy(ys_hbm.at[idx_v.at[k + 1]], rows_v.at[1 - b], sem_g.at[1 - b])
            w[k] = pltpu.async_copy(rows_v.at[b], yg_hbm.at[k].at[pl.ds(c * SC_W, SC_W)], sem_w.at[b])
        w[TOP_K - 2].wait()
        w[TOP_K - 1].wait()


def _sc_combine(ysp, dest3):
    mesh = plsc.VectorSubcoreMesh(core_axis_name="c", subcore_axis_name="s")
    return pl.kernel(
        _combine_body, mesh=mesh,
        out_type=jax.ShapeDtypeStruct((TOP_K, HALF_TOK, D_MODEL // 2), jnp.uint32),
        scratch_types=[pltpu.VMEM((TOP_K, SC_W), jnp.int32),
                       pltpu.VMEM((2, SC_W, D_MODEL // 2), jnp.uint32),
                       pltpu.SemaphoreType.DMA((2,)), pltpu.SemaphoreType.DMA((2,))],
    )(ysp, dest3)


def kernel(x, c, w_ada, b_ada, g_pre_mix, g_post_mix, w_in, conv_w, conv_b, conv_ln_g, conv_ln_b,
           ssm_a_re, ssm_a_im, ssm_log_dt, ssm_b_re, ssm_b_im, ssm_c_re, ssm_c_im, ssm_d,
           ssm_w_glu, ssm_b_glu, g_out_conv, g_out_ssm, w_out, g_pre_ffn, g_post_ffn,
           w_router, b_router, we_gate, we_up, we_down, ws_gate, ws_up, ws_down):
    l = 0
    x2 = x.reshape(N_TOK, D_MODEL)
    r1 = lambda a: a.reshape(1, -1)

    c_pad = jnp.zeros((SUBLANES, D_MODEL), F32).at[:BATCH].set(c)
    mod = _ada(c_pad, w_ada[l], r1(b_ada[l]))[:BATCH].reshape(BATCH, N_MOD, D_MODEL)
    mod = jnp.concatenate([mod, jnp.zeros((BATCH, MOD_ROWS - N_MOD, D_MODEL), F32)], axis=1)

    v, u, ut = _inproj(x2, mod, r1(g_pre_mix[l]), w_in[l].astype(BF16))
    cw = jnp.concatenate([conv_w[l].reshape(CONV_WIDTH, CONV_CH), jnp.zeros((1, CONV_CH), F32)], axis=0)
    a_n, we_gu, we_d = _conv(v.reshape(BATCH, SEQ, CONV_CH), cw, r1(conv_b[l]), r1(conv_ln_g[l]),
                             r1(conv_ln_b[l]), r1(g_out_conv[l]), we_gate[l], we_up[l], we_down[l])
    a_n = a_n.reshape(N_TOK, CONV_CH)

    pwr, pwi, s5_params, a_cat, b_q, b_s = _s5_operators(
        ssm_a_re[l], ssm_a_im[l], ssm_log_dt[l], ssm_b_re[l], ssm_b_im[l], ssm_c_re[l], ssm_c_im[l])
    yt = _s5(ut, pwr, pwi, s5_params, a_cat, b_q, b_s)

    tm = TM_MIX
    tri = (jnp.arange(tm)[:, None] < jnp.arange(tm)[None, :]).astype(BF16)
    wo = w_out[l].astype(BF16)
    wr_t = w_router[l].T
    wr_hi = wr_t.astype(BF16)
    wr_split = jnp.concatenate([wr_hi, (wr_t - wr_hi.astype(F32)).astype(BF16)], axis=0)
    mix_params = (r1(ssm_d[l]), ssm_w_glu[l].astype(BF16), r1(ssm_b_glu[l]), r1(g_out_ssm[l]),
                  wo[:CONV_CH], wo[CONV_CH:], r1(g_post_mix[l]), r1(g_pre_ffn[l]),
                  wr_split, b_router[l].reshape(N_EXPERTS, 1), tri)
    ws_gu = jnp.concatenate([ws_gate[l], ws_up[l]], axis=1).astype(BF16)
    ws_d = ws_down[l].astype(BF16)
    e_ids = jnp.arange(N_EXPERTS, dtype=jnp.int32)

    out = None
    for half in range(N_HALVES):
        x1, h2, eidx, rank, gw, cnt = _mix(half, a_n, yt, u, x2, mod, *mix_params)
        counts = cnt[:, 0].astype(jnp.int32)
        units = (counts + ROW_UNIT - 1) // ROW_UNIT
        unit0 = jnp.cumsum(units) - units
        nblk = (units + 1) // 2
        blk0 = jnp.cumsum(nblk) - nblk
        dest = rank + jnp.sum(jnp.where(eidx[..., None] == e_ids, unit0 * ROW_UNIT, 0), axis=-1)
        dest3 = dest.reshape(TOP_K, HALF_TOK // SC_W, SC_W).transpose(1, 0, 2)
        b_ids = jnp.arange(N_BLOCKS, dtype=jnp.int32)
        owner = (b_ids[:, None] >= blk0[None, :]) & (b_ids[:, None] < (blk0 + nblk)[None, :])
        pick = lambda v: jnp.sum(jnp.where(owner, v[None, :], 0), axis=1)
        local = b_ids - pick(blk0)
        bunit = pick(unit0) + 2 * local
        bfull = (2 * local + 2 <= pick(units)).astype(jnp.int32)

        xs = _sc_dispatch(h2, dest3)
        ys = _experts(blk0, nblk, bunit, bfull, xs, we_gu, we_d)
        yg = _sc_combine(ys, dest3)
        out = _final(half, out, h2, yg, gw.T, x1, mod, ws_gu, ws_d, r1(g_post_ffn[l]))
    return out.reshape(BATCH, SEQ, D_MODEL)
```

```python
import math

import jax
import jax.numpy as jnp
from jax import lax
from jax.experimental import pallas as pl
from jax.experimental.pallas import tpu as pltpu
from jax.experimental.pallas import tpu_sc as plsc

F32 = jnp.float32
BF16 = jnp.bfloat16

D_MODEL = 1024
BATCH = 2
SEQ = 8192
N_TOK = BATCH * SEQ
CONV_CH = 512
CONV_WIDTH = 31
SSM_CH = 512
SSM_GROUP_CH = 16
SSM_GROUPS = 32
SSM_STATE = 64
D_IN = 2 * CONV_CH + SSM_CH
N_EXPERTS = 64
TOP_K = 8
N_ROUTE_GROUPS = 8
TOPK_ROUTE_GROUPS = 4
D_EXPERT = 256
ROUTED_SCALE = 2.5
NORM_EPS = 1e-6

SUBLANES = 8
LANES = 128

N_MOD = 6
MOD_ROWS = SUBLANES
ADA_COLS = 3072
TM_IN = 1024
IN_SUBTILES = 4
TL_CONV = 512
CONV_HALO = 32
CONV_ROWS = 256
EXPERTS_PER_CONV_STEP = N_EXPERTS * TL_CONV // N_TOK
assert EXPERTS_PER_CONV_STEP * N_TOK == N_EXPERTS * TL_CONV
S5_Q = 32
S5_GROUPS_PER_STEP = 4
S5_QH = S5_Q * SSM_GROUP_CH
S5_CHUNKS = N_TOK // S5_Q
S5_CHUNKS_PER_SEQ = SEQ // S5_Q
TM_MIX = 1024
ROW_BLOCK = 512
ROW_UNIT = ROW_BLOCK // 2
EXPERT_AHEAD = 4
EXPERT_SLOTS = EXPERT_AHEAD + 1
HALF_TOK = SEQ
N_HALVES = N_TOK // HALF_TOK
N_UNITS = HALF_TOK * TOP_K // ROW_UNIT + N_EXPERTS
N_BLOCKS = (N_UNITS + N_EXPERTS) // 2
N_ROWS = N_UNITS * ROW_UNIT
TM_OUT = 512
SC_CORES = 2
SC_SUBCORES = 16
SC_WORKERS = SC_CORES * SC_SUBCORES
SC_W = 64
SC_CHUNKS_PER_WORKER = HALF_TOK // (SC_WORKERS * SC_W)
VMEM_LIMIT = 48 * 1024 * 1024


def _cparams(sem):
    return pltpu.CompilerParams(dimension_semantics=sem, vmem_limit_bytes=VMEM_LIMIT)


def _pack_rounded_halves(xr):
    n = xr.shape[-1] // 2
    lo = lax.bitcast_convert_type(xr[:, :n], jnp.uint32)
    hi = lax.bitcast_convert_type(xr[:, n:], jnp.uint32)
    return hi | (lo >> 16)


def _pack_halves(x):
    return _pack_rounded_halves(x.astype(BF16).astype(F32))


def _unpack_halves(p):
    lo = lax.bitcast_convert_type(p << 16, F32)
    hi = lax.bitcast_convert_type(p & jnp.uint32(0xFFFF0000), F32)
    return lo, hi


def _rms(x, g):
    return x * lax.rsqrt(jnp.mean(x * x, axis=-1, keepdims=True) + NORM_EPS) * g


def _split_bf16(x):
    hi = x.astype(BF16)
    return hi, (x - hi.astype(F32)).astype(BF16)


def _dot_nt_split(a, b):
    nt = (((1,), (1,)), ((), ()))
    a_hi, a_lo = _split_bf16(a)
    b_hi, b_lo = _split_bf16(b)
    m = a.shape[0]
    both = lax.dot_general(jnp.concatenate([a_hi, a_lo], axis=0), b_hi, nt, preferred_element_type=F32)
    return both[:m] + both[m:] + lax.dot_general(a_hi, b_lo, nt, preferred_element_type=F32)


def _ada_kernel(c_ref, w_ref, b_ref, o_ref):
    c = c_ref[...]
    a_hi, a_lo = _split_bf16(c * jax.nn.sigmoid(c))
    w_hi, w_lo = _split_bf16(w_ref[...])
    both = jnp.dot(jnp.concatenate([a_hi, a_lo], axis=0), w_hi, preferred_element_type=F32)
    o_ref[...] = (both[:SUBLANES] + both[SUBLANES:]
                  + jnp.dot(a_hi, w_lo, preferred_element_type=F32) + b_ref[...])


def _ada(c_pad, w_ada, b_ada):
    n = w_ada.shape[1]
    bn = ADA_COLS
    return pl.pallas_call(
        _ada_kernel,
        grid=(n // bn,),
        in_specs=[pl.BlockSpec((SUBLANES, D_MODEL), lambda j: (0, 0)),
                  pl.BlockSpec((D_MODEL, bn), lambda j: (0, j)),
                  pl.BlockSpec((1, bn), lambda j: (0, j))],
        out_specs=pl.BlockSpec((SUBLANES, bn), lambda j: (0, j)),
        out_shape=jax.ShapeDtypeStruct((SUBLANES, n), F32),
        compiler_params=_cparams(("arbitrary",)),
        name="ada_mod",
    )(c_pad, w_ada, b_ada)


GROUPS_PER_LANE_TILE = LANES // SSM_GROUP_CH


def _to_group_chunks(u, tile_ref, ut_ref):
    n_chunks = u.shape[0] // S5_Q
    for j in range(SSM_CH // LANES):
        tile_ref[j] = u[:, LANES * j:LANES * (j + 1)]
    for j in range(SSM_CH // LANES):
        rows_t = [tile_ref[j, pl.ds(t, n_chunks, stride=S5_Q), :] for t in range(S5_Q)]
        for gg in range(GROUPS_PER_LANE_TILE):
            lo = gg * SSM_GROUP_CH
            row = jnp.concatenate([r[:, lo:lo + SSM_GROUP_CH] for r in rows_t], axis=1)
            ut_ref[j * GROUPS_PER_LANE_TILE + gg] = row.astype(ut_ref.dtype)


def _from_group_chunks(yt_ref, tile_ref):
    n_chunks = yt_ref.shape[1]
    for j in range(SSM_CH // LANES):
        for t in range(S5_Q):
            lo = t * SSM_GROUP_CH
            piece = jnp.concatenate(
                [yt_ref[j * GROUPS_PER_LANE_TILE + gg, :, lo:lo + SSM_GROUP_CH]
                 for gg in range(GROUPS_PER_LANE_TILE)], axis=1)
            tile_ref[j, pl.ds(t, n_chunks, stride=S5_Q), :] = piece
    return jnp.concatenate([tile_ref[j] for j in range(SSM_CH // LANES)], axis=1)


def _inproj_kernel(x_ref, mod_ref, g_ref, w_ref, v_ref, u_ref, ut_ref, tile_ref):
    sh = mod_ref[0, 0:1, :]
    sc = mod_ref[0, 1:2, :]
    sub = TM_IN // IN_SUBTILES
    sub_chunks = sub // S5_Q
    for s in range(IN_SUBTILES):
        r = slice(s * sub, (s + 1) * sub)
        h = _rms(x_ref[r, :], g_ref[...]) * (1.0 + sc) + sh
        z = jnp.dot(h.astype(BF16), w_ref[...], preferred_element_type=F32)
        v_ref[r, :] = z[:, :CONV_CH] * jax.nn.sigmoid(z[:, CONV_CH:2 * CONV_CH])
        u = z[:, 2 * CONV_CH:]
        u_ref[r, :] = u
        _to_group_chunks(u, tile_ref.at[s], ut_ref.at[:, s * sub_chunks:(s + 1) * sub_chunks, :])


def _inproj(x2, mod, g_pre, w_in_bf):
    tiles_per_seq = SEQ // TM_IN
    return pl.pallas_call(
        _inproj_kernel,
        grid=(N_TOK // TM_IN,),
        in_specs=[pl.BlockSpec((TM_IN, D_MODEL), lambda i: (i, 0)),
                  pl.BlockSpec((1, MOD_ROWS, D_MODEL), lambda i: (i // tiles_per_seq, 0, 0)),
                  pl.BlockSpec((1, D_MODEL), lambda i: (0, 0)),
                  pl.BlockSpec((D_MODEL, D_IN), lambda i: (0, 0))],
        out_specs=[pl.BlockSpec((TM_IN, CONV_CH), lambda i: (i, 0)),
                   pl.BlockSpec((TM_IN, SSM_CH), lambda i: (i, 0)),
                   pl.BlockSpec((SSM_GROUPS, TM_IN // S5_Q, S5_QH), lambda i: (0, i, 0))],
        out_shape=[jax.ShapeDtypeStruct((N_TOK, CONV_CH), F32),
                   jax.ShapeDtypeStruct((N_TOK, SSM_CH), F32),
                   jax.ShapeDtypeStruct((SSM_GROUPS, S5_CHUNKS, S5_QH), BF16)],
        scratch_shapes=[pltpu.VMEM((IN_SUBTILES, SSM_CH // LANES, TM_IN // IN_SUBTILES, LANES), F32)],
        compiler_params=_cparams(("parallel",)),
        name="in_proj",
    )(x2, mod, g_pre, w_in_bf)


def _conv_kernel(vc_ref, vp_ref, w_ref, cb_ref, lg_ref, lb_ref, go_ref, wg_ref, wu_ref, wd_ref,
                 o_ref, wgu_o, wd_o, sh_ref):
    for q in range(EXPERTS_PER_CONV_STEP):
        wgu_o[q, :, :D_EXPERT] = wg_ref[q].astype(BF16)
        wgu_o[q, :, D_EXPERT:] = wu_ref[q].astype(BF16)
        wd_o[q] = wd_ref[q].astype(BF16)

    i = pl.program_id(1)
    keep = (i > 0).astype(F32)
    n_ext = TL_CONV + CONV_HALO
    sh_ref[0, 0:CONV_HALO, :] = vp_ref[0] * keep
    sh_ref[0, CONV_HALO:, :] = vc_ref[0]
    ext = sh_ref[0]
    for s in range(1, SUBLANES):
        sh_ref[s] = pltpu.roll(ext, n_ext - s, axis=0)
    off = CONV_HALO - (CONV_WIDTH - 1)
    for r in range(TL_CONV // CONV_ROWS):
        acc = None
        for j in range(CONV_WIDTH):
            s = (off + j) % SUBLANES
            al = r * CONV_ROWS + (off + j) - s
            term = w_ref[j:j + 1, :] * sh_ref[s, al:al + CONV_ROWS, :]
            acc = term if acc is None else acc + term
        y = acc + cb_ref[...]
        mu = jnp.mean(y, axis=-1, keepdims=True)
        d = y - mu
        var = jnp.mean(d * d, axis=-1, keepdims=True)
        yn = d * lax.rsqrt(var + NORM_EPS) * lg_ref[...] + lb_ref[...]
        a = yn * jax.nn.sigmoid(yn)
        o_ref[0, r * CONV_ROWS:(r + 1) * CONV_ROWS, :] = _rms(a, go_ref[...]).astype(BF16)


def _conv(v3, conv_w, conv_b, ln_g, ln_b, g_out, we_gate, we_up, we_down):
    halo_per_tile = TL_CONV // CONV_HALO
    steps_per_seq = SEQ // TL_CONV
    vec = pl.BlockSpec((1, CONV_CH), lambda b, i: (0, 0))
    ex = EXPERTS_PER_CONV_STEP
    w_in = pl.BlockSpec((ex, D_MODEL, D_EXPERT), lambda b, i: (b * steps_per_seq + i, 0, 0))
    return pl.pallas_call(
        _conv_kernel,
        grid=(BATCH, steps_per_seq),
        in_specs=[pl.BlockSpec((1, TL_CONV, CONV_CH), lambda b, i: (b, i, 0)),
                  pl.BlockSpec((1, CONV_HALO, CONV_CH),
                               lambda b, i: (b, jnp.maximum(i * halo_per_tile - 1, 0), 0)),
                  pl.BlockSpec((CONV_WIDTH + 1, CONV_CH), lambda b, i: (0, 0)),
                  vec, vec, vec, vec,
                  w_in, w_in,
                  pl.BlockSpec((ex, D_EXPERT, D_MODEL), lambda b, i: (b * steps_per_seq + i, 0, 0))],
        out_specs=[pl.BlockSpec((1, TL_CONV, CONV_CH), lambda b, i: (b, i, 0)),
                   pl.BlockSpec((ex, D_MODEL, 2 * D_EXPERT), lambda b, i: (b * steps_per_seq + i, 0, 0)),
                   pl.BlockSpec((ex, D_EXPERT, D_MODEL), lambda b, i: (b * steps_per_seq + i, 0, 0))],
        out_shape=[jax.ShapeDtypeStruct((BATCH, SEQ, CONV_CH), BF16),
                   jax.ShapeDtypeStruct((N_EXPERTS, D_MODEL, 2 * D_EXPERT), BF16),
                   jax.ShapeDtypeStruct((N_EXPERTS, D_EXPERT, D_MODEL), BF16)],
        scratch_shapes=[pltpu.VMEM((SUBLANES, TL_CONV + CONV_HALO, CONV_CH), F32)],
        compiler_params=_cparams(("parallel", "arbitrary")),
        name="conv_module",
    )(v3, v3, conv_w, conv_b, ln_g, ln_b, g_out, we_gate, we_up, we_down)


S5_GROUP_ROWS = S5_CHUNKS + SUBLANES


S5_POW_ROWS = (S5_Q + 1 + SUBLANES - 1) // SUBLANES * SUBLANES
(S5_BB_RI, S5_BB_NIR, S5_BB_IR, S5_BB_RNI, S5_CC_RI, S5_CC_NIR, S5_N_PARAM) = range(7)


def _s5_kernel(ut_ref, pwr_ref, pwi_ref, par_ref, a_ref, bq_ref, bs_ref, yt_ref, sin_s, sp_s):
    phase = pl.program_id(0)
    g = pl.program_id(1)
    q = S5_Q
    n = 2 * SSM_STATE

    def group_row0(k):
        return pl.multiple_of((g * S5_GROUPS_PER_STEP + k) * S5_GROUP_ROWS, SUBLANES)

    def lam_pow(k, j):
        return pwr_ref[k, j:j + 1, :], pwi_ref[k, j:j + 1, :]

    @pl.when(phase == 0)
    def _():
        for k in range(S5_GROUPS_PER_STEP):
            bb_ri, bb_nir = par_ref[k, S5_BB_RI], par_ref[k, S5_BB_NIR]
            bb_ir, bb_rni = par_ref[k, S5_BB_IR], par_ref[k, S5_BB_RNI]
            blk_q, blk_s = [], []
            for t in range(q):
                pr, pi_ = lam_pow(k, q - 1 - t)
                blk_q.append(pr * bb_ri + pi_ * bb_nir)
                blk_s.append(pr * bb_ir + pi_ * bb_rni)
            wst = jnp.concatenate([jnp.concatenate(blk_q, axis=0), jnp.concatenate(blk_s, axis=0)], axis=1)
            r = jnp.dot(ut_ref[k], wst.astype(BF16), preferred_element_type=F32)
            sin_s[0, pl.ds(group_row0(k), S5_CHUNKS), :] = r[:, :n]
            sin_s[1, pl.ds(group_row0(k), S5_CHUNKS), :] = r[:, n:]

    @pl.when((phase == 1) & (g == 0))
    def _():
        a = a_ref[...]
        bq = bq_ref[...]
        bs = bs_ref[...]

        def body(c, carry):
            nxt = []
            for b in range(BATCH):
                x, xs = carry[b]
                rows = pl.ds(b * S5_CHUNKS_PER_SEQ + c, SSM_GROUPS, stride=S5_GROUP_ROWS)
                sp_s[rows, :] = x
                nxt.append((a * x + bq * xs + sin_s[0, rows, :], a * xs + bs * x + sin_s[1, rows, :]))
            return tuple(nxt)

        z = jnp.zeros((SSM_GROUPS, n), F32)
        lax.fori_loop(0, S5_CHUNKS_PER_SEQ, body, tuple((z, z) for _ in range(BATCH)))

    @pl.when(phase == 1)
    def _():
        for k in range(S5_GROUPS_PER_STEP):
            cc_ri, cc_nir = par_ref[k, S5_CC_RI], par_ref[k, S5_CC_NIR]
            cl = []
            for j in range(q + 1):
                pr, pi_ = lam_pow(k, j)
                cl.append(pr * cc_ri + pi_ * cc_nir)
            cl_lo = jnp.concatenate(cl[:q], axis=0)
            cl_hi = jnp.concatenate(cl[1:], axis=0)
            lane = lax.broadcasted_iota(jnp.int32, (1, n), 1)
            vgt = (cl_hi * jnp.where(lane < SSM_STATE, 1.0, -1.0)).astype(BF16)
            kt = _dot_nt_split(par_ref[k, S5_BB_RNI], cl_lo)
            padded = jnp.concatenate([jnp.zeros_like(kt), kt], axis=1)
            tg = jnp.concatenate(
                [padded[:, (q - t) * SSM_GROUP_CH:(q - t) * SSM_GROUP_CH + S5_QH] for t in range(q)],
                axis=0).astype(BF16)
            sp = sp_s[pl.ds(group_row0(k), S5_CHUNKS), :]
            y = jnp.dot(ut_ref[k], tg, preferred_element_type=F32)
            yt_ref[k] = y + lax.dot_general(sp.astype(BF16), vgt, (((1,), (1,)), ((), ())),
                                            preferred_element_type=F32)


def _s5(ut, pwr, pwi, params, a_cat, b_q, b_s):
    vec = pl.BlockSpec((SSM_GROUPS, 2 * SSM_STATE), lambda p, g: (0, 0))
    gs = S5_GROUPS_PER_STEP
    powers = pl.BlockSpec((gs, S5_POW_ROWS, 2 * SSM_STATE), lambda p, g: (g, 0, 0))
    return pl.pallas_call(
        _s5_kernel,
        grid=(2, SSM_GROUPS // gs),
        in_specs=[pl.BlockSpec((gs, S5_CHUNKS, S5_QH), lambda p, g: (g, 0, 0)),
                  powers, powers,
                  pl.BlockSpec((gs, S5_N_PARAM, SSM_GROUP_CH, 2 * SSM_STATE), lambda p, g: (g, 0, 0, 0)),
                  vec, vec, vec],
        out_specs=pl.BlockSpec((gs, S5_CHUNKS, S5_QH), lambda p, g: (g * p, 0, 0)),
        out_shape=jax.ShapeDtypeStruct((SSM_GROUPS, S5_CHUNKS, S5_QH), F32),
        scratch_shapes=[pltpu.VMEM((2, SSM_GROUPS * S5_GROUP_ROWS, 2 * SSM_STATE), F32),
                        pltpu.VMEM((SSM_GROUPS * S5_GROUP_ROWS, 2 * SSM_STATE), F32)],
        compiler_params=_cparams(("arbitrary", "arbitrary")),
        name="s5_chunked",
    )(ut, pwr, pwi, params, a_cat, b_q, b_s)


def _s5_operators(a_re, a_im, log_dt, b_re, b_im, c_re, c_im):
    q = S5_Q
    dt = jnp.exp(log_dt)[:, None]
    ar, ai = a_re, a_im
    mag = jnp.exp(ar * dt)
    lr = mag * jnp.cos(ai * dt)
    li = mag * jnp.sin(ai * dt)
    den = ar * ar + ai * ai
    nr = lr - 1.0
    kr = (nr * ar + li * ai) / den
    ki = (li * ar - nr * ai) / den
    bbr = kr[..., None] * b_re - ki[..., None] * b_im
    bbi = kr[..., None] * b_im + ki[..., None] * b_re
    j = jnp.arange(q + 1, dtype=F32)[None, :, None]
    pmag = jnp.exp(ar[:, None, :] * dt[:, :, None] * j)
    pang = ai[:, None, :] * dt[:, :, None] * j
    pr = pmag * jnp.cos(pang)
    pi_ = pmag * jnp.sin(pang)
    pad = ((0, 0), (0, S5_POW_ROWS - (q + 1)), (0, 0))
    pwr = jnp.pad(jnp.concatenate([pr, pr], axis=-1), pad)
    pwi = jnp.pad(jnp.concatenate([pi_, pi_], axis=-1), pad)
    br_t = bbr.transpose(0, 2, 1)
    bi_t = bbi.transpose(0, 2, 1)
    cat = lambda a, b: jnp.concatenate([a, b], axis=-1)
    stack = [None] * S5_N_PARAM
    stack[S5_BB_RI] = cat(br_t, bi_t)
    stack[S5_BB_NIR] = cat(-bi_t, br_t)
    stack[S5_BB_IR] = cat(bi_t, br_t)
    stack[S5_BB_RNI] = cat(br_t, -bi_t)
    stack[S5_CC_RI] = cat(c_re, c_im)
    stack[S5_CC_NIR] = cat(-c_im, c_re)
    params = jnp.stack(stack, axis=1)
    aq_r, aq_i = pr[:, q], pi_[:, q]
    a_cat = cat(aq_r, aq_r)
    b_q = cat(-aq_i, aq_i)
    b_s = cat(aq_i, -aq_i)
    return pwr, pwi, params, a_cat, b_q, b_s


def _gelu_tanh(x):
    return 0.5 * x * (1.0 + jnp.tanh(math.sqrt(2.0 / math.pi) * (x + 0.044715 * (x * x * x))))


def _mix_kernel(an_ref, yt_ref, u_ref, x_ref, mod_ref, d_ref, wglu_ref, bglu_ref, gos_ref,
                woa_ref, wob_ref, gpm_ref, gpf_ref, wr_ref, br_ref, tri_ref,
                x1_ref, h2_ref, eidx_ref, rank_ref, gw_ref, cnt_ref, run_ref, tile_ref):
    i = pl.program_id(0)
    tm = TM_MIX

    @pl.when(i == 0)
    def _():
        run_ref[...] = jnp.zeros_like(run_ref)

    gt_m = mod_ref[0, 2:3, :]
    sh_f = mod_ref[0, 3:4, :]
    sc_f = mod_ref[0, 4:5, :]

    yy = _from_group_chunks(yt_ref, tile_ref) + d_ref[...] * u_ref[...]
    g = _gelu_tanh(yy)
    gl = jnp.dot(g.astype(BF16), wglu_ref[...], preferred_element_type=F32) + bglu_ref[...]
    ob = g * jax.nn.sigmoid(gl)
    bn = _rms(ob, gos_ref[...]).astype(BF16)
    o = (jnp.dot(an_ref[...], woa_ref[...], preferred_element_type=F32)
         + jnp.dot(bn, wob_ref[...], preferred_element_type=F32))
    x1 = x_ref[...] + gt_m * _rms(o, gpm_ref[...])
    x1_ref[...] = x1
    h2 = _rms(x1, gpf_ref[...]) * (1.0 + sc_f) + sh_f
    h2_hi = h2.astype(BF16)
    h2_hi32 = h2_hi.astype(F32)
    h2_ref[...] = _pack_rounded_halves(h2_hi32)

    h2_lo = (h2 - h2_hi32).astype(BF16)
    nt = (((1,), (1,)), ((), ()))
    both = lax.dot_general(wr_ref[...], h2_hi, nt, preferred_element_type=F32)
    logits = (both[:N_EXPERTS] + both[N_EXPERTS:]
              + lax.dot_general(wr_ref[:N_EXPERTS, :], h2_lo, nt, preferred_element_type=F32))
    scores = jax.nn.sigmoid(logits)
    biased = scores + br_ref[...]
    ng = N_ROUTE_GROUPS
    gsz = N_EXPERTS // ng
    b3 = biased.reshape(ng, gsz, tm)
    s3 = scores.reshape(ng, gsz, tm)
    sub = lax.broadcasted_iota(jnp.int32, (ng, gsz, tm), 1).astype(F32)
    grp = lax.broadcasted_iota(jnp.int32, (ng, gsz, tm), 0).astype(F32)
    eid = grp * gsz + sub
    neg = -jnp.inf
    m1 = jnp.max(b3, axis=1, keepdims=True)
    i1 = jnp.min(jnp.where(b3 == m1, sub, float(gsz)), axis=1, keepdims=True)
    m2 = jnp.max(jnp.where(sub == i1, neg, b3), axis=1, keepdims=True)
    gs = m1 + m2
    gi = lax.broadcasted_iota(jnp.int32, (ng, 1, tm), 0)
    beaten = jnp.zeros((ng, 1, tm), F32)
    for gp in range(ng):
        o_ = gs[gp:gp + 1]
        beats = (o_ > gs) | ((o_ == gs) & (gi > gp))
        beaten = beaten + beats.astype(F32)
    gmask = beaten < float(TOPK_ROUTE_GROUPS)
    masked = jnp.where(gmask, b3, neg)

    sels = []
    picked = jnp.zeros((ng, gsz, tm), F32)
    for k in range(TOP_K):
        m = jnp.max(jnp.max(masked, axis=0, keepdims=True), axis=1, keepdims=True)
        cand = jnp.where(masked == m, eid, float(N_EXPERTS))
        sel = jnp.min(jnp.min(cand, axis=0, keepdims=True), axis=1, keepdims=True)
        oh = eid == sel
        masked = jnp.where(oh, neg, masked)
        picked = jnp.where(oh, 1.0, picked)
        sels.append(sel)

    pm = picked.reshape(N_EXPERTS, tm)
    prefix = jnp.dot(pm.astype(BF16), tri_ref[...], preferred_element_type=F32) + run_ref[:, 0:1]
    p3 = prefix.reshape(ng, gsz, tm)
    run_new = run_ref[...] + jnp.sum(pm, axis=1, keepdims=True)
    run_ref[...] = run_new
    cnt_ref[...] = run_new

    sc_rows = []
    for k in range(TOP_K):
        oh = eid == sels[k]
        sc_k = jnp.sum(jnp.sum(jnp.where(oh, s3, 0.0), axis=0, keepdims=True), axis=1, keepdims=True)
        rk_k = jnp.sum(jnp.sum(jnp.where(oh, p3, 0.0), axis=0, keepdims=True), axis=1, keepdims=True)
        sc_rows.append(sc_k)
        eidx_ref[k:k + 1, :] = sels[k].reshape(1, tm).astype(jnp.int32)
        rank_ref[k:k + 1, :] = rk_k.reshape(1, tm).astype(jnp.int32)
    tot = sc_rows[0]
    for k in range(1, TOP_K):
        tot = tot + sc_rows[k]
    inv = ROUTED_SCALE / (tot + 1e-20)
    for k in range(TOP_K):
        gw_ref[k:k + 1, :] = (sc_rows[k] * inv).reshape(1, tm)


def _mix(half, a_n, yt, u2, x2, mod, d_skip, wglu_bf, b_glu, g_out_ssm, wo_a, wo_b, g_post_mix,
         g_pre_ffn, w_router_split, b_router_col, tri):
    tm = TM_MIX
    t0 = half * (HALF_TOK // tm)
    row = lambda n: pl.BlockSpec((1, n), lambda i: (0, 0))
    full = lambda a, b: pl.BlockSpec((a, b), lambda i: (0, 0))
    tok_in = lambda n: pl.BlockSpec((tm, n), lambda i: (t0 + i, 0))
    tok = lambda n: pl.BlockSpec((tm, n), lambda i: (i, 0))
    col = pl.BlockSpec((TOP_K, tm), lambda i: (0, i))
    return pl.pallas_call(
        _mix_kernel,
        grid=(HALF_TOK // tm,),
        in_specs=[tok_in(CONV_CH),
                  pl.BlockSpec((SSM_GROUPS, tm // S5_Q, S5_QH), lambda i: (0, t0 + i, 0)),
                  tok_in(SSM_CH), tok_in(D_MODEL),
                  pl.BlockSpec((1, MOD_ROWS, D_MODEL), lambda i: (half, 0, 0)),
                  row(SSM_CH), full(SSM_CH, SSM_CH), row(SSM_CH), row(SSM_CH),
                  full(CONV_CH, D_MODEL), full(SSM_CH, D_MODEL), row(D_MODEL), row(D_MODEL),
                  full(2 * N_EXPERTS, D_MODEL), full(N_EXPERTS, 1), full(tm, tm)],
        out_specs=[tok(D_MODEL), tok(D_MODEL // 2), col, col, col,
                   pl.BlockSpec((N_EXPERTS, LANES), lambda i: (0, 0))],
        out_shape=[jax.ShapeDtypeStruct((HALF_TOK, D_MODEL), F32),
                   jax.ShapeDtypeStruct((HALF_TOK, D_MODEL // 2), jnp.uint32),
                   jax.ShapeDtypeStruct((TOP_K, HALF_TOK), jnp.int32),
                   jax.ShapeDtypeStruct((TOP_K, HALF_TOK), jnp.int32),
                   jax.ShapeDtypeStruct((TOP_K, HALF_TOK), F32),
                   jax.ShapeDtypeStruct((N_EXPERTS, LANES), F32)],
        scratch_shapes=[pltpu.VMEM((N_EXPERTS, LANES), F32),
                        pltpu.VMEM((SSM_CH // LANES, tm, LANES), F32)],
        compiler_params=_cparams(("arbitrary",)),
        name="mix_out_router",
    )(a_n, yt, u2, x2, mod, d_skip, wglu_bf, b_glu, g_out_ssm, wo_a, wo_b, g_post_mix,
      g_pre_ffn, w_router_split, b_router_col, tri)


def _expert_kernel(blk0_ref, nblk_ref, bunit_ref, bfull_ref, xs_hbm, wgu_ref, wd_ref, ys_hbm,
                   xbuf, ybuf, sem_in, sem_out):
    e = pl.program_id(0)
    n = nblk_ref[e]
    b0 = blk0_ref[e]
    n_all = blk0_ref[N_EXPERTS - 1] + nblk_ref[N_EXPERTS - 1]

    def copies(b, slot, rows):
        hbm = pl.ds(pl.multiple_of(bunit_ref[b] * ROW_UNIT, ROW_UNIT), rows)
        buf = pl.ds(0, rows)
        return (pltpu.make_async_copy(xs_hbm.at[hbm], xbuf.at[slot, buf], sem_in.at[slot]),
                pltpu.make_async_copy(ybuf.at[slot, buf], ys_hbm.at[hbm], sem_out.at[slot]))

    def for_size(b, fn):
        full = bfull_ref[b] == 1

        @pl.when(full)
        def _():
            fn(ROW_BLOCK)

        @pl.when(jnp.logical_not(full))
        def _():
            fn(ROW_UNIT)

    for b in range(EXPERT_AHEAD):
        @pl.when((e == 0) & (b < n_all))
        def _():
            for_size(b, lambda rows: copies(b, b, rows)[0].start())

    def compute(slot, rows):
        x_lo, x_hi = _unpack_halves(xbuf[slot, pl.ds(0, rows)])
        x = jnp.concatenate([x_lo.astype(BF16), x_hi.astype(BF16)], axis=1)
        h = jnp.dot(x, wgu_ref[0], preferred_element_type=F32)
        hg = h[:, :D_EXPERT]
        act = hg * jax.nn.sigmoid(hg) * h[:, D_EXPERT:]
        ybuf[slot, pl.ds(0, rows)] = _pack_halves(
            jnp.dot(act.astype(BF16), wd_ref[0], preferred_element_type=F32))

    def block(b, carry):
        slot = b % EXPERT_SLOTS
        ahead = b + EXPERT_AHEAD

        @pl.when(ahead < n_all)
        def _():
            for_size(ahead, lambda rows: copies(ahead, ahead % EXPERT_SLOTS, rows)[0].start())

        @pl.when(b >= EXPERT_SLOTS)
        def _():
            for_size(b - EXPERT_SLOTS, lambda rows: copies(b - EXPERT_SLOTS, slot, rows)[1].wait())

        def work(rows):
            cp_in, cp_out = copies(b, slot, rows)
            cp_in.wait()
            compute(slot, rows)
            cp_out.start()

        for_size(b, work)
        return carry

    lax.fori_loop(b0, b0 + n, block, 0)

    @pl.when(e == N_EXPERTS - 1)
    def _():
        for j in range(1, EXPERT_SLOTS + 1):
            @pl.when(n_all >= j)
            def _():
                last = n_all - j
                for_size(last, lambda rows: copies(last, last % EXPERT_SLOTS, rows)[1].wait())


def _experts(blk0, nblk, bunit, bfull, xs, we_gu, we_d):
    any_spec = pl.BlockSpec(memory_space=pl.ANY)
    grid_spec = pltpu.PrefetchScalarGridSpec(
        num_scalar_prefetch=4,
        grid=(N_EXPERTS,),
        in_specs=[any_spec,
                  pl.BlockSpec((1, D_MODEL, 2 * D_EXPERT), lambda e, *_: (e, 0, 0)),
                  pl.BlockSpec((1, D_EXPERT, D_MODEL), lambda e, *_: (e, 0, 0))],
        out_specs=any_spec,
        scratch_shapes=[pltpu.VMEM((EXPERT_SLOTS, ROW_BLOCK, D_MODEL // 2), jnp.uint32),
                        pltpu.VMEM((EXPERT_SLOTS, ROW_BLOCK, D_MODEL // 2), jnp.uint32),
                        pltpu.SemaphoreType.DMA((EXPERT_SLOTS,)),
                        pltpu.SemaphoreType.DMA((EXPERT_SLOTS,))],
    )
    return pl.pallas_call(
        _expert_kernel,
        grid_spec=grid_spec,
        out_shape=jax.ShapeDtypeStruct((N_ROWS, D_MODEL // 2), jnp.uint32),
        compiler_params=_cparams(("arbitrary",)),
        name="routed_experts",
    )(blk0, nblk, bunit, bfull, xs, we_gu, we_d)


def _final_kernel(h2_ref, yg_ref, gw_ref, x1_ref, mod_ref, wgu_ref, wd_ref, g_ref, *rest):
    o_ref = rest[-1]
    half = D_MODEL // 2
    gt_f = mod_ref[0, 5:6, :]
    x_lo, x_hi = _unpack_halves(h2_ref[...])
    h = (jnp.dot(x_lo.astype(BF16), wgu_ref[:half, :], preferred_element_type=F32)
         + jnp.dot(x_hi.astype(BF16), wgu_ref[half:, :], preferred_element_type=F32))
    hg = h[:, :D_EXPERT]
    act = hg * jax.nn.sigmoid(hg) * h[:, D_EXPERT:]
    shared = jnp.dot(act.astype(BF16), wd_ref[...], preferred_element_type=F32)
    y_lo = shared[:, :half]
    y_hi = shared[:, half:]
    for k in range(TOP_K):
        r_lo, r_hi = _unpack_halves(yg_ref[k])
        w = gw_ref[:, k:k + 1]
        y_lo = y_lo + w * r_lo
        y_hi = y_hi + w * r_hi
    ms = (jnp.sum(y_lo * y_lo, axis=-1, keepdims=True)
          + jnp.sum(y_hi * y_hi, axis=-1, keepdims=True)) * (1.0 / D_MODEL)
    inv = lax.rsqrt(ms + NORM_EPS)
    o_ref[:, :half] = x1_ref[:, :half] + gt_f[:, :half] * (y_lo * inv * g_ref[:, :half])
    o_ref[:, half:] = x1_ref[:, half:] + gt_f[:, half:] * (y_hi * inv * g_ref[:, half:])


def _final(half, out_prev, h2p, yg, gw_t, x1, mod, ws_gu, ws_d, g_post_ffn):
    tm = TM_OUT
    t0 = half * (HALF_TOK // tm)
    tok = pl.BlockSpec((tm, D_MODEL), lambda i: (i, 0))
    in_specs = [pl.BlockSpec((tm, D_MODEL // 2), lambda i: (i, 0)),
                pl.BlockSpec((TOP_K, tm, D_MODEL // 2), lambda i: (0, i, 0)),
                pl.BlockSpec((tm, TOP_K), lambda i: (i, 0)),
                tok,
                pl.BlockSpec((1, MOD_ROWS, D_MODEL), lambda i: (half, 0, 0)),
                pl.BlockSpec((D_MODEL, 2 * D_EXPERT), lambda i: (0, 0)),
                pl.BlockSpec((D_EXPERT, D_MODEL), lambda i: (0, 0)),
                pl.BlockSpec((1, D_MODEL), lambda i: (0, 0))]
    args = [h2p, yg, gw_t, x1, mod, ws_gu, ws_d, g_post_ffn]
    aliases = {}
    if out_prev is not None:
        aliases = {len(args): 0}
        in_specs.append(pl.BlockSpec(memory_space=pl.ANY))
        args.append(out_prev)
    return pl.pallas_call(
        _final_kernel,
        grid=(HALF_TOK // tm,),
        in_specs=in_specs,
        out_specs=pl.BlockSpec((tm, D_MODEL), lambda i: (t0 + i, 0)),
        out_shape=jax.ShapeDtypeStruct((N_TOK, D_MODEL), F32),
        input_output_aliases=aliases,
        compiler_params=_cparams(("parallel",)),
        name="shared_final",
    )(*args)


def _sc_worker_id():
    return lax.axis_index("s") * SC_CORES + lax.axis_index("c")


def _dispatch_body(h_hbm, dest_hbm, xs_hbm, idx_v, rows_v, sem_l, sem_s):
    n = SC_CHUNKS_PER_WORKER
    c0 = _sc_worker_id() * n

    def load(i, b):
        return pltpu.async_copy(h_hbm.at[pl.ds((c0 + i) * SC_W, SC_W)], rows_v.at[b], sem_l.at[b])

    loads = [None] * n
    scat = [None] * n
    loads[0] = load(0, 0)
    for i in range(n):
        b = i % 2
        pltpu.sync_copy(dest_hbm.at[c0 + i], idx_v.at[b])
        loads[i].wait()
        if i + 1 < n:
            if i >= 1:
                for d in scat[i - 1]:
                    d.wait()
            loads[i + 1] = load(i + 1, 1 - b)
        scat[i] = [pltpu.async_copy(rows_v.at[b], xs_hbm.at[idx_v.at[b].at[k]], sem_s.at[b])
                   for k in range(TOP_K)]
    for i in (n - 2, n - 1):
        for d in scat[i]:
            d.wait()


def _sc_dispatch(h2p, dest3):
    mesh = plsc.VectorSubcoreMesh(core_axis_name="c", subcore_axis_name="s")
    return pl.kernel(
        _dispatch_body, mesh=mesh,
        out_type=jax.ShapeDtypeStruct((N_ROWS, D_MODEL // 2), jnp.uint32),
        scratch_types=[pltpu.VMEM((2, TOP_K, SC_W), jnp.int32),
                       pltpu.VMEM((2, SC_W, D_MODEL // 2), jnp.uint32),
                       pltpu.SemaphoreType.DMA((2,)), pltpu.SemaphoreType.DMA((2,))],
    )(h2p, dest3)


def _combine_body(ys_hbm, dest_hbm, yg_hbm, idx_v, rows_v, sem_g, sem_w):
    c0 = _sc_worker_id() * SC_CHUNKS_PER_WORKER

    @pl.loop(0, SC_CHUNKS_PER_WORKER)
    def _(i):
        c = c0 + i
        pltpu.sync_copy(dest_hbm.at[c], idx_v)
        g = [None] * TOP_K
        w = [None] * TOP_K
        g[0] = pltpu.async_copy(ys_hbm.at[idx_v.at[0]], rows_v.at[0], sem_g.at[0])
        for k in range(TOP_K):
            b = k % 2
            g[k].wait()
            if k + 1 < TOP_K:
                if k >= 1:
                    w[k - 1].wait()
                g[k + 1] = pltpu.async_copy(ys_hbm.at[idx_v.at[k + 1]], rows_v.at[1 - b], sem_g.at[1 - b])
            w[k] = pltpu.async_copy(rows_v.at[b], yg_hbm.at[k].at[pl.ds(c * SC_W, SC_W)], sem_w.at[b])
        w[TOP_K - 2].wait()
        w[TOP_K - 1].wait()


def _sc_combine(ysp, dest3):
    mesh = plsc.VectorSubcoreMesh(core_axis_name="c", subcore_axis_name="s")
    return pl.kernel(
        _combine_body, mesh=mesh,
        out_type=jax.ShapeDtypeStruct((TOP_K, HALF_TOK, D_MODEL // 2), jnp.uint32),
        scratch_types=[pltpu.VMEM((TOP_K, SC_W), jnp.int32),
                       pltpu.VMEM((2, SC_W, D_MODEL // 2), jnp.uint32),
                       pltpu.SemaphoreType.DMA((2,)), pltpu.SemaphoreType.DMA((2,))],
    )(ysp, dest3)


def kernel(x, c, w_ada, b_ada, g_pre_mix, g_post_mix, w_in, conv_w, conv_b, conv_ln_g, conv_ln_b,
           ssm_a_re, ssm_a_im, ssm_log_dt, ssm_b_re, ssm_b_im, ssm_c_re, ssm_c_im, ssm_d,
           ssm_w_glu, ssm_b_glu, g_out_conv, g_out_ssm, w_out, g_pre_ffn, g_post_ffn,
           w_router, b_router, we_gate, we_up, we_down, ws_gate, ws_up, ws_down):
    l = 0
    x2 = x.reshape(N_TOK, D_MODEL)
    r1 = lambda a: a.reshape(1, -1)

    c_pad = jnp.zeros((SUBLANES, D_MODEL), F32).at[:BATCH].set(c)
    mod = _ada(c_pad, w_ada[l], r1(b_ada[l]))[:BATCH].reshape(BATCH, N_MOD, D_MODEL)
    mod = jnp.concatenate([mod, jnp.zeros((BATCH, MOD_ROWS - N_MOD, D_MODEL), F32)], axis=1)

    v, u, ut = _inproj(x2, mod, r1(g_pre_mix[l]), w_in[l].astype(BF16))
    cw = jnp.concatenate([conv_w[l].reshape(CONV_WIDTH, CONV_CH), jnp.zeros((1, CONV_CH), F32)], axis=0)
    a_n, we_gu, we_d = _conv(v.reshape(BATCH, SEQ, CONV_CH), cw, r1(conv_b[l]), r1(conv_ln_g[l]),
                             r1(conv_ln_b[l]), r1(g_out_conv[l]), we_gate[l], we_up[l], we_down[l])
    a_n = a_n.reshape(N_TOK, CONV_CH)

    pwr, pwi, s5_params, a_cat, b_q, b_s = _s5_operators(
        ssm_a_re[l], ssm_a_im[l], ssm_log_dt[l], ssm_b_re[l], ssm_b_im[l], ssm_c_re[l], ssm_c_im[l])
    yt = _s5(ut, pwr, pwi, s5_params, a_cat, b_q, b_s)

    tm = TM_MIX
    tri = (jnp.arange(tm)[:, None] < jnp.arange(tm)[None, :]).astype(BF16)
    wo = w_out[l].astype(BF16)
    wr_t = w_router[l].T
    wr_hi = wr_t.astype(BF16)
    wr_split = jnp.concatenate([wr_hi, (wr_t - wr_hi.astype(F32)).astype(BF16)], axis=0)
    mix_params = (r1(ssm_d[l]), ssm_w_glu[l].astype(BF16), r1(ssm_b_glu[l]), r1(g_out_ssm[l]),
                  wo[:CONV_CH], wo[CONV_CH:], r1(g_post_mix[l]), r1(g_pre_ffn[l]),
                  wr_split, b_router[l].reshape(N_EXPERTS, 1), tri)
    ws_gu = jnp.concatenate([ws_gate[l], ws_up[l]], axis=1).astype(BF16)
    ws_d = ws_down[l].astype(BF16)
    e_ids = jnp.arange(N_EXPERTS, dtype=jnp.int32)

    out = None
    for half in range(N_HALVES):
        x1, h2, eidx, rank, gw, cnt = _mix(half, a_n, yt, u, x2, mod, *mix_params)
        counts = cnt[:, 0].astype(jnp.int32)
        units = (counts + ROW_UNIT - 1) // ROW_UNIT
        unit0 = jnp.cumsum(units) - units
        nblk = (units + 1) // 2
        blk0 = jnp.cumsum(nblk) - nblk
        dest = rank + jnp.sum(jnp.where(eidx[..., None] == e_ids, unit0 * ROW_UNIT, 0), axis=-1)
        dest3 = dest.reshape(TOP_K, HALF_TOK // SC_W, SC_W).transpose(1, 0, 2)
        b_ids = jnp.arange(N_BLOCKS, dtype=jnp.int32)
        owner = (b_ids[:, None] >= blk0[None, :]) & (b_ids[:, None] < (blk0 + nblk)[None, :])
        pick = lambda v: jnp.sum(jnp.where(owner, v[None, :], 0), axis=1)
        local = b_ids - pick(blk0)
        bunit = pick(unit0) + 2 * local
        bfull = (2 * local + 2 <= pick(units)).astype(jnp.int32)

        xs = _sc_dispatch(h2, dest3)
        ys = _experts(blk0, nblk, bunit, bfull, xs, we_gu, we_d)
        yg = _sc_combine(ys, dest3)
        out = _final(half, out, h2, yg, gw.T, x1, mod, ws_gu, ws_d, r1(g_post_ffn[l]))
    return out.reshape(BATCH, SEQ, D_MODEL)
```

```python
import math

import jax
import jax.numpy as jnp
from jax import lax
from jax.experimental import pallas as pl
from jax.experimental.pallas import tpu as pltpu
from jax.experimental.pallas import tpu_sc as plsc

F32 = jnp.float32
BF16 = jnp.bfloat16

D_MODEL = 1024
BATCH = 2
SEQ = 8192
N_TOK = BATCH * SEQ
CONV_CH = 512
CONV_WIDTH = 31
SSM_CH = 512
SSM_GROUP_CH = 16
SSM_GROUPS = 32
SSM_STATE = 64
D_IN = 2 * CONV_CH + SSM_CH
N_EXPERTS = 64
TOP_K = 8
N_ROUTE_GROUPS = 8
TOPK_ROUTE_GROUPS = 4
D_EXPERT = 256
ROUTED_SCALE = 2.5
NORM_EPS = 1e-6

SUBLANES = 8
LANES = 128

N_MOD = 6
MOD_ROWS = SUBLANES
ADA_COLS = 1536
TM_IN = 1024
IN_SUBTILES = 4
TL_CONV = 512
CONV_HALO = 32
CONV_ROWS = 256
EXPERTS_PER_CONV_STEP = N_EXPERTS * TL_CONV // N_TOK
assert EXPERTS_PER_CONV_STEP * N_TOK == N_EXPERTS * TL_CONV
S5_Q = 32
S5_GROUPS_PER_STEP = 4
S5_QH = S5_Q * SSM_GROUP_CH
S5_CHUNKS = N_TOK // S5_Q
S5_CHUNKS_PER_SEQ = SEQ // S5_Q
TM_MIX = 1024
ROW_BLOCK = 512
ROW_UNIT = ROW_BLOCK // 2
EXPERT_AHEAD = 4
EXPERT_SLOTS = EXPERT_AHEAD + 1
HALF_TOK = SEQ
N_HALVES = N_TOK // HALF_TOK
N_UNITS = HALF_TOK * TOP_K // ROW_UNIT + N_EXPERTS
N_BLOCKS = (N_UNITS + N_EXPERTS) // 2
N_ROWS = N_UNITS * ROW_UNIT
TM_OUT = 512
FINAL_ROW_BUFFERS = 3
SC_CORES = 2
SC_SUBCORES = 16
SC_WORKERS = SC_CORES * SC_SUBCORES
SC_W = 64
SC_CHUNKS_PER_WORKER = HALF_TOK // (SC_WORKERS * SC_W)
VMEM_LIMIT = 48 * 1024 * 1024


def _cparams(sem):
    return pltpu.CompilerParams(dimension_semantics=sem, vmem_limit_bytes=VMEM_LIMIT)


def _pack_rounded_halves(xr):
    n = xr.shape[-1] // 2
    lo = lax.bitcast_convert_type(xr[:, :n], jnp.uint32)
    hi = lax.bitcast_convert_type(xr[:, n:], jnp.uint32)
    return hi | (lo >> 16)


def _pack_halves(x):
    return _pack_rounded_halves(x.astype(BF16).astype(F32))


def _unpack_halves(p):
    lo = lax.bitcast_convert_type(p << 16, F32)
    hi = lax.bitcast_convert_type(p & jnp.uint32(0xFFFF0000), F32)
    return lo, hi


def _rms(x, g):
    return x * lax.rsqrt(jnp.mean(x * x, axis=-1, keepdims=True) + NORM_EPS) * g


def _split_bf16(x):
    hi = x.astype(BF16)
    return hi, (x - hi.astype(F32)).astype(BF16)


def _dot_nt_split(a, b):
    nt = (((1,), (1,)), ((), ()))
    a_hi, a_lo = _split_bf16(a)
    b_hi, b_lo = _split_bf16(b)
    m = a.shape[0]
    both = lax.dot_general(jnp.concatenate([a_hi, a_lo], axis=0), b_hi, nt, preferred_element_type=F32)
    return both[:m] + both[m:] + lax.dot_general(a_hi, b_lo, nt, preferred_element_type=F32)


def _ada_kernel(c_ref, w_ref, b_ref, o_ref):
    c = c_ref[...]
    a = c * jax.nn.sigmoid(c)
    o_ref[...] = jnp.dot(a, w_ref[...], preferred_element_type=F32,
                         precision=lax.Precision.HIGHEST) + b_ref[...]


def _ada(c_pad, w_ada, b_ada):
    n = w_ada.shape[1]
    bn = ADA_COLS
    return pl.pallas_call(
        _ada_kernel,
        grid=(n // bn,),
        in_specs=[pl.BlockSpec((SUBLANES, D_MODEL), lambda j: (0, 0)),
                  pl.BlockSpec((D_MODEL, bn), lambda j: (0, j)),
                  pl.BlockSpec((1, bn), lambda j: (0, j))],
        out_specs=pl.BlockSpec((SUBLANES, bn), lambda j: (0, j)),
        out_shape=jax.ShapeDtypeStruct((SUBLANES, n), F32),
        compiler_params=_cparams(("arbitrary",)),
        name="ada_mod",
    )(c_pad, w_ada, b_ada)


GROUPS_PER_LANE_TILE = LANES // SSM_GROUP_CH


def _to_group_chunks(u, tile_ref, ut_ref):
    n_chunks = u.shape[0] // S5_Q
    for j in range(SSM_CH // LANES):
        tile_ref[j] = u[:, LANES * j:LANES * (j + 1)]
    for j in range(SSM_CH // LANES):
        rows_t = [tile_ref[j, pl.ds(t, n_chunks, stride=S5_Q), :] for t in range(S5_Q)]
        for gg in range(GROUPS_PER_LANE_TILE):
            lo = gg * SSM_GROUP_CH
            row = jnp.concatenate([r[:, lo:lo + SSM_GROUP_CH] for r in rows_t], axis=1)
            ut_ref[j * GROUPS_PER_LANE_TILE + gg] = row.astype(ut_ref.dtype)


def _from_group_chunks(yt_ref, tile_ref):
    n_chunks = yt_ref.shape[1]
    for j in range(SSM_CH // LANES):
        for t in range(S5_Q):
            lo = t * SSM_GROUP_CH
            piece = jnp.concatenate(
                [yt_ref[j * GROUPS_PER_LANE_TILE + gg, :, lo:lo + SSM_GROUP_CH]
                 for gg in range(GROUPS_PER_LANE_TILE)], axis=1)
            tile_ref[j, pl.ds(t, n_chunks, stride=S5_Q), :] = piece
    return jnp.concatenate([tile_ref[j] for j in range(SSM_CH // LANES)], axis=1)


def _inproj_kernel(x_ref, mod_ref, g_ref, w_ref, v_ref, u_ref, ut_ref, tile_ref):
    sh = mod_ref[0, 0:1, :]
    sc = mod_ref[0, 1:2, :]
    sub = TM_IN // IN_SUBTILES
    sub_chunks = sub // S5_Q
    for s in range(IN_SUBTILES):
        r = slice(s * sub, (s + 1) * sub)
        h = _rms(x_ref[r, :], g_ref[...]) * (1.0 + sc) + sh
        z = jnp.dot(h.astype(BF16), w_ref[...], preferred_element_type=F32)
        v_ref[r, :] = z[:, :CONV_CH] * jax.nn.sigmoid(z[:, CONV_CH:2 * CONV_CH])
        u = z[:, 2 * CONV_CH:]
        u_ref[r, :] = u
        _to_group_chunks(u, tile_ref.at[s], ut_ref.at[:, s * sub_chunks:(s + 1) * sub_chunks, :])


def _inproj(x2, mod, g_pre, w_in_bf):
    tiles_per_seq = SEQ // TM_IN
    return pl.pallas_call(
        _inproj_kernel,
        grid=(N_TOK // TM_IN,),
        in_specs=[pl.BlockSpec((TM_IN, D_MODEL), lambda i: (i, 0)),
                  pl.BlockSpec((1, MOD_ROWS, D_MODEL), lambda i: (i // tiles_per_seq, 0, 0)),
                  pl.BlockSpec((1, D_MODEL), lambda i: (0, 0)),
                  pl.BlockSpec((D_MODEL, D_IN), lambda i: (0, 0))],
        out_specs=[pl.BlockSpec((TM_IN, CONV_CH), lambda i: (i, 0)),
                   pl.BlockSpec((TM_IN, SSM_CH), lambda i: (i, 0)),
                   pl.BlockSpec((SSM_GROUPS, TM_IN // S5_Q, S5_QH), lambda i: (0, i, 0))],
        out_shape=[jax.ShapeDtypeStruct((N_TOK, CONV_CH), F32),
                   jax.ShapeDtypeStruct((N_TOK, SSM_CH), F32),
                   jax.ShapeDtypeStruct((SSM_GROUPS, S5_CHUNKS, S5_QH), BF16)],
        scratch_shapes=[pltpu.VMEM((IN_SUBTILES, SSM_CH // LANES, TM_IN // IN_SUBTILES, LANES), F32)],
        compiler_params=_cparams(("parallel",)),
        name="in_proj",
    )(x2, mod, g_pre, w_in_bf)


def _conv_kernel(vc_ref, vp_ref, w_ref, cb_ref, lg_ref, lb_ref, go_ref, wg_ref, wu_ref, wd_ref,
                 o_ref, wgu_o, wd_o, sh_ref):
    for q in range(EXPERTS_PER_CONV_STEP):
        wgu_o[q, :, :D_EXPERT] = wg_ref[q].astype(BF16)
        wgu_o[q, :, D_EXPERT:] = wu_ref[q].astype(BF16)
        wd_o[q] = wd_ref[q].astype(BF16)

    i = pl.program_id(1)
    keep = (i > 0).astype(F32)
    n_ext = TL_CONV + CONV_HALO
    sh_ref[0, 0:CONV_HALO, :] = vp_ref[0] * keep
    sh_ref[0, CONV_HALO:, :] = vc_ref[0]
    ext = sh_ref[0]
    for s in range(1, SUBLANES):
        sh_ref[s] = pltpu.roll(ext, n_ext - s, axis=0)
    off = CONV_HALO - (CONV_WIDTH - 1)
    for r in range(TL_CONV // CONV_ROWS):
        acc = None
        for j in range(CONV_WIDTH):
            s = (off + j) % SUBLANES
            al = r * CONV_ROWS + (off + j) - s
            term = w_ref[j:j + 1, :] * sh_ref[s, al:al + CONV_ROWS, :]
            acc = term if acc is None else acc + term
        y = acc + cb_ref[...]
        mu = jnp.mean(y, axis=-1, keepdims=True)
        d = y - mu
        var = jnp.mean(d * d, axis=-1, keepdims=True)
        yn = d * lax.rsqrt(var + NORM_EPS) * lg_ref[...] + lb_ref[...]
        a = yn * jax.nn.sigmoid(yn)
        o_ref[0, r * CONV_ROWS:(r + 1) * CONV_ROWS, :] = _rms(a, go_ref[...]).astype(BF16)


def _conv(v3, conv_w, conv_b, ln_g, ln_b, g_out, we_gate, we_up, we_down):
    halo_per_tile = TL_CONV // CONV_HALO
    steps_per_seq = SEQ // TL_CONV
    vec = pl.BlockSpec((1, CONV_CH), lambda b, i: (0, 0))
    ex = EXPERTS_PER_CONV_STEP
    w_in = pl.BlockSpec((ex, D_MODEL, D_EXPERT), lambda b, i: (b * steps_per_seq + i, 0, 0))
    return pl.pallas_call(
        _conv_kernel,
        grid=(BATCH, steps_per_seq),
        in_specs=[pl.BlockSpec((1, TL_CONV, CONV_CH), lambda b, i: (b, i, 0)),
                  pl.BlockSpec((1, CONV_HALO, CONV_CH),
                               lambda b, i: (b, jnp.maximum(i * halo_per_tile - 1, 0), 0)),
                  pl.BlockSpec((CONV_WIDTH + 1, CONV_CH), lambda b, i: (0, 0)),
                  vec, vec, vec, vec,
                  w_in, w_in,
                  pl.BlockSpec((ex, D_EXPERT, D_MODEL), lambda b, i: (b * steps_per_seq + i, 0, 0))],
        out_specs=[pl.BlockSpec((1, TL_CONV, CONV_CH), lambda b, i: (b, i, 0)),
                   pl.BlockSpec((ex, D_MODEL, 2 * D_EXPERT), lambda b, i: (b * steps_per_seq + i, 0, 0)),
                   pl.BlockSpec((ex, D_EXPERT, D_MODEL), lambda b, i: (b * steps_per_seq + i, 0, 0))],
        out_shape=[jax.ShapeDtypeStruct((BATCH, SEQ, CONV_CH), BF16),
                   jax.ShapeDtypeStruct((N_EXPERTS, D_MODEL, 2 * D_EXPERT), BF16),
                   jax.ShapeDtypeStruct((N_EXPERTS, D_EXPERT, D_MODEL), BF16)],
        scratch_shapes=[pltpu.VMEM((SUBLANES, TL_CONV + CONV_HALO, CONV_CH), F32)],
        compiler_params=_cparams(("parallel", "arbitrary")),
        name="conv_module",
    )(v3, v3, conv_w, conv_b, ln_g, ln_b, g_out, we_gate, we_up, we_down)


S5_GROUP_ROWS = S5_CHUNKS + SUBLANES


S5_POW_ROWS = (S5_Q + 1 + SUBLANES - 1) // SUBLANES * SUBLANES
(S5_BB_RI, S5_BB_NIR, S5_BB_IR, S5_BB_RNI, S5_CC_RI, S5_CC_NIR, S5_N_PARAM) = range(7)


def _s5_kernel(ut_ref, pwr_ref, pwi_ref, par_ref, a_ref, bq_ref, bs_ref, yt_ref, sin_s, sp_s):
    phase = pl.program_id(0)
    g = pl.program_id(1)
    q = S5_Q
    n = 2 * SSM_STATE

    def group_row0(k):
        return pl.multiple_of((g * S5_GROUPS_PER_STEP + k) * S5_GROUP_ROWS, SUBLANES)

    def lam_pow(k, j):
        return pwr_ref[k, j:j + 1, :], pwi_ref[k, j:j + 1, :]

    @pl.when(phase == 0)
    def _():
        for k in range(S5_GROUPS_PER_STEP):
            bb_ri, bb_nir = par_ref[k, S5_BB_RI], par_ref[k, S5_BB_NIR]
            bb_ir, bb_rni = par_ref[k, S5_BB_IR], par_ref[k, S5_BB_RNI]
            blk_q, blk_s = [], []
            for t in range(q):
                pr, pi_ = lam_pow(k, q - 1 - t)
                blk_q.append(pr * bb_ri + pi_ * bb_nir)
                blk_s.append(pr * bb_ir + pi_ * bb_rni)
            wst = jnp.concatenate([jnp.concatenate(blk_q, axis=0), jnp.concatenate(blk_s, axis=0)], axis=1)
            r = jnp.dot(ut_ref[k], wst.astype(BF16), preferred_element_type=F32)
            sin_s[0, pl.ds(group_row0(k), S5_CHUNKS), :] = r[:, :n]
            sin_s[1, pl.ds(group_row0(k), S5_CHUNKS), :] = r[:, n:]

    @pl.when((phase == 1) & (g == 0))
    def _():
        a = a_ref[...]
        bq = bq_ref[...]
        bs = bs_ref[...]

        def body(c, carry):
            nxt = []
            for b in range(BATCH):
                x, xs = carry[b]
                rows = pl.ds(b * S5_CHUNKS_PER_SEQ + c, SSM_GROUPS, stride=S5_GROUP_ROWS)
                sp_s[rows, :] = x
                nxt.append((a * x + bq * xs + sin_s[0, rows, :], a * xs + bs * x + sin_s[1, rows, :]))
            return tuple(nxt)

        z = jnp.zeros((SSM_GROUPS, n), F32)
        lax.fori_loop(0, S5_CHUNKS_PER_SEQ, body, tuple((z, z) for _ in range(BATCH)))

    @pl.when(phase == 1)
    def _():
        for k in range(S5_GROUPS_PER_STEP):
            cc_ri, cc_nir = par_ref[k, S5_CC_RI], par_ref[k, S5_CC_NIR]
            cl = []
            for j in range(q + 1):
                pr, pi_ = lam_pow(k, j)
                cl.append(pr * cc_ri + pi_ * cc_nir)
            cl_lo = jnp.concatenate(cl[:q], axis=0)
            cl_hi = jnp.concatenate(cl[1:], axis=0)
            lane = lax.broadcasted_iota(jnp.int32, (1, n), 1)
            vgt = (cl_hi * jnp.where(lane < SSM_STATE, 1.0, -1.0)).astype(BF16)
            kt = _dot_nt_split(par_ref[k, S5_BB_RNI], cl_lo)
            padded = jnp.concatenate([jnp.zeros_like(kt), kt], axis=1)
            tg = jnp.concatenate(
                [padded[:, (q - t) * SSM_GROUP_CH:(q - t) * SSM_GROUP_CH + S5_QH] for t in range(q)],
                axis=0).astype(BF16)
            sp = sp_s[pl.ds(group_row0(k), S5_CHUNKS), :]
            y = jnp.dot(ut_ref[k], tg, preferred_element_type=F32)
            yt_ref[k] = y + lax.dot_general(sp.astype(BF16), vgt, (((1,), (1,)), ((), ())),
                                            preferred_element_type=F32)


def _s5(ut, pwr, pwi, params, a_cat, b_q, b_s):
    vec = pl.BlockSpec((SSM_GROUPS, 2 * SSM_STATE), lambda p, g: (0, 0))
    gs = S5_GROUPS_PER_STEP
    powers = pl.BlockSpec((gs, S5_POW_ROWS, 2 * SSM_STATE), lambda p, g: (g, 0, 0))
    return pl.pallas_call(
        _s5_kernel,
        grid=(2, SSM_GROUPS // gs),
        in_specs=[pl.BlockSpec((gs, S5_CHUNKS, S5_QH), lambda p, g: (g, 0, 0)),
                  powers, powers,
                  pl.BlockSpec((gs, S5_N_PARAM, SSM_GROUP_CH, 2 * SSM_STATE), lambda p, g: (g, 0, 0, 0)),
                  vec, vec, vec],
        out_specs=pl.BlockSpec((gs, S5_CHUNKS, S5_QH), lambda p, g: (g * p, 0, 0)),
        out_shape=jax.ShapeDtypeStruct((SSM_GROUPS, S5_CHUNKS, S5_QH), F32),
        scratch_shapes=[pltpu.VMEM((2, SSM_GROUPS * S5_GROUP_ROWS, 2 * SSM_STATE), F32),
                        pltpu.VMEM((SSM_GROUPS * S5_GROUP_ROWS, 2 * SSM_STATE), F32)],
        compiler_params=_cparams(("arbitrary", "arbitrary")),
        name="s5_chunked",
    )(ut, pwr, pwi, params, a_cat, b_q, b_s)


def _s5_operators(a_re, a_im, log_dt, b_re, b_im, c_re, c_im):
    q = S5_Q
    dt = jnp.exp(log_dt)[:, None]
    ar, ai = a_re, a_im
    mag = jnp.exp(ar * dt)
    lr = mag * jnp.cos(ai * dt)
    li = mag * jnp.sin(ai * dt)
    den = ar * ar + ai * ai
    nr = lr - 1.0
    kr = (nr * ar + li * ai) / den
    ki = (li * ar - nr * ai) / den
    bbr = kr[..., None] * b_re - ki[..., None] * b_im
    bbi = kr[..., None] * b_im + ki[..., None] * b_re
    j = jnp.arange(q + 1, dtype=F32)[None, :, None]
    pmag = jnp.exp(ar[:, None, :] * dt[:, :, None] * j)
    pang = ai[:, None, :] * dt[:, :, None] * j
    pr = pmag * jnp.cos(pang)
    pi_ = pmag * jnp.sin(pang)
    pad = ((0, 0), (0, S5_POW_ROWS - (q + 1)), (0, 0))
    pwr = jnp.pad(jnp.concatenate([pr, pr], axis=-1), pad)
    pwi = jnp.pad(jnp.concatenate([pi_, pi_], axis=-1), pad)
    br_t = bbr.transpose(0, 2, 1)
    bi_t = bbi.transpose(0, 2, 1)
    cat = lambda a, b: jnp.concatenate([a, b], axis=-1)
    stack = [None] * S5_N_PARAM
    stack[S5_BB_RI] = cat(br_t, bi_t)
    stack[S5_BB_NIR] = cat(-bi_t, br_t)
    stack[S5_BB_IR] = cat(bi_t, br_t)
    stack[S5_BB_RNI] = cat(br_t, -bi_t)
    stack[S5_CC_RI] = cat(c_re, c_im)
    stack[S5_CC_NIR] = cat(-c_im, c_re)
    params = jnp.stack(stack, axis=1)
    aq_r, aq_i = pr[:, q], pi_[:, q]
    a_cat = cat(aq_r, aq_r)
    b_q = cat(-aq_i, aq_i)
    b_s = cat(aq_i, -aq_i)
    return pwr, pwi, params, a_cat, b_q, b_s


def _gelu_tanh(x):
    return 0.5 * x * (1.0 + jnp.tanh(math.sqrt(2.0 / math.pi) * (x + 0.044715 * (x * x * x))))


def _mix_kernel(an_ref, yt_ref, u_ref, x_ref, mod_ref, d_ref, wglu_ref, bglu_ref, gos_ref,
                woa_ref, wob_ref, gpm_ref, gpf_ref, wr_ref, br_ref, tri_ref,
                x1_ref, h2_ref, eidx_ref, rank_ref, gw_ref, cnt_ref, run_ref, tile_ref):
    i = pl.program_id(0)
    tm = TM_MIX

    @pl.when(i == 0)
    def _():
        run_ref[...] = jnp.zeros_like(run_ref)

    gt_m = mod_ref[0, 2:3, :]
    sh_f = mod_ref[0, 3:4, :]
    sc_f = mod_ref[0, 4:5, :]

    yy = _from_group_chunks(yt_ref, tile_ref) + d_ref[...] * u_ref[...]
    g = _gelu_tanh(yy)
    gl = jnp.dot(g.astype(BF16), wglu_ref[...], preferred_element_type=F32) + bglu_ref[...]
    ob = g * jax.nn.sigmoid(gl)
    bn = _rms(ob, gos_ref[...]).astype(BF16)
    o = (jnp.dot(an_ref[...], woa_ref[...], preferred_element_type=F32)
         + jnp.dot(bn, wob_ref[...], preferred_element_type=F32))
    x1 = x_ref[...] + gt_m * _rms(o, gpm_ref[...])
    x1_ref[...] = x1
    h2 = _rms(x1, gpf_ref[...]) * (1.0 + sc_f) + sh_f
    h2_hi = h2.astype(BF16)
    h2_hi32 = h2_hi.astype(F32)
    h2_ref[...] = _pack_rounded_halves(h2_hi32)

    h2_lo = (h2 - h2_hi32).astype(BF16)
    nt = (((1,), (1,)), ((), ()))
    both = lax.dot_general(wr_ref[...], h2_hi, nt, preferred_element_type=F32)
    logits = (both[:N_EXPERTS] + both[N_EXPERTS:]
              + lax.dot_general(wr_ref[:N_EXPERTS, :], h2_lo, nt, preferred_element_type=F32))
    scores = jax.nn.sigmoid(logits)
    biased = scores + br_ref[...]
    ng = N_ROUTE_GROUPS
    gsz = N_EXPERTS // ng
    b3 = biased.reshape(ng, gsz, tm)
    s3 = scores.reshape(ng, gsz, tm)
    sub = lax.broadcasted_iota(jnp.int32, (ng, gsz, tm), 1).astype(F32)
    grp = lax.broadcasted_iota(jnp.int32, (ng, gsz, tm), 0).astype(F32)
    eid = grp * gsz + sub
    neg = -jnp.inf
    m1 = jnp.max(b3, axis=1, keepdims=True)
    i1 = jnp.min(jnp.where(b3 == m1, sub, float(gsz)), axis=1, keepdims=True)
    m2 = jnp.max(jnp.where(sub == i1, neg, b3), axis=1, keepdims=True)
    gs = m1 + m2
    gi = lax.broadcasted_iota(jnp.int32, (ng, 1, tm), 0)
    beaten = jnp.zeros((ng, 1, tm), F32)
    for gp in range(ng):
        o_ = gs[gp:gp + 1]
        beats = (o_ > gs) | ((o_ == gs) & (gi > gp))
        beaten = beaten + beats.astype(F32)
    gmask = beaten < float(TOPK_ROUTE_GROUPS)
    masked = jnp.where(gmask, b3, neg)

    sels = []
    picked = jnp.zeros((ng, gsz, tm), F32)
    for k in range(TOP_K):
        m = jnp.max(jnp.max(masked, axis=0, keepdims=True), axis=1, keepdims=True)
        cand = jnp.where(masked == m, eid, float(N_EXPERTS))
        sel = jnp.min(jnp.min(cand, axis=0, keepdims=True), axis=1, keepdims=True)
        oh = eid == sel
        masked = jnp.where(oh, neg, masked)
        picked = jnp.where(oh, 1.0, picked)
        sels.append(sel)

    pm = picked.reshape(N_EXPERTS, tm)
    prefix = jnp.dot(pm.astype(BF16), tri_ref[...], preferred_element_type=F32) + run_ref[:, 0:1]
    p3 = prefix.reshape(ng, gsz, tm)
    run_new = run_ref[...] + jnp.sum(pm, axis=1, keepdims=True)
    run_ref[...] = run_new
    cnt_ref[...] = run_new

    sc_rows = []
    for k in range(TOP_K):
        oh = eid == sels[k]
        sc_k = jnp.sum(jnp.sum(jnp.where(oh, s3, 0.0), axis=0, keepdims=True), axis=1, keepdims=True)
        rk_k = jnp.sum(jnp.sum(jnp.where(oh, p3, 0.0), axis=0, keepdims=True), axis=1, keepdims=True)
        sc_rows.append(sc_k)
        eidx_ref[k:k + 1, :] = sels[k].reshape(1, tm).astype(jnp.int32)
        rank_ref[k:k + 1, :] = rk_k.reshape(1, tm).astype(jnp.int32)
    tot = sc_rows[0]
    for k in range(1, TOP_K):
        tot = tot + sc_rows[k]
    inv = ROUTED_SCALE / (tot + 1e-20)
    for k in range(TOP_K):
        gw_ref[k:k + 1, :] = (sc_rows[k] * inv).reshape(1, tm)


def _mix(half, a_n, yt, u2, x2, mod, d_skip, wglu_bf, b_glu, g_out_ssm, wo_a, wo_b, g_post_mix,
         g_pre_ffn, w_router_split, b_router_col, tri):
    tm = TM_MIX
    t0 = half * (HALF_TOK // tm)
    row = lambda n: pl.BlockSpec((1, n), lambda i: (0, 0))
    full = lambda a, b: pl.BlockSpec((a, b), lambda i: (0, 0))
    tok_in = lambda n: pl.BlockSpec((tm, n), lambda i: (t0 + i, 0))
    tok = lambda n: pl.BlockSpec((tm, n), lambda i: (i, 0))
    col = pl.BlockSpec((TOP_K, tm), lambda i: (0, i))
    return pl.pallas_call(
        _mix_kernel,
        grid=(HALF_TOK // tm,),
        in_specs=[tok_in(CONV_CH),
                  pl.BlockSpec((SSM_GROUPS, tm // S5_Q, S5_QH), lambda i: (0, t0 + i, 0)),
                  tok_in(SSM_CH), tok_in(D_MODEL),
                  pl.BlockSpec((1, MOD_ROWS, D_MODEL), lambda i: (half, 0, 0)),
                  row(SSM_CH), full(SSM_CH, SSM_CH), row(SSM_CH), row(SSM_CH),
                  full(CONV_CH, D_MODEL), full(SSM_CH, D_MODEL), row(D_MODEL), row(D_MODEL),
                  full(2 * N_EXPERTS, D_MODEL), full(N_EXPERTS, 1), full(tm, tm)],
        out_specs=[tok(D_MODEL), tok(D_MODEL // 2), col, col, col,
                   pl.BlockSpec((N_EXPERTS, LANES), lambda i: (0, 0))],
        out_shape=[jax.ShapeDtypeStruct((HALF_TOK, D_MODEL), F32),
                   jax.ShapeDtypeStruct((HALF_TOK, D_MODEL // 2), jnp.uint32),
                   jax.ShapeDtypeStruct((TOP_K, HALF_TOK), jnp.int32),
                   jax.ShapeDtypeStruct((TOP_K, HALF_TOK), jnp.int32),
                   jax.ShapeDtypeStruct((TOP_K, HALF_TOK), F32),
                   jax.ShapeDtypeStruct((N_EXPERTS, LANES), F32)],
        scratch_shapes=[pltpu.VMEM((N_EXPERTS, LANES), F32),
                        pltpu.VMEM((SSM_CH // LANES, tm, LANES), F32)],
        compiler_params=_cparams(("arbitrary",)),
        name="mix_out_router",
    )(a_n, yt, u2, x2, mod, d_skip, wglu_bf, b_glu, g_out_ssm, wo_a, wo_b, g_post_mix,
      g_pre_ffn, w_router_split, b_router_col, tri)


def _expert_kernel(blk0_ref, nblk_ref, bunit_ref, bfull_ref, xs_hbm, wgu_ref, wd_ref, ys_hbm,
                   xbuf, ybuf, sem_in, sem_out):
    e = pl.program_id(0)
    n = nblk_ref[e]
    b0 = blk0_ref[e]
    n_all = blk0_ref[N_EXPERTS - 1] + nblk_ref[N_EXPERTS - 1]

    def copies(b, slot, rows):
        hbm = pl.ds(pl.multiple_of(bunit_ref[b] * ROW_UNIT, ROW_UNIT), rows)
        buf = pl.ds(0, rows)
        return (pltpu.make_async_copy(xs_hbm.at[hbm], xbuf.at[slot, buf], sem_in.at[slot]),
                pltpu.make_async_copy(ybuf.at[slot, buf], ys_hbm.at[hbm], sem_out.at[slot]))

    def for_size(b, fn):
        full = bfull_ref[b] == 1

        @pl.when(full)
        def _():
            fn(ROW_BLOCK)

        @pl.when(jnp.logical_not(full))
        def _():
            fn(ROW_UNIT)

    for b in range(EXPERT_AHEAD):
        @pl.when((e == 0) & (b < n_all))
        def _():
            for_size(b, lambda rows: copies(b, b, rows)[0].start())

    def compute(slot, rows):
        x_lo, x_hi = _unpack_halves(xbuf[slot, pl.ds(0, rows)])
        x = jnp.concatenate([x_lo.astype(BF16), x_hi.astype(BF16)], axis=1)
        h = jnp.dot(x, wgu_ref[0], preferred_element_type=F32)
        hg = h[:, :D_EXPERT]
        act = hg * jax.nn.sigmoid(hg) * h[:, D_EXPERT:]
        ybuf[slot, pl.ds(0, rows)] = _pack_halves(
            jnp.dot(act.astype(BF16), wd_ref[0], preferred_element_type=F32))

    def block(b, carry):
        slot = b % EXPERT_SLOTS
        ahead = b + EXPERT_AHEAD

        @pl.when(ahead < n_all)
        def _():
            for_size(ahead, lambda rows: copies(ahead, ahead % EXPERT_SLOTS, rows)[0].start())

        @pl.when(b >= EXPERT_SLOTS)
        def _():
            for_size(b - EXPERT_SLOTS, lambda rows: copies(b - EXPERT_SLOTS, slot, rows)[1].wait())

        def work(rows):
            cp_in, cp_out = copies(b, slot, rows)
            cp_in.wait()
            compute(slot, rows)
            cp_out.start()

        for_size(b, work)
        return carry

    lax.fori_loop(b0, b0 + n, block, 0)

    @pl.when(e == N_EXPERTS - 1)
    def _():
        for j in range(1, EXPERT_SLOTS + 1):
            @pl.when(n_all >= j)
            def _():
                last = n_all - j
                for_size(last, lambda rows: copies(last, last % EXPERT_SLOTS, rows)[1].wait())


def _experts(blk0, nblk, bunit, bfull, xs, we_gu, we_d):
    any_spec = pl.BlockSpec(memory_space=pl.ANY)
    grid_spec = pltpu.PrefetchScalarGridSpec(
        num_scalar_prefetch=4,
        grid=(N_EXPERTS,),
        in_specs=[any_spec,
                  pl.BlockSpec((1, D_MODEL, 2 * D_EXPERT), lambda e, *_: (e, 0, 0)),
                  pl.BlockSpec((1, D_EXPERT, D_MODEL), lambda e, *_: (e, 0, 0))],
        out_specs=any_spec,
        scratch_shapes=[pltpu.VMEM((EXPERT_SLOTS, ROW_BLOCK, D_MODEL // 2), jnp.uint32),
                        pltpu.VMEM((EXPERT_SLOTS, ROW_BLOCK, D_MODEL // 2), jnp.uint32),
                        pltpu.SemaphoreType.DMA((EXPERT_SLOTS,)),
                        pltpu.SemaphoreType.DMA((EXPERT_SLOTS,))],
    )
    return pl.pallas_call(
        _expert_kernel,
        grid_spec=grid_spec,
        out_shape=jax.ShapeDtypeStruct((N_ROWS, D_MODEL // 2), jnp.uint32),
        compiler_params=_cparams(("arbitrary",)),
        name="routed_experts",
    )(blk0, nblk, bunit, bfull, xs, we_gu, we_d)


def _final_kernel(h2_ref, yg_ref, gw_ref, x1_ref, mod_ref, wgu_ref, wd_ref, g_ref, o_ref):
    half = D_MODEL // 2
    gt_f = mod_ref[0, 5:6, :]
    x_lo, x_hi = _unpack_halves(h2_ref[...])
    h = (jnp.dot(x_lo.astype(BF16), wgu_ref[:half, :], preferred_element_type=F32)
         + jnp.dot(x_hi.astype(BF16), wgu_ref[half:, :], preferred_element_type=F32))
    hg = h[:, :D_EXPERT]
    act = hg * jax.nn.sigmoid(hg) * h[:, D_EXPERT:]
    shared = jnp.dot(act.astype(BF16), wd_ref[...], preferred_element_type=F32)
    y_lo = shared[:, :half]
    y_hi = shared[:, half:]
    for k in range(TOP_K):
        r_lo, r_hi = _unpack_halves(yg_ref[k])
        w = gw_ref[:, k:k + 1]
        y_lo = y_lo + w * r_lo
        y_hi = y_hi + w * r_hi
    ms = (jnp.sum(y_lo * y_lo, axis=-1, keepdims=True)
          + jnp.sum(y_hi * y_hi, axis=-1, keepdims=True)) * (1.0 / D_MODEL)
    inv = lax.rsqrt(ms + NORM_EPS)
    o_ref[:, :half] = x1_ref[:, :half] + gt_f[:, :half] * (y_lo * inv * g_ref[:, :half])
    o_ref[:, half:] = x1_ref[:, half:] + gt_f[:, half:] * (y_hi * inv * g_ref[:, half:])


def _final(half, out_prev, h2p, yg, gw_t, x1, mod, ws_gu, ws_d, g_post_ffn):
    tm = TM_OUT
    t0 = half * (HALF_TOK // tm)
    in_specs = [pl.BlockSpec((tm, D_MODEL // 2), lambda i: (i, 0)),
                pl.BlockSpec((TOP_K, tm, D_MODEL // 2), lambda i: (0, i, 0),
                             pipeline_mode=pl.Buffered(FINAL_ROW_BUFFERS)),
                pl.BlockSpec((tm, TOP_K), lambda i: (i, 0)),
                pl.BlockSpec((tm, D_MODEL), lambda i: (i, 0)),
                pl.BlockSpec((1, MOD_ROWS, D_MODEL), lambda i: (half, 0, 0)),
                pl.BlockSpec((D_MODEL, 2 * D_EXPERT), lambda i: (0, 0)),
                pl.BlockSpec((D_EXPERT, D_MODEL), lambda i: (0, 0)),
                pl.BlockSpec((1, D_MODEL), lambda i: (0, 0))]
    out_spec = pl.BlockSpec((tm, D_MODEL), lambda i: (t0 + i, 0))

    def stream(*refs):
        pltpu.emit_pipeline(_final_kernel, grid=(HALF_TOK // tm,), in_specs=in_specs,
                            out_specs=[out_spec])(*refs[:len(in_specs)], refs[-1])

    args = [h2p, yg, gw_t, x1, mod, ws_gu, ws_d, g_post_ffn]
    aliases = {}
    if out_prev is not None:
        aliases = {len(args): 0}
        args.append(out_prev)
    any_spec = pl.BlockSpec(memory_space=pl.ANY)
    return pl.pallas_call(
        stream,
        in_specs=[any_spec] * len(args),
        out_specs=any_spec,
        out_shape=jax.ShapeDtypeStruct((N_TOK, D_MODEL), F32),
        input_output_aliases=aliases,
        compiler_params=_cparams(None),
        name="shared_final",
    )(*args)


def _sc_worker_id():
    return lax.axis_index("s") * SC_CORES + lax.axis_index("c")


def _dispatch_body(h_hbm, dest_hbm, xs_hbm, idx_v, rows_v, sem_l, sem_s):
    n = SC_CHUNKS_PER_WORKER
    c0 = _sc_worker_id() * n

    def load(i, b):
        return pltpu.async_copy(h_hbm.at[pl.ds((c0 + i) * SC_W, SC_W)], rows_v.at[b], sem_l.at[b])

    loads = [None] * n
    scat = [None] * n
    loads[0] = load(0, 0)
    for i in range(n):
        b = i % 2
        pltpu.sync_copy(dest_hbm.at[c0 + i], idx_v.at[b])
        loads[i].wait()
        if i + 1 < n:
            if i >= 1:
                for d in scat[i - 1]:
                    d.wait()
            loads[i + 1] = load(i + 1, 1 - b)
        scat[i] = [pltpu.async_copy(rows_v.at[b], xs_hbm.at[idx_v.at[b].at[k]], sem_s.at[b])
                   for k in range(TOP_K)]
    for i in (n - 2, n - 1):
        for d in scat[i]:
            d.wait()


def _sc_dispatch(h2p, dest3):
    mesh = plsc.VectorSubcoreMesh(core_axis_name="c", subcore_axis_name="s")
    return pl.kernel(
        _dispatch_body, mesh=mesh,
        out_type=jax.ShapeDtypeStruct((N_ROWS, D_MODEL // 2), jnp.uint32),
        scratch_types=[pltpu.VMEM((2, TOP_K, SC_W), jnp.int32),
                       pltpu.VMEM((2, SC_W, D_MODEL // 2), jnp.uint32),
                       pltpu.SemaphoreType.DMA((2,)), pltpu.SemaphoreType.DMA((2,))],
    )(h2p, dest3)


def _combine_body(ys_hbm, dest_hbm, yg_hbm, idx_v, rows_v, sem_g, sem_w):
    c0 = _sc_worker_id() * SC_CHUNKS_PER_WORKER

    @pl.loop(0, SC_CHUNKS_PER_WORKER)
    def _(i):
        c = c0 + i
        pltpu.sync_copy(dest_hbm.at[c], idx_v)
        g = [None] * TOP_K
        w = [None] * TOP_K
        g[0] = pltpu.async_copy(ys_hbm.at[idx_v.at[0]], rows_v.at[0], sem_g.at[0])
        for k in range(TOP_K):
            b = k % 2
            g[k].wait()
            if k + 1 < TOP_K:
                if k >= 1:
                    w[k - 1].wait()
                g[k + 1] = pltpu.async_copy(ys_hbm.at[idx_v.at[k + 1]], rows_v.at[1 - b], sem_g.at[1 - b])
            w[k] = pltpu.async_copy(rows_v.at[b], yg_hbm.at[k].at[pl.ds(c * SC_W, SC_W)], sem_w.at[b])
        w[TOP_K - 2].wait()
        w[TOP_K - 1].wait()


def _sc_combine(ysp, dest3):
    mesh = plsc.VectorSubcoreMesh(core_axis_name="c", subcore_axis_name="s")
    return pl.kernel(
        _combine_body, mesh=mesh,
        out_type=jax.ShapeDtypeStruct((TOP_K, HALF_TOK, D_MODEL // 2), jnp.uint32),
        scratch_types=[pltpu.VMEM((TOP_K, SC_W), jnp.int32),
                       pltpu.VMEM((2, SC_W, D_MODEL // 2), jnp.uint32),
                       pltpu.SemaphoreType.DMA((2,)), pltpu.SemaphoreType.DMA((2,))],
    )(ysp, dest3)


def kernel(x, c, w_ada, b_ada, g_pre_mix, g_post_mix, w_in, conv_w, conv_b, conv_ln_g, conv_ln_b,
           ssm_a_re, ssm_a_im, ssm_log_dt, ssm_b_re, ssm_b_im, ssm_c_re, ssm_c_im, ssm_d,
           ssm_w_glu, ssm_b_glu, g_out_conv, g_out_ssm, w_out, g_pre_ffn, g_post_ffn,
           w_router, b_router, we_gate, we_up, we_down, ws_gate, ws_up, ws_down):
    l = 0
    x2 = x.reshape(N_TOK, D_MODEL)
    r1 = lambda a: a.reshape(1, -1)

    c_pad = jnp.zeros((SUBLANES, D_MODEL), F32).at[:BATCH].set(c)
    mod = _ada(c_pad, w_ada[l], r1(b_ada[l]))[:BATCH].reshape(BATCH, N_MOD, D_MODEL)
    mod = jnp.concatenate([mod, jnp.zeros((BATCH, MOD_ROWS - N_MOD, D_MODEL), F32)], axis=1)

    v, u, ut = _inproj(x2, mod, r1(g_pre_mix[l]), w_in[l].astype(BF16))
    cw = jnp.concatenate([conv_w[l].reshape(CONV_WIDTH, CONV_CH), jnp.zeros((1, CONV_CH), F32)], axis=0)
    a_n, we_gu, we_d = _conv(v.reshape(BATCH, SEQ, CONV_CH), cw, r1(conv_b[l]), r1(conv_ln_g[l]),
                             r1(conv_ln_b[l]), r1(g_out_conv[l]), we_gate[l], we_up[l], we_down[l])
    a_n = a_n.reshape(N_TOK, CONV_CH)

    pwr, pwi, s5_params, a_cat, b_q, b_s = _s5_operators(
        ssm_a_re[l], ssm_a_im[l], ssm_log_dt[l], ssm_b_re[l], ssm_b_im[l], ssm_c_re[l], ssm_c_im[l])
    yt = _s5(ut, pwr, pwi, s5_params, a_cat, b_q, b_s)

    tm = TM_MIX
    tri = (jnp.arange(tm)[:, None] < jnp.arange(tm)[None, :]).astype(BF16)
    wo = w_out[l].astype(BF16)
    wr_t = w_router[l].T
    wr_hi = wr_t.astype(BF16)
    wr_split = jnp.concatenate([wr_hi, (wr_t - wr_hi.astype(F32)).astype(BF16)], axis=0)
    mix_params = (r1(ssm_d[l]), ssm_w_glu[l].astype(BF16), r1(ssm_b_glu[l]), r1(g_out_ssm[l]),
                  wo[:CONV_CH], wo[CONV_CH:], r1(g_post_mix[l]), r1(g_pre_ffn[l]),
                  wr_split, b_router[l].reshape(N_EXPERTS, 1), tri)
    ws_gu = jnp.concatenate([ws_gate[l], ws_up[l]], axis=1).astype(BF16)
    ws_d = ws_down[l].astype(BF16)
    e_ids = jnp.arange(N_EXPERTS, dtype=jnp.int32)

    out = None
    for half in range(N_HALVES):
        x1, h2, eidx, rank, gw, cnt = _mix(half, a_n, yt, u, x2, mod, *mix_params)
        counts = cnt[:, 0].astype(jnp.int32)
        units = (counts + ROW_UNIT - 1) // ROW_UNIT
        unit0 = jnp.cumsum(units) - units
        nblk = (units + 1) // 2
        blk0 = jnp.cumsum(nblk) - nblk
        dest = rank + jnp.sum(jnp.where(eidx[..., None] == e_ids, unit0 * ROW_UNIT, 0), axis=-1)
        dest3 = dest.reshape(TOP_K, HALF_TOK // SC_W, SC_W).transpose(1, 0, 2)
        b_ids = jnp.arange(N_BLOCKS, dtype=jnp.int32)
        owner = (b_ids[:, None] >= blk0[None, :]) & (b_ids[:, None] < (blk0 + nblk)[None, :])
        pick = lambda v: jnp.sum(jnp.where(owner, v[None, :], 0), axis=1)
        local = b_ids - pick(blk0)
        bunit = pick(unit0) + 2 * local
        bfull = (2 * local + 2 <= pick(units)).astype(jnp.int32)

        xs = _sc_dispatch(h2, dest3)
        ys = _experts(blk0, nblk, bunit, bfull, xs, we_gu, we_d)
        yg = _sc_combine(ys, dest3)
        out = _final(half, out, h2, yg, gw.T, x1, mod, ws_gu, ws_d, r1(g_post_ffn[l]))
    return out.reshape(BATCH, SEQ, D_MODEL)
```

```python
import math

import jax
import jax.numpy as jnp
from jax import lax
from jax.experimental import pallas as pl
from jax.experimental.pallas import tpu as pltpu
from jax.experimental.pallas import tpu_sc as plsc

F32 = jnp.float32
BF16 = jnp.bfloat16

D_MODEL = 1024
BATCH = 2
SEQ = 8192
N_TOK = BATCH * SEQ
CONV_CH = 512
CONV_WIDTH = 31
SSM_CH = 512
SSM_GROUP_CH = 16
SSM_GROUPS = 32
SSM_STATE = 64
D_IN = 2 * CONV_CH + SSM_CH
N_EXPERTS = 64
TOP_K = 8
N_ROUTE_GROUPS = 8
TOPK_ROUTE_GROUPS = 4
D_EXPERT = 256
ROUTED_SCALE = 2.5
NORM_EPS = 1e-6

SUBLANES = 8
LANES = 128

N_MOD = 6
MOD_ROWS = SUBLANES
ADA_COLS = 1536
TM_IN = 1024
IN_SUBTILES = 4
TL_CONV = 512
CONV_HALO = 32
CONV_ROWS = 256
EXPERTS_PER_CONV_STEP = N_EXPERTS * TL_CONV // N_TOK
assert EXPERTS_PER_CONV_STEP * N_TOK == N_EXPERTS * TL_CONV
S5_Q = 32
S5_GROUPS_PER_STEP = 4
S5_QH = S5_Q * SSM_GROUP_CH
S5_CHUNKS = N_TOK // S5_Q
S5_CHUNKS_PER_SEQ = SEQ // S5_Q
TM_MIX = 1024
ROW_BLOCK = 512
ROW_UNIT = ROW_BLOCK // 2
EXPERT_AHEAD = 4
EXPERT_SLOTS = EXPERT_AHEAD + 1
HALF_TOK = SEQ
N_HALVES = N_TOK // HALF_TOK
N_UNITS = HALF_TOK * TOP_K // ROW_UNIT + N_EXPERTS
N_BLOCKS = (N_UNITS + N_EXPERTS) // 2
N_ROWS = N_UNITS * ROW_UNIT
TM_OUT = 512
SC_CORES = 2
SC_SUBCORES = 16
SC_WORKERS = SC_CORES * SC_SUBCORES
SC_W = 64
SC_CHUNKS_PER_WORKER = HALF_TOK // (SC_WORKERS * SC_W)
VMEM_LIMIT = 48 * 1024 * 1024


def _cparams(sem):
    return pltpu.CompilerParams(dimension_semantics=sem, vmem_limit_bytes=VMEM_LIMIT)


def _pack_rounded_halves(xr):
    n = xr.shape[-1] // 2
    lo = lax.bitcast_convert_type(xr[:, :n], jnp.uint32)
    hi = lax.bitcast_convert_type(xr[:, n:], jnp.uint32)
    return hi | (lo >> 16)


def _pack_halves(x):
    return _pack_rounded_halves(x.astype(BF16).astype(F32))


def _unpack_halves(p):
    lo = lax.bitcast_convert_type(p << 16, F32)
    hi = lax.bitcast_convert_type(p & jnp.uint32(0xFFFF0000), F32)
    return lo, hi


def _rms(x, g):
    return x * lax.rsqrt(jnp.mean(x * x, axis=-1, keepdims=True) + NORM_EPS) * g


def _split_bf16(x):
    hi = x.astype(BF16)
    return hi, (x - hi.astype(F32)).astype(BF16)


def _dot_nt_split(a, b):
    nt = (((1,), (1,)), ((), ()))
    a_hi, a_lo = _split_bf16(a)
    b_hi, b_lo = _split_bf16(b)
    m = a.shape[0]
    both = lax.dot_general(jnp.concatenate([a_hi, a_lo], axis=0), b_hi, nt, preferred_element_type=F32)
    return both[:m] + both[m:] + lax.dot_general(a_hi, b_lo, nt, preferred_element_type=F32)


def _ada_kernel(c_ref, w_ref, b_ref, o_ref):
    c = c_ref[...]
    a = c * jax.nn.sigmoid(c)
    o_ref[...] = jnp.dot(a, w_ref[...], preferred_element_type=F32,
                         precision=lax.Precision.HIGHEST) + b_ref[...]


def _ada(c_pad, w_ada, b_ada):
    n = w_ada.shape[1]
    bn = ADA_COLS
    return pl.pallas_call(
        _ada_kernel,
        grid=(n // bn,),
        in_specs=[pl.BlockSpec((SUBLANES, D_MODEL), lambda j: (0, 0)),
                  pl.BlockSpec((D_MODEL, bn), lambda j: (0, j)),
                  pl.BlockSpec((1, bn), lambda j: (0, j))],
        out_specs=pl.BlockSpec((SUBLANES, bn), lambda j: (0, j)),
        out_shape=jax.ShapeDtypeStruct((SUBLANES, n), F32),
        compiler_params=_cparams(("arbitrary",)),
        name="ada_mod",
    )(c_pad, w_ada, b_ada)


GROUPS_PER_LANE_TILE = LANES // SSM_GROUP_CH


def _to_group_chunks(u, tile_ref, ut_ref):
    n_chunks = u.shape[0] // S5_Q
    for j in range(SSM_CH // LANES):
        tile_ref[j] = u[:, LANES * j:LANES * (j + 1)]
    for j in range(SSM_CH // LANES):
        rows_t = [tile_ref[j, pl.ds(t, n_chunks, stride=S5_Q), :] for t in range(S5_Q)]
        for gg in range(GROUPS_PER_LANE_TILE):
            lo = gg * SSM_GROUP_CH
            row = jnp.concatenate([r[:, lo:lo + SSM_GROUP_CH] for r in rows_t], axis=1)
            ut_ref[j * GROUPS_PER_LANE_TILE + gg] = row.astype(ut_ref.dtype)


def _from_group_chunks(yt_ref, tile_ref):
    n_chunks = yt_ref.shape[1]
    for j in range(SSM_CH // LANES):
        for t in range(S5_Q):
            lo = t * SSM_GROUP_CH
            piece = jnp.concatenate(
                [yt_ref[j * GROUPS_PER_LANE_TILE + gg, :, lo:lo + SSM_GROUP_CH]
                 for gg in range(GROUPS_PER_LANE_TILE)], axis=1)
            tile_ref[j, pl.ds(t, n_chunks, stride=S5_Q), :] = piece
    return jnp.concatenate([tile_ref[j] for j in range(SSM_CH // LANES)], axis=1)


def _inproj_kernel(x_ref, mod_ref, g_ref, w_ref, v_ref, u_ref, ut_ref, tile_ref):
    sh = mod_ref[0, 0:1, :]
    sc = mod_ref[0, 1:2, :]
    sub = TM_IN // IN_SUBTILES
    sub_chunks = sub // S5_Q
    for s in range(IN_SUBTILES):
        r = slice(s * sub, (s + 1) * sub)
        h = _rms(x_ref[r, :], g_ref[...]) * (1.0 + sc) + sh
        z = jnp.dot(h.astype(BF16), w_ref[...], preferred_element_type=F32)
        v_ref[r, :] = z[:, :CONV_CH] * jax.nn.sigmoid(z[:, CONV_CH:2 * CONV_CH])
        u = z[:, 2 * CONV_CH:]
        u_ref[r, :] = u
        _to_group_chunks(u, tile_ref.at[s], ut_ref.at[:, s * sub_chunks:(s + 1) * sub_chunks, :])


def _inproj(x2, mod, g_pre, w_in_bf):
    tiles_per_seq = SEQ // TM_IN
    return pl.pallas_call(
        _inproj_kernel,
        grid=(N_TOK // TM_IN,),
        in_specs=[pl.BlockSpec((TM_IN, D_MODEL), lambda i: (i, 0)),
                  pl.BlockSpec((1, MOD_ROWS, D_MODEL), lambda i: (i // tiles_per_seq, 0, 0)),
                  pl.BlockSpec((1, D_MODEL), lambda i: (0, 0)),
                  pl.BlockSpec((D_MODEL, D_IN), lambda i: (0, 0))],
        out_specs=[pl.BlockSpec((TM_IN, CONV_CH), lambda i: (i, 0)),
                   pl.BlockSpec((TM_IN, SSM_CH), lambda i: (i, 0)),
                   pl.BlockSpec((SSM_GROUPS, TM_IN // S5_Q, S5_QH), lambda i: (0, i, 0))],
        out_shape=[jax.ShapeDtypeStruct((N_TOK, CONV_CH), F32),
                   jax.ShapeDtypeStruct((N_TOK, SSM_CH), F32),
                   jax.ShapeDtypeStruct((SSM_GROUPS, S5_CHUNKS, S5_QH), BF16)],
        scratch_shapes=[pltpu.VMEM((IN_SUBTILES, SSM_CH // LANES, TM_IN // IN_SUBTILES, LANES), F32)],
        compiler_params=_cparams(("parallel",)),
        name="in_proj",
    )(x2, mod, g_pre, w_in_bf)


def _conv_kernel(vc_ref, vp_ref, w_ref, cb_ref, lg_ref, lb_ref, go_ref, wg_ref, wu_ref, wd_ref,
                 o_ref, wgu_o, wd_o, sh_ref):
    for q in range(EXPERTS_PER_CONV_STEP):
        wgu_o[q, :, :D_EXPERT] = wg_ref[q].astype(BF16)
        wgu_o[q, :, D_EXPERT:] = wu_ref[q].astype(BF16)
        wd_o[q] = wd_ref[q].astype(BF16)

    i = pl.program_id(1)
    keep = (i > 0).astype(F32)
    n_ext = TL_CONV + CONV_HALO
    sh_ref[0, 0:CONV_HALO, :] = vp_ref[0] * keep
    sh_ref[0, CONV_HALO:, :] = vc_ref[0]
    ext = sh_ref[0]
    for s in range(1, SUBLANES):
        sh_ref[s] = pltpu.roll(ext, n_ext - s, axis=0)
    off = CONV_HALO - (CONV_WIDTH - 1)
    for r in range(TL_CONV // CONV_ROWS):
        acc = None
        for j in range(CONV_WIDTH):
            s = (off + j) % SUBLANES
            al = r * CONV_ROWS + (off + j) - s
            term = w_ref[j:j + 1, :] * sh_ref[s, al:al + CONV_ROWS, :]
            acc = term if acc is None else acc + term
        y = acc + cb_ref[...]
        mu = jnp.mean(y, axis=-1, keepdims=True)
        d = y - mu
        var = jnp.mean(d * d, axis=-1, keepdims=True)
        yn = d * lax.rsqrt(var + NORM_EPS) * lg_ref[...] + lb_ref[...]
        a = yn * jax.nn.sigmoid(yn)
        o_ref[0, r * CONV_ROWS:(r + 1) * CONV_ROWS, :] = _rms(a, go_ref[...]).astype(BF16)


def _conv(v3, conv_w, conv_b, ln_g, ln_b, g_out, we_gate, we_up, we_down):
    halo_per_tile = TL_CONV // CONV_HALO
    steps_per_seq = SEQ // TL_CONV
    vec = pl.BlockSpec((1, CONV_CH), lambda b, i: (0, 0))
    ex = EXPERTS_PER_CONV_STEP
    w_in = pl.BlockSpec((ex, D_MODEL, D_EXPERT), lambda b, i: (b * steps_per_seq + i, 0, 0))
    return pl.pallas_call(
        _conv_kernel,
        grid=(BATCH, steps_per_seq),
        in_specs=[pl.BlockSpec((1, TL_CONV, CONV_CH), lambda b, i: (b, i, 0)),
                  pl.BlockSpec((1, CONV_HALO, CONV_CH),
                               lambda b, i: (b, jnp.maximum(i * halo_per_tile - 1, 0), 0)),
                  pl.BlockSpec((CONV_WIDTH + 1, CONV_CH), lambda b, i: (0, 0)),
                  vec, vec, vec, vec,
                  w_in, w_in,
                  pl.BlockSpec((ex, D_EXPERT, D_MODEL), lambda b, i: (b * steps_per_seq + i, 0, 0))],
        out_specs=[pl.BlockSpec((1, TL_CONV, CONV_CH), lambda b, i: (b, i, 0)),
                   pl.BlockSpec((ex, D_MODEL, 2 * D_EXPERT), lambda b, i: (b * steps_per_seq + i, 0, 0)),
                   pl.BlockSpec((ex, D_EXPERT, D_MODEL), lambda b, i: (b * steps_per_seq + i, 0, 0))],
        out_shape=[jax.ShapeDtypeStruct((BATCH, SEQ, CONV_CH), BF16),
                   jax.ShapeDtypeStruct((N_EXPERTS, D_MODEL, 2 * D_EXPERT), BF16),
                   jax.ShapeDtypeStruct((N_EXPERTS, D_EXPERT, D_MODEL), BF16)],
        scratch_shapes=[pltpu.VMEM((SUBLANES, TL_CONV + CONV_HALO, CONV_CH), F32)],
        compiler_params=_cparams(("parallel", "arbitrary")),
        name="conv_module",
    )(v3, v3, conv_w, conv_b, ln_g, ln_b, g_out, we_gate, we_up, we_down)


S5_GROUP_ROWS = S5_CHUNKS + SUBLANES


S5_POW_ROWS = (S5_Q + 1 + SUBLANES - 1) // SUBLANES * SUBLANES
(S5_BB_RI, S5_BB_NIR, S5_BB_IR, S5_BB_RNI, S5_CC_RI, S5_CC_NIR, S5_N_PARAM) = range(7)


def _s5_kernel(ut_ref, pwr_ref, pwi_ref, par_ref, a_ref, bq_ref, bs_ref, yt_ref, sin_s, sp_s):
    phase = pl.program_id(0)
    g = pl.program_id(1)
    q = S5_Q
    n = 2 * SSM_STATE

    def group_row0(k):
        return pl.multiple_of((g * S5_GROUPS_PER_STEP + k) * S5_GROUP_ROWS, SUBLANES)

    def lam_pow(k, j):
        return pwr_ref[k, j:j + 1, :], pwi_ref[k, j:j + 1, :]

    @pl.when(phase == 0)
    def _():
        for k in range(S5_GROUPS_PER_STEP):
            bb_ri, bb_nir = par_ref[k, S5_BB_RI], par_ref[k, S5_BB_NIR]
            bb_ir, bb_rni = par_ref[k, S5_BB_IR], par_ref[k, S5_BB_RNI]
            blk_q, blk_s = [], []
            for t in range(q):
                pr, pi_ = lam_pow(k, q - 1 - t)
                blk_q.append(pr * bb_ri + pi_ * bb_nir)
                blk_s.append(pr * bb_ir + pi_ * bb_rni)
            wst = jnp.concatenate([jnp.concatenate(blk_q, axis=0), jnp.concatenate(blk_s, axis=0)], axis=1)
            r = jnp.dot(ut_ref[k], wst.astype(BF16), preferred_element_type=F32)
            sin_s[0, pl.ds(group_row0(k), S5_CHUNKS), :] = r[:, :n]
            sin_s[1, pl.ds(group_row0(k), S5_CHUNKS), :] = r[:, n:]

    @pl.when((phase == 1) & (g == 0))
    def _():
        a = a_ref[...]
        bq = bq_ref[...]
        bs = bs_ref[...]

        def body(c, carry):
            nxt = []
            for b in range(BATCH):
                x, xs = carry[b]
                rows = pl.ds(b * S5_CHUNKS_PER_SEQ + c, SSM_GROUPS, stride=S5_GROUP_ROWS)
                sp_s[rows, :] = x
                nxt.append((a * x + bq * xs + sin_s[0, rows, :], a * xs + bs * x + sin_s[1, rows, :]))
            return tuple(nxt)

        z = jnp.zeros((SSM_GROUPS, n), F32)
        lax.fori_loop(0, S5_CHUNKS_PER_SEQ, body, tuple((z, z) for _ in range(BATCH)))

    @pl.when(phase == 1)
    def _():
        for k in range(S5_GROUPS_PER_STEP):
            cc_ri, cc_nir = par_ref[k, S5_CC_RI], par_ref[k, S5_CC_NIR]
            cl = []
            for j in range(q + 1):
                pr, pi_ = lam_pow(k, j)
                cl.append(pr * cc_ri + pi_ * cc_nir)
            cl_lo = jnp.concatenate(cl[:q], axis=0)
            cl_hi = jnp.concatenate(cl[1:], axis=0)
            lane = lax.broadcasted_iota(jnp.int32, (1, n), 1)
            vgt = (cl_hi * jnp.where(lane < SSM_STATE, 1.0, -1.0)).astype(BF16)
            kt = _dot_nt_split(par_ref[k, S5_BB_RNI], cl_lo)
            padded = jnp.concatenate([jnp.zeros_like(kt), kt], axis=1)
            tg = jnp.concatenate(
                [padded[:, (q - t) * SSM_GROUP_CH:(q - t) * SSM_GROUP_CH + S5_QH] for t in range(q)],
                axis=0).astype(BF16)
            sp = sp_s[pl.ds(group_row0(k), S5_CHUNKS), :]
            y = jnp.dot(ut_ref[k], tg, preferred_element_type=F32)
            yt_ref[k] = y + lax.dot_general(sp.astype(BF16), vgt, (((1,), (1,)), ((), ())),
                                            preferred_element_type=F32)


def _s5(ut, pwr, pwi, params, a_cat, b_q, b_s):
    vec = pl.BlockSpec((SSM_GROUPS, 2 * SSM_STATE), lambda p, g: (0, 0))
    gs = S5_GROUPS_PER_STEP
    powers = pl.BlockSpec((gs, S5_POW_ROWS, 2 * SSM_STATE), lambda p, g: (g, 0, 0))
    return pl.pallas_call(
        _s5_kernel,
        grid=(2, SSM_GROUPS // gs),
        in_specs=[pl.BlockSpec((gs, S5_CHUNKS, S5_QH), lambda p, g: (g, 0, 0)),
                  powers, powers,
                  pl.BlockSpec((gs, S5_N_PARAM, SSM_GROUP_CH, 2 * SSM_STATE), lambda p, g: (g, 0, 0, 0)),
                  vec, vec, vec],
        out_specs=pl.BlockSpec((gs, S5_CHUNKS, S5_QH), lambda p, g: (g * p, 0, 0)),
        out_shape=jax.ShapeDtypeStruct((SSM_GROUPS, S5_CHUNKS, S5_QH), F32),
        scratch_shapes=[pltpu.VMEM((2, SSM_GROUPS * S5_GROUP_ROWS, 2 * SSM_STATE), F32),
                        pltpu.VMEM((SSM_GROUPS * S5_GROUP_ROWS, 2 * SSM_STATE), F32)],
        compiler_params=_cparams(("arbitrary", "arbitrary")),
        name="s5_chunked",
    )(ut, pwr, pwi, params, a_cat, b_q, b_s)


def _s5_operators(a_re, a_im, log_dt, b_re, b_im, c_re, c_im):
    q = S5_Q
    dt = jnp.exp(log_dt)[:, None]
    ar, ai = a_re, a_im
    mag = jnp.exp(ar * dt)
    lr = mag * jnp.cos(ai * dt)
    li = mag * jnp.sin(ai * dt)
    den = ar * ar + ai * ai
    nr = lr - 1.0
    kr = (nr * ar + li * ai) / den
    ki = (li * ar - nr * ai) / den
    bbr = kr[..., None] * b_re - ki[..., None] * b_im
    bbi = kr[..., None] * b_im + ki[..., None] * b_re
    j = jnp.arange(q + 1, dtype=F32)[None, :, None]
    pmag = jnp.exp(ar[:, None, :] * dt[:, :, None] * j)
    pang = ai[:, None, :] * dt[:, :, None] * j
    pr = pmag * jnp.cos(pang)
    pi_ = pmag * jnp.sin(pang)
    pad = ((0, 0), (0, S5_POW_ROWS - (q + 1)), (0, 0))
    pwr = jnp.pad(jnp.concatenate([pr, pr], axis=-1), pad)
    pwi = jnp.pad(jnp.concatenate([pi_, pi_], axis=-1), pad)
    br_t = bbr.transpose(0, 2, 1)
    bi_t = bbi.transpose(0, 2, 1)
    cat = lambda a, b: jnp.concatenate([a, b], axis=-1)
    stack = [None] * S5_N_PARAM
    stack[S5_BB_RI] = cat(br_t, bi_t)
    stack[S5_BB_NIR] = cat(-bi_t, br_t)
    stack[S5_BB_IR] = cat(bi_t, br_t)
    stack[S5_BB_RNI] = cat(br_t, -bi_t)
    stack[S5_CC_RI] = cat(c_re, c_im)
    stack[S5_CC_NIR] = cat(-c_im, c_re)
    params = jnp.stack(stack, axis=1)
    aq_r, aq_i = pr[:, q], pi_[:, q]
    a_cat = cat(aq_r, aq_r)
    b_q = cat(-aq_i, aq_i)
    b_s = cat(aq_i, -aq_i)
    return pwr, pwi, params, a_cat, b_q, b_s


def _gelu_tanh(x):
    return 0.5 * x * (1.0 + jnp.tanh(math.sqrt(2.0 / math.pi) * (x + 0.044715 * (x * x * x))))


def _mix_kernel(an_ref, yt_ref, u_ref, x_ref, mod_ref, d_ref, wglu_ref, bglu_ref, gos_ref,
                woa_ref, wob_ref, gpm_ref, gpf_ref, wr_ref, br_ref, tri_ref,
                x1_ref, h2_ref, eidx_ref, rank_ref, gw_ref, cnt_ref, run_ref, tile_ref):
    i = pl.program_id(0)
    tm = TM_MIX

    @pl.when(i == 0)
    def _():
        run_ref[...] = jnp.zeros_like(run_ref)

    gt_m = mod_ref[0, 2:3, :]
    sh_f = mod_ref[0, 3:4, :]
    sc_f = mod_ref[0, 4:5, :]

    yy = _from_group_chunks(yt_ref, tile_ref) + d_ref[...] * u_ref[...]
    g = _gelu_tanh(yy)
    gl = jnp.dot(g.astype(BF16), wglu_ref[...], preferred_element_type=F32) + bglu_ref[...]
    ob = g * jax.nn.sigmoid(gl)
    bn = _rms(ob, gos_ref[...]).astype(BF16)
    o = (jnp.dot(an_ref[...], woa_ref[...], preferred_element_type=F32)
         + jnp.dot(bn, wob_ref[...], preferred_element_type=F32))
    x1 = x_ref[...] + gt_m * _rms(o, gpm_ref[...])
    x1_ref[...] = x1
    h2 = _rms(x1, gpf_ref[...]) * (1.0 + sc_f) + sh_f
    h2_hi = h2.astype(BF16)
    h2_hi32 = h2_hi.astype(F32)
    h2_ref[...] = _pack_rounded_halves(h2_hi32)

    h2_lo = (h2 - h2_hi32).astype(BF16)
    nt = (((1,), (1,)), ((), ()))
    both = lax.dot_general(wr_ref[...], h2_hi, nt, preferred_element_type=F32)
    logits = (both[:N_EXPERTS] + both[N_EXPERTS:]
              + lax.dot_general(wr_ref[:N_EXPERTS, :], h2_lo, nt, preferred_element_type=F32))
    scores = jax.nn.sigmoid(logits)
    biased = scores + br_ref[...]
    ng = N_ROUTE_GROUPS
    gsz = N_EXPERTS // ng
    b3 = biased.reshape(ng, gsz, tm)
    s3 = scores.reshape(ng, gsz, tm)
    sub = lax.broadcasted_iota(jnp.int32, (ng, gsz, tm), 1).astype(F32)
    grp = lax.broadcasted_iota(jnp.int32, (ng, gsz, tm), 0).astype(F32)
    eid = grp * gsz + sub
    neg = -jnp.inf
    m1 = jnp.max(b3, axis=1, keepdims=True)
    i1 = jnp.min(jnp.where(b3 == m1, sub, float(gsz)), axis=1, keepdims=True)
    m2 = jnp.max(jnp.where(sub == i1, neg, b3), axis=1, keepdims=True)
    gs = m1 + m2
    gi = lax.broadcasted_iota(jnp.int32, (ng, 1, tm), 0)
    beaten = jnp.zeros((ng, 1, tm), F32)
    for gp in range(ng):
        o_ = gs[gp:gp + 1]
        beats = (o_ > gs) | ((o_ == gs) & (gi > gp))
        beaten = beaten + beats.astype(F32)
    gmask = beaten < float(TOPK_ROUTE_GROUPS)
    masked = jnp.where(gmask, b3, neg)

    sels = []
    picked = jnp.zeros((ng, gsz, tm), F32)
    for k in range(TOP_K):
        m = jnp.max(jnp.max(masked, axis=0, keepdims=True), axis=1, keepdims=True)
        cand = jnp.where(masked == m, eid, float(N_EXPERTS))
        sel = jnp.min(jnp.min(cand, axis=0, keepdims=True), axis=1, keepdims=True)
        oh = eid == sel
        masked = jnp.where(oh, neg, masked)
        picked = jnp.where(oh, 1.0, picked)
        sels.append(sel)

    pm = picked.reshape(N_EXPERTS, tm)
    prefix = jnp.dot(pm.astype(BF16), tri_ref[...], preferred_element_type=F32) + run_ref[:, 0:1]
    p3 = prefix.reshape(ng, gsz, tm)
    run_new = run_ref[...] + jnp.sum(pm, axis=1, keepdims=True)
    run_ref[...] = run_new
    cnt_ref[...] = run_new

    sc_rows = []
    for k in range(TOP_K):
        oh = eid == sels[k]
        sc_k = jnp.sum(jnp.sum(jnp.where(oh, s3, 0.0), axis=0, keepdims=True), axis=1, keepdims=True)
        rk_k = jnp.sum(jnp.sum(jnp.where(oh, p3, 0.0), axis=0, keepdims=True), axis=1, keepdims=True)
        sc_rows.append(sc_k)
        eidx_ref[k:k + 1, :] = sels[k].reshape(1, tm).astype(jnp.int32)
        rank_ref[k:k + 1, :] = rk_k.reshape(1, tm).astype(jnp.int32)
    tot = sc_rows[0]
    for k in range(1, TOP_K):
        tot = tot + sc_rows[k]
    inv = ROUTED_SCALE / (tot + 1e-20)
    for k in range(TOP_K):
        gw_ref[k:k + 1, :] = (sc_rows[k] * inv).reshape(1, tm)


def _mix(half, a_n, yt, u2, x2, mod, d_skip, wglu_bf, b_glu, g_out_ssm, wo_a, wo_b, g_post_mix,
         g_pre_ffn, w_router_split, b_router_col, tri):
    tm = TM_MIX
    t0 = half * (HALF_TOK // tm)
    row = lambda n: pl.BlockSpec((1, n), lambda i: (0, 0))
    full = lambda a, b: pl.BlockSpec((a, b), lambda i: (0, 0))
    tok_in = lambda n: pl.BlockSpec((tm, n), lambda i: (t0 + i, 0))
    tok = lambda n: pl.BlockSpec((tm, n), lambda i: (i, 0))
    col = pl.BlockSpec((TOP_K, tm), lambda i: (0, i))
    return pl.pallas_call(
        _mix_kernel,
        grid=(HALF_TOK // tm,),
        in_specs=[tok_in(CONV_CH),
                  pl.BlockSpec((SSM_GROUPS, tm // S5_Q, S5_QH), lambda i: (0, t0 + i, 0)),
                  tok_in(SSM_CH), tok_in(D_MODEL),
                  pl.BlockSpec((1, MOD_ROWS, D_MODEL), lambda i: (half, 0, 0)),
                  row(SSM_CH), full(SSM_CH, SSM_CH), row(SSM_CH), row(SSM_CH),
                  full(CONV_CH, D_MODEL), full(SSM_CH, D_MODEL), row(D_MODEL), row(D_MODEL),
                  full(2 * N_EXPERTS, D_MODEL), full(N_EXPERTS, 1), full(tm, tm)],
        out_specs=[tok(D_MODEL), tok(D_MODEL // 2), col, col, col,
                   pl.BlockSpec((N_EXPERTS, LANES), lambda i: (0, 0))],
        out_shape=[jax.ShapeDtypeStruct((HALF_TOK, D_MODEL), F32),
                   jax.ShapeDtypeStruct((HALF_TOK, D_MODEL // 2), jnp.uint32),
                   jax.ShapeDtypeStruct((TOP_K, HALF_TOK), jnp.int32),
                   jax.ShapeDtypeStruct((TOP_K, HALF_TOK), jnp.int32),
                   jax.ShapeDtypeStruct((TOP_K, HALF_TOK), F32),
                   jax.ShapeDtypeStruct((N_EXPERTS, LANES), F32)],
        scratch_shapes=[pltpu.VMEM((N_EXPERTS, LANES), F32),
                        pltpu.VMEM((SSM_CH // LANES, tm, LANES), F32)],
        compiler_params=_cparams(("arbitrary",)),
        name="mix_out_router",
    )(a_n, yt, u2, x2, mod, d_skip, wglu_bf, b_glu, g_out_ssm, wo_a, wo_b, g_post_mix,
      g_pre_ffn, w_router_split, b_router_col, tri)


def _expert_kernel(blk0_ref, nblk_ref, bunit_ref, bfull_ref, xs_hbm, wgu_ref, wd_ref, ys_hbm,
                   xbuf, ybuf, sem_in, sem_out):
    e = pl.program_id(0)
    n = nblk_ref[e]
    b0 = blk0_ref[e]
    n_all = blk0_ref[N_EXPERTS - 1] + nblk_ref[N_EXPERTS - 1]

    def copies(b, slot, rows):
        hbm = pl.ds(pl.multiple_of(bunit_ref[b] * ROW_UNIT, ROW_UNIT), rows)
        buf = pl.ds(0, rows)
        return (pltpu.make_async_copy(xs_hbm.at[hbm], xbuf.at[slot, buf], sem_in.at[slot]),
                pltpu.make_async_copy(ybuf.at[slot, buf], ys_hbm.at[hbm], sem_out.at[slot]))

    def for_size(b, fn):
        full = bfull_ref[b] == 1

        @pl.when(full)
        def _():
            fn(ROW_BLOCK)

        @pl.when(jnp.logical_not(full))
        def _():
            fn(ROW_UNIT)

    for b in range(EXPERT_AHEAD):
        @pl.when((e == 0) & (b < n_all))
        def _():
            for_size(b, lambda rows: copies(b, b, rows)[0].start())

    def compute(slot, rows):
        x_lo, x_hi = _unpack_halves(xbuf[slot, pl.ds(0, rows)])
        x = jnp.concatenate([x_lo.astype(BF16), x_hi.astype(BF16)], axis=1)
        h = jnp.dot(x, wgu_ref[0], preferred_element_type=F32)
        hg = h[:, :D_EXPERT]
        act = hg * jax.nn.sigmoid(hg) * h[:, D_EXPERT:]
        ybuf[slot, pl.ds(0, rows)] = _pack_halves(
            jnp.dot(act.astype(BF16), wd_ref[0], preferred_element_type=F32))

    def block(b, carry):
        slot = b % EXPERT_SLOTS
        ahead = b + EXPERT_AHEAD

        @pl.when(ahead < n_all)
        def _():
            for_size(ahead, lambda rows: copies(ahead, ahead % EXPERT_SLOTS, rows)[0].start())

        @pl.when(b >= EXPERT_SLOTS)
        def _():
            for_size(b - EXPERT_SLOTS, lambda rows: copies(b - EXPERT_SLOTS, slot, rows)[1].wait())

        def work(rows):
            cp_in, cp_out = copies(b, slot, rows)
            cp_in.wait()
            compute(slot, rows)
            cp_out.start()

        for_size(b, work)
        return carry

    lax.fori_loop(b0, b0 + n, block, 0)

    @pl.when(e == N_EXPERTS - 1)
    def _():
        for j in range(1, EXPERT_SLOTS + 1):
            @pl.when(n_all >= j)
            def _():
                last = n_all - j
                for_size(last, lambda rows: copies(last, last % EXPERT_SLOTS, rows)[1].wait())


def _experts(blk0, nblk, bunit, bfull, xs, we_gu, we_d):
    any_spec = pl.BlockSpec(memory_space=pl.ANY)
    grid_spec = pltpu.PrefetchScalarGridSpec(
        num_scalar_prefetch=4,
        grid=(N_EXPERTS,),
        in_specs=[any_spec,
                  pl.BlockSpec((1, D_MODEL, 2 * D_EXPERT), lambda e, *_: (e, 0, 0)),
                  pl.BlockSpec((1, D_EXPERT, D_MODEL), lambda e, *_: (e, 0, 0))],
        out_specs=any_spec,
        scratch_shapes=[pltpu.VMEM((EXPERT_SLOTS, ROW_BLOCK, D_MODEL // 2), jnp.uint32),
                        pltpu.VMEM((EXPERT_SLOTS, ROW_BLOCK, D_MODEL // 2), jnp.uint32),
                        pltpu.SemaphoreType.DMA((EXPERT_SLOTS,)),
                        pltpu.SemaphoreType.DMA((EXPERT_SLOTS,))],
    )
    return pl.pallas_call(
        _expert_kernel,
        grid_spec=grid_spec,
        out_shape=jax.ShapeDtypeStruct((N_ROWS, D_MODEL // 2), jnp.uint32),
        compiler_params=_cparams(("arbitrary",)),
        name="routed_experts",
    )(blk0, nblk, bunit, bfull, xs, we_gu, we_d)


def _final_kernel(h2_ref, yg_ref, gw_ref, x1_ref, mod_ref, wgu_ref, wd_ref, g_ref, *rest):
    o_ref = rest[-1]
    half = D_MODEL // 2
    gt_f = mod_ref[0, 5:6, :]
    x_lo, x_hi = _unpack_halves(h2_ref[...])
    h = (jnp.dot(x_lo.astype(BF16), wgu_ref[:half, :], preferred_element_type=F32)
         + jnp.dot(x_hi.astype(BF16), wgu_ref[half:, :], preferred_element_type=F32))
    hg = h[:, :D_EXPERT]
    act = hg * jax.nn.sigmoid(hg) * h[:, D_EXPERT:]
    shared = jnp.dot(act.astype(BF16), wd_ref[...], preferred_element_type=F32)
    y_lo = shared[:, :half]
    y_hi = shared[:, half:]
    gw = gw_ref[...]
    gw_cols = jnp.transpose(jnp.concatenate([gw, jnp.zeros((LANES - TOP_K, gw.shape[1]), F32)], axis=0))
    for k in range(TOP_K):
        r_lo, r_hi = _unpack_halves(yg_ref[k])
        w = gw_cols[:, k:k + 1]
        y_lo = y_lo + w * r_lo
        y_hi = y_hi + w * r_hi
    ms = (jnp.sum(y_lo * y_lo, axis=-1, keepdims=True)
          + jnp.sum(y_hi * y_hi, axis=-1, keepdims=True)) * (1.0 / D_MODEL)
    inv = lax.rsqrt(ms + NORM_EPS)
    o_ref[:, :half] = x1_ref[:, :half] + gt_f[:, :half] * (y_lo * inv * g_ref[:, :half])
    o_ref[:, half:] = x1_ref[:, half:] + gt_f[:, half:] * (y_hi * inv * g_ref[:, half:])


def _final(half, out_prev, h2p, yg, gw, x1, mod, ws_gu, ws_d, g_post_ffn):
    tm = TM_OUT
    t0 = half * (HALF_TOK // tm)
    tok = pl.BlockSpec((tm, D_MODEL), lambda i: (i, 0))
    in_specs = [pl.BlockSpec((tm, D_MODEL // 2), lambda i: (i, 0)),
                pl.BlockSpec((TOP_K, tm, D_MODEL // 2), lambda i: (0, i, 0)),
                pl.BlockSpec((TOP_K, tm), lambda i: (0, i)),
                tok,
                pl.BlockSpec((1, MOD_ROWS, D_MODEL), lambda i: (half, 0, 0)),
                pl.BlockSpec((D_MODEL, 2 * D_EXPERT), lambda i: (0, 0)),
                pl.BlockSpec((D_EXPERT, D_MODEL), lambda i: (0, 0)),
                pl.BlockSpec((1, D_MODEL), lambda i: (0, 0))]
    args = [h2p, yg, gw, x1, mod, ws_gu, ws_d, g_post_ffn]
    aliases = {}
    if out_prev is not None:
        aliases = {len(args): 0}
        in_specs.append(pl.BlockSpec(memory_space=pl.ANY))
        args.append(out_prev)
    return pl.pallas_call(
        _final_kernel,
        grid=(HALF_TOK // tm,),
        in_specs=in_specs,
        out_specs=pl.BlockSpec((tm, D_MODEL), lambda i: (t0 + i, 0)),
        out_shape=jax.ShapeDtypeStruct((N_TOK, D_MODEL), F32),
        input_output_aliases=aliases,
        compiler_params=_cparams(("parallel",)),
        name="shared_final",
    )(*args)


def _sc_worker_id():
    return lax.axis_index("s") * SC_CORES + lax.axis_index("c")


def _dispatch_body(h_hbm, dest_hbm, xs_hbm, idx_v, rows_v, sem_l, sem_s):
    n = SC_CHUNKS_PER_WORKER
    c0 = _sc_worker_id() * n

    def load(i, b):
        return pltpu.async_copy(h_hbm.at[pl.ds((c0 + i) * SC_W, SC_W)], rows_v.at[b], sem_l.at[b])

    loads = [None] * n
    scat = [None] * n
    loads[0] = load(0, 0)
    for i in range(n):
        b = i % 2
        pltpu.sync_copy(dest_hbm.at[c0 + i], idx_v.at[b])
        loads[i].wait()
        if i + 1 < n:
            if i >= 1:
                for d in scat[i - 1]:
                    d.wait()
            loads[i + 1] = load(i + 1, 1 - b)
        scat[i] = [pltpu.async_copy(rows_v.at[b], xs_hbm.at[idx_v.at[b].at[k]], sem_s.at[b])
                   for k in range(TOP_K)]
    for i in (n - 2, n - 1):
        for d in scat[i]:
            d.wait()


def _sc_dispatch(h2p, dest3):
    mesh = plsc.VectorSubcoreMesh(core_axis_name="c", subcore_axis_name="s")
    return pl.kernel(
        _dispatch_body, mesh=mesh,
        out_type=jax.ShapeDtypeStruct((N_ROWS, D_MODEL // 2), jnp.uint32),
        scratch_types=[pltpu.VMEM((2, TOP_K, SC_W), jnp.int32),
                       pltpu.VMEM((2, SC_W, D_MODEL // 2), jnp.uint32),
                       pltpu.SemaphoreType.DMA((2,)), pltpu.SemaphoreType.DMA((2,))],
    )(h2p, dest3)


def _combine_body(ys_hbm, dest_hbm, yg_hbm, idx_v, rows_v, sem_g, sem_w):
    c0 = _sc_worker_id() * SC_CHUNKS_PER_WORKER

    @pl.loop(0, SC_CHUNKS_PER_WORKER)
    def _(i):
        c = c0 + i
        pltpu.sync_copy(dest_hbm.at[c], idx_v)
        g = [None] * TOP_K
        w = [None] * TOP_K
        g[0] = pltpu.async_copy(ys_hbm.at[idx_v.at[0]], rows_v.at[0], sem_g.at[0])
        for k in range(TOP_K):
            b = k % 2
            g[k].wait()
            if k + 1 < TOP_K:
                if k >= 1:
                    w[k - 1].wait()
                g[k + 1] = pltpu.async_copy(ys_hbm.at[idx_v.at[k + 1]], rows_v.at[1 - b], sem_g.at[1 - b])
            w[k] = pltpu.async_copy(rows_v.at[b], yg_hbm.at[k].at[pl.ds(c * SC_W, SC_W)], sem_w.at[b])
        w[TOP_K - 2].wait()
        w[TOP_K - 1].wait()


def _sc_combine(ysp, dest3):
    mesh = plsc.VectorSubcoreMesh(core_axis_name="c", subcore_axis_name="s")
    return pl.kernel(
        _combine_body, mesh=mesh,
        out_type=jax.ShapeDtypeStruct((TOP_K, HALF_TOK, D_MODEL // 2), jnp.uint32),
        scratch_types=[pltpu.VMEM((TOP_K, SC_W), jnp.int32),
                       pltpu.VMEM((2, SC_W, D_MODEL // 2), jnp.uint32),
                       pltpu.SemaphoreType.DMA((2,)), pltpu.SemaphoreType.DMA((2,))],
    )(ysp, dest3)


def kernel(x, c, w_ada, b_ada, g_pre_mix, g_post_mix, w_in, conv_w, conv_b, conv_ln_g, conv_ln_b,
           ssm_a_re, ssm_a_im, ssm_log_dt, ssm_b_re, ssm_b_im, ssm_c_re, ssm_c_im, ssm_d,
           ssm_w_glu, ssm_b_glu, g_out_conv, g_out_ssm, w_out, g_pre_ffn, g_post_ffn,
           w_router, b_router, we_gate, we_up, we_down, ws_gate, ws_up, ws_down):
    l = 0
    x2 = x.reshape(N_TOK, D_MODEL)
    r1 = lambda a: a.reshape(1, -1)

    c_pad = jnp.zeros((SUBLANES, D_MODEL), F32).at[:BATCH].set(c)
    mod = _ada(c_pad, w_ada[l], r1(b_ada[l]))[:BATCH].reshape(BATCH, N_MOD, D_MODEL)
    mod = jnp.concatenate([mod, jnp.zeros((BATCH, MOD_ROWS - N_MOD, D_MODEL), F32)], axis=1)

    v, u, ut = _inproj(x2, mod, r1(g_pre_mix[l]), w_in[l].astype(BF16))
    cw = jnp.concatenate([conv_w[l].reshape(CONV_WIDTH, CONV_CH), jnp.zeros((1, CONV_CH), F32)], axis=0)
    a_n, we_gu, we_d = _conv(v.reshape(BATCH, SEQ, CONV_CH), cw, r1(conv_b[l]), r1(conv_ln_g[l]),
                             r1(conv_ln_b[l]), r1(g_out_conv[l]), we_gate[l], we_up[l], we_down[l])
    a_n = a_n.reshape(N_TOK, CONV_CH)

    pwr, pwi, s5_params, a_cat, b_q, b_s = _s5_operators(
        ssm_a_re[l], ssm_a_im[l], ssm_log_dt[l], ssm_b_re[l], ssm_b_im[l], ssm_c_re[l], ssm_c_im[l])
    yt = _s5(ut, pwr, pwi, s5_params, a_cat, b_q, b_s)

    tm = TM_MIX
    tri = (jnp.arange(tm)[:, None] < jnp.arange(tm)[None, :]).astype(BF16)
    wo = w_out[l].astype(BF16)
    wr_t = w_router[l].T
    wr_hi = wr_t.astype(BF16)
    wr_split = jnp.concatenate([wr_hi, (wr_t - wr_hi.astype(F32)).astype(BF16)], axis=0)
    mix_params = (r1(ssm_d[l]), ssm_w_glu[l].astype(BF16), r1(ssm_b_glu[l]), r1(g_out_ssm[l]),
                  wo[:CONV_CH], wo[CONV_CH:], r1(g_post_mix[l]), r1(g_pre_ffn[l]),
                  wr_split, b_router[l].reshape(N_EXPERTS, 1), tri)
    ws_gu = jnp.concatenate([ws_gate[l], ws_up[l]], axis=1).astype(BF16)
    ws_d = ws_down[l].astype(BF16)
    e_ids = jnp.arange(N_EXPERTS, dtype=jnp.int32)

    out = None
    for half in range(N_HALVES):
        x1, h2, eidx, rank, gw, cnt = _mix(half, a_n, yt, u, x2, mod, *mix_params)
        counts = cnt[:, 0].astype(jnp.int32)
        units = (counts + ROW_UNIT - 1) // ROW_UNIT
        unit0 = jnp.cumsum(units) - units
        nblk = (units + 1) // 2
        blk0 = jnp.cumsum(nblk) - nblk
        dest = rank + jnp.sum(jnp.where(eidx[..., None] == e_ids, unit0 * ROW_UNIT, 0), axis=-1)
        dest3 = dest.reshape(TOP_K, HALF_TOK // SC_W, SC_W).transpose(1, 0, 2)
        b_ids = jnp.arange(N_BLOCKS, dtype=jnp.int32)
        owner = (b_ids[:, None] >= blk0[None, :]) & (b_ids[:, None] < (blk0 + nblk)[None, :])
        pick = lambda v: jnp.sum(jnp.where(owner, v[None, :], 0), axis=1)
        local = b_ids - pick(blk0)
        bunit = pick(unit0) + 2 * local
        bfull = (2 * local + 2 <= pick(units)).astype(jnp.int32)

        xs = _sc_dispatch(h2, dest3)
        ys = _experts(blk0, nblk, bunit, bfull, xs, we_gu, we_d)
        yg = _sc_combine(ys, dest3)
        out = _final(half, out, h2, yg, gw, x1, mod, ws_gu, ws_d, r1(g_post_ffn[l]))
    return out.reshape(BATCH, SEQ, D_MODEL)
```

```python
import math

import jax
import jax.numpy as jnp
from jax import lax
from jax.experimental import pallas as pl
from jax.experimental.pallas import tpu as pltpu
from jax.experimental.pallas import tpu_sc as plsc

F32 = jnp.float32
BF16 = jnp.bfloat16

D_MODEL = 1024
BATCH = 2
SEQ = 8192
N_TOK = BATCH * SEQ
CONV_CH = 512
CONV_WIDTH = 31
SSM_CH = 512
SSM_GROUP_CH = 16
SSM_GROUPS = 32
SSM_STATE = 64
D_IN = 2 * CONV_CH + SSM_CH
N_EXPERTS = 64
TOP_K = 8
N_ROUTE_GROUPS = 8
TOPK_ROUTE_GROUPS = 4
D_EXPERT = 256
ROUTED_SCALE = 2.5
NORM_EPS = 1e-6

SUBLANES = 8
LANES = 128

N_MOD = 6
MOD_ROWS = SUBLANES
ADA_COLS = 3072
TM_IN = 1024
IN_SUBTILES = 4
TL_CONV = 512
CONV_HALO = 32
CONV_ROWS = 256
EXPERTS_PER_CONV_STEP = N_EXPERTS * TL_CONV // N_TOK
assert EXPERTS_PER_CONV_STEP * N_TOK == N_EXPERTS * TL_CONV
S5_Q = 32
S5_GROUPS_PER_STEP = 4
S5_QH = S5_Q * SSM_GROUP_CH
S5_CHUNKS = N_TOK // S5_Q
S5_CHUNKS_PER_SEQ = SEQ // S5_Q
TM_MIX = 1024
ROW_BLOCK = 512
ROW_UNIT = ROW_BLOCK // 2
EXPERT_AHEAD = 4
EXPERT_SLOTS = EXPERT_AHEAD + 1
HALF_TOK = SEQ
N_HALVES = N_TOK // HALF_TOK
N_UNITS = HALF_TOK * TOP_K // ROW_UNIT + N_EXPERTS
N_BLOCKS = (N_UNITS + N_EXPERTS) // 2
N_ROWS = N_UNITS * ROW_UNIT
TM_OUT = 512
SC_CORES = 2
SC_SUBCORES = 16
SC_WORKERS = SC_CORES * SC_SUBCORES
SC_W = 64
SC_CHUNKS_PER_WORKER = HALF_TOK // (SC_WORKERS * SC_W)
VMEM_LIMIT = 48 * 1024 * 1024


def _cparams(sem):
    return pltpu.CompilerParams(dimension_semantics=sem, vmem_limit_bytes=VMEM_LIMIT)


def _pack_rounded_halves(xr):
    n = xr.shape[-1] // 2
    lo = lax.bitcast_convert_type(xr[:, :n], jnp.uint32)
    hi = lax.bitcast_convert_type(xr[:, n:], jnp.uint32)
    return hi | (lo >> 16)


def _pack_halves(x):
    return _pack_rounded_halves(x.astype(BF16).astype(F32))


def _unpack_halves(p):
    lo = lax.bitcast_convert_type(p << 16, F32)
    hi = lax.bitcast_convert_type(p & jnp.uint32(0xFFFF0000), F32)
    return lo, hi


def _rms(x, g):
    return x * lax.rsqrt(jnp.mean(x * x, axis=-1, keepdims=True) + NORM_EPS) * g


def _split_bf16(x):
    hi = x.astype(BF16)
    return hi, (x - hi.astype(F32)).astype(BF16)


def _dot_nt_split(a, b):
    nt = (((1,), (1,)), ((), ()))
    a_hi, a_lo = _split_bf16(a)
    b_hi, b_lo = _split_bf16(b)
    m = a.shape[0]
    both = lax.dot_general(jnp.concatenate([a_hi, a_lo], axis=0), b_hi, nt, preferred_element_type=F32)
    return both[:m] + both[m:] + lax.dot_general(a_hi, b_lo, nt, preferred_element_type=F32)


def _ada_kernel(c_ref, w_ref, b_ref, o_ref):
    c = c_ref[...]
    a_hi, a_lo = _split_bf16(c * jax.nn.sigmoid(c))
    w_hi, w_lo = _split_bf16(w_ref[...])
    both = jnp.dot(jnp.concatenate([a_hi, a_lo], axis=0), w_hi, preferred_element_type=F32)
    o_ref[...] = (both[:SUBLANES] + both[SUBLANES:]
                  + jnp.dot(a_hi, w_lo, preferred_element_type=F32) + b_ref[...])


def _ada(c_pad, w_ada, b_ada):
    n = w_ada.shape[1]
    bn = ADA_COLS
    return pl.pallas_call(
        _ada_kernel,
        grid=(n // bn,),
        in_specs=[pl.BlockSpec((SUBLANES, D_MODEL), lambda j: (0, 0)),
                  pl.BlockSpec((D_MODEL, bn), lambda j: (0, j)),
                  pl.BlockSpec((1, bn), lambda j: (0, j))],
        out_specs=pl.BlockSpec((SUBLANES, bn), lambda j: (0, j)),
        out_shape=jax.ShapeDtypeStruct((SUBLANES, n), F32),
        compiler_params=_cparams(("arbitrary",)),
        name="ada_mod",
    )(c_pad, w_ada, b_ada)


GROUPS_PER_LANE_TILE = LANES // SSM_GROUP_CH


def _to_group_chunks(u, tile_ref, ut_ref):
    n_chunks = u.shape[0] // S5_Q
    for j in range(SSM_CH // LANES):
        tile_ref[j] = u[:, LANES * j:LANES * (j + 1)]
    for j in range(SSM_CH // LANES):
        rows_t = [tile_ref[j, pl.ds(t, n_chunks, stride=S5_Q), :] for t in range(S5_Q)]
        for gg in range(GROUPS_PER_LANE_TILE):
            lo = gg * SSM_GROUP_CH
            row = jnp.concatenate([r[:, lo:lo + SSM_GROUP_CH] for r in rows_t], axis=1)
            ut_ref[j * GROUPS_PER_LANE_TILE + gg] = row.astype(ut_ref.dtype)


def _from_group_chunks(yt_ref, tile_ref):
    n_chunks = yt_ref.shape[1]
    for j in range(SSM_CH // LANES):
        for t in range(S5_Q):
            lo = t * SSM_GROUP_CH
            piece = jnp.concatenate(
                [yt_ref[j * GROUPS_PER_LANE_TILE + gg, :, lo:lo + SSM_GROUP_CH]
                 for gg in range(GROUPS_PER_LANE_TILE)], axis=1)
            tile_ref[j, pl.ds(t, n_chunks, stride=S5_Q), :] = piece
    return jnp.concatenate([tile_ref[j] for j in range(SSM_CH // LANES)], axis=1)


def _inproj_kernel(x_ref, mod_ref, g_ref, w_ref, v_ref, u_ref, ut_ref, tile_ref):
    sh = mod_ref[0, 0:1, :]
    sc = mod_ref[0, 1:2, :]
    sub = TM_IN // IN_SUBTILES
    sub_chunks = sub // S5_Q
    for s in range(IN_SUBTILES):
        r = slice(s * sub, (s + 1) * sub)
        h = _rms(x_ref[r, :], g_ref[...]) * (1.0 + sc) + sh
        z = jnp.dot(h.astype(BF16), w_ref[...], preferred_element_type=F32)
        v_ref[r, :] = z[:, :CONV_CH] * jax.nn.sigmoid(z[:, CONV_CH:2 * CONV_CH])
        u = z[:, 2 * CONV_CH:]
        u_ref[r, :] = u
        _to_group_chunks(u, tile_ref.at[s], ut_ref.at[:, s * sub_chunks:(s + 1) * sub_chunks, :])


def _inproj(x2, mod, g_pre, w_in_bf):
    tiles_per_seq = SEQ // TM_IN
    return pl.pallas_call(
        _inproj_kernel,
        grid=(N_TOK // TM_IN,),
        in_specs=[pl.BlockSpec((TM_IN, D_MODEL), lambda i: (i, 0)),
                  pl.BlockSpec((1, MOD_ROWS, D_MODEL), lambda i: (i // tiles_per_seq, 0, 0)),
                  pl.BlockSpec((1, D_MODEL), lambda i: (0, 0)),
                  pl.BlockSpec((D_MODEL, D_IN), lambda i: (0, 0))],
        out_specs=[pl.BlockSpec((TM_IN, CONV_CH), lambda i: (i, 0)),
                   pl.BlockSpec((TM_IN, SSM_CH), lambda i: (i, 0)),
                   pl.BlockSpec((SSM_GROUPS, TM_IN // S5_Q, S5_QH), lambda i: (0, i, 0))],
        out_shape=[jax.ShapeDtypeStruct((N_TOK, CONV_CH), F32),
                   jax.ShapeDtypeStruct((N_TOK, SSM_CH), F32),
                   jax.ShapeDtypeStruct((SSM_GROUPS, S5_CHUNKS, S5_QH), BF16)],
        scratch_shapes=[pltpu.VMEM((IN_SUBTILES, SSM_CH // LANES, TM_IN // IN_SUBTILES, LANES), F32)],
        compiler_params=_cparams(("parallel",)),
        name="in_proj",
    )(x2, mod, g_pre, w_in_bf)


def _conv_kernel(vc_ref, vp_ref, w_ref, cb_ref, lg_ref, lb_ref, go_ref, wg_ref, wu_ref, wd_ref,
                 o_ref, wgu_o, wd_o, sh_ref):
    for q in range(EXPERTS_PER_CONV_STEP):
        wgu_o[q, :, :D_EXPERT] = wg_ref[q].astype(BF16)
        wgu_o[q, :, D_EXPERT:] = wu_ref[q].astype(BF16)
        wd_o[q] = wd_ref[q].astype(BF16)

    i = pl.program_id(1)
    keep = (i > 0).astype(F32)
    n_ext = TL_CONV + CONV_HALO
    sh_ref[0, 0:CONV_HALO, :] = vp_ref[0] * keep
    sh_ref[0, CONV_HALO:, :] = vc_ref[0]
    ext = sh_ref[0]
    for s in range(1, SUBLANES):
        sh_ref[s] = pltpu.roll(ext, n_ext - s, axis=0)
    off = CONV_HALO - (CONV_WIDTH - 1)
    for r in range(TL_CONV // CONV_ROWS):
        acc = None
        for j in range(CONV_WIDTH):
            s = (off + j) % SUBLANES
            al = r * CONV_ROWS + (off + j) - s
            term = w_ref[j:j + 1, :] * sh_ref[s, al:al + CONV_ROWS, :]
            acc = term if acc is None else acc + term
        y = acc + cb_ref[...]
        mu = jnp.mean(y, axis=-1, keepdims=True)
        d = y - mu
        var = jnp.mean(d * d, axis=-1, keepdims=True)
        yn = d * lax.rsqrt(var + NORM_EPS) * lg_ref[...] + lb_ref[...]
        a = yn * jax.nn.sigmoid(yn)
        o_ref[0, r * CONV_ROWS:(r + 1) * CONV_ROWS, :] = _rms(a, go_ref[...]).astype(BF16)


def _conv(v3, conv_w, conv_b, ln_g, ln_b, g_out, we_gate, we_up, we_down):
    halo_per_tile = TL_CONV // CONV_HALO
    steps_per_seq = SEQ // TL_CONV
    vec = pl.BlockSpec((1, CONV_CH), lambda b, i: (0, 0))
    ex = EXPERTS_PER_CONV_STEP
    w_in = pl.BlockSpec((ex, D_MODEL, D_EXPERT), lambda b, i: (b * steps_per_seq + i, 0, 0))
    return pl.pallas_call(
        _conv_kernel,
        grid=(BATCH, steps_per_seq),
        in_specs=[pl.BlockSpec((1, TL_CONV, CONV_CH), lambda b, i: (b, i, 0)),
                  pl.BlockSpec((1, CONV_HALO, CONV_CH),
                               lambda b, i: (b, jnp.maximum(i * halo_per_tile - 1, 0), 0)),
                  pl.BlockSpec((CONV_WIDTH + 1, CONV_CH), lambda b, i: (0, 0)),
                  vec, vec, vec, vec,
                  w_in, w_in,
                  pl.BlockSpec((ex, D_EXPERT, D_MODEL), lambda b, i: (b * steps_per_seq + i, 0, 0))],
        out_specs=[pl.BlockSpec((1, TL_CONV, CONV_CH), lambda b, i: (b, i, 0)),
                   pl.BlockSpec((ex, D_MODEL, 2 * D_EXPERT), lambda b, i: (b * steps_per_seq + i, 0, 0)),
                   pl.BlockSpec((ex, D_EXPERT, D_MODEL), lambda b, i: (b * steps_per_seq + i, 0, 0))],
        out_shape=[jax.ShapeDtypeStruct((BATCH, SEQ, CONV_CH), BF16),
                   jax.ShapeDtypeStruct((N_EXPERTS, D_MODEL, 2 * D_EXPERT), BF16),
                   jax.ShapeDtypeStruct((N_EXPERTS, D_EXPERT, D_MODEL), BF16)],
        scratch_shapes=[pltpu.VMEM((SUBLANES, TL_CONV + CONV_HALO, CONV_CH), F32)],
        compiler_params=_cparams(("parallel", "arbitrary")),
        name="conv_module",
    )(v3, v3, conv_w, conv_b, ln_g, ln_b, g_out, we_gate, we_up, we_down)


S5_GROUP_ROWS = S5_CHUNKS + SUBLANES


S5_POW_ROWS = (S5_Q + 1 + SUBLANES - 1) // SUBLANES * SUBLANES
(S5_BB_RI, S5_BB_NIR, S5_BB_IR, S5_BB_RNI, S5_CC_RI, S5_CC_NIR, S5_N_PARAM) = range(7)


def _s5_kernel(ut_ref, pwr_ref, pwi_ref, par_ref, a_ref, bq_ref, bs_ref, yt_ref, sin_s, sp_s):
    phase = pl.program_id(0)
    g = pl.program_id(1)
    q = S5_Q
    n = 2 * SSM_STATE

    def group_row0(k):
        return pl.multiple_of((g * S5_GROUPS_PER_STEP + k) * S5_GROUP_ROWS, SUBLANES)

    def lam_pow(k, j):
        return pwr_ref[k, j:j + 1, :], pwi_ref[k, j:j + 1, :]

    @pl.when(phase == 0)
    def _():
        for k in range(S5_GROUPS_PER_STEP):
            bb_ri, bb_nir = par_ref[k, S5_BB_RI], par_ref[k, S5_BB_NIR]
            bb_ir, bb_rni = par_ref[k, S5_BB_IR], par_ref[k, S5_BB_RNI]
            blk_q, blk_s = [], []
            for t in range(q):
                pr, pi_ = lam_pow(k, q - 1 - t)
                blk_q.append(pr * bb_ri + pi_ * bb_nir)
                blk_s.append(pr * bb_ir + pi_ * bb_rni)
            wst = jnp.concatenate([jnp.concatenate(blk_q, axis=0), jnp.concatenate(blk_s, axis=0)], axis=1)
            r = jnp.dot(ut_ref[k], wst.astype(BF16), preferred_element_type=F32)
            sin_s[0, pl.ds(group_row0(k), S5_CHUNKS), :] = r[:, :n]
            sin_s[1, pl.ds(group_row0(k), S5_CHUNKS), :] = r[:, n:]

    @pl.when((phase == 1) & (g == 0))
    def _():
        a = a_ref[...]
        bq = bq_ref[...]
        bs = bs_ref[...]

        def body(c, carry):
            nxt = []
            for b in range(BATCH):
                x, xs = carry[b]
                rows = pl.ds(b * S5_CHUNKS_PER_SEQ + c, SSM_GROUPS, stride=S5_GROUP_ROWS)
                sp_s[rows, :] = x
                nxt.append((a * x + bq * xs + sin_s[0, rows, :], a * xs + bs * x + sin_s[1, rows, :]))
            return tuple(nxt)

        z = jnp.zeros((SSM_GROUPS, n), F32)
        lax.fori_loop(0, S5_CHUNKS_PER_SEQ, body, tuple((z, z) for _ in range(BATCH)))

    @pl.when(phase == 1)
    def _():
        for k in range(S5_GROUPS_PER_STEP):
            cc_ri, cc_nir = par_ref[k, S5_CC_RI], par_ref[k, S5_CC_NIR]
            cl = []
            for j in range(q + 1):
                pr, pi_ = lam_pow(k, j)
                cl.append(pr * cc_ri + pi_ * cc_nir)
            cl_lo = jnp.concatenate(cl[:q], axis=0)
            cl_hi = jnp.concatenate(cl[1:], axis=0)
            lane = lax.broadcasted_iota(jnp.int32, (1, n), 1)
            vgt = (cl_hi * jnp.where(lane < SSM_STATE, 1.0, -1.0)).astype(BF16)
            kt = _dot_nt_split(par_ref[k, S5_BB_RNI], cl_lo)
            padded = jnp.concatenate([jnp.zeros_like(kt), kt], axis=1)
            tg = jnp.concatenate(
                [padded[:, (q - t) * SSM_GROUP_CH:(q - t) * SSM_GROUP_CH + S5_QH] for t in range(q)],
                axis=0).astype(BF16)
            sp = sp_s[pl.ds(group_row0(k), S5_CHUNKS), :]
            y = jnp.dot(ut_ref[k], tg, preferred_element_type=F32)
            yt_ref[k] = y + lax.dot_general(sp.astype(BF16), vgt, (((1,), (1,)), ((), ())),
                                            preferred_element_type=F32)


def _s5(ut, pwr, pwi, params, a_cat, b_q, b_s):
    vec = pl.BlockSpec((SSM_GROUPS, 2 * SSM_STATE), lambda p, g: (0, 0))
    gs = S5_GROUPS_PER_STEP
    powers = pl.BlockSpec((gs, S5_POW_ROWS, 2 * SSM_STATE), lambda p, g: (g, 0, 0))
    return pl.pallas_call(
        _s5_kernel,
        grid=(2, SSM_GROUPS // gs),
        in_specs=[pl.BlockSpec((gs, S5_CHUNKS, S5_QH), lambda p, g: (g, 0, 0)),
                  powers, powers,
                  pl.BlockSpec((gs, S5_N_PARAM, SSM_GROUP_CH, 2 * SSM_STATE), lambda p, g: (g, 0, 0, 0)),
                  vec, vec, vec],
        out_specs=pl.BlockSpec((gs, S5_CHUNKS, S5_QH), lambda p, g: (g * p, 0, 0)),
        out_shape=jax.ShapeDtypeStruct((SSM_GROUPS, S5_CHUNKS, S5_QH), F32),
        scratch_shapes=[pltpu.VMEM((2, SSM_GROUPS * S5_GROUP_ROWS, 2 * SSM_STATE), F32),
                        pltpu.VMEM((SSM_GROUPS * S5_GROUP_ROWS, 2 * SSM_STATE), F32)],
        compiler_params=_cparams(("arbitrary", "arbitrary")),
        name="s5_chunked",
    )(ut, pwr, pwi, params, a_cat, b_q, b_s)


def _s5_operators(a_re, a_im, log_dt, b_re, b_im, c_re, c_im):
    q = S5_Q
    dt = jnp.exp(log_dt)[:, None]
    ar, ai = a_re, a_im
    mag = jnp.exp(ar * dt)
    lr = mag * jnp.cos(ai * dt)
    li = mag * jnp.sin(ai * dt)
    den = ar * ar + ai * ai
    nr = lr - 1.0
    kr = (nr * ar + li * ai) / den
    ki = (li * ar - nr * ai) / den
    bbr = kr[..., None] * b_re - ki[..., None] * b_im
    bbi = kr[..., None] * b_im + ki[..., None] * b_re
    j = jnp.arange(q + 1, dtype=F32)[None, :, None]
    pmag = jnp.exp(ar[:, None, :] * dt[:, :, None] * j)
    pang = ai[:, None, :] * dt[:, :, None] * j
    pr = pmag * jnp.cos(pang)
    pi_ = pmag * jnp.sin(pang)
    pad = ((0, 0), (0, S5_POW_ROWS - (q + 1)), (0, 0))
    pwr = jnp.pad(jnp.concatenate([pr, pr], axis=-1), pad)
    pwi = jnp.pad(jnp.concatenate([pi_, pi_], axis=-1), pad)
    br_t = bbr.transpose(0, 2, 1)
    bi_t = bbi.transpose(0, 2, 1)
    cat = lambda a, b: jnp.concatenate([a, b], axis=-1)
    stack = [None] * S5_N_PARAM
    stack[S5_BB_RI] = cat(br_t, bi_t)
    stack[S5_BB_NIR] = cat(-bi_t, br_t)
    stack[S5_BB_IR] = cat(bi_t, br_t)
    stack[S5_BB_RNI] = cat(br_t, -bi_t)
    stack[S5_CC_RI] = cat(c_re, c_im)
    stack[S5_CC_NIR] = cat(-c_im, c_re)
    params = jnp.stack(stack, axis=1)
    aq_r, aq_i = pr[:, q], pi_[:, q]
    a_cat = cat(aq_r, aq_r)
    b_q = cat(-aq_i, aq_i)
    b_s = cat(aq_i, -aq_i)
    return pwr, pwi, params, a_cat, b_q, b_s


def _gelu_tanh(x):
    return 0.5 * x * (1.0 + jnp.tanh(math.sqrt(2.0 / math.pi) * (x + 0.044715 * (x * x * x))))


def _mix_kernel(an_ref, yt_ref, u_ref, x_ref, mod_ref, d_ref, wglu_ref, bglu_ref, gos_ref,
                woa_ref, wob_ref, gpm_ref, gpf_ref, wr_ref, br_ref, tri_ref,
                x1_ref, h2_ref, eidx_ref, rank_ref, gw_ref, cnt_ref, run_ref, tile_ref):
    i = pl.program_id(0)
    tm = TM_MIX

    @pl.when(i == 0)
    def _():
        run_ref[...] = jnp.zeros_like(run_ref)

    gt_m = mod_ref[0, 2:3, :]
    sh_f = mod_ref[0, 3:4, :]
    sc_f = mod_ref[0, 4:5, :]

    yy = _from_group_chunks(yt_ref, tile_ref) + d_ref[...] * u_ref[...]
    g = _gelu_tanh(yy)
    gl = jnp.dot(g.astype(BF16), wglu_ref[...], preferred_element_type=F32) + bglu_ref[...]
    ob = g * jax.nn.sigmoid(gl)
    bn = _rms(ob, gos_ref[...]).astype(BF16)
    o = (jnp.dot(an_ref[...], woa_ref[...], preferred_element_type=F32)
         + jnp.dot(bn, wob_ref[...], preferred_element_type=F32))
    x1 = x_ref[...] + gt_m * _rms(o, gpm_ref[...])
    x1_ref[...] = x1
    h2 = _rms(x1, gpf_ref[...]) * (1.0 + sc_f) + sh_f
    h2_hi = h2.astype(BF16)
    h2_hi32 = h2_hi.astype(F32)
    h2_ref[...] = _pack_rounded_halves(h2_hi32)

    h2_lo = (h2 - h2_hi32).astype(BF16)
    nt = (((1,), (1,)), ((), ()))
    both = lax.dot_general(wr_ref[...], h2_hi, nt, preferred_element_type=F32)
    logits = (both[:N_EXPERTS] + both[N_EXPERTS:]
              + lax.dot_general(wr_ref[:N_EXPERTS, :], h2_lo, nt, preferred_element_type=F32))
    scores = jax.nn.sigmoid(logits)
    biased = scores + br_ref[...]
    ng = N_ROUTE_GROUPS
    gsz = N_EXPERTS // ng
    b3 = biased.reshape(ng, gsz, tm)
    s3 = scores.reshape(ng, gsz, tm)
    sub = lax.broadcasted_iota(jnp.int32, (ng, gsz, tm), 1).astype(F32)
    grp = lax.broadcasted_iota(jnp.int32, (ng, gsz, tm), 0).astype(F32)
    eid = grp * gsz + sub
    neg = -jnp.inf
    m1 = jnp.max(b3, axis=1, keepdims=True)
    i1 = jnp.min(jnp.where(b3 == m1, sub, float(gsz)), axis=1, keepdims=True)
    m2 = jnp.max(jnp.where(sub == i1, neg, b3), axis=1, keepdims=True)
    gs = m1 + m2
    gi = lax.broadcasted_iota(jnp.int32, (ng, 1, tm), 0)
    beaten = jnp.zeros((ng, 1, tm), F32)
    for gp in range(ng):
        o_ = gs[gp:gp + 1]
        beats = (o_ > gs) | ((o_ == gs) & (gi > gp))
        beaten = beaten + beats.astype(F32)
    gmask = beaten < float(TOPK_ROUTE_GROUPS)
    masked = jnp.where(gmask, b3, neg)

    sels = []
    picked = jnp.zeros((ng, gsz, tm), F32)
    for k in range(TOP_K):
        m = jnp.max(jnp.max(masked, axis=0, keepdims=True), axis=1, keepdims=True)
        cand = jnp.where(masked == m, eid, float(N_EXPERTS))
        sel = jnp.min(jnp.min(cand, axis=0, keepdims=True), axis=1, keepdims=True)
        oh = eid == sel
        masked = jnp.where(oh, neg, masked)
        picked = jnp.where(oh, 1.0, picked)
        sels.append(sel)

    pm = picked.reshape(N_EXPERTS, tm)
    prefix = jnp.dot(pm.astype(BF16), tri_ref[...], preferred_element_type=F32) + run_ref[:, 0:1]
    p3 = prefix.reshape(ng, gsz, tm)
    run_new = run_ref[...] + jnp.sum(pm, axis=1, keepdims=True)
    run_ref[...] = run_new
    cnt_ref[...] = run_new

    sc_rows = []
    for k in range(TOP_K):
        oh = eid == sels[k]
        sc_k = jnp.sum(jnp.sum(jnp.where(oh, s3, 0.0), axis=0, keepdims=True), axis=1, keepdims=True)
        rk_k = jnp.sum(jnp.sum(jnp.where(oh, p3, 0.0), axis=0, keepdims=True), axis=1, keepdims=True)
        sc_rows.append(sc_k)
        eidx_ref[k:k + 1, :] = sels[k].reshape(1, tm).astype(jnp.int32)
        rank_ref[k:k + 1, :] = rk_k.reshape(1, tm).astype(jnp.int32)
    tot = sc_rows[0]
    for k in range(1, TOP_K):
        tot = tot + sc_rows[k]
    inv = ROUTED_SCALE / (tot + 1e-20)
    for k in range(TOP_K):
        gw_ref[k:k + 1, :] = (sc_rows[k] * inv).reshape(1, tm)


def _mix(half, a_n, yt, u2, x2, mod, d_skip, wglu_bf, b_glu, g_out_ssm, wo_a, wo_b, g_post_mix,
         g_pre_ffn, w_router_split, b_router_col, tri):
    tm = TM_MIX
    t0 = half * (HALF_TOK // tm)
    row = lambda n: pl.BlockSpec((1, n), lambda i: (0, 0))
    full = lambda a, b: pl.BlockSpec((a, b), lambda i: (0, 0))
    tok_in = lambda n: pl.BlockSpec((tm, n), lambda i: (t0 + i, 0))
    tok = lambda n: pl.BlockSpec((tm, n), lambda i: (i, 0))
    col = pl.BlockSpec((TOP_K, tm), lambda i: (0, i))
    return pl.pallas_call(
        _mix_kernel,
        grid=(HALF_TOK // tm,),
        in_specs=[tok_in(CONV_CH),
                  pl.BlockSpec((SSM_GROUPS, tm // S5_Q, S5_QH), lambda i: (0, t0 + i, 0)),
                  tok_in(SSM_CH), tok_in(D_MODEL),
                  pl.BlockSpec((1, MOD_ROWS, D_MODEL), lambda i: (half, 0, 0)),
                  row(SSM_CH), full(SSM_CH, SSM_CH), row(SSM_CH), row(SSM_CH),
                  full(CONV_CH, D_MODEL), full(SSM_CH, D_MODEL), row(D_MODEL), row(D_MODEL),
                  full(2 * N_EXPERTS, D_MODEL), full(N_EXPERTS, 1), full(tm, tm)],
        out_specs=[tok(D_MODEL), tok(D_MODEL // 2), col, col, col,
                   pl.BlockSpec((N_EXPERTS, LANES), lambda i: (0, 0))],
        out_shape=[jax.ShapeDtypeStruct((HALF_TOK, D_MODEL), F32),
                   jax.ShapeDtypeStruct((HALF_TOK, D_MODEL // 2), jnp.uint32),
                   jax.ShapeDtypeStruct((TOP_K, HALF_TOK), jnp.int32),
                   jax.ShapeDtypeStruct((TOP_K, HALF_TOK), jnp.int32),
                   jax.ShapeDtypeStruct((TOP_K, HALF_TOK), F32),
                   jax.ShapeDtypeStruct((N_EXPERTS, LANES), F32)],
        scratch_shapes=[pltpu.VMEM((N_EXPERTS, LANES), F32),
                        pltpu.VMEM((SSM_CH // LANES, tm, LANES), F32)],
        compiler_params=_cparams(("arbitrary",)),
        name="mix_out_router",
    )(a_n, yt, u2, x2, mod, d_skip, wglu_bf, b_glu, g_out_ssm, wo_a, wo_b, g_post_mix,
      g_pre_ffn, w_router_split, b_router_col, tri)


def _expert_kernel(blk0_ref, nblk_ref, bunit_ref, bfull_ref, xs_hbm, wgu_ref, wd_ref, ys_hbm,
                   xbuf, ybuf, sem_in, sem_out):
    e = pl.program_id(0)
    n = nblk_ref[e]
    b0 = blk0_ref[e]
    n_all = blk0_ref[N_EXPERTS - 1] + nblk_ref[N_EXPERTS - 1]

    def copies(b, slot, rows):
        hbm = pl.ds(pl.multiple_of(bunit_ref[b] * ROW_UNIT, ROW_UNIT), rows)
        buf = pl.ds(0, rows)
        return (pltpu.make_async_copy(xs_hbm.at[hbm], xbuf.at[slot, buf], sem_in.at[slot]),
                pltpu.make_async_copy(ybuf.at[slot, buf], ys_hbm.at[hbm], sem_out.at[slot]))

    def for_size(b, fn):
        full = bfull_ref[b] == 1

        @pl.when(full)
        def _():
            fn(ROW_BLOCK)

        @pl.when(jnp.logical_not(full))
        def _():
            fn(ROW_UNIT)

    for b in range(EXPERT_AHEAD):
        @pl.when((e == 0) & (b < n_all))
        def _():
            for_size(b, lambda rows: copies(b, b, rows)[0].start())

    def compute(slot, rows):
        x_lo, x_hi = _unpack_halves(xbuf[slot, pl.ds(0, rows)])
        x = jnp.concatenate([x_lo.astype(BF16), x_hi.astype(BF16)], axis=1)
        h = jnp.dot(x, wgu_ref[0], preferred_element_type=F32)
        hg = h[:, :D_EXPERT]
        act = hg * jax.nn.sigmoid(hg) * h[:, D_EXPERT:]
        ybuf[slot, pl.ds(0, rows)] = _pack_halves(
            jnp.dot(act.astype(BF16), wd_ref[0], preferred_element_type=F32))

    def block(b, carry):
        slot = b % EXPERT_SLOTS
        ahead = b + EXPERT_AHEAD

        @pl.when(ahead < n_all)
        def _():
            for_size(ahead, lambda rows: copies(ahead, ahead % EXPERT_SLOTS, rows)[0].start())

        @pl.when(b >= EXPERT_SLOTS)
        def _():
            for_size(b - EXPERT_SLOTS, lambda rows: copies(b - EXPERT_SLOTS, slot, rows)[1].wait())

        def work(rows):
            cp_in, cp_out = copies(b, slot, rows)
            cp_in.wait()
            compute(slot, rows)
            cp_out.start()

        for_size(b, work)
        return carry

    lax.fori_loop(b0, b0 + n, block, 0)

    @pl.when(e == N_EXPERTS - 1)
    def _():
        for j in range(1, EXPERT_SLOTS + 1):
            @pl.when(n_all >= j)
            def _():
                last = n_all - j
                for_size(last, lambda rows: copies(last, last % EXPERT_SLOTS, rows)[1].wait())


def _experts(blk0, nblk, bunit, bfull, xs, we_gu, we_d):
    any_spec = pl.BlockSpec(memory_space=pl.ANY)
    grid_spec = pltpu.PrefetchScalarGridSpec(
        num_scalar_prefetch=4,
        grid=(N_EXPERTS,),
        in_specs=[any_spec,
                  pl.BlockSpec((1, D_MODEL, 2 * D_EXPERT), lambda e, *_: (e, 0, 0)),
                  pl.BlockSpec((1, D_EXPERT, D_MODEL), lambda e, *_: (e, 0, 0))],
        out_specs=any_spec,
        scratch_shapes=[pltpu.VMEM((EXPERT_SLOTS, ROW_BLOCK, D_MODEL // 2), jnp.uint32),
                        pltpu.VMEM((EXPERT_SLOTS, ROW_BLOCK, D_MODEL // 2), jnp.uint32),
                        pltpu.SemaphoreType.DMA((EXPERT_SLOTS,)),
                        pltpu.SemaphoreType.DMA((EXPERT_SLOTS,))],
    )
    return pl.pallas_call(
        _expert_kernel,
        grid_spec=grid_spec,
        out_shape=jax.ShapeDtypeStruct((N_ROWS, D_MODEL // 2), jnp.uint32),
        compiler_params=_cparams(("arbitrary",)),
        name="routed_experts",
    )(blk0, nblk, bunit, bfull, xs, we_gu, we_d)


def _final_kernel(h2_ref, yg_ref, gw_ref, x1_ref, mod_ref, wgu_ref, wd_ref, g_ref, *rest):
    o_ref = rest[-1]
    half = D_MODEL // 2
    gt_f = mod_ref[0, 5:6, :]
    x_lo, x_hi = _unpack_halves(h2_ref[...])
    h = (jnp.dot(x_lo.astype(BF16), wgu_ref[:half, :], preferred_element_type=F32)
         + jnp.dot(x_hi.astype(BF16), wgu_ref[half:, :], preferred_element_type=F32))
    hg = h[:, :D_EXPERT]
    act = hg * jax.nn.sigmoid(hg) * h[:, D_EXPERT:]
    shared = jnp.dot(act.astype(BF16), wd_ref[...], preferred_element_type=F32)
    y_lo = shared[:, :half]
    y_hi = shared[:, half:]
    gw = gw_ref[...]
    gw_cols = jnp.transpose(jnp.concatenate([gw, jnp.zeros((LANES - TOP_K, gw.shape[1]), F32)], axis=0))
    for k in range(TOP_K):
        r_lo, r_hi = _unpack_halves(yg_ref[k])
        w = gw_cols[:, k:k + 1]
        y_lo = y_lo + w * r_lo
        y_hi = y_hi + w * r_hi
    ms = (jnp.sum(y_lo * y_lo, axis=-1, keepdims=True)
          + jnp.sum(y_hi * y_hi, axis=-1, keepdims=True)) * (1.0 / D_MODEL)
    inv = lax.rsqrt(ms + NORM_EPS)
    o_ref[:, :half] = x1_ref[:, :half] + gt_f[:, :half] * (y_lo * inv * g_ref[:, :half])
    o_ref[:, half:] = x1_ref[:, half:] + gt_f[:, half:] * (y_hi * inv * g_ref[:, half:])


def _final(half, out_prev, h2p, yg, gw, x1, mod, ws_gu, ws_d, g_post_ffn):
    tm = TM_OUT
    t0 = half * (HALF_TOK // tm)
    tok = pl.BlockSpec((tm, D_MODEL), lambda i: (i, 0))
    in_specs = [pl.BlockSpec((tm, D_MODEL // 2), lambda i: (i, 0)),
                pl.BlockSpec((TOP_K, tm, D_MODEL // 2), lambda i: (0, i, 0)),
                pl.BlockSpec((TOP_K, tm), lambda i: (0, i)),
                tok,
                pl.BlockSpec((1, MOD_ROWS, D_MODEL), lambda i: (half, 0, 0)),
                pl.BlockSpec((D_MODEL, 2 * D_EXPERT), lambda i: (0, 0)),
                pl.BlockSpec((D_EXPERT, D_MODEL), lambda i: (0, 0)),
                pl.BlockSpec((1, D_MODEL), lambda i: (0, 0))]
    args = [h2p, yg, gw, x1, mod, ws_gu, ws_d, g_post_ffn]
    aliases = {}
    if out_prev is not None:
        aliases = {len(args): 0}
        in_specs.append(pl.BlockSpec(memory_space=pl.ANY))
        args.append(out_prev)
    return pl.pallas_call(
        _final_kernel,
        grid=(HALF_TOK // tm,),
        in_specs=in_specs,
        out_specs=pl.BlockSpec((tm, D_MODEL), lambda i: (t0 + i, 0)),
        out_shape=jax.ShapeDtypeStruct((N_TOK, D_MODEL), F32),
        input_output_aliases=aliases,
        compiler_params=_cparams(("parallel",)),
        name="shared_final",
    )(*args)


def _sc_worker_id():
    return lax.axis_index("s") * SC_CORES + lax.axis_index("c")


def _dispatch_body(h_hbm, dest_hbm, xs_hbm, idx_v, rows_v, sem_l, sem_s):
    n = SC_CHUNKS_PER_WORKER
    c0 = _sc_worker_id() * n

    def load(i, b):
        return pltpu.async_copy(h_hbm.at[pl.ds((c0 + i) * SC_W, SC_W)], rows_v.at[b], sem_l.at[b])

    loads = [None] * n
    scat = [None] * n
    loads[0] = load(0, 0)
    for i in range(n):
        b = i % 2
        pltpu.sync_copy(dest_hbm.at[c0 + i], idx_v.at[b])
        loads[i].wait()
        if i + 1 < n:
            if i >= 1:
                for d in scat[i - 1]:
                    d.wait()
            loads[i + 1] = load(i + 1, 1 - b)
        scat[i] = [pltpu.async_copy(rows_v.at[b], xs_hbm.at[idx_v.at[b].at[k]], sem_s.at[b])
                   for k in range(TOP_K)]
    for i in (n - 2, n - 1):
        for d in scat[i]:
            d.wait()


def _sc_dispatch(h2p, dest3):
    mesh = plsc.VectorSubcoreMesh(core_axis_name="c", subcore_axis_name="s")
    return pl.kernel(
        _dispatch_body, mesh=mesh,
        out_type=jax.ShapeDtypeStruct((N_ROWS, D_MODEL // 2), jnp.uint32),
        scratch_types=[pltpu.VMEM((2, TOP_K, SC_W), jnp.int32),
                       pltpu.VMEM((2, SC_W, D_MODEL // 2), jnp.uint32),
                       pltpu.SemaphoreType.DMA((2,)), pltpu.SemaphoreType.DMA((2,))],
    )(h2p, dest3)


def _combine_body(ys_hbm, dest_hbm, yg_hbm, idx_v, rows_v, sem_g, sem_w):
    c0 = _sc_worker_id() * SC_CHUNKS_PER_WORKER

    @pl.loop(0, SC_CHUNKS_PER_WORKER)
    def _(i):
        c = c0 + i
        pltpu.sync_copy(dest_hbm.at[c], idx_v)
        g = [None] * TOP_K
        w = [None] * TOP_K
        g[0] = pltpu.async_copy(ys_hbm.at[idx_v.at[0]], rows_v.at[0], sem_g.at[0])
        for k in range(TOP_K):
            b = k % 2
            g[k].wait()
            if k + 1 < TOP_K:
                if k >= 1:
                    w[k - 1].wait()
                g[k + 1] = pltpu.async_copy(ys_hbm.at[idx_v.at[k + 1]], rows_v.at[1 - b], sem_g.at[1 - b])
            w[k] = pltpu.async_copy(rows_v.at[b], yg_hbm.at[k].at[pl.ds(c * SC_W, SC_W)], sem_w.at[b])
        w[TOP_K - 2].wait()
        w[TOP_K - 1].wait()


def _sc_combine(ysp, dest3):
    mesh = plsc.VectorSubcoreMesh(core_axis_name="c", subcore_axis_name="s")
    return pl.kernel(
        _combine_body, mesh=mesh,
        out_type=jax.ShapeDtypeStruct((TOP_K, HALF_TOK, D_MODEL // 2), jnp.uint32),
        scratch_types=[pltpu.VMEM((TOP_K, SC_W), jnp.int32),
                       pltpu.VMEM((2, SC_W, D_MODEL // 2), jnp.uint32),
                       pltpu.SemaphoreType.DMA((2,)), pltpu.SemaphoreType.DMA((2,))],
    )(ysp, dest3)


def kernel(x, c, w_ada, b_ada, g_pre_mix, g_post_mix, w_in, conv_w, conv_b, conv_ln_g, conv_ln_b,
           ssm_a_re, ssm_a_im, ssm_log_dt, ssm_b_re, ssm_b_im, ssm_c_re, ssm_c_im, ssm_d,
           ssm_w_glu, ssm_b_glu, g_out_conv, g_out_ssm, w_out, g_pre_ffn, g_post_ffn,
           w_router, b_router, we_gate, we_up, we_down, ws_gate, ws_up, ws_down):
    l = 0
    x2 = x.reshape(N_TOK, D_MODEL)
    r1 = lambda a: a.reshape(1, -1)

    c_pad = jnp.zeros((SUBLANES, D_MODEL), F32).at[:BATCH].set(c)
    mod = _ada(c_pad, w_ada[l], r1(b_ada[l]))[:BATCH].reshape(BATCH, N_MOD, D_MODEL)
    mod = jnp.concatenate([mod, jnp.zeros((BATCH, MOD_ROWS - N_MOD, D_MODEL), F32)], axis=1)

    v, u, ut = _inproj(x2, mod, r1(g_pre_mix[l]), w_in[l].astype(BF16))
    cw = jnp.concatenate([conv_w[l].reshape(CONV_WIDTH, CONV_CH), jnp.zeros((1, CONV_CH), F32)], axis=0)
    a_n, we_gu, we_d = _conv(v.reshape(BATCH, SEQ, CONV_CH), cw, r1(conv_b[l]), r1(conv_ln_g[l]),
                             r1(conv_ln_b[l]), r1(g_out_conv[l]), we_gate[l], we_up[l], we_down[l])
    a_n = a_n.reshape(N_TOK, CONV_CH)

    pwr, pwi, s5_params, a_cat, b_q, b_s = _s5_operators(
        ssm_a_re[l], ssm_a_im[l], ssm_log_dt[l], ssm_b_re[l], ssm_b_im[l], ssm_c_re[l], ssm_c_im[l])
    yt = _s5(ut, pwr, pwi, s5_params, a_cat, b_q, b_s)

    tm = TM_MIX
    tri = (jnp.arange(tm)[:, None] < jnp.arange(tm)[None, :]).astype(BF16)
    wo = w_out[l].astype(BF16)
    wr_t = w_router[l].T
    wr_hi = wr_t.astype(BF16)
    wr_split = jnp.concatenate([wr_hi, (wr_t - wr_hi.astype(F32)).astype(BF16)], axis=0)
    mix_params = (r1(ssm_d[l]), ssm_w_glu[l].astype(BF16), r1(ssm_b_glu[l]), r1(g_out_ssm[l]),
                  wo[:CONV_CH], wo[CONV_CH:], r1(g_post_mix[l]), r1(g_pre_ffn[l]),
                  wr_split, b_router[l].reshape(N_EXPERTS, 1), tri)
    ws_gu = jnp.concatenate([ws_gate[l], ws_up[l]], axis=1).astype(BF16)
    ws_d = ws_down[l].astype(BF16)
    e_ids = jnp.arange(N_EXPERTS, dtype=jnp.int32)

    out = None
    for half in range(N_HALVES):
        x1, h2, eidx, rank, gw, cnt = _mix(half, a_n, yt, u, x2, mod, *mix_params)
        counts = cnt[:, 0].astype(jnp.int32)
        units = (counts + ROW_UNIT - 1) // ROW_UNIT
        unit0 = jnp.cumsum(units) - units
        nblk = (units + 1) // 2
        blk0 = jnp.cumsum(nblk) - nblk
        dest = rank + jnp.sum(jnp.where(eidx[..., None] == e_ids, unit0 * ROW_UNIT, 0), axis=-1)
        dest3 = dest.reshape(TOP_K, HALF_TOK // SC_W, SC_W).transpose(1, 0, 2)
        b_ids = jnp.arange(N_BLOCKS, dtype=jnp.int32)
        owner = (b_ids[:, None] >= blk0[None, :]) & (b_ids[:, None] < (blk0 + nblk)[None, :])
        pick = lambda v: jnp.sum(jnp.where(owner, v[None, :], 0), axis=1)
        local = b_ids - pick(blk0)
        bunit = pick(unit0) + 2 * local
        bfull = (2 * local + 2 <= pick(units)).astype(jnp.int32)

        xs = _sc_dispatch(h2, dest3)
        ys = _experts(blk0, nblk, bunit, bfull, xs, we_gu, we_d)
        yg = _sc_combine(ys, dest3)
        out = _final(half, out, h2, yg, gw, x1, mod, ws_gu, ws_d, r1(g_post_ffn[l]))
    return out.reshape(BATCH, SEQ, D_MODEL)
```
